```python
import math
import numpy as np
import jax
import jax.numpy as jnp
from jax import lax

D_MODEL = 2048
BATCH = 8
SEQ = 4096
DEPTH = 2

N_MIXERS = 2
N_A = (DEPTH + 1) // 2
N_B = DEPTH // 2
BLK = 128
EPS = 1e-6

MLA_HEADS = 16
Q_LORA = 512
KV_LORA = 512
NOPE_DIM = 128
ROPE_DIM = 64
V_DIM = 128
QK_DIM = NOPE_DIM + ROPE_DIM
ROPE_THETA = 10000.0

DIL_PAIRS = ((128, 1), (512, 4), (2048, 16))
DIL_GROUPS = len(DIL_PAIRS)
DIL_HEADS = 8
DIL_HEAD_DIM = 128
ALIBI_TOTAL_HEADS = DIL_GROUPS * DIL_HEADS

D_FF = 5632

kernel_name = "hybrid_mla_dilated_macaron"


def rmsnorm(t, g):
    tf = t.astype(jnp.float32)
    y = tf * lax.rsqrt(jnp.mean(tf * tf, axis=-1, keepdims=True) + EPS)
    return (y * g.astype(jnp.float32)).astype(t.dtype)


def swiglu(xn, w_in, w_out):
    gate, up = jnp.split(xn @ w_in, 2, axis=-1)
    return (jax.nn.silu(gate) * up) @ w_out


def rope_tables(S):
    inv = 1.0 / (ROPE_THETA ** (jnp.arange(0, ROPE_DIM, 2, dtype=jnp.float32) / ROPE_DIM))
    ang = jnp.arange(S, dtype=jnp.float32)[:, None] * inv[None, :]
    return jnp.cos(ang), jnp.sin(ang)


def apply_rope(t, cos, sin):
    t1, t2 = jnp.split(t, 2, axis=-1)
    c = cos[None, :, None, :].astype(t.dtype)
    s = sin[None, :, None, :].astype(t.dtype)
    return jnp.concatenate([t1 * c - t2 * s, t1 * s + t2 * c], axis=-1)


def mla_mixer(xn, w_down, g_cq, g_ckv, w_uq, w_ukv, g_qn, g_kn, w_o):
    B, S, _ = xn.shape
    lat = xn @ w_down
    c_q = rmsnorm(lat[..., :Q_LORA], g_cq)
    c_kv = rmsnorm(lat[..., Q_LORA:Q_LORA + KV_LORA], g_ckv)
    k_pe = lat[..., Q_LORA + KV_LORA:]
    q = (c_q @ w_uq).reshape(B, S, MLA_HEADS, QK_DIM)
    kv = (c_kv @ w_ukv).reshape(B, S, MLA_HEADS, NOPE_DIM + V_DIM)
    k_nope, v = kv[..., :NOPE_DIM], kv[..., NOPE_DIM:]
    k = jnp.concatenate(
        [k_nope, jnp.broadcast_to(k_pe[:, :, None, :], (B, S, MLA_HEADS, ROPE_DIM))], axis=-1)
    q = rmsnorm(q, g_qn)
    k = rmsnorm(k, g_kn)
    cos, sin = rope_tables(S)
    q = jnp.concatenate([q[..., :NOPE_DIM], apply_rope(q[..., NOPE_DIM:], cos, sin)], axis=-1)
    k = jnp.concatenate([k[..., :NOPE_DIM], apply_rope(k[..., NOPE_DIM:], cos, sin)], axis=-1)

    scale = 1.0 / math.sqrt(QK_DIM)
    nb = S // BLK
    qb = q.reshape(B, nb, BLK, MLA_HEADS, QK_DIM).transpose(1, 0, 2, 3, 4)
    starts = jnp.arange(nb, dtype=jnp.int32) * BLK
    kf = k.astype(jnp.float32)
    vf = v.astype(jnp.float32)
    kpos = jnp.arange(S, dtype=jnp.int32)

    def block(args):
        qblk, s0 = args
        s = jnp.einsum('bqhe,bkhe->bhqk', qblk.astype(jnp.float32), kf) * scale
        qpos = s0 + jnp.arange(BLK, dtype=jnp.int32)
        s = jnp.where((kpos[None, :] <= qpos[:, None])[None, None], s, -jnp.inf)
        p = jax.nn.softmax(s, axis=-1)
        return jnp.einsum('bhqk,bkhe->bqhe', p, vf)

    o = lax.map(block, (qb, starts))
    o = o.transpose(1, 0, 2, 3, 4).reshape(B, S, MLA_HEADS * V_DIM).astype(xn.dtype)
    return o @ w_o


def dilated_group_attn(q, k, v, window, dilation, slopes):
    B, S, H, dh = q.shape
    win_sub = window // dilation
    L = S // dilation
    Lp = -(-L // BLK) * BLK
    nb = Lp // BLK

    def to_sub(t):
        t = t.astype(jnp.float32).reshape(B, L, dilation, H, dh).transpose(0, 2, 1, 3, 4)
        return jnp.pad(t, ((0, 0), (0, 0), (0, Lp - L), (0, 0), (0, 0)))

    qb = to_sub(q).reshape(B, dilation, nb, BLK, H, dh)
    kb = to_sub(k).reshape(B, dilation, nb, BLK, H, dh)
    vb = to_sub(v).reshape(B, dilation, nb, BLK, H, dh)
    pad_prev = ((0, 0), (0, 0), (1, 0), (0, 0), (0, 0), (0, 0))
    kcat = jnp.concatenate([jnp.pad(kb, pad_prev)[:, :, :-1], kb], axis=3)
    vcat = jnp.concatenate([jnp.pad(vb, pad_prev)[:, :, :-1], vb], axis=3)

    iq = jnp.arange(BLK, dtype=jnp.int32)
    ik = jnp.arange(2 * BLK, dtype=jnp.int32)
    nidx = jnp.arange(nb, dtype=jnp.int32)
    dist = iq[:, None] + BLK - ik[None, :]
    key_ok = (nidx[:, None] * BLK - BLK + ik[None, :]) >= 0
    mask = ((dist >= 0) & (dist <= win_sub))[None] & key_ok[:, None, :]
    bias = -slopes[:, None, None] * (dilation * dist).astype(jnp.float32)[None]

    scale = 1.0 / math.sqrt(dh)
    s = jnp.einsum('bdnqhe,bdnkhe->bdnhqk', qb, kcat) * scale + bias[None, None, None]
    s = jnp.where(mask[None, None, :, None], s, -jnp.inf)
    lse = jax.nn.logsumexp(s, axis=-1)
    p = jnp.exp(s - lse[..., None])
    o = jnp.einsum('bdnhqk,bdnkhe->bdnqhe', p, vcat)

    o = o.reshape(B, dilation, Lp, H, dh)[:, :, :L].transpose(0, 2, 1, 3, 4).reshape(B, S, H, dh)
    lse = lse.transpose(0, 1, 2, 4, 3).reshape(B, dilation, Lp, H)[:, :, :L]
    lse = lse.transpose(0, 2, 1, 3).reshape(B, S, H)
    return o, lse


def alibi_slopes():
    k = np.arange(1, ALIBI_TOTAL_HEADS + 1, dtype=np.float32)
    return jnp.asarray(2.0 ** (-8.0 * k / ALIBI_TOTAL_HEADS), dtype=jnp.float32)


def dilated_mixer(xn, w_qkv, g_qn, g_kn, w_o):
    B, S, _ = xn.shape
    qkv = (xn @ w_qkv).reshape(B, S, 3, DIL_GROUPS, DIL_HEADS, DIL_HEAD_DIM)
    q = rmsnorm(qkv[:, :, 0], g_qn)
    k = rmsnorm(qkv[:, :, 1], g_kn)
    v = qkv[:, :, 2]
    slopes = alibi_slopes().reshape(DIL_GROUPS, DIL_HEADS)
    outs, lses = [], []
    for g, (window, dilation) in enumerate(DIL_PAIRS):
        o_g, lse_g = dilated_group_attn(q[:, :, g], k[:, :, g], v[:, :, g], window, dilation, slopes[g])
        outs.append(o_g)
        lses.append(lse_g)
    o = jnp.stack(outs, axis=2)
    w = jax.nn.softmax(jnp.stack(lses, axis=2), axis=2)
    o = jnp.sum(o * w[..., None], axis=2).reshape(B, S, DIL_HEADS * DIL_HEAD_DIM).astype(xn.dtype)
    return o @ w_o


def _fwd_setup_inputs(seed: int = 0) -> dict:
    key = jax.random.key(seed)
    ks = iter(jax.random.split(key, 40))

    def w(shape, fan_in):
        return jax.random.normal(next(ks), shape, jnp.float32) * (fan_in ** -0.5)

    def gain(shape):
        return 1.0 + 0.02 * jax.random.normal(next(ks), shape, jnp.float32)

    D = D_MODEL
    return {
        "x": jax.random.normal(next(ks), (BATCH, SEQ, D), jnp.float32),
        "ffn1_norm": gain((DEPTH, D)),
        "ffn1_w_in": w((DEPTH, D, 2 * D_FF), D),
        "ffn1_w_out": w((DEPTH, D_FF, D), D_FF),
        "mix_norm": gain((DEPTH, D)),
        "ffn2_norm": gain((DEPTH, D)),
        "ffn2_w_in": w((DEPTH, D, 2 * D_FF), D),
        "ffn2_w_out": w((DEPTH, D_FF, D), D_FF),
        "mla_w_down": w((N_A, D, Q_LORA + KV_LORA + ROPE_DIM), D),
        "mla_g_cq": gain((N_A, Q_LORA)),
        "mla_g_ckv": gain((N_A, KV_LORA)),
        "mla_w_uq": w((N_A, Q_LORA, MLA_HEADS * QK_DIM), Q_LORA),
        "mla_w_ukv": w((N_A, KV_LORA, MLA_HEADS * (NOPE_DIM + V_DIM)), KV_LORA),
        "mla_g_qn": gain((N_A, QK_DIM)),
        "mla_g_kn": gain((N_A, QK_DIM)),
        "mla_w_o": w((N_A, MLA_HEADS * V_DIM, D), MLA_HEADS * V_DIM),
        "dil_w_qkv": w((N_B, D, 3 * DIL_GROUPS * DIL_HEADS * DIL_HEAD_DIM), D),
        "dil_g_qn": gain((N_B, DIL_HEAD_DIM)),
        "dil_g_kn": gain((N_B, DIL_HEAD_DIM)),
        "dil_w_o": w((N_B, DIL_HEADS * DIL_HEAD_DIM, D), DIL_HEADS * DIL_HEAD_DIM),
    }


def _fwd_reference(x, ffn1_norm, ffn1_w_in, ffn1_w_out, mix_norm, ffn2_norm, ffn2_w_in, ffn2_w_out,
              mla_w_down, mla_g_cq, mla_g_ckv, mla_w_uq, mla_w_ukv, mla_g_qn, mla_g_kn, mla_w_o,
              dil_w_qkv, dil_g_qn, dil_g_kn, dil_w_o):
    for i in range(DEPTH):
        j = i // N_MIXERS
        x = x + 0.5 * swiglu(rmsnorm(x, ffn1_norm[i]), ffn1_w_in[i], ffn1_w_out[i])
        xn = rmsnorm(x, mix_norm[i])
        if i % N_MIXERS == 0:
            y = mla_mixer(xn, mla_w_down[j], mla_g_cq[j], mla_g_ckv[j], mla_w_uq[j], mla_w_ukv[j],
                          mla_g_qn[j], mla_g_kn[j], mla_w_o[j])
        else:
            y = dilated_mixer(xn, dil_w_qkv[j], dil_g_qn[j], dil_g_kn[j], dil_w_o[j])
        x = x + y
        x = x + 0.5 * swiglu(rmsnorm(x, ffn2_norm[i]), ffn2_w_in[i], ffn2_w_out[i])
    return x


import jax as _jax
import jax.numpy as _jnp

TWIN_FORMAT = 'train_step'
FWD_PARAMS = ['x', 'ffn1_norm', 'ffn1_w_in', 'ffn1_w_out', 'mix_norm', 'ffn2_norm', 'ffn2_w_in', 'ffn2_w_out', 'mla_w_down', 'mla_g_cq', 'mla_g_ckv', 'mla_w_uq', 'mla_w_ukv', 'mla_g_qn', 'mla_g_kn', 'mla_w_o', 'dil_w_qkv', 'dil_g_qn', 'dil_g_kn', 'dil_w_o']
TWIN_WEIGHTS = ['ffn1_norm', 'ffn1_w_in', 'ffn1_w_out', 'mix_norm', 'ffn2_norm', 'ffn2_w_in', 'ffn2_w_out', 'mla_w_down', 'mla_g_cq', 'mla_g_ckv', 'mla_w_uq', 'mla_w_ukv', 'mla_g_qn', 'mla_g_kn', 'mla_w_o', 'dil_w_qkv', 'dil_g_qn', 'dil_g_kn', 'dil_w_o']
TWIN_DIFF_INPUT = 'x'
TWIN_INPUTS = ['x', 'ffn1_norm', 'ffn1_w_in', 'ffn1_w_out', 'mix_norm', 'ffn2_norm', 'ffn2_w_in', 'ffn2_w_out', 'mla_w_down', 'mla_g_cq', 'mla_g_ckv', 'mla_w_uq', 'mla_w_ukv', 'mla_g_qn', 'mla_g_kn', 'mla_w_o', 'dil_w_qkv', 'dil_g_qn', 'dil_g_kn', 'dil_w_o', 'loss_target', 'm_ffn1_norm', 'm_ffn1_w_in', 'm_ffn1_w_out', 'm_mix_norm', 'm_ffn2_norm', 'm_ffn2_w_in', 'm_ffn2_w_out', 'm_mla_w_down', 'm_mla_g_cq', 'm_mla_g_ckv', 'm_mla_w_uq', 'm_mla_w_ukv', 'm_mla_g_qn', 'm_mla_g_kn', 'm_mla_w_o', 'm_dil_w_qkv', 'm_dil_g_qn', 'm_dil_g_kn', 'm_dil_w_o', 'v_ffn1_norm', 'v_ffn1_w_in', 'v_ffn1_w_out', 'v_mix_norm', 'v_ffn2_norm', 'v_ffn2_w_in', 'v_ffn2_w_out', 'v_mla_w_down', 'v_mla_g_cq', 'v_mla_g_ckv', 'v_mla_w_uq', 'v_mla_w_ukv', 'v_mla_g_qn', 'v_mla_g_kn', 'v_mla_w_o', 'v_dil_w_qkv', 'v_dil_g_qn', 'v_dil_g_kn', 'v_dil_w_o']
TWIN_OUTPUTS = ['loss', 'grad_x', 'grad_ffn1_norm', 'grad_ffn1_w_in', 'grad_ffn1_w_out', 'grad_mix_norm', 'grad_ffn2_norm', 'grad_ffn2_w_in', 'grad_ffn2_w_out', 'grad_mla_w_down', 'grad_mla_g_cq', 'grad_mla_g_ckv', 'grad_mla_w_uq', 'grad_mla_w_ukv', 'grad_mla_g_qn', 'grad_mla_g_kn', 'grad_mla_w_o', 'grad_dil_w_qkv', 'grad_dil_g_qn', 'grad_dil_g_kn', 'grad_dil_w_o', 'delta_ffn1_norm', 'delta_ffn1_w_in', 'delta_ffn1_w_out', 'delta_mix_norm', 'delta_ffn2_norm', 'delta_ffn2_w_in', 'delta_ffn2_w_out', 'delta_mla_w_down', 'delta_mla_g_cq', 'delta_mla_g_ckv', 'delta_mla_w_uq', 'delta_mla_w_ukv', 'delta_mla_g_qn', 'delta_mla_g_kn', 'delta_mla_w_o', 'delta_dil_w_qkv', 'delta_dil_g_qn', 'delta_dil_g_kn', 'delta_dil_w_o', 'new_m_ffn1_norm', 'new_m_ffn1_w_in', 'new_m_ffn1_w_out', 'new_m_mix_norm', 'new_m_ffn2_norm', 'new_m_ffn2_w_in', 'new_m_ffn2_w_out', 'new_m_mla_w_down', 'new_m_mla_g_cq', 'new_m_mla_g_ckv', 'new_m_mla_w_uq', 'new_m_mla_w_ukv', 'new_m_mla_g_qn', 'new_m_mla_g_kn', 'new_m_mla_w_o', 'new_m_dil_w_qkv', 'new_m_dil_g_qn', 'new_m_dil_g_kn', 'new_m_dil_w_o', 'new_v_ffn1_norm', 'new_v_ffn1_w_in', 'new_v_ffn1_w_out', 'new_v_mix_norm', 'new_v_ffn2_norm', 'new_v_ffn2_w_in', 'new_v_ffn2_w_out', 'new_v_mla_w_down', 'new_v_mla_g_cq', 'new_v_mla_g_ckv', 'new_v_mla_w_uq', 'new_v_mla_w_ukv', 'new_v_mla_g_qn', 'new_v_mla_g_kn', 'new_v_mla_w_o', 'new_v_dil_w_qkv', 'new_v_dil_g_qn', 'new_v_dil_g_kn', 'new_v_dil_w_o']
TWIN_LEAF_KINDS = {'loss': 'loss', 'grad_x': 'grad_x', 'grad_ffn1_norm': 'grad_w', 'grad_ffn1_w_in': 'grad_w', 'grad_ffn1_w_out': 'grad_w', 'grad_mix_norm': 'grad_w', 'grad_ffn2_norm': 'grad_w', 'grad_ffn2_w_in': 'grad_w', 'grad_ffn2_w_out': 'grad_w', 'grad_mla_w_down': 'grad_w', 'grad_mla_g_cq': 'grad_w', 'grad_mla_g_ckv': 'grad_w', 'grad_mla_w_uq': 'grad_w', 'grad_mla_w_ukv': 'grad_w', 'grad_mla_g_qn': 'grad_w', 'grad_mla_g_kn': 'grad_w', 'grad_mla_w_o': 'grad_w', 'grad_dil_w_qkv': 'grad_w', 'grad_dil_g_qn': 'grad_w', 'grad_dil_g_kn': 'grad_w', 'grad_dil_w_o': 'grad_w', 'delta_ffn1_norm': 'delta_w', 'delta_ffn1_w_in': 'delta_w', 'delta_ffn1_w_out': 'delta_w', 'delta_mix_norm': 'delta_w', 'delta_ffn2_norm': 'delta_w', 'delta_ffn2_w_in': 'delta_w', 'delta_ffn2_w_out': 'delta_w', 'delta_mla_w_down': 'delta_w', 'delta_mla_g_cq': 'delta_w', 'delta_mla_g_ckv': 'delta_w', 'delta_mla_w_uq': 'delta_w', 'delta_mla_w_ukv': 'delta_w', 'delta_mla_g_qn': 'delta_w', 'delta_mla_g_kn': 'delta_w', 'delta_mla_w_o': 'delta_w', 'delta_dil_w_qkv': 'delta_w', 'delta_dil_g_qn': 'delta_w', 'delta_dil_g_kn': 'delta_w', 'delta_dil_w_o': 'delta_w', 'new_m_ffn1_norm': 'new_m', 'new_m_ffn1_w_in': 'new_m', 'new_m_ffn1_w_out': 'new_m', 'new_m_mix_norm': 'new_m', 'new_m_ffn2_norm': 'new_m', 'new_m_ffn2_w_in': 'new_m', 'new_m_ffn2_w_out': 'new_m', 'new_m_mla_w_down': 'new_m', 'new_m_mla_g_cq': 'new_m', 'new_m_mla_g_ckv': 'new_m', 'new_m_mla_w_uq': 'new_m', 'new_m_mla_w_ukv': 'new_m', 'new_m_mla_g_qn': 'new_m', 'new_m_mla_g_kn': 'new_m', 'new_m_mla_w_o': 'new_m', 'new_m_dil_w_qkv': 'new_m', 'new_m_dil_g_qn': 'new_m', 'new_m_dil_g_kn': 'new_m', 'new_m_dil_w_o': 'new_m', 'new_v_ffn1_norm': 'new_v', 'new_v_ffn1_w_in': 'new_v', 'new_v_ffn1_w_out': 'new_v', 'new_v_mix_norm': 'new_v', 'new_v_ffn2_norm': 'new_v', 'new_v_ffn2_w_in': 'new_v', 'new_v_ffn2_w_out': 'new_v', 'new_v_mla_w_down': 'new_v', 'new_v_mla_g_cq': 'new_v', 'new_v_mla_g_ckv': 'new_v', 'new_v_mla_w_uq': 'new_v', 'new_v_mla_w_ukv': 'new_v', 'new_v_mla_g_qn': 'new_v', 'new_v_mla_g_kn': 'new_v', 'new_v_mla_w_o': 'new_v', 'new_v_dil_w_qkv': 'new_v', 'new_v_dil_g_qn': 'new_v', 'new_v_dil_g_kn': 'new_v', 'new_v_dil_w_o': 'new_v'}


def _forward(args):
    return _fwd_reference(*[args[k] for k in FWD_PARAMS])


def _output_shape():
    def fwd():
        inp = _fwd_setup_inputs(0)
        return _fwd_reference(*[inp[k] for k in FWD_PARAMS])
    out = _jax.eval_shape(fwd)
    return out.shape, out.dtype

N_MICROBATCH = 1
ADAM_LR = 0.001
ADAM_B1 = 0.9
ADAM_B2 = 0.999
ADAM_EPS = 1e-08
ADAM_WD = 0.01
ADAM_STEP = 10
PER_EXAMPLE_BATCH_AXIS = {'x': 0, 'loss_target': 0}
SHARED_INPUTS = []
_WEIGHT_DTYPES = {'ffn1_norm': _jnp.float32, 'ffn1_w_in': _jnp.float32, 'ffn1_w_out': _jnp.float32, 'mix_norm': _jnp.float32, 'ffn2_norm': _jnp.float32, 'ffn2_w_in': _jnp.float32, 'ffn2_w_out': _jnp.float32, 'mla_w_down': _jnp.float32, 'mla_g_cq': _jnp.float32, 'mla_g_ckv': _jnp.float32, 'mla_w_uq': _jnp.float32, 'mla_w_ukv': _jnp.float32, 'mla_g_qn': _jnp.float32, 'mla_g_kn': _jnp.float32, 'mla_w_o': _jnp.float32, 'dil_w_qkv': _jnp.float32, 'dil_g_qn': _jnp.float32, 'dil_g_kn': _jnp.float32, 'dil_w_o': _jnp.float32}
MOMENT_SCALE = {'ffn1_norm': 3.082423e+00, 'ffn1_w_in': 3.942435e-02, 'ffn1_w_out': 6.339103e-02, 'mix_norm': 7.887933e-01, 'ffn2_norm': 3.089597e+00, 'ffn2_w_in': 3.395382e-02, 'ffn2_w_out': 5.728281e-02, 'mla_w_down': 1.339226e-01, 'mla_g_cq': 8.867431e-02, 'mla_g_ckv': 8.548316e-01, 'mla_w_uq': 3.762312e-02, 'mla_w_ukv': 5.519765e-02, 'mla_g_qn': 1.164304e+00, 'mla_g_kn': 1.163151e+00, 'mla_w_o': 6.314622e-02, 'dil_w_qkv': 5.843794e-02, 'dil_g_qn': 1.071070e+01, 'dil_g_kn': 1.069492e+01, 'dil_w_o': 6.053600e-02}


def _to_microbatches(a, axis):
    t = _jnp.moveaxis(a, axis, 0)
    t = t.reshape((N_MICROBATCH, t.shape[0] // N_MICROBATCH) + t.shape[1:])
    return _jnp.moveaxis(t, 1, axis + 1)


def setup_inputs(seed: int = 0) -> dict:
    inp = _fwd_setup_inputs(seed)
    key = _jax.random.fold_in(_jax.random.key(seed), 7919)
    shape, _ = _output_shape()
    out = dict(inp)
    out["loss_target"] = _jax.random.normal(_jax.random.fold_in(key, 0), shape, _jnp.float32)
    for i, name in enumerate(TWIN_WEIGHTS):
        w = inp[name].astype(_jnp.float32)
        if MOMENT_SCALE is None:
            s = _jnp.sqrt(_jnp.mean(_jnp.square(w)) + 1e-30)
        else:
            s = MOMENT_SCALE[name]
        km, kv = _jax.random.split(_jax.random.fold_in(key, i + 1))
        out[name] = w
        out["m_" + name] = s * _jax.random.normal(km, w.shape, _jnp.float32)
        out["v_" + name] = (s * s) * _jax.random.uniform(kv, w.shape, _jnp.float32, 0.5, 1.5)
    if N_MICROBATCH > 1:
        for name, axis in PER_EXAMPLE_BATCH_AXIS.items():
            out[name] = _to_microbatches(out[name], axis)
    return {'x': out['x'], 'ffn1_norm': out['ffn1_norm'], 'ffn1_w_in': out['ffn1_w_in'], 'ffn1_w_out': out['ffn1_w_out'], 'mix_norm': out['mix_norm'], 'ffn2_norm': out['ffn2_norm'], 'ffn2_w_in': out['ffn2_w_in'], 'ffn2_w_out': out['ffn2_w_out'], 'mla_w_down': out['mla_w_down'], 'mla_g_cq': out['mla_g_cq'], 'mla_g_ckv': out['mla_g_ckv'], 'mla_w_uq': out['mla_w_uq'], 'mla_w_ukv': out['mla_w_ukv'], 'mla_g_qn': out['mla_g_qn'], 'mla_g_kn': out['mla_g_kn'], 'mla_w_o': out['mla_w_o'], 'dil_w_qkv': out['dil_w_qkv'], 'dil_g_qn': out['dil_g_qn'], 'dil_g_kn': out['dil_g_kn'], 'dil_w_o': out['dil_w_o'], 'loss_target': out['loss_target'], 'm_ffn1_norm': out['m_ffn1_norm'], 'm_ffn1_w_in': out['m_ffn1_w_in'], 'm_ffn1_w_out': out['m_ffn1_w_out'], 'm_mix_norm': out['m_mix_norm'], 'm_ffn2_norm': out['m_ffn2_norm'], 'm_ffn2_w_in': out['m_ffn2_w_in'], 'm_ffn2_w_out': out['m_ffn2_w_out'], 'm_mla_w_down': out['m_mla_w_down'], 'm_mla_g_cq': out['m_mla_g_cq'], 'm_mla_g_ckv': out['m_mla_g_ckv'], 'm_mla_w_uq': out['m_mla_w_uq'], 'm_mla_w_ukv': out['m_mla_w_ukv'], 'm_mla_g_qn': out['m_mla_g_qn'], 'm_mla_g_kn': out['m_mla_g_kn'], 'm_mla_w_o': out['m_mla_w_o'], 'm_dil_w_qkv': out['m_dil_w_qkv'], 'm_dil_g_qn': out['m_dil_g_qn'], 'm_dil_g_kn': out['m_dil_g_kn'], 'm_dil_w_o': out['m_dil_w_o'], 'v_ffn1_norm': out['v_ffn1_norm'], 'v_ffn1_w_in': out['v_ffn1_w_in'], 'v_ffn1_w_out': out['v_ffn1_w_out'], 'v_mix_norm': out['v_mix_norm'], 'v_ffn2_norm': out['v_ffn2_norm'], 'v_ffn2_w_in': out['v_ffn2_w_in'], 'v_ffn2_w_out': out['v_ffn2_w_out'], 'v_mla_w_down': out['v_mla_w_down'], 'v_mla_g_cq': out['v_mla_g_cq'], 'v_mla_g_ckv': out['v_mla_g_ckv'], 'v_mla_w_uq': out['v_mla_w_uq'], 'v_mla_w_ukv': out['v_mla_w_ukv'], 'v_mla_g_qn': out['v_mla_g_qn'], 'v_mla_g_kn': out['v_mla_g_kn'], 'v_mla_w_o': out['v_mla_w_o'], 'v_dil_w_qkv': out['v_dil_w_qkv'], 'v_dil_g_qn': out['v_dil_g_qn'], 'v_dil_g_kn': out['v_dil_g_kn'], 'v_dil_w_o': out['v_dil_w_o']}


def _loss(weights, diff, rest, loss_target):
    with _jax.named_scope("forward"):
        args = {**rest, TWIN_DIFF_INPUT: diff, **{k: w.astype(_WEIGHT_DTYPES[k]) for k, w in weights.items()}}
        y = _forward(args)
    with _jax.named_scope("loss_head"):
        err = _jnp.square(y.astype(_jnp.float32) - loss_target)
        return 0.5 * _jnp.sum(_jnp.mean(err, axis=-1)) if err.ndim else 0.5 * err


def _adamw(w, g, m, v):
    m = ADAM_B1 * m + (1.0 - ADAM_B1) * g
    v = ADAM_B2 * v + (1.0 - ADAM_B2) * _jnp.square(g)
    m_hat = m / (1.0 - ADAM_B1 ** ADAM_STEP)
    v_hat = v / (1.0 - ADAM_B2 ** ADAM_STEP)
    delta = -ADAM_LR * (m_hat / (_jnp.sqrt(v_hat) + ADAM_EPS) + ADAM_WD * w)
    return delta, m, v


def reference(x, ffn1_norm, ffn1_w_in, ffn1_w_out, mix_norm, ffn2_norm, ffn2_w_in, ffn2_w_out, mla_w_down, mla_g_cq, mla_g_ckv, mla_w_uq, mla_w_ukv, mla_g_qn, mla_g_kn, mla_w_o, dil_w_qkv, dil_g_qn, dil_g_kn, dil_w_o, loss_target, m_ffn1_norm, m_ffn1_w_in, m_ffn1_w_out, m_mix_norm, m_ffn2_norm, m_ffn2_w_in, m_ffn2_w_out, m_mla_w_down, m_mla_g_cq, m_mla_g_ckv, m_mla_w_uq, m_mla_w_ukv, m_mla_g_qn, m_mla_g_kn, m_mla_w_o, m_dil_w_qkv, m_dil_g_qn, m_dil_g_kn, m_dil_w_o, v_ffn1_norm, v_ffn1_w_in, v_ffn1_w_out, v_mix_norm, v_ffn2_norm, v_ffn2_w_in, v_ffn2_w_out, v_mla_w_down, v_mla_g_cq, v_mla_g_ckv, v_mla_w_uq, v_mla_w_ukv, v_mla_g_qn, v_mla_g_kn, v_mla_w_o, v_dil_w_qkv, v_dil_g_qn, v_dil_g_kn, v_dil_w_o):
    given = dict(x=x, ffn1_norm=ffn1_norm, ffn1_w_in=ffn1_w_in, ffn1_w_out=ffn1_w_out, mix_norm=mix_norm, ffn2_norm=ffn2_norm, ffn2_w_in=ffn2_w_in, ffn2_w_out=ffn2_w_out, mla_w_down=mla_w_down, mla_g_cq=mla_g_cq, mla_g_ckv=mla_g_ckv, mla_w_uq=mla_w_uq, mla_w_ukv=mla_w_ukv, mla_g_qn=mla_g_qn, mla_g_kn=mla_g_kn, mla_w_o=mla_w_o, dil_w_qkv=dil_w_qkv, dil_g_qn=dil_g_qn, dil_g_kn=dil_g_kn, dil_w_o=dil_w_o, loss_target=loss_target, m_ffn1_norm=m_ffn1_norm, m_ffn1_w_in=m_ffn1_w_in, m_ffn1_w_out=m_ffn1_w_out, m_mix_norm=m_mix_norm, m_ffn2_norm=m_ffn2_norm, m_ffn2_w_in=m_ffn2_w_in, m_ffn2_w_out=m_ffn2_w_out, m_mla_w_down=m_mla_w_down, m_mla_g_cq=m_mla_g_cq, m_mla_g_ckv=m_mla_g_ckv, m_mla_w_uq=m_mla_w_uq, m_mla_w_ukv=m_mla_w_ukv, m_mla_g_qn=m_mla_g_qn, m_mla_g_kn=m_mla_g_kn, m_mla_w_o=m_mla_w_o, m_dil_w_qkv=m_dil_w_qkv, m_dil_g_qn=m_dil_g_qn, m_dil_g_kn=m_dil_g_kn, m_dil_w_o=m_dil_w_o, v_ffn1_norm=v_ffn1_norm, v_ffn1_w_in=v_ffn1_w_in, v_ffn1_w_out=v_ffn1_w_out, v_mix_norm=v_mix_norm, v_ffn2_norm=v_ffn2_norm, v_ffn2_w_in=v_ffn2_w_in, v_ffn2_w_out=v_ffn2_w_out, v_mla_w_down=v_mla_w_down, v_mla_g_cq=v_mla_g_cq, v_mla_g_ckv=v_mla_g_ckv, v_mla_w_uq=v_mla_w_uq, v_mla_w_ukv=v_mla_w_ukv, v_mla_g_qn=v_mla_g_qn, v_mla_g_kn=v_mla_g_kn, v_mla_w_o=v_mla_w_o, v_dil_w_qkv=v_dil_w_qkv, v_dil_g_qn=v_dil_g_qn, v_dil_g_kn=v_dil_g_kn, v_dil_w_o=v_dil_w_o)
    weights = {n: given[n] for n in TWIN_WEIGHTS}
    shared = {n: given[n] for n in SHARED_INPUTS}
    per_example = {n: given[n] for n in ['x']}
    grad_fn = _jax.value_and_grad(_loss, argnums=(0, 1))

    def one_microbatch(ex, loss_target):
        ex = dict(ex)
        diff = ex.pop(TWIN_DIFF_INPUT)
        return grad_fn(weights, diff, {**shared, **ex}, loss_target)

    if N_MICROBATCH == 1:
        loss, (grad_w, grad_x) = one_microbatch(per_example, given["loss_target"])
    else:
        def body(carry, xs):
            loss_sum, grad_sum = carry
            l_k, (gw_k, gx_k) = one_microbatch(xs[0], xs[1])
            with _jax.named_scope("update"):
                return (loss_sum + l_k, _jax.tree.map(_jnp.add, grad_sum, gw_k)), gx_k

        init = (_jnp.zeros((), _jnp.float32), _jax.tree.map(_jnp.zeros_like, weights))
        (loss, grad_w), grad_x = _jax.lax.scan(body, init, (per_example, given["loss_target"]))
    with _jax.named_scope("update"):
        delta_w, new_m, new_v = {}, {}, {}
        for n in TWIN_WEIGHTS:
            delta_w[n], new_m[n], new_v[n] = _adamw(weights[n], grad_w[n], given["m_" + n], given["v_" + n])
    return (loss, grad_x, *[grad_w[n] for n in TWIN_WEIGHTS], *[delta_w[n] for n in TWIN_WEIGHTS],
            *[new_m[n] for n in TWIN_WEIGHTS], *[new_v[n] for n in TWIN_WEIGHTS])
```

```python
import math

import numpy as np
import jax
import jax.numpy as jnp
from jax import lax
from jax.experimental import pallas as pl
from jax.experimental.pallas import tpu as pltpu

MXU_DTYPE = jnp.bfloat16
WIRE_DTYPE = jnp.bfloat16
EPS = 1e-6
NEG = -1e30
N_CHIPS = 4
MESH = pl.DeviceIdType.MESH
ANY = pl.BlockSpec(memory_space=pl.ANY)
LANES = 128

MLA_HEADS = 16
NOPE_DIM = 128
ROPE_DIM = 64
QK_DIM = NOPE_DIM + ROPE_DIM
QK_PAD = 2 * LANES
ROPE_THETA = 10000.0
DIL_PAIRS = ((128, 1), (512, 4), (2048, 16))
DIL_HEADS = 8
BLK = 128

ADAM_LR = 0.001
ADAM_B1 = 0.9
ADAM_B2 = 0.999
ADAM_EPS = 1e-08
ADAM_WD = 0.01
ADAM_STEP = 10

NT = (((1,), (1,)), ((), ()))
TN = (((0,), (0,)), ((), ()))

SHARD_AXIS = {"ffn1_w_in": 1, "ffn1_w_out": 0, "ffn2_w_in": 1, "ffn2_w_out": 0, "mla_w_down": 0, "mla_w_uq": 1,
              "mla_w_ukv": 1, "mla_w_o": 0, "dil_w_qkv": 1, "dil_w_o": 1}
BIG = tuple(SHARD_AXIS)
SMALL = ("ffn1_norm", "mix_norm", "ffn2_norm", "mla_g_cq", "mla_g_ckv", "mla_g_qn", "mla_g_kn", "dil_g_qn", "dil_g_kn")
WEIGHTS = ("ffn1_norm", "ffn1_w_in", "ffn1_w_out", "mix_norm", "ffn2_norm", "ffn2_w_in", "ffn2_w_out", "mla_w_down",
           "mla_g_cq", "mla_g_ckv", "mla_w_uq", "mla_w_ukv", "mla_g_qn", "mla_g_kn", "mla_w_o", "dil_w_qkv", "dil_g_qn",
           "dil_g_kn", "dil_w_o")


def _tile(dim, pref, mult=LANES):
    if dim <= pref:
        return dim
    t = (pref // mult) * mult
    while t >= mult:
        if dim % t == 0:
            return t
        t -= mult
    return dim


def _params(*sem):
    return pltpu.CompilerParams(dimension_semantics=sem)


def _f32(shape):
    return jax.ShapeDtypeStruct(shape, jnp.float32)


def _act(shape):
    return jax.ShapeDtypeStruct(shape, MXU_DTYPE)


class Sharded:
    def __init__(self, arr, layer, axis):
        self.arr, self.layer, self.axis = arr, layer, axis
        _, _, r, c = arr.shape
        self.shape = (N_CHIPS * r, c) if axis == 0 else (r, N_CHIPS * c)
        self.per = r if axis == 0 else c

    def spec(self, tr, tc, rc_of):
        l = self.layer
        if self.axis == 0:
            n = self.per // tr

            def imap(*g):
                bi, bj = rc_of(*g)
                return (bi // n, l, bi % n, bj)
        else:
            n = self.per // tc

            def imap(*g):
                bi, bj = rc_of(*g)
                return (bj // n, l, bi, bj % n)
        return pl.BlockSpec((None, None, tr, tc), imap)


def _spec2(tr, tc, rc_of):
    return pl.BlockSpec((tr, tc), lambda *g: rc_of(*g))


def matmul(a, b, *, ta=False, tb=False, out_dtype=jnp.float32, scale=None, res=None, out_axis=None,
           name, tm=1024, tn=1024, tk=512):
    am, ak = (a.shape[1], a.shape[0]) if ta else a.shape
    bk, bn = (b.shape[1], b.shape[0]) if tb else b.shape
    assert ak == bk, (name, a.shape, b.shape, ta, tb)
    M, N, K = am, bn, ak

    def per(x, axis):
        return x.per if isinstance(x, Sharded) and x.axis == axis else None

    def pick(dim, pref, *pers):
        for p in pers:
            if p is not None:
                return _tile(p, pref)
        return _tile(dim, pref)

    tm = pick(M, tm, per(a, 1 if ta else 0), M // N_CHIPS if out_axis == 0 else None)
    tn = pick(N, tn, per(b, 0 if tb else 1), N // N_CHIPS if out_axis == 1 else None)
    tk = pick(K, tk, per(a, 0 if ta else 1), per(b, 1 if tb else 0))
    assert M % tm == 0 and N % tn == 0 and K % tk == 0, (name, M, N, K, tm, tn, tk)
    nk = K // tk

    a_rc = (lambda i, j, k: (k, i)) if ta else (lambda i, j, k: (i, k))
    b_rc = (lambda i, j, k: (j, k)) if tb else (lambda i, j, k: (k, j))
    a_blk = (tk, tm) if ta else (tm, tk)
    b_blk = (tn, tk) if tb else (tk, tn)
    a_spec = a.spec(*a_blk, a_rc) if isinstance(a, Sharded) else _spec2(*a_blk, a_rc)
    b_spec = b.spec(*b_blk, b_rc) if isinstance(b, Sharded) else _spec2(*b_blk, b_rc)
    dn = (((0 if ta else 1,), (1 if tb else 0,)), ((), ()))
    has_res = res is not None

    def body(*refs):
        if has_res:
            a_ref, b_ref, r_ref, o_ref, acc = refs
        else:
            a_ref, b_ref, o_ref, acc = refs
        k = pl.program_id(2)

        @pl.when(k == 0)
        def _():
            acc[...] = jnp.zeros_like(acc)

        acc[...] += lax.dot_general(a_ref[...].astype(MXU_DTYPE), b_ref[...].astype(MXU_DTYPE), dn,
                                    preferred_element_type=jnp.float32)

        @pl.when(k == nk - 1)
        def _():
            r = acc[...]
            if scale is not None:
                r = r * scale
            if has_res:
                r = r + r_ref[...]
            o_ref[...] = r.astype(o_ref.dtype)

    in_specs = [a_spec, b_spec]
    args = [a.arr if isinstance(a, Sharded) else a, b.arr if isinstance(b, Sharded) else b]
    if has_res:
        in_specs.append(_spec2(tm, tn, lambda i, j, k: (i, j)))
        args.append(res)
    o_rc = lambda i, j, k: (i, j)
    if out_axis is None:
        out_shape = jax.ShapeDtypeStruct((M, N), out_dtype)
        out_spec = _spec2(tm, tn, o_rc)
    else:
        shp = (N_CHIPS, 1, M // N_CHIPS, N) if out_axis == 0 else (N_CHIPS, 1, M, N // N_CHIPS)
        out_shape = jax.ShapeDtypeStruct(shp, out_dtype)
        out_spec = Sharded(out_shape, 0, out_axis).spec(tm, tn, o_rc)
    return pl.pallas_call(
        body, name=name, out_shape=out_shape, grid=(M // tm, N // tn, nk),
        in_specs=in_specs, out_specs=out_spec,
        scratch_shapes=[pltpu.VMEM((tm, tn), jnp.float32)],
        compiler_params=_params("parallel", "parallel", "arbitrary"),
    )(*args)


def _me():
    return lax.axis_index("x"), lax.axis_index("y"), lax.axis_index("c")


def _other_chips(x, y):
    return [(1 - x, y), (x, 1 - y), (1 - x, 1 - y)]


def all_gather_weight(w, name):
    L, two, h, c = w.shape

    def body(w_ref, o_ref, send_sems, recv_sems, local_sem):
        x, y, cc = _me()
        s_me = 2 * x + y
        sibling = (x, y, 1 - cc)
        chips = _other_chips(x, y)

        def copy(k, src, dst, to):
            return pltpu.make_async_remote_copy(src_ref=src, dst_ref=dst, send_sem=send_sems.at[k],
                                                recv_sem=recv_sems.at[k], device_id=to, device_id_type=MESH)

        mine = pltpu.make_async_copy(w_ref, o_ref.at[s_me], local_sem)
        mine.start()
        first = [copy(j, w_ref.at[:, cc], o_ref.at[s_me, :, cc], (*chip, cc)) for j, chip in enumerate(chips)]
        for cp in first:
            cp.start()
        passed = []
        for j, (px, py) in enumerate(chips):
            got = o_ref.at[2 * px + py, :, cc]
            copy(j, got, got, (px, py, cc)).wait_recv()
            fwd = copy(3 + j, got, got, sibling)
            fwd.start()
            passed.append(fwd)
        for j, (px, py) in enumerate(chips):
            got = o_ref.at[2 * px + py, :, 1 - cc]
            copy(3 + j, got, got, sibling).wait_recv()
        for cp in first + passed:
            cp.wait_send()
        mine.wait()

    return pl.pallas_call(
        body, name=name, out_shape=jax.ShapeDtypeStruct((N_CHIPS, L, two, h, c), w.dtype),
        in_specs=[ANY], out_specs=ANY,
        scratch_shapes=[pltpu.SemaphoreType.DMA((6,)), pltpu.SemaphoreType.DMA((6,)), pltpu.SemaphoreType.DMA],
    )(w)


def sibling_send_halves(g, name):
    n, L, two, h, c = g.shape

    def body(g_ref, o_ref, send_sem, recv_sem):
        x, y, cc = _me()
        cp = pltpu.make_async_remote_copy(src_ref=g_ref.at[:, :, 1 - cc], dst_ref=o_ref, send_sem=send_sem,
                                          recv_sem=recv_sem, device_id=(x, y, 1 - cc), device_id_type=MESH)
        cp.start()
        cp.wait()

    return pl.pallas_call(
        body, name=name, out_shape=jax.ShapeDtypeStruct((n, L, h, c), g.dtype),
        in_specs=[ANY], out_specs=ANY,
        scratch_shapes=[pltpu.SemaphoreType.DMA, pltpu.SemaphoreType.DMA],
    )(g)


def chip_exchange(p, name):
    n, L, h, c = p.shape

    def body(p_ref, o_ref, send_sems, recv_sems):
        x, y, cc = _me()
        cps = []
        for j, (px, py) in enumerate(_other_chips(x, y)):
            cp = pltpu.make_async_remote_copy(src_ref=p_ref.at[2 * px + py], dst_ref=o_ref.at[j], send_sem=send_sems.at[j],
                                              recv_sem=recv_sems.at[j], device_id=(px, py, cc), device_id_type=MESH)
            cp.start()
            cps.append(cp)
        for cp in cps:
            cp.wait()

    return pl.pallas_call(
        body, name=name, out_shape=jax.ShapeDtypeStruct((3, L, h, c), p.dtype),
        in_specs=[ANY], out_specs=ANY,
        scratch_shapes=[pltpu.SemaphoreType.DMA((3,)), pltpu.SemaphoreType.DMA((3,))],
    )(p)


def sibling_gather_halves(r, name):
    L, h, c = r.shape

    def body(r_ref, o_ref, send_sem, recv_sem, local_sem):
        x, y, cc = _me()
        mine = pltpu.make_async_copy(r_ref, o_ref.at[:, cc], local_sem)
        mine.start()
        cp = pltpu.make_async_remote_copy(src_ref=r_ref, dst_ref=o_ref.at[:, cc], send_sem=send_sem,
                                          recv_sem=recv_sem, device_id=(x, y, 1 - cc), device_id_type=MESH)
        cp.start()
        cp.wait()
        mine.wait()

    return pl.pallas_call(
        body, name=name, out_shape=jax.ShapeDtypeStruct((L, 2, h, c), r.dtype),
        in_specs=[ANY], out_specs=ANY,
        scratch_shapes=[pltpu.SemaphoreType.DMA, pltpu.SemaphoreType.DMA, pltpu.SemaphoreType.DMA],
    )(r)


def add_sibling(g, r1, core, name):
    n, L, two, h, c = g.shape
    th = _tile(h, 512, 16)
    tc = _tile(c, 1024)

    def body(core_ref, g_ref, r_ref, o_ref):
        o_ref[...] = (g_ref[...].astype(jnp.float32) + r_ref[...].astype(jnp.float32)).astype(o_ref.dtype)

    grid_spec = pltpu.PrefetchScalarGridSpec(
        num_scalar_prefetch=1, grid=(n, L, h // th, c // tc),
        in_specs=[pl.BlockSpec((None, None, None, th, tc), lambda s, l, i, j, core: (s, l, core[0], i, j)),
                  pl.BlockSpec((None, None, th, tc), lambda s, l, i, j, core: (s, l, i, j))],
        out_specs=pl.BlockSpec((None, None, th, tc), lambda s, l, i, j, core: (s, l, i, j)))
    return pl.pallas_call(body, name=name, grid_spec=grid_spec, out_shape=jax.ShapeDtypeStruct((n, L, h, c), WIRE_DTYPE),
                          compiler_params=_params("parallel", "parallel", "parallel", "parallel"))(core, g, r1)


def add_chips(p, r2, shard, name):
    n, L, h, c = p.shape
    th = _tile(h, 512, 16)
    tc = _tile(c, 1024)

    def body(shard_ref, p_ref, r_ref, o_ref):
        acc = p_ref[...].astype(jnp.float32)
        for j in range(3):
            acc = acc + r_ref[j].astype(jnp.float32)
        o_ref[...] = acc

    grid_spec = pltpu.PrefetchScalarGridSpec(
        num_scalar_prefetch=1, grid=(L, h // th, c // tc),
        in_specs=[pl.BlockSpec((None, None, th, tc), lambda l, i, j, shard: (shard[0], l, i, j)),
                  pl.BlockSpec((3, None, th, tc), lambda l, i, j, shard: (0, l, i, j))],
        out_specs=pl.BlockSpec((None, th, tc), lambda l, i, j, shard: (l, i, j)))
    return pl.pallas_call(body, name=name, grid_spec=grid_spec, out_shape=_f32((L, h, c)),
                          compiler_params=_params("parallel", "parallel", "parallel"))(shard, p, r2)


def reduce_scatter_grad(g4, core, shard, name):
    n, L, r, c = g4.shape
    g = g4.reshape(n, L, 2, r // 2, c)
    r1 = sibling_send_halves(g, name + "_d2d")
    p = add_sibling(g, r1, core, name + "_add1")
    r2 = chip_exchange(p, name + "_ici")
    red = add_chips(p, r2, shard, name + "_add2")
    return sibling_gather_halves(red, name + "_gather").reshape(r, c)


def all_reduce_small(v, name):
    R, C = v.shape

    def body(v_ref, o_ref, buf, send_sems, recv_sems):
        x, y, cc = _me()
        buf[0] = v_ref[...]
        cps = []
        for k in range(1, 8):
            dx, dy, dc = (k >> 2) & 1, (k >> 1) & 1, k & 1
            to = (x ^ dx, y ^ dy, cc ^ dc)
            cp = pltpu.make_async_remote_copy(src_ref=v_ref, dst_ref=buf.at[k], send_sem=send_sems.at[k],
                                              recv_sem=recv_sems.at[k], device_id=to, device_id_type=MESH)
            cp.start()
            cps.append(cp)
        for cp in cps:
            cp.wait()
        me = 4 * x + 2 * y + cc
        acc = buf[me]
        for a in range(1, 8):
            acc = acc + buf[a ^ me]
        o_ref[...] = acc

    vm = pl.BlockSpec(memory_space=pltpu.VMEM)
    return pl.pallas_call(
        body, name=name, out_shape=_f32((R, C)), in_specs=[vm], out_specs=vm,
        scratch_shapes=[pltpu.VMEM((8, R, C), jnp.float32), pltpu.SemaphoreType.DMA((8,)), pltpu.SemaphoreType.DMA((8,))],
    )(v)


def _rstd(x, n):
    return lax.rsqrt(jnp.sum(x * x, axis=-1, keepdims=True) * (1.0 / n) + EPS)


def _accumulate(ref, part, first):
    @pl.when(first)
    def _():
        ref[...] = part

    @pl.when(jnp.logical_not(first))
    def _():
        ref[...] += part


def rmsnorm_fwd(x, g, name):
    S, D = x.shape
    ts = _tile(S, 256, 8)

    def body(x_ref, g_ref, o_ref):
        xv = x_ref[...]
        o_ref[...] = (xv * _rstd(xv, D) * g_ref[...]).astype(o_ref.dtype)

    return pl.pallas_call(
        body, name=name, out_shape=_act((S, D)), grid=(S // ts,),
        in_specs=[pl.BlockSpec((ts, D), lambda i: (i, 0)), pl.BlockSpec((1, D), lambda i: (0, 0))],
        out_specs=pl.BlockSpec((ts, D), lambda i: (i, 0)), compiler_params=_params("parallel"))(x, g)


def _norm_bwd(x, g, dy, n):
    r = _rstd(x, n)
    xh = x * r
    dxh = dy * g
    dx = r * (dxh - xh * (jnp.sum(dxh * xh, axis=-1, keepdims=True) * (1.0 / n)))
    return dx, dy * xh


def rmsnorm_bwd(x, g, dy, dres, name):
    S, D = x.shape
    ts = _tile(S, 256, 8)

    def body(x_ref, g_ref, dy_ref, dres_ref, dx_ref, dg_ref):
        dx, dgp = _norm_bwd(x_ref[...], g_ref[...], dy_ref[...], D)
        dx_ref[...] = dres_ref[...] + dx
        _accumulate(dg_ref, jnp.sum(dgp, axis=0, keepdims=True), pl.program_id(0) == 0)

    row = pl.BlockSpec((ts, D), lambda i: (i, 0))
    vec = pl.BlockSpec((1, D), lambda i: (0, 0))
    return pl.pallas_call(
        body, name=name, out_shape=(_f32((S, D)), _f32((1, D))), grid=(S // ts,),
        in_specs=[row, vec, row, row], out_specs=(row, vec), compiler_params=_params("arbitrary"))(x, g, dy, dres)


def _sigmoid(x):
    return 1.0 / (1.0 + jnp.exp(-x))


def swiglu_fwd(u, name):
    S, F2 = u.shape
    F = F2 // 2
    ts, tf = _tile(S, 512, 8), _tile(F, 512)
    nf = F // tf

    def body(g_ref, u_ref, o_ref):
        gt = g_ref[...]
        o_ref[...] = (gt * _sigmoid(gt) * u_ref[...]).astype(o_ref.dtype)

    return pl.pallas_call(
        body, name=name, out_shape=_act((S, F)), grid=(S // ts, nf),
        in_specs=[pl.BlockSpec((ts, tf), lambda i, j: (i, j)), pl.BlockSpec((ts, tf), lambda i, j: (i, j + nf))],
        out_specs=pl.BlockSpec((ts, tf), lambda i, j: (i, j)), compiler_params=_params("parallel", "parallel"))(u, u)


def swiglu_bwd(u, da, name):
    S, F2 = u.shape
    F = F2 // 2
    ts, tf = _tile(S, 512, 8), _tile(F, 512)
    nf = F // tf

    def body(g_ref, u_ref, da_ref, o_ref):
        gt, up, d = g_ref[...], u_ref[...], da_ref[...]
        s = _sigmoid(gt)
        dgate = d * up * (s * (1.0 + gt * (1.0 - s)))
        dup = d * (gt * s)
        o_ref[...] = jnp.where(pl.program_id(1) < nf, dgate, dup).astype(o_ref.dtype)

    return pl.pallas_call(
        body, name=name, out_shape=_act((S, F2)), grid=(S // ts, 2 * nf),
        in_specs=[pl.BlockSpec((ts, tf), lambda i, j: (i, j % nf)), pl.BlockSpec((ts, tf), lambda i, j: (i, j % nf + nf)),
                  pl.BlockSpec((ts, tf), lambda i, j: (i, j % nf))],
        out_specs=pl.BlockSpec((ts, tf), lambda i, j: (i, j)), compiler_params=_params("parallel", "parallel"))(u, u, da)


def loss_head(y, t, name):
    S, D = y.shape
    ts = _tile(S, 256, 8)

    def body(y_ref, t_ref, dy_ref, l_ref):
        e = y_ref[...] - t_ref[...]
        dy_ref[...] = e * (1.0 / D)
        l_ref[...] = jnp.full(l_ref.shape, 0.5 * jnp.sum(jnp.sum(e * e, axis=-1, keepdims=True) * (1.0 / D)), jnp.float32)

    row = pl.BlockSpec((ts, D), lambda i: (i, 0))
    dy, parts = pl.pallas_call(
        body, name=name, out_shape=(_f32((S, D)), _f32((S // ts, 8, LANES))), grid=(S // ts,),
        in_specs=[row, row], out_specs=(row, pl.BlockSpec((None, 8, LANES), lambda i: (i, 0, 0))),
        compiler_params=_params("parallel"))(y, t)
    return jnp.sum(parts[:, 0, 0]), dy


def rope_tables(S):
    inv = 1.0 / (ROPE_THETA ** (jnp.arange(0, ROPE_DIM, 2, dtype=jnp.float32) / ROPE_DIM))
    ang = jnp.arange(S, dtype=jnp.float32)[:, None] * inv[None, :]
    c, s = jnp.cos(ang), jnp.sin(ang)
    z = jnp.zeros_like(c)
    return (jnp.concatenate([c, c, z, z], axis=1), jnp.concatenate([-s, z, z, z], axis=1),
            jnp.concatenate([z, s, z, z], axis=1))


def _rope(x, cos, sa, sb):
    return x * cos + pltpu.roll(x, 96, 1) * sa + pltpu.roll(x, 32, 1) * sb


def _rope_t(d, cos, sa, sb):
    return d * cos + pltpu.roll(d * sa, 32, 1) + pltpu.roll(d * sb, 96, 1)


def _head_norm(x1, x2, g):
    r = lax.rsqrt((jnp.sum(x1 * x1, axis=-1, keepdims=True) + jnp.sum(x2 * x2, axis=-1, keepdims=True)) * (1.0 / QK_DIM) + EPS)
    return x1 * r * g[:, :LANES], x2 * r * g[:, LANES:], r


def _head_norm_bwd(x1, x2, g, d1, d2):
    _, _, r = _head_norm(x1, x2, g)
    h1, h2 = x1 * r, x2 * r
    e1, e2 = d1 * g[:, :LANES], d2 * g[:, LANES:]
    m = (jnp.sum(e1 * h1, axis=-1, keepdims=True) + jnp.sum(e2 * h2, axis=-1, keepdims=True)) * (1.0 / QK_DIM)
    return r * (e1 - h1 * m), r * (e2 - h2 * m), d1 * h1, d2 * h2


def mla_latent_fwd(lat, g_cq, g_ckv, name):
    S, W = lat.shape
    QL, KL = g_cq.shape[1], g_ckv.shape[1]
    ts = _tile(S, 256, 8)

    def body(l_ref, gq_ref, gk_ref, cq_ref, ckv_ref):
        a, b = l_ref[:, :QL], l_ref[:, QL:QL + KL]
        cq_ref[...] = (a * _rstd(a, QL) * gq_ref[...]).astype(cq_ref.dtype)
        ckv_ref[...] = (b * _rstd(b, KL) * gk_ref[...]).astype(ckv_ref.dtype)

    return pl.pallas_call(
        body, name=name, out_shape=(_act((S, QL)), _act((S, KL))), grid=(S // ts,),
        in_specs=[pl.BlockSpec((ts, W), lambda i: (i, 0)), pl.BlockSpec((1, QL), lambda i: (0, 0)),
                  pl.BlockSpec((1, KL), lambda i: (0, 0))],
        out_specs=(pl.BlockSpec((ts, QL), lambda i: (i, 0)), pl.BlockSpec((ts, KL), lambda i: (i, 0))),
        compiler_params=_params("parallel"))(lat, g_cq, g_ckv)


def mla_latent_bwd(dcq, dckv, dkpe, lat, g_cq, g_ckv, name):
    S, W = lat.shape
    QL, KL = g_cq.shape[1], g_ckv.shape[1]
    ts = _tile(S, 256, 8)

    def body(dq_ref, dk_ref, dp_ref, l_ref, gq_ref, gk_ref, o_ref, dgq_ref, dgk_ref):
        first = pl.program_id(0) == 0
        da, ga = _norm_bwd(l_ref[:, :QL], gq_ref[...], dq_ref[...], QL)
        db, gb = _norm_bwd(l_ref[:, QL:QL + KL], gk_ref[...], dk_ref[...], KL)
        o_ref[:, :QL] = da.astype(o_ref.dtype)
        o_ref[:, QL:QL + KL] = db.astype(o_ref.dtype)
        o_ref[:, QL + KL:] = dp_ref[...].astype(o_ref.dtype)
        _accumulate(dgq_ref, jnp.sum(ga, axis=0, keepdims=True), first)
        _accumulate(dgk_ref, jnp.sum(gb, axis=0, keepdims=True), first)

    row = lambda n: pl.BlockSpec((ts, n), lambda i: (i, 0))
    vec = lambda n: pl.BlockSpec((1, n), lambda i: (0, 0))
    return pl.pallas_call(
        body, name=name, out_shape=(_act((S, W)), _f32((1, QL)), _f32((1, KL))), grid=(S // ts,),
        in_specs=[row(QL), row(KL), row(LANES), row(W), vec(QL), vec(KL)], out_specs=(row(W), vec(QL), vec(KL)),
        compiler_params=_params("arbitrary"))(dcq, dckv, dkpe, lat, g_cq, g_ckv)


def mla_q_prep_fwd(qraw, g, tabs, H, name):
    S = qraw.shape[0]
    ts = _tile(S, 256, 8)

    def body(x_ref, g_ref, c_ref, a_ref, b_ref, o_ref):
        y1, y2, _ = _head_norm(x_ref[:, :LANES], x_ref[:, LANES:], g_ref[...])
        o_ref[:, :LANES] = y1.astype(o_ref.dtype)
        o_ref[:, LANES:] = _rope(y2, c_ref[...], a_ref[...], b_ref[...]).astype(o_ref.dtype)

    tab = pl.BlockSpec((ts, LANES), lambda i, h: (i, 0))
    return pl.pallas_call(
        body, name=name, out_shape=_act((H, S, QK_PAD)), grid=(S // ts, H),
        in_specs=[pl.BlockSpec((ts, QK_PAD), lambda i, h: (i, h)), pl.BlockSpec((1, QK_PAD), lambda i, h: (0, 0)), tab, tab, tab],
        out_specs=pl.BlockSpec((None, ts, QK_PAD), lambda i, h: (h, i, 0)),
        compiler_params=_params("parallel", "parallel"))(qraw, g, *tabs)


def mla_q_prep_bwd(dq, qraw, g, tabs, H, name):
    S = qraw.shape[0]
    ts = _tile(S, 256, 8)

    def body(d_ref, x_ref, g_ref, c_ref, a_ref, b_ref, o_ref, dg_ref):
        d2 = _rope_t(d_ref[:, LANES:], c_ref[...], a_ref[...], b_ref[...])
        dx1, dx2, g1, g2 = _head_norm_bwd(x_ref[:, :LANES], x_ref[:, LANES:], g_ref[...], d_ref[:, :LANES], d2)
        o_ref[:, :LANES] = dx1.astype(o_ref.dtype)
        o_ref[:, LANES:] = dx2.astype(o_ref.dtype)
        first = jnp.logical_and(pl.program_id(0) == 0, pl.program_id(1) == 0)
        part = jnp.concatenate([jnp.sum(g1, axis=0, keepdims=True), jnp.sum(g2, axis=0, keepdims=True)], axis=1)
        _accumulate(dg_ref, part, first)

    tab = pl.BlockSpec((ts, LANES), lambda i, h: (i, 0))
    vec = pl.BlockSpec((1, QK_PAD), lambda i, h: (0, 0))
    return pl.pallas_call(
        body, name=name, out_shape=(_act((S, H * QK_PAD)), _f32((1, QK_PAD))), grid=(S // ts, H),
        in_specs=[pl.BlockSpec((None, ts, QK_PAD), lambda i, h: (h, i, 0)), pl.BlockSpec((ts, QK_PAD), lambda i, h: (i, h)),
                  vec, tab, tab, tab],
        out_specs=(pl.BlockSpec((ts, QK_PAD), lambda i, h: (i, h)), vec),
        compiler_params=_params("arbitrary", "arbitrary"))(dq, qraw, g, *tabs)


def mla_k_prep_fwd(kvraw, lat, g, tabs, H, pe_blk, name):
    S = kvraw.shape[0]
    ts = _tile(S, 256, 8)

    def body(x_ref, p_ref, g_ref, c_ref, a_ref, b_ref, k_ref, v_ref):
        y1, y2, _ = _head_norm(x_ref[:, :LANES], p_ref[...], g_ref[...])
        k_ref[:, :LANES] = y1.astype(k_ref.dtype)
        k_ref[:, LANES:] = _rope(y2, c_ref[...], a_ref[...], b_ref[...]).astype(k_ref.dtype)
        v_ref[...] = x_ref[:, LANES:].astype(v_ref.dtype)

    tab = pl.BlockSpec((ts, LANES), lambda i, h: (i, 0))
    return pl.pallas_call(
        body, name=name, out_shape=(_act((H, S, QK_PAD)), _act((H, S, LANES))), grid=(S // ts, H),
        in_specs=[pl.BlockSpec((ts, QK_PAD), lambda i, h: (i, h)), pl.BlockSpec((ts, LANES), lambda i, h: (i, pe_blk)),
                  pl.BlockSpec((1, QK_PAD), lambda i, h: (0, 0)), tab, tab, tab],
        out_specs=(pl.BlockSpec((None, ts, QK_PAD), lambda i, h: (h, i, 0)), pl.BlockSpec((None, ts, LANES), lambda i, h: (h, i, 0))),
        compiler_params=_params("parallel", "parallel"))(kvraw, lat, g, *tabs)


def mla_k_prep_bwd(dk, dv, kvraw, lat, g, tabs, H, pe_blk, name):
    S = kvraw.shape[0]
    ts = _tile(S, 256, 8)

    def body(dk_ref, dv_ref, x_ref, p_ref, g_ref, c_ref, a_ref, b_ref, o_ref, dp_ref, dg_ref):
        i, h = pl.program_id(0), pl.program_id(1)
        d2 = _rope_t(dk_ref[:, LANES:], c_ref[...], a_ref[...], b_ref[...])
        dx1, dx2, g1, g2 = _head_norm_bwd(x_ref[:, :LANES], p_ref[...], g_ref[...], dk_ref[:, :LANES], d2)
        o_ref[:, :LANES] = dx1.astype(o_ref.dtype)
        o_ref[:, LANES:] = dv_ref[...].astype(o_ref.dtype)
        _accumulate(dp_ref, dx2, h == 0)
        part = jnp.concatenate([jnp.sum(g1, axis=0, keepdims=True), jnp.sum(g2, axis=0, keepdims=True)], axis=1)
        _accumulate(dg_ref, part, jnp.logical_and(i == 0, h == 0))

    tab = pl.BlockSpec((ts, LANES), lambda i, h: (i, 0))
    vec = pl.BlockSpec((1, QK_PAD), lambda i, h: (0, 0))
    return pl.pallas_call(
        body, name=name, out_shape=(_act((S, H * QK_PAD)), _f32((S, LANES)), _f32((1, QK_PAD))), grid=(S // ts, H),
        in_specs=[pl.BlockSpec((None, ts, QK_PAD), lambda i, h: (h, i, 0)), pl.BlockSpec((None, ts, LANES), lambda i, h: (h, i, 0)),
                  pl.BlockSpec((ts, QK_PAD), lambda i, h: (i, h)), pl.BlockSpec((ts, LANES), lambda i, h: (i, pe_blk)),
                  vec, tab, tab, tab],
        out_specs=(pl.BlockSpec((ts, QK_PAD), lambda i, h: (i, h)), tab, vec),
        compiler_params=_params("arbitrary", "arbitrary"))(dk, dv, kvraw, lat, g, *tabs)


def _causal_scores(q, k, scale, qi, ki):
    s = lax.dot_general(q, k, NT, preferred_element_type=jnp.float32) * scale
    row = lax.broadcasted_iota(jnp.int32, s.shape, 0)
    col = lax.broadcasted_iota(jnp.int32, s.shape, 1)
    return jnp.where(jnp.logical_or(ki < qi, col <= row), s, NEG)


def mla_attention_fwd(q, k, v, name):
    H, S, _ = q.shape
    t = _tile(S, 512)
    n = S // t
    scale = 1.0 / math.sqrt(QK_DIM)

    def body(q_ref, k_ref, v_ref, o_ref, lse_ref, m_sc, l_sc, acc):
        qi, ki = pl.program_id(1), pl.program_id(2)

        @pl.when(ki == 0)
        def _():
            m_sc[...] = jnp.full(m_sc.shape, NEG, jnp.float32)
            l_sc[...] = jnp.zeros_like(l_sc)
            acc[...] = jnp.zeros_like(acc)

        @pl.when(ki <= qi)
        def _():
            s = _causal_scores(q_ref[...], k_ref[...], scale, qi, ki)
            m_new = jnp.maximum(m_sc[...], jnp.max(s, axis=-1, keepdims=True))
            alpha = jnp.exp(m_sc[...] - m_new)
            p = jnp.exp(s - m_new)
            l_sc[...] = alpha * l_sc[...] + jnp.sum(p, axis=-1, keepdims=True)
            acc[...] = alpha * acc[...] + jnp.dot(p.astype(MXU_DTYPE), v_ref[...], preferred_element_type=jnp.float32)
            m_sc[...] = m_new

        @pl.when(ki == qi)
        def _():
            o_ref[...] = (acc[...] / l_sc[...]).astype(o_ref.dtype)
            lse_ref[...] = m_sc[...] + jnp.log(l_sc[...])

    kv = lambda w: pl.BlockSpec((None, t, w), lambda h, qi, ki: (h, jnp.minimum(ki, qi), 0))
    return pl.pallas_call(
        body, name=name, out_shape=(_act((S, H * LANES)), _f32((H, S, 1))), grid=(H, n, n),
        in_specs=[pl.BlockSpec((None, t, QK_PAD), lambda h, qi, ki: (h, qi, 0)), kv(QK_PAD), kv(LANES)],
        out_specs=(pl.BlockSpec((t, LANES), lambda h, qi, ki: (qi, h)), pl.BlockSpec((None, t, 1), lambda h, qi, ki: (h, qi, 0))),
        scratch_shapes=[pltpu.VMEM((t, 1), jnp.float32), pltpu.VMEM((t, 1), jnp.float32), pltpu.VMEM((t, LANES), jnp.float32)],
        compiler_params=_params("parallel", "parallel", "arbitrary"))(q, k, v)


def attention_delta(do, o, H, name):
    S = do.shape[0]
    ts = _tile(S, 512, 8)

    def body(d_ref, o_ref, out_ref):
        out_ref[...] = jnp.sum(d_ref[...] * o_ref[...].astype(jnp.float32), axis=-1, keepdims=True)

    blk = pl.BlockSpec((ts, LANES), lambda i, h: (i, h))
    return pl.pallas_call(
        body, name=name, out_shape=_f32((H, S, 1)), grid=(S // ts, H), in_specs=[blk, blk],
        out_specs=pl.BlockSpec((None, ts, 1), lambda i, h: (h, i, 0)), compiler_params=_params("parallel", "parallel"))(do, o)


def mla_attention_bwd_dq(q, k, v, do, lse, delta, name):
    H, S, _ = q.shape
    t = _tile(S, 512)
    n = S // t
    scale = 1.0 / math.sqrt(QK_DIM)

    def body(q_ref, k_ref, v_ref, do_ref, lse_ref, dl_ref, dq_ref, acc):
        qi, ki = pl.program_id(1), pl.program_id(2)

        @pl.when(ki == 0)
        def _():
            acc[...] = jnp.zeros_like(acc)

        @pl.when(ki <= qi)
        def _():
            p = jnp.exp(_causal_scores(q_ref[...], k_ref[...], scale, qi, ki) - lse_ref[...])
            dp = lax.dot_general(do_ref[...].astype(MXU_DTYPE), v_ref[...], NT, preferred_element_type=jnp.float32)
            ds = p * (dp - dl_ref[...])
            acc[...] += jnp.dot(ds.astype(MXU_DTYPE), k_ref[...], preferred_element_type=jnp.float32)

        @pl.when(ki == qi)
        def _():
            dq_ref[...] = acc[...] * scale

    kv = lambda w: pl.BlockSpec((None, t, w), lambda h, qi, ki: (h, jnp.minimum(ki, qi), 0))
    col = pl.BlockSpec((None, t, 1), lambda h, qi, ki: (h, qi, 0))
    qspec = pl.BlockSpec((None, t, QK_PAD), lambda h, qi, ki: (h, qi, 0))
    return pl.pallas_call(
        body, name=name, out_shape=_f32((H, S, QK_PAD)), grid=(H, n, n),
        in_specs=[qspec, kv(QK_PAD), kv(LANES), pl.BlockSpec((t, LANES), lambda h, qi, ki: (qi, h)), col, col],
        out_specs=qspec, scratch_shapes=[pltpu.VMEM((t, QK_PAD), jnp.float32)],
        compiler_params=_params("parallel", "parallel", "arbitrary"))(q, k, v, do, lse, delta)


def mla_attention_bwd_dkv(q, k, v, do, lse, delta, name):
    H, S, _ = q.shape
    t = _tile(S, 512)
    n = S // t
    scale = 1.0 / math.sqrt(QK_DIM)

    def body(q_ref, k_ref, v_ref, do_ref, lse_ref, dl_ref, dk_ref, dv_ref, dk_acc, dv_acc):
        ki, qi = pl.program_id(1), pl.program_id(2)

        @pl.when(qi == 0)
        def _():
            dk_acc[...] = jnp.zeros_like(dk_acc)
            dv_acc[...] = jnp.zeros_like(dv_acc)

        @pl.when(qi >= ki)
        def _():
            p = jnp.exp(_causal_scores(q_ref[...], k_ref[...], scale, qi, ki) - lse_ref[...])
            dob = do_ref[...].astype(MXU_DTYPE)
            dv_acc[...] += lax.dot_general(p.astype(MXU_DTYPE), dob, TN, preferred_element_type=jnp.float32)
            dp = lax.dot_general(dob, v_ref[...], NT, preferred_element_type=jnp.float32)
            ds = p * (dp - dl_ref[...])
            dk_acc[...] += lax.dot_general(ds.astype(MXU_DTYPE), q_ref[...], TN, preferred_element_type=jnp.float32)

        @pl.when(qi == n - 1)
        def _():
            dk_ref[...] = dk_acc[...] * scale
            dv_ref[...] = dv_acc[...]

    qrow = lambda h, ki, qi: (h, jnp.maximum(qi, ki), 0)
    kv = lambda w: pl.BlockSpec((None, t, w), lambda h, ki, qi: (h, ki, 0))
    col = pl.BlockSpec((None, t, 1), qrow)
    return pl.pallas_call(
        body, name=name, out_shape=(_f32((H, S, QK_PAD)), _f32((H, S, LANES))), grid=(H, n, n),
        in_specs=[pl.BlockSpec((None, t, QK_PAD), qrow), kv(QK_PAD), kv(LANES),
                  pl.BlockSpec((t, LANES), lambda h, ki, qi: (jnp.maximum(qi, ki), h)), col, col],
        out_specs=(kv(QK_PAD), kv(LANES)),
        scratch_shapes=[pltpu.VMEM((t, QK_PAD), jnp.float32), pltpu.VMEM((t, LANES), jnp.float32)],
        compiler_params=_params("parallel", "parallel", "arbitrary"))(q, k, v, do, lse, delta)


def _alibi_slopes(G, Hd):
    k = np.arange(1, G * Hd + 1, dtype=np.float32)
    s = (2.0 ** (-8.0 * k / (G * Hd))).astype(np.float32).reshape(G, Hd)
    return jnp.asarray(np.broadcast_to(s[:, :, None, None], (G, Hd, 1, LANES)).copy())


def _dil_scores(qn, kn, scale, slope_d, prev, valid):
    s = lax.dot_general(qn, kn, NT, preferred_element_type=jnp.float32) * scale
    iq = lax.broadcasted_iota(jnp.int32, s.shape, 0)
    ik = lax.broadcasted_iota(jnp.int32, s.shape, 1)
    dist = iq - ik + (BLK if prev else 0)
    ok = (ik >= iq) if prev else (ik <= iq)
    s = s - slope_d * dist.astype(jnp.float32)
    return jnp.where(jnp.logical_and(ok, valid), s, NEG)


def _dil_specs(d, nb, Hd, G, g, ncol):
    def spec(kind, shift):
        col0 = (kind * G + g) * Hd

        def imap(r, n, h):
            return (jnp.clip(n + shift, 0, nb - 1), r * ncol + col0 + h)
        return pl.BlockSpec((BLK, LANES), imap)
    return spec


def dilated_fwd(qkv, gq, gk, slopes, g, d, Hd, G, name):
    S, C = qkv.shape
    ncol = C // LANES
    nb = S // d // BLK
    view = qkv.reshape(S // d, d * C)
    scale = 1.0 / math.sqrt(LANES)
    spec = _dil_specs(d, nb, Hd, G, g, ncol)

    def body(q_ref, kc_ref, kp_ref, vc_ref, vp_ref, gq_ref, gk_ref, sl_ref, o_ref, l_ref):
        n = pl.program_id(1)
        nrm = lambda t, gg: (t * _rstd(t, LANES) * gg).astype(MXU_DTYPE)
        qn = nrm(q_ref[...], gq_ref[...])
        slope_d = sl_ref[:, :1] * float(d)
        sc = _dil_scores(qn, nrm(kc_ref[...], gk_ref[...]), scale, slope_d, False, True)
        sp = _dil_scores(qn, nrm(kp_ref[...], gk_ref[...]), scale, slope_d, True, n > 0)
        m = jnp.maximum(jnp.max(sc, axis=-1, keepdims=True), jnp.max(sp, axis=-1, keepdims=True))
        lse = m + jnp.log(jnp.sum(jnp.exp(sc - m), axis=-1, keepdims=True) + jnp.sum(jnp.exp(sp - m), axis=-1, keepdims=True))
        o = jnp.dot(jnp.exp(sc - lse).astype(MXU_DTYPE), vc_ref[...].astype(MXU_DTYPE), preferred_element_type=jnp.float32)
        o = o + jnp.dot(jnp.exp(sp - lse).astype(MXU_DTYPE), vp_ref[...].astype(MXU_DTYPE), preferred_element_type=jnp.float32)
        o_ref[...] = o
        l_ref[...] = jnp.broadcast_to(lse, l_ref.shape)

    vec = pl.BlockSpec((1, LANES), lambda r, n, h: (0, 0))
    out = pl.BlockSpec((BLK, LANES), lambda r, n, h: (n, r * Hd + h))
    o, l = pl.pallas_call(
        body, name=name, out_shape=(_f32((S // d, d * Hd * LANES)), _f32((S // d, d * Hd * LANES))), grid=(d, nb, Hd),
        in_specs=[spec(0, 0), spec(1, 0), spec(1, -1), spec(2, 0), spec(2, -1), vec, vec,
                  pl.BlockSpec((None, None, 1, LANES), lambda r, n, h: (g, h, 0, 0))],
        out_specs=(out, out), compiler_params=_params("parallel", "parallel", "parallel"),
    )(view, view, view, view, view, gq, gk, slopes)
    return o.reshape(S, Hd * LANES), l.reshape(S, Hd * LANES)


def dilated_merge(os_, ls_, name):
    S, W = os_[0].shape
    G = len(os_)
    ts, tw = _tile(S, 512, 8), _tile(W, 512)

    def body(*refs):
        o_refs, l_refs, (o_ref, t_ref) = refs[:G], refs[G:2 * G], refs[2 * G:]
        ls = [r[...] for r in l_refs]
        m = ls[0]
        for l in ls[1:]:
            m = jnp.maximum(m, l)
        es = [jnp.exp(l - m) for l in ls]
        tot = es[0]
        for e in es[1:]:
            tot = tot + e
        acc = o_refs[0][...] * (es[0] / tot)
        for r, e in zip(o_refs[1:], es[1:]):
            acc = acc + r[...] * (e / tot)
        o_ref[...] = acc.astype(o_ref.dtype)
        t_ref[...] = m + jnp.log(tot)

    blk = pl.BlockSpec((ts, tw), lambda i, j: (i, j))
    return pl.pallas_call(
        body, name=name, out_shape=(_act((S, W)), _f32((S, W))), grid=(S // ts, W // tw),
        in_specs=[blk] * (2 * G), out_specs=(blk, blk), compiler_params=_params("parallel", "parallel"))(*os_, *ls_)


def dilated_delta(do, o, name):
    S, W = do.shape
    ts = _tile(S, 512, 8)

    def body(d_ref, o_ref, out_ref):
        out_ref[...] = jnp.broadcast_to(jnp.sum(d_ref[...] * o_ref[...].astype(jnp.float32), axis=-1, keepdims=True), out_ref.shape)

    blk = pl.BlockSpec((ts, LANES), lambda i, h: (i, h))
    return pl.pallas_call(body, name=name, out_shape=_f32((S, W)), grid=(S // ts, W // LANES), in_specs=[blk, blk],
                          out_specs=blk, compiler_params=_params("parallel", "parallel"))(do, o)


def dilated_bwd(qkv, do, lse, delta, gq, gk, slopes, g, d, Hd, G, name):
    S, C = qkv.shape
    ncol = C // LANES
    nb = S // d // BLK
    W = Hd * LANES
    view = qkv.reshape(S // d, d * C)
    hview = lambda t: t.reshape(S // d, d * W)
    scale = 1.0 / math.sqrt(LANES)
    spec = _dil_specs(d, nb, Hd, G, g, ncol)

    def hspec(shift):
        return pl.BlockSpec((BLK, LANES), lambda r, n, h: (jnp.clip(n + shift, 0, nb - 1), r * Hd + h))

    def body(q_ref, qx_ref, kc_ref, kp_ref, vc_ref, vp_ref, do_ref, dox_ref, l_ref, lx_ref, dl_ref, dlx_ref,
             gq_ref, gk_ref, sl_ref, dq_ref, dk_ref, dv_ref, dgq_ref, dgk_ref):
        r, n, h = pl.program_id(0), pl.program_id(1), pl.program_id(2)
        first = jnp.logical_and(jnp.logical_and(r == 0, n == 0), h == 0)
        gqv, gkv = gq_ref[...], gk_ref[...]
        nrm = lambda t, gg: (t * _rstd(t, LANES) * gg).astype(MXU_DTYPE)
        f32dot = lambda a, b, dn: lax.dot_general(a, b, dn, preferred_element_type=jnp.float32)
        qn, qxn = nrm(q_ref[...], gqv), nrm(qx_ref[...], gqv)
        kcn, kpn = nrm(kc_ref[...], gkv), nrm(kp_ref[...], gkv)
        vc, vp = vc_ref[...].astype(MXU_DTYPE), vp_ref[...].astype(MXU_DTYPE)
        dob, doxb = do_ref[...].astype(MXU_DTYPE), dox_ref[...].astype(MXU_DTYPE)
        slope_d = sl_ref[:, :1] * float(d)
        lrow, lxrow = l_ref[:, :1], lx_ref[:, :1]
        drow, dxrow = dl_ref[:, :1], dlx_ref[:, :1]
        pc = jnp.exp(_dil_scores(qn, kcn, scale, slope_d, False, True) - lrow)
        pp = jnp.exp(_dil_scores(qn, kpn, scale, slope_d, True, n > 0) - lrow)
        dsc = pc * (f32dot(dob, vc, NT) - drow)
        dsp = pp * (f32dot(dob, vp, NT) - drow)
        dqn = (jnp.dot(dsc.astype(MXU_DTYPE), kcn, preferred_element_type=jnp.float32)
               + jnp.dot(dsp.astype(MXU_DTYPE), kpn, preferred_element_type=jnp.float32)) * scale
        dq, dgq = _norm_bwd(q_ref[...], gqv, dqn, LANES)
        dq_ref[...] = dq.astype(dq_ref.dtype)
        px = jnp.exp(_dil_scores(qxn, kcn, scale, slope_d, True, n < nb - 1) - lxrow)
        dsx = px * (f32dot(doxb, vc, NT) - dxrow)
        dkn = (f32dot(dsc.astype(MXU_DTYPE), qn, TN) + f32dot(dsx.astype(MXU_DTYPE), qxn, TN)) * scale
        dk, dgk = _norm_bwd(kc_ref[...], gkv, dkn, LANES)
        dk_ref[...] = dk.astype(dk_ref.dtype)
        dv_ref[...] = (f32dot(pc.astype(MXU_DTYPE), dob, TN) + f32dot(px.astype(MXU_DTYPE), doxb, TN)).astype(dv_ref.dtype)
        _accumulate(dgq_ref, jnp.sum(dgq, axis=0, keepdims=True), first)
        _accumulate(dgk_ref, jnp.sum(dgk, axis=0, keepdims=True), first)

    vec = pl.BlockSpec((1, LANES), lambda r, n, h: (0, 0))
    out = hspec(0)
    dq, dk, dv, dgq, dgk = pl.pallas_call(
        body, name=name,
        out_shape=(_act((S // d, d * W)), _act((S // d, d * W)), _act((S // d, d * W)), _f32((1, LANES)), _f32((1, LANES))),
        grid=(d, nb, Hd),
        in_specs=[spec(0, 0), spec(0, 1), spec(1, 0), spec(1, -1), spec(2, 0), spec(2, -1), hspec(0), hspec(1), hspec(0), hspec(1),
                  hspec(0), hspec(1), vec, vec, pl.BlockSpec((None, None, 1, LANES), lambda r, n, h: (g, h, 0, 0))],
        out_specs=(out, out, out, vec, vec), compiler_params=_params("arbitrary", "arbitrary", "arbitrary"),
    )(view, view, view, view, view, view, hview(do), hview(do), hview(lse), hview(lse), hview(delta), hview(delta), gq, gk, slopes)
    return dq.reshape(S, W), dk.reshape(S, W), dv.reshape(S, W), dgq, dgk


def adamw(w, g, m, v, layer, prev, name):
    L, r, c = w.shape
    tr, tc = _tile(r, 256, 8), _tile(c, 1024)
    c1 = 1.0 / (1.0 - ADAM_B1 ** ADAM_STEP)
    c2 = 1.0 / (1.0 - ADAM_B2 ** ADAM_STEP)

    def body(*refs):
        w_ref, g_ref, m_ref, v_ref = refs[:4]
        go_ref, d_ref, mo_ref, vo_ref = refs[-4:]
        gv = g_ref[...]
        mn = ADAM_B1 * m_ref[...] + (1.0 - ADAM_B1) * gv
        vn = ADAM_B2 * v_ref[...] + (1.0 - ADAM_B2) * (gv * gv)
        go_ref[...] = gv
        d_ref[...] = -ADAM_LR * ((mn * c1) / (jnp.sqrt(vn * c2) + ADAM_EPS) + ADAM_WD * w_ref[...])
        mo_ref[...] = mn
        vo_ref[...] = vn

    lay = pl.BlockSpec((None, tr, tc), lambda i, j: (layer, i, j))
    flat = pl.BlockSpec((tr, tc), lambda i, j: (i, j))
    ins = [w, g, m, v] + (list(prev) if prev is not None else [])
    in_specs = [lay, flat, lay, lay] + ([ANY] * 4 if prev is not None else [])
    return pl.pallas_call(
        body, name=name, out_shape=tuple(_f32((L, r, c)) for _ in range(4)), grid=(r // tr, c // tc),
        in_specs=in_specs, out_specs=(lay, lay, lay, lay),
        input_output_aliases=({4 + k: k for k in range(4)} if prev is not None else {}),
        compiler_params=_params("parallel", "parallel"))(*ins)


def _ffn_fwd(h, g, w_in, w_out, tag):
    xn = rmsnorm_fwd(h, g, tag + "_norm")
    u = matmul(xn, w_in, name=tag + "_in", tn=1408)
    a = swiglu_fwd(u, tag + "_act")
    out = matmul(a, w_out, scale=0.5, res=h, name=tag + "_out", tk=1408)
    return out, (h, xn, u, a)


def _ffn_bwd(dout, saved, g, w_in, w_out, tag):
    h, xn, u, a = saved
    dw_out = matmul(a, dout, ta=True, scale=0.5, out_dtype=WIRE_DTYPE, out_axis=0, name=tag + "_dwout", tm=1408)
    da = matmul(dout, w_out, tb=True, scale=0.5, name=tag + "_da", tn=1408)
    du = swiglu_bwd(u, da, tag + "_dact")
    dw_in = matmul(xn, du, ta=True, out_dtype=WIRE_DTYPE, out_axis=1, name=tag + "_dwin", tn=1408)
    dxn = matmul(du, w_in, tb=True, name=tag + "_dxn", tk=1408)
    dh, dg = rmsnorm_bwd(h, g, dxn, dout, tag + "_dnorm")
    return dh, dg, dw_in, dw_out


def _pad_gain(g):
    return jnp.pad(g, ((0, 0), (0, QK_PAD - QK_DIM)))


def _mla_fwd(h, P, W, tabs, H):
    g_mix, g_cq, g_ckv = P["mix_norm"][0:1], P["mla_g_cq"], P["mla_g_ckv"]
    pe_blk = (g_cq.shape[1] + g_ckv.shape[1]) // LANES
    xn = rmsnorm_fwd(h, g_mix, "mla_norm")
    lat = matmul(xn, W["mla_w_down"], name="mla_down", tn=W["mla_w_down"].shape[1])
    cq, ckv = mla_latent_fwd(lat, g_cq, g_ckv, "mla_latent")
    qraw = matmul(cq, W["mla_w_uq"], name="mla_uq")
    kvraw = matmul(ckv, W["mla_w_ukv"], name="mla_ukv")
    q = mla_q_prep_fwd(qraw, _pad_gain(P["mla_g_qn"]), tabs, H, "mla_qprep")
    k, v = mla_k_prep_fwd(kvraw, lat, _pad_gain(P["mla_g_kn"]), tabs, H, pe_blk, "mla_kprep")
    o, lse = mla_attention_fwd(q, k, v, "mla_attn")
    out = matmul(o, W["mla_w_o"], res=h, name="mla_o")
    return out, (h, xn, lat, cq, ckv, qraw, kvraw, q, k, v, o, lse, pe_blk)


def _mla_bwd(dout, saved, P, W, tabs, H):
    h, xn, lat, cq, ckv, qraw, kvraw, q, k, v, o, lse, pe_blk = saved
    G = {}
    G["mla_w_o"] = matmul(o, dout, ta=True, out_dtype=WIRE_DTYPE, out_axis=0, name="mla_dwo", tm=512)
    do = matmul(dout, W["mla_w_o"], tb=True, name="mla_do", tn=512)
    delta = attention_delta(do, o, H, "mla_delta")
    dq = mla_attention_bwd_dq(q, k, v, do, lse, delta, "mla_attn_dq")
    dk, dv = mla_attention_bwd_dkv(q, k, v, do, lse, delta, "mla_attn_dkv")
    dqraw, dgq = mla_q_prep_bwd(dq, qraw, _pad_gain(P["mla_g_qn"]), tabs, H, "mla_dqprep")
    dkvraw, dkpe, dgk = mla_k_prep_bwd(dk, dv, kvraw, lat, _pad_gain(P["mla_g_kn"]), tabs, H, pe_blk, "mla_dkprep")
    G["mla_w_uq"] = matmul(cq, dqraw, ta=True, out_dtype=WIRE_DTYPE, out_axis=1, name="mla_dwuq")
    dcq = matmul(dqraw, W["mla_w_uq"], tb=True, name="mla_dcq", tk=1024)
    G["mla_w_ukv"] = matmul(ckv, dkvraw, ta=True, out_dtype=WIRE_DTYPE, out_axis=1, name="mla_dwukv")
    dckv = matmul(dkvraw, W["mla_w_ukv"], tb=True, name="mla_dckv", tk=1024)
    dlat, dgcq, dgckv = mla_latent_bwd(dcq, dckv, dkpe, lat, P["mla_g_cq"], P["mla_g_ckv"], "mla_dlatent")
    G["mla_w_down"] = matmul(xn, dlat, ta=True, out_dtype=WIRE_DTYPE, out_axis=0, name="mla_dwdown", tm=512, tn=dlat.shape[1])
    dxn = matmul(dlat, W["mla_w_down"], tb=True, name="mla_dxn", tn=512, tk=dlat.shape[1])
    dh, dgm = rmsnorm_bwd(h, P["mix_norm"][0:1], dxn, dout, "mla_dnorm")
    G.update(mla_g_qn=dgq[:, :QK_DIM], mla_g_kn=dgk[:, :QK_DIM], mla_g_cq=dgcq, mla_g_ckv=dgckv)
    return dh, dgm, G


def _dil_fwd(h, P, W, slopes, Hd):
    G = len(DIL_PAIRS)
    xn = rmsnorm_fwd(h, P["mix_norm"][1:2], "dil_norm")
    qkv = matmul(xn, W["dil_w_qkv"], name="dil_qkv", tn=1152)
    os_, ls_ = [], []
    for g, (_, d) in enumerate(DIL_PAIRS):
        o_g, l_g = dilated_fwd(qkv, P["dil_g_qn"], P["dil_g_kn"], slopes, g, d, Hd, G, f"dil_attn{g}")
        os_.append(o_g)
        ls_.append(l_g)
    o, lse = dilated_merge(os_, ls_, "dil_merge")
    out = matmul(o, W["dil_w_o"], res=h, name="dil_o", tn=512)
    return out, (h, xn, qkv, o, lse)


def _dil_bwd(dout, saved, P, W, slopes, Hd):
    h, xn, qkv, o, lse = saved
    G = {}
    ngrp = len(DIL_PAIRS)
    G["dil_w_o"] = matmul(o, dout, ta=True, out_dtype=WIRE_DTYPE, out_axis=1, name="dil_dwo", tn=512)
    do = matmul(dout, W["dil_w_o"], tb=True, name="dil_do", tk=512)
    delta = dilated_delta(do, o, "dil_delta")
    parts = [dilated_bwd(qkv, do, lse, delta, P["dil_g_qn"], P["dil_g_kn"], slopes, g, d, Hd, ngrp, f"dil_dattn{g}")
             for g, (_, d) in enumerate(DIL_PAIRS)]
    dqkv = jnp.concatenate([p[kind] for kind in range(3) for p in parts], axis=1)
    G["dil_w_qkv"] = matmul(xn, dqkv, ta=True, out_dtype=WIRE_DTYPE, out_axis=1, name="dil_dwqkv", tn=1152)
    dxn = matmul(dqkv, W["dil_w_qkv"], tb=True, name="dil_dxn", tk=1152)
    dh, dgm = rmsnorm_bwd(h, P["mix_norm"][1:2], dxn, dout, "dil_dnorm")
    G["dil_g_qn"] = parts[0][3] + parts[1][3] + parts[2][3]
    G["dil_g_kn"] = parts[0][4] + parts[1][4] + parts[2][4]
    return dh, dgm, G


def local_step(x, target, P, W4):
    S, D = x.shape
    H = W4["mla_w_ukv"][0].shape[3] * N_CHIPS // QK_PAD
    Hd = W4["dil_w_o"][0].shape[2] // LANES
    tabs = rope_tables(S)
    slopes = _alibi_slopes(len(DIL_PAIRS), Hd)
    sw = lambda name, l: Sharded(W4[name][l], 0, SHARD_AXIS[name])
    Wm = {n: sw(n, 0) for n in ("mla_w_down", "mla_w_uq", "mla_w_ukv", "mla_w_o")}
    Wd = {n: sw(n, 0) for n in ("dil_w_qkv", "dil_w_o")}
    row = lambda name, i: P[name][i:i + 1]

    h = x
    saved = []
    for i in range(2):
        h, s1 = _ffn_fwd(h, row("ffn1_norm", i), sw("ffn1_w_in", i), sw("ffn1_w_out", i), f"l{i}_ffn1")
        h, sm = _mla_fwd(h, P, Wm, tabs, H) if i == 0 else _dil_fwd(h, P, Wd, slopes, Hd)
        h, s2 = _ffn_fwd(h, row("ffn2_norm", i), sw("ffn2_w_in", i), sw("ffn2_w_out", i), f"l{i}_ffn2")
        saved.append((s1, sm, s2))
    loss, dh = loss_head(h, target, "loss")

    gw = {n: [None, None] for n in ("ffn1_w_in", "ffn1_w_out", "ffn2_w_in", "ffn2_w_out")}
    gs = {n: [None, None] for n in ("ffn1_norm", "mix_norm", "ffn2_norm")}
    for i in (1, 0):
        s1, sm, s2 = saved[i]
        dh, gs["ffn2_norm"][i], gw["ffn2_w_in"][i], gw["ffn2_w_out"][i] = _ffn_bwd(
            dh, s2, row("ffn2_norm", i), sw("ffn2_w_in", i), sw("ffn2_w_out", i), f"l{i}_ffn2")
        dh, gs["mix_norm"][i], gm = _mla_bwd(dh, sm, P, Wm, tabs, H) if i == 0 else _dil_bwd(dh, sm, P, Wd, slopes, Hd)
        for n, val in gm.items():
            if n in SHARD_AXIS:
                gw[n] = [val]
            else:
                gs[n] = [val]
        dh, gs["ffn1_norm"][i], gw["ffn1_w_in"][i], gw["ffn1_w_out"][i] = _ffn_bwd(
            dh, s1, row("ffn1_norm", i), sw("ffn1_w_in", i), sw("ffn1_w_out", i), f"l{i}_ffn1")
    gsmall = {n: jnp.concatenate(v, axis=0) for n, v in gs.items()}
    return loss, dh, gw, gsmall


def _pad_heads(w, real, padded):
    lead, n = w.shape[:-1], w.shape[-1] // real
    w = jnp.pad(w.reshape(*lead, n, real), [(0, 0)] * (len(lead) + 1) + [(0, padded - real)])
    return w.reshape(*lead, n * padded)


def _unpad_heads(w, real, padded):
    lead, n = w.shape[:-1], w.shape[-1] // padded
    return w.reshape(*lead, n, padded)[..., :real].reshape(*lead, n * real)


def _pack_small(gs):
    flat = jnp.concatenate([gs[n].reshape(-1) for n in SMALL])
    rows = -(-flat.shape[0] // LANES)
    rows = -(-rows // 8) * 8
    return jnp.pad(flat, (0, rows * LANES - flat.shape[0])).reshape(rows, LANES)


def _unpack_small(packed, like):
    flat, out, off = packed.reshape(-1), {}, 0
    for n in SMALL:
        size = int(np.prod(like[n].shape))
        out[n] = flat[off:off + size].reshape(like[n].shape)
        off += size
    return out


def kernel(x, ffn1_norm, ffn1_w_in, ffn1_w_out, mix_norm, ffn2_norm, ffn2_w_in, ffn2_w_out, mla_w_down, mla_g_cq, mla_g_ckv, mla_w_uq, mla_w_ukv, mla_g_qn, mla_g_kn, mla_w_o, dil_w_qkv, dil_g_qn, dil_g_kn, dil_w_o, loss_target, m_ffn1_norm, m_ffn1_w_in, m_ffn1_w_out, m_mix_norm, m_ffn2_norm, m_ffn2_w_in, m_ffn2_w_out, m_mla_w_down, m_mla_g_cq, m_mla_g_ckv, m_mla_w_uq, m_mla_w_ukv, m_mla_g_qn, m_mla_g_kn, m_mla_w_o, m_dil_w_qkv, m_dil_g_qn, m_dil_g_kn, m_dil_w_o, v_ffn1_norm, v_ffn1_w_in, v_ffn1_w_out, v_mix_norm, v_ffn2_norm, v_ffn2_w_in, v_ffn2_w_out, v_mla_w_down, v_mla_g_cq, v_mla_g_ckv, v_mla_w_uq, v_mla_w_ukv, v_mla_g_qn, v_mla_g_kn, v_mla_w_o, v_dil_w_qkv, v_dil_g_qn, v_dil_g_kn, v_dil_w_o):
    args = dict(locals())
    w = {n: args[n] for n in WEIGHTS}
    m = {n: args["m_" + n] for n in WEIGHTS}
    v = {n: args["v_" + n] for n in WEIGHTS}
    cx, cy, cc = _me()
    core = jnp.reshape(cc, (1,)).astype(jnp.int32)
    shard = jnp.reshape(2 * cx + cy, (1,)).astype(jnp.int32)
    pe_pad = LANES - ROPE_DIM

    W4 = {}
    for n in BIG:
        L, r, c = w[n].shape
        W4[n] = []
        for l in range(L):
            full = all_gather_weight(w[n][l].astype(WIRE_DTYPE).reshape(1, 2, r // 2, c), f"ag_{n}{l}").reshape(N_CHIPS, 1, r, c)
            if n == "mla_w_down":
                full = jnp.pad(full, ((0, 0), (0, 0), (0, 0), (0, pe_pad)))
            if n == "mla_w_uq":
                full = _pad_heads(full, QK_DIM, QK_PAD)
            W4[n].append(full)

    loss, grad_x, gw, gsmall = local_step(x[0], loss_target[0], {n: w[n] for n in SMALL}, W4)
    loss = lax.psum(loss, ("x", "y", "c"))

    outs = {}
    for n in BIG:
        prev = None
        for l, g4 in enumerate(gw[n]):
            if n == "mla_w_down":
                g4 = g4[..., :g4.shape[-1] - pe_pad]
            if n == "mla_w_uq":
                g4 = _unpad_heads(g4, QK_DIM, QK_PAD)
            g = reduce_scatter_grad(g4, core, shard, f"rs_{n}{l}")
            prev = adamw(w[n], g, m[n], v[n], l, prev, f"adamw_{n}{l}")
        outs[n] = prev
    small = _unpack_small(all_reduce_small(_pack_small(gsmall), "ar_small"), gsmall)
    for n in SMALL:
        outs[n] = tuple(o[0] for o in adamw(w[n][None], small[n], m[n][None], v[n][None], 0, None, f"adamw_{n}"))

    return (loss, grad_x[None], *[outs[n][0] for n in WEIGHTS], *[outs[n][1] for n in WEIGHTS],
            *[outs[n][2] for n in WEIGHTS], *[outs[n][3] for n in WEIGHTS])
```

```python
import math

import numpy as np
import jax
import jax.numpy as jnp
from jax import lax
from jax.experimental import pallas as pl
from jax.experimental.pallas import tpu as pltpu

MXU_DTYPE = jnp.bfloat16
WIRE_DTYPE = jnp.bfloat16
EPS = 1e-6
NEG = -1e30
N_CHIPS = 4
MESH = pl.DeviceIdType.MESH
ANY = pl.BlockSpec(memory_space=pl.ANY)
LANES = 128

MLA_HEADS = 16
NOPE_DIM = 128
ROPE_DIM = 64
QK_DIM = NOPE_DIM + ROPE_DIM
QK_PAD = 2 * LANES
ROPE_THETA = 10000.0
DIL_PAIRS = ((128, 1), (512, 4), (2048, 16))
DIL_HEADS = 8
BLK = 128

ADAM_LR = 0.001
ADAM_B1 = 0.9
ADAM_B2 = 0.999
ADAM_EPS = 1e-08
ADAM_WD = 0.01
ADAM_STEP = 10

NT = (((1,), (1,)), ((), ()))
TN = (((0,), (0,)), ((), ()))

SHARD_AXIS = {"ffn1_w_in": 1, "ffn1_w_out": 0, "ffn2_w_in": 1, "ffn2_w_out": 0, "mla_w_down": 0, "mla_w_uq": 1,
              "mla_w_ukv": 1, "mla_w_o": 0, "dil_w_qkv": 1, "dil_w_o": 1}
BIG = tuple(SHARD_AXIS)
USE_ORDER = (("ffn1_w_in", "ffn1_w_out", "mla_w_down", "mla_w_uq", "mla_w_ukv", "mla_w_o", "ffn2_w_in", "ffn2_w_out"),
             ("ffn1_w_in", "ffn1_w_out", "dil_w_qkv", "dil_w_o", "ffn2_w_in", "ffn2_w_out"))
SMALL = ("ffn1_norm", "mix_norm", "ffn2_norm", "mla_g_cq", "mla_g_ckv", "mla_g_qn", "mla_g_kn", "dil_g_qn", "dil_g_kn")
WEIGHTS = ("ffn1_norm", "ffn1_w_in", "ffn1_w_out", "mix_norm", "ffn2_norm", "ffn2_w_in", "ffn2_w_out", "mla_w_down",
           "mla_g_cq", "mla_g_ckv", "mla_w_uq", "mla_w_ukv", "mla_g_qn", "mla_g_kn", "mla_w_o", "dil_w_qkv", "dil_g_qn",
           "dil_g_kn", "dil_w_o")


def _tile(dim, pref, mult=LANES):
    if dim <= pref:
        return dim
    t = (pref // mult) * mult
    while t >= mult:
        if dim % t == 0:
            return t
        t -= mult
    return dim


def _params(*sem):
    return pltpu.CompilerParams(dimension_semantics=sem)


def _f32(shape):
    return jax.ShapeDtypeStruct(shape, jnp.float32)


def _act(shape):
    return jax.ShapeDtypeStruct(shape, MXU_DTYPE)


class Sharded:
    def __init__(self, arr, layer, axis):
        self.arr, self.layer, self.axis = arr, layer, axis
        _, _, r, c = arr.shape
        self.shape = (N_CHIPS * r, c) if axis == 0 else (r, N_CHIPS * c)
        self.per = r if axis == 0 else c

    def spec(self, tr, tc, rc_of):
        l = self.layer
        if self.axis == 0:
            n = self.per // tr

            def imap(*g):
                bi, bj = rc_of(*g)
                return (bi // n, l, bi % n, bj)
        else:
            n = self.per // tc

            def imap(*g):
                bi, bj = rc_of(*g)
                return (bj // n, l, bi, bj % n)
        return pl.BlockSpec((None, None, tr, tc), imap)


def _spec2(tr, tc, rc_of):
    return pl.BlockSpec((tr, tc), lambda *g: rc_of(*g))


def matmul(a, b, *, ta=False, tb=False, out_dtype=jnp.float32, scale=None, res=None, out_axis=None,
           name, tm=1024, tn=1024, tk=512):
    am, ak = (a.shape[1], a.shape[0]) if ta else a.shape
    bk, bn = (b.shape[1], b.shape[0]) if tb else b.shape
    assert ak == bk, (name, a.shape, b.shape, ta, tb)
    M, N, K = am, bn, ak

    def per(x, axis):
        return x.per if isinstance(x, Sharded) and x.axis == axis else None

    def pick(dim, pref, *pers):
        for p in pers:
            if p is not None:
                return _tile(p, pref)
        return _tile(dim, pref)

    tm = pick(M, tm, per(a, 1 if ta else 0), M // N_CHIPS if out_axis == 0 else None)
    tn = pick(N, tn, per(b, 0 if tb else 1), N // N_CHIPS if out_axis == 1 else None)
    tk = pick(K, tk, per(a, 0 if ta else 1), per(b, 1 if tb else 0))
    assert M % tm == 0 and N % tn == 0 and K % tk == 0, (name, M, N, K, tm, tn, tk)
    nk = K // tk

    a_rc = (lambda i, j, k: (k, i)) if ta else (lambda i, j, k: (i, k))
    b_rc = (lambda i, j, k: (j, k)) if tb else (lambda i, j, k: (k, j))
    a_blk = (tk, tm) if ta else (tm, tk)
    b_blk = (tn, tk) if tb else (tk, tn)
    a_spec = a.spec(*a_blk, a_rc) if isinstance(a, Sharded) else _spec2(*a_blk, a_rc)
    b_spec = b.spec(*b_blk, b_rc) if isinstance(b, Sharded) else _spec2(*b_blk, b_rc)
    dn = (((0 if ta else 1,), (1 if tb else 0,)), ((), ()))
    has_res = res is not None

    def body(*refs):
        if has_res:
            a_ref, b_ref, r_ref, o_ref, acc = refs
        else:
            a_ref, b_ref, o_ref, acc = refs
        k = pl.program_id(2)

        @pl.when(k == 0)
        def _():
            acc[...] = jnp.zeros_like(acc)

        acc[...] += lax.dot_general(a_ref[...].astype(MXU_DTYPE), b_ref[...].astype(MXU_DTYPE), dn,
                                    preferred_element_type=jnp.float32)

        @pl.when(k == nk - 1)
        def _():
            r = acc[...]
            if scale is not None:
                r = r * scale
            if has_res:
                r = r + r_ref[...]
            o_ref[...] = r.astype(o_ref.dtype)

    in_specs = [a_spec, b_spec]
    args = [a.arr if isinstance(a, Sharded) else a, b.arr if isinstance(b, Sharded) else b]
    if has_res:
        in_specs.append(_spec2(tm, tn, lambda i, j, k: (i, j)))
        args.append(res)
    o_rc = lambda i, j, k: (i, j)
    if out_axis is None:
        out_shape = jax.ShapeDtypeStruct((M, N), out_dtype)
        out_spec = _spec2(tm, tn, o_rc)
    else:
        shp = (N_CHIPS, 1, M // N_CHIPS, N) if out_axis == 0 else (N_CHIPS, 1, M, N // N_CHIPS)
        out_shape = jax.ShapeDtypeStruct(shp, out_dtype)
        out_spec = Sharded(out_shape, 0, out_axis).spec(tm, tn, o_rc)
    return pl.pallas_call(
        body, name=name, out_shape=out_shape, grid=(M // tm, N // tn, nk),
        in_specs=in_specs, out_specs=out_spec,
        scratch_shapes=[pltpu.VMEM((tm, tn), jnp.float32)],
        compiler_params=_params("parallel", "parallel", "arbitrary"),
    )(*args)


def _me():
    return lax.axis_index("x"), lax.axis_index("y"), lax.axis_index("c")


def _other_chips(x, y):
    return [(1 - x, y), (x, 1 - y), (1 - x, 1 - y)]


HBM = pl.BlockSpec(memory_space=pltpu.HBM)
SEM = pl.BlockSpec(memory_space=pltpu.SEMAPHORE)
N_PEERS = 3
TOKEN = jax.ShapeDtypeStruct((8, LANES), jnp.float32)


def _split_params():
    return pltpu.CompilerParams(has_side_effects=pltpu.SideEffectType.DATAFLOW_SIDE_EFFECTING)


def _in_hbm(a):
    return pltpu.with_memory_space_constraint(a, pltpu.HBM)


def exchange_start(src, land, src_of, dst_of, name):
    def body(src_ref, land_ref, *outs):
        sems, token = outs[:2 * N_PEERS], outs[-1]
        x, y, cc = _me()
        for j, (px, py) in enumerate(_other_chips(x, y)):
            pltpu.make_async_remote_copy(
                src_ref=src_of(src_ref, j, (px, py), (x, y, cc)), dst_ref=dst_of(land_ref, j, (px, py), (x, y, cc)),
                send_sem=sems[j], recv_sem=sems[N_PEERS + j], device_id=(px, py, cc), device_id_type=MESH).start()
        token[...] = jnp.zeros_like(token)

    outs = pl.pallas_call(
        body, name=name,
        out_shape=(pltpu.SemaphoreType.DMA(()),) * (2 * N_PEERS) + (pltpu.HBM(src.shape, src.dtype), pltpu.HBM(land.shape, land.dtype), TOKEN),
        in_specs=(HBM, HBM), out_specs=(SEM,) * (2 * N_PEERS) + (HBM, HBM, pl.BlockSpec(memory_space=pltpu.VMEM)),
        input_output_aliases={0: 2 * N_PEERS, 1: 2 * N_PEERS + 1}, compiler_params=_split_params(),
    )(_in_hbm(src), _in_hbm(land))
    return outs[:2 * N_PEERS], outs[2 * N_PEERS], outs[2 * N_PEERS + 1], outs[-1]


def exchange_wait(sems, src, land, src_of, dst_of, after, name):
    def body(src_ref, land_ref, *rest):
        sems_ = rest[:2 * N_PEERS]
        x, y, cc = _me()
        for j, (px, py) in enumerate(_other_chips(x, y)):
            cp = pltpu.make_async_remote_copy(
                src_ref=src_of(src_ref, j, (px, py), (x, y, cc)), dst_ref=dst_of(land_ref, j, (px, py), (x, y, cc)),
                send_sem=sems_[j], recv_sem=sems_[N_PEERS + j], device_id=(px, py, cc), device_id_type=MESH)
            cp.wait_send()
            cp.wait_recv()

    return pl.pallas_call(
        body, name=name, out_shape=(pltpu.HBM(src.shape, src.dtype), pltpu.HBM(land.shape, land.dtype)),
        in_specs=(HBM, HBM) + (SEM,) * (2 * N_PEERS) + (ANY,), out_specs=(HBM, HBM),
        input_output_aliases={0: 0, 1: 1}, compiler_params=_split_params(),
    )(src, land, *sems, after)[1]


def _ag_src(ref, j, chip, me):
    return ref.at[:, me[2]]


def _ag_dst(ref, j, chip, me):
    return ref.at[2 * me[0] + me[1], :, me[2]]


def _ag_got(ref, j, chip, me):
    return ref.at[2 * chip[0] + chip[1], :, me[2]]


def all_gather_finish(w, land, name):
    def body(w_ref, land_ref, o_ref, send_sems, recv_sems, local_sem):
        x, y, cc = _me()
        mine = pltpu.make_async_copy(w_ref, o_ref.at[2 * x + y], local_sem)
        mine.start()
        cps = []
        for j, (px, py) in enumerate(_other_chips(x, y)):
            cp = pltpu.make_async_remote_copy(
                src_ref=o_ref.at[2 * px + py, :, cc], dst_ref=o_ref.at[2 * px + py, :, cc], send_sem=send_sems.at[j],
                recv_sem=recv_sems.at[j], device_id=(x, y, 1 - cc), device_id_type=MESH)
            cp.start()
            cps.append(cp)
        for j, (px, py) in enumerate(_other_chips(x, y)):
            got = o_ref.at[2 * px + py, :, 1 - cc]
            pltpu.make_async_remote_copy(src_ref=got, dst_ref=got, send_sem=send_sems.at[j], recv_sem=recv_sems.at[j],
                                         device_id=(x, y, 1 - cc), device_id_type=MESH).wait_recv()
        for cp in cps:
            cp.wait_send()
        mine.wait()

    return pl.pallas_call(
        body, name=name, out_shape=jax.ShapeDtypeStruct(land.shape, land.dtype), in_specs=[ANY, ANY], out_specs=ANY,
        input_output_aliases={1: 0},
        scratch_shapes=[pltpu.SemaphoreType.DMA((N_PEERS,)), pltpu.SemaphoreType.DMA((N_PEERS,)), pltpu.SemaphoreType.DMA],
    )(w, land)


def _rs_src(ref, j, chip, me):
    return ref.at[2 * chip[0] + chip[1]]


def _rs_dst(ref, j, chip, me):
    return ref.at[j]


def sibling_send_halves(g, after, name):
    n, L, two, h, c = g.shape

    def body(g_ref, after_ref, o_ref, send_sem, recv_sem):
        x, y, cc = _me()
        cp = pltpu.make_async_remote_copy(src_ref=g_ref.at[:, :, 1 - cc], dst_ref=o_ref, send_sem=send_sem,
                                          recv_sem=recv_sem, device_id=(x, y, 1 - cc), device_id_type=MESH)
        cp.start()
        cp.wait()

    return pl.pallas_call(
        body, name=name, out_shape=jax.ShapeDtypeStruct((n, L, h, c), g.dtype),
        in_specs=[ANY, ANY], out_specs=ANY,
        scratch_shapes=[pltpu.SemaphoreType.DMA, pltpu.SemaphoreType.DMA],
    )(g, after)


def sibling_gather_halves(r, name):
    L, h, c = r.shape

    def body(r_ref, o_ref, send_sem, recv_sem, local_sem):
        x, y, cc = _me()
        mine = pltpu.make_async_copy(r_ref, o_ref.at[:, cc], local_sem)
        mine.start()
        cp = pltpu.make_async_remote_copy(src_ref=r_ref, dst_ref=o_ref.at[:, cc], send_sem=send_sem,
                                          recv_sem=recv_sem, device_id=(x, y, 1 - cc), device_id_type=MESH)
        cp.start()
        cp.wait()
        mine.wait()

    return pl.pallas_call(
        body, name=name, out_shape=jax.ShapeDtypeStruct((L, 2, h, c), r.dtype),
        in_specs=[ANY], out_specs=ANY,
        scratch_shapes=[pltpu.SemaphoreType.DMA, pltpu.SemaphoreType.DMA, pltpu.SemaphoreType.DMA],
    )(r)


def add_sibling(g, r1, core, name):
    n, L, two, h, c = g.shape
    th = _tile(h, 512, 16)
    tc = _tile(c, 1024)

    def body(core_ref, g_ref, r_ref, o_ref):
        o_ref[...] = (g_ref[...].astype(jnp.float32) + r_ref[...].astype(jnp.float32)).astype(o_ref.dtype)

    grid_spec = pltpu.PrefetchScalarGridSpec(
        num_scalar_prefetch=1, grid=(n, L, h // th, c // tc),
        in_specs=[pl.BlockSpec((None, None, None, th, tc), lambda s, l, i, j, core: (s, l, core[0], i, j)),
                  pl.BlockSpec((None, None, th, tc), lambda s, l, i, j, core: (s, l, i, j))],
        out_specs=pl.BlockSpec((None, None, th, tc), lambda s, l, i, j, core: (s, l, i, j)))
    return pl.pallas_call(body, name=name, grid_spec=grid_spec, out_shape=jax.ShapeDtypeStruct((n, L, h, c), WIRE_DTYPE),
                          compiler_params=_params("parallel", "parallel", "parallel", "parallel"))(core, g, r1)


def add_chips(p, r2, shard, name):
    n, L, h, c = p.shape
    th = _tile(h, 512, 16)
    tc = _tile(c, 1024)

    def body(shard_ref, p_ref, r_ref, o_ref):
        acc = p_ref[...].astype(jnp.float32)
        for j in range(3):
            acc = acc + r_ref[j].astype(jnp.float32)
        o_ref[...] = acc

    grid_spec = pltpu.PrefetchScalarGridSpec(
        num_scalar_prefetch=1, grid=(L, h // th, c // tc),
        in_specs=[pl.BlockSpec((None, None, th, tc), lambda l, i, j, shard: (shard[0], l, i, j)),
                  pl.BlockSpec((3, None, th, tc), lambda l, i, j, shard: (0, l, i, j))],
        out_specs=pl.BlockSpec((None, th, tc), lambda l, i, j, shard: (l, i, j)))
    return pl.pallas_call(body, name=name, grid_spec=grid_spec, out_shape=_f32((L, h, c)),
                          compiler_params=_params("parallel", "parallel", "parallel"))(shard, p, r2)


def reduce_scatter_start(g4, core, after, name):
    n, L, r, c = g4.shape
    g = g4.reshape(n, L, 2, r // 2, c)
    r1 = sibling_send_halves(g, after, name + "_d2d")
    p = add_sibling(g, r1, core, name + "_add1")
    land = lax.empty((N_PEERS, L, r // 2, c), p.dtype)
    return exchange_start(p, land, _rs_src, _rs_dst, name + "_ici_start")


def reduce_scatter_finish(pending, shard, after, name):
    sems, p, land, _ = pending
    r2 = exchange_wait(sems, p, land, _rs_src, _rs_dst, after, name + "_ici_wait")
    red = add_chips(p, r2, shard, name + "_add2")
    L, h, c = red.shape
    return sibling_gather_halves(red, name + "_gather").reshape(2 * h, c)


def all_reduce_small(v, name):
    R, C = v.shape

    def body(v_ref, o_ref, buf, send_sems, recv_sems):
        x, y, cc = _me()
        buf[0] = v_ref[...]
        cps = []
        for k in range(1, 8):
            dx, dy, dc = (k >> 2) & 1, (k >> 1) & 1, k & 1
            to = (x ^ dx, y ^ dy, cc ^ dc)
            cp = pltpu.make_async_remote_copy(src_ref=v_ref, dst_ref=buf.at[k], send_sem=send_sems.at[k],
                                              recv_sem=recv_sems.at[k], device_id=to, device_id_type=MESH)
            cp.start()
            cps.append(cp)
        for cp in cps:
            cp.wait()
        me = 4 * x + 2 * y + cc
        acc = buf[me]
        for a in range(1, 8):
            acc = acc + buf[a ^ me]
        o_ref[...] = acc

    vm = pl.BlockSpec(memory_space=pltpu.VMEM)
    return pl.pallas_call(
        body, name=name, out_shape=_f32((R, C)), in_specs=[vm], out_specs=vm,
        scratch_shapes=[pltpu.VMEM((8, R, C), jnp.float32), pltpu.SemaphoreType.DMA((8,)), pltpu.SemaphoreType.DMA((8,))],
    )(v)


def _rstd(x, n):
    return lax.rsqrt(jnp.sum(x * x, axis=-1, keepdims=True) * (1.0 / n) + EPS)


def _accumulate(ref, part, first):
    @pl.when(first)
    def _():
        ref[...] = part

    @pl.when(jnp.logical_not(first))
    def _():
        ref[...] += part


def rmsnorm_fwd(x, g, name):
    S, D = x.shape
    ts = _tile(S, 256, 8)

    def body(x_ref, g_ref, o_ref):
        xv = x_ref[...]
        o_ref[...] = (xv * _rstd(xv, D) * g_ref[...]).astype(o_ref.dtype)

    return pl.pallas_call(
        body, name=name, out_shape=_act((S, D)), grid=(S // ts,),
        in_specs=[pl.BlockSpec((ts, D), lambda i: (i, 0)), pl.BlockSpec((1, D), lambda i: (0, 0))],
        out_specs=pl.BlockSpec((ts, D), lambda i: (i, 0)), compiler_params=_params("parallel"))(x, g)


def _norm_bwd(x, g, dy, n):
    r = _rstd(x, n)
    xh = x * r
    dxh = dy * g
    dx = r * (dxh - xh * (jnp.sum(dxh * xh, axis=-1, keepdims=True) * (1.0 / n)))
    return dx, dy * xh


def rmsnorm_bwd(x, g, dy, dres, name):
    S, D = x.shape
    ts = _tile(S, 256, 8)

    def body(x_ref, g_ref, dy_ref, dres_ref, dx_ref, dg_ref):
        dx, dgp = _norm_bwd(x_ref[...], g_ref[...], dy_ref[...], D)
        dx_ref[...] = dres_ref[...] + dx
        _accumulate(dg_ref, jnp.sum(dgp, axis=0, keepdims=True), pl.program_id(0) == 0)

    row = pl.BlockSpec((ts, D), lambda i: (i, 0))
    vec = pl.BlockSpec((1, D), lambda i: (0, 0))
    return pl.pallas_call(
        body, name=name, out_shape=(_f32((S, D)), _f32((1, D))), grid=(S // ts,),
        in_specs=[row, vec, row, row], out_specs=(row, vec), compiler_params=_params("arbitrary"))(x, g, dy, dres)


def _sigmoid(x):
    return 1.0 / (1.0 + jnp.exp(-x))


def swiglu_fwd(u, name):
    S, F2 = u.shape
    F = F2 // 2
    ts, tf = _tile(S, 512, 8), _tile(F, 512)
    nf = F // tf

    def body(g_ref, u_ref, o_ref):
        gt = g_ref[...]
        o_ref[...] = (gt * _sigmoid(gt) * u_ref[...]).astype(o_ref.dtype)

    return pl.pallas_call(
        body, name=name, out_shape=_act((S, F)), grid=(S // ts, nf),
        in_specs=[pl.BlockSpec((ts, tf), lambda i, j: (i, j)), pl.BlockSpec((ts, tf), lambda i, j: (i, j + nf))],
        out_specs=pl.BlockSpec((ts, tf), lambda i, j: (i, j)), compiler_params=_params("parallel", "parallel"))(u, u)


def swiglu_bwd(u, da, name):
    S, F2 = u.shape
    F = F2 // 2
    ts, tf = _tile(S, 512, 8), _tile(F, 512)
    nf = F // tf

    def body(g_ref, u_ref, da_ref, o_ref):
        gt, up, d = g_ref[...], u_ref[...], da_ref[...]
        s = _sigmoid(gt)
        dgate = d * up * (s * (1.0 + gt * (1.0 - s)))
        dup = d * (gt * s)
        o_ref[...] = jnp.where(pl.program_id(1) < nf, dgate, dup).astype(o_ref.dtype)

    return pl.pallas_call(
        body, name=name, out_shape=_act((S, F2)), grid=(S // ts, 2 * nf),
        in_specs=[pl.BlockSpec((ts, tf), lambda i, j: (i, j % nf)), pl.BlockSpec((ts, tf), lambda i, j: (i, j % nf + nf)),
                  pl.BlockSpec((ts, tf), lambda i, j: (i, j % nf))],
        out_specs=pl.BlockSpec((ts, tf), lambda i, j: (i, j)), compiler_params=_params("parallel", "parallel"))(u, u, da)


def loss_head(y, t, name):
    S, D = y.shape
    ts = _tile(S, 256, 8)

    def body(y_ref, t_ref, dy_ref, l_ref):
        e = y_ref[...] - t_ref[...]
        dy_ref[...] = e * (1.0 / D)
        l_ref[...] = jnp.full(l_ref.shape, 0.5 * jnp.sum(jnp.sum(e * e, axis=-1, keepdims=True) * (1.0 / D)), jnp.float32)

    row = pl.BlockSpec((ts, D), lambda i: (i, 0))
    dy, parts = pl.pallas_call(
        body, name=name, out_shape=(_f32((S, D)), _f32((S // ts, 8, LANES))), grid=(S // ts,),
        in_specs=[row, row], out_specs=(row, pl.BlockSpec((None, 8, LANES), lambda i: (i, 0, 0))),
        compiler_params=_params("parallel"))(y, t)
    return jnp.sum(parts[:, 0, 0]), dy


def rope_tables(S):
    inv = 1.0 / (ROPE_THETA ** (jnp.arange(0, ROPE_DIM, 2, dtype=jnp.float32) / ROPE_DIM))
    ang = jnp.arange(S, dtype=jnp.float32)[:, None] * inv[None, :]
    c, s = jnp.cos(ang), jnp.sin(ang)
    z = jnp.zeros_like(c)
    return (jnp.concatenate([c, c, z, z], axis=1), jnp.concatenate([-s, z, z, z], axis=1),
            jnp.concatenate([z, s, z, z], axis=1))


def _rope(x, cos, sa, sb):
    return x * cos + pltpu.roll(x, 96, 1) * sa + pltpu.roll(x, 32, 1) * sb


def _rope_t(d, cos, sa, sb):
    return d * cos + pltpu.roll(d * sa, 32, 1) + pltpu.roll(d * sb, 96, 1)


def _head_norm(x1, x2, g):
    r = lax.rsqrt((jnp.sum(x1 * x1, axis=-1, keepdims=True) + jnp.sum(x2 * x2, axis=-1, keepdims=True)) * (1.0 / QK_DIM) + EPS)
    return x1 * r * g[:, :LANES], x2 * r * g[:, LANES:], r


def _head_norm_bwd(x1, x2, g, d1, d2):
    _, _, r = _head_norm(x1, x2, g)
    h1, h2 = x1 * r, x2 * r
    e1, e2 = d1 * g[:, :LANES], d2 * g[:, LANES:]
    m = (jnp.sum(e1 * h1, axis=-1, keepdims=True) + jnp.sum(e2 * h2, axis=-1, keepdims=True)) * (1.0 / QK_DIM)
    return r * (e1 - h1 * m), r * (e2 - h2 * m), d1 * h1, d2 * h2


def mla_latent_fwd(lat, g_cq, g_ckv, name):
    S, W = lat.shape
    QL, KL = g_cq.shape[1], g_ckv.shape[1]
    ts = _tile(S, 256, 8)

    def body(l_ref, gq_ref, gk_ref, cq_ref, ckv_ref):
        a, b = l_ref[:, :QL], l_ref[:, QL:QL + KL]
        cq_ref[...] = (a * _rstd(a, QL) * gq_ref[...]).astype(cq_ref.dtype)
        ckv_ref[...] = (b * _rstd(b, KL) * gk_ref[...]).astype(ckv_ref.dtype)

    return pl.pallas_call(
        body, name=name, out_shape=(_act((S, QL)), _act((S, KL))), grid=(S // ts,),
        in_specs=[pl.BlockSpec((ts, W), lambda i: (i, 0)), pl.BlockSpec((1, QL), lambda i: (0, 0)),
                  pl.BlockSpec((1, KL), lambda i: (0, 0))],
        out_specs=(pl.BlockSpec((ts, QL), lambda i: (i, 0)), pl.BlockSpec((ts, KL), lambda i: (i, 0))),
        compiler_params=_params("parallel"))(lat, g_cq, g_ckv)


def mla_latent_bwd(dcq, dckv, dkpe, lat, g_cq, g_ckv, name):
    S, W = lat.shape
    QL, KL = g_cq.shape[1], g_ckv.shape[1]
    ts = _tile(S, 256, 8)

    def body(dq_ref, dk_ref, dp_ref, l_ref, gq_ref, gk_ref, o_ref, dgq_ref, dgk_ref):
        first = pl.program_id(0) == 0
        da, ga = _norm_bwd(l_ref[:, :QL], gq_ref[...], dq_ref[...], QL)
        db, gb = _norm_bwd(l_ref[:, QL:QL + KL], gk_ref[...], dk_ref[...], KL)
        o_ref[:, :QL] = da.astype(o_ref.dtype)
        o_ref[:, QL:QL + KL] = db.astype(o_ref.dtype)
        o_ref[:, QL + KL:] = dp_ref[...].astype(o_ref.dtype)
        _accumulate(dgq_ref, jnp.sum(ga, axis=0, keepdims=True), first)
        _accumulate(dgk_ref, jnp.sum(gb, axis=0, keepdims=True), first)

    row = lambda n: pl.BlockSpec((ts, n), lambda i: (i, 0))
    vec = lambda n: pl.BlockSpec((1, n), lambda i: (0, 0))
    return pl.pallas_call(
        body, name=name, out_shape=(_act((S, W)), _f32((1, QL)), _f32((1, KL))), grid=(S // ts,),
        in_specs=[row(QL), row(KL), row(LANES), row(W), vec(QL), vec(KL)], out_specs=(row(W), vec(QL), vec(KL)),
        compiler_params=_params("arbitrary"))(dcq, dckv, dkpe, lat, g_cq, g_ckv)


def mla_q_prep_fwd(qraw, g, tabs, H, name):
    S = qraw.shape[0]
    ts = _tile(S, 256, 8)

    def body(x_ref, g_ref, c_ref, a_ref, b_ref, o_ref):
        y1, y2, _ = _head_norm(x_ref[:, :LANES], x_ref[:, LANES:], g_ref[...])
        o_ref[:, :LANES] = y1.astype(o_ref.dtype)
        o_ref[:, LANES:] = _rope(y2, c_ref[...], a_ref[...], b_ref[...]).astype(o_ref.dtype)

    tab = pl.BlockSpec((ts, LANES), lambda i, h: (i, 0))
    return pl.pallas_call(
        body, name=name, out_shape=_act((H, S, QK_PAD)), grid=(S // ts, H),
        in_specs=[pl.BlockSpec((ts, QK_PAD), lambda i, h: (i, h)), pl.BlockSpec((1, QK_PAD), lambda i, h: (0, 0)), tab, tab, tab],
        out_specs=pl.BlockSpec((None, ts, QK_PAD), lambda i, h: (h, i, 0)),
        compiler_params=_params("parallel", "parallel"))(qraw, g, *tabs)


def mla_q_prep_bwd(dq, qraw, g, tabs, H, name):
    S = qraw.shape[0]
    ts = _tile(S, 256, 8)

    def body(d_ref, x_ref, g_ref, c_ref, a_ref, b_ref, o_ref, dg_ref):
        d2 = _rope_t(d_ref[:, LANES:], c_ref[...], a_ref[...], b_ref[...])
        dx1, dx2, g1, g2 = _head_norm_bwd(x_ref[:, :LANES], x_ref[:, LANES:], g_ref[...], d_ref[:, :LANES], d2)
        o_ref[:, :LANES] = dx1.astype(o_ref.dtype)
        o_ref[:, LANES:] = dx2.astype(o_ref.dtype)
        first = jnp.logical_and(pl.program_id(0) == 0, pl.program_id(1) == 0)
        part = jnp.concatenate([jnp.sum(g1, axis=0, keepdims=True), jnp.sum(g2, axis=0, keepdims=True)], axis=1)
        _accumulate(dg_ref, part, first)

    tab = pl.BlockSpec((ts, LANES), lambda i, h: (i, 0))
    vec = pl.BlockSpec((1, QK_PAD), lambda i, h: (0, 0))
    return pl.pallas_call(
        body, name=name, out_shape=(_act((S, H * QK_PAD)), _f32((1, QK_PAD))), grid=(S // ts, H),
        in_specs=[pl.BlockSpec((None, ts, QK_PAD), lambda i, h: (h, i, 0)), pl.BlockSpec((ts, QK_PAD), lambda i, h: (i, h)),
                  vec, tab, tab, tab],
        out_specs=(pl.BlockSpec((ts, QK_PAD), lambda i, h: (i, h)), vec),
        compiler_params=_params("arbitrary", "arbitrary"))(dq, qraw, g, *tabs)


def mla_k_prep_fwd(kvraw, lat, g, tabs, H, pe_blk, name):
    S = kvraw.shape[0]
    ts = _tile(S, 256, 8)

    def body(x_ref, p_ref, g_ref, c_ref, a_ref, b_ref, k_ref, v_ref):
        y1, y2, _ = _head_norm(x_ref[:, :LANES], p_ref[...], g_ref[...])
        k_ref[:, :LANES] = y1.astype(k_ref.dtype)
        k_ref[:, LANES:] = _rope(y2, c_ref[...], a_ref[...], b_ref[...]).astype(k_ref.dtype)
        v_ref[...] = x_ref[:, LANES:].astype(v_ref.dtype)

    tab = pl.BlockSpec((ts, LANES), lambda i, h: (i, 0))
    return pl.pallas_call(
        body, name=name, out_shape=(_act((H, S, QK_PAD)), _act((H, S, LANES))), grid=(S // ts, H),
        in_specs=[pl.BlockSpec((ts, QK_PAD), lambda i, h: (i, h)), pl.BlockSpec((ts, LANES), lambda i, h: (i, pe_blk)),
                  pl.BlockSpec((1, QK_PAD), lambda i, h: (0, 0)), tab, tab, tab],
        out_specs=(pl.BlockSpec((None, ts, QK_PAD), lambda i, h: (h, i, 0)), pl.BlockSpec((None, ts, LANES), lambda i, h: (h, i, 0))),
        compiler_params=_params("parallel", "parallel"))(kvraw, lat, g, *tabs)


def mla_k_prep_bwd(dk, dv, kvraw, lat, g, tabs, H, pe_blk, name):
    S = kvraw.shape[0]
    ts = _tile(S, 256, 8)

    def body(dk_ref, dv_ref, x_ref, p_ref, g_ref, c_ref, a_ref, b_ref, o_ref, dp_ref, dg_ref):
        i, h = pl.program_id(0), pl.program_id(1)
        d2 = _rope_t(dk_ref[:, LANES:], c_ref[...], a_ref[...], b_ref[...])
        dx1, dx2, g1, g2 = _head_norm_bwd(x_ref[:, :LANES], p_ref[...], g_ref[...], dk_ref[:, :LANES], d2)
        o_ref[:, :LANES] = dx1.astype(o_ref.dtype)
        o_ref[:, LANES:] = dv_ref[...].astype(o_ref.dtype)
        _accumulate(dp_ref, dx2, h == 0)
        part = jnp.concatenate([jnp.sum(g1, axis=0, keepdims=True), jnp.sum(g2, axis=0, keepdims=True)], axis=1)
        _accumulate(dg_ref, part, jnp.logical_and(i == 0, h == 0))

    tab = pl.BlockSpec((ts, LANES), lambda i, h: (i, 0))
    vec = pl.BlockSpec((1, QK_PAD), lambda i, h: (0, 0))
    return pl.pallas_call(
        body, name=name, out_shape=(_act((S, H * QK_PAD)), _f32((S, LANES)), _f32((1, QK_PAD))), grid=(S // ts, H),
        in_specs=[pl.BlockSpec((None, ts, QK_PAD), lambda i, h: (h, i, 0)), pl.BlockSpec((None, ts, LANES), lambda i, h: (h, i, 0)),
                  pl.BlockSpec((ts, QK_PAD), lambda i, h: (i, h)), pl.BlockSpec((ts, LANES), lambda i, h: (i, pe_blk)),
                  vec, tab, tab, tab],
        out_specs=(pl.BlockSpec((ts, QK_PAD), lambda i, h: (i, h)), tab, vec),
        compiler_params=_params("arbitrary", "arbitrary"))(dk, dv, kvraw, lat, g, *tabs)


def _causal_scores(q, k, scale, diagonal):
    s = lax.dot_general(q, k, NT, preferred_element_type=jnp.float32) * scale
    if not diagonal:
        return s
    row = lax.broadcasted_iota(jnp.int32, s.shape, 0)
    col = lax.broadcasted_iota(jnp.int32, s.shape, 1)
    return jnp.where(col <= row, s, NEG)


def _on_causal_blocks(qi, ki, step):
    @pl.when(ki < qi)
    def _():
        step(False)

    @pl.when(ki == qi)
    def _():
        step(True)


def mla_attention_fwd(q, k, v, name):
    H, S, _ = q.shape
    t = _tile(S, 512)
    n = S // t
    scale = 1.0 / math.sqrt(QK_DIM)

    def body(q_ref, k_ref, v_ref, o_ref, lse_ref, m_sc, l_sc, acc):
        qi, ki = pl.program_id(1), pl.program_id(2)

        @pl.when(ki == 0)
        def _():
            m_sc[...] = jnp.full(m_sc.shape, NEG, jnp.float32)
            l_sc[...] = jnp.zeros_like(l_sc)
            acc[...] = jnp.zeros_like(acc)

        def step(diagonal):
            s = _causal_scores(q_ref[...], k_ref[...], scale, diagonal)
            m_new = jnp.maximum(m_sc[...], jnp.max(s, axis=-1, keepdims=True))
            alpha = jnp.exp(m_sc[...] - m_new)
            p = jnp.exp(s - m_new)
            l_sc[...] = alpha * l_sc[...] + jnp.sum(p, axis=-1, keepdims=True)
            acc[...] = alpha * acc[...] + jnp.dot(p.astype(MXU_DTYPE), v_ref[...], preferred_element_type=jnp.float32)
            m_sc[...] = m_new

        _on_causal_blocks(qi, ki, step)

        @pl.when(ki == qi)
        def _():
            o_ref[...] = (acc[...] / l_sc[...]).astype(o_ref.dtype)
            lse_ref[...] = m_sc[...] + jnp.log(l_sc[...])

    kv = lambda w: pl.BlockSpec((None, t, w), lambda h, qi, ki: (h, jnp.minimum(ki, qi), 0))
    return pl.pallas_call(
        body, name=name, out_shape=(_act((S, H * LANES)), _f32((H, S, 1))), grid=(H, n, n),
        in_specs=[pl.BlockSpec((None, t, QK_PAD), lambda h, qi, ki: (h, qi, 0)), kv(QK_PAD), kv(LANES)],
        out_specs=(pl.BlockSpec((t, LANES), lambda h, qi, ki: (qi, h)), pl.BlockSpec((None, t, 1), lambda h, qi, ki: (h, qi, 0))),
        scratch_shapes=[pltpu.VMEM((t, 1), jnp.float32), pltpu.VMEM((t, 1), jnp.float32), pltpu.VMEM((t, LANES), jnp.float32)],
        compiler_params=_params("parallel", "parallel", "arbitrary"))(q, k, v)


def attention_delta(do, o, H, name):
    S = do.shape[0]
    ts = _tile(S, 512, 8)

    def body(d_ref, o_ref, out_ref):
        out_ref[...] = jnp.sum(d_ref[...] * o_ref[...].astype(jnp.float32), axis=-1, keepdims=True)

    blk = pl.BlockSpec((ts, LANES), lambda i, h: (i, h))
    return pl.pallas_call(
        body, name=name, out_shape=_f32((H, S, 1)), grid=(S // ts, H), in_specs=[blk, blk],
        out_specs=pl.BlockSpec((None, ts, 1), lambda i, h: (h, i, 0)), compiler_params=_params("parallel", "parallel"))(do, o)


def mla_attention_bwd_dq(q, k, v, do, lse, delta, name):
    H, S, _ = q.shape
    t = _tile(S, 512)
    n = S // t
    scale = 1.0 / math.sqrt(QK_DIM)

    def body(q_ref, k_ref, v_ref, do_ref, lse_ref, dl_ref, dq_ref, acc):
        qi, ki = pl.program_id(1), pl.program_id(2)

        @pl.when(ki == 0)
        def _():
            acc[...] = jnp.zeros_like(acc)

        def step(diagonal):
            p = jnp.exp(_causal_scores(q_ref[...], k_ref[...], scale, diagonal) - lse_ref[...])
            dp = lax.dot_general(do_ref[...].astype(MXU_DTYPE), v_ref[...], NT, preferred_element_type=jnp.float32)
            ds = p * (dp - dl_ref[...])
            acc[...] += jnp.dot(ds.astype(MXU_DTYPE), k_ref[...], preferred_element_type=jnp.float32)

        _on_causal_blocks(qi, ki, step)

        @pl.when(ki == qi)
        def _():
            dq_ref[...] = acc[...] * scale

    kv = lambda w: pl.BlockSpec((None, t, w), lambda h, qi, ki: (h, jnp.minimum(ki, qi), 0))
    col = pl.BlockSpec((None, t, 1), lambda h, qi, ki: (h, qi, 0))
    qspec = pl.BlockSpec((None, t, QK_PAD), lambda h, qi, ki: (h, qi, 0))
    return pl.pallas_call(
        body, name=name, out_shape=_f32((H, S, QK_PAD)), grid=(H, n, n),
        in_specs=[qspec, kv(QK_PAD), kv(LANES), pl.BlockSpec((t, LANES), lambda h, qi, ki: (qi, h)), col, col],
        out_specs=qspec, scratch_shapes=[pltpu.VMEM((t, QK_PAD), jnp.float32)],
        compiler_params=_params("parallel", "parallel", "arbitrary"))(q, k, v, do, lse, delta)


def mla_attention_bwd_dkv(q, k, v, do, lse, delta, name):
    H, S, _ = q.shape
    t = _tile(S, 512)
    n = S // t
    scale = 1.0 / math.sqrt(QK_DIM)

    def body(q_ref, k_ref, v_ref, do_ref, lse_ref, dl_ref, dk_ref, dv_ref, dk_acc, dv_acc):
        ki, qi = pl.program_id(1), pl.program_id(2)

        @pl.when(qi == 0)
        def _():
            dk_acc[...] = jnp.zeros_like(dk_acc)
            dv_acc[...] = jnp.zeros_like(dv_acc)

        def step(diagonal):
            p = jnp.exp(_causal_scores(q_ref[...], k_ref[...], scale, diagonal) - lse_ref[...])
            dob = do_ref[...].astype(MXU_DTYPE)
            dv_acc[...] += lax.dot_general(p.astype(MXU_DTYPE), dob, TN, preferred_element_type=jnp.float32)
            dp = lax.dot_general(dob, v_ref[...], NT, preferred_element_type=jnp.float32)
            ds = p * (dp - dl_ref[...])
            dk_acc[...] += lax.dot_general(ds.astype(MXU_DTYPE), q_ref[...], TN, preferred_element_type=jnp.float32)

        _on_causal_blocks(qi, ki, step)

        @pl.when(qi == n - 1)
        def _():
            dk_ref[...] = dk_acc[...] * scale
            dv_ref[...] = dv_acc[...]

    qrow = lambda h, ki, qi: (h, jnp.maximum(qi, ki), 0)
    kv = lambda w: pl.BlockSpec((None, t, w), lambda h, ki, qi: (h, ki, 0))
    col = pl.BlockSpec((None, t, 1), qrow)
    return pl.pallas_call(
        body, name=name, out_shape=(_f32((H, S, QK_PAD)), _f32((H, S, LANES))), grid=(H, n, n),
        in_specs=[pl.BlockSpec((None, t, QK_PAD), qrow), kv(QK_PAD), kv(LANES),
                  pl.BlockSpec((t, LANES), lambda h, ki, qi: (jnp.maximum(qi, ki), h)), col, col],
        out_specs=(kv(QK_PAD), kv(LANES)),
        scratch_shapes=[pltpu.VMEM((t, QK_PAD), jnp.float32), pltpu.VMEM((t, LANES), jnp.float32)],
        compiler_params=_params("parallel", "parallel", "arbitrary"))(q, k, v, do, lse, delta)


def _alibi_slopes(G, Hd):
    k = np.arange(1, G * Hd + 1, dtype=np.float32)
    s = (2.0 ** (-8.0 * k / (G * Hd))).astype(np.float32).reshape(G, Hd)
    return jnp.asarray(np.broadcast_to(s[:, :, None, None], (G, Hd, 1, LANES)).copy())


def _dil_scores(qn, kn, scale, slope_d, prev, valid):
    s = lax.dot_general(qn, kn, NT, preferred_element_type=jnp.float32) * scale
    iq = lax.broadcasted_iota(jnp.int32, s.shape, 0)
    ik = lax.broadcasted_iota(jnp.int32, s.shape, 1)
    dist = iq - ik + (BLK if prev else 0)
    ok = (ik >= iq) if prev else (ik <= iq)
    s = s - slope_d * dist.astype(jnp.float32)
    return jnp.where(jnp.logical_and(ok, valid), s, NEG)


def _dil_specs(d, nb, Hd, G, g, ncol):
    def spec(kind, shift):
        col0 = (kind * G + g) * Hd

        def imap(r, n, h):
            return (jnp.clip(n + shift, 0, nb - 1), r * ncol + col0 + h)
        return pl.BlockSpec((BLK, LANES), imap)
    return spec


def dilated_fwd(qkv, gq, gk, slopes, g, d, Hd, G, name):
    S, C = qkv.shape
    ncol = C // LANES
    nb = S // d // BLK
    view = qkv.reshape(S // d, d * C)
    scale = 1.0 / math.sqrt(LANES)
    spec = _dil_specs(d, nb, Hd, G, g, ncol)

    def body(q_ref, kc_ref, kp_ref, vc_ref, vp_ref, gq_ref, gk_ref, sl_ref, o_ref, l_ref):
        n = pl.program_id(1)
        nrm = lambda t, gg: (t * _rstd(t, LANES) * gg).astype(MXU_DTYPE)
        qn = nrm(q_ref[...], gq_ref[...])
        slope_d = sl_ref[:, :1] * float(d)
        sc = _dil_scores(qn, nrm(kc_ref[...], gk_ref[...]), scale, slope_d, False, True)
        sp = _dil_scores(qn, nrm(kp_ref[...], gk_ref[...]), scale, slope_d, True, n > 0)
        m = jnp.maximum(jnp.max(sc, axis=-1, keepdims=True), jnp.max(sp, axis=-1, keepdims=True))
        lse = m + jnp.log(jnp.sum(jnp.exp(sc - m), axis=-1, keepdims=True) + jnp.sum(jnp.exp(sp - m), axis=-1, keepdims=True))
        o = jnp.dot(jnp.exp(sc - lse).astype(MXU_DTYPE), vc_ref[...].astype(MXU_DTYPE), preferred_element_type=jnp.float32)
        o = o + jnp.dot(jnp.exp(sp - lse).astype(MXU_DTYPE), vp_ref[...].astype(MXU_DTYPE), preferred_element_type=jnp.float32)
        o_ref[...] = o
        l_ref[...] = jnp.broadcast_to(lse, l_ref.shape)

    vec = pl.BlockSpec((1, LANES), lambda r, n, h: (0, 0))
    out = pl.BlockSpec((BLK, LANES), lambda r, n, h: (n, r * Hd + h))
    o, l = pl.pallas_call(
        body, name=name, out_shape=(_f32((S // d, d * Hd * LANES)), _f32((S // d, d * Hd * LANES))), grid=(d, nb, Hd),
        in_specs=[spec(0, 0), spec(1, 0), spec(1, -1), spec(2, 0), spec(2, -1), vec, vec,
                  pl.BlockSpec((None, None, 1, LANES), lambda r, n, h: (g, h, 0, 0))],
        out_specs=(out, out), compiler_params=_params("parallel", "parallel", "parallel"),
    )(view, view, view, view, view, gq, gk, slopes)
    return o.reshape(S, Hd * LANES), l.reshape(S, Hd * LANES)


def dilated_merge(os_, ls_, name):
    S, W = os_[0].shape
    G = len(os_)
    ts, tw = _tile(S, 512, 8), _tile(W, 512)

    def body(*refs):
        o_refs, l_refs, (o_ref, t_ref) = refs[:G], refs[G:2 * G], refs[2 * G:]
        ls = [r[...] for r in l_refs]
        m = ls[0]
        for l in ls[1:]:
            m = jnp.maximum(m, l)
        es = [jnp.exp(l - m) for l in ls]
        tot = es[0]
        for e in es[1:]:
            tot = tot + e
        acc = o_refs[0][...] * (es[0] / tot)
        for r, e in zip(o_refs[1:], es[1:]):
            acc = acc + r[...] * (e / tot)
        o_ref[...] = acc.astype(o_ref.dtype)
        t_ref[...] = m + jnp.log(tot)

    blk = pl.BlockSpec((ts, tw), lambda i, j: (i, j))
    return pl.pallas_call(
        body, name=name, out_shape=(_act((S, W)), _f32((S, W))), grid=(S // ts, W // tw),
        in_specs=[blk] * (2 * G), out_specs=(blk, blk), compiler_params=_params("parallel", "parallel"))(*os_, *ls_)


def dilated_delta(do, o, name):
    S, W = do.shape
    ts = _tile(S, 512, 8)

    def body(d_ref, o_ref, out_ref):
        out_ref[...] = jnp.broadcast_to(jnp.sum(d_ref[...] * o_ref[...].astype(jnp.float32), axis=-1, keepdims=True), out_ref.shape)

    blk = pl.BlockSpec((ts, LANES), lambda i, h: (i, h))
    return pl.pallas_call(body, name=name, out_shape=_f32((S, W)), grid=(S // ts, W // LANES), in_specs=[blk, blk],
                          out_specs=blk, compiler_params=_params("parallel", "parallel"))(do, o)


def dilated_bwd(qkv, do, lse, delta, gq, gk, slopes, g, d, Hd, G, name):
    S, C = qkv.shape
    ncol = C // LANES
    nb = S // d // BLK
    W = Hd * LANES
    view = qkv.reshape(S // d, d * C)
    hview = lambda t: t.reshape(S // d, d * W)
    scale = 1.0 / math.sqrt(LANES)
    spec = _dil_specs(d, nb, Hd, G, g, ncol)

    def hspec(shift):
        return pl.BlockSpec((BLK, LANES), lambda r, n, h: (jnp.clip(n + shift, 0, nb - 1), r * Hd + h))

    def body(q_ref, qx_ref, kc_ref, kp_ref, vc_ref, vp_ref, do_ref, dox_ref, l_ref, lx_ref, dl_ref, dlx_ref,
             gq_ref, gk_ref, sl_ref, dq_ref, dk_ref, dv_ref, dgq_ref, dgk_ref):
        r, n, h = pl.program_id(0), pl.program_id(1), pl.program_id(2)
        first = jnp.logical_and(jnp.logical_and(r == 0, n == 0), h == 0)
        gqv, gkv = gq_ref[...], gk_ref[...]
        nrm = lambda t, gg: (t * _rstd(t, LANES) * gg).astype(MXU_DTYPE)
        f32dot = lambda a, b, dn: lax.dot_general(a, b, dn, preferred_element_type=jnp.float32)
        qn, qxn = nrm(q_ref[...], gqv), nrm(qx_ref[...], gqv)
        kcn, kpn = nrm(kc_ref[...], gkv), nrm(kp_ref[...], gkv)
        vc, vp = vc_ref[...].astype(MXU_DTYPE), vp_ref[...].astype(MXU_DTYPE)
        dob, doxb = do_ref[...].astype(MXU_DTYPE), dox_ref[...].astype(MXU_DTYPE)
        slope_d = sl_ref[:, :1] * float(d)
        lrow, lxrow = l_ref[:, :1], lx_ref[:, :1]
        drow, dxrow = dl_ref[:, :1], dlx_ref[:, :1]
        pc = jnp.exp(_dil_scores(qn, kcn, scale, slope_d, False, True) - lrow)
        pp = jnp.exp(_dil_scores(qn, kpn, scale, slope_d, True, n > 0) - lrow)
        dsc = pc * (f32dot(dob, vc, NT) - drow)
        dsp = pp * (f32dot(dob, vp, NT) - drow)
        dqn = (jnp.dot(dsc.astype(MXU_DTYPE), kcn, preferred_element_type=jnp.float32)
               + jnp.dot(dsp.astype(MXU_DTYPE), kpn, preferred_element_type=jnp.float32)) * scale
        dq, dgq = _norm_bwd(q_ref[...], gqv, dqn, LANES)
        dq_ref[...] = dq.astype(dq_ref.dtype)
        px = jnp.exp(_dil_scores(qxn, kcn, scale, slope_d, True, n < nb - 1) - lxrow)
        dsx = px * (f32dot(doxb, vc, NT) - dxrow)
        dkn = (f32dot(dsc.astype(MXU_DTYPE), qn, TN) + f32dot(dsx.astype(MXU_DTYPE), qxn, TN)) * scale
        dk, dgk = _norm_bwd(kc_ref[...], gkv, dkn, LANES)
        dk_ref[...] = dk.astype(dk_ref.dtype)
        dv_ref[...] = (f32dot(pc.astype(MXU_DTYPE), dob, TN) + f32dot(px.astype(MXU_DTYPE), doxb, TN)).astype(dv_ref.dtype)
        _accumulate(dgq_ref, jnp.sum(dgq, axis=0, keepdims=True), first)
        _accumulate(dgk_ref, jnp.sum(dgk, axis=0, keepdims=True), first)

    vec = pl.BlockSpec((1, LANES), lambda r, n, h: (0, 0))
    out = hspec(0)
    dq, dk, dv, dgq, dgk = pl.pallas_call(
        body, name=name,
        out_shape=(_act((S // d, d * W)), _act((S // d, d * W)), _act((S // d, d * W)), _f32((1, LANES)), _f32((1, LANES))),
        grid=(d, nb, Hd),
        in_specs=[spec(0, 0), spec(0, 1), spec(1, 0), spec(1, -1), spec(2, 0), spec(2, -1), hspec(0), hspec(1), hspec(0), hspec(1),
                  hspec(0), hspec(1), vec, vec, pl.BlockSpec((None, None, 1, LANES), lambda r, n, h: (g, h, 0, 0))],
        out_specs=(out, out, out, vec, vec), compiler_params=_params("arbitrary", "arbitrary", "arbitrary"),
    )(view, view, view, view, view, view, hview(do), hview(do), hview(lse), hview(lse), hview(delta), hview(delta), gq, gk, slopes)
    return dq.reshape(S, W), dk.reshape(S, W), dv.reshape(S, W), dgq, dgk


def adamw(w, g, m, v, layer, prev, name):
    L, r, c = w.shape
    tr, tc = _tile(r, 256, 8), _tile(c, 1024)
    c1 = 1.0 / (1.0 - ADAM_B1 ** ADAM_STEP)
    c2 = 1.0 / (1.0 - ADAM_B2 ** ADAM_STEP)

    def body(*refs):
        w_ref, g_ref, m_ref, v_ref = refs[:4]
        go_ref, d_ref, mo_ref, vo_ref = refs[-4:]
        gv = g_ref[...]
        mn = ADAM_B1 * m_ref[...] + (1.0 - ADAM_B1) * gv
        vn = ADAM_B2 * v_ref[...] + (1.0 - ADAM_B2) * (gv * gv)
        go_ref[...] = gv
        d_ref[...] = -ADAM_LR * ((mn * c1) / (jnp.sqrt(vn * c2) + ADAM_EPS) + ADAM_WD * w_ref[...])
        mo_ref[...] = mn
        vo_ref[...] = vn

    lay = pl.BlockSpec((None, tr, tc), lambda i, j: (layer, i, j))
    flat = pl.BlockSpec((tr, tc), lambda i, j: (i, j))
    ins = [w, g, m, v] + (list(prev) if prev is not None else [])
    in_specs = [lay, flat, lay, lay] + ([ANY] * 4 if prev is not None else [])
    return pl.pallas_call(
        body, name=name, out_shape=tuple(_f32((L, r, c)) for _ in range(4)), grid=(r // tr, c // tc),
        in_specs=in_specs, out_specs=(lay, lay, lay, lay),
        input_output_aliases=({4 + k: k for k in range(4)} if prev is not None else {}),
        compiler_params=_params("parallel", "parallel"))(*ins)


def _ffn_fwd(h, g, W, kind, tag):
    xn = rmsnorm_fwd(h, g, tag + "_norm")
    u = matmul(xn, W(kind + "_w_in", h), name=tag + "_in", tn=1408)
    a = swiglu_fwd(u, tag + "_act")
    out = matmul(a, W(kind + "_w_out", u), scale=0.5, res=h, name=tag + "_out", tk=1408)
    return out, (h, xn, u, a)


def _ffn_bwd(dout, saved, g, W, emit, kind, tag):
    h, xn, u, a = saved
    emit(kind + "_w_out", matmul(a, dout, ta=True, scale=0.5, out_dtype=WIRE_DTYPE, out_axis=0, name=tag + "_dwout", tm=1408))
    da = matmul(dout, W(kind + "_w_out", None), tb=True, scale=0.5, name=tag + "_da", tn=1408)
    du = swiglu_bwd(u, da, tag + "_dact")
    emit(kind + "_w_in", matmul(xn, du, ta=True, out_dtype=WIRE_DTYPE, out_axis=1, name=tag + "_dwin", tn=1408))
    dxn = matmul(du, W(kind + "_w_in", None), tb=True, name=tag + "_dxn", tk=1408)
    return rmsnorm_bwd(h, g, dxn, dout, tag + "_dnorm")


def _pad_gain(g):
    return jnp.pad(g, ((0, 0), (0, QK_PAD - QK_DIM)))


def _mla_fwd(h, P, W, tabs, H):
    g_mix, g_cq, g_ckv = P["mix_norm"][0:1], P["mla_g_cq"], P["mla_g_ckv"]
    pe_blk = (g_cq.shape[1] + g_ckv.shape[1]) // LANES
    xn = rmsnorm_fwd(h, g_mix, "mla_norm")
    w_down = W("mla_w_down", h)
    lat = matmul(xn, w_down, name="mla_down", tn=w_down.shape[1])
    cq, ckv = mla_latent_fwd(lat, g_cq, g_ckv, "mla_latent")
    qraw = matmul(cq, W("mla_w_uq", lat), name="mla_uq")
    kvraw = matmul(ckv, W("mla_w_ukv", qraw), name="mla_ukv")
    q = mla_q_prep_fwd(qraw, _pad_gain(P["mla_g_qn"]), tabs, H, "mla_qprep")
    k, v = mla_k_prep_fwd(kvraw, lat, _pad_gain(P["mla_g_kn"]), tabs, H, pe_blk, "mla_kprep")
    o, lse = mla_attention_fwd(q, k, v, "mla_attn")
    out = matmul(o, W("mla_w_o", lse), res=h, name="mla_o")
    return out, (h, xn, lat, cq, ckv, qraw, kvraw, q, k, v, o, lse, pe_blk)


def _mla_bwd(dout, saved, P, W, emit, tabs, H):
    h, xn, lat, cq, ckv, qraw, kvraw, q, k, v, o, lse, pe_blk = saved
    emit("mla_w_o", matmul(o, dout, ta=True, out_dtype=WIRE_DTYPE, out_axis=0, name="mla_dwo", tm=512))
    do = matmul(dout, W("mla_w_o", None), tb=True, name="mla_do", tn=512)
    delta = attention_delta(do, o, H, "mla_delta")
    dq = mla_attention_bwd_dq(q, k, v, do, lse, delta, "mla_attn_dq")
    dk, dv = mla_attention_bwd_dkv(q, k, v, do, lse, delta, "mla_attn_dkv")
    dqraw, dgq = mla_q_prep_bwd(dq, qraw, _pad_gain(P["mla_g_qn"]), tabs, H, "mla_dqprep")
    dkvraw, dkpe, dgk = mla_k_prep_bwd(dk, dv, kvraw, lat, _pad_gain(P["mla_g_kn"]), tabs, H, pe_blk, "mla_dkprep")
    emit("mla_w_uq", matmul(cq, dqraw, ta=True, out_dtype=WIRE_DTYPE, out_axis=1, name="mla_dwuq"))
    dcq = matmul(dqraw, W("mla_w_uq", None), tb=True, name="mla_dcq", tk=1024)
    emit("mla_w_ukv", matmul(ckv, dkvraw, ta=True, out_dtype=WIRE_DTYPE, out_axis=1, name="mla_dwukv"))
    dckv = matmul(dkvraw, W("mla_w_ukv", None), tb=True, name="mla_dckv", tk=1024)
    dlat, dgcq, dgckv = mla_latent_bwd(dcq, dckv, dkpe, lat, P["mla_g_cq"], P["mla_g_ckv"], "mla_dlatent")
    emit("mla_w_down", matmul(xn, dlat, ta=True, out_dtype=WIRE_DTYPE, out_axis=0, name="mla_dwdown", tm=512, tn=dlat.shape[1]))
    dxn = matmul(dlat, W("mla_w_down", None), tb=True, name="mla_dxn", tn=512, tk=dlat.shape[1])
    dh, dgm = rmsnorm_bwd(h, P["mix_norm"][0:1], dxn, dout, "mla_dnorm")
    return dh, dgm, dict(mla_g_qn=dgq[:, :QK_DIM], mla_g_kn=dgk[:, :QK_DIM], mla_g_cq=dgcq, mla_g_ckv=dgckv)


def _dil_fwd(h, P, W, slopes, Hd):
    G = len(DIL_PAIRS)
    xn = rmsnorm_fwd(h, P["mix_norm"][1:2], "dil_norm")
    qkv = matmul(xn, W("dil_w_qkv", h), name="dil_qkv", tn=1152)
    os_, ls_ = [], []
    for g, (_, d) in enumerate(DIL_PAIRS):
        o_g, l_g = dilated_fwd(qkv, P["dil_g_qn"], P["dil_g_kn"], slopes, g, d, Hd, G, f"dil_attn{g}")
        os_.append(o_g)
        ls_.append(l_g)
    o, lse = dilated_merge(os_, ls_, "dil_merge")
    out = matmul(o, W("dil_w_o", lse), res=h, name="dil_o", tn=512)
    return out, (h, xn, qkv, o, lse)


def _dil_bwd(dout, saved, P, W, emit, slopes, Hd):
    h, xn, qkv, o, lse = saved
    ngrp = len(DIL_PAIRS)
    emit("dil_w_o", matmul(o, dout, ta=True, out_dtype=WIRE_DTYPE, out_axis=1, name="dil_dwo", tn=512))
    do = matmul(dout, W("dil_w_o", None), tb=True, name="dil_do", tk=512)
    delta = dilated_delta(do, o, "dil_delta")
    parts = [dilated_bwd(qkv, do, lse, delta, P["dil_g_qn"], P["dil_g_kn"], slopes, g, d, Hd, ngrp, f"dil_dattn{g}")
             for g, (_, d) in enumerate(DIL_PAIRS)]
    dqkv = jnp.concatenate([p[kind] for kind in range(3) for p in parts], axis=1)
    emit("dil_w_qkv", matmul(xn, dqkv, ta=True, out_dtype=WIRE_DTYPE, out_axis=1, name="dil_dwqkv", tn=1152))
    dxn = matmul(dqkv, W("dil_w_qkv", None), tb=True, name="dil_dxn", tk=1152)
    dh, dgm = rmsnorm_bwd(h, P["mix_norm"][1:2], dxn, dout, "dil_dnorm")
    return dh, dgm, dict(dil_g_qn=parts[0][3] + parts[1][3] + parts[2][3], dil_g_kn=parts[0][4] + parts[1][4] + parts[2][4])


def local_step(x, target, P, get_w, on_grad):
    S, D = x.shape
    H, Hd = MLA_HEADS, DIL_HEADS
    tabs = rope_tables(S)
    slopes = _alibi_slopes(len(DIL_PAIRS), Hd)
    cache = {}

    def weights_of(layer):
        def W(name, after):
            if (name, layer) not in cache:
                cache[name, layer] = Sharded(get_w(name, layer, after), 0, SHARD_AXIS[name])
            return cache[name, layer]
        return W

    row = lambda name, i: P[name][i:i + 1]
    h = x
    saved = []
    for i in range(2):
        W = weights_of(i)
        h, s1 = _ffn_fwd(h, row("ffn1_norm", i), W, "ffn1", f"l{i}_ffn1")
        h, sm = _mla_fwd(h, P, weights_of(0), tabs, H) if i == 0 else _dil_fwd(h, P, weights_of(0), slopes, Hd)
        h, s2 = _ffn_fwd(h, row("ffn2_norm", i), W, "ffn2", f"l{i}_ffn2")
        saved.append((s1, sm, s2))
    loss, dh = loss_head(h, target, "loss")

    gs = {n: [None, None] for n in ("ffn1_norm", "mix_norm", "ffn2_norm")}
    for i in (1, 0):
        s1, sm, s2 = saved[i]
        W = weights_of(i)
        emit = lambda name, g4, layer=i: on_grad(name, layer, g4)
        emit0 = lambda name, g4: on_grad(name, 0, g4)
        dh, gs["ffn2_norm"][i] = _ffn_bwd(dh, s2, row("ffn2_norm", i), W, emit, "ffn2", f"l{i}_ffn2")
        if i == 0:
            dh, gs["mix_norm"][i], gm = _mla_bwd(dh, sm, P, weights_of(0), emit0, tabs, H)
        else:
            dh, gs["mix_norm"][i], gm = _dil_bwd(dh, sm, P, weights_of(0), emit0, slopes, Hd)
        gs.update({n: [val] for n, val in gm.items()})
        dh, gs["ffn1_norm"][i] = _ffn_bwd(dh, s1, row("ffn1_norm", i), W, emit, "ffn1", f"l{i}_ffn1")
    gsmall = {n: jnp.concatenate(v, axis=0) for n, v in gs.items()}
    return loss, dh, gsmall


def _pad_heads(w, real, padded):
    lead, n = w.shape[:-1], w.shape[-1] // real
    w = jnp.pad(w.reshape(*lead, n, real), [(0, 0)] * (len(lead) + 1) + [(0, padded - real)])
    return w.reshape(*lead, n * padded)


def _unpad_heads(w, real, padded):
    lead, n = w.shape[:-1], w.shape[-1] // padded
    return w.reshape(*lead, n, padded)[..., :real].reshape(*lead, n * real)


def _pack_small(gs):
    flat = jnp.concatenate([gs[n].reshape(-1) for n in SMALL])
    rows = -(-flat.shape[0] // LANES)
    rows = -(-rows // 8) * 8
    return jnp.pad(flat, (0, rows * LANES - flat.shape[0])).reshape(rows, LANES)


def _unpack_small(packed, like):
    flat, out, off = packed.reshape(-1), {}, 0
    for n in SMALL:
        size = int(np.prod(like[n].shape))
        out[n] = flat[off:off + size].reshape(like[n].shape)
        off += size
    return out


def kernel(x, ffn1_norm, ffn1_w_in, ffn1_w_out, mix_norm, ffn2_norm, ffn2_w_in, ffn2_w_out, mla_w_down, mla_g_cq, mla_g_ckv, mla_w_uq, mla_w_ukv, mla_g_qn, mla_g_kn, mla_w_o, dil_w_qkv, dil_g_qn, dil_g_kn, dil_w_o, loss_target, m_ffn1_norm, m_ffn1_w_in, m_ffn1_w_out, m_mix_norm, m_ffn2_norm, m_ffn2_w_in, m_ffn2_w_out, m_mla_w_down, m_mla_g_cq, m_mla_g_ckv, m_mla_w_uq, m_mla_w_ukv, m_mla_g_qn, m_mla_g_kn, m_mla_w_o, m_dil_w_qkv, m_dil_g_qn, m_dil_g_kn, m_dil_w_o, v_ffn1_norm, v_ffn1_w_in, v_ffn1_w_out, v_mix_norm, v_ffn2_norm, v_ffn2_w_in, v_ffn2_w_out, v_mla_w_down, v_mla_g_cq, v_mla_g_ckv, v_mla_w_uq, v_mla_w_ukv, v_mla_g_qn, v_mla_g_kn, v_mla_w_o, v_dil_w_qkv, v_dil_g_qn, v_dil_g_kn, v_dil_w_o):
    args = dict(locals())
    w = {n: args[n] for n in WEIGHTS}
    m = {n: args["m_" + n] for n in WEIGHTS}
    v = {n: args["v_" + n] for n in WEIGHTS}
    cx, cy, cc = _me()
    core = jnp.reshape(cc, (1,)).astype(jnp.int32)
    shard = jnp.reshape(2 * cx + cy, (1,)).astype(jnp.int32)
    pe_pad = LANES - ROPE_DIM

    started = {}
    for l in range(2):
        for n in USE_ORDER[l]:
            _, r, c = w[n].shape
            src = w[n][l if w[n].shape[0] > 1 else 0].astype(WIRE_DTYPE).reshape(1, 2, r // 2, c)
            land = lax.empty((N_CHIPS, 1, 2, r // 2, c), WIRE_DTYPE)
            started[n, l if w[n].shape[0] > 1 else 0] = exchange_start(src, land, _ag_src, _ag_dst, f"ag_{n}{l}_start")
    all_started = sum(s[3] for s in started.values())
    n_started = len(started)

    def get_w(n, l, after):
        behind = all_started if len(started) == n_started else after
        sems, src, land, _ = started.pop((n, l))
        _, _, _, h, c = land.shape
        land = exchange_wait(sems, src, land, _ag_src, _ag_dst, behind, f"ag_{n}{l}_wait")
        full = all_gather_finish(src, land, f"ag_{n}{l}_finish").reshape(N_CHIPS, 1, 2 * h, c)
        if n == "mla_w_down":
            full = jnp.pad(full, ((0, 0), (0, 0), (0, 0), (0, pe_pad)))
        if n == "mla_w_uq":
            full = _pad_heads(full, QK_DIM, QK_PAD)
        return full

    pending = []

    def on_grad(n, l, g4):
        if n == "mla_w_down":
            g4 = g4[..., :g4.shape[-1] - pe_pad]
        if n == "mla_w_uq":
            g4 = _unpad_heads(g4, QK_DIM, QK_PAD)
        after = pending[-1][2][3] if pending else g4
        pending.append((n, l, reduce_scatter_start(g4, core, after, f"rs_{n}{l}")))

    loss, grad_x, gsmall = local_step(x[0], loss_target[0], {n: w[n] for n in SMALL}, get_w, on_grad)
    loss = lax.psum(loss, ("x", "y", "c"))

    outs = {n: None for n in BIG}
    for n, l, pend in pending:
        g = reduce_scatter_finish(pend, shard, grad_x, f"rs_{n}{l}")
        outs[n] = adamw(w[n], g, m[n], v[n], l, outs[n], f"adamw_{n}{l}")
    small = _unpack_small(all_reduce_small(_pack_small(gsmall), "ar_small"), gsmall)
    for n in SMALL:
        outs[n] = tuple(o[0] for o in adamw(w[n][None], small[n], m[n][None], v[n][None], 0, None, f"adamw_{n}"))

    return (loss, grad_x[None], *[outs[n][0] for n in WEIGHTS], *[outs[n][1] for n in WEIGHTS],
            *[outs[n][2] for n in WEIGHTS], *[outs[n][3] for n in WEIGHTS])
```

```python
import math

import numpy as np
import jax
import jax.numpy as jnp
from jax import lax
from jax.experimental import pallas as pl
from jax.experimental.pallas import tpu as pltpu

MXU_DTYPE = jnp.bfloat16
WIRE_DTYPE = jnp.bfloat16
EPS = 1e-6
NEG = -1e30
N_CHIPS = 4
MESH = pl.DeviceIdType.MESH
ANY = pl.BlockSpec(memory_space=pl.ANY)
LANES = 128

MLA_HEADS = 16
NOPE_DIM = 128
ROPE_DIM = 64
QK_DIM = NOPE_DIM + ROPE_DIM
QK_PAD = 2 * LANES
ROPE_THETA = 10000.0
DIL_PAIRS = ((128, 1), (512, 4), (2048, 16))
DIL_HEADS = 8
BLK = 128

ADAM_LR = 0.001
ADAM_B1 = 0.9
ADAM_B2 = 0.999
ADAM_EPS = 1e-08
ADAM_WD = 0.01
ADAM_STEP = 10

NT = (((1,), (1,)), ((), ()))
TN = (((0,), (0,)), ((), ()))

SHARD_AXIS = {"ffn1_w_in": 1, "ffn1_w_out": 0, "ffn2_w_in": 1, "ffn2_w_out": 0, "mla_w_down": 0, "mla_w_uq": 1,
              "mla_w_ukv": 1, "mla_w_o": 0, "dil_w_qkv": 1, "dil_w_o": 1}
BIG = tuple(SHARD_AXIS)
USE_ORDER = (("ffn1_w_in", "ffn1_w_out", "mla_w_down", "mla_w_uq", "mla_w_ukv", "mla_w_o", "ffn2_w_in", "ffn2_w_out"),
             ("ffn1_w_in", "ffn1_w_out", "dil_w_qkv", "dil_w_o", "ffn2_w_in", "ffn2_w_out"))
SMALL = ("ffn1_norm", "mix_norm", "ffn2_norm", "mla_g_cq", "mla_g_ckv", "mla_g_qn", "mla_g_kn", "dil_g_qn", "dil_g_kn")
WEIGHTS = ("ffn1_norm", "ffn1_w_in", "ffn1_w_out", "mix_norm", "ffn2_norm", "ffn2_w_in", "ffn2_w_out", "mla_w_down",
           "mla_g_cq", "mla_g_ckv", "mla_w_uq", "mla_w_ukv", "mla_g_qn", "mla_g_kn", "mla_w_o", "dil_w_qkv", "dil_g_qn",
           "dil_g_kn", "dil_w_o")


def _tile(dim, pref, mult=LANES):
    if dim <= pref:
        return dim
    t = (pref // mult) * mult
    while t >= mult:
        if dim % t == 0:
            return t
        t -= mult
    return dim


def _params(*sem):
    return pltpu.CompilerParams(dimension_semantics=sem)


def _f32(shape):
    return jax.ShapeDtypeStruct(shape, jnp.float32)


def _act(shape):
    return jax.ShapeDtypeStruct(shape, MXU_DTYPE)


class Sharded:
    def __init__(self, arr, layer, axis):
        self.arr, self.layer, self.axis = arr, layer, axis
        n, _, r, c = arr.shape
        self.shape = (n * r, c) if axis == 0 else (r, n * c)
        self.per = r if axis == 0 else c

    def spec(self, tr, tc, rc_of):
        l = self.layer
        if self.axis == 0:
            n = self.per // tr

            def imap(*g):
                bi, bj = rc_of(*g)
                return (bi // n, l, bi % n, bj)
        else:
            n = self.per // tc

            def imap(*g):
                bi, bj = rc_of(*g)
                return (bj // n, l, bi, bj % n)
        return pl.BlockSpec((None, None, tr, tc), imap)


def _spec2(tr, tc, rc_of):
    return pl.BlockSpec((tr, tc), lambda *g: rc_of(*g))


def matmul(a, b, *, ta=False, tb=False, out_dtype=jnp.float32, scale=None, res=None, out_axis=None,
           name, tm=1024, tn=1024, tk=512):
    am, ak = (a.shape[1], a.shape[0]) if ta else a.shape
    bk, bn = (b.shape[1], b.shape[0]) if tb else b.shape
    assert ak == bk, (name, a.shape, b.shape, ta, tb)
    M, N, K = am, bn, ak

    def per(x, axis):
        return x.per if isinstance(x, Sharded) and x.axis == axis else None

    def pick(dim, pref, *pers):
        return _tile(math.gcd(dim, *[p for p in pers if p is not None]), pref)

    tm = pick(M, tm, per(a, 1 if ta else 0), M // N_CHIPS if out_axis == 0 else None)
    tn = pick(N, tn, per(b, 0 if tb else 1), N // N_CHIPS if out_axis == 1 else None)
    tk = pick(K, tk, per(a, 0 if ta else 1), per(b, 1 if tb else 0))
    assert M % tm == 0 and N % tn == 0 and K % tk == 0, (name, M, N, K, tm, tn, tk)
    nk = K // tk

    a_rc = (lambda i, j, k: (k, i)) if ta else (lambda i, j, k: (i, k))
    b_rc = (lambda i, j, k: (j, k)) if tb else (lambda i, j, k: (k, j))
    a_blk = (tk, tm) if ta else (tm, tk)
    b_blk = (tn, tk) if tb else (tk, tn)
    a_spec = a.spec(*a_blk, a_rc) if isinstance(a, Sharded) else _spec2(*a_blk, a_rc)
    b_spec = b.spec(*b_blk, b_rc) if isinstance(b, Sharded) else _spec2(*b_blk, b_rc)
    dn = (((0 if ta else 1,), (1 if tb else 0,)), ((), ()))
    has_res = res is not None

    def body(*refs):
        if has_res:
            a_ref, b_ref, r_ref, o_ref, acc = refs
        else:
            a_ref, b_ref, o_ref, acc = refs
        k = pl.program_id(2)

        @pl.when(k == 0)
        def _():
            acc[...] = jnp.zeros_like(acc)

        acc[...] += lax.dot_general(a_ref[...].astype(MXU_DTYPE), b_ref[...].astype(MXU_DTYPE), dn,
                                    preferred_element_type=jnp.float32)

        @pl.when(k == nk - 1)
        def _():
            r = acc[...]
            if scale is not None:
                r = r * scale
            if has_res:
                r = r + r_ref[...]
            o_ref[...] = r.astype(o_ref.dtype)

    in_specs = [a_spec, b_spec]
    args = [a.arr if isinstance(a, Sharded) else a, b.arr if isinstance(b, Sharded) else b]
    if has_res:
        in_specs.append(_spec2(tm, tn, lambda i, j, k: (i, j)))
        args.append(res)
    o_rc = lambda i, j, k: (i, j)
    if out_axis is None:
        out_shape = jax.ShapeDtypeStruct((M, N), out_dtype)
        out_spec = _spec2(tm, tn, o_rc)
    else:
        shp = (N_CHIPS, 1, M // N_CHIPS, N) if out_axis == 0 else (N_CHIPS, 1, M, N // N_CHIPS)
        out_shape = jax.ShapeDtypeStruct(shp, out_dtype)
        out_spec = Sharded(out_shape, 0, out_axis).spec(tm, tn, o_rc)
    return pl.pallas_call(
        body, name=name, out_shape=out_shape, grid=(M // tm, N // tn, nk),
        in_specs=in_specs, out_specs=out_spec,
        scratch_shapes=[pltpu.VMEM((tm, tn), jnp.float32)],
        compiler_params=_params("parallel", "parallel", "arbitrary"),
    )(*args)


def _me():
    return lax.axis_index("x"), lax.axis_index("y"), lax.axis_index("c")


def _other_chips(x, y):
    return [(1 - x, y), (x, 1 - y), (1 - x, 1 - y)]


HBM = pl.BlockSpec(memory_space=pltpu.HBM)
SEM = pl.BlockSpec(memory_space=pltpu.SEMAPHORE)
N_PEERS = 3
TOKEN = jax.ShapeDtypeStruct((8, LANES), jnp.float32)


def _split_params():
    return pltpu.CompilerParams(has_side_effects=pltpu.SideEffectType.DATAFLOW_SIDE_EFFECTING)


def _in_hbm(a):
    return pltpu.with_memory_space_constraint(a, pltpu.HBM)


def exchange_start(bufs, src_of, dst_of, name):
    nb = len(bufs)

    def body(*refs):
        src_ref, land_ref = refs[0], refs[nb - 1]
        sems, token = refs[nb:nb + 2 * N_PEERS], refs[-1]
        x, y, cc = _me()
        for j, (px, py) in enumerate(_other_chips(x, y)):
            pltpu.make_async_remote_copy(
                src_ref=src_of(src_ref, j, (px, py), (x, y, cc)), dst_ref=dst_of(land_ref, j, (px, py), (x, y, cc)),
                send_sem=sems[j], recv_sem=sems[N_PEERS + j], device_id=(px, py, cc), device_id_type=MESH).start()
        token[...] = jnp.zeros_like(token)

    outs = pl.pallas_call(
        body, name=name,
        out_shape=(pltpu.SemaphoreType.DMA(()),) * (2 * N_PEERS) + tuple(pltpu.HBM(b.shape, b.dtype) for b in bufs) + (TOKEN,),
        in_specs=(HBM,) * nb, out_specs=(SEM,) * (2 * N_PEERS) + (HBM,) * nb + (pl.BlockSpec(memory_space=pltpu.VMEM),),
        input_output_aliases={k: 2 * N_PEERS + k for k in range(nb)}, compiler_params=_split_params(),
    )(*[_in_hbm(b) for b in bufs])
    return outs[:2 * N_PEERS], outs[2 * N_PEERS:2 * N_PEERS + nb], outs[-1]


def exchange_wait(sems, bufs, src_of, got_of, after, name):
    nb = len(bufs)

    def body(*refs):
        src_ref, land_ref = refs[0], refs[nb - 1]
        sems_ = refs[nb:nb + 2 * N_PEERS]
        x, y, cc = _me()
        for j, (px, py) in enumerate(_other_chips(x, y)):
            cp = pltpu.make_async_remote_copy(
                src_ref=src_of(src_ref, j, (px, py), (x, y, cc)), dst_ref=got_of(land_ref, j, (px, py), (x, y, cc)),
                send_sem=sems_[j], recv_sem=sems_[N_PEERS + j], device_id=(px, py, cc), device_id_type=MESH)
            cp.wait_send()
            cp.wait_recv()

    return pl.pallas_call(
        body, name=name, out_shape=tuple(pltpu.HBM(b.shape, b.dtype) for b in bufs),
        in_specs=(HBM,) * nb + (SEM,) * (2 * N_PEERS) + (ANY,), out_specs=(HBM,) * nb,
        input_output_aliases={k: k for k in range(nb)}, compiler_params=_split_params(),
    )(*bufs, *sems, after)


def cast_into_shards(w, layer, shard, name):
    L, r, c = w.shape
    tr, tc = _tile(r, 512, 16), _tile(c, 1024)

    def body(shard_ref, w_ref, o_ref):
        o_ref[...] = w_ref[...].astype(o_ref.dtype)

    grid_spec = pltpu.PrefetchScalarGridSpec(
        num_scalar_prefetch=1, grid=(r // tr, c // tc),
        in_specs=[pl.BlockSpec((None, tr, tc), lambda i, j, sh: (layer, i, j))],
        out_specs=pl.BlockSpec((None, None, tr, tc), lambda i, j, sh: (sh[0], 0, i, j)))
    return pl.pallas_call(body, name=name, grid_spec=grid_spec, out_shape=jax.ShapeDtypeStruct((N_CHIPS, 1, r, c), WIRE_DTYPE),
                          compiler_params=_params("parallel", "parallel"))(shard, w)


def _ag_mine(ref, j, chip, me):
    return ref.at[2 * me[0] + me[1], :, me[2]]


def _ag_got(ref, j, chip, me):
    return ref.at[2 * chip[0] + chip[1], :, me[2]]


def all_gather_finish(land, name):
    def body(land_ref, o_ref, send_sems, recv_sems):
        x, y, cc = _me()
        cps = []
        for j, (px, py) in enumerate(_other_chips(x, y)):
            cp = pltpu.make_async_remote_copy(
                src_ref=o_ref.at[2 * px + py, :, cc], dst_ref=o_ref.at[2 * px + py, :, cc], send_sem=send_sems.at[j],
                recv_sem=recv_sems.at[j], device_id=(x, y, 1 - cc), device_id_type=MESH)
            cp.start()
            cps.append(cp)
        for j, (px, py) in enumerate(_other_chips(x, y)):
            got = o_ref.at[2 * px + py, :, 1 - cc]
            pltpu.make_async_remote_copy(src_ref=got, dst_ref=got, send_sem=send_sems.at[j], recv_sem=recv_sems.at[j],
                                         device_id=(x, y, 1 - cc), device_id_type=MESH).wait_recv()
        for cp in cps:
            cp.wait_send()

    return pl.pallas_call(
        body, name=name, out_shape=jax.ShapeDtypeStruct(land.shape, land.dtype), in_specs=[ANY], out_specs=ANY,
        input_output_aliases={0: 0},
        scratch_shapes=[pltpu.SemaphoreType.DMA((N_PEERS,)), pltpu.SemaphoreType.DMA((N_PEERS,))],
    )(land)


def _rs_src(ref, j, chip, me):
    return ref.at[2 * chip[0] + chip[1]]


def _rs_dst(ref, j, chip, me):
    return ref.at[j]


def sibling_send_halves(g, after, name):
    n, L, two, h, c = g.shape

    def body(g_ref, after_ref, o_ref, send_sem, recv_sem):
        x, y, cc = _me()
        cp = pltpu.make_async_remote_copy(src_ref=g_ref.at[:, :, 1 - cc], dst_ref=o_ref, send_sem=send_sem,
                                          recv_sem=recv_sem, device_id=(x, y, 1 - cc), device_id_type=MESH)
        cp.start()
        cp.wait()

    return pl.pallas_call(
        body, name=name, out_shape=jax.ShapeDtypeStruct((n, L, h, c), g.dtype),
        in_specs=[ANY, ANY], out_specs=ANY,
        scratch_shapes=[pltpu.SemaphoreType.DMA, pltpu.SemaphoreType.DMA],
    )(g, after)


def sibling_gather_halves(r, name):
    def body(r_ref, o_ref, send_sem, recv_sem):
        x, y, cc = _me()
        cp = pltpu.make_async_remote_copy(src_ref=o_ref.at[:, cc], dst_ref=o_ref.at[:, cc], send_sem=send_sem,
                                          recv_sem=recv_sem, device_id=(x, y, 1 - cc), device_id_type=MESH)
        cp.start()
        cp.wait()

    return pl.pallas_call(
        body, name=name, out_shape=jax.ShapeDtypeStruct(r.shape, r.dtype), in_specs=[ANY], out_specs=ANY,
        input_output_aliases={0: 0}, scratch_shapes=[pltpu.SemaphoreType.DMA, pltpu.SemaphoreType.DMA],
    )(r)


def add_sibling(g, r1, core, name):
    n, L, two, h, c = g.shape
    th = _tile(h, 512, 16)
    tc = _tile(c, 1024)

    def body(core_ref, g_ref, r_ref, o_ref):
        o_ref[...] = (g_ref[...].astype(jnp.float32) + r_ref[...].astype(jnp.float32)).astype(o_ref.dtype)

    grid_spec = pltpu.PrefetchScalarGridSpec(
        num_scalar_prefetch=1, grid=(n, L, h // th, c // tc),
        in_specs=[pl.BlockSpec((None, None, None, th, tc), lambda s, l, i, j, core: (s, l, core[0], i, j)),
                  pl.BlockSpec((None, None, th, tc), lambda s, l, i, j, core: (s, l, i, j))],
        out_specs=pl.BlockSpec((None, None, th, tc), lambda s, l, i, j, core: (s, l, i, j)))
    return pl.pallas_call(body, name=name, grid_spec=grid_spec, out_shape=jax.ShapeDtypeStruct((n, L, h, c), WIRE_DTYPE),
                          compiler_params=_params("parallel", "parallel", "parallel", "parallel"))(core, g, r1)


def add_chips(p, r2, place, name):
    n, L, h, c = p.shape
    th = _tile(h, 512, 16)
    tc = _tile(c, 1024)

    def body(shard_ref, core_ref, p_ref, r_ref, o_ref):
        acc = p_ref[...].astype(jnp.float32)
        for j in range(3):
            acc = acc + r_ref[j].astype(jnp.float32)
        o_ref[...] = acc

    grid_spec = pltpu.PrefetchScalarGridSpec(
        num_scalar_prefetch=2, grid=(L, h // th, c // tc),
        in_specs=[pl.BlockSpec((None, None, th, tc), lambda l, i, j, shard, core: (shard[0], l, i, j)),
                  pl.BlockSpec((3, None, th, tc), lambda l, i, j, shard, core: (0, l, i, j))],
        out_specs=pl.BlockSpec((None, None, th, tc), lambda l, i, j, shard, core: (l, core[0], i, j)))
    return pl.pallas_call(body, name=name, grid_spec=grid_spec, out_shape=_f32((L, 2, h, c)),
                          compiler_params=_params("parallel", "parallel", "parallel"))(*place, p, r2)


def reduce_scatter_start(g4, core, after, name):
    n, L, r, c = g4.shape
    g = g4.reshape(n, L, 2, r // 2, c)
    r1 = sibling_send_halves(g, after, name + "_d2d")
    p = add_sibling(g, r1, core, name + "_add1")
    land = lax.empty((N_PEERS, L, r // 2, c), p.dtype)
    return exchange_start((p, land), _rs_src, _rs_dst, name + "_ici_start")


def reduce_scatter_finish(pending, place, after, name):
    sems, bufs, _ = pending
    p, r2 = exchange_wait(sems, bufs, _rs_src, _rs_dst, after, name + "_ici_wait")
    red = add_chips(p, r2, place, name + "_add2")
    L, two, h, c = red.shape
    return sibling_gather_halves(red, name + "_gather").reshape(2 * h, c)


def all_reduce_small(v, name):
    R, C = v.shape

    def body(v_ref, o_ref, buf, send_sems, recv_sems):
        x, y, cc = _me()
        buf[0] = v_ref[...]
        cps = []
        for k in range(1, 8):
            dx, dy, dc = (k >> 2) & 1, (k >> 1) & 1, k & 1
            to = (x ^ dx, y ^ dy, cc ^ dc)
            cp = pltpu.make_async_remote_copy(src_ref=v_ref, dst_ref=buf.at[k], send_sem=send_sems.at[k],
                                              recv_sem=recv_sems.at[k], device_id=to, device_id_type=MESH)
            cp.start()
            cps.append(cp)
        for cp in cps:
            cp.wait()
        me = 4 * x + 2 * y + cc
        acc = buf[me]
        for a in range(1, 8):
            acc = acc + buf[a ^ me]
        o_ref[...] = acc

    vm = pl.BlockSpec(memory_space=pltpu.VMEM)
    return pl.pallas_call(
        body, name=name, out_shape=_f32((R, C)), in_specs=[vm], out_specs=vm,
        scratch_shapes=[pltpu.VMEM((8, R, C), jnp.float32), pltpu.SemaphoreType.DMA((8,)), pltpu.SemaphoreType.DMA((8,))],
    )(v)


def _rstd(x, n):
    return lax.rsqrt(jnp.sum(x * x, axis=-1, keepdims=True) * (1.0 / n) + EPS)


def _accumulate(ref, part, first):
    @pl.when(first)
    def _():
        ref[...] = part

    @pl.when(jnp.logical_not(first))
    def _():
        ref[...] += part


def rmsnorm_fwd(x, g, name):
    S, D = x.shape
    ts = _tile(S, 256, 8)

    def body(x_ref, g_ref, o_ref):
        xv = x_ref[...]
        o_ref[...] = (xv * _rstd(xv, D) * g_ref[...]).astype(o_ref.dtype)

    return pl.pallas_call(
        body, name=name, out_shape=_act((S, D)), grid=(S // ts,),
        in_specs=[pl.BlockSpec((ts, D), lambda i: (i, 0)), pl.BlockSpec((1, D), lambda i: (0, 0))],
        out_specs=pl.BlockSpec((ts, D), lambda i: (i, 0)), compiler_params=_params("parallel"))(x, g)


def _norm_bwd(x, g, dy, n):
    r = _rstd(x, n)
    xh = x * r
    dxh = dy * g
    dx = r * (dxh - xh * (jnp.sum(dxh * xh, axis=-1, keepdims=True) * (1.0 / n)))
    return dx, dy * xh


def rmsnorm_bwd(x, g, dy, dres, name):
    S, D = x.shape
    ts = _tile(S, 256, 8)

    def body(x_ref, g_ref, dy_ref, dres_ref, dx_ref, dg_ref):
        dx, dgp = _norm_bwd(x_ref[...], g_ref[...], dy_ref[...], D)
        dx_ref[...] = dres_ref[...] + dx
        _accumulate(dg_ref, jnp.sum(dgp, axis=0, keepdims=True), pl.program_id(0) == 0)

    row = pl.BlockSpec((ts, D), lambda i: (i, 0))
    vec = pl.BlockSpec((1, D), lambda i: (0, 0))
    return pl.pallas_call(
        body, name=name, out_shape=(_f32((S, D)), _f32((1, D))), grid=(S // ts,),
        in_specs=[row, vec, row, row], out_specs=(row, vec), compiler_params=_params("arbitrary"))(x, g, dy, dres)


def _sigmoid(x):
    return 1.0 / (1.0 + jnp.exp(-x))


def swiglu_fwd(u, name):
    S, F2 = u.shape
    F = F2 // 2
    ts, tf = _tile(S, 512, 8), _tile(F, 512)
    nf = F // tf

    def body(g_ref, u_ref, o_ref):
        gt = g_ref[...]
        o_ref[...] = (gt * _sigmoid(gt) * u_ref[...]).astype(o_ref.dtype)

    return pl.pallas_call(
        body, name=name, out_shape=_act((S, F)), grid=(S // ts, nf),
        in_specs=[pl.BlockSpec((ts, tf), lambda i, j: (i, j)), pl.BlockSpec((ts, tf), lambda i, j: (i, j + nf))],
        out_specs=pl.BlockSpec((ts, tf), lambda i, j: (i, j)), compiler_params=_params("parallel", "parallel"))(u, u)


def swiglu_bwd(u, da, name):
    S, F2 = u.shape
    F = F2 // 2
    ts, tf = _tile(S, 512, 8), _tile(F, 512)
    nf = F // tf

    def body(g_ref, u_ref, da_ref, o_ref):
        gt, up, d = g_ref[...], u_ref[...], da_ref[...]
        s = _sigmoid(gt)
        o_ref[0] = (d * up * (s * (1.0 + gt * (1.0 - s)))).astype(o_ref.dtype)
        o_ref[1] = (d * (gt * s)).astype(o_ref.dtype)

    blk = pl.BlockSpec((ts, tf), lambda i, j: (i, j))
    return pl.pallas_call(
        body, name=name, out_shape=_act((2, 1, S, F)), grid=(S // ts, nf),
        in_specs=[blk, pl.BlockSpec((ts, tf), lambda i, j: (i, j + nf)), blk],
        out_specs=pl.BlockSpec((2, None, ts, tf), lambda i, j: (0, 0, i, j)),
        compiler_params=_params("parallel", "parallel"))(u, u, da)


def loss_head(y, t, name):
    S, D = y.shape
    ts = _tile(S, 256, 8)

    def body(y_ref, t_ref, dy_ref, l_ref):
        e = y_ref[...] - t_ref[...]
        dy_ref[...] = e * (1.0 / D)
        l_ref[...] = jnp.full(l_ref.shape, 0.5 * jnp.sum(jnp.sum(e * e, axis=-1, keepdims=True) * (1.0 / D)), jnp.float32)

    row = pl.BlockSpec((ts, D), lambda i: (i, 0))
    dy, parts = pl.pallas_call(
        body, name=name, out_shape=(_f32((S, D)), _f32((S // ts, 8, LANES))), grid=(S // ts,),
        in_specs=[row, row], out_specs=(row, pl.BlockSpec((None, 8, LANES), lambda i: (i, 0, 0))),
        compiler_params=_params("parallel"))(y, t)
    return jnp.sum(parts[:, 0, 0]), dy


def rope_tables(S):
    inv = 1.0 / (ROPE_THETA ** (jnp.arange(0, ROPE_DIM, 2, dtype=jnp.float32) / ROPE_DIM))
    ang = jnp.arange(S, dtype=jnp.float32)[:, None] * inv[None, :]
    c, s = jnp.cos(ang), jnp.sin(ang)
    z = jnp.zeros_like(c)
    return (jnp.concatenate([c, c, z, z], axis=1), jnp.concatenate([-s, z, z, z], axis=1),
            jnp.concatenate([z, s, z, z], axis=1))


def _rope(x, cos, sa, sb):
    return x * cos + pltpu.roll(x, 96, 1) * sa + pltpu.roll(x, 32, 1) * sb


def _rope_t(d, cos, sa, sb):
    return d * cos + pltpu.roll(d * sa, 32, 1) + pltpu.roll(d * sb, 96, 1)


def _head_norm(x1, x2, g):
    r = lax.rsqrt((jnp.sum(x1 * x1, axis=-1, keepdims=True) + jnp.sum(x2 * x2, axis=-1, keepdims=True)) * (1.0 / QK_DIM) + EPS)
    return x1 * r * g[:, :LANES], x2 * r * g[:, LANES:], r


def _head_norm_bwd(x1, x2, g, d1, d2):
    _, _, r = _head_norm(x1, x2, g)
    h1, h2 = x1 * r, x2 * r
    e1, e2 = d1 * g[:, :LANES], d2 * g[:, LANES:]
    m = (jnp.sum(e1 * h1, axis=-1, keepdims=True) + jnp.sum(e2 * h2, axis=-1, keepdims=True)) * (1.0 / QK_DIM)
    return r * (e1 - h1 * m), r * (e2 - h2 * m), d1 * h1, d2 * h2


def mla_latent_fwd(lat, g_cq, g_ckv, name):
    S, W = lat.shape
    QL, KL = g_cq.shape[1], g_ckv.shape[1]
    ts = _tile(S, 256, 8)

    def body(l_ref, gq_ref, gk_ref, cq_ref, ckv_ref):
        a, b = l_ref[:, :QL], l_ref[:, QL:QL + KL]
        cq_ref[...] = (a * _rstd(a, QL) * gq_ref[...]).astype(cq_ref.dtype)
        ckv_ref[...] = (b * _rstd(b, KL) * gk_ref[...]).astype(ckv_ref.dtype)

    return pl.pallas_call(
        body, name=name, out_shape=(_act((S, QL)), _act((S, KL))), grid=(S // ts,),
        in_specs=[pl.BlockSpec((ts, W), lambda i: (i, 0)), pl.BlockSpec((1, QL), lambda i: (0, 0)),
                  pl.BlockSpec((1, KL), lambda i: (0, 0))],
        out_specs=(pl.BlockSpec((ts, QL), lambda i: (i, 0)), pl.BlockSpec((ts, KL), lambda i: (i, 0))),
        compiler_params=_params("parallel"))(lat, g_cq, g_ckv)


def mla_latent_bwd(dcq, dckv, dkpe, lat, g_cq, g_ckv, name):
    S, W = lat.shape
    QL, KL = g_cq.shape[1], g_ckv.shape[1]
    ts = _tile(S, 256, 8)

    def body(dq_ref, dk_ref, dp_ref, l_ref, gq_ref, gk_ref, o_ref, dgq_ref, dgk_ref):
        first = pl.program_id(0) == 0
        da, ga = _norm_bwd(l_ref[:, :QL], gq_ref[...], dq_ref[...], QL)
        db, gb = _norm_bwd(l_ref[:, QL:QL + KL], gk_ref[...], dk_ref[...], KL)
        o_ref[:, :QL] = da.astype(o_ref.dtype)
        o_ref[:, QL:QL + KL] = db.astype(o_ref.dtype)
        o_ref[:, QL + KL:] = dp_ref[...].astype(o_ref.dtype)
        _accumulate(dgq_ref, jnp.sum(ga, axis=0, keepdims=True), first)
        _accumulate(dgk_ref, jnp.sum(gb, axis=0, keepdims=True), first)

    row = lambda n: pl.BlockSpec((ts, n), lambda i: (i, 0))
    vec = lambda n: pl.BlockSpec((1, n), lambda i: (0, 0))
    return pl.pallas_call(
        body, name=name, out_shape=(_act((S, W)), _f32((1, QL)), _f32((1, KL))), grid=(S // ts,),
        in_specs=[row(QL), row(KL), row(LANES), row(W), vec(QL), vec(KL)], out_specs=(row(W), vec(QL), vec(KL)),
        compiler_params=_params("arbitrary"))(dcq, dckv, dkpe, lat, g_cq, g_ckv)


def mla_q_prep_fwd(qraw, g, tabs, H, name):
    S = qraw.shape[0]
    ts = _tile(S, 256, 8)

    def body(x_ref, g_ref, c_ref, a_ref, b_ref, o_ref):
        y1, y2, _ = _head_norm(x_ref[:, :LANES], x_ref[:, LANES:], g_ref[...])
        o_ref[:, :LANES] = y1.astype(o_ref.dtype)
        o_ref[:, LANES:] = _rope(y2, c_ref[...], a_ref[...], b_ref[...]).astype(o_ref.dtype)

    tab = pl.BlockSpec((ts, LANES), lambda i, h: (i, 0))
    return pl.pallas_call(
        body, name=name, out_shape=_act((H, S, QK_PAD)), grid=(S // ts, H),
        in_specs=[pl.BlockSpec((ts, QK_PAD), lambda i, h: (i, h)), pl.BlockSpec((1, QK_PAD), lambda i, h: (0, 0)), tab, tab, tab],
        out_specs=pl.BlockSpec((None, ts, QK_PAD), lambda i, h: (h, i, 0)),
        compiler_params=_params("parallel", "parallel"))(qraw, g, *tabs)


def mla_q_prep_bwd(dq, qraw, g, tabs, H, name):
    S = qraw.shape[0]
    ts = _tile(S, 256, 8)

    def body(d_ref, x_ref, g_ref, c_ref, a_ref, b_ref, o_ref, dg_ref):
        d2 = _rope_t(d_ref[:, LANES:], c_ref[...], a_ref[...], b_ref[...])
        dx1, dx2, g1, g2 = _head_norm_bwd(x_ref[:, :LANES], x_ref[:, LANES:], g_ref[...], d_ref[:, :LANES], d2)
        o_ref[:, :LANES] = dx1.astype(o_ref.dtype)
        o_ref[:, LANES:] = dx2.astype(o_ref.dtype)
        first = jnp.logical_and(pl.program_id(0) == 0, pl.program_id(1) == 0)
        part = jnp.concatenate([jnp.sum(g1, axis=0, keepdims=True), jnp.sum(g2, axis=0, keepdims=True)], axis=1)
        _accumulate(dg_ref, part, first)

    tab = pl.BlockSpec((ts, LANES), lambda i, h: (i, 0))
    vec = pl.BlockSpec((1, QK_PAD), lambda i, h: (0, 0))
    return pl.pallas_call(
        body, name=name, out_shape=(_act((S, H * QK_PAD)), _f32((1, QK_PAD))), grid=(S // ts, H),
        in_specs=[pl.BlockSpec((None, ts, QK_PAD), lambda i, h: (h, i, 0)), pl.BlockSpec((ts, QK_PAD), lambda i, h: (i, h)),
                  vec, tab, tab, tab],
        out_specs=(pl.BlockSpec((ts, QK_PAD), lambda i, h: (i, h)), vec),
        compiler_params=_params("arbitrary", "arbitrary"))(dq, qraw, g, *tabs)


def mla_k_prep_fwd(kvraw, lat, g, tabs, H, pe_blk, name):
    S = kvraw.shape[0]
    ts = _tile(S, 256, 8)

    def body(x_ref, p_ref, g_ref, c_ref, a_ref, b_ref, k_ref, v_ref):
        y1, y2, _ = _head_norm(x_ref[:, :LANES], p_ref[...], g_ref[...])
        k_ref[:, :LANES] = y1.astype(k_ref.dtype)
        k_ref[:, LANES:] = _rope(y2, c_ref[...], a_ref[...], b_ref[...]).astype(k_ref.dtype)
        v_ref[...] = x_ref[:, LANES:].astype(v_ref.dtype)

    tab = pl.BlockSpec((ts, LANES), lambda i, h: (i, 0))
    return pl.pallas_call(
        body, name=name, out_shape=(_act((H, S, QK_PAD)), _act((H, S, LANES))), grid=(S // ts, H),
        in_specs=[pl.BlockSpec((ts, QK_PAD), lambda i, h: (i, h)), pl.BlockSpec((ts, LANES), lambda i, h: (i, pe_blk)),
                  pl.BlockSpec((1, QK_PAD), lambda i, h: (0, 0)), tab, tab, tab],
        out_specs=(pl.BlockSpec((None, ts, QK_PAD), lambda i, h: (h, i, 0)), pl.BlockSpec((None, ts, LANES), lambda i, h: (h, i, 0))),
        compiler_params=_params("parallel", "parallel"))(kvraw, lat, g, *tabs)


def mla_k_prep_bwd(dk, dv, kvraw, lat, g, tabs, H, pe_blk, name):
    S = kvraw.shape[0]
    ts = _tile(S, 256, 8)

    def body(dk_ref, dv_ref, x_ref, p_ref, g_ref, c_ref, a_ref, b_ref, o_ref, dp_ref, dg_ref):
        i, h = pl.program_id(0), pl.program_id(1)
        d2 = _rope_t(dk_ref[:, LANES:], c_ref[...], a_ref[...], b_ref[...])
        dx1, dx2, g1, g2 = _head_norm_bwd(x_ref[:, :LANES], p_ref[...], g_ref[...], dk_ref[:, :LANES], d2)
        o_ref[:, :LANES] = dx1.astype(o_ref.dtype)
        o_ref[:, LANES:] = dv_ref[...].astype(o_ref.dtype)
        _accumulate(dp_ref, dx2, h == 0)
        part = jnp.concatenate([jnp.sum(g1, axis=0, keepdims=True), jnp.sum(g2, axis=0, keepdims=True)], axis=1)
        _accumulate(dg_ref, part, jnp.logical_and(i == 0, h == 0))

    tab = pl.BlockSpec((ts, LANES), lambda i, h: (i, 0))
    vec = pl.BlockSpec((1, QK_PAD), lambda i, h: (0, 0))
    return pl.pallas_call(
        body, name=name, out_shape=(_act((S, H * QK_PAD)), _f32((S, LANES)), _f32((1, QK_PAD))), grid=(S // ts, H),
        in_specs=[pl.BlockSpec((None, ts, QK_PAD), lambda i, h: (h, i, 0)), pl.BlockSpec((None, ts, LANES), lambda i, h: (h, i, 0)),
                  pl.BlockSpec((ts, QK_PAD), lambda i, h: (i, h)), pl.BlockSpec((ts, LANES), lambda i, h: (i, pe_blk)),
                  vec, tab, tab, tab],
        out_specs=(pl.BlockSpec((ts, QK_PAD), lambda i, h: (i, h)), tab, vec),
        compiler_params=_params("arbitrary", "arbitrary"))(dk, dv, kvraw, lat, g, *tabs)


def _causal_scores(q, k, scale, diagonal):
    s = lax.dot_general(q, k, NT, preferred_element_type=jnp.float32) * scale
    if not diagonal:
        return s
    row = lax.broadcasted_iota(jnp.int32, s.shape, 0)
    col = lax.broadcasted_iota(jnp.int32, s.shape, 1)
    return jnp.where(col <= row, s, NEG)


def _on_causal_blocks(qi, ki, step):
    @pl.when(ki < qi)
    def _():
        step(False)

    @pl.when(ki == qi)
    def _():
        step(True)


def mla_attention_fwd(q, k, v, name):
    H, S, _ = q.shape
    t = _tile(S, 512)
    n = S // t
    scale = 1.0 / math.sqrt(QK_DIM)

    def body(q_ref, k_ref, v_ref, o_ref, lse_ref, m_sc, l_sc, acc):
        qi, ki = pl.program_id(1), pl.program_id(2)

        @pl.when(ki == 0)
        def _():
            m_sc[...] = jnp.full(m_sc.shape, NEG, jnp.float32)
            l_sc[...] = jnp.zeros_like(l_sc)
            acc[...] = jnp.zeros_like(acc)

        def step(diagonal):
            s = _causal_scores(q_ref[...], k_ref[...], scale, diagonal)
            m_new = jnp.maximum(m_sc[...], jnp.max(s, axis=-1, keepdims=True))
            alpha = jnp.exp(m_sc[...] - m_new)
            p = jnp.exp(s - m_new)
            l_sc[...] = alpha * l_sc[...] + jnp.sum(p, axis=-1, keepdims=True)
            acc[...] = alpha * acc[...] + jnp.dot(p.astype(MXU_DTYPE), v_ref[...], preferred_element_type=jnp.float32)
            m_sc[...] = m_new

        _on_causal_blocks(qi, ki, step)

        @pl.when(ki == qi)
        def _():
            o_ref[...] = (acc[...] / l_sc[...]).astype(o_ref.dtype)
            lse_ref[...] = m_sc[...] + jnp.log(l_sc[...])

    kv = lambda w: pl.BlockSpec((None, t, w), lambda h, qi, ki: (h, jnp.minimum(ki, qi), 0))
    return pl.pallas_call(
        body, name=name, out_shape=(_act((S, H * LANES)), _f32((H, S, 1))), grid=(H, n, n),
        in_specs=[pl.BlockSpec((None, t, QK_PAD), lambda h, qi, ki: (h, qi, 0)), kv(QK_PAD), kv(LANES)],
        out_specs=(pl.BlockSpec((t, LANES), lambda h, qi, ki: (qi, h)), pl.BlockSpec((None, t, 1), lambda h, qi, ki: (h, qi, 0))),
        scratch_shapes=[pltpu.VMEM((t, 1), jnp.float32), pltpu.VMEM((t, 1), jnp.float32), pltpu.VMEM((t, LANES), jnp.float32)],
        compiler_params=_params("parallel", "parallel", "arbitrary"))(q, k, v)


def attention_delta(do, o, H, name):
    S = do.shape[0]
    ts = _tile(S, 512, 8)

    def body(d_ref, o_ref, out_ref):
        out_ref[...] = jnp.sum(d_ref[...] * o_ref[...].astype(jnp.float32), axis=-1, keepdims=True)

    blk = pl.BlockSpec((ts, LANES), lambda i, h: (i, h))
    return pl.pallas_call(
        body, name=name, out_shape=_f32((H, S, 1)), grid=(S // ts, H), in_specs=[blk, blk],
        out_specs=pl.BlockSpec((None, ts, 1), lambda i, h: (h, i, 0)), compiler_params=_params("parallel", "parallel"))(do, o)


def mla_attention_bwd_dq(q, k, v, do, lse, delta, name):
    H, S, _ = q.shape
    t = _tile(S, 512)
    n = S // t
    scale = 1.0 / math.sqrt(QK_DIM)

    def body(q_ref, k_ref, v_ref, do_ref, lse_ref, dl_ref, dq_ref, acc):
        qi, ki = pl.program_id(1), pl.program_id(2)

        @pl.when(ki == 0)
        def _():
            acc[...] = jnp.zeros_like(acc)

        def step(diagonal):
            p = jnp.exp(_causal_scores(q_ref[...], k_ref[...], scale, diagonal) - lse_ref[...])
            dp = lax.dot_general(do_ref[...].astype(MXU_DTYPE), v_ref[...], NT, preferred_element_type=jnp.float32)
            ds = p * (dp - dl_ref[...])
            acc[...] += jnp.dot(ds.astype(MXU_DTYPE), k_ref[...], preferred_element_type=jnp.float32)

        _on_causal_blocks(qi, ki, step)

        @pl.when(ki == qi)
        def _():
            dq_ref[...] = acc[...] * scale

    kv = lambda w: pl.BlockSpec((None, t, w), lambda h, qi, ki: (h, jnp.minimum(ki, qi), 0))
    col = pl.BlockSpec((None, t, 1), lambda h, qi, ki: (h, qi, 0))
    qspec = pl.BlockSpec((None, t, QK_PAD), lambda h, qi, ki: (h, qi, 0))
    return pl.pallas_call(
        body, name=name, out_shape=_f32((H, S, QK_PAD)), grid=(H, n, n),
        in_specs=[qspec, kv(QK_PAD), kv(LANES), pl.BlockSpec((t, LANES), lambda h, qi, ki: (qi, h)), col, col],
        out_specs=qspec, scratch_shapes=[pltpu.VMEM((t, QK_PAD), jnp.float32)],
        compiler_params=_params("parallel", "parallel", "arbitrary"))(q, k, v, do, lse, delta)


def mla_attention_bwd_dkv(q, k, v, do, lse, delta, name):
    H, S, _ = q.shape
    t = _tile(S, 512)
    n = S // t
    scale = 1.0 / math.sqrt(QK_DIM)

    def body(q_ref, k_ref, v_ref, do_ref, lse_ref, dl_ref, dk_ref, dv_ref, dk_acc, dv_acc):
        ki, qi = pl.program_id(1), pl.program_id(2)

        @pl.when(qi == 0)
        def _():
            dk_acc[...] = jnp.zeros_like(dk_acc)
            dv_acc[...] = jnp.zeros_like(dv_acc)

        def step(diagonal):
            p = jnp.exp(_causal_scores(q_ref[...], k_ref[...], scale, diagonal) - lse_ref[...])
            dob = do_ref[...].astype(MXU_DTYPE)
            dv_acc[...] += lax.dot_general(p.astype(MXU_DTYPE), dob, TN, preferred_element_type=jnp.float32)
            dp = lax.dot_general(dob, v_ref[...], NT, preferred_element_type=jnp.float32)
            ds = p * (dp - dl_ref[...])
            dk_acc[...] += lax.dot_general(ds.astype(MXU_DTYPE), q_ref[...], TN, preferred_element_type=jnp.float32)

        _on_causal_blocks(qi, ki, step)

        @pl.when(qi == n - 1)
        def _():
            dk_ref[...] = dk_acc[...] * scale
            dv_ref[...] = dv_acc[...]

    qrow = lambda h, ki, qi: (h, jnp.maximum(qi, ki), 0)
    kv = lambda w: pl.BlockSpec((None, t, w), lambda h, ki, qi: (h, ki, 0))
    col = pl.BlockSpec((None, t, 1), qrow)
    return pl.pallas_call(
        body, name=name, out_shape=(_f32((H, S, QK_PAD)), _f32((H, S, LANES))), grid=(H, n, n),
        in_specs=[pl.BlockSpec((None, t, QK_PAD), qrow), kv(QK_PAD), kv(LANES),
                  pl.BlockSpec((t, LANES), lambda h, ki, qi: (jnp.maximum(qi, ki), h)), col, col],
        out_specs=(kv(QK_PAD), kv(LANES)),
        scratch_shapes=[pltpu.VMEM((t, QK_PAD), jnp.float32), pltpu.VMEM((t, LANES), jnp.float32)],
        compiler_params=_params("parallel", "parallel", "arbitrary"))(q, k, v, do, lse, delta)


def _alibi_slopes(G, Hd):
    k = np.arange(1, G * Hd + 1, dtype=np.float32)
    s = (2.0 ** (-8.0 * k / (G * Hd))).astype(np.float32).reshape(G, Hd)
    return jnp.asarray(np.broadcast_to(s[:, :, None, None], (G, Hd, 1, LANES)).copy())


def _dil_scores(qn, kn, scale, slope_d, prev, valid):
    s = lax.dot_general(qn, kn, NT, preferred_element_type=jnp.float32) * scale
    iq = lax.broadcasted_iota(jnp.int32, s.shape, 0)
    ik = lax.broadcasted_iota(jnp.int32, s.shape, 1)
    dist = iq - ik + (BLK if prev else 0)
    ok = (ik >= iq) if prev else (ik <= iq)
    s = s - slope_d * dist.astype(jnp.float32)
    return jnp.where(jnp.logical_and(ok, valid), s, NEG)


def _dil_specs(d, nb, Hd, G, g, ncol):
    def spec(kind, shift):
        col0 = (kind * G + g) * Hd

        def imap(r, n, h):
            return (jnp.clip(n + shift, 0, nb - 1), r * ncol + col0 + h)
        return pl.BlockSpec((BLK, LANES), imap)
    return spec


def dilated_fwd(qkv, gq, gk, slopes, g, d, Hd, G, name):
    S, C = qkv.shape
    ncol = C // LANES
    nb = S // d // BLK
    view = qkv.reshape(S // d, d * C)
    scale = 1.0 / math.sqrt(LANES)
    spec = _dil_specs(d, nb, Hd, G, g, ncol)

    def body(q_ref, kc_ref, kp_ref, vc_ref, vp_ref, gq_ref, gk_ref, sl_ref, o_ref, l_ref):
        n = pl.program_id(1)
        nrm = lambda t, gg: (t * _rstd(t, LANES) * gg).astype(MXU_DTYPE)
        qn = nrm(q_ref[...], gq_ref[...])
        slope_d = sl_ref[:, :1] * float(d)
        sc = _dil_scores(qn, nrm(kc_ref[...], gk_ref[...]), scale, slope_d, False, True)
        sp = _dil_scores(qn, nrm(kp_ref[...], gk_ref[...]), scale, slope_d, True, n > 0)
        m = jnp.maximum(jnp.max(sc, axis=-1, keepdims=True), jnp.max(sp, axis=-1, keepdims=True))
        lse = m + jnp.log(jnp.sum(jnp.exp(sc - m), axis=-1, keepdims=True) + jnp.sum(jnp.exp(sp - m), axis=-1, keepdims=True))
        o = jnp.dot(jnp.exp(sc - lse).astype(MXU_DTYPE), vc_ref[...].astype(MXU_DTYPE), preferred_element_type=jnp.float32)
        o = o + jnp.dot(jnp.exp(sp - lse).astype(MXU_DTYPE), vp_ref[...].astype(MXU_DTYPE), preferred_element_type=jnp.float32)
        o_ref[...] = o
        l_ref[...] = jnp.broadcast_to(lse, l_ref.shape)

    vec = pl.BlockSpec((1, LANES), lambda r, n, h: (0, 0))
    out = pl.BlockSpec((BLK, LANES), lambda r, n, h: (n, r * Hd + h))
    o, l = pl.pallas_call(
        body, name=name, out_shape=(_f32((S // d, d * Hd * LANES)), _f32((S // d, d * Hd * LANES))), grid=(d, nb, Hd),
        in_specs=[spec(0, 0), spec(1, 0), spec(1, -1), spec(2, 0), spec(2, -1), vec, vec,
                  pl.BlockSpec((None, None, 1, LANES), lambda r, n, h: (g, h, 0, 0))],
        out_specs=(out, out), compiler_params=_params("parallel", "parallel", "parallel"),
    )(view, view, view, view, view, gq, gk, slopes)
    return o.reshape(S, Hd * LANES), l.reshape(S, Hd * LANES)


def dilated_merge(os_, ls_, name):
    S, W = os_[0].shape
    G = len(os_)
    ts, tw = _tile(S, 512, 8), _tile(W, 512)

    def body(*refs):
        o_refs, l_refs, (o_ref, t_ref) = refs[:G], refs[G:2 * G], refs[2 * G:]
        ls = [r[...] for r in l_refs]
        m = ls[0]
        for l in ls[1:]:
            m = jnp.maximum(m, l)
        es = [jnp.exp(l - m) for l in ls]
        tot = es[0]
        for e in es[1:]:
            tot = tot + e
        acc = o_refs[0][...] * (es[0] / tot)
        for r, e in zip(o_refs[1:], es[1:]):
            acc = acc + r[...] * (e / tot)
        o_ref[...] = acc.astype(o_ref.dtype)
        t_ref[...] = m + jnp.log(tot)

    blk = pl.BlockSpec((ts, tw), lambda i, j: (i, j))
    return pl.pallas_call(
        body, name=name, out_shape=(_act((S, W)), _f32((S, W))), grid=(S // ts, W // tw),
        in_specs=[blk] * (2 * G), out_specs=(blk, blk), compiler_params=_params("parallel", "parallel"))(*os_, *ls_)


def dilated_delta(do, o, name):
    S, W = do.shape
    ts = _tile(S, 512, 8)

    def body(d_ref, o_ref, out_ref):
        out_ref[...] = jnp.broadcast_to(jnp.sum(d_ref[...] * o_ref[...].astype(jnp.float32), axis=-1, keepdims=True), out_ref.shape)

    blk = pl.BlockSpec((ts, LANES), lambda i, h: (i, h))
    return pl.pallas_call(body, name=name, out_shape=_f32((S, W)), grid=(S // ts, W // LANES), in_specs=[blk, blk],
                          out_specs=blk, compiler_params=_params("parallel", "parallel"))(do, o)


def dilated_bwd(qkv, do, lse, delta, gq, gk, slopes, g, d, Hd, G, name):
    S, C = qkv.shape
    ncol = C // LANES
    nb = S // d // BLK
    W = Hd * LANES
    view = qkv.reshape(S // d, d * C)
    hview = lambda t: t.reshape(S // d, d * W)
    scale = 1.0 / math.sqrt(LANES)
    spec = _dil_specs(d, nb, Hd, G, g, ncol)

    def hspec(shift):
        return pl.BlockSpec((BLK, LANES), lambda r, n, h: (jnp.clip(n + shift, 0, nb - 1), r * Hd + h))

    def body(q_ref, qx_ref, kc_ref, kp_ref, vc_ref, vp_ref, do_ref, dox_ref, l_ref, lx_ref, dl_ref, dlx_ref,
             gq_ref, gk_ref, sl_ref, dq_ref, dk_ref, dv_ref, dgq_ref, dgk_ref):
        r, n, h = pl.program_id(0), pl.program_id(1), pl.program_id(2)
        first = jnp.logical_and(jnp.logical_and(r == 0, n == 0), h == 0)
        gqv, gkv = gq_ref[...], gk_ref[...]
        nrm = lambda t, gg: (t * _rstd(t, LANES) * gg).astype(MXU_DTYPE)
        f32dot = lambda a, b, dn: lax.dot_general(a, b, dn, preferred_element_type=jnp.float32)
        qn, qxn = nrm(q_ref[...], gqv), nrm(qx_ref[...], gqv)
        kcn, kpn = nrm(kc_ref[...], gkv), nrm(kp_ref[...], gkv)
        vc, vp = vc_ref[...].astype(MXU_DTYPE), vp_ref[...].astype(MXU_DTYPE)
        dob, doxb = do_ref[...].astype(MXU_DTYPE), dox_ref[...].astype(MXU_DTYPE)
        slope_d = sl_ref[:, :1] * float(d)
        lrow, lxrow = l_ref[:, :1], lx_ref[:, :1]
        drow, dxrow = dl_ref[:, :1], dlx_ref[:, :1]
        pc = jnp.exp(_dil_scores(qn, kcn, scale, slope_d, False, True) - lrow)
        pp = jnp.exp(_dil_scores(qn, kpn, scale, slope_d, True, n > 0) - lrow)
        dsc = pc * (f32dot(dob, vc, NT) - drow)
        dsp = pp * (f32dot(dob, vp, NT) - drow)
        dqn = (jnp.dot(dsc.astype(MXU_DTYPE), kcn, preferred_element_type=jnp.float32)
               + jnp.dot(dsp.astype(MXU_DTYPE), kpn, preferred_element_type=jnp.float32)) * scale
        dq, dgq = _norm_bwd(q_ref[...], gqv, dqn, LANES)
        dq_ref[...] = dq.astype(dq_ref.dtype)
        px = jnp.exp(_dil_scores(qxn, kcn, scale, slope_d, True, n < nb - 1) - lxrow)
        dsx = px * (f32dot(doxb, vc, NT) - dxrow)
        dkn = (f32dot(dsc.astype(MXU_DTYPE), qn, TN) + f32dot(dsx.astype(MXU_DTYPE), qxn, TN)) * scale
        dk, dgk = _norm_bwd(kc_ref[...], gkv, dkn, LANES)
        dk_ref[...] = dk.astype(dk_ref.dtype)
        dv_ref[...] = (f32dot(pc.astype(MXU_DTYPE), dob, TN) + f32dot(px.astype(MXU_DTYPE), doxb, TN)).astype(dv_ref.dtype)
        _accumulate(dgq_ref, jnp.sum(dgq, axis=0, keepdims=True), first)
        _accumulate(dgk_ref, jnp.sum(dgk, axis=0, keepdims=True), first)

    vec = pl.BlockSpec((1, LANES), lambda r, n, h: (0, 0))
    out = hspec(0)
    dq, dk, dv, dgq, dgk = pl.pallas_call(
        body, name=name,
        out_shape=(_act((S // d, d * W)), _act((S // d, d * W)), _act((S // d, d * W)), _f32((1, LANES)), _f32((1, LANES))),
        grid=(d, nb, Hd),
        in_specs=[spec(0, 0), spec(0, 1), spec(1, 0), spec(1, -1), spec(2, 0), spec(2, -1), hspec(0), hspec(1), hspec(0), hspec(1),
                  hspec(0), hspec(1), vec, vec, pl.BlockSpec((None, None, 1, LANES), lambda r, n, h: (g, h, 0, 0))],
        out_specs=(out, out, out, vec, vec), compiler_params=_params("arbitrary", "arbitrary", "arbitrary"),
    )(view, view, view, view, view, view, hview(do), hview(do), hview(lse), hview(lse), hview(delta), hview(delta), gq, gk, slopes)
    return dq.reshape(S, W), dk.reshape(S, W), dv.reshape(S, W), dgq, dgk


def adamw(w, g, m, v, layer, prev, name):
    L, r, c = w.shape
    tr, tc = _tile(r, 256, 8), _tile(c, 1024)
    c1 = 1.0 / (1.0 - ADAM_B1 ** ADAM_STEP)
    c2 = 1.0 / (1.0 - ADAM_B2 ** ADAM_STEP)

    def body(*refs):
        w_ref, g_ref, m_ref, v_ref = refs[:4]
        go_ref, d_ref, mo_ref, vo_ref = refs[-4:]
        gv = g_ref[...]
        mn = ADAM_B1 * m_ref[...] + (1.0 - ADAM_B1) * gv
        vn = ADAM_B2 * v_ref[...] + (1.0 - ADAM_B2) * (gv * gv)
        go_ref[...] = gv
        d_ref[...] = -ADAM_LR * ((mn * c1) / (jnp.sqrt(vn * c2) + ADAM_EPS) + ADAM_WD * w_ref[...])
        mo_ref[...] = mn
        vo_ref[...] = vn

    lay = pl.BlockSpec((None, tr, tc), lambda i, j: (layer, i, j))
    flat = pl.BlockSpec((tr, tc), lambda i, j: (i, j))
    ins = [w, g, m, v] + (list(prev) if prev is not None else [])
    in_specs = [lay, flat, lay, lay] + ([ANY] * 4 if prev is not None else [])
    return pl.pallas_call(
        body, name=name, out_shape=tuple(_f32((L, r, c)) for _ in range(4)), grid=(r // tr, c // tc),
        in_specs=in_specs, out_specs=(lay, lay, lay, lay),
        input_output_aliases=({4 + k: k for k in range(4)} if prev is not None else {}),
        compiler_params=_params("parallel", "parallel"))(*ins)


def _ffn_fwd(h, g, W, kind, tag):
    xn = rmsnorm_fwd(h, g, tag + "_norm")
    u = matmul(xn, W(kind + "_w_in", h), name=tag + "_in", tn=1408)
    a = swiglu_fwd(u, tag + "_act")
    out = matmul(a, W(kind + "_w_out", u), scale=0.5, res=h, name=tag + "_out", tk=1408)
    return out, (h, xn, u, a)


def _ffn_bwd(dout, saved, g, W, emit, kind, tag):
    h, xn, u, a = saved
    emit(kind + "_w_out", matmul(a, dout, ta=True, scale=0.5, out_dtype=WIRE_DTYPE, out_axis=0, name=tag + "_dwout", tm=1408))
    da = matmul(dout, W(kind + "_w_out", None), tb=True, scale=0.5, name=tag + "_da", tn=1408)
    du = Sharded(swiglu_bwd(u, da, tag + "_dact"), 0, 1)
    emit(kind + "_w_in", matmul(xn, du, ta=True, out_dtype=WIRE_DTYPE, out_axis=1, name=tag + "_dwin", tn=1408))
    dxn = matmul(du, W(kind + "_w_in", None), tb=True, name=tag + "_dxn", tk=1408)
    return rmsnorm_bwd(h, g, dxn, dout, tag + "_dnorm")


def _pad_gain(g):
    return jnp.pad(g, ((0, 0), (0, QK_PAD - QK_DIM)))


def _mla_fwd(h, P, W, tabs, H):
    g_mix, g_cq, g_ckv = P["mix_norm"][0:1], P["mla_g_cq"], P["mla_g_ckv"]
    pe_blk = (g_cq.shape[1] + g_ckv.shape[1]) // LANES
    xn = rmsnorm_fwd(h, g_mix, "mla_norm")
    w_down = W("mla_w_down", h)
    lat = matmul(xn, w_down, name="mla_down", tn=w_down.shape[1])
    cq, ckv = mla_latent_fwd(lat, g_cq, g_ckv, "mla_latent")
    qraw = matmul(cq, W("mla_w_uq", lat), name="mla_uq")
    kvraw = matmul(ckv, W("mla_w_ukv", qraw), name="mla_ukv")
    q = mla_q_prep_fwd(qraw, _pad_gain(P["mla_g_qn"]), tabs, H, "mla_qprep")
    k, v = mla_k_prep_fwd(kvraw, lat, _pad_gain(P["mla_g_kn"]), tabs, H, pe_blk, "mla_kprep")
    o, lse = mla_attention_fwd(q, k, v, "mla_attn")
    out = matmul(o, W("mla_w_o", lse), res=h, name="mla_o")
    return out, (h, xn, lat, cq, ckv, qraw, kvraw, q, k, v, o, lse, pe_blk)


def _mla_bwd(dout, saved, P, W, emit, tabs, H):
    h, xn, lat, cq, ckv, qraw, kvraw, q, k, v, o, lse, pe_blk = saved
    emit("mla_w_o", matmul(o, dout, ta=True, out_dtype=WIRE_DTYPE, out_axis=0, name="mla_dwo", tm=512))
    do = matmul(dout, W("mla_w_o", None), tb=True, name="mla_do", tn=512)
    delta = attention_delta(do, o, H, "mla_delta")
    dq = mla_attention_bwd_dq(q, k, v, do, lse, delta, "mla_attn_dq")
    dk, dv = mla_attention_bwd_dkv(q, k, v, do, lse, delta, "mla_attn_dkv")
    dqraw, dgq = mla_q_prep_bwd(dq, qraw, _pad_gain(P["mla_g_qn"]), tabs, H, "mla_dqprep")
    dkvraw, dkpe, dgk = mla_k_prep_bwd(dk, dv, kvraw, lat, _pad_gain(P["mla_g_kn"]), tabs, H, pe_blk, "mla_dkprep")
    emit("mla_w_uq", matmul(cq, dqraw, ta=True, out_dtype=WIRE_DTYPE, out_axis=1, name="mla_dwuq"))
    dcq = matmul(dqraw, W("mla_w_uq", None), tb=True, name="mla_dcq", tk=1024)
    emit("mla_w_ukv", matmul(ckv, dkvraw, ta=True, out_dtype=WIRE_DTYPE, out_axis=1, name="mla_dwukv"))
    dckv = matmul(dkvraw, W("mla_w_ukv", None), tb=True, name="mla_dckv", tk=1024)
    dlat, dgcq, dgckv = mla_latent_bwd(dcq, dckv, dkpe, lat, P["mla_g_cq"], P["mla_g_ckv"], "mla_dlatent")
    emit("mla_w_down", matmul(xn, dlat, ta=True, out_dtype=WIRE_DTYPE, out_axis=0, name="mla_dwdown", tm=512, tn=dlat.shape[1]))
    dxn = matmul(dlat, W("mla_w_down", None), tb=True, name="mla_dxn", tn=512, tk=dlat.shape[1])
    dh, dgm = rmsnorm_bwd(h, P["mix_norm"][0:1], dxn, dout, "mla_dnorm")
    return dh, dgm, dict(mla_g_qn=dgq[:, :QK_DIM], mla_g_kn=dgk[:, :QK_DIM], mla_g_cq=dgcq, mla_g_ckv=dgckv)


def _dil_fwd(h, P, W, slopes, Hd):
    G = len(DIL_PAIRS)
    xn = rmsnorm_fwd(h, P["mix_norm"][1:2], "dil_norm")
    qkv = matmul(xn, W("dil_w_qkv", h), name="dil_qkv", tn=1152)
    os_, ls_ = [], []
    for g, (_, d) in enumerate(DIL_PAIRS):
        o_g, l_g = dilated_fwd(qkv, P["dil_g_qn"], P["dil_g_kn"], slopes, g, d, Hd, G, f"dil_attn{g}")
        os_.append(o_g)
        ls_.append(l_g)
    o, lse = dilated_merge(os_, ls_, "dil_merge")
    out = matmul(o, W("dil_w_o", lse), res=h, name="dil_o", tn=512)
    return out, (h, xn, qkv, o, lse)


def _dil_bwd(dout, saved, P, W, emit, slopes, Hd):
    h, xn, qkv, o, lse = saved
    ngrp = len(DIL_PAIRS)
    emit("dil_w_o", matmul(o, dout, ta=True, out_dtype=WIRE_DTYPE, out_axis=1, name="dil_dwo", tn=512))
    do = matmul(dout, W("dil_w_o", None), tb=True, name="dil_do", tk=512)
    delta = dilated_delta(do, o, "dil_delta")
    parts = [dilated_bwd(qkv, do, lse, delta, P["dil_g_qn"], P["dil_g_kn"], slopes, g, d, Hd, ngrp, f"dil_dattn{g}")
             for g, (_, d) in enumerate(DIL_PAIRS)]
    dqkv = jnp.concatenate([p[kind] for kind in range(3) for p in parts], axis=1)
    emit("dil_w_qkv", matmul(xn, dqkv, ta=True, out_dtype=WIRE_DTYPE, out_axis=1, name="dil_dwqkv", tn=1152))
    dxn = matmul(dqkv, W("dil_w_qkv", None), tb=True, name="dil_dxn", tk=1152)
    dh, dgm = rmsnorm_bwd(h, P["mix_norm"][1:2], dxn, dout, "dil_dnorm")
    return dh, dgm, dict(dil_g_qn=parts[0][3] + parts[1][3] + parts[2][3], dil_g_kn=parts[0][4] + parts[1][4] + parts[2][4])


def local_step(x, target, P, get_w, on_grad):
    S, D = x.shape
    H, Hd = MLA_HEADS, DIL_HEADS
    tabs = rope_tables(S)
    slopes = _alibi_slopes(len(DIL_PAIRS), Hd)
    cache = {}

    def weights_of(layer):
        def W(name, after):
            if (name, layer) not in cache:
                cache[name, layer] = Sharded(get_w(name, layer, after), 0, SHARD_AXIS[name])
            return cache[name, layer]
        return W

    row = lambda name, i: P[name][i:i + 1]
    h = x
    saved = []
    for i in range(2):
        W = weights_of(i)
        h, s1 = _ffn_fwd(h, row("ffn1_norm", i), W, "ffn1", f"l{i}_ffn1")
        h, sm = _mla_fwd(h, P, weights_of(0), tabs, H) if i == 0 else _dil_fwd(h, P, weights_of(0), slopes, Hd)
        h, s2 = _ffn_fwd(h, row("ffn2_norm", i), W, "ffn2", f"l{i}_ffn2")
        saved.append((s1, sm, s2))
    loss, dh = loss_head(h, target, "loss")

    gs = {n: [None, None] for n in ("ffn1_norm", "mix_norm", "ffn2_norm")}
    for i in (1, 0):
        s1, sm, s2 = saved[i]
        W = weights_of(i)
        emit = lambda name, g4, layer=i: on_grad(name, layer, g4)
        emit0 = lambda name, g4: on_grad(name, 0, g4)
        dh, gs["ffn2_norm"][i] = _ffn_bwd(dh, s2, row("ffn2_norm", i), W, emit, "ffn2", f"l{i}_ffn2")
        if i == 0:
            dh, gs["mix_norm"][i], gm = _mla_bwd(dh, sm, P, weights_of(0), emit0, tabs, H)
        else:
            dh, gs["mix_norm"][i], gm = _dil_bwd(dh, sm, P, weights_of(0), emit0, slopes, Hd)
        gs.update({n: [val] for n, val in gm.items()})
        dh, gs["ffn1_norm"][i] = _ffn_bwd(dh, s1, row("ffn1_norm", i), W, emit, "ffn1", f"l{i}_ffn1")
    gsmall = {n: jnp.concatenate(v, axis=0) for n, v in gs.items()}
    return loss, dh, gsmall


def _pad_heads(w, real, padded):
    lead, n = w.shape[:-1], w.shape[-1] // real
    w = jnp.pad(w.reshape(*lead, n, real), [(0, 0)] * (len(lead) + 1) + [(0, padded - real)])
    return w.reshape(*lead, n * padded)


def _unpad_heads(w, real, padded):
    lead, n = w.shape[:-1], w.shape[-1] // padded
    return w.reshape(*lead, n, padded)[..., :real].reshape(*lead, n * real)


def _pack_small(gs):
    flat = jnp.concatenate([gs[n].reshape(-1) for n in SMALL])
    rows = -(-flat.shape[0] // LANES)
    rows = -(-rows // 8) * 8
    return jnp.pad(flat, (0, rows * LANES - flat.shape[0])).reshape(rows, LANES)


def _unpack_small(packed, like):
    flat, out, off = packed.reshape(-1), {}, 0
    for n in SMALL:
        size = int(np.prod(like[n].shape))
        out[n] = flat[off:off + size].reshape(like[n].shape)
        off += size
    return out


def kernel(x, ffn1_norm, ffn1_w_in, ffn1_w_out, mix_norm, ffn2_norm, ffn2_w_in, ffn2_w_out, mla_w_down, mla_g_cq, mla_g_ckv, mla_w_uq, mla_w_ukv, mla_g_qn, mla_g_kn, mla_w_o, dil_w_qkv, dil_g_qn, dil_g_kn, dil_w_o, loss_target, m_ffn1_norm, m_ffn1_w_in, m_ffn1_w_out, m_mix_norm, m_ffn2_norm, m_ffn2_w_in, m_ffn2_w_out, m_mla_w_down, m_mla_g_cq, m_mla_g_ckv, m_mla_w_uq, m_mla_w_ukv, m_mla_g_qn, m_mla_g_kn, m_mla_w_o, m_dil_w_qkv, m_dil_g_qn, m_dil_g_kn, m_dil_w_o, v_ffn1_norm, v_ffn1_w_in, v_ffn1_w_out, v_mix_norm, v_ffn2_norm, v_ffn2_w_in, v_ffn2_w_out, v_mla_w_down, v_mla_g_cq, v_mla_g_ckv, v_mla_w_uq, v_mla_w_ukv, v_mla_g_qn, v_mla_g_kn, v_mla_w_o, v_dil_w_qkv, v_dil_g_qn, v_dil_g_kn, v_dil_w_o):
    args = dict(locals())
    w = {n: args[n] for n in WEIGHTS}
    m = {n: args["m_" + n] for n in WEIGHTS}
    v = {n: args["v_" + n] for n in WEIGHTS}
    cx, cy, cc = _me()
    core = jnp.reshape(cc, (1,)).astype(jnp.int32)
    shard = jnp.reshape(2 * cx + cy, (1,)).astype(jnp.int32)
    place = (shard, core)
    pe_pad = LANES - ROPE_DIM

    started = {}
    for layer in range(2):
        for n in USE_ORDER[layer]:
            L, r, c = w[n].shape
            l = layer if L > 1 else 0
            land = cast_into_shards(w[n], l, shard, f"ag_{n}{l}_cast").reshape(N_CHIPS, 1, 2, r // 2, c)
            started[n, l] = exchange_start((land,), _ag_mine, _ag_mine, f"ag_{n}{l}_start")
    all_started = sum(s[2] for s in started.values())
    n_started = len(started)

    def get_w(n, l, after):
        behind = all_started if len(started) == n_started else after
        sems, bufs, _ = started.pop((n, l))
        _, _, _, h, c = bufs[0].shape
        (land,) = exchange_wait(sems, bufs, _ag_mine, _ag_got, behind, f"ag_{n}{l}_wait")
        full = all_gather_finish(land, f"ag_{n}{l}_finish").reshape(N_CHIPS, 1, 2 * h, c)
        if n == "mla_w_down":
            full = jnp.pad(full, ((0, 0), (0, 0), (0, 0), (0, pe_pad)))
        if n == "mla_w_uq":
            full = _pad_heads(full, QK_DIM, QK_PAD)
        return full

    pending = []

    def on_grad(n, l, g4):
        if n == "mla_w_down":
            g4 = g4[..., :g4.shape[-1] - pe_pad]
        if n == "mla_w_uq":
            g4 = _unpad_heads(g4, QK_DIM, QK_PAD)
        after = pending[-1][2][2] if pending else g4
        pending.append((n, l, reduce_scatter_start(g4, core, after, f"rs_{n}{l}")))

    loss, grad_x, gsmall = local_step(x[0], loss_target[0], {n: w[n] for n in SMALL}, get_w, on_grad)
    loss = lax.psum(loss, ("x", "y", "c"))

    outs = {n: None for n in BIG}
    for n, l, pend in pending:
        g = reduce_scatter_finish(pend, place, grad_x, f"rs_{n}{l}")
        outs[n] = adamw(w[n], g, m[n], v[n], l, outs[n], f"adamw_{n}{l}")
    small = _unpack_small(all_reduce_small(_pack_small(gsmall), "ar_small"), gsmall)
    for n in SMALL:
        outs[n] = tuple(o[0] for o in adamw(w[n][None], small[n], m[n][None], v[n][None], 0, None, f"adamw_{n}"))

    return (loss, grad_x[None], *[outs[n][0] for n in WEIGHTS], *[outs[n][1] for n in WEIGHTS],
            *[outs[n][2] for n in WEIGHTS], *[outs[n][3] for n in WEIGHTS])
```

```python
import math

import numpy as np
import jax
import jax.numpy as jnp
from jax import lax
from jax.experimental import pallas as pl
from jax.experimental.pallas import tpu as pltpu

MXU_DTYPE = jnp.bfloat16
WIRE_DTYPE = jnp.bfloat16
EPS = 1e-6
NEG = -1e30
N_CHIPS = 4
MESH = pl.DeviceIdType.MESH
ANY = pl.BlockSpec(memory_space=pl.ANY)
LANES = 128

MLA_HEADS = 16
NOPE_DIM = 128
ROPE_DIM = 64
QK_DIM = NOPE_DIM + ROPE_DIM
QK_PAD = 2 * LANES
PREP_ROWS = 1024
ATTN_BLOCK = 1024
AG_AHEAD = 2
RS_WINDOW = 2
ROPE_THETA = 10000.0
DIL_PAIRS = ((128, 1), (512, 4), (2048, 16))
DIL_HEADS = 8
BLK = 128

ADAM_LR = 0.001
ADAM_B1 = 0.9
ADAM_B2 = 0.999
ADAM_EPS = 1e-08
ADAM_WD = 0.01
ADAM_STEP = 10

NT = (((1,), (1,)), ((), ()))
TN = (((0,), (0,)), ((), ()))

SHARD_AXIS = {"ffn1_w_in": 1, "ffn1_w_out": 0, "ffn2_w_in": 1, "ffn2_w_out": 0, "mla_w_down": 0, "mla_w_uq": 1,
              "mla_w_ukv": 1, "mla_w_o": 0, "dil_w_qkv": 1, "dil_w_o": 1}
BIG = tuple(SHARD_AXIS)
USE_ORDER = (("ffn1_w_in", "ffn1_w_out", "mla_w_down", "mla_w_uq", "mla_w_ukv", "mla_w_o", "ffn2_w_in", "ffn2_w_out"),
             ("ffn1_w_in", "ffn1_w_out", "dil_w_qkv", "dil_w_o", "ffn2_w_in", "ffn2_w_out"))
SMALL = ("ffn1_norm", "mix_norm", "ffn2_norm", "mla_g_cq", "mla_g_ckv", "mla_g_qn", "mla_g_kn", "dil_g_qn", "dil_g_kn")
WEIGHTS = ("ffn1_norm", "ffn1_w_in", "ffn1_w_out", "mix_norm", "ffn2_norm", "ffn2_w_in", "ffn2_w_out", "mla_w_down",
           "mla_g_cq", "mla_g_ckv", "mla_w_uq", "mla_w_ukv", "mla_g_qn", "mla_g_kn", "mla_w_o", "dil_w_qkv", "dil_g_qn",
           "dil_g_kn", "dil_w_o")


def _tile(dim, pref, mult=LANES):
    if dim <= pref:
        return dim
    t = (pref // mult) * mult
    while t >= mult:
        if dim % t == 0:
            return t
        t -= mult
    return dim


def _params(*sem):
    return pltpu.CompilerParams(dimension_semantics=sem)


def _f32(shape):
    return jax.ShapeDtypeStruct(shape, jnp.float32)


def _act(shape):
    return jax.ShapeDtypeStruct(shape, MXU_DTYPE)


class Sharded:
    def __init__(self, arr, layer, axis):
        self.arr, self.layer, self.axis = arr, layer, axis
        n, _, r, c = arr.shape
        self.shape = (n * r, c) if axis == 0 else (r, n * c)
        self.per = r if axis == 0 else c

    def spec(self, tr, tc, rc_of):
        l = self.layer
        if self.axis == 0:
            n = self.per // tr

            def imap(*g):
                bi, bj = rc_of(*g)
                return (bi // n, l, bi % n, bj)
        else:
            n = self.per // tc

            def imap(*g):
                bi, bj = rc_of(*g)
                return (bj // n, l, bi, bj % n)
        return pl.BlockSpec((None, None, tr, tc), imap)


def _spec2(tr, tc, rc_of):
    return pl.BlockSpec((tr, tc), lambda *g: rc_of(*g))


def matmul(a, b, *, ta=False, tb=False, out_dtype=jnp.float32, scale=None, res=None, out_axis=None,
           name, tm=1024, tn=1024, tk=512):
    am, ak = (a.shape[1], a.shape[0]) if ta else a.shape
    bk, bn = (b.shape[1], b.shape[0]) if tb else b.shape
    assert ak == bk, (name, a.shape, b.shape, ta, tb)
    M, N, K = am, bn, ak

    def per(x, axis):
        return x.per if isinstance(x, Sharded) and x.axis == axis else None

    def pick(dim, pref, *pers):
        return _tile(math.gcd(dim, *[p for p in pers if p is not None]), pref)

    tm = pick(M, tm, per(a, 1 if ta else 0), M // N_CHIPS if out_axis == 0 else None)
    tn = pick(N, tn, per(b, 0 if tb else 1), N // N_CHIPS if out_axis == 1 else None)
    tk = pick(K, tk, per(a, 0 if ta else 1), per(b, 1 if tb else 0))
    assert M % tm == 0 and N % tn == 0 and K % tk == 0, (name, M, N, K, tm, tn, tk)
    nk = K // tk

    a_rc = (lambda i, j, k: (k, i)) if ta else (lambda i, j, k: (i, k))
    b_rc = (lambda i, j, k: (j, k)) if tb else (lambda i, j, k: (k, j))
    a_blk = (tk, tm) if ta else (tm, tk)
    b_blk = (tn, tk) if tb else (tk, tn)
    a_spec = a.spec(*a_blk, a_rc) if isinstance(a, Sharded) else _spec2(*a_blk, a_rc)
    b_spec = b.spec(*b_blk, b_rc) if isinstance(b, Sharded) else _spec2(*b_blk, b_rc)
    dn = (((0 if ta else 1,), (1 if tb else 0,)), ((), ()))
    has_res = res is not None

    def body(*refs):
        if has_res:
            a_ref, b_ref, r_ref, o_ref, acc = refs
        else:
            a_ref, b_ref, o_ref, acc = refs
        k = pl.program_id(2)

        @pl.when(k == 0)
        def _():
            acc[...] = jnp.zeros_like(acc)

        acc[...] += lax.dot_general(a_ref[...].astype(MXU_DTYPE), b_ref[...].astype(MXU_DTYPE), dn,
                                    preferred_element_type=jnp.float32)

        @pl.when(k == nk - 1)
        def _():
            r = acc[...]
            if scale is not None:
                r = r * scale
            if has_res:
                r = r + r_ref[...]
            o_ref[...] = r.astype(o_ref.dtype)

    in_specs = [a_spec, b_spec]
    args = [a.arr if isinstance(a, Sharded) else a, b.arr if isinstance(b, Sharded) else b]
    if has_res:
        in_specs.append(_spec2(tm, tn, lambda i, j, k: (i, j)))
        args.append(res)
    o_rc = lambda i, j, k: (i, j)
    if out_axis is None:
        out_shape = jax.ShapeDtypeStruct((M, N), out_dtype)
        out_spec = _spec2(tm, tn, o_rc)
    else:
        shp = (N_CHIPS, 1, M // N_CHIPS, N) if out_axis == 0 else (N_CHIPS, 1, M, N // N_CHIPS)
        out_shape = jax.ShapeDtypeStruct(shp, out_dtype)
        out_spec = Sharded(out_shape, 0, out_axis).spec(tm, tn, o_rc)
    return pl.pallas_call(
        body, name=name, out_shape=out_shape, grid=(M // tm, N // tn, nk),
        in_specs=in_specs, out_specs=out_spec,
        scratch_shapes=[pltpu.VMEM((tm, tn), jnp.float32)],
        compiler_params=_params("parallel", "parallel", "arbitrary"),
    )(*args)


def _me():
    return lax.axis_index("x"), lax.axis_index("y"), lax.axis_index("c")


def _other_chips(x, y):
    return [(1 - x, y), (x, 1 - y), (1 - x, 1 - y)]


HBM = pl.BlockSpec(memory_space=pltpu.HBM)
SEM = pl.BlockSpec(memory_space=pltpu.SEMAPHORE)
N_PEERS = 3
TOKEN = jax.ShapeDtypeStruct((8, LANES), jnp.float32)


def _split_params():
    return pltpu.CompilerParams(has_side_effects=pltpu.SideEffectType.DATAFLOW_SIDE_EFFECTING)


def _in_hbm(a):
    return pltpu.with_memory_space_constraint(a, pltpu.HBM)


def exchange_start(bufs, src_of, dst_of, name):
    nb = len(bufs)

    def body(*refs):
        src_ref, land_ref = refs[0], refs[nb - 1]
        sems, token = refs[nb:nb + 2 * N_PEERS], refs[-1]
        x, y, cc = _me()
        for j, (px, py) in enumerate(_other_chips(x, y)):
            pltpu.make_async_remote_copy(
                src_ref=src_of(src_ref, j, (px, py), (x, y, cc)), dst_ref=dst_of(land_ref, j, (px, py), (x, y, cc)),
                send_sem=sems[j], recv_sem=sems[N_PEERS + j], device_id=(px, py, cc), device_id_type=MESH).start()
        token[...] = jnp.zeros_like(token)

    outs = pl.pallas_call(
        body, name=name,
        out_shape=(pltpu.SemaphoreType.DMA(()),) * (2 * N_PEERS) + tuple(pltpu.HBM(b.shape, b.dtype) for b in bufs) + (TOKEN,),
        in_specs=(HBM,) * nb, out_specs=(SEM,) * (2 * N_PEERS) + (HBM,) * nb + (pl.BlockSpec(memory_space=pltpu.VMEM),),
        input_output_aliases={k: 2 * N_PEERS + k for k in range(nb)}, compiler_params=_split_params(),
    )(*[_in_hbm(b) for b in bufs])
    return outs[:2 * N_PEERS], outs[2 * N_PEERS:2 * N_PEERS + nb], outs[-1]


def exchange_wait(sems, bufs, src_of, got_of, after, name):
    nb = len(bufs)

    def body(*refs):
        src_ref, land_ref = refs[0], refs[nb - 1]
        sems_ = refs[nb:nb + 2 * N_PEERS]
        x, y, cc = _me()
        for j, (px, py) in enumerate(_other_chips(x, y)):
            cp = pltpu.make_async_remote_copy(
                src_ref=src_of(src_ref, j, (px, py), (x, y, cc)), dst_ref=got_of(land_ref, j, (px, py), (x, y, cc)),
                send_sem=sems_[j], recv_sem=sems_[N_PEERS + j], device_id=(px, py, cc), device_id_type=MESH)
            cp.wait_send()
            cp.wait_recv()

    return pl.pallas_call(
        body, name=name, out_shape=tuple(pltpu.HBM(b.shape, b.dtype) for b in bufs),
        in_specs=(HBM,) * nb + (SEM,) * (2 * N_PEERS) + (ANY,), out_specs=(HBM,) * nb,
        input_output_aliases={k: k for k in range(nb)}, compiler_params=_split_params(),
    )(*bufs, *sems, after)


def cast_into_shards(w, layer, shard, after, name):
    L, r, c = w.shape
    tr, tc = _tile(r, 512, 16), _tile(c, 1024)

    def body(shard_ref, w_ref, after_ref, o_ref):
        o_ref[...] = w_ref[...].astype(o_ref.dtype)

    grid_spec = pltpu.PrefetchScalarGridSpec(
        num_scalar_prefetch=1, grid=(r // tr, c // tc),
        in_specs=[pl.BlockSpec((None, tr, tc), lambda i, j, sh: (layer, i, j)), ANY],
        out_specs=pl.BlockSpec((None, None, tr, tc), lambda i, j, sh: (sh[0], 0, i, j)))
    return pl.pallas_call(body, name=name, grid_spec=grid_spec, out_shape=jax.ShapeDtypeStruct((N_CHIPS, 1, r, c), WIRE_DTYPE),
                          compiler_params=_params("parallel", "parallel"))(shard, w, after)


def _ag_mine(ref, j, chip, me):
    return ref.at[2 * me[0] + me[1], :, me[2]]


def _ag_got(ref, j, chip, me):
    return ref.at[2 * chip[0] + chip[1], :, me[2]]


def all_gather_finish(land, name):
    def body(land_ref, o_ref, send_sems, recv_sems):
        x, y, cc = _me()
        cps = []
        for j, (px, py) in enumerate(_other_chips(x, y)):
            cp = pltpu.make_async_remote_copy(
                src_ref=o_ref.at[2 * px + py, :, cc], dst_ref=o_ref.at[2 * px + py, :, cc], send_sem=send_sems.at[j],
                recv_sem=recv_sems.at[j], device_id=(x, y, 1 - cc), device_id_type=MESH)
            cp.start()
            cps.append(cp)
        for j, (px, py) in enumerate(_other_chips(x, y)):
            got = o_ref.at[2 * px + py, :, 1 - cc]
            pltpu.make_async_remote_copy(src_ref=got, dst_ref=got, send_sem=send_sems.at[j], recv_sem=recv_sems.at[j],
                                         device_id=(x, y, 1 - cc), device_id_type=MESH).wait_recv()
        for cp in cps:
            cp.wait_send()

    return pl.pallas_call(
        body, name=name, out_shape=jax.ShapeDtypeStruct(land.shape, land.dtype), in_specs=[ANY], out_specs=ANY,
        input_output_aliases={0: 0},
        scratch_shapes=[pltpu.SemaphoreType.DMA((N_PEERS,)), pltpu.SemaphoreType.DMA((N_PEERS,))],
    )(land)


def _rs_src(ref, j, chip, me):
    return ref.at[2 * chip[0] + chip[1]]


def _rs_dst(ref, j, chip, me):
    return ref.at[j]


def sibling_send_halves(g, after, name):
    n, L, two, h, c = g.shape

    def body(g_ref, after_ref, o_ref, send_sem, recv_sem):
        x, y, cc = _me()
        cp = pltpu.make_async_remote_copy(src_ref=g_ref.at[:, :, 1 - cc], dst_ref=o_ref, send_sem=send_sem,
                                          recv_sem=recv_sem, device_id=(x, y, 1 - cc), device_id_type=MESH)
        cp.start()
        cp.wait()

    return pl.pallas_call(
        body, name=name, out_shape=jax.ShapeDtypeStruct((n, L, h, c), g.dtype),
        in_specs=[ANY, ANY], out_specs=ANY,
        scratch_shapes=[pltpu.SemaphoreType.DMA, pltpu.SemaphoreType.DMA],
    )(g, after)


def sibling_gather_halves(r, name):
    def body(r_ref, o_ref, send_sem, recv_sem):
        x, y, cc = _me()
        cp = pltpu.make_async_remote_copy(src_ref=o_ref.at[:, cc], dst_ref=o_ref.at[:, cc], send_sem=send_sem,
                                          recv_sem=recv_sem, device_id=(x, y, 1 - cc), device_id_type=MESH)
        cp.start()
        cp.wait()

    return pl.pallas_call(
        body, name=name, out_shape=jax.ShapeDtypeStruct(r.shape, r.dtype), in_specs=[ANY], out_specs=ANY,
        input_output_aliases={0: 0}, scratch_shapes=[pltpu.SemaphoreType.DMA, pltpu.SemaphoreType.DMA],
    )(r)


def add_sibling(g, r1, core, name):
    n, L, two, h, c = g.shape
    th = _tile(h, 512, 16)
    tc = _tile(c, 1024)

    def body(core_ref, g_ref, r_ref, o_ref):
        o_ref[...] = (g_ref[...].astype(jnp.float32) + r_ref[...].astype(jnp.float32)).astype(o_ref.dtype)

    grid_spec = pltpu.PrefetchScalarGridSpec(
        num_scalar_prefetch=1, grid=(n, L, h // th, c // tc),
        in_specs=[pl.BlockSpec((None, None, None, th, tc), lambda s, l, i, j, core: (s, l, core[0], i, j)),
                  pl.BlockSpec((None, None, th, tc), lambda s, l, i, j, core: (s, l, i, j))],
        out_specs=pl.BlockSpec((None, None, th, tc), lambda s, l, i, j, core: (s, l, i, j)))
    return pl.pallas_call(body, name=name, grid_spec=grid_spec, out_shape=jax.ShapeDtypeStruct((n, L, h, c), WIRE_DTYPE),
                          compiler_params=_params("parallel", "parallel", "parallel", "parallel"))(core, g, r1)


def add_chips(p, r2, place, name):
    n, L, h, c = p.shape
    th = _tile(h, 512, 16)
    tc = _tile(c, 1024)

    def body(shard_ref, core_ref, p_ref, r_ref, o_ref):
        acc = p_ref[...].astype(jnp.float32)
        for j in range(3):
            acc = acc + r_ref[j].astype(jnp.float32)
        o_ref[...] = acc

    grid_spec = pltpu.PrefetchScalarGridSpec(
        num_scalar_prefetch=2, grid=(L, h // th, c // tc),
        in_specs=[pl.BlockSpec((None, None, th, tc), lambda l, i, j, shard, core: (shard[0], l, i, j)),
                  pl.BlockSpec((3, None, th, tc), lambda l, i, j, shard, core: (0, l, i, j))],
        out_specs=pl.BlockSpec((None, None, th, tc), lambda l, i, j, shard, core: (l, core[0], i, j)))
    return pl.pallas_call(body, name=name, grid_spec=grid_spec, out_shape=_f32((L, 2, h, c)),
                          compiler_params=_params("parallel", "parallel", "parallel"))(*place, p, r2)


def reduce_scatter_start(g4, core, after, name):
    n, L, r, c = g4.shape
    g = g4.reshape(n, L, 2, r // 2, c)
    r1 = sibling_send_halves(g, after, name + "_d2d")
    p = add_sibling(g, r1, core, name + "_add1")
    land = lax.empty((N_PEERS, L, r // 2, c), p.dtype)
    return exchange_start((p, land), _rs_src, _rs_dst, name + "_ici_start")


def reduce_scatter_finish(pending, place, after, name):
    sems, bufs, _ = pending
    p, r2 = exchange_wait(sems, bufs, _rs_src, _rs_dst, after, name + "_ici_wait")
    red = add_chips(p, r2, place, name + "_add2")
    L, two, h, c = red.shape
    return sibling_gather_halves(red, name + "_gather").reshape(2 * h, c)


def all_reduce_small(v, name):
    R, C = v.shape

    def body(v_ref, o_ref, buf, send_sems, recv_sems):
        x, y, cc = _me()
        buf[0] = v_ref[...]
        cps = []
        for k in range(1, 8):
            dx, dy, dc = (k >> 2) & 1, (k >> 1) & 1, k & 1
            to = (x ^ dx, y ^ dy, cc ^ dc)
            cp = pltpu.make_async_remote_copy(src_ref=v_ref, dst_ref=buf.at[k], send_sem=send_sems.at[k],
                                              recv_sem=recv_sems.at[k], device_id=to, device_id_type=MESH)
            cp.start()
            cps.append(cp)
        for cp in cps:
            cp.wait()
        me = 4 * x + 2 * y + cc
        acc = buf[me]
        for a in range(1, 8):
            acc = acc + buf[a ^ me]
        o_ref[...] = acc

    vm = pl.BlockSpec(memory_space=pltpu.VMEM)
    return pl.pallas_call(
        body, name=name, out_shape=_f32((R, C)), in_specs=[vm], out_specs=vm,
        scratch_shapes=[pltpu.VMEM((8, R, C), jnp.float32), pltpu.SemaphoreType.DMA((8,)), pltpu.SemaphoreType.DMA((8,))],
    )(v)


def _rstd(x, n):
    return lax.rsqrt(jnp.sum(x * x, axis=-1, keepdims=True) * (1.0 / n) + EPS)


def _accumulate(ref, part, first):
    @pl.when(first)
    def _():
        ref[...] = part

    @pl.when(jnp.logical_not(first))
    def _():
        ref[...] += part


def rmsnorm_fwd(x, g, name):
    S, D = x.shape
    ts = _tile(S, 256, 8)

    def body(x_ref, g_ref, o_ref):
        xv = x_ref[...]
        o_ref[...] = (xv * _rstd(xv, D) * g_ref[...]).astype(o_ref.dtype)

    return pl.pallas_call(
        body, name=name, out_shape=_act((S, D)), grid=(S // ts,),
        in_specs=[pl.BlockSpec((ts, D), lambda i: (i, 0)), pl.BlockSpec((1, D), lambda i: (0, 0))],
        out_specs=pl.BlockSpec((ts, D), lambda i: (i, 0)), compiler_params=_params("parallel"))(x, g)


def _norm_bwd(x, g, dy, n):
    r = _rstd(x, n)
    xh = x * r
    dxh = dy * g
    dx = r * (dxh - xh * (jnp.sum(dxh * xh, axis=-1, keepdims=True) * (1.0 / n)))
    return dx, dy * xh


def rmsnorm_bwd(x, g, dy, dres, name):
    S, D = x.shape
    ts = _tile(S, 256, 8)

    def body(x_ref, g_ref, dy_ref, dres_ref, dx_ref, dg_ref):
        dx, dgp = _norm_bwd(x_ref[...], g_ref[...], dy_ref[...], D)
        dx_ref[...] = dres_ref[...] + dx
        _accumulate(dg_ref, jnp.sum(dgp, axis=0, keepdims=True), pl.program_id(0) == 0)

    row = pl.BlockSpec((ts, D), lambda i: (i, 0))
    vec = pl.BlockSpec((1, D), lambda i: (0, 0))
    return pl.pallas_call(
        body, name=name, out_shape=(_f32((S, D)), _f32((1, D))), grid=(S // ts,),
        in_specs=[row, vec, row, row], out_specs=(row, vec), compiler_params=_params("arbitrary"))(x, g, dy, dres)


def _sigmoid(x):
    return 1.0 / (1.0 + jnp.exp(-x))


def swiglu_fwd(u, name):
    S, F2 = u.shape
    F = F2 // 2
    ts, tf = _tile(S, 512, 8), _tile(F, 512)
    nf = F // tf

    def body(g_ref, u_ref, o_ref):
        gt = g_ref[...]
        o_ref[...] = (gt * _sigmoid(gt) * u_ref[...]).astype(o_ref.dtype)

    return pl.pallas_call(
        body, name=name, out_shape=_act((S, F)), grid=(S // ts, nf),
        in_specs=[pl.BlockSpec((ts, tf), lambda i, j: (i, j)), pl.BlockSpec((ts, tf), lambda i, j: (i, j + nf))],
        out_specs=pl.BlockSpec((ts, tf), lambda i, j: (i, j)), compiler_params=_params("parallel", "parallel"))(u, u)


def swiglu_bwd(u, da, name):
    S, F2 = u.shape
    F = F2 // 2
    ts, tf = _tile(S, 512, 8), _tile(F, 512)
    nf = F // tf

    def body(g_ref, u_ref, da_ref, o_ref):
        gt, up, d = g_ref[...], u_ref[...], da_ref[...]
        s = _sigmoid(gt)
        o_ref[0] = (d * up * (s * (1.0 + gt * (1.0 - s)))).astype(o_ref.dtype)
        o_ref[1] = (d * (gt * s)).astype(o_ref.dtype)

    blk = pl.BlockSpec((ts, tf), lambda i, j: (i, j))
    return pl.pallas_call(
        body, name=name, out_shape=_act((2, 1, S, F)), grid=(S // ts, nf),
        in_specs=[blk, pl.BlockSpec((ts, tf), lambda i, j: (i, j + nf)), blk],
        out_specs=pl.BlockSpec((2, None, ts, tf), lambda i, j: (0, 0, i, j)),
        compiler_params=_params("parallel", "parallel"))(u, u, da)


def loss_head(y, t, name):
    S, D = y.shape
    ts = _tile(S, 256, 8)

    def body(y_ref, t_ref, dy_ref, l_ref):
        e = y_ref[...] - t_ref[...]
        dy_ref[...] = e * (1.0 / D)
        l_ref[...] = jnp.full(l_ref.shape, 0.5 * jnp.sum(jnp.sum(e * e, axis=-1, keepdims=True) * (1.0 / D)), jnp.float32)

    row = pl.BlockSpec((ts, D), lambda i: (i, 0))
    dy, parts = pl.pallas_call(
        body, name=name, out_shape=(_f32((S, D)), _f32((S // ts, 8, LANES))), grid=(S // ts,),
        in_specs=[row, row], out_specs=(row, pl.BlockSpec((None, 8, LANES), lambda i: (i, 0, 0))),
        compiler_params=_params("parallel"))(y, t)
    return jnp.sum(parts[:, 0, 0]), dy


def rope_tables(S):
    inv = 1.0 / (ROPE_THETA ** (jnp.arange(0, ROPE_DIM, 2, dtype=jnp.float32) / ROPE_DIM))
    ang = jnp.arange(S, dtype=jnp.float32)[:, None] * inv[None, :]
    c, s = jnp.cos(ang), jnp.sin(ang)
    z = jnp.zeros_like(c)
    return (jnp.concatenate([c, c, z, z], axis=1), jnp.concatenate([-s, z, z, z], axis=1),
            jnp.concatenate([z, s, z, z], axis=1))


def _rope(x, cos, sa, sb):
    return x * cos + pltpu.roll(x, 96, 1) * sa + pltpu.roll(x, 32, 1) * sb


def _rope_t(d, cos, sa, sb):
    return d * cos + pltpu.roll(d * sa, 32, 1) + pltpu.roll(d * sb, 96, 1)


def _head_norm(x1, x2, g):
    r = lax.rsqrt((jnp.sum(x1 * x1, axis=-1, keepdims=True) + jnp.sum(x2 * x2, axis=-1, keepdims=True)) * (1.0 / QK_DIM) + EPS)
    return x1 * r * g[:, :LANES], x2 * r * g[:, LANES:], r


def _head_norm_bwd(x1, x2, g, d1, d2):
    _, _, r = _head_norm(x1, x2, g)
    h1, h2 = x1 * r, x2 * r
    e1, e2 = d1 * g[:, :LANES], d2 * g[:, LANES:]
    m = (jnp.sum(e1 * h1, axis=-1, keepdims=True) + jnp.sum(e2 * h2, axis=-1, keepdims=True)) * (1.0 / QK_DIM)
    return r * (e1 - h1 * m), r * (e2 - h2 * m), d1 * h1, d2 * h2


def mla_latent_fwd(lat, g_cq, g_ckv, name):
    S, W = lat.shape
    QL, KL = g_cq.shape[1], g_ckv.shape[1]
    ts = _tile(S, 256, 8)

    def body(l_ref, gq_ref, gk_ref, cq_ref, ckv_ref):
        a, b = l_ref[:, :QL], l_ref[:, QL:QL + KL]
        cq_ref[...] = (a * _rstd(a, QL) * gq_ref[...]).astype(cq_ref.dtype)
        ckv_ref[...] = (b * _rstd(b, KL) * gk_ref[...]).astype(ckv_ref.dtype)

    return pl.pallas_call(
        body, name=name, out_shape=(_act((S, QL)), _act((S, KL))), grid=(S // ts,),
        in_specs=[pl.BlockSpec((ts, W), lambda i: (i, 0)), pl.BlockSpec((1, QL), lambda i: (0, 0)),
                  pl.BlockSpec((1, KL), lambda i: (0, 0))],
        out_specs=(pl.BlockSpec((ts, QL), lambda i: (i, 0)), pl.BlockSpec((ts, KL), lambda i: (i, 0))),
        compiler_params=_params("parallel"))(lat, g_cq, g_ckv)


def mla_latent_bwd(dcq, dckv, dkpe, lat, g_cq, g_ckv, name):
    S, W = lat.shape
    QL, KL = g_cq.shape[1], g_ckv.shape[1]
    ts = _tile(S, 256, 8)

    def body(dq_ref, dk_ref, dp_ref, l_ref, gq_ref, gk_ref, o_ref, dgq_ref, dgk_ref):
        first = pl.program_id(0) == 0
        da, ga = _norm_bwd(l_ref[:, :QL], gq_ref[...], dq_ref[...], QL)
        db, gb = _norm_bwd(l_ref[:, QL:QL + KL], gk_ref[...], dk_ref[...], KL)
        o_ref[:, :QL] = da.astype(o_ref.dtype)
        o_ref[:, QL:QL + KL] = db.astype(o_ref.dtype)
        o_ref[:, QL + KL:] = dp_ref[...].astype(o_ref.dtype)
        _accumulate(dgq_ref, jnp.sum(ga, axis=0, keepdims=True), first)
        _accumulate(dgk_ref, jnp.sum(gb, axis=0, keepdims=True), first)

    row = lambda n: pl.BlockSpec((ts, n), lambda i: (i, 0))
    vec = lambda n: pl.BlockSpec((1, n), lambda i: (0, 0))
    return pl.pallas_call(
        body, name=name, out_shape=(_act((S, W)), _f32((1, QL)), _f32((1, KL))), grid=(S // ts,),
        in_specs=[row(QL), row(KL), row(LANES), row(W), vec(QL), vec(KL)], out_specs=(row(W), vec(QL), vec(KL)),
        compiler_params=_params("arbitrary"))(dcq, dckv, dkpe, lat, g_cq, g_ckv)


def mla_q_prep_fwd(qraw, g, tabs, H, name):
    S = qraw.shape[0]
    ts = _tile(S, PREP_ROWS, 8)

    def body(x_ref, g_ref, c_ref, a_ref, b_ref, o_ref):
        y1, y2, _ = _head_norm(x_ref[:, :LANES], x_ref[:, LANES:], g_ref[...])
        o_ref[:, :LANES] = y1.astype(o_ref.dtype)
        o_ref[:, LANES:] = _rope(y2, c_ref[...], a_ref[...], b_ref[...]).astype(o_ref.dtype)

    tab = pl.BlockSpec((ts, LANES), lambda i, h: (i, 0))
    return pl.pallas_call(
        body, name=name, out_shape=_act((H, S, QK_PAD)), grid=(S // ts, H),
        in_specs=[pl.BlockSpec((ts, QK_PAD), lambda i, h: (i, h)), pl.BlockSpec((1, QK_PAD), lambda i, h: (0, 0)), tab, tab, tab],
        out_specs=pl.BlockSpec((None, ts, QK_PAD), lambda i, h: (h, i, 0)),
        compiler_params=_params("parallel", "parallel"))(qraw, g, *tabs)


def mla_q_prep_bwd(dq, qraw, g, tabs, H, name):
    S = qraw.shape[0]
    ts = _tile(S, PREP_ROWS, 8)

    def body(d_ref, x_ref, g_ref, c_ref, a_ref, b_ref, o_ref, dg_ref):
        d2 = _rope_t(d_ref[:, LANES:], c_ref[...], a_ref[...], b_ref[...])
        dx1, dx2, g1, g2 = _head_norm_bwd(x_ref[:, :LANES], x_ref[:, LANES:], g_ref[...], d_ref[:, :LANES], d2)
        o_ref[:, :LANES] = dx1.astype(o_ref.dtype)
        o_ref[:, LANES:] = dx2.astype(o_ref.dtype)
        first = jnp.logical_and(pl.program_id(0) == 0, pl.program_id(1) == 0)
        part = jnp.concatenate([jnp.sum(g1, axis=0, keepdims=True), jnp.sum(g2, axis=0, keepdims=True)], axis=1)
        _accumulate(dg_ref, part, first)

    tab = pl.BlockSpec((ts, LANES), lambda i, h: (i, 0))
    vec = pl.BlockSpec((1, QK_PAD), lambda i, h: (0, 0))
    return pl.pallas_call(
        body, name=name, out_shape=(_act((S, H * QK_PAD)), _f32((1, QK_PAD))), grid=(S // ts, H),
        in_specs=[pl.BlockSpec((None, ts, QK_PAD), lambda i, h: (h, i, 0)), pl.BlockSpec((ts, QK_PAD), lambda i, h: (i, h)),
                  vec, tab, tab, tab],
        out_specs=(pl.BlockSpec((ts, QK_PAD), lambda i, h: (i, h)), vec),
        compiler_params=_params("arbitrary", "arbitrary"))(dq, qraw, g, *tabs)


def mla_k_prep_fwd(kvraw, lat, g, tabs, H, pe_blk, name):
    S = kvraw.shape[0]
    ts = _tile(S, PREP_ROWS, 8)

    def body(x_ref, p_ref, g_ref, c_ref, a_ref, b_ref, k_ref, v_ref):
        y1, y2, _ = _head_norm(x_ref[:, :LANES], p_ref[...], g_ref[...])
        k_ref[:, :LANES] = y1.astype(k_ref.dtype)
        k_ref[:, LANES:] = _rope(y2, c_ref[...], a_ref[...], b_ref[...]).astype(k_ref.dtype)
        v_ref[...] = x_ref[:, LANES:].astype(v_ref.dtype)

    tab = pl.BlockSpec((ts, LANES), lambda i, h: (i, 0))
    return pl.pallas_call(
        body, name=name, out_shape=(_act((H, S, QK_PAD)), _act((H, S, LANES))), grid=(S // ts, H),
        in_specs=[pl.BlockSpec((ts, QK_PAD), lambda i, h: (i, h)), pl.BlockSpec((ts, LANES), lambda i, h: (i, pe_blk)),
                  pl.BlockSpec((1, QK_PAD), lambda i, h: (0, 0)), tab, tab, tab],
        out_specs=(pl.BlockSpec((None, ts, QK_PAD), lambda i, h: (h, i, 0)), pl.BlockSpec((None, ts, LANES), lambda i, h: (h, i, 0))),
        compiler_params=_params("parallel", "parallel"))(kvraw, lat, g, *tabs)


def mla_k_prep_bwd(dk, dv, kvraw, lat, g, tabs, H, pe_blk, name):
    S = kvraw.shape[0]
    ts = _tile(S, PREP_ROWS, 8)

    def body(dk_ref, dv_ref, x_ref, p_ref, g_ref, c_ref, a_ref, b_ref, o_ref, dp_ref, dg_ref):
        i, h = pl.program_id(0), pl.program_id(1)
        d2 = _rope_t(dk_ref[:, LANES:], c_ref[...], a_ref[...], b_ref[...])
        dx1, dx2, g1, g2 = _head_norm_bwd(x_ref[:, :LANES], p_ref[...], g_ref[...], dk_ref[:, :LANES], d2)
        o_ref[:, :LANES] = dx1.astype(o_ref.dtype)
        o_ref[:, LANES:] = dv_ref[...].astype(o_ref.dtype)
        _accumulate(dp_ref, dx2, h == 0)
        part = jnp.concatenate([jnp.sum(g1, axis=0, keepdims=True), jnp.sum(g2, axis=0, keepdims=True)], axis=1)
        _accumulate(dg_ref, part, jnp.logical_and(i == 0, h == 0))

    tab = pl.BlockSpec((ts, LANES), lambda i, h: (i, 0))
    vec = pl.BlockSpec((1, QK_PAD), lambda i, h: (0, 0))
    return pl.pallas_call(
        body, name=name, out_shape=(_act((S, H * QK_PAD)), _f32((S, LANES)), _f32((1, QK_PAD))), grid=(S // ts, H),
        in_specs=[pl.BlockSpec((None, ts, QK_PAD), lambda i, h: (h, i, 0)), pl.BlockSpec((None, ts, LANES), lambda i, h: (h, i, 0)),
                  pl.BlockSpec((ts, QK_PAD), lambda i, h: (i, h)), pl.BlockSpec((ts, LANES), lambda i, h: (i, pe_blk)),
                  vec, tab, tab, tab],
        out_specs=(pl.BlockSpec((ts, QK_PAD), lambda i, h: (i, h)), tab, vec),
        compiler_params=_params("arbitrary", "arbitrary"))(dk, dv, kvraw, lat, g, *tabs)


def _causal_scores(q, k, scale, diagonal):
    s = lax.dot_general(q, k, NT, preferred_element_type=jnp.float32) * scale
    if not diagonal:
        return s
    row = lax.broadcasted_iota(jnp.int32, s.shape, 0)
    col = lax.broadcasted_iota(jnp.int32, s.shape, 1)
    return jnp.where(col <= row, s, NEG)


def _on_causal_blocks(qi, ki, step):
    @pl.when(ki < qi)
    def _():
        step(False)

    @pl.when(ki == qi)
    def _():
        step(True)


def mla_attention_fwd(q, k, v, name):
    H, S, _ = q.shape
    t = _tile(S, ATTN_BLOCK)
    n = S // t
    scale = 1.0 / math.sqrt(QK_DIM)

    def body(q_ref, k_ref, v_ref, o_ref, lse_ref, m_sc, l_sc, acc):
        qi, ki = pl.program_id(1), pl.program_id(2)

        @pl.when(ki == 0)
        def _():
            m_sc[...] = jnp.full(m_sc.shape, NEG, jnp.float32)
            l_sc[...] = jnp.zeros_like(l_sc)
            acc[...] = jnp.zeros_like(acc)

        def step(diagonal):
            s = _causal_scores(q_ref[...], k_ref[...], scale, diagonal)
            m_new = jnp.maximum(m_sc[...], jnp.max(s, axis=-1, keepdims=True))
            alpha = jnp.exp(m_sc[...] - m_new)
            p = jnp.exp(s - m_new)
            l_sc[...] = alpha * l_sc[...] + jnp.sum(p, axis=-1, keepdims=True)
            acc[...] = alpha * acc[...] + jnp.dot(p.astype(MXU_DTYPE), v_ref[...], preferred_element_type=jnp.float32)
            m_sc[...] = m_new

        _on_causal_blocks(qi, ki, step)

        @pl.when(ki == qi)
        def _():
            o_ref[...] = (acc[...] / l_sc[...]).astype(o_ref.dtype)
            lse_ref[...] = m_sc[...] + jnp.log(l_sc[...])

    kv = lambda w: pl.BlockSpec((None, t, w), lambda h, qi, ki: (h, jnp.minimum(ki, qi), 0))
    return pl.pallas_call(
        body, name=name, out_shape=(_act((S, H * LANES)), _f32((H, S, 1))), grid=(H, n, n),
        in_specs=[pl.BlockSpec((None, t, QK_PAD), lambda h, qi, ki: (h, qi, 0)), kv(QK_PAD), kv(LANES)],
        out_specs=(pl.BlockSpec((t, LANES), lambda h, qi, ki: (qi, h)), pl.BlockSpec((None, t, 1), lambda h, qi, ki: (h, qi, 0))),
        scratch_shapes=[pltpu.VMEM((t, 1), jnp.float32), pltpu.VMEM((t, 1), jnp.float32), pltpu.VMEM((t, LANES), jnp.float32)],
        compiler_params=_params("parallel", "parallel", "arbitrary"))(q, k, v)


def attention_delta(do, o, H, name):
    S = do.shape[0]
    ts = _tile(S, 512, 8)

    def body(d_ref, o_ref, out_ref):
        out_ref[...] = jnp.sum(d_ref[...] * o_ref[...].astype(jnp.float32), axis=-1, keepdims=True)

    blk = pl.BlockSpec((ts, LANES), lambda i, h: (i, h))
    return pl.pallas_call(
        body, name=name, out_shape=_f32((H, S, 1)), grid=(S // ts, H), in_specs=[blk, blk],
        out_specs=pl.BlockSpec((None, ts, 1), lambda i, h: (h, i, 0)), compiler_params=_params("parallel", "parallel"))(do, o)


def mla_attention_bwd_dq(q, k, v, do, lse, delta, name):
    H, S, _ = q.shape
    t = _tile(S, ATTN_BLOCK)
    n = S // t
    scale = 1.0 / math.sqrt(QK_DIM)

    def body(q_ref, k_ref, v_ref, do_ref, lse_ref, dl_ref, dq_ref, acc):
        qi, ki = pl.program_id(1), pl.program_id(2)

        @pl.when(ki == 0)
        def _():
            acc[...] = jnp.zeros_like(acc)

        def step(diagonal):
            p = jnp.exp(_causal_scores(q_ref[...], k_ref[...], scale, diagonal) - lse_ref[...])
            dp = lax.dot_general(do_ref[...].astype(MXU_DTYPE), v_ref[...], NT, preferred_element_type=jnp.float32)
            ds = p * (dp - dl_ref[...])
            acc[...] += jnp.dot(ds.astype(MXU_DTYPE), k_ref[...], preferred_element_type=jnp.float32)

        _on_causal_blocks(qi, ki, step)

        @pl.when(ki == qi)
        def _():
            dq_ref[...] = acc[...] * scale

    kv = lambda w: pl.BlockSpec((None, t, w), lambda h, qi, ki: (h, jnp.minimum(ki, qi), 0))
    col = pl.BlockSpec((None, t, 1), lambda h, qi, ki: (h, qi, 0))
    qspec = pl.BlockSpec((None, t, QK_PAD), lambda h, qi, ki: (h, qi, 0))
    return pl.pallas_call(
        body, name=name, out_shape=_f32((H, S, QK_PAD)), grid=(H, n, n),
        in_specs=[qspec, kv(QK_PAD), kv(LANES), pl.BlockSpec((t, LANES), lambda h, qi, ki: (qi, h)), col, col],
        out_specs=qspec, scratch_shapes=[pltpu.VMEM((t, QK_PAD), jnp.float32)],
        compiler_params=_params("parallel", "parallel", "arbitrary"))(q, k, v, do, lse, delta)


def mla_attention_bwd_dkv(q, k, v, do, lse, delta, name):
    H, S, _ = q.shape
    t = _tile(S, ATTN_BLOCK)
    n = S // t
    scale = 1.0 / math.sqrt(QK_DIM)

    def body(q_ref, k_ref, v_ref, do_ref, lse_ref, dl_ref, dk_ref, dv_ref, dk_acc, dv_acc):
        ki, qi = pl.program_id(1), pl.program_id(2)

        @pl.when(qi == 0)
        def _():
            dk_acc[...] = jnp.zeros_like(dk_acc)
            dv_acc[...] = jnp.zeros_like(dv_acc)

        def step(diagonal):
            p = jnp.exp(_causal_scores(q_ref[...], k_ref[...], scale, diagonal) - lse_ref[...])
            dob = do_ref[...].astype(MXU_DTYPE)
            dv_acc[...] += lax.dot_general(p.astype(MXU_DTYPE), dob, TN, preferred_element_type=jnp.float32)
            dp = lax.dot_general(dob, v_ref[...], NT, preferred_element_type=jnp.float32)
            ds = p * (dp - dl_ref[...])
            dk_acc[...] += lax.dot_general(ds.astype(MXU_DTYPE), q_ref[...], TN, preferred_element_type=jnp.float32)

        _on_causal_blocks(qi, ki, step)

        @pl.when(qi == n - 1)
        def _():
            dk_ref[...] = dk_acc[...] * scale
            dv_ref[...] = dv_acc[...]

    qrow = lambda h, ki, qi: (h, jnp.maximum(qi, ki), 0)
    kv = lambda w: pl.BlockSpec((None, t, w), lambda h, ki, qi: (h, ki, 0))
    col = pl.BlockSpec((None, t, 1), qrow)
    return pl.pallas_call(
        body, name=name, out_shape=(_f32((H, S, QK_PAD)), _f32((H, S, LANES))), grid=(H, n, n),
        in_specs=[pl.BlockSpec((None, t, QK_PAD), qrow), kv(QK_PAD), kv(LANES),
                  pl.BlockSpec((t, LANES), lambda h, ki, qi: (jnp.maximum(qi, ki), h)), col, col],
        out_specs=(kv(QK_PAD), kv(LANES)),
        scratch_shapes=[pltpu.VMEM((t, QK_PAD), jnp.float32), pltpu.VMEM((t, LANES), jnp.float32)],
        compiler_params=_params("parallel", "parallel", "arbitrary"))(q, k, v, do, lse, delta)


def _alibi_slopes(G, Hd):
    k = np.arange(1, G * Hd + 1, dtype=np.float32)
    s = (2.0 ** (-8.0 * k / (G * Hd))).astype(np.float32).reshape(G, Hd)
    return jnp.asarray(np.broadcast_to(s[:, :, None, None], (G, Hd, 1, LANES)).copy())


def _dil_scores(qn, kn, scale, slope_d, prev, valid):
    s = lax.dot_general(qn, kn, NT, preferred_element_type=jnp.float32) * scale
    iq = lax.broadcasted_iota(jnp.int32, s.shape, 0)
    ik = lax.broadcasted_iota(jnp.int32, s.shape, 1)
    dist = iq - ik + (BLK if prev else 0)
    ok = (ik >= iq) if prev else (ik <= iq)
    s = s - slope_d * dist.astype(jnp.float32)
    return jnp.where(jnp.logical_and(ok, valid), s, NEG)


def _dil_specs(d, nb, Hd, G, g, ncol):
    def spec(kind, shift):
        col0 = (kind * G + g) * Hd

        def imap(r, n, h):
            return (jnp.clip(n + shift, 0, nb - 1), r * ncol + col0 + h)
        return pl.BlockSpec((BLK, LANES), imap)
    return spec


def dilated_fwd(qkv, gq, gk, slopes, g, d, Hd, G, name):
    S, C = qkv.shape
    ncol = C // LANES
    nb = S // d // BLK
    view = qkv.reshape(S // d, d * C)
    scale = 1.0 / math.sqrt(LANES)
    spec = _dil_specs(d, nb, Hd, G, g, ncol)

    def body(q_ref, kc_ref, kp_ref, vc_ref, vp_ref, gq_ref, gk_ref, sl_ref, o_ref, l_ref):
        n = pl.program_id(1)
        nrm = lambda t, gg: (t * _rstd(t, LANES) * gg).astype(MXU_DTYPE)
        qn = nrm(q_ref[...], gq_ref[...])
        slope_d = sl_ref[:, :1] * float(d)
        sc = _dil_scores(qn, nrm(kc_ref[...], gk_ref[...]), scale, slope_d, False, True)
        sp = _dil_scores(qn, nrm(kp_ref[...], gk_ref[...]), scale, slope_d, True, n > 0)
        m = jnp.maximum(jnp.max(sc, axis=-1, keepdims=True), jnp.max(sp, axis=-1, keepdims=True))
        lse = m + jnp.log(jnp.sum(jnp.exp(sc - m), axis=-1, keepdims=True) + jnp.sum(jnp.exp(sp - m), axis=-1, keepdims=True))
        o = jnp.dot(jnp.exp(sc - lse).astype(MXU_DTYPE), vc_ref[...].astype(MXU_DTYPE), preferred_element_type=jnp.float32)
        o = o + jnp.dot(jnp.exp(sp - lse).astype(MXU_DTYPE), vp_ref[...].astype(MXU_DTYPE), preferred_element_type=jnp.float32)
        o_ref[...] = o
        l_ref[...] = jnp.broadcast_to(lse, l_ref.shape)

    vec = pl.BlockSpec((1, LANES), lambda r, n, h: (0, 0))
    out = pl.BlockSpec((BLK, LANES), lambda r, n, h: (n, r * Hd + h))
    o, l = pl.pallas_call(
        body, name=name, out_shape=(_f32((S // d, d * Hd * LANES)), _f32((S // d, d * Hd * LANES))), grid=(d, nb, Hd),
        in_specs=[spec(0, 0), spec(1, 0), spec(1, -1), spec(2, 0), spec(2, -1), vec, vec,
                  pl.BlockSpec((None, None, 1, LANES), lambda r, n, h: (g, h, 0, 0))],
        out_specs=(out, out), compiler_params=_params("parallel", "parallel", "parallel"),
    )(view, view, view, view, view, gq, gk, slopes)
    return o.reshape(S, Hd * LANES), l.reshape(S, Hd * LANES)


def dilated_merge(os_, ls_, name):
    S, W = os_[0].shape
    G = len(os_)
    ts, tw = _tile(S, 512, 8), _tile(W, 512)

    def body(*refs):
        o_refs, l_refs, (o_ref, t_ref) = refs[:G], refs[G:2 * G], refs[2 * G:]
        ls = [r[...] for r in l_refs]
        m = ls[0]
        for l in ls[1:]:
            m = jnp.maximum(m, l)
        es = [jnp.exp(l - m) for l in ls]
        tot = es[0]
        for e in es[1:]:
            tot = tot + e
        acc = o_refs[0][...] * (es[0] / tot)
        for r, e in zip(o_refs[1:], es[1:]):
            acc = acc + r[...] * (e / tot)
        o_ref[...] = acc.astype(o_ref.dtype)
        t_ref[...] = m + jnp.log(tot)

    blk = pl.BlockSpec((ts, tw), lambda i, j: (i, j))
    return pl.pallas_call(
        body, name=name, out_shape=(_act((S, W)), _f32((S, W))), grid=(S // ts, W // tw),
        in_specs=[blk] * (2 * G), out_specs=(blk, blk), compiler_params=_params("parallel", "parallel"))(*os_, *ls_)


def dilated_delta(do, o, name):
    S, W = do.shape
    ts = _tile(S, 512, 8)

    def body(d_ref, o_ref, out_ref):
        out_ref[...] = jnp.broadcast_to(jnp.sum(d_ref[...] * o_ref[...].astype(jnp.float32), axis=-1, keepdims=True), out_ref.shape)

    blk = pl.BlockSpec((ts, LANES), lambda i, h: (i, h))
    return pl.pallas_call(body, name=name, out_shape=_f32((S, W)), grid=(S // ts, W // LANES), in_specs=[blk, blk],
                          out_specs=blk, compiler_params=_params("parallel", "parallel"))(do, o)


def dilated_bwd(qkv, do, lse, delta, gq, gk, slopes, g, d, Hd, G, name):
    S, C = qkv.shape
    ncol = C // LANES
    nb = S // d // BLK
    W = Hd * LANES
    view = qkv.reshape(S // d, d * C)
    hview = lambda t: t.reshape(S // d, d * W)
    scale = 1.0 / math.sqrt(LANES)
    spec = _dil_specs(d, nb, Hd, G, g, ncol)

    def hspec(shift):
        return pl.BlockSpec((BLK, LANES), lambda r, n, h: (jnp.clip(n + shift, 0, nb - 1), r * Hd + h))

    def body(q_ref, qx_ref, kc_ref, kp_ref, vc_ref, vp_ref, do_ref, dox_ref, l_ref, lx_ref, dl_ref, dlx_ref,
             gq_ref, gk_ref, sl_ref, dq_ref, dk_ref, dv_ref, dgq_ref, dgk_ref):
        r, n, h = pl.program_id(0), pl.program_id(1), pl.program_id(2)
        first = jnp.logical_and(jnp.logical_and(r == 0, n == 0), h == 0)
        gqv, gkv = gq_ref[...], gk_ref[...]
        nrm = lambda t, gg: (t * _rstd(t, LANES) * gg).astype(MXU_DTYPE)
        f32dot = lambda a, b, dn: lax.dot_general(a, b, dn, preferred_element_type=jnp.float32)
        qn, qxn = nrm(q_ref[...], gqv), nrm(qx_ref[...], gqv)
        kcn, kpn = nrm(kc_ref[...], gkv), nrm(kp_ref[...], gkv)
        vc, vp = vc_ref[...].astype(MXU_DTYPE), vp_ref[...].astype(MXU_DTYPE)
        dob, doxb = do_ref[...].astype(MXU_DTYPE), dox_ref[...].astype(MXU_DTYPE)
        slope_d = sl_ref[:, :1] * float(d)
        lrow, lxrow = l_ref[:, :1], lx_ref[:, :1]
        drow, dxrow = dl_ref[:, :1], dlx_ref[:, :1]
        pc = jnp.exp(_dil_scores(qn, kcn, scale, slope_d, False, True) - lrow)
        pp = jnp.exp(_dil_scores(qn, kpn, scale, slope_d, True, n > 0) - lrow)
        dsc = pc * (f32dot(dob, vc, NT) - drow)
        dsp = pp * (f32dot(dob, vp, NT) - drow)
        dqn = (jnp.dot(dsc.astype(MXU_DTYPE), kcn, preferred_element_type=jnp.float32)
               + jnp.dot(dsp.astype(MXU_DTYPE), kpn, preferred_element_type=jnp.float32)) * scale
        dq, dgq = _norm_bwd(q_ref[...], gqv, dqn, LANES)
        dq_ref[...] = dq.astype(dq_ref.dtype)
        px = jnp.exp(_dil_scores(qxn, kcn, scale, slope_d, True, n < nb - 1) - lxrow)
        dsx = px * (f32dot(doxb, vc, NT) - dxrow)
        dkn = (f32dot(dsc.astype(MXU_DTYPE), qn, TN) + f32dot(dsx.astype(MXU_DTYPE), qxn, TN)) * scale
        dk, dgk = _norm_bwd(kc_ref[...], gkv, dkn, LANES)
        dk_ref[...] = dk.astype(dk_ref.dtype)
        dv_ref[...] = (f32dot(pc.astype(MXU_DTYPE), dob, TN) + f32dot(px.astype(MXU_DTYPE), doxb, TN)).astype(dv_ref.dtype)
        _accumulate(dgq_ref, jnp.sum(dgq, axis=0, keepdims=True), first)
        _accumulate(dgk_ref, jnp.sum(dgk, axis=0, keepdims=True), first)

    vec = pl.BlockSpec((1, LANES), lambda r, n, h: (0, 0))
    out = hspec(0)
    dq, dk, dv, dgq, dgk = pl.pallas_call(
        body, name=name,
        out_shape=(_act((S // d, d * W)), _act((S // d, d * W)), _act((S // d, d * W)), _f32((1, LANES)), _f32((1, LANES))),
        grid=(d, nb, Hd),
        in_specs=[spec(0, 0), spec(0, 1), spec(1, 0), spec(1, -1), spec(2, 0), spec(2, -1), hspec(0), hspec(1), hspec(0), hspec(1),
                  hspec(0), hspec(1), vec, vec, pl.BlockSpec((None, None, 1, LANES), lambda r, n, h: (g, h, 0, 0))],
        out_specs=(out, out, out, vec, vec), compiler_params=_params("arbitrary", "arbitrary", "arbitrary"),
    )(view, view, view, view, view, view, hview(do), hview(do), hview(lse), hview(lse), hview(delta), hview(delta), gq, gk, slopes)
    return dq.reshape(S, W), dk.reshape(S, W), dv.reshape(S, W), dgq, dgk


def adamw(w, g, m, v, layer, prev, name):
    L, r, c = w.shape
    tr, tc = _tile(r, 256, 8), _tile(c, 1024)
    c1 = 1.0 / (1.0 - ADAM_B1 ** ADAM_STEP)
    c2 = 1.0 / (1.0 - ADAM_B2 ** ADAM_STEP)

    def body(*refs):
        w_ref, g_ref, m_ref, v_ref = refs[:4]
        go_ref, d_ref, mo_ref, vo_ref = refs[-4:]
        gv = g_ref[...]
        mn = ADAM_B1 * m_ref[...] + (1.0 - ADAM_B1) * gv
        vn = ADAM_B2 * v_ref[...] + (1.0 - ADAM_B2) * (gv * gv)
        go_ref[...] = gv
        d_ref[...] = -ADAM_LR * ((mn * c1) / (jnp.sqrt(vn * c2) + ADAM_EPS) + ADAM_WD * w_ref[...])
        mo_ref[...] = mn
        vo_ref[...] = vn

    lay = pl.BlockSpec((None, tr, tc), lambda i, j: (layer, i, j))
    flat = pl.BlockSpec((tr, tc), lambda i, j: (i, j))
    ins = [w, g, m, v] + (list(prev) if prev is not None else [])
    in_specs = [lay, flat, lay, lay] + ([ANY] * 4 if prev is not None else [])
    return pl.pallas_call(
        body, name=name, out_shape=tuple(_f32((L, r, c)) for _ in range(4)), grid=(r // tr, c // tc),
        in_specs=in_specs, out_specs=(lay, lay, lay, lay),
        input_output_aliases=({4 + k: k for k in range(4)} if prev is not None else {}),
        compiler_params=_params("parallel", "parallel"))(*ins)


def _ffn_fwd(h, g, W, kind, tag):
    xn = rmsnorm_fwd(h, g, tag + "_norm")
    u = matmul(xn, W(kind + "_w_in", h), name=tag + "_in", tn=1408)
    a = swiglu_fwd(u, tag + "_act")
    out = matmul(a, W(kind + "_w_out", u), scale=0.5, res=h, name=tag + "_out", tk=1408)
    return out, (h, xn, u, a)


def _ffn_bwd(dout, saved, g, W, emit, kind, tag):
    h, xn, u, a = saved
    emit(kind + "_w_out", matmul(a, dout, ta=True, scale=0.5, out_dtype=WIRE_DTYPE, out_axis=0, name=tag + "_dwout", tm=1408))
    da = matmul(dout, W(kind + "_w_out", None), tb=True, scale=0.5, name=tag + "_da", tn=1408)
    du = Sharded(swiglu_bwd(u, da, tag + "_dact"), 0, 1)
    emit(kind + "_w_in", matmul(xn, du, ta=True, out_dtype=WIRE_DTYPE, out_axis=1, name=tag + "_dwin", tn=1408))
    dxn = matmul(du, W(kind + "_w_in", None), tb=True, name=tag + "_dxn", tk=1408)
    return rmsnorm_bwd(h, g, dxn, dout, tag + "_dnorm")


def _pad_gain(g):
    return jnp.pad(g, ((0, 0), (0, QK_PAD - QK_DIM)))


def _mla_fwd(h, P, W, tabs, H):
    g_mix, g_cq, g_ckv = P["mix_norm"][0:1], P["mla_g_cq"], P["mla_g_ckv"]
    pe_blk = (g_cq.shape[1] + g_ckv.shape[1]) // LANES
    xn = rmsnorm_fwd(h, g_mix, "mla_norm")
    w_down = W("mla_w_down", h)
    lat = matmul(xn, w_down, name="mla_down", tn=w_down.shape[1])
    cq, ckv = mla_latent_fwd(lat, g_cq, g_ckv, "mla_latent")
    qraw = matmul(cq, W("mla_w_uq", lat), name="mla_uq")
    kvraw = matmul(ckv, W("mla_w_ukv", qraw), name="mla_ukv")
    q = mla_q_prep_fwd(qraw, _pad_gain(P["mla_g_qn"]), tabs, H, "mla_qprep")
    k, v = mla_k_prep_fwd(kvraw, lat, _pad_gain(P["mla_g_kn"]), tabs, H, pe_blk, "mla_kprep")
    o, lse = mla_attention_fwd(q, k, v, "mla_attn")
    out = matmul(o, W("mla_w_o", lse), res=h, name="mla_o")
    return out, (h, xn, lat, cq, ckv, qraw, kvraw, q, k, v, o, lse, pe_blk)


def _mla_bwd(dout, saved, P, W, emit, tabs, H):
    h, xn, lat, cq, ckv, qraw, kvraw, q, k, v, o, lse, pe_blk = saved
    emit("mla_w_o", matmul(o, dout, ta=True, out_dtype=WIRE_DTYPE, out_axis=0, name="mla_dwo", tm=512))
    do = matmul(dout, W("mla_w_o", None), tb=True, name="mla_do", tn=512)
    delta = attention_delta(do, o, H, "mla_delta")
    dq = mla_attention_bwd_dq(q, k, v, do, lse, delta, "mla_attn_dq")
    dk, dv = mla_attention_bwd_dkv(q, k, v, do, lse, delta, "mla_attn_dkv")
    dqraw, dgq = mla_q_prep_bwd(dq, qraw, _pad_gain(P["mla_g_qn"]), tabs, H, "mla_dqprep")
    dkvraw, dkpe, dgk = mla_k_prep_bwd(dk, dv, kvraw, lat, _pad_gain(P["mla_g_kn"]), tabs, H, pe_blk, "mla_dkprep")
    emit("mla_w_uq", matmul(cq, dqraw, ta=True, out_dtype=WIRE_DTYPE, out_axis=1, name="mla_dwuq"))
    dcq = matmul(dqraw, W("mla_w_uq", None), tb=True, name="mla_dcq", tk=1024)
    emit("mla_w_ukv", matmul(ckv, dkvraw, ta=True, out_dtype=WIRE_DTYPE, out_axis=1, name="mla_dwukv"))
    dckv = matmul(dkvraw, W("mla_w_ukv", None), tb=True, name="mla_dckv", tk=1024)
    dlat, dgcq, dgckv = mla_latent_bwd(dcq, dckv, dkpe, lat, P["mla_g_cq"], P["mla_g_ckv"], "mla_dlatent")
    emit("mla_w_down", matmul(xn, dlat, ta=True, out_dtype=WIRE_DTYPE, out_axis=0, name="mla_dwdown", tm=512, tn=dlat.shape[1]))
    dxn = matmul(dlat, W("mla_w_down", None), tb=True, name="mla_dxn", tn=512, tk=dlat.shape[1])
    dh, dgm = rmsnorm_bwd(h, P["mix_norm"][0:1], dxn, dout, "mla_dnorm")
    return dh, dgm, dict(mla_g_qn=dgq[:, :QK_DIM], mla_g_kn=dgk[:, :QK_DIM], mla_g_cq=dgcq, mla_g_ckv=dgckv)


def _dil_fwd(h, P, W, slopes, Hd):
    G = len(DIL_PAIRS)
    xn = rmsnorm_fwd(h, P["mix_norm"][1:2], "dil_norm")
    qkv = matmul(xn, W("dil_w_qkv", h), name="dil_qkv", tn=1152)
    os_, ls_ = [], []
    for g, (_, d) in enumerate(DIL_PAIRS):
        o_g, l_g = dilated_fwd(qkv, P["dil_g_qn"], P["dil_g_kn"], slopes, g, d, Hd, G, f"dil_attn{g}")
        os_.append(o_g)
        ls_.append(l_g)
    o, lse = dilated_merge(os_, ls_, "dil_merge")
    out = matmul(o, W("dil_w_o", lse), res=h, name="dil_o", tn=512)
    return out, (h, xn, qkv, o, lse)


def _dil_bwd(dout, saved, P, W, emit, slopes, Hd):
    h, xn, qkv, o, lse = saved
    ngrp = len(DIL_PAIRS)
    emit("dil_w_o", matmul(o, dout, ta=True, out_dtype=WIRE_DTYPE, out_axis=1, name="dil_dwo", tn=512))
    do = matmul(dout, W("dil_w_o", None), tb=True, name="dil_do", tk=512)
    delta = dilated_delta(do, o, "dil_delta")
    parts = [dilated_bwd(qkv, do, lse, delta, P["dil_g_qn"], P["dil_g_kn"], slopes, g, d, Hd, ngrp, f"dil_dattn{g}")
             for g, (_, d) in enumerate(DIL_PAIRS)]
    dqkv = jnp.concatenate([p[kind] for kind in range(3) for p in parts], axis=1)
    emit("dil_w_qkv", matmul(xn, dqkv, ta=True, out_dtype=WIRE_DTYPE, out_axis=1, name="dil_dwqkv", tn=1152))
    dxn = matmul(dqkv, W("dil_w_qkv", None), tb=True, name="dil_dxn", tk=1152)
    dh, dgm = rmsnorm_bwd(h, P["mix_norm"][1:2], dxn, dout, "dil_dnorm")
    return dh, dgm, dict(dil_g_qn=parts[0][3] + parts[1][3] + parts[2][3], dil_g_kn=parts[0][4] + parts[1][4] + parts[2][4])


def local_step(x, target, P, get_w, on_grad):
    S, D = x.shape
    H, Hd = MLA_HEADS, DIL_HEADS
    tabs = rope_tables(S)
    slopes = _alibi_slopes(len(DIL_PAIRS), Hd)
    cache = {}

    def weights_of(layer):
        def W(name, after):
            if (name, layer) not in cache:
                cache[name, layer] = Sharded(get_w(name, layer, after), 0, SHARD_AXIS[name])
            return cache[name, layer]
        return W

    row = lambda name, i: P[name][i:i + 1]
    h = x
    saved = []
    for i in range(2):
        W = weights_of(i)
        h, s1 = _ffn_fwd(h, row("ffn1_norm", i), W, "ffn1", f"l{i}_ffn1")
        h, sm = _mla_fwd(h, P, weights_of(0), tabs, H) if i == 0 else _dil_fwd(h, P, weights_of(0), slopes, Hd)
        h, s2 = _ffn_fwd(h, row("ffn2_norm", i), W, "ffn2", f"l{i}_ffn2")
        saved.append((s1, sm, s2))
    loss, dh = loss_head(h, target, "loss")

    gs = {n: [None, None] for n in ("ffn1_norm", "mix_norm", "ffn2_norm")}
    for i in (1, 0):
        s1, sm, s2 = saved[i]
        W = weights_of(i)
        emit = lambda name, g4, layer=i: on_grad(name, layer, g4)
        emit0 = lambda name, g4: on_grad(name, 0, g4)
        dh, gs["ffn2_norm"][i] = _ffn_bwd(dh, s2, row("ffn2_norm", i), W, emit, "ffn2", f"l{i}_ffn2")
        if i == 0:
            dh, gs["mix_norm"][i], gm = _mla_bwd(dh, sm, P, weights_of(0), emit0, tabs, H)
        else:
            dh, gs["mix_norm"][i], gm = _dil_bwd(dh, sm, P, weights_of(0), emit0, slopes, Hd)
        gs.update({n: [val] for n, val in gm.items()})
        dh, gs["ffn1_norm"][i] = _ffn_bwd(dh, s1, row("ffn1_norm", i), W, emit, "ffn1", f"l{i}_ffn1")
    gsmall = {n: jnp.concatenate(v, axis=0) for n, v in gs.items()}
    return loss, dh, gsmall


def _pad_heads(w, real, padded):
    lead, n = w.shape[:-1], w.shape[-1] // real
    w = jnp.pad(w.reshape(*lead, n, real), [(0, 0)] * (len(lead) + 1) + [(0, padded - real)])
    return w.reshape(*lead, n * padded)


def _unpad_heads(w, real, padded):
    lead, n = w.shape[:-1], w.shape[-1] // padded
    return w.reshape(*lead, n, padded)[..., :real].reshape(*lead, n * real)


def _pack_small(gs):
    flat = jnp.concatenate([gs[n].reshape(-1) for n in SMALL])
    rows = -(-flat.shape[0] // LANES)
    rows = -(-rows // 8) * 8
    return jnp.pad(flat, (0, rows * LANES - flat.shape[0])).reshape(rows, LANES)


def _unpack_small(packed, like):
    flat, out, off = packed.reshape(-1), {}, 0
    for n in SMALL:
        size = int(np.prod(like[n].shape))
        out[n] = flat[off:off + size].reshape(like[n].shape)
        off += size
    return out


def kernel(x, ffn1_norm, ffn1_w_in, ffn1_w_out, mix_norm, ffn2_norm, ffn2_w_in, ffn2_w_out, mla_w_down, mla_g_cq, mla_g_ckv, mla_w_uq, mla_w_ukv, mla_g_qn, mla_g_kn, mla_w_o, dil_w_qkv, dil_g_qn, dil_g_kn, dil_w_o, loss_target, m_ffn1_norm, m_ffn1_w_in, m_ffn1_w_out, m_mix_norm, m_ffn2_norm, m_ffn2_w_in, m_ffn2_w_out, m_mla_w_down, m_mla_g_cq, m_mla_g_ckv, m_mla_w_uq, m_mla_w_ukv, m_mla_g_qn, m_mla_g_kn, m_mla_w_o, m_dil_w_qkv, m_dil_g_qn, m_dil_g_kn, m_dil_w_o, v_ffn1_norm, v_ffn1_w_in, v_ffn1_w_out, v_mix_norm, v_ffn2_norm, v_ffn2_w_in, v_ffn2_w_out, v_mla_w_down, v_mla_g_cq, v_mla_g_ckv, v_mla_w_uq, v_mla_w_ukv, v_mla_g_qn, v_mla_g_kn, v_mla_w_o, v_dil_w_qkv, v_dil_g_qn, v_dil_g_kn, v_dil_w_o):
    args = dict(locals())
    w = {n: args[n] for n in WEIGHTS}
    m = {n: args["m_" + n] for n in WEIGHTS}
    v = {n: args["v_" + n] for n in WEIGHTS}
    cx, cy, cc = _me()
    core = jnp.reshape(cc, (1,)).astype(jnp.int32)
    shard = jnp.reshape(2 * cx + cy, (1,)).astype(jnp.int32)
    place = (shard, core)
    pe_pad = LANES - ROPE_DIM

    order = [(n, layer if w[n].shape[0] > 1 else 0) for layer in range(2) for n in USE_ORDER[layer]]
    started = {}

    def get_w(n, l, after):
        k = order.index((n, l))
        behind = after
        for n2, l2 in order[len(started):min(k + AG_AHEAD, len(order) - 1) + 1 if k else 2]:
            _, r, c = w[n2].shape
            land = cast_into_shards(w[n2], l2, shard, after, f"ag_{n2}{l2}_cast").reshape(N_CHIPS, 1, 2, r // 2, c)
            started[n2, l2] = exchange_start((land,), _ag_mine, _ag_mine, f"ag_{n2}{l2}_start")
            behind = started[n2, l2][2]
        sems, bufs, _ = started[n, l]
        _, _, _, h, c = bufs[0].shape
        (land,) = exchange_wait(sems, bufs, _ag_mine, _ag_got, behind, f"ag_{n}{l}_wait")
        full = all_gather_finish(land, f"ag_{n}{l}_finish").reshape(N_CHIPS, 1, 2 * h, c)
        if n == "mla_w_down":
            full = jnp.pad(full, ((0, 0), (0, 0), (0, 0), (0, pe_pad)))
        if n == "mla_w_uq":
            full = _pad_heads(full, QK_DIM, QK_PAD)
        return full

    pending = []
    outs = {n: None for n in BIG}
    last_token = [None]

    def finish_oldest(after):
        n, l, pend = pending.pop(0)
        g = reduce_scatter_finish(pend, place, after, f"rs_{n}{l}")
        outs[n] = adamw(w[n], g, m[n], v[n], l, outs[n], f"adamw_{n}{l}")

    def on_grad(n, l, g4):
        if n == "mla_w_down":
            g4 = g4[..., :g4.shape[-1] - pe_pad]
        if n == "mla_w_uq":
            g4 = _unpad_heads(g4, QK_DIM, QK_PAD)
        after = last_token[0] if last_token[0] is not None else g4
        pending.append((n, l, reduce_scatter_start(g4, core, after, f"rs_{n}{l}")))
        last_token[0] = pending[-1][2][2]
        if len(pending) > RS_WINDOW:
            finish_oldest(last_token[0])

    loss, grad_x, gsmall = local_step(x[0], loss_target[0], {n: w[n] for n in SMALL}, get_w, on_grad)
    loss = lax.psum(loss, ("x", "y", "c"))
    while pending:
        finish_oldest(grad_x)
    small = _unpack_small(all_reduce_small(_pack_small(gsmall), "ar_small"), gsmall)
    for n in SMALL:
        outs[n] = tuple(o[0] for o in adamw(w[n][None], small[n], m[n][None], v[n][None], 0, None, f"adamw_{n}"))

    return (loss, grad_x[None], *[outs[n][0] for n in WEIGHTS], *[outs[n][1] for n in WEIGHTS],
            *[outs[n][2] for n in WEIGHTS], *[outs[n][3] for n in WEIGHTS])
```

```python
import math

import numpy as np
import jax
import jax.numpy as jnp
from jax import lax
from jax.experimental import pallas as pl
from jax.experimental.pallas import tpu as pltpu

MXU_DTYPE = jnp.bfloat16
WIRE_DTYPE = jnp.bfloat16
EPS = 1e-6
NEG = -1e30
N_CHIPS = 4
MESH = pl.DeviceIdType.MESH
ANY = pl.BlockSpec(memory_space=pl.ANY)
LANES = 128

MLA_HEADS = 16
NOPE_DIM = 128
ROPE_DIM = 64
QK_DIM = NOPE_DIM + ROPE_DIM
QK_PAD = 2 * LANES
PREP_ROWS = 1024
ATTN_BLOCK = 1024
AG_AHEAD = 2
RS_WINDOW = 2
ROPE_THETA = 10000.0
DIL_PAIRS = ((128, 1), (512, 4), (2048, 16))
DIL_HEADS = 8
BLK = 128

ADAM_LR = 0.001
ADAM_B1 = 0.9
ADAM_B2 = 0.999
ADAM_EPS = 1e-08
ADAM_WD = 0.01
ADAM_STEP = 10

NT = (((1,), (1,)), ((), ()))
TN = (((0,), (0,)), ((), ()))

SHARD_AXIS = {"ffn1_w_in": 1, "ffn1_w_out": 0, "ffn2_w_in": 1, "ffn2_w_out": 0, "mla_w_down": 0, "mla_w_uq": 1,
              "mla_w_ukv": 1, "mla_w_o": 0, "dil_w_qkv": 1, "dil_w_o": 1}
BIG = tuple(SHARD_AXIS)
USE_ORDER = (("ffn1_w_in", "ffn1_w_out", "mla_w_down", "mla_w_uq", "mla_w_ukv", "mla_w_o", "ffn2_w_in", "ffn2_w_out"),
             ("ffn1_w_in", "ffn1_w_out", "dil_w_qkv", "dil_w_o", "ffn2_w_in", "ffn2_w_out"))
SMALL = ("ffn1_norm", "mix_norm", "ffn2_norm", "mla_g_cq", "mla_g_ckv", "mla_g_qn", "mla_g_kn", "dil_g_qn", "dil_g_kn")
WEIGHTS = ("ffn1_norm", "ffn1_w_in", "ffn1_w_out", "mix_norm", "ffn2_norm", "ffn2_w_in", "ffn2_w_out", "mla_w_down",
           "mla_g_cq", "mla_g_ckv", "mla_w_uq", "mla_w_ukv", "mla_g_qn", "mla_g_kn", "mla_w_o", "dil_w_qkv", "dil_g_qn",
           "dil_g_kn", "dil_w_o")


def _tile(dim, pref, mult=LANES):
    if dim <= pref:
        return dim
    t = (pref // mult) * mult
    while t >= mult:
        if dim % t == 0:
            return t
        t -= mult
    return dim


def _params(*sem):
    return pltpu.CompilerParams(dimension_semantics=sem)


def _f32(shape):
    return jax.ShapeDtypeStruct(shape, jnp.float32)


def _act(shape):
    return jax.ShapeDtypeStruct(shape, MXU_DTYPE)


class Sharded:
    def __init__(self, arr, layer, axis):
        self.arr, self.layer, self.axis = arr, layer, axis
        n, _, r, c = arr.shape
        self.shape = (n * r, c) if axis == 0 else (r, n * c)
        self.per = r if axis == 0 else c

    def spec(self, tr, tc, rc_of):
        l = self.layer
        if self.axis == 0:
            n = self.per // tr

            def imap(*g):
                bi, bj = rc_of(*g)
                return (bi // n, l, bi % n, bj)
        else:
            n = self.per // tc

            def imap(*g):
                bi, bj = rc_of(*g)
                return (bj // n, l, bi, bj % n)
        return pl.BlockSpec((None, None, tr, tc), imap)


def _spec2(tr, tc, rc_of):
    return pl.BlockSpec((tr, tc), lambda *g: rc_of(*g))


def matmul(a, b, *, ta=False, tb=False, out_dtype=jnp.float32, scale=None, res=None, out_axis=None,
           name, tm=1024, tn=1024, tk=512):
    am, ak = (a.shape[1], a.shape[0]) if ta else a.shape
    bk, bn = (b.shape[1], b.shape[0]) if tb else b.shape
    assert ak == bk, (name, a.shape, b.shape, ta, tb)
    M, N, K = am, bn, ak

    def per(x, axis):
        return x.per if isinstance(x, Sharded) and x.axis == axis else None

    def pick(dim, pref, *pers):
        return _tile(math.gcd(dim, *[p for p in pers if p is not None]), pref)

    tm = pick(M, tm, per(a, 1 if ta else 0), M // N_CHIPS if out_axis == 0 else None)
    tn = pick(N, tn, per(b, 0 if tb else 1), N // N_CHIPS if out_axis == 1 else None)
    tk = pick(K, tk, per(a, 0 if ta else 1), per(b, 1 if tb else 0))
    assert M % tm == 0 and N % tn == 0 and K % tk == 0, (name, M, N, K, tm, tn, tk)
    nk = K // tk

    a_rc = (lambda i, j, k: (k, i)) if ta else (lambda i, j, k: (i, k))
    b_rc = (lambda i, j, k: (j, k)) if tb else (lambda i, j, k: (k, j))
    a_blk = (tk, tm) if ta else (tm, tk)
    b_blk = (tn, tk) if tb else (tk, tn)
    a_spec = a.spec(*a_blk, a_rc) if isinstance(a, Sharded) else _spec2(*a_blk, a_rc)
    b_spec = b.spec(*b_blk, b_rc) if isinstance(b, Sharded) else _spec2(*b_blk, b_rc)
    dn = (((0 if ta else 1,), (1 if tb else 0,)), ((), ()))
    has_res = res is not None

    def body(*refs):
        if has_res:
            a_ref, b_ref, r_ref, o_ref, acc = refs
        else:
            a_ref, b_ref, o_ref, acc = refs
        k = pl.program_id(2)

        @pl.when(k == 0)
        def _():
            acc[...] = jnp.zeros_like(acc)

        acc[...] += lax.dot_general(a_ref[...].astype(MXU_DTYPE), b_ref[...].astype(MXU_DTYPE), dn,
                                    preferred_element_type=jnp.float32)

        @pl.when(k == nk - 1)
        def _():
            r = acc[...]
            if scale is not None:
                r = r * scale
            if has_res:
                r = r + r_ref[...]
            o_ref[...] = r.astype(o_ref.dtype)

    in_specs = [a_spec, b_spec]
    args = [a.arr if isinstance(a, Sharded) else a, b.arr if isinstance(b, Sharded) else b]
    if has_res:
        in_specs.append(_spec2(tm, tn, lambda i, j, k: (i, j)))
        args.append(res)
    o_rc = lambda i, j, k: (i, j)
    if out_axis is None:
        out_shape = jax.ShapeDtypeStruct((M, N), out_dtype)
        out_spec = _spec2(tm, tn, o_rc)
    else:
        shp = (N_CHIPS, 1, M // N_CHIPS, N) if out_axis == 0 else (N_CHIPS, 1, M, N // N_CHIPS)
        out_shape = jax.ShapeDtypeStruct(shp, out_dtype)
        out_spec = Sharded(out_shape, 0, out_axis).spec(tm, tn, o_rc)
    return pl.pallas_call(
        body, name=name, out_shape=out_shape, grid=(M // tm, N // tn, nk),
        in_specs=in_specs, out_specs=out_spec,
        scratch_shapes=[pltpu.VMEM((tm, tn), jnp.float32)],
        compiler_params=_params("parallel", "parallel", "arbitrary"),
    )(*args)


def _me():
    return lax.axis_index("x"), lax.axis_index("y"), lax.axis_index("c")


def _other_chips(x, y):
    return [(1 - x, y), (x, 1 - y), (1 - x, 1 - y)]


HBM = pl.BlockSpec(memory_space=pltpu.HBM)
SEM = pl.BlockSpec(memory_space=pltpu.SEMAPHORE)
N_PEERS = 3
TOKEN = jax.ShapeDtypeStruct((8, LANES), jnp.float32)


def _split_params():
    return pltpu.CompilerParams(has_side_effects=pltpu.SideEffectType.DATAFLOW_SIDE_EFFECTING)


def _in_hbm(a):
    return pltpu.with_memory_space_constraint(a, pltpu.HBM)


def exchange_start(bufs, src_of, dst_of, name):
    nb = len(bufs)

    def body(*refs):
        src_ref, land_ref = refs[0], refs[nb - 1]
        sems, token = refs[nb:nb + 2 * N_PEERS], refs[-1]
        x, y, cc = _me()
        for j, (px, py) in enumerate(_other_chips(x, y)):
            pltpu.make_async_remote_copy(
                src_ref=src_of(src_ref, j, (px, py), (x, y, cc)), dst_ref=dst_of(land_ref, j, (px, py), (x, y, cc)),
                send_sem=sems[j], recv_sem=sems[N_PEERS + j], device_id=(px, py, cc), device_id_type=MESH).start()
        token[...] = jnp.zeros_like(token)

    outs = pl.pallas_call(
        body, name=name,
        out_shape=(pltpu.SemaphoreType.DMA(()),) * (2 * N_PEERS) + tuple(pltpu.HBM(b.shape, b.dtype) for b in bufs) + (TOKEN,),
        in_specs=(HBM,) * nb, out_specs=(SEM,) * (2 * N_PEERS) + (HBM,) * nb + (pl.BlockSpec(memory_space=pltpu.VMEM),),
        input_output_aliases={k: 2 * N_PEERS + k for k in range(nb)}, compiler_params=_split_params(),
    )(*[_in_hbm(b) for b in bufs])
    return outs[:2 * N_PEERS], outs[2 * N_PEERS:2 * N_PEERS + nb], outs[-1]


def exchange_wait(sems, bufs, src_of, got_of, after, name):
    nb = len(bufs)

    def body(*refs):
        src_ref, land_ref = refs[0], refs[nb - 1]
        sems_ = refs[nb:nb + 2 * N_PEERS]
        x, y, cc = _me()
        for j, (px, py) in enumerate(_other_chips(x, y)):
            cp = pltpu.make_async_remote_copy(
                src_ref=src_of(src_ref, j, (px, py), (x, y, cc)), dst_ref=got_of(land_ref, j, (px, py), (x, y, cc)),
                send_sem=sems_[j], recv_sem=sems_[N_PEERS + j], device_id=(px, py, cc), device_id_type=MESH)
            cp.wait_send()
            cp.wait_recv()

    return pl.pallas_call(
        body, name=name, out_shape=tuple(pltpu.HBM(b.shape, b.dtype) for b in bufs),
        in_specs=(HBM,) * nb + (SEM,) * (2 * N_PEERS) + (ANY,), out_specs=(HBM,) * nb,
        input_output_aliases={k: k for k in range(nb)}, compiler_params=_split_params(),
    )(*bufs, *sems, after)


def cast_into_shards(w, layer, shard, after, name):
    L, r, c = w.shape
    tr, tc = _tile(r, 512, 16), _tile(c, 1024)

    def body(shard_ref, w_ref, after_ref, o_ref):
        o_ref[...] = w_ref[...].astype(o_ref.dtype)

    grid_spec = pltpu.PrefetchScalarGridSpec(
        num_scalar_prefetch=1, grid=(r // tr, c // tc),
        in_specs=[pl.BlockSpec((None, tr, tc), lambda i, j, sh: (layer, i, j)), ANY],
        out_specs=pl.BlockSpec((None, None, tr, tc), lambda i, j, sh: (sh[0], 0, i, j)))
    return pl.pallas_call(body, name=name, grid_spec=grid_spec, out_shape=jax.ShapeDtypeStruct((N_CHIPS, 1, r, c), WIRE_DTYPE),
                          compiler_params=_params("parallel", "parallel"))(shard, w, after)


def _ag_mine(ref, j, chip, me):
    return ref.at[2 * me[0] + me[1], :, me[2]]


def _ag_got(ref, j, chip, me):
    return ref.at[2 * chip[0] + chip[1], :, me[2]]


def all_gather_finish(land, name):
    def body(land_ref, o_ref, send_sems, recv_sems):
        x, y, cc = _me()
        cps = []
        for j, (px, py) in enumerate(_other_chips(x, y)):
            cp = pltpu.make_async_remote_copy(
                src_ref=o_ref.at[2 * px + py, :, cc], dst_ref=o_ref.at[2 * px + py, :, cc], send_sem=send_sems.at[j],
                recv_sem=recv_sems.at[j], device_id=(x, y, 1 - cc), device_id_type=MESH)
            cp.start()
            cps.append(cp)
        for j, (px, py) in enumerate(_other_chips(x, y)):
            got = o_ref.at[2 * px + py, :, 1 - cc]
            pltpu.make_async_remote_copy(src_ref=got, dst_ref=got, send_sem=send_sems.at[j], recv_sem=recv_sems.at[j],
                                         device_id=(x, y, 1 - cc), device_id_type=MESH).wait_recv()
        for cp in cps:
            cp.wait_send()

    return pl.pallas_call(
        body, name=name, out_shape=jax.ShapeDtypeStruct(land.shape, land.dtype), in_specs=[ANY], out_specs=ANY,
        input_output_aliases={0: 0},
        scratch_shapes=[pltpu.SemaphoreType.DMA((N_PEERS,)), pltpu.SemaphoreType.DMA((N_PEERS,))],
    )(land)


def _rs_src(ref, j, chip, me):
    return ref.at[2 * chip[0] + chip[1]]


def _rs_dst(ref, j, chip, me):
    return ref.at[j]


def sibling_send_halves(g, after, name):
    n, L, two, h, c = g.shape

    def body(g_ref, after_ref, o_ref, send_sem, recv_sem):
        x, y, cc = _me()
        cp = pltpu.make_async_remote_copy(src_ref=g_ref.at[:, :, 1 - cc], dst_ref=o_ref, send_sem=send_sem,
                                          recv_sem=recv_sem, device_id=(x, y, 1 - cc), device_id_type=MESH)
        cp.start()
        cp.wait()

    return pl.pallas_call(
        body, name=name, out_shape=jax.ShapeDtypeStruct((n, L, h, c), g.dtype),
        in_specs=[ANY, ANY], out_specs=ANY,
        scratch_shapes=[pltpu.SemaphoreType.DMA, pltpu.SemaphoreType.DMA],
    )(g, after)


def sibling_gather_halves(r, name):
    def body(r_ref, o_ref, send_sem, recv_sem):
        x, y, cc = _me()
        cp = pltpu.make_async_remote_copy(src_ref=o_ref.at[:, cc], dst_ref=o_ref.at[:, cc], send_sem=send_sem,
                                          recv_sem=recv_sem, device_id=(x, y, 1 - cc), device_id_type=MESH)
        cp.start()
        cp.wait()

    return pl.pallas_call(
        body, name=name, out_shape=jax.ShapeDtypeStruct(r.shape, r.dtype), in_specs=[ANY], out_specs=ANY,
        input_output_aliases={0: 0}, scratch_shapes=[pltpu.SemaphoreType.DMA, pltpu.SemaphoreType.DMA],
    )(r)


def add_sibling(g, r1, core, name):
    n, L, two, h, c = g.shape
    th = _tile(h, 512, 16)
    tc = _tile(c, 1024)

    def body(core_ref, g_ref, r_ref, o_ref):
        o_ref[...] = (g_ref[...].astype(jnp.float32) + r_ref[...].astype(jnp.float32)).astype(o_ref.dtype)

    grid_spec = pltpu.PrefetchScalarGridSpec(
        num_scalar_prefetch=1, grid=(n, L, h // th, c // tc),
        in_specs=[pl.BlockSpec((None, None, None, th, tc), lambda s, l, i, j, core: (s, l, core[0], i, j)),
                  pl.BlockSpec((None, None, th, tc), lambda s, l, i, j, core: (s, l, i, j))],
        out_specs=pl.BlockSpec((None, None, th, tc), lambda s, l, i, j, core: (s, l, i, j)))
    return pl.pallas_call(body, name=name, grid_spec=grid_spec, out_shape=jax.ShapeDtypeStruct((n, L, h, c), WIRE_DTYPE),
                          compiler_params=_params("parallel", "parallel", "parallel", "parallel"))(core, g, r1)


def add_chips(p, r2, place, name):
    n, L, h, c = p.shape
    th = _tile(h, 512, 16)
    tc = _tile(c, 1024)

    def body(shard_ref, core_ref, p_ref, r_ref, o_ref):
        acc = p_ref[...].astype(jnp.float32)
        for j in range(3):
            acc = acc + r_ref[j].astype(jnp.float32)
        o_ref[...] = acc

    grid_spec = pltpu.PrefetchScalarGridSpec(
        num_scalar_prefetch=2, grid=(L, h // th, c // tc),
        in_specs=[pl.BlockSpec((None, None, th, tc), lambda l, i, j, shard, core: (shard[0], l, i, j)),
                  pl.BlockSpec((3, None, th, tc), lambda l, i, j, shard, core: (0, l, i, j))],
        out_specs=pl.BlockSpec((None, None, th, tc), lambda l, i, j, shard, core: (l, core[0], i, j)))
    return pl.pallas_call(body, name=name, grid_spec=grid_spec, out_shape=_f32((L, 2, h, c)),
                          compiler_params=_params("parallel", "parallel", "parallel"))(*place, p, r2)


def reduce_scatter_start(g4, core, after, name):
    n, L, r, c = g4.shape
    g = g4.reshape(n, L, 2, r // 2, c)
    r1 = sibling_send_halves(g, after, name + "_d2d")
    p = add_sibling(g, r1, core, name + "_add1")
    land = lax.empty((N_PEERS, L, r // 2, c), p.dtype)
    return exchange_start((p, land), _rs_src, _rs_dst, name + "_ici_start")


def reduce_scatter_finish(pending, place, after, name):
    sems, bufs, _ = pending
    p, r2 = exchange_wait(sems, bufs, _rs_src, _rs_dst, after, name + "_ici_wait")
    red = add_chips(p, r2, place, name + "_add2")
    L, two, h, c = red.shape
    return sibling_gather_halves(red, name + "_gather").reshape(2 * h, c)


def all_reduce_small(v, name):
    R, C = v.shape

    def body(v_ref, o_ref, buf, send_sems, recv_sems):
        x, y, cc = _me()
        buf[0] = v_ref[...]
        cps = []
        for k in range(1, 8):
            dx, dy, dc = (k >> 2) & 1, (k >> 1) & 1, k & 1
            to = (x ^ dx, y ^ dy, cc ^ dc)
            cp = pltpu.make_async_remote_copy(src_ref=v_ref, dst_ref=buf.at[k], send_sem=send_sems.at[k],
                                              recv_sem=recv_sems.at[k], device_id=to, device_id_type=MESH)
            cp.start()
            cps.append(cp)
        for cp in cps:
            cp.wait()
        me = 4 * x + 2 * y + cc
        acc = buf[me]
        for a in range(1, 8):
            acc = acc + buf[a ^ me]
        o_ref[...] = acc

    vm = pl.BlockSpec(memory_space=pltpu.VMEM)
    return pl.pallas_call(
        body, name=name, out_shape=_f32((R, C)), in_specs=[vm], out_specs=vm,
        scratch_shapes=[pltpu.VMEM((8, R, C), jnp.float32), pltpu.SemaphoreType.DMA((8,)), pltpu.SemaphoreType.DMA((8,))],
    )(v)


def _rstd(x, n):
    return lax.rsqrt(jnp.sum(x * x, axis=-1, keepdims=True) * (1.0 / n) + EPS)


def _accumulate(ref, part, first):
    @pl.when(first)
    def _():
        ref[...] = part

    @pl.when(jnp.logical_not(first))
    def _():
        ref[...] += part


def rmsnorm_fwd(x, g, name):
    S, D = x.shape
    ts = _tile(S, 256, 8)

    def body(x_ref, g_ref, o_ref):
        xv = x_ref[...]
        o_ref[...] = (xv * _rstd(xv, D) * g_ref[...]).astype(o_ref.dtype)

    return pl.pallas_call(
        body, name=name, out_shape=_act((S, D)), grid=(S // ts,),
        in_specs=[pl.BlockSpec((ts, D), lambda i: (i, 0)), pl.BlockSpec((1, D), lambda i: (0, 0))],
        out_specs=pl.BlockSpec((ts, D), lambda i: (i, 0)), compiler_params=_params("parallel"))(x, g)


def _norm_bwd(x, g, dy, n):
    r = _rstd(x, n)
    xh = x * r
    dxh = dy * g
    dx = r * (dxh - xh * (jnp.sum(dxh * xh, axis=-1, keepdims=True) * (1.0 / n)))
    return dx, dy * xh


def rmsnorm_bwd(x, g, dy, dres, name):
    S, D = x.shape
    ts = _tile(S, 256, 8)

    def body(x_ref, g_ref, dy_ref, dres_ref, dx_ref, dg_ref):
        dx, dgp = _norm_bwd(x_ref[...], g_ref[...], dy_ref[...], D)
        dx_ref[...] = dres_ref[...] + dx
        _accumulate(dg_ref, jnp.sum(dgp, axis=0, keepdims=True), pl.program_id(0) == 0)

    row = pl.BlockSpec((ts, D), lambda i: (i, 0))
    vec = pl.BlockSpec((1, D), lambda i: (0, 0))
    return pl.pallas_call(
        body, name=name, out_shape=(_f32((S, D)), _f32((1, D))), grid=(S // ts,),
        in_specs=[row, vec, row, row], out_specs=(row, vec), compiler_params=_params("arbitrary"))(x, g, dy, dres)


def _sigmoid(x):
    return 1.0 / (1.0 + jnp.exp(-x))


def swiglu_fwd(u, name):
    S, F2 = u.shape
    F = F2 // 2
    ts, tf = _tile(S, 512, 8), _tile(F, 512)
    nf = F // tf

    def body(g_ref, u_ref, o_ref):
        gt = g_ref[...]
        o_ref[...] = (gt * _sigmoid(gt) * u_ref[...]).astype(o_ref.dtype)

    return pl.pallas_call(
        body, name=name, out_shape=_act((S, F)), grid=(S // ts, nf),
        in_specs=[pl.BlockSpec((ts, tf), lambda i, j: (i, j)), pl.BlockSpec((ts, tf), lambda i, j: (i, j + nf))],
        out_specs=pl.BlockSpec((ts, tf), lambda i, j: (i, j)), compiler_params=_params("parallel", "parallel"))(u, u)


def swiglu_bwd(u, da, name):
    S, F2 = u.shape
    F = F2 // 2
    ts, tf = _tile(S, 512, 8), _tile(F, 512)
    nf = F // tf

    def body(g_ref, u_ref, da_ref, o_ref):
        gt, up, d = g_ref[...], u_ref[...], da_ref[...]
        s = _sigmoid(gt)
        o_ref[0] = (d * up * (s * (1.0 + gt * (1.0 - s)))).astype(o_ref.dtype)
        o_ref[1] = (d * (gt * s)).astype(o_ref.dtype)

    blk = pl.BlockSpec((ts, tf), lambda i, j: (i, j))
    return pl.pallas_call(
        body, name=name, out_shape=_act((2, 1, S, F)), grid=(S // ts, nf),
        in_specs=[blk, pl.BlockSpec((ts, tf), lambda i, j: (i, j + nf)), blk],
        out_specs=pl.BlockSpec((2, None, ts, tf), lambda i, j: (0, 0, i, j)),
        compiler_params=_params("parallel", "parallel"))(u, u, da)


def loss_head(y, t, name):
    S, D = y.shape
    ts = _tile(S, 256, 8)

    def body(y_ref, t_ref, dy_ref, l_ref):
        e = y_ref[...] - t_ref[...]
        dy_ref[...] = e * (1.0 / D)
        l_ref[...] = jnp.full(l_ref.shape, 0.5 * jnp.sum(jnp.sum(e * e, axis=-1, keepdims=True) * (1.0 / D)), jnp.float32)

    row = pl.BlockSpec((ts, D), lambda i: (i, 0))
    dy, parts = pl.pallas_call(
        body, name=name, out_shape=(_f32((S, D)), _f32((S // ts, 8, LANES))), grid=(S // ts,),
        in_specs=[row, row], out_specs=(row, pl.BlockSpec((None, 8, LANES), lambda i: (i, 0, 0))),
        compiler_params=_params("parallel"))(y, t)
    return jnp.sum(parts[:, 0, 0]), dy


def rope_tables(S):
    inv = 1.0 / (ROPE_THETA ** (jnp.arange(0, ROPE_DIM, 2, dtype=jnp.float32) / ROPE_DIM))
    ang = jnp.arange(S, dtype=jnp.float32)[:, None] * inv[None, :]
    c, s = jnp.cos(ang), jnp.sin(ang)
    z = jnp.zeros_like(c)
    return (jnp.concatenate([c, c, z, z], axis=1), jnp.concatenate([-s, z, z, z], axis=1),
            jnp.concatenate([z, s, z, z], axis=1))


def _rope(x, cos, sa, sb):
    return x * cos + pltpu.roll(x, 96, 1) * sa + pltpu.roll(x, 32, 1) * sb


def _rope_t(d, cos, sa, sb):
    return d * cos + pltpu.roll(d * sa, 32, 1) + pltpu.roll(d * sb, 96, 1)


def _head_norm(x1, x2, g):
    r = lax.rsqrt((jnp.sum(x1 * x1, axis=-1, keepdims=True) + jnp.sum(x2 * x2, axis=-1, keepdims=True)) * (1.0 / QK_DIM) + EPS)
    return x1 * r * g[:, :LANES], x2 * r * g[:, LANES:], r


def _head_norm_bwd(x1, x2, g, d1, d2):
    _, _, r = _head_norm(x1, x2, g)
    h1, h2 = x1 * r, x2 * r
    e1, e2 = d1 * g[:, :LANES], d2 * g[:, LANES:]
    m = (jnp.sum(e1 * h1, axis=-1, keepdims=True) + jnp.sum(e2 * h2, axis=-1, keepdims=True)) * (1.0 / QK_DIM)
    return r * (e1 - h1 * m), r * (e2 - h2 * m), d1 * h1, d2 * h2


def mla_latent_fwd(lat, g_cq, g_ckv, name):
    S, W = lat.shape
    QL, KL = g_cq.shape[1], g_ckv.shape[1]
    ts = _tile(S, 256, 8)

    def body(l_ref, gq_ref, gk_ref, cq_ref, ckv_ref):
        a, b = l_ref[:, :QL], l_ref[:, QL:QL + KL]
        cq_ref[...] = (a * _rstd(a, QL) * gq_ref[...]).astype(cq_ref.dtype)
        ckv_ref[...] = (b * _rstd(b, KL) * gk_ref[...]).astype(ckv_ref.dtype)

    return pl.pallas_call(
        body, name=name, out_shape=(_act((S, QL)), _act((S, KL))), grid=(S // ts,),
        in_specs=[pl.BlockSpec((ts, W), lambda i: (i, 0)), pl.BlockSpec((1, QL), lambda i: (0, 0)),
                  pl.BlockSpec((1, KL), lambda i: (0, 0))],
        out_specs=(pl.BlockSpec((ts, QL), lambda i: (i, 0)), pl.BlockSpec((ts, KL), lambda i: (i, 0))),
        compiler_params=_params("parallel"))(lat, g_cq, g_ckv)


def mla_latent_bwd(dcq, dckv, dkpe, lat, g_cq, g_ckv, name):
    S, W = lat.shape
    QL, KL = g_cq.shape[1], g_ckv.shape[1]
    ts = _tile(S, 256, 8)

    def body(dq_ref, dk_ref, dp_ref, l_ref, gq_ref, gk_ref, o_ref, dgq_ref, dgk_ref):
        first = pl.program_id(0) == 0
        da, ga = _norm_bwd(l_ref[:, :QL], gq_ref[...], dq_ref[...], QL)
        db, gb = _norm_bwd(l_ref[:, QL:QL + KL], gk_ref[...], dk_ref[...], KL)
        o_ref[:, :QL] = da.astype(o_ref.dtype)
        o_ref[:, QL:QL + KL] = db.astype(o_ref.dtype)
        o_ref[:, QL + KL:] = dp_ref[...].astype(o_ref.dtype)
        _accumulate(dgq_ref, jnp.sum(ga, axis=0, keepdims=True), first)
        _accumulate(dgk_ref, jnp.sum(gb, axis=0, keepdims=True), first)

    row = lambda n: pl.BlockSpec((ts, n), lambda i: (i, 0))
    vec = lambda n: pl.BlockSpec((1, n), lambda i: (0, 0))
    return pl.pallas_call(
        body, name=name, out_shape=(_act((S, W)), _f32((1, QL)), _f32((1, KL))), grid=(S // ts,),
        in_specs=[row(QL), row(KL), row(LANES), row(W), vec(QL), vec(KL)], out_specs=(row(W), vec(QL), vec(KL)),
        compiler_params=_params("arbitrary"))(dcq, dckv, dkpe, lat, g_cq, g_ckv)


def mla_q_prep_fwd(qraw, g, tabs, H, name):
    S = qraw.shape[0]
    ts = _tile(S, PREP_ROWS, 8)

    def body(x_ref, g_ref, c_ref, a_ref, b_ref, o_ref):
        y1, y2, _ = _head_norm(x_ref[:, :LANES], x_ref[:, LANES:], g_ref[...])
        o_ref[:, :LANES] = y1.astype(o_ref.dtype)
        o_ref[:, LANES:] = _rope(y2, c_ref[...], a_ref[...], b_ref[...]).astype(o_ref.dtype)

    tab = pl.BlockSpec((ts, LANES), lambda i, h: (i, 0))
    return pl.pallas_call(
        body, name=name, out_shape=_act((H, S, QK_PAD)), grid=(S // ts, H),
        in_specs=[pl.BlockSpec((ts, QK_PAD), lambda i, h: (i, h)), pl.BlockSpec((1, QK_PAD), lambda i, h: (0, 0)), tab, tab, tab],
        out_specs=pl.BlockSpec((None, ts, QK_PAD), lambda i, h: (h, i, 0)),
        compiler_params=_params("parallel", "parallel"))(qraw, g, *tabs)


def mla_q_prep_bwd(dq, qraw, g, tabs, H, name):
    S = qraw.shape[0]
    ts = _tile(S, PREP_ROWS, 8)

    def body(d_ref, x_ref, g_ref, c_ref, a_ref, b_ref, o_ref, dg_ref):
        d2 = _rope_t(d_ref[:, LANES:], c_ref[...], a_ref[...], b_ref[...])
        dx1, dx2, g1, g2 = _head_norm_bwd(x_ref[:, :LANES], x_ref[:, LANES:], g_ref[...], d_ref[:, :LANES], d2)
        o_ref[:, :LANES] = dx1.astype(o_ref.dtype)
        o_ref[:, LANES:] = dx2.astype(o_ref.dtype)
        first = jnp.logical_and(pl.program_id(0) == 0, pl.program_id(1) == 0)
        part = jnp.concatenate([jnp.sum(g1, axis=0, keepdims=True), jnp.sum(g2, axis=0, keepdims=True)], axis=1)
        _accumulate(dg_ref, part, first)

    tab = pl.BlockSpec((ts, LANES), lambda i, h: (i, 0))
    vec = pl.BlockSpec((1, QK_PAD), lambda i, h: (0, 0))
    return pl.pallas_call(
        body, name=name, out_shape=(_act((S, H * QK_PAD)), _f32((1, QK_PAD))), grid=(S // ts, H),
        in_specs=[pl.BlockSpec((None, ts, QK_PAD), lambda i, h: (h, i, 0)), pl.BlockSpec((ts, QK_PAD), lambda i, h: (i, h)),
                  vec, tab, tab, tab],
        out_specs=(pl.BlockSpec((ts, QK_PAD), lambda i, h: (i, h)), vec),
        compiler_params=_params("arbitrary", "arbitrary"))(dq, qraw, g, *tabs)


def mla_k_prep_fwd(kvraw, lat, g, tabs, H, pe_blk, name):
    S = kvraw.shape[0]
    ts = _tile(S, PREP_ROWS, 8)

    def body(x_ref, p_ref, g_ref, c_ref, a_ref, b_ref, k_ref, v_ref):
        y1, y2, _ = _head_norm(x_ref[:, :LANES], p_ref[...], g_ref[...])
        k_ref[:, :LANES] = y1.astype(k_ref.dtype)
        k_ref[:, LANES:] = _rope(y2, c_ref[...], a_ref[...], b_ref[...]).astype(k_ref.dtype)
        v_ref[...] = x_ref[:, LANES:].astype(v_ref.dtype)

    tab = pl.BlockSpec((ts, LANES), lambda i, h: (i, 0))
    return pl.pallas_call(
        body, name=name, out_shape=(_act((H, S, QK_PAD)), _act((H, S, LANES))), grid=(S // ts, H),
        in_specs=[pl.BlockSpec((ts, QK_PAD), lambda i, h: (i, h)), pl.BlockSpec((ts, LANES), lambda i, h: (i, pe_blk)),
                  pl.BlockSpec((1, QK_PAD), lambda i, h: (0, 0)), tab, tab, tab],
        out_specs=(pl.BlockSpec((None, ts, QK_PAD), lambda i, h: (h, i, 0)), pl.BlockSpec((None, ts, LANES), lambda i, h: (h, i, 0))),
        compiler_params=_params("parallel", "parallel"))(kvraw, lat, g, *tabs)


def mla_k_prep_bwd(dk, dv, kvraw, lat, g, tabs, H, pe_blk, name):
    S = kvraw.shape[0]
    ts = _tile(S, PREP_ROWS, 8)

    def body(dk_ref, dv_ref, x_ref, p_ref, g_ref, c_ref, a_ref, b_ref, o_ref, dp_ref, dg_ref):
        i, h = pl.program_id(0), pl.program_id(1)
        d2 = _rope_t(dk_ref[:, LANES:], c_ref[...], a_ref[...], b_ref[...])
        dx1, dx2, g1, g2 = _head_norm_bwd(x_ref[:, :LANES], p_ref[...], g_ref[...], dk_ref[:, :LANES], d2)
        o_ref[:, :LANES] = dx1.astype(o_ref.dtype)
        o_ref[:, LANES:] = dv_ref[...].astype(o_ref.dtype)
        _accumulate(dp_ref, dx2, h == 0)
        part = jnp.concatenate([jnp.sum(g1, axis=0, keepdims=True), jnp.sum(g2, axis=0, keepdims=True)], axis=1)
        _accumulate(dg_ref, part, jnp.logical_and(i == 0, h == 0))

    tab = pl.BlockSpec((ts, LANES), lambda i, h: (i, 0))
    vec = pl.BlockSpec((1, QK_PAD), lambda i, h: (0, 0))
    return pl.pallas_call(
        body, name=name, out_shape=(_act((S, H * QK_PAD)), _f32((S, LANES)), _f32((1, QK_PAD))), grid=(S // ts, H),
        in_specs=[pl.BlockSpec((None, ts, QK_PAD), lambda i, h: (h, i, 0)), pl.BlockSpec((None, ts, LANES), lambda i, h: (h, i, 0)),
                  pl.BlockSpec((ts, QK_PAD), lambda i, h: (i, h)), pl.BlockSpec((ts, LANES), lambda i, h: (i, pe_blk)),
                  vec, tab, tab, tab],
        out_specs=(pl.BlockSpec((ts, QK_PAD), lambda i, h: (i, h)), tab, vec),
        compiler_params=_params("arbitrary", "arbitrary"))(dk, dv, kvraw, lat, g, *tabs)


def _causal_scores(q, k, scale, diagonal):
    s = lax.dot_general(q, k, NT, preferred_element_type=jnp.float32) * scale
    if not diagonal:
        return s
    row = lax.broadcasted_iota(jnp.int32, s.shape, 0)
    col = lax.broadcasted_iota(jnp.int32, s.shape, 1)
    return jnp.where(col <= row, s, NEG)


def _on_causal_blocks(qi, ki, step):
    @pl.when(ki < qi)
    def _():
        step(False)

    @pl.when(ki == qi)
    def _():
        step(True)


def mla_attention_fwd(q, k, v, name):
    H, S, _ = q.shape
    t = _tile(S, ATTN_BLOCK)
    n = S // t
    scale = 1.0 / math.sqrt(QK_DIM)

    def body(q_ref, k_ref, v_ref, o_ref, lse_ref, m_sc, l_sc, acc):
        qi, ki = pl.program_id(1), pl.program_id(2)

        @pl.when(ki == 0)
        def _():
            m_sc[...] = jnp.full(m_sc.shape, NEG, jnp.float32)
            l_sc[...] = jnp.zeros_like(l_sc)
            acc[...] = jnp.zeros_like(acc)

        def step(diagonal):
            s = _causal_scores(q_ref[...], k_ref[...], scale, diagonal)
            m_new = jnp.maximum(m_sc[...], jnp.max(s, axis=-1, keepdims=True))
            alpha = jnp.exp(m_sc[...] - m_new)
            p = jnp.exp(s - m_new)
            l_sc[...] = alpha * l_sc[...] + jnp.sum(p, axis=-1, keepdims=True)
            acc[...] = alpha * acc[...] + jnp.dot(p.astype(MXU_DTYPE), v_ref[...], preferred_element_type=jnp.float32)
            m_sc[...] = m_new

        _on_causal_blocks(qi, ki, step)

        @pl.when(ki == qi)
        def _():
            o_ref[...] = (acc[...] / l_sc[...]).astype(o_ref.dtype)
            lse_ref[...] = m_sc[...] + jnp.log(l_sc[...])

    kv = lambda w: pl.BlockSpec((None, t, w), lambda h, qi, ki: (h, jnp.minimum(ki, qi), 0))
    return pl.pallas_call(
        body, name=name, out_shape=(_act((S, H * LANES)), _f32((H, S, 1))), grid=(H, n, n),
        in_specs=[pl.BlockSpec((None, t, QK_PAD), lambda h, qi, ki: (h, qi, 0)), kv(QK_PAD), kv(LANES)],
        out_specs=(pl.BlockSpec((t, LANES), lambda h, qi, ki: (qi, h)), pl.BlockSpec((None, t, 1), lambda h, qi, ki: (h, qi, 0))),
        scratch_shapes=[pltpu.VMEM((t, 1), jnp.float32), pltpu.VMEM((t, 1), jnp.float32), pltpu.VMEM((t, LANES), jnp.float32)],
        compiler_params=_params("parallel", "parallel", "arbitrary"))(q, k, v)


def attention_delta(do, o, H, name):
    S = do.shape[0]
    ts = _tile(S, 512, 8)

    def body(d_ref, o_ref, out_ref):
        out_ref[...] = jnp.sum(d_ref[...] * o_ref[...].astype(jnp.float32), axis=-1, keepdims=True)

    blk = pl.BlockSpec((ts, LANES), lambda i, h: (i, h))
    return pl.pallas_call(
        body, name=name, out_shape=_f32((H, S, 1)), grid=(S // ts, H), in_specs=[blk, blk],
        out_specs=pl.BlockSpec((None, ts, 1), lambda i, h: (h, i, 0)), compiler_params=_params("parallel", "parallel"))(do, o)


def mla_attention_bwd_dq(q, k, v, do, lse, delta, name):
    H, S, _ = q.shape
    t = _tile(S, ATTN_BLOCK)
    n = S // t
    scale = 1.0 / math.sqrt(QK_DIM)

    def body(q_ref, k_ref, v_ref, do_ref, lse_ref, dl_ref, dq_ref, acc):
        qi, ki = pl.program_id(1), pl.program_id(2)

        @pl.when(ki == 0)
        def _():
            acc[...] = jnp.zeros_like(acc)

        def step(diagonal):
            p = jnp.exp(_causal_scores(q_ref[...], k_ref[...], scale, diagonal) - lse_ref[...])
            dp = lax.dot_general(do_ref[...].astype(MXU_DTYPE), v_ref[...], NT, preferred_element_type=jnp.float32)
            ds = p * (dp - dl_ref[...])
            acc[...] += jnp.dot(ds.astype(MXU_DTYPE), k_ref[...], preferred_element_type=jnp.float32)

        _on_causal_blocks(qi, ki, step)

        @pl.when(ki == qi)
        def _():
            dq_ref[...] = acc[...] * scale

    kv = lambda w: pl.BlockSpec((None, t, w), lambda h, qi, ki: (h, jnp.minimum(ki, qi), 0))
    col = pl.BlockSpec((None, t, 1), lambda h, qi, ki: (h, qi, 0))
    qspec = pl.BlockSpec((None, t, QK_PAD), lambda h, qi, ki: (h, qi, 0))
    return pl.pallas_call(
        body, name=name, out_shape=_f32((H, S, QK_PAD)), grid=(H, n, n),
        in_specs=[qspec, kv(QK_PAD), kv(LANES), pl.BlockSpec((t, LANES), lambda h, qi, ki: (qi, h)), col, col],
        out_specs=qspec, scratch_shapes=[pltpu.VMEM((t, QK_PAD), jnp.float32)],
        compiler_params=_params("parallel", "parallel", "arbitrary"))(q, k, v, do, lse, delta)


def mla_attention_bwd_dkv(q, k, v, do, lse, delta, name):
    H, S, _ = q.shape
    t = _tile(S, ATTN_BLOCK)
    n = S // t
    scale = 1.0 / math.sqrt(QK_DIM)

    def body(q_ref, k_ref, v_ref, do_ref, lse_ref, dl_ref, dk_ref, dv_ref, dk_acc, dv_acc):
        ki, qi = pl.program_id(1), pl.program_id(2)

        @pl.when(qi == 0)
        def _():
            dk_acc[...] = jnp.zeros_like(dk_acc)
            dv_acc[...] = jnp.zeros_like(dv_acc)

        def step(diagonal):
            p = jnp.exp(_causal_scores(q_ref[...], k_ref[...], scale, diagonal) - lse_ref[...])
            dob = do_ref[...].astype(MXU_DTYPE)
            dv_acc[...] += lax.dot_general(p.astype(MXU_DTYPE), dob, TN, preferred_element_type=jnp.float32)
            dp = lax.dot_general(dob, v_ref[...], NT, preferred_element_type=jnp.float32)
            ds = p * (dp - dl_ref[...])
            dk_acc[...] += lax.dot_general(ds.astype(MXU_DTYPE), q_ref[...], TN, preferred_element_type=jnp.float32)

        _on_causal_blocks(qi, ki, step)

        @pl.when(qi == n - 1)
        def _():
            dk_ref[...] = dk_acc[...] * scale
            dv_ref[...] = dv_acc[...]

    qrow = lambda h, ki, qi: (h, jnp.maximum(qi, ki), 0)
    kv = lambda w: pl.BlockSpec((None, t, w), lambda h, ki, qi: (h, ki, 0))
    col = pl.BlockSpec((None, t, 1), qrow)
    return pl.pallas_call(
        body, name=name, out_shape=(_f32((H, S, QK_PAD)), _f32((H, S, LANES))), grid=(H, n, n),
        in_specs=[pl.BlockSpec((None, t, QK_PAD), qrow), kv(QK_PAD), kv(LANES),
                  pl.BlockSpec((t, LANES), lambda h, ki, qi: (jnp.maximum(qi, ki), h)), col, col],
        out_specs=(kv(QK_PAD), kv(LANES)),
        scratch_shapes=[pltpu.VMEM((t, QK_PAD), jnp.float32), pltpu.VMEM((t, LANES), jnp.float32)],
        compiler_params=_params("parallel", "parallel", "arbitrary"))(q, k, v, do, lse, delta)


def _alibi_slopes(G, Hd):
    k = np.arange(1, G * Hd + 1, dtype=np.float32)
    s = (2.0 ** (-8.0 * k / (G * Hd))).astype(np.float32).reshape(G, Hd)
    return jnp.asarray(np.broadcast_to(s[:, :, None, None], (G, Hd, 1, LANES)).copy())


def _dil_scores(qn, kn, scale, slope_d, prev, valid):
    s = lax.dot_general(qn, kn, NT, preferred_element_type=jnp.float32) * scale
    iq = lax.broadcasted_iota(jnp.int32, s.shape, 0)
    ik = lax.broadcasted_iota(jnp.int32, s.shape, 1)
    dist = iq - ik + (BLK if prev else 0)
    ok = (ik >= iq) if prev else (ik <= iq)
    s = s - slope_d * dist.astype(jnp.float32)
    return jnp.where(jnp.logical_and(ok, valid), s, NEG)


def _dil_specs(d, nblk, Hd, G, g):
    def spec(kind, shift):
        col0 = (kind * G + g) * Hd
        return pl.BlockSpec((BLK * d, LANES), lambda n, h: (jnp.clip(n + shift, 0, nblk - 1), col0 + h))
    return spec


def _head_spec(d, nblk, shift):
    return pl.BlockSpec((BLK * d, LANES), lambda n, h: (jnp.clip(n + shift, 0, nblk - 1), h))


def dilated_fwd(qkv, gq, gk, slopes, g, d, Hd, G, name):
    S, C = qkv.shape
    nblk = S // (BLK * d)
    scale = 1.0 / math.sqrt(LANES)
    spec = _dil_specs(d, nblk, Hd, G, g)

    def body(q_ref, kc_ref, kp_ref, vc_ref, vp_ref, gq_ref, gk_ref, sl_ref, o_ref, l_ref):
        n = pl.program_id(0)
        nrm = lambda t, gg: (t * _rstd(t, LANES) * gg).astype(MXU_DTYPE)
        slope_d = sl_ref[:, :1] * float(d)

        def residue(r, carry):
            rows = pl.ds(r, BLK, stride=d)
            qn = nrm(q_ref[rows, :], gq_ref[...])
            sc = _dil_scores(qn, nrm(kc_ref[rows, :], gk_ref[...]), scale, slope_d, False, True)
            sp = _dil_scores(qn, nrm(kp_ref[rows, :], gk_ref[...]), scale, slope_d, True, n > 0)
            m = jnp.maximum(jnp.max(sc, axis=-1, keepdims=True), jnp.max(sp, axis=-1, keepdims=True))
            lse = m + jnp.log(jnp.sum(jnp.exp(sc - m), axis=-1, keepdims=True) + jnp.sum(jnp.exp(sp - m), axis=-1, keepdims=True))
            o = jnp.dot(jnp.exp(sc - lse).astype(MXU_DTYPE), vc_ref[rows, :].astype(MXU_DTYPE), preferred_element_type=jnp.float32)
            o = o + jnp.dot(jnp.exp(sp - lse).astype(MXU_DTYPE), vp_ref[rows, :].astype(MXU_DTYPE), preferred_element_type=jnp.float32)
            o_ref[rows, :] = o
            l_ref[rows, :] = jnp.broadcast_to(lse, (BLK, LANES))
            return carry

        lax.fori_loop(0, d, residue, 0)

    vec = pl.BlockSpec((1, LANES), lambda n, h: (0, 0))
    out = _head_spec(d, nblk, 0)
    return pl.pallas_call(
        body, name=name, out_shape=(_f32((S, Hd * LANES)), _f32((S, Hd * LANES))), grid=(nblk, Hd),
        in_specs=[spec(0, 0), spec(1, 0), spec(1, -1), spec(2, 0), spec(2, -1), vec, vec,
                  pl.BlockSpec((None, None, 1, LANES), lambda n, h: (g, h, 0, 0))],
        out_specs=(out, out), compiler_params=_params("parallel", "parallel"),
    )(qkv, qkv, qkv, qkv, qkv, gq, gk, slopes)


def dilated_merge(os_, ls_, name):
    S, W = os_[0].shape
    G = len(os_)
    ts, tw = _tile(S, 512, 8), _tile(W, 512)

    def body(*refs):
        o_refs, l_refs, (o_ref, t_ref) = refs[:G], refs[G:2 * G], refs[2 * G:]
        ls = [r[...] for r in l_refs]
        m = ls[0]
        for l in ls[1:]:
            m = jnp.maximum(m, l)
        es = [jnp.exp(l - m) for l in ls]
        tot = es[0]
        for e in es[1:]:
            tot = tot + e
        acc = o_refs[0][...] * (es[0] / tot)
        for r, e in zip(o_refs[1:], es[1:]):
            acc = acc + r[...] * (e / tot)
        o_ref[...] = acc.astype(o_ref.dtype)
        t_ref[...] = m + jnp.log(tot)

    blk = pl.BlockSpec((ts, tw), lambda i, j: (i, j))
    return pl.pallas_call(
        body, name=name, out_shape=(_act((S, W)), _f32((S, W))), grid=(S // ts, W // tw),
        in_specs=[blk] * (2 * G), out_specs=(blk, blk), compiler_params=_params("parallel", "parallel"))(*os_, *ls_)


def dilated_delta(do, o, name):
    S, W = do.shape
    ts = _tile(S, 512, 8)

    def body(d_ref, o_ref, out_ref):
        out_ref[...] = jnp.broadcast_to(jnp.sum(d_ref[...] * o_ref[...].astype(jnp.float32), axis=-1, keepdims=True), out_ref.shape)

    blk = pl.BlockSpec((ts, LANES), lambda i, h: (i, h))
    return pl.pallas_call(body, name=name, out_shape=_f32((S, W)), grid=(S // ts, W // LANES), in_specs=[blk, blk],
                          out_specs=blk, compiler_params=_params("parallel", "parallel"))(do, o)


def dilated_bwd(qkv, do, lse, delta, gq, gk, slopes, g, d, Hd, G, name):
    S, C = qkv.shape
    nblk = S // (BLK * d)
    W = Hd * LANES
    scale = 1.0 / math.sqrt(LANES)
    spec = _dil_specs(d, nblk, Hd, G, g)
    hspec = lambda shift: _head_spec(d, nblk, shift)

    def body(q_ref, qx_ref, kc_ref, kp_ref, vc_ref, vp_ref, do_ref, dox_ref, l_ref, lx_ref, dl_ref, dlx_ref,
             gq_ref, gk_ref, sl_ref, dq_ref, dk_ref, dv_ref, dgq_ref, dgk_ref):
        n, h = pl.program_id(0), pl.program_id(1)
        gqv, gkv = gq_ref[...], gk_ref[...]
        nrm = lambda t, gg: (t * _rstd(t, LANES) * gg).astype(MXU_DTYPE)
        f32dot = lambda a, b, dn: lax.dot_general(a, b, dn, preferred_element_type=jnp.float32)
        slope_d = sl_ref[:, :1] * float(d)

        def residue(r, carry):
            rows = pl.ds(r, BLK, stride=d)
            q, kc = q_ref[rows, :], kc_ref[rows, :]
            qn, qxn = nrm(q, gqv), nrm(qx_ref[rows, :], gqv)
            kcn, kpn = nrm(kc, gkv), nrm(kp_ref[rows, :], gkv)
            vc, vp = vc_ref[rows, :].astype(MXU_DTYPE), vp_ref[rows, :].astype(MXU_DTYPE)
            dob, doxb = do_ref[rows, :].astype(MXU_DTYPE), dox_ref[rows, :].astype(MXU_DTYPE)
            lrow, lxrow = l_ref[rows, :][:, :1], lx_ref[rows, :][:, :1]
            drow, dxrow = dl_ref[rows, :][:, :1], dlx_ref[rows, :][:, :1]
            pc = jnp.exp(_dil_scores(qn, kcn, scale, slope_d, False, True) - lrow)
            pp = jnp.exp(_dil_scores(qn, kpn, scale, slope_d, True, n > 0) - lrow)
            dsc = pc * (f32dot(dob, vc, NT) - drow)
            dsp = pp * (f32dot(dob, vp, NT) - drow)
            dqn = (jnp.dot(dsc.astype(MXU_DTYPE), kcn, preferred_element_type=jnp.float32)
                   + jnp.dot(dsp.astype(MXU_DTYPE), kpn, preferred_element_type=jnp.float32)) * scale
            dq, dgq = _norm_bwd(q, gqv, dqn, LANES)
            dq_ref[rows, :] = dq
            px = jnp.exp(_dil_scores(qxn, kcn, scale, slope_d, True, n < nblk - 1) - lxrow)
            dsx = px * (f32dot(doxb, vc, NT) - dxrow)
            dkn = (f32dot(dsc.astype(MXU_DTYPE), qn, TN) + f32dot(dsx.astype(MXU_DTYPE), qxn, TN)) * scale
            dk, dgk = _norm_bwd(kc, gkv, dkn, LANES)
            dk_ref[rows, :] = dk
            dv_ref[rows, :] = f32dot(pc.astype(MXU_DTYPE), dob, TN) + f32dot(px.astype(MXU_DTYPE), doxb, TN)
            return carry[0] + jnp.sum(dgq, axis=0, keepdims=True), carry[1] + jnp.sum(dgk, axis=0, keepdims=True)

        zero = jnp.zeros((1, LANES), jnp.float32)
        dgq_sum, dgk_sum = lax.fori_loop(0, d, residue, (zero, zero))
        first = jnp.logical_and(n == 0, h == 0)
        _accumulate(dgq_ref, dgq_sum, first)
        _accumulate(dgk_ref, dgk_sum, first)

    vec = pl.BlockSpec((1, LANES), lambda n, h: (0, 0))
    out = hspec(0)
    return pl.pallas_call(
        body, name=name, out_shape=(_f32((S, W)), _f32((S, W)), _f32((S, W)), _f32((1, LANES)), _f32((1, LANES))),
        grid=(nblk, Hd),
        in_specs=[spec(0, 0), spec(0, 1), spec(1, 0), spec(1, -1), spec(2, 0), spec(2, -1), hspec(0), hspec(1), hspec(0), hspec(1),
                  hspec(0), hspec(1), vec, vec, pl.BlockSpec((None, None, 1, LANES), lambda n, h: (g, h, 0, 0))],
        out_specs=(out, out, out, vec, vec), compiler_params=_params("arbitrary", "arbitrary"),
    )(qkv, qkv, qkv, qkv, qkv, qkv, do, do, lse, lse, delta, delta, gq, gk, slopes)


def adamw(w, g, m, v, layer, prev, name):
    L, r, c = w.shape
    tr, tc = _tile(r, 256, 8), _tile(c, 1024)
    c1 = 1.0 / (1.0 - ADAM_B1 ** ADAM_STEP)
    c2 = 1.0 / (1.0 - ADAM_B2 ** ADAM_STEP)

    def body(*refs):
        w_ref, g_ref, m_ref, v_ref = refs[:4]
        go_ref, d_ref, mo_ref, vo_ref = refs[-4:]
        gv = g_ref[...]
        mn = ADAM_B1 * m_ref[...] + (1.0 - ADAM_B1) * gv
        vn = ADAM_B2 * v_ref[...] + (1.0 - ADAM_B2) * (gv * gv)
        go_ref[...] = gv
        d_ref[...] = -ADAM_LR * ((mn * c1) / (jnp.sqrt(vn * c2) + ADAM_EPS) + ADAM_WD * w_ref[...])
        mo_ref[...] = mn
        vo_ref[...] = vn

    lay = pl.BlockSpec((None, tr, tc), lambda i, j: (layer, i, j))
    flat = pl.BlockSpec((tr, tc), lambda i, j: (i, j))
    ins = [w, g, m, v] + (list(prev) if prev is not None else [])
    in_specs = [lay, flat, lay, lay] + ([ANY] * 4 if prev is not None else [])
    return pl.pallas_call(
        body, name=name, out_shape=tuple(_f32((L, r, c)) for _ in range(4)), grid=(r // tr, c // tc),
        in_specs=in_specs, out_specs=(lay, lay, lay, lay),
        input_output_aliases=({4 + k: k for k in range(4)} if prev is not None else {}),
        compiler_params=_params("parallel", "parallel"))(*ins)


def _ffn_fwd(h, g, W, kind, tag):
    xn = rmsnorm_fwd(h, g, tag + "_norm")
    u = matmul(xn, W(kind + "_w_in", h), name=tag + "_in", tn=1408, tk=2048)
    a = swiglu_fwd(u, tag + "_act")
    out = matmul(a, W(kind + "_w_out", u), scale=0.5, res=h, name=tag + "_out", tk=2816)
    return out, (h, xn, u, a)


def _ffn_bwd(dout, saved, g, W, emit, kind, tag):
    h, xn, u, a = saved
    emit(kind + "_w_out", matmul(a, dout, ta=True, scale=0.5, out_dtype=WIRE_DTYPE, out_axis=0, name=tag + "_dwout", tm=1408, tk=2048))
    da = matmul(dout, W(kind + "_w_out", None), tb=True, scale=0.5, name=tag + "_da", tn=1408, tk=1024)
    du = Sharded(swiglu_bwd(u, da, tag + "_dact"), 0, 1)
    emit(kind + "_w_in", matmul(xn, du, ta=True, out_dtype=WIRE_DTYPE, out_axis=1, name=tag + "_dwin", tn=1408, tk=2048))
    dxn = matmul(du, W(kind + "_w_in", None), tb=True, name=tag + "_dxn", tk=2816)
    return rmsnorm_bwd(h, g, dxn, dout, tag + "_dnorm")


def _pad_gain(g):
    return jnp.pad(g, ((0, 0), (0, QK_PAD - QK_DIM)))


def _mla_fwd(h, P, W, tabs, H):
    g_mix, g_cq, g_ckv = P["mix_norm"][0:1], P["mla_g_cq"], P["mla_g_ckv"]
    pe_blk = (g_cq.shape[1] + g_ckv.shape[1]) // LANES
    xn = rmsnorm_fwd(h, g_mix, "mla_norm")
    w_down = W("mla_w_down", h)
    lat = matmul(xn, w_down, name="mla_down", tn=w_down.shape[1])
    cq, ckv = mla_latent_fwd(lat, g_cq, g_ckv, "mla_latent")
    qraw = matmul(cq, W("mla_w_uq", lat), name="mla_uq")
    kvraw = matmul(ckv, W("mla_w_ukv", qraw), name="mla_ukv")
    q = mla_q_prep_fwd(qraw, _pad_gain(P["mla_g_qn"]), tabs, H, "mla_qprep")
    k, v = mla_k_prep_fwd(kvraw, lat, _pad_gain(P["mla_g_kn"]), tabs, H, pe_blk, "mla_kprep")
    o, lse = mla_attention_fwd(q, k, v, "mla_attn")
    out = matmul(o, W("mla_w_o", lse), res=h, name="mla_o")
    return out, (h, xn, lat, cq, ckv, qraw, kvraw, q, k, v, o, lse, pe_blk)


def _mla_bwd(dout, saved, P, W, emit, tabs, H):
    h, xn, lat, cq, ckv, qraw, kvraw, q, k, v, o, lse, pe_blk = saved
    emit("mla_w_o", matmul(o, dout, ta=True, out_dtype=WIRE_DTYPE, out_axis=0, name="mla_dwo", tm=512))
    do = matmul(dout, W("mla_w_o", None), tb=True, name="mla_do", tn=512)
    delta = attention_delta(do, o, H, "mla_delta")
    dq = mla_attention_bwd_dq(q, k, v, do, lse, delta, "mla_attn_dq")
    dk, dv = mla_attention_bwd_dkv(q, k, v, do, lse, delta, "mla_attn_dkv")
    dqraw, dgq = mla_q_prep_bwd(dq, qraw, _pad_gain(P["mla_g_qn"]), tabs, H, "mla_dqprep")
    dkvraw, dkpe, dgk = mla_k_prep_bwd(dk, dv, kvraw, lat, _pad_gain(P["mla_g_kn"]), tabs, H, pe_blk, "mla_dkprep")
    emit("mla_w_uq", matmul(cq, dqraw, ta=True, out_dtype=WIRE_DTYPE, out_axis=1, name="mla_dwuq"))
    dcq = matmul(dqraw, W("mla_w_uq", None), tb=True, name="mla_dcq", tk=1024)
    emit("mla_w_ukv", matmul(ckv, dkvraw, ta=True, out_dtype=WIRE_DTYPE, out_axis=1, name="mla_dwukv"))
    dckv = matmul(dkvraw, W("mla_w_ukv", None), tb=True, name="mla_dckv", tk=1024)
    dlat, dgcq, dgckv = mla_latent_bwd(dcq, dckv, dkpe, lat, P["mla_g_cq"], P["mla_g_ckv"], "mla_dlatent")
    emit("mla_w_down", matmul(xn, dlat, ta=True, out_dtype=WIRE_DTYPE, out_axis=0, name="mla_dwdown", tm=512, tn=dlat.shape[1]))
    dxn = matmul(dlat, W("mla_w_down", None), tb=True, name="mla_dxn", tn=512, tk=dlat.shape[1])
    dh, dgm = rmsnorm_bwd(h, P["mix_norm"][0:1], dxn, dout, "mla_dnorm")
    return dh, dgm, dict(mla_g_qn=dgq[:, :QK_DIM], mla_g_kn=dgk[:, :QK_DIM], mla_g_cq=dgcq, mla_g_ckv=dgckv)


def _dil_fwd(h, P, W, slopes, Hd):
    G = len(DIL_PAIRS)
    xn = rmsnorm_fwd(h, P["mix_norm"][1:2], "dil_norm")
    qkv = matmul(xn, W("dil_w_qkv", h), name="dil_qkv", tn=1152)
    os_, ls_ = [], []
    for g, (_, d) in enumerate(DIL_PAIRS):
        o_g, l_g = dilated_fwd(qkv, P["dil_g_qn"], P["dil_g_kn"], slopes, g, d, Hd, G, f"dil_attn{g}")
        os_.append(o_g)
        ls_.append(l_g)
    o, lse = dilated_merge(os_, ls_, "dil_merge")
    out = matmul(o, W("dil_w_o", lse), res=h, name="dil_o", tn=512)
    return out, (h, xn, qkv, o, lse)


def _dil_bwd(dout, saved, P, W, emit, slopes, Hd):
    h, xn, qkv, o, lse = saved
    ngrp = len(DIL_PAIRS)
    emit("dil_w_o", matmul(o, dout, ta=True, out_dtype=WIRE_DTYPE, out_axis=1, name="dil_dwo", tn=512))
    do = matmul(dout, W("dil_w_o", None), tb=True, name="dil_do", tk=512)
    delta = dilated_delta(do, o, "dil_delta")
    parts = [dilated_bwd(qkv, do, lse, delta, P["dil_g_qn"], P["dil_g_kn"], slopes, g, d, Hd, ngrp, f"dil_dattn{g}")
             for g, (_, d) in enumerate(DIL_PAIRS)]
    dqkv = jnp.concatenate([p[kind] for kind in range(3) for p in parts], axis=1).astype(MXU_DTYPE)
    emit("dil_w_qkv", matmul(xn, dqkv, ta=True, out_dtype=WIRE_DTYPE, out_axis=1, name="dil_dwqkv", tn=1152))
    dxn = matmul(dqkv, W("dil_w_qkv", None), tb=True, name="dil_dxn", tk=1152)
    dh, dgm = rmsnorm_bwd(h, P["mix_norm"][1:2], dxn, dout, "dil_dnorm")
    return dh, dgm, dict(dil_g_qn=parts[0][3] + parts[1][3] + parts[2][3], dil_g_kn=parts[0][4] + parts[1][4] + parts[2][4])


def local_step(x, target, P, get_w, on_grad):
    S, D = x.shape
    H, Hd = MLA_HEADS, DIL_HEADS
    tabs = rope_tables(S)
    slopes = _alibi_slopes(len(DIL_PAIRS), Hd)
    cache = {}

    def weights_of(layer):
        def W(name, after):
            if (name, layer) not in cache:
                cache[name, layer] = Sharded(get_w(name, layer, after), 0, SHARD_AXIS[name])
            return cache[name, layer]
        return W

    row = lambda name, i: P[name][i:i + 1]
    h = x
    saved = []
    for i in range(2):
        W = weights_of(i)
        h, s1 = _ffn_fwd(h, row("ffn1_norm", i), W, "ffn1", f"l{i}_ffn1")
        h, sm = _mla_fwd(h, P, weights_of(0), tabs, H) if i == 0 else _dil_fwd(h, P, weights_of(0), slopes, Hd)
        h, s2 = _ffn_fwd(h, row("ffn2_norm", i), W, "ffn2", f"l{i}_ffn2")
        saved.append((s1, sm, s2))
    loss, dh = loss_head(h, target, "loss")

    gs = {n: [None, None] for n in ("ffn1_norm", "mix_norm", "ffn2_norm")}
    for i in (1, 0):
        s1, sm, s2 = saved[i]
        W = weights_of(i)
        emit = lambda name, g4, layer=i: on_grad(name, layer, g4)
        emit0 = lambda name, g4: on_grad(name, 0, g4)
        dh, gs["ffn2_norm"][i] = _ffn_bwd(dh, s2, row("ffn2_norm", i), W, emit, "ffn2", f"l{i}_ffn2")
        if i == 0:
            dh, gs["mix_norm"][i], gm = _mla_bwd(dh, sm, P, weights_of(0), emit0, tabs, H)
        else:
            dh, gs["mix_norm"][i], gm = _dil_bwd(dh, sm, P, weights_of(0), emit0, slopes, Hd)
        gs.update({n: [val] for n, val in gm.items()})
        dh, gs["ffn1_norm"][i] = _ffn_bwd(dh, s1, row("ffn1_norm", i), W, emit, "ffn1", f"l{i}_ffn1")
    gsmall = {n: jnp.concatenate(v, axis=0) for n, v in gs.items()}
    return loss, dh, gsmall


def _pad_heads(w, real, padded):
    lead, n = w.shape[:-1], w.shape[-1] // real
    w = jnp.pad(w.reshape(*lead, n, real), [(0, 0)] * (len(lead) + 1) + [(0, padded - real)])
    return w.reshape(*lead, n * padded)


def _unpad_heads(w, real, padded):
    lead, n = w.shape[:-1], w.shape[-1] // padded
    return w.reshape(*lead, n, padded)[..., :real].reshape(*lead, n * real)


def _pack_small(gs):
    flat = jnp.concatenate([gs[n].reshape(-1) for n in SMALL])
    rows = -(-flat.shape[0] // LANES)
    rows = -(-rows // 8) * 8
    return jnp.pad(flat, (0, rows * LANES - flat.shape[0])).reshape(rows, LANES)


def _unpack_small(packed, like):
    flat, out, off = packed.reshape(-1), {}, 0
    for n in SMALL:
        size = int(np.prod(like[n].shape))
        out[n] = flat[off:off + size].reshape(like[n].shape)
        off += size
    return out


def kernel(x, ffn1_norm, ffn1_w_in, ffn1_w_out, mix_norm, ffn2_norm, ffn2_w_in, ffn2_w_out, mla_w_down, mla_g_cq, mla_g_ckv, mla_w_uq, mla_w_ukv, mla_g_qn, mla_g_kn, mla_w_o, dil_w_qkv, dil_g_qn, dil_g_kn, dil_w_o, loss_target, m_ffn1_norm, m_ffn1_w_in, m_ffn1_w_out, m_mix_norm, m_ffn2_norm, m_ffn2_w_in, m_ffn2_w_out, m_mla_w_down, m_mla_g_cq, m_mla_g_ckv, m_mla_w_uq, m_mla_w_ukv, m_mla_g_qn, m_mla_g_kn, m_mla_w_o, m_dil_w_qkv, m_dil_g_qn, m_dil_g_kn, m_dil_w_o, v_ffn1_norm, v_ffn1_w_in, v_ffn1_w_out, v_mix_norm, v_ffn2_norm, v_ffn2_w_in, v_ffn2_w_out, v_mla_w_down, v_mla_g_cq, v_mla_g_ckv, v_mla_w_uq, v_mla_w_ukv, v_mla_g_qn, v_mla_g_kn, v_mla_w_o, v_dil_w_qkv, v_dil_g_qn, v_dil_g_kn, v_dil_w_o):
    args = dict(locals())
    w = {n: args[n] for n in WEIGHTS}
    m = {n: args["m_" + n] for n in WEIGHTS}
    v = {n: args["v_" + n] for n in WEIGHTS}
    cx, cy, cc = _me()
    core = jnp.reshape(cc, (1,)).astype(jnp.int32)
    shard = jnp.reshape(2 * cx + cy, (1,)).astype(jnp.int32)
    place = (shard, core)
    pe_pad = LANES - ROPE_DIM

    order = [(n, layer if w[n].shape[0] > 1 else 0) for layer in range(2) for n in USE_ORDER[layer]]
    started = {}

    def get_w(n, l, after):
        k = order.index((n, l))
        behind = after
        for n2, l2 in order[len(started):min(k + AG_AHEAD, len(order) - 1) + 1 if k else 2]:
            _, r, c = w[n2].shape
            land = cast_into_shards(w[n2], l2, shard, after, f"ag_{n2}{l2}_cast").reshape(N_CHIPS, 1, 2, r // 2, c)
            started[n2, l2] = exchange_start((land,), _ag_mine, _ag_mine, f"ag_{n2}{l2}_start")
            behind = started[n2, l2][2]
        sems, bufs, _ = started[n, l]
        _, _, _, h, c = bufs[0].shape
        (land,) = exchange_wait(sems, bufs, _ag_mine, _ag_got, behind, f"ag_{n}{l}_wait")
        full = all_gather_finish(land, f"ag_{n}{l}_finish").reshape(N_CHIPS, 1, 2 * h, c)
        if n == "mla_w_down":
            full = jnp.pad(full, ((0, 0), (0, 0), (0, 0), (0, pe_pad)))
        if n == "mla_w_uq":
            full = _pad_heads(full, QK_DIM, QK_PAD)
        return full

    pending = []
    outs = {n: None for n in BIG}
    last_token = [None]

    def finish_oldest(after):
        n, l, pend = pending.pop(0)
        g = reduce_scatter_finish(pend, place, after, f"rs_{n}{l}")
        outs[n] = adamw(w[n], g, m[n], v[n], l, outs[n], f"adamw_{n}{l}")

    def on_grad(n, l, g4):
        if n == "mla_w_down":
            g4 = g4[..., :g4.shape[-1] - pe_pad]
        if n == "mla_w_uq":
            g4 = _unpad_heads(g4, QK_DIM, QK_PAD)
        after = last_token[0] if last_token[0] is not None else g4
        pending.append((n, l, reduce_scatter_start(g4, core, after, f"rs_{n}{l}")))
        last_token[0] = pending[-1][2][2]
        if len(pending) > RS_WINDOW:
            finish_oldest(last_token[0])

    loss, grad_x, gsmall = local_step(x[0], loss_target[0], {n: w[n] for n in SMALL}, get_w, on_grad)
    loss = lax.psum(loss, ("x", "y", "c"))
    while pending:
        finish_oldest(grad_x)
    small = _unpack_small(all_reduce_small(_pack_small(gsmall), "ar_small"), gsmall)
    for n in SMALL:
        outs[n] = tuple(o[0] for o in adamw(w[n][None], small[n], m[n][None], v[n][None], 0, None, f"adamw_{n}"))

    return (loss, grad_x[None], *[outs[n][0] for n in WEIGHTS], *[outs[n][1] for n in WEIGHTS],
            *[outs[n][2] for n in WEIGHTS], *[outs[n][3] for n in WEIGHTS])
```

```python
import math

import numpy as np
import jax
import jax.numpy as jnp
from jax import lax
from jax.experimental import pallas as pl
from jax.experimental.pallas import tpu as pltpu

MXU_DTYPE = jnp.bfloat16
WIRE_DTYPE = jnp.bfloat16
EPS = 1e-6
NEG = -1e30
N_CHIPS = 4
MESH = pl.DeviceIdType.MESH
ANY = pl.BlockSpec(memory_space=pl.ANY)
LANES = 128

MLA_HEADS = 16
NOPE_DIM = 128
ROPE_DIM = 64
QK_DIM = NOPE_DIM + ROPE_DIM
QK_PAD = 2 * LANES
PREP_ROWS = 1024
ATTN_BLOCK = 1024
AG_AHEAD = 2
RS_WINDOW = 2
ROPE_THETA = 10000.0
DIL_PAIRS = ((128, 1), (512, 4), (2048, 16))
DIL_HEADS = 8
BLK = 128

ADAM_LR = 0.001
ADAM_B1 = 0.9
ADAM_B2 = 0.999
ADAM_EPS = 1e-08
ADAM_WD = 0.01
ADAM_STEP = 10

NT = (((1,), (1,)), ((), ()))
TN = (((0,), (0,)), ((), ()))

SHARD_AXIS = {"ffn1_w_in": 1, "ffn1_w_out": 0, "ffn2_w_in": 1, "ffn2_w_out": 0, "mla_w_down": 0, "mla_w_uq": 1,
              "mla_w_ukv": 1, "mla_w_o": 0, "dil_w_qkv": 1, "dil_w_o": 1}
BIG = tuple(SHARD_AXIS)
USE_ORDER = (("ffn1_w_in", "ffn1_w_out", "mla_w_down", "mla_w_uq", "mla_w_ukv", "mla_w_o", "ffn2_w_in", "ffn2_w_out"),
             ("ffn1_w_in", "ffn1_w_out", "dil_w_qkv", "dil_w_o", "ffn2_w_in", "ffn2_w_out"))
SMALL = ("ffn1_norm", "mix_norm", "ffn2_norm", "mla_g_cq", "mla_g_ckv", "mla_g_qn", "mla_g_kn", "dil_g_qn", "dil_g_kn")
WEIGHTS = ("ffn1_norm", "ffn1_w_in", "ffn1_w_out", "mix_norm", "ffn2_norm", "ffn2_w_in", "ffn2_w_out", "mla_w_down",
           "mla_g_cq", "mla_g_ckv", "mla_w_uq", "mla_w_ukv", "mla_g_qn", "mla_g_kn", "mla_w_o", "dil_w_qkv", "dil_g_qn",
           "dil_g_kn", "dil_w_o")


def _tile(dim, pref, mult=LANES):
    if dim <= pref:
        return dim
    t = (pref // mult) * mult
    while t >= mult:
        if dim % t == 0:
            return t
        t -= mult
    return dim


def _params(*sem):
    return pltpu.CompilerParams(dimension_semantics=sem)


def _f32(shape):
    return jax.ShapeDtypeStruct(shape, jnp.float32)


def _act(shape):
    return jax.ShapeDtypeStruct(shape, MXU_DTYPE)


class Sharded:
    def __init__(self, arr, layer, axis):
        self.arr, self.layer, self.axis = arr, layer, axis
        n, _, r, c = arr.shape
        self.shape = (n * r, c) if axis == 0 else (r, n * c)
        self.per = r if axis == 0 else c

    def spec(self, tr, tc, rc_of):
        l = self.layer
        if self.axis == 0:
            n = self.per // tr

            def imap(*g):
                bi, bj = rc_of(*g)
                return (bi // n, l, bi % n, bj)
        else:
            n = self.per // tc

            def imap(*g):
                bi, bj = rc_of(*g)
                return (bj // n, l, bi, bj % n)
        return pl.BlockSpec((None, None, tr, tc), imap)


def _spec2(tr, tc, rc_of):
    return pl.BlockSpec((tr, tc), lambda *g: rc_of(*g))


def matmul(a, b, *, ta=False, tb=False, out_dtype=jnp.float32, scale=None, res=None, out_axis=None,
           name, tm=1024, tn=1024, tk=512):
    am, ak = (a.shape[1], a.shape[0]) if ta else a.shape
    bk, bn = (b.shape[1], b.shape[0]) if tb else b.shape
    assert ak == bk, (name, a.shape, b.shape, ta, tb)
    M, N, K = am, bn, ak

    def per(x, axis):
        return x.per if isinstance(x, Sharded) and x.axis == axis else None

    def pick(dim, pref, *pers):
        return _tile(math.gcd(dim, *[p for p in pers if p is not None]), pref)

    tm = pick(M, tm, per(a, 1 if ta else 0), M // N_CHIPS if out_axis == 0 else None)
    tn = pick(N, tn, per(b, 0 if tb else 1), N // N_CHIPS if out_axis == 1 else None)
    tk = pick(K, tk, per(a, 0 if ta else 1), per(b, 1 if tb else 0))
    assert M % tm == 0 and N % tn == 0 and K % tk == 0, (name, M, N, K, tm, tn, tk)
    nk = K // tk

    a_rc = (lambda i, j, k: (k, i)) if ta else (lambda i, j, k: (i, k))
    b_rc = (lambda i, j, k: (j, k)) if tb else (lambda i, j, k: (k, j))
    a_blk = (tk, tm) if ta else (tm, tk)
    b_blk = (tn, tk) if tb else (tk, tn)
    a_spec = a.spec(*a_blk, a_rc) if isinstance(a, Sharded) else _spec2(*a_blk, a_rc)
    b_spec = b.spec(*b_blk, b_rc) if isinstance(b, Sharded) else _spec2(*b_blk, b_rc)
    dn = (((0 if ta else 1,), (1 if tb else 0,)), ((), ()))
    has_res = res is not None

    def body(*refs):
        if has_res:
            a_ref, b_ref, r_ref, o_ref, acc = refs
        else:
            a_ref, b_ref, o_ref, acc = refs
        k = pl.program_id(2)

        @pl.when(k == 0)
        def _():
            acc[...] = jnp.zeros_like(acc)

        acc[...] += lax.dot_general(a_ref[...].astype(MXU_DTYPE), b_ref[...].astype(MXU_DTYPE), dn,
                                    preferred_element_type=jnp.float32)

        @pl.when(k == nk - 1)
        def _():
            r = acc[...]
            if scale is not None:
                r = r * scale
            if has_res:
                r = r + r_ref[...]
            o_ref[...] = r.astype(o_ref.dtype)

    in_specs = [a_spec, b_spec]
    args = [a.arr if isinstance(a, Sharded) else a, b.arr if isinstance(b, Sharded) else b]
    if has_res:
        in_specs.append(_spec2(tm, tn, lambda i, j, k: (i, j)))
        args.append(res)
    o_rc = lambda i, j, k: (i, j)
    if out_axis is None:
        out_shape = jax.ShapeDtypeStruct((M, N), out_dtype)
        out_spec = _spec2(tm, tn, o_rc)
    else:
        shp = (N_CHIPS, 1, M // N_CHIPS, N) if out_axis == 0 else (N_CHIPS, 1, M, N // N_CHIPS)
        out_shape = jax.ShapeDtypeStruct(shp, out_dtype)
        out_spec = Sharded(out_shape, 0, out_axis).spec(tm, tn, o_rc)
    return pl.pallas_call(
        body, name=name, out_shape=out_shape, grid=(M // tm, N // tn, nk),
        in_specs=in_specs, out_specs=out_spec,
        scratch_shapes=[pltpu.VMEM((tm, tn), jnp.float32)],
        compiler_params=_params("parallel", "parallel", "arbitrary"),
    )(*args)


def _me():
    return lax.axis_index("x"), lax.axis_index("y"), lax.axis_index("c")


def _other_chips(x, y):
    return [(1 - x, y), (x, 1 - y), (1 - x, 1 - y)]


HBM = pl.BlockSpec(memory_space=pltpu.HBM)
SEM = pl.BlockSpec(memory_space=pltpu.SEMAPHORE)
N_PEERS = 3
TOKEN = jax.ShapeDtypeStruct((8, LANES), jnp.float32)


def _split_params():
    return pltpu.CompilerParams(has_side_effects=pltpu.SideEffectType.DATAFLOW_SIDE_EFFECTING)


def _in_hbm(a):
    return pltpu.with_memory_space_constraint(a, pltpu.HBM)


def exchange_start(bufs, src_of, dst_of, after, name):
    nb = len(bufs)

    def body(*refs):
        src_ref, land_ref = refs[0], refs[nb - 1]
        sems, token = refs[nb + 1:nb + 1 + 2 * N_PEERS], refs[-1]
        x, y, cc = _me()
        for j, (px, py) in enumerate(_other_chips(x, y)):
            pltpu.make_async_remote_copy(
                src_ref=src_of(src_ref, j, (px, py), (x, y, cc)), dst_ref=dst_of(land_ref, j, (px, py), (x, y, cc)),
                send_sem=sems[j], recv_sem=sems[N_PEERS + j], device_id=(px, py, cc), device_id_type=MESH).start()
        token[...] = jnp.zeros_like(token)

    outs = pl.pallas_call(
        body, name=name,
        out_shape=(pltpu.SemaphoreType.DMA(()),) * (2 * N_PEERS) + tuple(pltpu.HBM(b.shape, b.dtype) for b in bufs) + (TOKEN,),
        in_specs=(HBM,) * nb + (ANY,), out_specs=(SEM,) * (2 * N_PEERS) + (HBM,) * nb + (pl.BlockSpec(memory_space=pltpu.VMEM),),
        input_output_aliases={k: 2 * N_PEERS + k for k in range(nb)}, compiler_params=_split_params(),
    )(*[_in_hbm(b) for b in bufs], after)
    return outs[:2 * N_PEERS], outs[2 * N_PEERS:2 * N_PEERS + nb], outs[-1]


def exchange_wait(sems, bufs, src_of, got_of, after, name):
    nb = len(bufs)

    def body(*refs):
        src_ref, land_ref = refs[0], refs[nb - 1]
        sems_ = refs[nb:nb + 2 * N_PEERS]
        x, y, cc = _me()
        for j, (px, py) in enumerate(_other_chips(x, y)):
            cp = pltpu.make_async_remote_copy(
                src_ref=src_of(src_ref, j, (px, py), (x, y, cc)), dst_ref=got_of(land_ref, j, (px, py), (x, y, cc)),
                send_sem=sems_[j], recv_sem=sems_[N_PEERS + j], device_id=(px, py, cc), device_id_type=MESH)
            cp.wait_send()
            cp.wait_recv()

    return pl.pallas_call(
        body, name=name, out_shape=tuple(pltpu.HBM(b.shape, b.dtype) for b in bufs),
        in_specs=(HBM,) * nb + (SEM,) * (2 * N_PEERS) + (ANY,), out_specs=(HBM,) * nb,
        input_output_aliases={k: k for k in range(nb)}, compiler_params=_split_params(),
    )(*bufs, *sems, after)


def cast_into_shards(w, layer, shard, after, name):
    L, r, c = w.shape
    tr, tc = _tile(r, 512, 16), _tile(c, 1024)

    def body(shard_ref, w_ref, after_ref, o_ref):
        o_ref[...] = w_ref[...].astype(o_ref.dtype)

    grid_spec = pltpu.PrefetchScalarGridSpec(
        num_scalar_prefetch=1, grid=(r // tr, c // tc),
        in_specs=[pl.BlockSpec((None, tr, tc), lambda i, j, sh: (layer, i, j)), ANY],
        out_specs=pl.BlockSpec((None, None, tr, tc), lambda i, j, sh: (sh[0], 0, i, j)))
    return pl.pallas_call(body, name=name, grid_spec=grid_spec, out_shape=jax.ShapeDtypeStruct((N_CHIPS, 1, r, c), WIRE_DTYPE),
                          compiler_params=_params("parallel", "parallel"))(shard, w, after)


def _ag_mine(ref, j, chip, me):
    return ref.at[2 * me[0] + me[1], :, me[2]]


def _ag_got(ref, j, chip, me):
    return ref.at[2 * chip[0] + chip[1], :, me[2]]


def all_gather_finish(land, name):
    def body(land_ref, o_ref, send_sems, recv_sems):
        x, y, cc = _me()
        cps = []
        for j, (px, py) in enumerate(_other_chips(x, y)):
            cp = pltpu.make_async_remote_copy(
                src_ref=o_ref.at[2 * px + py, :, cc], dst_ref=o_ref.at[2 * px + py, :, cc], send_sem=send_sems.at[j],
                recv_sem=recv_sems.at[j], device_id=(x, y, 1 - cc), device_id_type=MESH)
            cp.start()
            cps.append(cp)
        for j, (px, py) in enumerate(_other_chips(x, y)):
            got = o_ref.at[2 * px + py, :, 1 - cc]
            pltpu.make_async_remote_copy(src_ref=got, dst_ref=got, send_sem=send_sems.at[j], recv_sem=recv_sems.at[j],
                                         device_id=(x, y, 1 - cc), device_id_type=MESH).wait_recv()
        for cp in cps:
            cp.wait_send()

    return pl.pallas_call(
        body, name=name, out_shape=jax.ShapeDtypeStruct(land.shape, land.dtype), in_specs=[ANY], out_specs=ANY,
        input_output_aliases={0: 0},
        scratch_shapes=[pltpu.SemaphoreType.DMA((N_PEERS,)), pltpu.SemaphoreType.DMA((N_PEERS,))],
    )(land)


def _rs_src(ref, j, chip, me):
    return ref.at[2 * chip[0] + chip[1]]


def _rs_dst(ref, j, chip, me):
    return ref.at[j]


def sibling_send_halves(g, after, name):
    n, L, two, h, c = g.shape

    def body(g_ref, after_ref, o_ref, send_sem, recv_sem):
        x, y, cc = _me()
        cp = pltpu.make_async_remote_copy(src_ref=g_ref.at[:, :, 1 - cc], dst_ref=o_ref, send_sem=send_sem,
                                          recv_sem=recv_sem, device_id=(x, y, 1 - cc), device_id_type=MESH)
        cp.start()
        cp.wait()

    return pl.pallas_call(
        body, name=name, out_shape=jax.ShapeDtypeStruct((n, L, h, c), g.dtype),
        in_specs=[ANY, ANY], out_specs=ANY,
        scratch_shapes=[pltpu.SemaphoreType.DMA, pltpu.SemaphoreType.DMA],
    )(g, after)


def sibling_gather_halves(r, name):
    def body(r_ref, o_ref, send_sem, recv_sem):
        x, y, cc = _me()
        cp = pltpu.make_async_remote_copy(src_ref=o_ref.at[:, cc], dst_ref=o_ref.at[:, cc], send_sem=send_sem,
                                          recv_sem=recv_sem, device_id=(x, y, 1 - cc), device_id_type=MESH)
        cp.start()
        cp.wait()

    return pl.pallas_call(
        body, name=name, out_shape=jax.ShapeDtypeStruct(r.shape, r.dtype), in_specs=[ANY], out_specs=ANY,
        input_output_aliases={0: 0}, scratch_shapes=[pltpu.SemaphoreType.DMA, pltpu.SemaphoreType.DMA],
    )(r)


def add_sibling(g, r1, core, name):
    n, L, two, h, c = g.shape
    th = _tile(h, 512, 16)
    tc = _tile(c, 1024)

    def body(core_ref, g_ref, r_ref, o_ref):
        o_ref[...] = (g_ref[...].astype(jnp.float32) + r_ref[...].astype(jnp.float32)).astype(o_ref.dtype)

    grid_spec = pltpu.PrefetchScalarGridSpec(
        num_scalar_prefetch=1, grid=(n, L, h // th, c // tc),
        in_specs=[pl.BlockSpec((None, None, None, th, tc), lambda s, l, i, j, core: (s, l, core[0], i, j)),
                  pl.BlockSpec((None, None, th, tc), lambda s, l, i, j, core: (s, l, i, j))],
        out_specs=pl.BlockSpec((None, None, th, tc), lambda s, l, i, j, core: (s, l, i, j)))
    return pl.pallas_call(body, name=name, grid_spec=grid_spec, out_shape=jax.ShapeDtypeStruct((n, L, h, c), WIRE_DTYPE),
                          compiler_params=_params("parallel", "parallel", "parallel", "parallel"))(core, g, r1)


def add_chips(p, r2, place, name):
    n, L, h, c = p.shape
    th = _tile(h, 512, 16)
    tc = _tile(c, 1024)

    def body(shard_ref, core_ref, p_ref, r_ref, o_ref):
        acc = p_ref[...].astype(jnp.float32)
        for j in range(3):
            acc = acc + r_ref[j].astype(jnp.float32)
        o_ref[...] = acc

    grid_spec = pltpu.PrefetchScalarGridSpec(
        num_scalar_prefetch=2, grid=(L, h // th, c // tc),
        in_specs=[pl.BlockSpec((None, None, th, tc), lambda l, i, j, shard, core: (shard[0], l, i, j)),
                  pl.BlockSpec((3, None, th, tc), lambda l, i, j, shard, core: (0, l, i, j))],
        out_specs=pl.BlockSpec((None, None, th, tc), lambda l, i, j, shard, core: (l, core[0], i, j)))
    return pl.pallas_call(body, name=name, grid_spec=grid_spec, out_shape=_f32((L, 2, h, c)),
                          compiler_params=_params("parallel", "parallel", "parallel"))(*place, p, r2)


def reduce_scatter_start(g4, core, after, name):
    n, L, r, c = g4.shape
    g = g4.reshape(n, L, 2, r // 2, c)
    r1 = sibling_send_halves(g, after, name + "_d2d")
    p = add_sibling(g, r1, core, name + "_add1")
    land = lax.empty((N_PEERS, L, r // 2, c), p.dtype)
    return exchange_start((p, land), _rs_src, _rs_dst, r1, name + "_ici_start")


def reduce_scatter_finish(pending, place, after, name):
    sems, bufs, _ = pending
    p, r2 = exchange_wait(sems, bufs, _rs_src, _rs_dst, after, name + "_ici_wait")
    red = add_chips(p, r2, place, name + "_add2")
    L, two, h, c = red.shape
    return sibling_gather_halves(red, name + "_gather").reshape(2 * h, c)


def all_reduce_small(v, name):
    R, C = v.shape

    def body(v_ref, o_ref, buf, send_sems, recv_sems):
        x, y, cc = _me()
        buf[0] = v_ref[...]
        cps = []
        for k in range(1, 8):
            dx, dy, dc = (k >> 2) & 1, (k >> 1) & 1, k & 1
            to = (x ^ dx, y ^ dy, cc ^ dc)
            cp = pltpu.make_async_remote_copy(src_ref=v_ref, dst_ref=buf.at[k], send_sem=send_sems.at[k],
                                              recv_sem=recv_sems.at[k], device_id=to, device_id_type=MESH)
            cp.start()
            cps.append(cp)
        for cp in cps:
            cp.wait()
        me = 4 * x + 2 * y + cc
        acc = buf[me]
        for a in range(1, 8):
            acc = acc + buf[a ^ me]
        o_ref[...] = acc

    vm = pl.BlockSpec(memory_space=pltpu.VMEM)
    return pl.pallas_call(
        body, name=name, out_shape=_f32((R, C)), in_specs=[vm], out_specs=vm,
        scratch_shapes=[pltpu.VMEM((8, R, C), jnp.float32), pltpu.SemaphoreType.DMA((8,)), pltpu.SemaphoreType.DMA((8,))],
    )(v)


def _rstd(x, n):
    return lax.rsqrt(jnp.sum(x * x, axis=-1, keepdims=True) * (1.0 / n) + EPS)


def _accumulate(ref, part, first):
    @pl.when(first)
    def _():
        ref[...] = part

    @pl.when(jnp.logical_not(first))
    def _():
        ref[...] += part


def rmsnorm_fwd(x, g, name):
    S, D = x.shape
    ts = _tile(S, 256, 8)

    def body(x_ref, g_ref, o_ref):
        xv = x_ref[...]
        o_ref[...] = (xv * _rstd(xv, D) * g_ref[...]).astype(o_ref.dtype)

    return pl.pallas_call(
        body, name=name, out_shape=_act((S, D)), grid=(S // ts,),
        in_specs=[pl.BlockSpec((ts, D), lambda i: (i, 0)), pl.BlockSpec((1, D), lambda i: (0, 0))],
        out_specs=pl.BlockSpec((ts, D), lambda i: (i, 0)), compiler_params=_params("parallel"))(x, g)


def _norm_bwd(x, g, dy, n):
    r = _rstd(x, n)
    xh = x * r
    dxh = dy * g
    dx = r * (dxh - xh * (jnp.sum(dxh * xh, axis=-1, keepdims=True) * (1.0 / n)))
    return dx, dy * xh


def rmsnorm_bwd(x, g, dy, dres, name):
    S, D = x.shape
    ts = _tile(S, 256, 8)

    def body(x_ref, g_ref, dy_ref, dres_ref, dx_ref, dg_ref):
        dx, dgp = _norm_bwd(x_ref[...], g_ref[...], dy_ref[...], D)
        dx_ref[...] = dres_ref[...] + dx
        _accumulate(dg_ref, jnp.sum(dgp, axis=0, keepdims=True), pl.program_id(0) == 0)

    row = pl.BlockSpec((ts, D), lambda i: (i, 0))
    vec = pl.BlockSpec((1, D), lambda i: (0, 0))
    return pl.pallas_call(
        body, name=name, out_shape=(_f32((S, D)), _f32((1, D))), grid=(S // ts,),
        in_specs=[row, vec, row, row], out_specs=(row, vec), compiler_params=_params("arbitrary"))(x, g, dy, dres)


def _sigmoid(x):
    return 1.0 / (1.0 + jnp.exp(-x))


def swiglu_in(xn, w_in, name, tm=512, tn=1408):
    S, D = xn.shape
    F = w_in.shape[1] // 2
    tm, tn = _tile(S, tm, 8), _tile(math.gcd(F, w_in.per), tn)
    nf = F // tn

    def body(x_ref, wg_ref, wu_ref, u_ref, a_ref):
        x = x_ref[...].astype(MXU_DTYPE)
        gt = jnp.dot(x, wg_ref[...], preferred_element_type=jnp.float32)
        up = jnp.dot(x, wu_ref[...], preferred_element_type=jnp.float32)
        u_ref[0] = gt.astype(u_ref.dtype)
        u_ref[1] = up.astype(u_ref.dtype)
        a_ref[...] = (gt * _sigmoid(gt) * up).astype(a_ref.dtype)

    return pl.pallas_call(
        body, name=name, out_shape=(_act((2, 1, S, F)), _act((S, F))), grid=(nf, S // tm),
        in_specs=[pl.BlockSpec((tm, D), lambda j, i: (i, 0)), w_in.spec(D, tn, lambda j, i: (0, j)),
                  w_in.spec(D, tn, lambda j, i: (0, j + nf))],
        out_specs=(pl.BlockSpec((2, None, tm, tn), lambda j, i: (0, 0, i, j)), pl.BlockSpec((tm, tn), lambda j, i: (i, j))),
        compiler_params=_params("parallel", "parallel"))(xn, w_in.arr, w_in.arr)


def swiglu_out_bwd(dout, w_out, u, scale, name, tm=512, tn=1408):
    S, D = dout.shape
    F = w_out.shape[0]
    tm, tn = _tile(S, tm, 8), _tile(math.gcd(F, w_out.per), tn)

    def body(d_ref, w_ref, u_ref, o_ref):
        da = lax.dot_general(d_ref[...].astype(MXU_DTYPE), w_ref[...], NT, preferred_element_type=jnp.float32) * scale
        gt, up = u_ref[0].astype(jnp.float32), u_ref[1].astype(jnp.float32)
        s = _sigmoid(gt)
        o_ref[0] = (da * up * (s * (1.0 + gt * (1.0 - s)))).astype(o_ref.dtype)
        o_ref[1] = (da * (gt * s)).astype(o_ref.dtype)

    planes = pl.BlockSpec((2, None, tm, tn), lambda j, i: (0, 0, i, j))
    return pl.pallas_call(
        body, name=name, out_shape=_act((2, 1, S, F)), grid=(F // tn, S // tm),
        in_specs=[pl.BlockSpec((tm, D), lambda j, i: (i, 0)), w_out.spec(tn, D, lambda j, i: (j, 0)), planes],
        out_specs=planes, compiler_params=_params("parallel", "parallel"))(dout, w_out.arr, u)


def loss_head(y, t, name):
    S, D = y.shape
    ts = _tile(S, 256, 8)

    def body(y_ref, t_ref, dy_ref, l_ref):
        e = y_ref[...] - t_ref[...]
        dy_ref[...] = e * (1.0 / D)
        l_ref[...] = jnp.full(l_ref.shape, 0.5 * jnp.sum(jnp.sum(e * e, axis=-1, keepdims=True) * (1.0 / D)), jnp.float32)

    row = pl.BlockSpec((ts, D), lambda i: (i, 0))
    dy, parts = pl.pallas_call(
        body, name=name, out_shape=(_f32((S, D)), _f32((S // ts, 8, LANES))), grid=(S // ts,),
        in_specs=[row, row], out_specs=(row, pl.BlockSpec((None, 8, LANES), lambda i: (i, 0, 0))),
        compiler_params=_params("parallel"))(y, t)
    return jnp.sum(parts[:, 0, 0]), dy


def rope_tables(S):
    inv = 1.0 / (ROPE_THETA ** (jnp.arange(0, ROPE_DIM, 2, dtype=jnp.float32) / ROPE_DIM))
    ang = jnp.arange(S, dtype=jnp.float32)[:, None] * inv[None, :]
    c, s = jnp.cos(ang), jnp.sin(ang)
    z = jnp.zeros_like(c)
    return (jnp.concatenate([c, c, z, z], axis=1), jnp.concatenate([-s, z, z, z], axis=1),
            jnp.concatenate([z, s, z, z], axis=1))


def _rope(x, cos, sa, sb):
    return x * cos + pltpu.roll(x, 96, 1) * sa + pltpu.roll(x, 32, 1) * sb


def _rope_t(d, cos, sa, sb):
    return d * cos + pltpu.roll(d * sa, 32, 1) + pltpu.roll(d * sb, 96, 1)


def _head_norm(x1, x2, g):
    r = lax.rsqrt((jnp.sum(x1 * x1, axis=-1, keepdims=True) + jnp.sum(x2 * x2, axis=-1, keepdims=True)) * (1.0 / QK_DIM) + EPS)
    return x1 * r * g[:, :LANES], x2 * r * g[:, LANES:], r


def _head_norm_bwd(x1, x2, g, d1, d2):
    _, _, r = _head_norm(x1, x2, g)
    h1, h2 = x1 * r, x2 * r
    e1, e2 = d1 * g[:, :LANES], d2 * g[:, LANES:]
    m = (jnp.sum(e1 * h1, axis=-1, keepdims=True) + jnp.sum(e2 * h2, axis=-1, keepdims=True)) * (1.0 / QK_DIM)
    return r * (e1 - h1 * m), r * (e2 - h2 * m), d1 * h1, d2 * h2


def mla_latent_fwd(lat, g_cq, g_ckv, name):
    S, W = lat.shape
    QL, KL = g_cq.shape[1], g_ckv.shape[1]
    ts = _tile(S, 256, 8)

    def body(l_ref, gq_ref, gk_ref, cq_ref, ckv_ref):
        a, b = l_ref[:, :QL], l_ref[:, QL:QL + KL]
        cq_ref[...] = (a * _rstd(a, QL) * gq_ref[...]).astype(cq_ref.dtype)
        ckv_ref[...] = (b * _rstd(b, KL) * gk_ref[...]).astype(ckv_ref.dtype)

    return pl.pallas_call(
        body, name=name, out_shape=(_act((S, QL)), _act((S, KL))), grid=(S // ts,),
        in_specs=[pl.BlockSpec((ts, W), lambda i: (i, 0)), pl.BlockSpec((1, QL), lambda i: (0, 0)),
                  pl.BlockSpec((1, KL), lambda i: (0, 0))],
        out_specs=(pl.BlockSpec((ts, QL), lambda i: (i, 0)), pl.BlockSpec((ts, KL), lambda i: (i, 0))),
        compiler_params=_params("parallel"))(lat, g_cq, g_ckv)


def mla_latent_bwd(dcq, dckv, dkpe, lat, g_cq, g_ckv, name):
    S, W = lat.shape
    QL, KL = g_cq.shape[1], g_ckv.shape[1]
    ts = _tile(S, 256, 8)

    def body(dq_ref, dk_ref, dp_ref, l_ref, gq_ref, gk_ref, o_ref, dgq_ref, dgk_ref):
        first = pl.program_id(0) == 0
        da, ga = _norm_bwd(l_ref[:, :QL], gq_ref[...], dq_ref[...], QL)
        db, gb = _norm_bwd(l_ref[:, QL:QL + KL], gk_ref[...], dk_ref[...], KL)
        o_ref[:, :QL] = da.astype(o_ref.dtype)
        o_ref[:, QL:QL + KL] = db.astype(o_ref.dtype)
        o_ref[:, QL + KL:] = dp_ref[...].astype(o_ref.dtype)
        _accumulate(dgq_ref, jnp.sum(ga, axis=0, keepdims=True), first)
        _accumulate(dgk_ref, jnp.sum(gb, axis=0, keepdims=True), first)

    row = lambda n: pl.BlockSpec((ts, n), lambda i: (i, 0))
    vec = lambda n: pl.BlockSpec((1, n), lambda i: (0, 0))
    return pl.pallas_call(
        body, name=name, out_shape=(_act((S, W)), _f32((1, QL)), _f32((1, KL))), grid=(S // ts,),
        in_specs=[row(QL), row(KL), row(LANES), row(W), vec(QL), vec(KL)], out_specs=(row(W), vec(QL), vec(KL)),
        compiler_params=_params("arbitrary"))(dcq, dckv, dkpe, lat, g_cq, g_ckv)


def mla_q_prep_fwd(qraw, g, tabs, H, name):
    S = qraw.shape[0]
    ts = _tile(S, PREP_ROWS, 8)

    def body(x_ref, g_ref, c_ref, a_ref, b_ref, o_ref):
        y1, y2, _ = _head_norm(x_ref[:, :LANES], x_ref[:, LANES:], g_ref[...])
        o_ref[:, :LANES] = y1.astype(o_ref.dtype)
        o_ref[:, LANES:] = _rope(y2, c_ref[...], a_ref[...], b_ref[...]).astype(o_ref.dtype)

    tab = pl.BlockSpec((ts, LANES), lambda i, h: (i, 0))
    return pl.pallas_call(
        body, name=name, out_shape=_act((H, S, QK_PAD)), grid=(S // ts, H),
        in_specs=[pl.BlockSpec((ts, QK_PAD), lambda i, h: (i, h)), pl.BlockSpec((1, QK_PAD), lambda i, h: (0, 0)), tab, tab, tab],
        out_specs=pl.BlockSpec((None, ts, QK_PAD), lambda i, h: (h, i, 0)),
        compiler_params=_params("parallel", "parallel"))(qraw, g, *tabs)


def mla_q_prep_bwd(dq, qraw, g, tabs, H, name):
    S = qraw.shape[0]
    ts = _tile(S, PREP_ROWS, 8)

    def body(d_ref, x_ref, g_ref, c_ref, a_ref, b_ref, o_ref, dg_ref):
        d2 = _rope_t(d_ref[:, LANES:], c_ref[...], a_ref[...], b_ref[...])
        dx1, dx2, g1, g2 = _head_norm_bwd(x_ref[:, :LANES], x_ref[:, LANES:], g_ref[...], d_ref[:, :LANES], d2)
        o_ref[:, :LANES] = dx1.astype(o_ref.dtype)
        o_ref[:, LANES:] = dx2.astype(o_ref.dtype)
        first = jnp.logical_and(pl.program_id(0) == 0, pl.program_id(1) == 0)
        part = jnp.concatenate([jnp.sum(g1, axis=0, keepdims=True), jnp.sum(g2, axis=0, keepdims=True)], axis=1)
        _accumulate(dg_ref, part, first)

    tab = pl.BlockSpec((ts, LANES), lambda i, h: (i, 0))
    vec = pl.BlockSpec((1, QK_PAD), lambda i, h: (0, 0))
    return pl.pallas_call(
        body, name=name, out_shape=(_act((S, H * QK_PAD)), _f32((1, QK_PAD))), grid=(S // ts, H),
        in_specs=[pl.BlockSpec((None, ts, QK_PAD), lambda i, h: (h, i, 0)), pl.BlockSpec((ts, QK_PAD), lambda i, h: (i, h)),
                  vec, tab, tab, tab],
        out_specs=(pl.BlockSpec((ts, QK_PAD), lambda i, h: (i, h)), vec),
        compiler_params=_params("arbitrary", "arbitrary"))(dq, qraw, g, *tabs)


def mla_k_prep_fwd(kvraw, lat, g, tabs, H, pe_blk, name):
    S = kvraw.shape[0]
    ts = _tile(S, PREP_ROWS, 8)

    def body(x_ref, p_ref, g_ref, c_ref, a_ref, b_ref, k_ref, v_ref):
        y1, y2, _ = _head_norm(x_ref[:, :LANES], p_ref[...], g_ref[...])
        k_ref[:, :LANES] = y1.astype(k_ref.dtype)
        k_ref[:, LANES:] = _rope(y2, c_ref[...], a_ref[...], b_ref[...]).astype(k_ref.dtype)
        v_ref[...] = x_ref[:, LANES:].astype(v_ref.dtype)

    tab = pl.BlockSpec((ts, LANES), lambda i, h: (i, 0))
    return pl.pallas_call(
        body, name=name, out_shape=(_act((H, S, QK_PAD)), _act((H, S, LANES))), grid=(S // ts, H),
        in_specs=[pl.BlockSpec((ts, QK_PAD), lambda i, h: (i, h)), pl.BlockSpec((ts, LANES), lambda i, h: (i, pe_blk)),
                  pl.BlockSpec((1, QK_PAD), lambda i, h: (0, 0)), tab, tab, tab],
        out_specs=(pl.BlockSpec((None, ts, QK_PAD), lambda i, h: (h, i, 0)), pl.BlockSpec((None, ts, LANES), lambda i, h: (h, i, 0))),
        compiler_params=_params("parallel", "parallel"))(kvraw, lat, g, *tabs)


def mla_k_prep_bwd(dk, dv, kvraw, lat, g, tabs, H, pe_blk, name):
    S = kvraw.shape[0]
    ts = _tile(S, PREP_ROWS, 8)

    def body(dk_ref, dv_ref, x_ref, p_ref, g_ref, c_ref, a_ref, b_ref, o_ref, dp_ref, dg_ref):
        i, h = pl.program_id(0), pl.program_id(1)
        d2 = _rope_t(dk_ref[:, LANES:], c_ref[...], a_ref[...], b_ref[...])
        dx1, dx2, g1, g2 = _head_norm_bwd(x_ref[:, :LANES], p_ref[...], g_ref[...], dk_ref[:, :LANES], d2)
        o_ref[:, :LANES] = dx1.astype(o_ref.dtype)
        o_ref[:, LANES:] = dv_ref[...].astype(o_ref.dtype)
        _accumulate(dp_ref, dx2, h == 0)
        part = jnp.concatenate([jnp.sum(g1, axis=0, keepdims=True), jnp.sum(g2, axis=0, keepdims=True)], axis=1)
        _accumulate(dg_ref, part, jnp.logical_and(i == 0, h == 0))

    tab = pl.BlockSpec((ts, LANES), lambda i, h: (i, 0))
    vec = pl.BlockSpec((1, QK_PAD), lambda i, h: (0, 0))
    return pl.pallas_call(
        body, name=name, out_shape=(_act((S, H * QK_PAD)), _f32((S, LANES)), _f32((1, QK_PAD))), grid=(S // ts, H),
        in_specs=[pl.BlockSpec((None, ts, QK_PAD), lambda i, h: (h, i, 0)), pl.BlockSpec((None, ts, LANES), lambda i, h: (h, i, 0)),
                  pl.BlockSpec((ts, QK_PAD), lambda i, h: (i, h)), pl.BlockSpec((ts, LANES), lambda i, h: (i, pe_blk)),
                  vec, tab, tab, tab],
        out_specs=(pl.BlockSpec((ts, QK_PAD), lambda i, h: (i, h)), tab, vec),
        compiler_params=_params("arbitrary", "arbitrary"))(dk, dv, kvraw, lat, g, *tabs)


def _causal_scores(q, k, scale, diagonal):
    s = lax.dot_general(q, k, NT, preferred_element_type=jnp.float32) * scale
    if not diagonal:
        return s
    row = lax.broadcasted_iota(jnp.int32, s.shape, 0)
    col = lax.broadcasted_iota(jnp.int32, s.shape, 1)
    return jnp.where(col <= row, s, NEG)


def _on_causal_blocks(qi, ki, step):
    @pl.when(ki < qi)
    def _():
        step(False)

    @pl.when(ki == qi)
    def _():
        step(True)


def mla_attention_fwd(q, k, v, name):
    H, S, _ = q.shape
    t = _tile(S, ATTN_BLOCK)
    n = S // t
    scale = 1.0 / math.sqrt(QK_DIM)

    def body(q_ref, k_ref, v_ref, o_ref, lse_ref, m_sc, l_sc, acc):
        qi, ki = pl.program_id(1), pl.program_id(2)

        @pl.when(ki == 0)
        def _():
            m_sc[...] = jnp.full(m_sc.shape, NEG, jnp.float32)
            l_sc[...] = jnp.zeros_like(l_sc)
            acc[...] = jnp.zeros_like(acc)

        def step(diagonal):
            s = _causal_scores(q_ref[...], k_ref[...], scale, diagonal)
            m_new = jnp.maximum(m_sc[...], jnp.max(s, axis=-1, keepdims=True))
            alpha = jnp.exp(m_sc[...] - m_new)
            p = jnp.exp(s - m_new)
            l_sc[...] = alpha * l_sc[...] + jnp.sum(p, axis=-1, keepdims=True)
            acc[...] = alpha * acc[...] + jnp.dot(p.astype(MXU_DTYPE), v_ref[...], preferred_element_type=jnp.float32)
            m_sc[...] = m_new

        _on_causal_blocks(qi, ki, step)

        @pl.when(ki == qi)
        def _():
            o_ref[...] = (acc[...] / l_sc[...]).astype(o_ref.dtype)
            lse_ref[...] = m_sc[...] + jnp.log(l_sc[...])

    kv = lambda w: pl.BlockSpec((None, t, w), lambda h, qi, ki: (h, jnp.minimum(ki, qi), 0))
    return pl.pallas_call(
        body, name=name, out_shape=(_act((S, H * LANES)), _f32((H, S, 1))), grid=(H, n, n),
        in_specs=[pl.BlockSpec((None, t, QK_PAD), lambda h, qi, ki: (h, qi, 0)), kv(QK_PAD), kv(LANES)],
        out_specs=(pl.BlockSpec((t, LANES), lambda h, qi, ki: (qi, h)), pl.BlockSpec((None, t, 1), lambda h, qi, ki: (h, qi, 0))),
        scratch_shapes=[pltpu.VMEM((t, 1), jnp.float32), pltpu.VMEM((t, 1), jnp.float32), pltpu.VMEM((t, LANES), jnp.float32)],
        compiler_params=_params("parallel", "parallel", "arbitrary"))(q, k, v)


def attention_delta(do, o, H, name):
    S = do.shape[0]
    ts = _tile(S, 512, 8)

    def body(d_ref, o_ref, out_ref):
        out_ref[...] = jnp.sum(d_ref[...] * o_ref[...].astype(jnp.float32), axis=-1, keepdims=True)

    blk = pl.BlockSpec((ts, LANES), lambda i, h: (i, h))
    return pl.pallas_call(
        body, name=name, out_shape=_f32((H, S, 1)), grid=(S // ts, H), in_specs=[blk, blk],
        out_specs=pl.BlockSpec((None, ts, 1), lambda i, h: (h, i, 0)), compiler_params=_params("parallel", "parallel"))(do, o)


def mla_attention_bwd_dq(q, k, v, do, lse, delta, name):
    H, S, _ = q.shape
    t = _tile(S, ATTN_BLOCK)
    n = S // t
    scale = 1.0 / math.sqrt(QK_DIM)

    def body(q_ref, k_ref, v_ref, do_ref, lse_ref, dl_ref, dq_ref, acc):
        qi, ki = pl.program_id(1), pl.program_id(2)

        @pl.when(ki == 0)
        def _():
            acc[...] = jnp.zeros_like(acc)

        def step(diagonal):
            p = jnp.exp(_causal_scores(q_ref[...], k_ref[...], scale, diagonal) - lse_ref[...])
            dp = lax.dot_general(do_ref[...].astype(MXU_DTYPE), v_ref[...], NT, preferred_element_type=jnp.float32)
            ds = p * (dp - dl_ref[...])
            acc[...] += jnp.dot(ds.astype(MXU_DTYPE), k_ref[...], preferred_element_type=jnp.float32)

        _on_causal_blocks(qi, ki, step)

        @pl.when(ki == qi)
        def _():
            dq_ref[...] = acc[...] * scale

    kv = lambda w: pl.BlockSpec((None, t, w), lambda h, qi, ki: (h, jnp.minimum(ki, qi), 0))
    col = pl.BlockSpec((None, t, 1), lambda h, qi, ki: (h, qi, 0))
    qspec = pl.BlockSpec((None, t, QK_PAD), lambda h, qi, ki: (h, qi, 0))
    return pl.pallas_call(
        body, name=name, out_shape=_f32((H, S, QK_PAD)), grid=(H, n, n),
        in_specs=[qspec, kv(QK_PAD), kv(LANES), pl.BlockSpec((t, LANES), lambda h, qi, ki: (qi, h)), col, col],
        out_specs=qspec, scratch_shapes=[pltpu.VMEM((t, QK_PAD), jnp.float32)],
        compiler_params=_params("parallel", "parallel", "arbitrary"))(q, k, v, do, lse, delta)


def mla_attention_bwd_dkv(q, k, v, do, lse, delta, name):
    H, S, _ = q.shape
    t = _tile(S, ATTN_BLOCK)
    n = S // t
    scale = 1.0 / math.sqrt(QK_DIM)

    def body(q_ref, k_ref, v_ref, do_ref, lse_ref, dl_ref, dk_ref, dv_ref, dk_acc, dv_acc):
        ki, qi = pl.program_id(1), pl.program_id(2)

        @pl.when(qi == 0)
        def _():
            dk_acc[...] = jnp.zeros_like(dk_acc)
            dv_acc[...] = jnp.zeros_like(dv_acc)

        def step(diagonal):
            p = jnp.exp(_causal_scores(q_ref[...], k_ref[...], scale, diagonal) - lse_ref[...])
            dob = do_ref[...].astype(MXU_DTYPE)
            dv_acc[...] += lax.dot_general(p.astype(MXU_DTYPE), dob, TN, preferred_element_type=jnp.float32)
            dp = lax.dot_general(dob, v_ref[...], NT, preferred_element_type=jnp.float32)
            ds = p * (dp - dl_ref[...])
            dk_acc[...] += lax.dot_general(ds.astype(MXU_DTYPE), q_ref[...], TN, preferred_element_type=jnp.float32)

        _on_causal_blocks(qi, ki, step)

        @pl.when(qi == n - 1)
        def _():
            dk_ref[...] = dk_acc[...] * scale
            dv_ref[...] = dv_acc[...]

    qrow = lambda h, ki, qi: (h, jnp.maximum(qi, ki), 0)
    kv = lambda w: pl.BlockSpec((None, t, w), lambda h, ki, qi: (h, ki, 0))
    col = pl.BlockSpec((None, t, 1), qrow)
    return pl.pallas_call(
        body, name=name, out_shape=(_f32((H, S, QK_PAD)), _f32((H, S, LANES))), grid=(H, n, n),
        in_specs=[pl.BlockSpec((None, t, QK_PAD), qrow), kv(QK_PAD), kv(LANES),
                  pl.BlockSpec((t, LANES), lambda h, ki, qi: (jnp.maximum(qi, ki), h)), col, col],
        out_specs=(kv(QK_PAD), kv(LANES)),
        scratch_shapes=[pltpu.VMEM((t, QK_PAD), jnp.float32), pltpu.VMEM((t, LANES), jnp.float32)],
        compiler_params=_params("parallel", "parallel", "arbitrary"))(q, k, v, do, lse, delta)


def _alibi_slopes(G, Hd):
    k = np.arange(1, G * Hd + 1, dtype=np.float32)
    s = (2.0 ** (-8.0 * k / (G * Hd))).astype(np.float32).reshape(G, Hd)
    return jnp.asarray(np.broadcast_to(s[:, :, None, None], (G, Hd, 1, LANES)).copy())


def _dil_scores(qn, kn, scale, slope_d, prev, valid):
    s = lax.dot_general(qn, kn, NT, preferred_element_type=jnp.float32) * scale
    iq = lax.broadcasted_iota(jnp.int32, s.shape, 0)
    ik = lax.broadcasted_iota(jnp.int32, s.shape, 1)
    dist = iq - ik + (BLK if prev else 0)
    ok = (ik >= iq) if prev else (ik <= iq)
    s = s - slope_d * dist.astype(jnp.float32)
    return jnp.where(jnp.logical_and(ok, valid), s, NEG)


def _dil_specs(d, nblk, Hd, G, g):
    def spec(kind, shift):
        col0 = (kind * G + g) * Hd
        return pl.BlockSpec((BLK * d, LANES), lambda n, h: (jnp.clip(n + shift, 0, nblk - 1), col0 + h))
    return spec


def _head_spec(d, nblk, shift):
    return pl.BlockSpec((BLK * d, LANES), lambda n, h: (jnp.clip(n + shift, 0, nblk - 1), h))


def dilated_fwd(qkv, gq, gk, slopes, g, d, Hd, G, name):
    S, C = qkv.shape
    nblk = S // (BLK * d)
    scale = 1.0 / math.sqrt(LANES)
    spec = _dil_specs(d, nblk, Hd, G, g)

    def body(q_ref, kc_ref, kp_ref, vc_ref, vp_ref, gq_ref, gk_ref, sl_ref, o_ref, l_ref):
        n = pl.program_id(0)
        nrm = lambda t, gg: (t * _rstd(t, LANES) * gg).astype(MXU_DTYPE)
        slope_d = sl_ref[:, :1] * float(d)

        def residue(r, carry):
            rows = pl.ds(r, BLK, stride=d)
            qn = nrm(q_ref[rows, :], gq_ref[...])
            sc = _dil_scores(qn, nrm(kc_ref[rows, :], gk_ref[...]), scale, slope_d, False, True)
            sp = _dil_scores(qn, nrm(kp_ref[rows, :], gk_ref[...]), scale, slope_d, True, n > 0)
            m = jnp.maximum(jnp.max(sc, axis=-1, keepdims=True), jnp.max(sp, axis=-1, keepdims=True))
            lse = m + jnp.log(jnp.sum(jnp.exp(sc - m), axis=-1, keepdims=True) + jnp.sum(jnp.exp(sp - m), axis=-1, keepdims=True))
            o = jnp.dot(jnp.exp(sc - lse).astype(MXU_DTYPE), vc_ref[rows, :].astype(MXU_DTYPE), preferred_element_type=jnp.float32)
            o = o + jnp.dot(jnp.exp(sp - lse).astype(MXU_DTYPE), vp_ref[rows, :].astype(MXU_DTYPE), preferred_element_type=jnp.float32)
            o_ref[rows, :] = o
            l_ref[rows, :] = jnp.broadcast_to(lse, (BLK, LANES))
            return carry

        lax.fori_loop(0, d, residue, 0)

    vec = pl.BlockSpec((1, LANES), lambda n, h: (0, 0))
    out = _head_spec(d, nblk, 0)
    return pl.pallas_call(
        body, name=name, out_shape=(_f32((S, Hd * LANES)), _f32((S, Hd * LANES))), grid=(nblk, Hd),
        in_specs=[spec(0, 0), spec(1, 0), spec(1, -1), spec(2, 0), spec(2, -1), vec, vec,
                  pl.BlockSpec((None, None, 1, LANES), lambda n, h: (g, h, 0, 0))],
        out_specs=(out, out), compiler_params=_params("parallel", "parallel"),
    )(qkv, qkv, qkv, qkv, qkv, gq, gk, slopes)


def dilated_merge(os_, ls_, name):
    S, W = os_[0].shape
    G = len(os_)
    ts, tw = _tile(S, 512, 8), _tile(W, 512)

    def body(*refs):
        o_refs, l_refs, (o_ref, t_ref) = refs[:G], refs[G:2 * G], refs[2 * G:]
        ls = [r[...] for r in l_refs]
        m = ls[0]
        for l in ls[1:]:
            m = jnp.maximum(m, l)
        es = [jnp.exp(l - m) for l in ls]
        tot = es[0]
        for e in es[1:]:
            tot = tot + e
        acc = o_refs[0][...] * (es[0] / tot)
        for r, e in zip(o_refs[1:], es[1:]):
            acc = acc + r[...] * (e / tot)
        o_ref[...] = acc.astype(o_ref.dtype)
        t_ref[...] = m + jnp.log(tot)

    blk = pl.BlockSpec((ts, tw), lambda i, j: (i, j))
    return pl.pallas_call(
        body, name=name, out_shape=(_act((S, W)), _f32((S, W))), grid=(S // ts, W // tw),
        in_specs=[blk] * (2 * G), out_specs=(blk, blk), compiler_params=_params("parallel", "parallel"))(*os_, *ls_)


def dilated_delta(do, o, name):
    S, W = do.shape
    ts = _tile(S, 512, 8)

    def body(d_ref, o_ref, out_ref):
        out_ref[...] = jnp.broadcast_to(jnp.sum(d_ref[...] * o_ref[...].astype(jnp.float32), axis=-1, keepdims=True), out_ref.shape)

    blk = pl.BlockSpec((ts, LANES), lambda i, h: (i, h))
    return pl.pallas_call(body, name=name, out_shape=_f32((S, W)), grid=(S // ts, W // LANES), in_specs=[blk, blk],
                          out_specs=blk, compiler_params=_params("parallel", "parallel"))(do, o)


def dilated_bwd(qkv, do, lse, delta, gq, gk, slopes, g, d, Hd, G, name):
    S, C = qkv.shape
    nblk = S // (BLK * d)
    W = Hd * LANES
    scale = 1.0 / math.sqrt(LANES)
    spec = _dil_specs(d, nblk, Hd, G, g)
    hspec = lambda shift: _head_spec(d, nblk, shift)

    def body(q_ref, qx_ref, kc_ref, kp_ref, vc_ref, vp_ref, do_ref, dox_ref, l_ref, lx_ref, dl_ref, dlx_ref,
             gq_ref, gk_ref, sl_ref, dq_ref, dk_ref, dv_ref, dgq_ref, dgk_ref):
        n, h = pl.program_id(0), pl.program_id(1)
        gqv, gkv = gq_ref[...], gk_ref[...]
        nrm = lambda t, gg: (t * _rstd(t, LANES) * gg).astype(MXU_DTYPE)
        f32dot = lambda a, b, dn: lax.dot_general(a, b, dn, preferred_element_type=jnp.float32)
        slope_d = sl_ref[:, :1] * float(d)

        def residue(r, carry):
            rows = pl.ds(r, BLK, stride=d)
            q, kc = q_ref[rows, :], kc_ref[rows, :]
            qn, qxn = nrm(q, gqv), nrm(qx_ref[rows, :], gqv)
            kcn, kpn = nrm(kc, gkv), nrm(kp_ref[rows, :], gkv)
            vc, vp = vc_ref[rows, :].astype(MXU_DTYPE), vp_ref[rows, :].astype(MXU_DTYPE)
            dob, doxb = do_ref[rows, :].astype(MXU_DTYPE), dox_ref[rows, :].astype(MXU_DTYPE)
            lrow, lxrow = l_ref[rows, :][:, :1], lx_ref[rows, :][:, :1]
            drow, dxrow = dl_ref[rows, :][:, :1], dlx_ref[rows, :][:, :1]
            pc = jnp.exp(_dil_scores(qn, kcn, scale, slope_d, False, True) - lrow)
            pp = jnp.exp(_dil_scores(qn, kpn, scale, slope_d, True, n > 0) - lrow)
            dsc = pc * (f32dot(dob, vc, NT) - drow)
            dsp = pp * (f32dot(dob, vp, NT) - drow)
            dqn = (jnp.dot(dsc.astype(MXU_DTYPE), kcn, preferred_element_type=jnp.float32)
                   + jnp.dot(dsp.astype(MXU_DTYPE), kpn, preferred_element_type=jnp.float32)) * scale
            dq, dgq = _norm_bwd(q, gqv, dqn, LANES)
            dq_ref[rows, :] = dq
            px = jnp.exp(_dil_scores(qxn, kcn, scale, slope_d, True, n < nblk - 1) - lxrow)
            dsx = px * (f32dot(doxb, vc, NT) - dxrow)
            dkn = (f32dot(dsc.astype(MXU_DTYPE), qn, TN) + f32dot(dsx.astype(MXU_DTYPE), qxn, TN)) * scale
            dk, dgk = _norm_bwd(kc, gkv, dkn, LANES)
            dk_ref[rows, :] = dk
            dv_ref[rows, :] = f32dot(pc.astype(MXU_DTYPE), dob, TN) + f32dot(px.astype(MXU_DTYPE), doxb, TN)
            return carry[0] + jnp.sum(dgq, axis=0, keepdims=True), carry[1] + jnp.sum(dgk, axis=0, keepdims=True)

        zero = jnp.zeros((1, LANES), jnp.float32)
        dgq_sum, dgk_sum = lax.fori_loop(0, d, residue, (zero, zero))
        first = jnp.logical_and(n == 0, h == 0)
        _accumulate(dgq_ref, dgq_sum, first)
        _accumulate(dgk_ref, dgk_sum, first)

    vec = pl.BlockSpec((1, LANES), lambda n, h: (0, 0))
    out = hspec(0)
    return pl.pallas_call(
        body, name=name, out_shape=(_f32((S, W)), _f32((S, W)), _f32((S, W)), _f32((1, LANES)), _f32((1, LANES))),
        grid=(nblk, Hd),
        in_specs=[spec(0, 0), spec(0, 1), spec(1, 0), spec(1, -1), spec(2, 0), spec(2, -1), hspec(0), hspec(1), hspec(0), hspec(1),
                  hspec(0), hspec(1), vec, vec, pl.BlockSpec((None, None, 1, LANES), lambda n, h: (g, h, 0, 0))],
        out_specs=(out, out, out, vec, vec), compiler_params=_params("arbitrary", "arbitrary"),
    )(qkv, qkv, qkv, qkv, qkv, qkv, do, do, lse, lse, delta, delta, gq, gk, slopes)


def adamw(w, g, m, v, layer, prev, name):
    L, r, c = w.shape
    tr, tc = _tile(r, 512, 8), _tile(c, 1024)
    c1 = 1.0 / (1.0 - ADAM_B1 ** ADAM_STEP)
    c2 = 1.0 / (1.0 - ADAM_B2 ** ADAM_STEP)

    def body(*refs):
        w_ref, g_ref, m_ref, v_ref = refs[:4]
        go_ref, d_ref, mo_ref, vo_ref = refs[-4:]
        gv = g_ref[...]
        mn = ADAM_B1 * m_ref[...] + (1.0 - ADAM_B1) * gv
        vn = ADAM_B2 * v_ref[...] + (1.0 - ADAM_B2) * (gv * gv)
        go_ref[...] = gv
        d_ref[...] = -ADAM_LR * ((mn * c1) / (jnp.sqrt(vn * c2) + ADAM_EPS) + ADAM_WD * w_ref[...])
        mo_ref[...] = mn
        vo_ref[...] = vn

    lay = pl.BlockSpec((None, tr, tc), lambda i, j: (layer, i, j))
    flat = pl.BlockSpec((tr, tc), lambda i, j: (i, j))
    ins = [w, g, m, v] + (list(prev) if prev is not None else [])
    in_specs = [lay, flat, lay, lay] + ([ANY] * 4 if prev is not None else [])
    return pl.pallas_call(
        body, name=name, out_shape=tuple(_f32((L, r, c)) for _ in range(4)), grid=(r // tr, c // tc),
        in_specs=in_specs, out_specs=(lay, lay, lay, lay),
        input_output_aliases=({4 + k: k for k in range(4)} if prev is not None else {}),
        compiler_params=_params("parallel", "parallel"))(*ins)


def _ffn_fwd(h, g, W, kind, tag):
    xn = rmsnorm_fwd(h, g, tag + "_norm")
    u, a = swiglu_in(xn, W(kind + "_w_in", h), tag + "_in")
    out = matmul(a, W(kind + "_w_out", a), scale=0.5, res=h, name=tag + "_out", tk=2816)
    return out, (h, xn, u, a)


def _ffn_bwd(dout, saved, g, W, emit, kind, tag):
    h, xn, u, a = saved
    emit(kind + "_w_out", matmul(a, dout, ta=True, scale=0.5, out_dtype=WIRE_DTYPE, out_axis=0, name=tag + "_dwout", tm=1408, tk=2048))
    du = Sharded(swiglu_out_bwd(dout, W(kind + "_w_out", None), u, 0.5, tag + "_da"), 0, 1)
    emit(kind + "_w_in", matmul(xn, du, ta=True, out_dtype=WIRE_DTYPE, out_axis=1, name=tag + "_dwin", tn=1408, tk=2048))
    dxn = matmul(du, W(kind + "_w_in", None), tb=True, name=tag + "_dxn", tk=2816)
    return rmsnorm_bwd(h, g, dxn, dout, tag + "_dnorm")


def _pad_gain(g):
    return jnp.pad(g, ((0, 0), (0, QK_PAD - QK_DIM)))


def _mla_fwd(h, P, W, tabs, H):
    g_mix, g_cq, g_ckv = P["mix_norm"][0:1], P["mla_g_cq"], P["mla_g_ckv"]
    pe_blk = (g_cq.shape[1] + g_ckv.shape[1]) // LANES
    xn = rmsnorm_fwd(h, g_mix, "mla_norm")
    w_down = W("mla_w_down", h)
    lat = matmul(xn, w_down, name="mla_down", tn=w_down.shape[1])
    cq, ckv = mla_latent_fwd(lat, g_cq, g_ckv, "mla_latent")
    qraw = matmul(cq, W("mla_w_uq", lat), name="mla_uq")
    kvraw = matmul(ckv, W("mla_w_ukv", qraw), name="mla_ukv")
    q = mla_q_prep_fwd(qraw, _pad_gain(P["mla_g_qn"]), tabs, H, "mla_qprep")
    k, v = mla_k_prep_fwd(kvraw, lat, _pad_gain(P["mla_g_kn"]), tabs, H, pe_blk, "mla_kprep")
    o, lse = mla_attention_fwd(q, k, v, "mla_attn")
    out = matmul(o, W("mla_w_o", lse), res=h, name="mla_o")
    return out, (h, xn, lat, cq, ckv, qraw, kvraw, q, k, v, o, lse, pe_blk)


def _mla_bwd(dout, saved, P, W, emit, tabs, H):
    h, xn, lat, cq, ckv, qraw, kvraw, q, k, v, o, lse, pe_blk = saved
    emit("mla_w_o", matmul(o, dout, ta=True, out_dtype=WIRE_DTYPE, out_axis=0, name="mla_dwo", tm=512))
    do = matmul(dout, W("mla_w_o", None), tb=True, name="mla_do", tn=512)
    delta = attention_delta(do, o, H, "mla_delta")
    dq = mla_attention_bwd_dq(q, k, v, do, lse, delta, "mla_attn_dq")
    dk, dv = mla_attention_bwd_dkv(q, k, v, do, lse, delta, "mla_attn_dkv")
    dqraw, dgq = mla_q_prep_bwd(dq, qraw, _pad_gain(P["mla_g_qn"]), tabs, H, "mla_dqprep")
    dkvraw, dkpe, dgk = mla_k_prep_bwd(dk, dv, kvraw, lat, _pad_gain(P["mla_g_kn"]), tabs, H, pe_blk, "mla_dkprep")
    emit("mla_w_uq", matmul(cq, dqraw, ta=True, out_dtype=WIRE_DTYPE, out_axis=1, name="mla_dwuq"))
    dcq = matmul(dqraw, W("mla_w_uq", None), tb=True, name="mla_dcq", tk=1024)
    emit("mla_w_ukv", matmul(ckv, dkvraw, ta=True, out_dtype=WIRE_DTYPE, out_axis=1, name="mla_dwukv"))
    dckv = matmul(dkvraw, W("mla_w_ukv", None), tb=True, name="mla_dckv", tk=1024)
    dlat, dgcq, dgckv = mla_latent_bwd(dcq, dckv, dkpe, lat, P["mla_g_cq"], P["mla_g_ckv"], "mla_dlatent")
    emit("mla_w_down", matmul(xn, dlat, ta=True, out_dtype=WIRE_DTYPE, out_axis=0, name="mla_dwdown", tm=512, tn=dlat.shape[1]))
    dxn = matmul(dlat, W("mla_w_down", None), tb=True, name="mla_dxn", tn=512, tk=dlat.shape[1])
    dh, dgm = rmsnorm_bwd(h, P["mix_norm"][0:1], dxn, dout, "mla_dnorm")
    return dh, dgm, dict(mla_g_qn=dgq[:, :QK_DIM], mla_g_kn=dgk[:, :QK_DIM], mla_g_cq=dgcq, mla_g_ckv=dgckv)


def _dil_fwd(h, P, W, slopes, Hd):
    G = len(DIL_PAIRS)
    xn = rmsnorm_fwd(h, P["mix_norm"][1:2], "dil_norm")
    qkv = matmul(xn, W("dil_w_qkv", h), name="dil_qkv", tn=1152)
    os_, ls_ = [], []
    for g, (_, d) in enumerate(DIL_PAIRS):
        o_g, l_g = dilated_fwd(qkv, P["dil_g_qn"], P["dil_g_kn"], slopes, g, d, Hd, G, f"dil_attn{g}")
        os_.append(o_g)
        ls_.append(l_g)
    o, lse = dilated_merge(os_, ls_, "dil_merge")
    out = matmul(o, W("dil_w_o", lse), res=h, name="dil_o", tn=512)
    return out, (h, xn, qkv, o, lse)


def _dil_bwd(dout, saved, P, W, emit, slopes, Hd):
    h, xn, qkv, o, lse = saved
    ngrp = len(DIL_PAIRS)
    emit("dil_w_o", matmul(o, dout, ta=True, out_dtype=WIRE_DTYPE, out_axis=1, name="dil_dwo", tn=512))
    do = matmul(dout, W("dil_w_o", None), tb=True, name="dil_do", tk=512)
    delta = dilated_delta(do, o, "dil_delta")
    parts = [dilated_bwd(qkv, do, lse, delta, P["dil_g_qn"], P["dil_g_kn"], slopes, g, d, Hd, ngrp, f"dil_dattn{g}")
             for g, (_, d) in enumerate(DIL_PAIRS)]
    dqkv = jnp.concatenate([p[kind] for kind in range(3) for p in parts], axis=1).astype(MXU_DTYPE)
    emit("dil_w_qkv", matmul(xn, dqkv, ta=True, out_dtype=WIRE_DTYPE, out_axis=1, name="dil_dwqkv", tn=1152))
    dxn = matmul(dqkv, W("dil_w_qkv", None), tb=True, name="dil_dxn", tk=1152)
    dh, dgm = rmsnorm_bwd(h, P["mix_norm"][1:2], dxn, dout, "dil_dnorm")
    return dh, dgm, dict(dil_g_qn=parts[0][3] + parts[1][3] + parts[2][3], dil_g_kn=parts[0][4] + parts[1][4] + parts[2][4])


def local_step(x, target, P, get_w, on_grad):
    S, D = x.shape
    H, Hd = MLA_HEADS, DIL_HEADS
    tabs = rope_tables(S)
    slopes = _alibi_slopes(len(DIL_PAIRS), Hd)
    cache = {}

    def weights_of(layer):
        def W(name, after):
            if (name, layer) not in cache:
                cache[name, layer] = Sharded(get_w(name, layer, after), 0, SHARD_AXIS[name])
            return cache[name, layer]
        return W

    row = lambda name, i: P[name][i:i + 1]
    h = x
    saved = []
    for i in range(2):
        W = weights_of(i)
        h, s1 = _ffn_fwd(h, row("ffn1_norm", i), W, "ffn1", f"l{i}_ffn1")
        h, sm = _mla_fwd(h, P, weights_of(0), tabs, H) if i == 0 else _dil_fwd(h, P, weights_of(0), slopes, Hd)
        h, s2 = _ffn_fwd(h, row("ffn2_norm", i), W, "ffn2", f"l{i}_ffn2")
        saved.append((s1, sm, s2))
    loss, dh = loss_head(h, target, "loss")

    gs = {n: [None, None] for n in ("ffn1_norm", "mix_norm", "ffn2_norm")}
    for i in (1, 0):
        s1, sm, s2 = saved[i]
        W = weights_of(i)
        emit = lambda name, g4, layer=i: on_grad(name, layer, g4)
        emit0 = lambda name, g4: on_grad(name, 0, g4)
        dh, gs["ffn2_norm"][i] = _ffn_bwd(dh, s2, row("ffn2_norm", i), W, emit, "ffn2", f"l{i}_ffn2")
        if i == 0:
            dh, gs["mix_norm"][i], gm = _mla_bwd(dh, sm, P, weights_of(0), emit0, tabs, H)
        else:
            dh, gs["mix_norm"][i], gm = _dil_bwd(dh, sm, P, weights_of(0), emit0, slopes, Hd)
        gs.update({n: [val] for n, val in gm.items()})
        dh, gs["ffn1_norm"][i] = _ffn_bwd(dh, s1, row("ffn1_norm", i), W, emit, "ffn1", f"l{i}_ffn1")
    gsmall = {n: jnp.concatenate(v, axis=0) for n, v in gs.items()}
    return loss, dh, gsmall


def _pad_heads(w, real, padded):
    lead, n = w.shape[:-1], w.shape[-1] // real
    w = jnp.pad(w.reshape(*lead, n, real), [(0, 0)] * (len(lead) + 1) + [(0, padded - real)])
    return w.reshape(*lead, n * padded)


def _unpad_heads(w, real, padded):
    lead, n = w.shape[:-1], w.shape[-1] // padded
    return w.reshape(*lead, n, padded)[..., :real].reshape(*lead, n * real)


def _pack_small(gs):
    flat = jnp.concatenate([gs[n].reshape(-1) for n in SMALL])
    rows = -(-flat.shape[0] // LANES)
    rows = -(-rows // 8) * 8
    return jnp.pad(flat, (0, rows * LANES - flat.shape[0])).reshape(rows, LANES)


def _unpack_small(packed, like):
    flat, out, off = packed.reshape(-1), {}, 0
    for n in SMALL:
        size = int(np.prod(like[n].shape))
        out[n] = flat[off:off + size].reshape(like[n].shape)
        off += size
    return out


def kernel(x, ffn1_norm, ffn1_w_in, ffn1_w_out, mix_norm, ffn2_norm, ffn2_w_in, ffn2_w_out, mla_w_down, mla_g_cq, mla_g_ckv, mla_w_uq, mla_w_ukv, mla_g_qn, mla_g_kn, mla_w_o, dil_w_qkv, dil_g_qn, dil_g_kn, dil_w_o, loss_target, m_ffn1_norm, m_ffn1_w_in, m_ffn1_w_out, m_mix_norm, m_ffn2_norm, m_ffn2_w_in, m_ffn2_w_out, m_mla_w_down, m_mla_g_cq, m_mla_g_ckv, m_mla_w_uq, m_mla_w_ukv, m_mla_g_qn, m_mla_g_kn, m_mla_w_o, m_dil_w_qkv, m_dil_g_qn, m_dil_g_kn, m_dil_w_o, v_ffn1_norm, v_ffn1_w_in, v_ffn1_w_out, v_mix_norm, v_ffn2_norm, v_ffn2_w_in, v_ffn2_w_out, v_mla_w_down, v_mla_g_cq, v_mla_g_ckv, v_mla_w_uq, v_mla_w_ukv, v_mla_g_qn, v_mla_g_kn, v_mla_w_o, v_dil_w_qkv, v_dil_g_qn, v_dil_g_kn, v_dil_w_o):
    args = dict(locals())
    w = {n: args[n] for n in WEIGHTS}
    m = {n: args["m_" + n] for n in WEIGHTS}
    v = {n: args["v_" + n] for n in WEIGHTS}
    cx, cy, cc = _me()
    core = jnp.reshape(cc, (1,)).astype(jnp.int32)
    shard = jnp.reshape(2 * cx + cy, (1,)).astype(jnp.int32)
    place = (shard, core)
    pe_pad = LANES - ROPE_DIM

    order = [(n, layer if w[n].shape[0] > 1 else 0) for layer in range(2) for n in USE_ORDER[layer]]
    lands, started = {}, {}

    def cast(key, after):
        _, r, c = w[key[0]].shape
        lands[key] = cast_into_shards(w[key[0]], key[1], shard, after, "ag_%s%d_cast" % key).reshape(N_CHIPS, 1, 2, r // 2, c)
        return lands[key]

    def start(key, after):
        started[key] = exchange_start((lands.pop(key),), _ag_mine, _ag_mine, after, "ag_%s%d_start" % key)
        return started[key][2]

    def get_w(n, l, after):
        k = order.index((n, l))
        behind = after
        if k == 0:
            for key in order[:2]:
                cast(key, after)
            for key in order[:2]:
                behind = start(key, after)
            for key in order[2:]:
                behind = cast(key, behind)
        else:
            for key in order[len(started):min(k + AG_AHEAD, len(order) - 1) + 1]:
                behind = start(key, after)
        sems, bufs, _ = started[n, l]
        _, _, _, h, c = bufs[0].shape
        (land,) = exchange_wait(sems, bufs, _ag_mine, _ag_got, behind, f"ag_{n}{l}_wait")
        full = all_gather_finish(land, f"ag_{n}{l}_finish").reshape(N_CHIPS, 1, 2 * h, c)
        if n == "mla_w_down":
            full = jnp.pad(full, ((0, 0), (0, 0), (0, 0), (0, pe_pad)))
        if n == "mla_w_uq":
            full = _pad_heads(full, QK_DIM, QK_PAD)
        return full

    pending = []
    outs = {n: None for n in BIG}
    last_token = [None]

    def finish_oldest(after):
        n, l, pend = pending.pop(0)
        g = reduce_scatter_finish(pend, place, after, f"rs_{n}{l}")
        outs[n] = adamw(w[n], g, m[n], v[n], l, outs[n], f"adamw_{n}{l}")

    def on_grad(n, l, g4):
        if n == "mla_w_down":
            g4 = g4[..., :g4.shape[-1] - pe_pad]
        if n == "mla_w_uq":
            g4 = _unpad_heads(g4, QK_DIM, QK_PAD)
        after = last_token[0] if last_token[0] is not None else g4
        pending.append((n, l, reduce_scatter_start(g4, core, after, f"rs_{n}{l}")))
        last_token[0] = pending[-1][2][2]
        if len(pending) > RS_WINDOW:
            finish_oldest(last_token[0])

    loss, grad_x, gsmall = local_step(x[0], loss_target[0], {n: w[n] for n in SMALL}, get_w, on_grad)
    loss = lax.psum(loss, ("x", "y", "c"))
    while pending:
        finish_oldest(grad_x)
    small = _unpack_small(all_reduce_small(_pack_small(gsmall), "ar_small"), gsmall)
    for n in SMALL:
        outs[n] = tuple(o[0] for o in adamw(w[n][None], small[n], m[n][None], v[n][None], 0, None, f"adamw_{n}"))

    return (loss, grad_x[None], *[outs[n][0] for n in WEIGHTS], *[outs[n][1] for n in WEIGHTS],
            *[outs[n][2] for n in WEIGHTS], *[outs[n][3] for n in WEIGHTS])
```

```python
import math

import numpy as np
import jax
import jax.numpy as jnp
from jax import lax
from jax.experimental import pallas as pl
from jax.experimental.pallas import tpu as pltpu

MXU_DTYPE = jnp.bfloat16
WIRE_DTYPE = jnp.bfloat16
EPS = 1e-6
NEG = -1e30
N_CHIPS = 4
MESH = pl.DeviceIdType.MESH
ANY = pl.BlockSpec(memory_space=pl.ANY)
LANES = 128

MLA_HEADS = 16
NOPE_DIM = 128
ROPE_DIM = 64
QK_DIM = NOPE_DIM + ROPE_DIM
QK_PAD = 2 * LANES
PREP_ROWS = 1024
ATTN_BLOCK = 1024
AG_AHEAD = 2
RS_WINDOW = 2
ROPE_THETA = 10000.0
DIL_PAIRS = ((128, 1), (512, 4), (2048, 16))
DIL_HEADS = 8
BLK = 128

ADAM_LR = 0.001
ADAM_B1 = 0.9
ADAM_B2 = 0.999
ADAM_EPS = 1e-08
ADAM_WD = 0.01
ADAM_STEP = 10

NT = (((1,), (1,)), ((), ()))
TN = (((0,), (0,)), ((), ()))

SHARD_AXIS = {"ffn1_w_in": 1, "ffn1_w_out": 0, "ffn2_w_in": 1, "ffn2_w_out": 0, "mla_w_down": 0, "mla_w_uq": 1,
              "mla_w_ukv": 1, "mla_w_o": 0, "dil_w_qkv": 1, "dil_w_o": 1}
BIG = tuple(SHARD_AXIS)
USE_ORDER = (("ffn1_w_in", "ffn1_w_out", "mla_w_down", "mla_w_uq", "mla_w_ukv", "mla_w_o", "ffn2_w_in", "ffn2_w_out"),
             ("ffn1_w_in", "ffn1_w_out", "dil_w_qkv", "dil_w_o", "ffn2_w_in", "ffn2_w_out"))
SMALL = ("ffn1_norm", "mix_norm", "ffn2_norm", "mla_g_cq", "mla_g_ckv", "mla_g_qn", "mla_g_kn", "dil_g_qn", "dil_g_kn")
WEIGHTS = ("ffn1_norm", "ffn1_w_in", "ffn1_w_out", "mix_norm", "ffn2_norm", "ffn2_w_in", "ffn2_w_out", "mla_w_down",
           "mla_g_cq", "mla_g_ckv", "mla_w_uq", "mla_w_ukv", "mla_g_qn", "mla_g_kn", "mla_w_o", "dil_w_qkv", "dil_g_qn",
           "dil_g_kn", "dil_w_o")


def _tile(dim, pref, mult=LANES):
    if dim <= pref:
        return dim
    t = (pref // mult) * mult
    while t >= mult:
        if dim % t == 0:
            return t
        t -= mult
    return dim


def _params(*sem):
    return pltpu.CompilerParams(dimension_semantics=sem)


def _f32(shape):
    return jax.ShapeDtypeStruct(shape, jnp.float32)


def _act(shape):
    return jax.ShapeDtypeStruct(shape, MXU_DTYPE)


class Sharded:
    def __init__(self, arr, layer, axis):
        self.arr, self.layer, self.axis = arr, layer, axis
        n, _, r, c = arr.shape
        self.shape = (n * r, c) if axis == 0 else (r, n * c)
        self.per = r if axis == 0 else c

    def spec(self, tr, tc, rc_of):
        l = self.layer
        if self.axis == 0:
            n = self.per // tr

            def imap(*g):
                bi, bj = rc_of(*g)
                return (bi // n, l, bi % n, bj)
        else:
            n = self.per // tc

            def imap(*g):
                bi, bj = rc_of(*g)
                return (bj // n, l, bi, bj % n)
        return pl.BlockSpec((None, None, tr, tc), imap)


def _spec2(tr, tc, rc_of):
    return pl.BlockSpec((tr, tc), lambda *g: rc_of(*g))


def matmul(a, b, *, ta=False, tb=False, out_dtype=jnp.float32, scale=None, res=None, out_axis=None,
           name, tm=1024, tn=1024, tk=512):
    am, ak = (a.shape[1], a.shape[0]) if ta else a.shape
    bk, bn = (b.shape[1], b.shape[0]) if tb else b.shape
    assert ak == bk, (name, a.shape, b.shape, ta, tb)
    M, N, K = am, bn, ak

    def per(x, axis):
        return x.per if isinstance(x, Sharded) and x.axis == axis else None

    def pick(dim, pref, *pers):
        return _tile(math.gcd(dim, *[p for p in pers if p is not None]), pref)

    tm = pick(M, tm, per(a, 1 if ta else 0), M // N_CHIPS if out_axis == 0 else None)
    tn = pick(N, tn, per(b, 0 if tb else 1), N // N_CHIPS if out_axis == 1 else None)
    tk = pick(K, tk, per(a, 0 if ta else 1), per(b, 1 if tb else 0))
    assert M % tm == 0 and N % tn == 0 and K % tk == 0, (name, M, N, K, tm, tn, tk)
    nk = K // tk

    a_rc = (lambda i, j, k: (k, i)) if ta else (lambda i, j, k: (i, k))
    b_rc = (lambda i, j, k: (j, k)) if tb else (lambda i, j, k: (k, j))
    a_blk = (tk, tm) if ta else (tm, tk)
    b_blk = (tn, tk) if tb else (tk, tn)
    a_spec = a.spec(*a_blk, a_rc) if isinstance(a, Sharded) else _spec2(*a_blk, a_rc)
    b_spec = b.spec(*b_blk, b_rc) if isinstance(b, Sharded) else _spec2(*b_blk, b_rc)
    dn = (((0 if ta else 1,), (1 if tb else 0,)), ((), ()))
    has_res = res is not None

    def body(*refs):
        if has_res:
            a_ref, b_ref, r_ref, o_ref, acc = refs
        else:
            a_ref, b_ref, o_ref, acc = refs
        k = pl.program_id(2)

        @pl.when(k == 0)
        def _():
            acc[...] = jnp.zeros_like(acc)

        acc[...] += lax.dot_general(a_ref[...].astype(MXU_DTYPE), b_ref[...].astype(MXU_DTYPE), dn,
                                    preferred_element_type=jnp.float32)

        @pl.when(k == nk - 1)
        def _():
            r = acc[...]
            if scale is not None:
                r = r * scale
            if has_res:
                r = r + r_ref[...]
            o_ref[...] = r.astype(o_ref.dtype)

    in_specs = [a_spec, b_spec]
    args = [a.arr if isinstance(a, Sharded) else a, b.arr if isinstance(b, Sharded) else b]
    if has_res:
        in_specs.append(_spec2(tm, tn, lambda i, j, k: (i, j)))
        args.append(res)
    o_rc = lambda i, j, k: (i, j)
    if out_axis is None:
        out_shape = jax.ShapeDtypeStruct((M, N), out_dtype)
        out_spec = _spec2(tm, tn, o_rc)
    else:
        shp = (N_CHIPS, 1, M // N_CHIPS, N) if out_axis == 0 else (N_CHIPS, 1, M, N // N_CHIPS)
        out_shape = jax.ShapeDtypeStruct(shp, out_dtype)
        out_spec = Sharded(out_shape, 0, out_axis).spec(tm, tn, o_rc)
    return pl.pallas_call(
        body, name=name, out_shape=out_shape, grid=(M // tm, N // tn, nk),
        in_specs=in_specs, out_specs=out_spec,
        scratch_shapes=[pltpu.VMEM((tm, tn), jnp.float32)],
        compiler_params=_params("parallel", "parallel", "arbitrary"),
    )(*args)


def _me():
    return lax.axis_index("x"), lax.axis_index("y"), lax.axis_index("c")


def _other_chips(x, y):
    return [(1 - x, y), (x, 1 - y), (1 - x, 1 - y)]


HBM = pl.BlockSpec(memory_space=pltpu.HBM)
SEM = pl.BlockSpec(memory_space=pltpu.SEMAPHORE)
N_PEERS = 3
TOKEN = jax.ShapeDtypeStruct((8, LANES), jnp.float32)


def _split_params():
    return pltpu.CompilerParams(has_side_effects=pltpu.SideEffectType.DATAFLOW_SIDE_EFFECTING)


def _in_hbm(a):
    return pltpu.with_memory_space_constraint(a, pltpu.HBM)


def exchange_start(bufs, src_of, dst_of, after, name):
    nb = len(bufs)

    def body(*refs):
        src_ref, land_ref = refs[0], refs[nb - 1]
        sems, token = refs[nb + 1:nb + 1 + 2 * N_PEERS], refs[-1]
        x, y, cc = _me()
        for j, (px, py) in enumerate(_other_chips(x, y)):
            pltpu.make_async_remote_copy(
                src_ref=src_of(src_ref, j, (px, py), (x, y, cc)), dst_ref=dst_of(land_ref, j, (px, py), (x, y, cc)),
                send_sem=sems[j], recv_sem=sems[N_PEERS + j], device_id=(px, py, cc), device_id_type=MESH).start()
        token[...] = jnp.zeros_like(token)

    outs = pl.pallas_call(
        body, name=name,
        out_shape=(pltpu.SemaphoreType.DMA(()),) * (2 * N_PEERS) + tuple(pltpu.HBM(b.shape, b.dtype) for b in bufs) + (TOKEN,),
        in_specs=(HBM,) * nb + (ANY,), out_specs=(SEM,) * (2 * N_PEERS) + (HBM,) * nb + (pl.BlockSpec(memory_space=pltpu.VMEM),),
        input_output_aliases={k: 2 * N_PEERS + k for k in range(nb)}, compiler_params=_split_params(),
    )(*[_in_hbm(b) for b in bufs], after)
    return outs[:2 * N_PEERS], outs[2 * N_PEERS:2 * N_PEERS + nb], outs[-1]


def exchange_wait(sems, bufs, src_of, got_of, after, name):
    nb = len(bufs)

    def body(*refs):
        src_ref, land_ref = refs[0], refs[nb - 1]
        sems_ = refs[nb:nb + 2 * N_PEERS]
        x, y, cc = _me()
        for j, (px, py) in enumerate(_other_chips(x, y)):
            cp = pltpu.make_async_remote_copy(
                src_ref=src_of(src_ref, j, (px, py), (x, y, cc)), dst_ref=got_of(land_ref, j, (px, py), (x, y, cc)),
                send_sem=sems_[j], recv_sem=sems_[N_PEERS + j], device_id=(px, py, cc), device_id_type=MESH)
            cp.wait_send()
            cp.wait_recv()

    return pl.pallas_call(
        body, name=name, out_shape=tuple(pltpu.HBM(b.shape, b.dtype) for b in bufs),
        in_specs=(HBM,) * nb + (SEM,) * (2 * N_PEERS) + (ANY,), out_specs=(HBM,) * nb,
        input_output_aliases={k: k for k in range(nb)}, compiler_params=_split_params(),
    )(*bufs, *sems, after)


def cast_into_shards(w, layer, shard, after, name):
    L, r, c = w.shape
    tr, tc = _tile(r, 512, 16), _tile(c, 1024)

    def body(shard_ref, w_ref, after_ref, o_ref):
        o_ref[...] = w_ref[...].astype(o_ref.dtype)

    grid_spec = pltpu.PrefetchScalarGridSpec(
        num_scalar_prefetch=1, grid=(r // tr, c // tc),
        in_specs=[pl.BlockSpec((None, tr, tc), lambda i, j, sh: (layer, i, j)), ANY],
        out_specs=pl.BlockSpec((None, None, tr, tc), lambda i, j, sh: (sh[0], 0, i, j)))
    return pl.pallas_call(body, name=name, grid_spec=grid_spec, out_shape=jax.ShapeDtypeStruct((N_CHIPS, 1, r, c), WIRE_DTYPE),
                          compiler_params=_params("parallel", "parallel"))(shard, w, after)


def _ag_mine(ref, j, chip, me):
    return ref.at[2 * me[0] + me[1], :, me[2]]


def _ag_got(ref, j, chip, me):
    return ref.at[2 * chip[0] + chip[1], :, me[2]]


def all_gather_finish(land, name):
    def body(land_ref, o_ref, send_sems, recv_sems):
        x, y, cc = _me()
        cps = []
        for j, (px, py) in enumerate(_other_chips(x, y)):
            cp = pltpu.make_async_remote_copy(
                src_ref=o_ref.at[2 * px + py, :, cc], dst_ref=o_ref.at[2 * px + py, :, cc], send_sem=send_sems.at[j],
                recv_sem=recv_sems.at[j], device_id=(x, y, 1 - cc), device_id_type=MESH)
            cp.start()
            cps.append(cp)
        for j, (px, py) in enumerate(_other_chips(x, y)):
            got = o_ref.at[2 * px + py, :, 1 - cc]
            pltpu.make_async_remote_copy(src_ref=got, dst_ref=got, send_sem=send_sems.at[j], recv_sem=recv_sems.at[j],
                                         device_id=(x, y, 1 - cc), device_id_type=MESH).wait_recv()
        for cp in cps:
            cp.wait_send()

    return pl.pallas_call(
        body, name=name, out_shape=jax.ShapeDtypeStruct(land.shape, land.dtype), in_specs=[ANY], out_specs=ANY,
        input_output_aliases={0: 0},
        scratch_shapes=[pltpu.SemaphoreType.DMA((N_PEERS,)), pltpu.SemaphoreType.DMA((N_PEERS,))],
    )(land)


def _rs_src(ref, j, chip, me):
    return ref.at[2 * chip[0] + chip[1]]


def _rs_dst(ref, j, chip, me):
    return ref.at[j]


def sibling_send_halves(g, after, name):
    n, L, two, h, c = g.shape

    def body(g_ref, after_ref, o_ref, send_sem, recv_sem):
        x, y, cc = _me()
        cp = pltpu.make_async_remote_copy(src_ref=g_ref.at[:, :, 1 - cc], dst_ref=o_ref, send_sem=send_sem,
                                          recv_sem=recv_sem, device_id=(x, y, 1 - cc), device_id_type=MESH)
        cp.start()
        cp.wait()

    return pl.pallas_call(
        body, name=name, out_shape=jax.ShapeDtypeStruct((n, L, h, c), g.dtype),
        in_specs=[ANY, ANY], out_specs=ANY,
        scratch_shapes=[pltpu.SemaphoreType.DMA, pltpu.SemaphoreType.DMA],
    )(g, after)


def sibling_gather_halves(r, name):
    def body(r_ref, o_ref, send_sem, recv_sem):
        x, y, cc = _me()
        cp = pltpu.make_async_remote_copy(src_ref=o_ref.at[:, cc], dst_ref=o_ref.at[:, cc], send_sem=send_sem,
                                          recv_sem=recv_sem, device_id=(x, y, 1 - cc), device_id_type=MESH)
        cp.start()
        cp.wait()

    return pl.pallas_call(
        body, name=name, out_shape=jax.ShapeDtypeStruct(r.shape, r.dtype), in_specs=[ANY], out_specs=ANY,
        input_output_aliases={0: 0}, scratch_shapes=[pltpu.SemaphoreType.DMA, pltpu.SemaphoreType.DMA],
    )(r)


def add_sibling(g, r1, core, name):
    n, L, two, h, c = g.shape
    th = _tile(h, 512, 16)
    tc = _tile(c, 1024)

    def body(core_ref, g_ref, r_ref, o_ref):
        o_ref[...] = (g_ref[...].astype(jnp.float32) + r_ref[...].astype(jnp.float32)).astype(o_ref.dtype)

    grid_spec = pltpu.PrefetchScalarGridSpec(
        num_scalar_prefetch=1, grid=(n, L, h // th, c // tc),
        in_specs=[pl.BlockSpec((None, None, None, th, tc), lambda s, l, i, j, core: (s, l, core[0], i, j)),
                  pl.BlockSpec((None, None, th, tc), lambda s, l, i, j, core: (s, l, i, j))],
        out_specs=pl.BlockSpec((None, None, th, tc), lambda s, l, i, j, core: (s, l, i, j)))
    return pl.pallas_call(body, name=name, grid_spec=grid_spec, out_shape=jax.ShapeDtypeStruct((n, L, h, c), WIRE_DTYPE),
                          compiler_params=_params("parallel", "parallel", "parallel", "parallel"))(core, g, r1)


def add_chips(p, r2, place, name):
    n, L, h, c = p.shape
    th = _tile(h, 512, 16)
    tc = _tile(c, 1024)

    def body(shard_ref, core_ref, p_ref, r_ref, o_ref):
        acc = p_ref[...].astype(jnp.float32)
        for j in range(3):
            acc = acc + r_ref[j].astype(jnp.float32)
        o_ref[...] = acc

    grid_spec = pltpu.PrefetchScalarGridSpec(
        num_scalar_prefetch=2, grid=(L, h // th, c // tc),
        in_specs=[pl.BlockSpec((None, None, th, tc), lambda l, i, j, shard, core: (shard[0], l, i, j)),
                  pl.BlockSpec((3, None, th, tc), lambda l, i, j, shard, core: (0, l, i, j))],
        out_specs=pl.BlockSpec((None, None, th, tc), lambda l, i, j, shard, core: (l, core[0], i, j)))
    return pl.pallas_call(body, name=name, grid_spec=grid_spec, out_shape=_f32((L, 2, h, c)),
                          compiler_params=_params("parallel", "parallel", "parallel"))(*place, p, r2)


def reduce_scatter_start(g4, core, after, name):
    n, L, r, c = g4.shape
    g = g4.reshape(n, L, 2, r // 2, c)
    r1 = sibling_send_halves(g, after, name + "_d2d")
    p = add_sibling(g, r1, core, name + "_add1")
    land = lax.empty((N_PEERS, L, r // 2, c), p.dtype)
    return exchange_start((p, land), _rs_src, _rs_dst, r1, name + "_ici_start")


def reduce_scatter_finish(pending, place, after, name):
    sems, bufs, _ = pending
    p, r2 = exchange_wait(sems, bufs, _rs_src, _rs_dst, after, name + "_ici_wait")
    red = add_chips(p, r2, place, name + "_add2")
    L, two, h, c = red.shape
    return sibling_gather_halves(red, name + "_gather").reshape(2 * h, c)


def all_reduce_small(v, name):
    R, C = v.shape

    def body(v_ref, o_ref, buf, send_sems, recv_sems):
        x, y, cc = _me()
        buf[0] = v_ref[...]
        cps = []
        for k in range(1, 8):
            dx, dy, dc = (k >> 2) & 1, (k >> 1) & 1, k & 1
            to = (x ^ dx, y ^ dy, cc ^ dc)
            cp = pltpu.make_async_remote_copy(src_ref=v_ref, dst_ref=buf.at[k], send_sem=send_sems.at[k],
                                              recv_sem=recv_sems.at[k], device_id=to, device_id_type=MESH)
            cp.start()
            cps.append(cp)
        for cp in cps:
            cp.wait()
        me = 4 * x + 2 * y + cc
        acc = buf[me]
        for a in range(1, 8):
            acc = acc + buf[a ^ me]
        o_ref[...] = acc

    vm = pl.BlockSpec(memory_space=pltpu.VMEM)
    return pl.pallas_call(
        body, name=name, out_shape=_f32((R, C)), in_specs=[vm], out_specs=vm,
        scratch_shapes=[pltpu.VMEM((8, R, C), jnp.float32), pltpu.SemaphoreType.DMA((8,)), pltpu.SemaphoreType.DMA((8,))],
    )(v)


def _rstd(x, n):
    return lax.rsqrt(jnp.sum(x * x, axis=-1, keepdims=True) * (1.0 / n) + EPS)


def _accumulate(ref, part, first):
    @pl.when(first)
    def _():
        ref[...] = part

    @pl.when(jnp.logical_not(first))
    def _():
        ref[...] += part


def rmsnorm_fwd(x, g, name):
    S, D = x.shape
    ts = _tile(S, 256, 8)

    def body(x_ref, g_ref, o_ref):
        xv = x_ref[...]
        o_ref[...] = (xv * _rstd(xv, D) * g_ref[...]).astype(o_ref.dtype)

    return pl.pallas_call(
        body, name=name, out_shape=_act((S, D)), grid=(S // ts,),
        in_specs=[pl.BlockSpec((ts, D), lambda i: (i, 0)), pl.BlockSpec((1, D), lambda i: (0, 0))],
        out_specs=pl.BlockSpec((ts, D), lambda i: (i, 0)), compiler_params=_params("parallel"))(x, g)


def _norm_bwd(x, g, dy, n):
    r = _rstd(x, n)
    xh = x * r
    dxh = dy * g
    dx = r * (dxh - xh * (jnp.sum(dxh * xh, axis=-1, keepdims=True) * (1.0 / n)))
    return dx, dy * xh


def rmsnorm_bwd(x, g, dy, dres, name):
    S, D = x.shape
    ts = _tile(S, 256, 8)

    def body(x_ref, g_ref, dy_ref, dres_ref, dx_ref, dg_ref):
        dx, dgp = _norm_bwd(x_ref[...], g_ref[...], dy_ref[...], D)
        dx_ref[...] = dres_ref[...] + dx
        _accumulate(dg_ref, jnp.sum(dgp, axis=0, keepdims=True), pl.program_id(0) == 0)

    row = pl.BlockSpec((ts, D), lambda i: (i, 0))
    vec = pl.BlockSpec((1, D), lambda i: (0, 0))
    return pl.pallas_call(
        body, name=name, out_shape=(_f32((S, D)), _f32((1, D))), grid=(S // ts,),
        in_specs=[row, vec, row, row], out_specs=(row, vec), compiler_params=_params("arbitrary"))(x, g, dy, dres)


def _sigmoid(x):
    return 1.0 / (1.0 + jnp.exp(-x))


def swiglu_in(xn, w_in, name, tm=512, tn=1408):
    S, D = xn.shape
    F = w_in.shape[1] // 2
    tm, tn = _tile(S, tm, 8), _tile(math.gcd(F, w_in.per), tn)
    nf = F // tn

    def body(x_ref, wg_ref, wu_ref, u_ref, a_ref):
        x = x_ref[...].astype(MXU_DTYPE)
        gt = jnp.dot(x, wg_ref[...], preferred_element_type=jnp.float32)
        up = jnp.dot(x, wu_ref[...], preferred_element_type=jnp.float32)
        u_ref[0] = gt.astype(u_ref.dtype)
        u_ref[1] = up.astype(u_ref.dtype)
        a_ref[...] = (gt * _sigmoid(gt) * up).astype(a_ref.dtype)

    return pl.pallas_call(
        body, name=name, out_shape=(_act((2, 1, S, F)), _act((S, F))), grid=(nf, S // tm),
        in_specs=[pl.BlockSpec((tm, D), lambda j, i: (i, 0)), w_in.spec(D, tn, lambda j, i: (0, j)),
                  w_in.spec(D, tn, lambda j, i: (0, j + nf))],
        out_specs=(pl.BlockSpec((2, None, tm, tn), lambda j, i: (0, 0, i, j)), pl.BlockSpec((tm, tn), lambda j, i: (i, j))),
        compiler_params=_params("parallel", "parallel"))(xn, w_in.arr, w_in.arr)


def swiglu_out_bwd(dout, w_out, u, scale, name, tm=512, tn=1408):
    S, D = dout.shape
    F = w_out.shape[0]
    tm, tn = _tile(S, tm, 8), _tile(math.gcd(F, w_out.per), tn)

    def body(d_ref, w_ref, u_ref, o_ref):
        da = lax.dot_general(d_ref[...].astype(MXU_DTYPE), w_ref[...], NT, preferred_element_type=jnp.float32) * scale
        gt, up = u_ref[0].astype(jnp.float32), u_ref[1].astype(jnp.float32)
        s = _sigmoid(gt)
        o_ref[0] = (da * up * (s * (1.0 + gt * (1.0 - s)))).astype(o_ref.dtype)
        o_ref[1] = (da * (gt * s)).astype(o_ref.dtype)

    planes = pl.BlockSpec((2, None, tm, tn), lambda j, i: (0, 0, i, j))
    return pl.pallas_call(
        body, name=name, out_shape=_act((2, 1, S, F)), grid=(F // tn, S // tm),
        in_specs=[pl.BlockSpec((tm, D), lambda j, i: (i, 0)), w_out.spec(tn, D, lambda j, i: (j, 0)), planes],
        out_specs=planes, compiler_params=_params("parallel", "parallel"))(dout, w_out.arr, u)


def loss_head(y, t, name):
    S, D = y.shape
    ts = _tile(S, 256, 8)

    def body(y_ref, t_ref, dy_ref, l_ref):
        e = y_ref[...] - t_ref[...]
        dy_ref[...] = e * (1.0 / D)
        l_ref[...] = jnp.full(l_ref.shape, 0.5 * jnp.sum(jnp.sum(e * e, axis=-1, keepdims=True) * (1.0 / D)), jnp.float32)

    row = pl.BlockSpec((ts, D), lambda i: (i, 0))
    dy, parts = pl.pallas_call(
        body, name=name, out_shape=(_f32((S, D)), _f32((S // ts, 8, LANES))), grid=(S // ts,),
        in_specs=[row, row], out_specs=(row, pl.BlockSpec((None, 8, LANES), lambda i: (i, 0, 0))),
        compiler_params=_params("parallel"))(y, t)
    return jnp.sum(parts[:, 0, 0]), dy


def rope_tables(S):
    inv = 1.0 / (ROPE_THETA ** (jnp.arange(0, ROPE_DIM, 2, dtype=jnp.float32) / ROPE_DIM))
    ang = jnp.arange(S, dtype=jnp.float32)[:, None] * inv[None, :]
    c, s = jnp.cos(ang), jnp.sin(ang)
    z = jnp.zeros_like(c)
    return (jnp.concatenate([c, c, z, z], axis=1), jnp.concatenate([-s, z, z, z], axis=1),
            jnp.concatenate([z, s, z, z], axis=1))


def _rope(x, cos, sa, sb):
    return x * cos + pltpu.roll(x, 96, 1) * sa + pltpu.roll(x, 32, 1) * sb


def _rope_t(d, cos, sa, sb):
    return d * cos + pltpu.roll(d * sa, 32, 1) + pltpu.roll(d * sb, 96, 1)


def _head_norm(x1, x2, g):
    r = lax.rsqrt((jnp.sum(x1 * x1, axis=-1, keepdims=True) + jnp.sum(x2 * x2, axis=-1, keepdims=True)) * (1.0 / QK_DIM) + EPS)
    return x1 * r * g[:, :LANES], x2 * r * g[:, LANES:], r


def _head_norm_bwd(x1, x2, g, d1, d2):
    _, _, r = _head_norm(x1, x2, g)
    h1, h2 = x1 * r, x2 * r
    e1, e2 = d1 * g[:, :LANES], d2 * g[:, LANES:]
    m = (jnp.sum(e1 * h1, axis=-1, keepdims=True) + jnp.sum(e2 * h2, axis=-1, keepdims=True)) * (1.0 / QK_DIM)
    return r * (e1 - h1 * m), r * (e2 - h2 * m), d1 * h1, d2 * h2


def mla_latent_fwd(lat, g_cq, g_ckv, name):
    S, W = lat.shape
    QL, KL = g_cq.shape[1], g_ckv.shape[1]
    ts = _tile(S, 256, 8)

    def body(l_ref, gq_ref, gk_ref, cq_ref, ckv_ref):
        a, b = l_ref[:, :QL], l_ref[:, QL:QL + KL]
        cq_ref[...] = (a * _rstd(a, QL) * gq_ref[...]).astype(cq_ref.dtype)
        ckv_ref[...] = (b * _rstd(b, KL) * gk_ref[...]).astype(ckv_ref.dtype)

    return pl.pallas_call(
        body, name=name, out_shape=(_act((S, QL)), _act((S, KL))), grid=(S // ts,),
        in_specs=[pl.BlockSpec((ts, W), lambda i: (i, 0)), pl.BlockSpec((1, QL), lambda i: (0, 0)),
                  pl.BlockSpec((1, KL), lambda i: (0, 0))],
        out_specs=(pl.BlockSpec((ts, QL), lambda i: (i, 0)), pl.BlockSpec((ts, KL), lambda i: (i, 0))),
        compiler_params=_params("parallel"))(lat, g_cq, g_ckv)


def mla_latent_bwd(dcq, dckv, dkpe, lat, g_cq, g_ckv, name):
    S, W = lat.shape
    QL, KL = g_cq.shape[1], g_ckv.shape[1]
    ts = _tile(S, 256, 8)

    def body(dq_ref, dk_ref, dp_ref, l_ref, gq_ref, gk_ref, o_ref, dgq_ref, dgk_ref):
        first = pl.program_id(0) == 0
        da, ga = _norm_bwd(l_ref[:, :QL], gq_ref[...], dq_ref[...], QL)
        db, gb = _norm_bwd(l_ref[:, QL:QL + KL], gk_ref[...], dk_ref[...], KL)
        o_ref[:, :QL] = da.astype(o_ref.dtype)
        o_ref[:, QL:QL + KL] = db.astype(o_ref.dtype)
        o_ref[:, QL + KL:] = dp_ref[...].astype(o_ref.dtype)
        _accumulate(dgq_ref, jnp.sum(ga, axis=0, keepdims=True), first)
        _accumulate(dgk_ref, jnp.sum(gb, axis=0, keepdims=True), first)

    row = lambda n: pl.BlockSpec((ts, n), lambda i: (i, 0))
    vec = lambda n: pl.BlockSpec((1, n), lambda i: (0, 0))
    return pl.pallas_call(
        body, name=name, out_shape=(_act((S, W)), _f32((1, QL)), _f32((1, KL))), grid=(S // ts,),
        in_specs=[row(QL), row(KL), row(LANES), row(W), vec(QL), vec(KL)], out_specs=(row(W), vec(QL), vec(KL)),
        compiler_params=_params("arbitrary"))(dcq, dckv, dkpe, lat, g_cq, g_ckv)


def mla_q_prep_fwd(qraw, g, tabs, H, name):
    S = qraw.shape[0]
    ts = _tile(S, PREP_ROWS, 8)

    def body(x_ref, g_ref, c_ref, a_ref, b_ref, o_ref):
        y1, y2, _ = _head_norm(x_ref[:, :LANES], x_ref[:, LANES:], g_ref[...])
        o_ref[:, :LANES] = y1.astype(o_ref.dtype)
        o_ref[:, LANES:] = _rope(y2, c_ref[...], a_ref[...], b_ref[...]).astype(o_ref.dtype)

    tab = pl.BlockSpec((ts, LANES), lambda i, h: (i, 0))
    return pl.pallas_call(
        body, name=name, out_shape=_act((H, S, QK_PAD)), grid=(S // ts, H),
        in_specs=[pl.BlockSpec((ts, QK_PAD), lambda i, h: (i, h)), pl.BlockSpec((1, QK_PAD), lambda i, h: (0, 0)), tab, tab, tab],
        out_specs=pl.BlockSpec((None, ts, QK_PAD), lambda i, h: (h, i, 0)),
        compiler_params=_params("parallel", "parallel"))(qraw, g, *tabs)


def mla_q_prep_bwd(dq, qraw, g, tabs, H, name):
    S = qraw.shape[0]
    ts = _tile(S, PREP_ROWS, 8)

    def body(d_ref, x_ref, g_ref, c_ref, a_ref, b_ref, o_ref, dg_ref):
        d2 = _rope_t(d_ref[:, LANES:], c_ref[...], a_ref[...], b_ref[...])
        dx1, dx2, g1, g2 = _head_norm_bwd(x_ref[:, :LANES], x_ref[:, LANES:], g_ref[...], d_ref[:, :LANES], d2)
        o_ref[:, :LANES] = dx1.astype(o_ref.dtype)
        o_ref[:, LANES:] = dx2.astype(o_ref.dtype)
        first = jnp.logical_and(pl.program_id(0) == 0, pl.program_id(1) == 0)
        part = jnp.concatenate([jnp.sum(g1, axis=0, keepdims=True), jnp.sum(g2, axis=0, keepdims=True)], axis=1)
        _accumulate(dg_ref, part, first)

    tab = pl.BlockSpec((ts, LANES), lambda i, h: (i, 0))
    vec = pl.BlockSpec((1, QK_PAD), lambda i, h: (0, 0))
    return pl.pallas_call(
        body, name=name, out_shape=(_act((S, H * QK_PAD)), _f32((1, QK_PAD))), grid=(S // ts, H),
        in_specs=[pl.BlockSpec((None, ts, QK_PAD), lambda i, h: (h, i, 0)), pl.BlockSpec((ts, QK_PAD), lambda i, h: (i, h)),
                  vec, tab, tab, tab],
        out_specs=(pl.BlockSpec((ts, QK_PAD), lambda i, h: (i, h)), vec),
        compiler_params=_params("arbitrary", "arbitrary"))(dq, qraw, g, *tabs)


def mla_k_prep_fwd(kvraw, lat, g, tabs, H, pe_blk, name):
    S = kvraw.shape[0]
    ts = _tile(S, PREP_ROWS, 8)

    def body(x_ref, p_ref, g_ref, c_ref, a_ref, b_ref, k_ref, v_ref):
        y1, y2, _ = _head_norm(x_ref[:, :LANES], p_ref[...], g_ref[...])
        k_ref[:, :LANES] = y1.astype(k_ref.dtype)
        k_ref[:, LANES:] = _rope(y2, c_ref[...], a_ref[...], b_ref[...]).astype(k_ref.dtype)
        v_ref[...] = x_ref[:, LANES:].astype(v_ref.dtype)

    tab = pl.BlockSpec((ts, LANES), lambda i, h: (i, 0))
    return pl.pallas_call(
        body, name=name, out_shape=(_act((H, S, QK_PAD)), _act((H, S, LANES))), grid=(S // ts, H),
        in_specs=[pl.BlockSpec((ts, QK_PAD), lambda i, h: (i, h)), pl.BlockSpec((ts, LANES), lambda i, h: (i, pe_blk)),
                  pl.BlockSpec((1, QK_PAD), lambda i, h: (0, 0)), tab, tab, tab],
        out_specs=(pl.BlockSpec((None, ts, QK_PAD), lambda i, h: (h, i, 0)), pl.BlockSpec((None, ts, LANES), lambda i, h: (h, i, 0))),
        compiler_params=_params("parallel", "parallel"))(kvraw, lat, g, *tabs)


def mla_k_prep_bwd(dk, dv, kvraw, lat, g, tabs, H, pe_blk, name):
    S = kvraw.shape[0]
    ts = _tile(S, PREP_ROWS, 8)

    def body(dk_ref, dv_ref, x_ref, p_ref, g_ref, c_ref, a_ref, b_ref, o_ref, dp_ref, dg_ref):
        i, h = pl.program_id(0), pl.program_id(1)
        d2 = _rope_t(dk_ref[:, LANES:], c_ref[...], a_ref[...], b_ref[...])
        dx1, dx2, g1, g2 = _head_norm_bwd(x_ref[:, :LANES], p_ref[...], g_ref[...], dk_ref[:, :LANES], d2)
        o_ref[:, :LANES] = dx1.astype(o_ref.dtype)
        o_ref[:, LANES:] = dv_ref[...].astype(o_ref.dtype)
        _accumulate(dp_ref, dx2, h == 0)
        part = jnp.concatenate([jnp.sum(g1, axis=0, keepdims=True), jnp.sum(g2, axis=0, keepdims=True)], axis=1)
        _accumulate(dg_ref, part, jnp.logical_and(i == 0, h == 0))

    tab = pl.BlockSpec((ts, LANES), lambda i, h: (i, 0))
    vec = pl.BlockSpec((1, QK_PAD), lambda i, h: (0, 0))
    return pl.pallas_call(
        body, name=name, out_shape=(_act((S, H * QK_PAD)), _f32((S, LANES)), _f32((1, QK_PAD))), grid=(S // ts, H),
        in_specs=[pl.BlockSpec((None, ts, QK_PAD), lambda i, h: (h, i, 0)), pl.BlockSpec((None, ts, LANES), lambda i, h: (h, i, 0)),
                  pl.BlockSpec((ts, QK_PAD), lambda i, h: (i, h)), pl.BlockSpec((ts, LANES), lambda i, h: (i, pe_blk)),
                  vec, tab, tab, tab],
        out_specs=(pl.BlockSpec((ts, QK_PAD), lambda i, h: (i, h)), tab, vec),
        compiler_params=_params("arbitrary", "arbitrary"))(dk, dv, kvraw, lat, g, *tabs)


def _causal_scores(q, k, scale, diagonal):
    s = lax.dot_general(q, k, NT, preferred_element_type=jnp.float32) * scale
    if not diagonal:
        return s
    row = lax.broadcasted_iota(jnp.int32, s.shape, 0)
    col = lax.broadcasted_iota(jnp.int32, s.shape, 1)
    return jnp.where(col <= row, s, NEG)


def _on_causal_blocks(qi, ki, step):
    @pl.when(ki < qi)
    def _():
        step(False)

    @pl.when(ki == qi)
    def _():
        step(True)


def mla_attention_fwd(q, k, v, name):
    H, S, _ = q.shape
    t = _tile(S, ATTN_BLOCK)
    n = S // t
    scale = 1.0 / math.sqrt(QK_DIM)

    def body(q_ref, k_ref, v_ref, o_ref, lse_ref, m_sc, l_sc, acc):
        qi, ki = pl.program_id(1), pl.program_id(2)

        @pl.when(ki == 0)
        def _():
            m_sc[...] = jnp.full(m_sc.shape, NEG, jnp.float32)
            l_sc[...] = jnp.zeros_like(l_sc)
            acc[...] = jnp.zeros_like(acc)

        def step(diagonal):
            s = _causal_scores(q_ref[...], k_ref[...], scale, diagonal)
            m_new = jnp.maximum(m_sc[...], jnp.max(s, axis=-1, keepdims=True))
            alpha = jnp.exp(m_sc[...] - m_new)
            p = jnp.exp(s - m_new)
            l_sc[...] = alpha * l_sc[...] + jnp.sum(p, axis=-1, keepdims=True)
            acc[...] = alpha * acc[...] + jnp.dot(p.astype(MXU_DTYPE), v_ref[...], preferred_element_type=jnp.float32)
            m_sc[...] = m_new

        _on_causal_blocks(qi, ki, step)

        @pl.when(ki == qi)
        def _():
            o_ref[...] = (acc[...] / l_sc[...]).astype(o_ref.dtype)
            lse_ref[...] = m_sc[...] + jnp.log(l_sc[...])

    kv = lambda w: pl.BlockSpec((None, t, w), lambda h, qi, ki: (h, jnp.minimum(ki, qi), 0))
    return pl.pallas_call(
        body, name=name, out_shape=(_act((S, H * LANES)), _f32((H, S, 1))), grid=(H, n, n),
        in_specs=[pl.BlockSpec((None, t, QK_PAD), lambda h, qi, ki: (h, qi, 0)), kv(QK_PAD), kv(LANES)],
        out_specs=(pl.BlockSpec((t, LANES), lambda h, qi, ki: (qi, h)), pl.BlockSpec((None, t, 1), lambda h, qi, ki: (h, qi, 0))),
        scratch_shapes=[pltpu.VMEM((t, 1), jnp.float32), pltpu.VMEM((t, 1), jnp.float32), pltpu.VMEM((t, LANES), jnp.float32)],
        compiler_params=_params("parallel", "parallel", "arbitrary"))(q, k, v)


def attention_delta(do, o, H, name):
    S = do.shape[0]
    ts = _tile(S, 512, 8)

    def body(d_ref, o_ref, out_ref):
        out_ref[...] = jnp.sum(d_ref[...] * o_ref[...].astype(jnp.float32), axis=-1, keepdims=True)

    blk = pl.BlockSpec((ts, LANES), lambda i, h: (i, h))
    return pl.pallas_call(
        body, name=name, out_shape=_f32((H, S, 1)), grid=(S // ts, H), in_specs=[blk, blk],
        out_specs=pl.BlockSpec((None, ts, 1), lambda i, h: (h, i, 0)), compiler_params=_params("parallel", "parallel"))(do, o)


def mla_attention_bwd(q, k, v, do, lse, delta, name):
    H, S, _ = q.shape
    t = _tile(S, ATTN_BLOCK)
    n = S // t
    scale = 1.0 / math.sqrt(QK_DIM)

    def body(q_ref, k_ref, v_ref, do_ref, lse_ref, dl_ref, dq_ref, dk_ref, dv_ref, dk_acc, dv_acc):
        ki, qi = pl.program_id(1), pl.program_id(2)
        rows = pl.ds(pl.multiple_of(qi * t, t), t)

        @pl.when(qi == 0)
        def _():
            dk_acc[...] = jnp.zeros_like(dk_acc)
            dv_acc[...] = jnp.zeros_like(dv_acc)

        @pl.when(ki == 0)
        def _():
            dq_ref[rows, :] = jnp.zeros((t, QK_PAD), jnp.float32)

        def step(diagonal):
            p = jnp.exp(_causal_scores(q_ref[...], k_ref[...], scale, diagonal) - lse_ref[...])
            dob = do_ref[...].astype(MXU_DTYPE)
            dv_acc[...] += lax.dot_general(p.astype(MXU_DTYPE), dob, TN, preferred_element_type=jnp.float32)
            dp = lax.dot_general(dob, v_ref[...], NT, preferred_element_type=jnp.float32)
            ds = (p * (dp - dl_ref[...])).astype(MXU_DTYPE)
            dk_acc[...] += lax.dot_general(ds, q_ref[...], TN, preferred_element_type=jnp.float32)
            dq_ref[rows, :] += jnp.dot(ds, k_ref[...], preferred_element_type=jnp.float32) * scale

        _on_causal_blocks(qi, ki, step)

        @pl.when(qi == n - 1)
        def _():
            dk_ref[...] = dk_acc[...] * scale
            dv_ref[...] = dv_acc[...]

    qrow = lambda h, ki, qi: (h, jnp.maximum(qi, ki), 0)
    kv = lambda w: pl.BlockSpec((None, t, w), lambda h, ki, qi: (h, ki, 0))
    col = pl.BlockSpec((None, t, 1), qrow)
    return pl.pallas_call(
        body, name=name, out_shape=(_f32((H, S, QK_PAD)), _f32((H, S, QK_PAD)), _f32((H, S, LANES))), grid=(H, n, n),
        in_specs=[pl.BlockSpec((None, t, QK_PAD), qrow), kv(QK_PAD), kv(LANES),
                  pl.BlockSpec((t, LANES), lambda h, ki, qi: (jnp.maximum(qi, ki), h)), col, col],
        out_specs=(pl.BlockSpec((None, S, QK_PAD), lambda h, ki, qi: (h, 0, 0)), kv(QK_PAD), kv(LANES)),
        scratch_shapes=[pltpu.VMEM((t, QK_PAD), jnp.float32), pltpu.VMEM((t, LANES), jnp.float32)],
        compiler_params=_params("parallel", "arbitrary", "arbitrary"))(q, k, v, do, lse, delta)


def _alibi_slopes(G, Hd):
    k = np.arange(1, G * Hd + 1, dtype=np.float32)
    s = (2.0 ** (-8.0 * k / (G * Hd))).astype(np.float32).reshape(G, Hd)
    return jnp.asarray(np.broadcast_to(s[:, :, None, None], (G, Hd, 1, LANES)).copy())


def _dil_scores(qn, kn, scale, slope_d, prev, valid):
    s = lax.dot_general(qn, kn, NT, preferred_element_type=jnp.float32) * scale
    iq = lax.broadcasted_iota(jnp.int32, s.shape, 0)
    ik = lax.broadcasted_iota(jnp.int32, s.shape, 1)
    dist = iq - ik + (BLK if prev else 0)
    ok = (ik >= iq) if prev else (ik <= iq)
    s = s - slope_d * dist.astype(jnp.float32)
    return jnp.where(jnp.logical_and(ok, valid), s, NEG)


def _dil_heads(d, Hd, block_bytes):
    hb = max(1, min(Hd, block_bytes // (BLK * LANES * 4))) if d == 1 else 1
    assert Hd % hb == 0, (Hd, hb)
    return hb


def _dil_rows(r, d):
    return pl.ds(r, BLK, stride=d) if d > 1 else slice(None)


def _loop_residues(d, residue, init):
    if d == 1:
        return residue(0, init)
    return lax.fori_loop(0, d // 2, lambda i, carry: residue(2 * i + 1, residue(2 * i, carry)), init)


def _dil_specs(d, nblk, Hd, G, g, hb):
    def spec(kind, shift):
        col0 = (kind * G + g) * Hd // hb
        return pl.BlockSpec((BLK * d, hb * LANES), lambda n, hg: (jnp.clip(n + shift, 0, nblk - 1), col0 + hg))
    return spec


def _head_spec(d, nblk, hb, shift):
    return pl.BlockSpec((BLK * d, hb * LANES), lambda n, hg: (jnp.clip(n + shift, 0, nblk - 1), hg))


def dilated_fwd(qkv, gq, gk, slopes, g, d, Hd, G, name):
    S, C = qkv.shape
    nblk = S // (BLK * d)
    hb = _dil_heads(d, Hd, 2 << 20)
    scale = 1.0 / math.sqrt(LANES)
    spec = _dil_specs(d, nblk, Hd, G, g, hb)

    def body(q_ref, kc_ref, kp_ref, vc_ref, vp_ref, gq_ref, gk_ref, sl_ref, o_ref, l_ref):
        n = pl.program_id(0)
        nrm = lambda t, gg: (t * _rstd(t, LANES) * gg).astype(MXU_DTYPE)

        def residue(r, carry):
            rows = _dil_rows(r, d)
            for hh in range(hb):
                cols = slice(hh * LANES, (hh + 1) * LANES)
                slope_d = sl_ref[hh][:, :1] * float(d)
                qn = nrm(q_ref[rows, cols], gq_ref[...])
                sc = _dil_scores(qn, nrm(kc_ref[rows, cols], gk_ref[...]), scale, slope_d, False, True)
                sp = _dil_scores(qn, nrm(kp_ref[rows, cols], gk_ref[...]), scale, slope_d, True, n > 0)
                m = jnp.maximum(jnp.max(sc, axis=-1, keepdims=True), jnp.max(sp, axis=-1, keepdims=True))
                lse = m + jnp.log(jnp.sum(jnp.exp(sc - m), axis=-1, keepdims=True) + jnp.sum(jnp.exp(sp - m), axis=-1, keepdims=True))
                o = jnp.dot(jnp.exp(sc - lse).astype(MXU_DTYPE), vc_ref[rows, cols].astype(MXU_DTYPE), preferred_element_type=jnp.float32)
                o = o + jnp.dot(jnp.exp(sp - lse).astype(MXU_DTYPE), vp_ref[rows, cols].astype(MXU_DTYPE), preferred_element_type=jnp.float32)
                o_ref[rows, cols] = o
                l_ref[rows, cols] = jnp.broadcast_to(lse, (BLK, LANES))
            return carry

        _loop_residues(d, residue, 0)

    vec = pl.BlockSpec((1, LANES), lambda n, hg: (0, 0))
    out = _head_spec(d, nblk, hb, 0)
    return pl.pallas_call(
        body, name=name, out_shape=(_f32((S, Hd * LANES)), _f32((S, Hd * LANES))), grid=(nblk, Hd // hb),
        in_specs=[spec(0, 0), spec(1, 0), spec(1, -1), spec(2, 0), spec(2, -1), vec, vec,
                  pl.BlockSpec((None, hb, 1, LANES), lambda n, hg: (g, hg, 0, 0))],
        out_specs=(out, out), compiler_params=_params("parallel", "parallel"),
    )(qkv, qkv, qkv, qkv, qkv, gq, gk, slopes)


def dilated_merge(os_, ls_, name):
    S, W = os_[0].shape
    G = len(os_)
    ts, tw = _tile(S, 512, 8), _tile(W, 512)

    def body(*refs):
        o_refs, l_refs, (o_ref, t_ref) = refs[:G], refs[G:2 * G], refs[2 * G:]
        ls = [r[...] for r in l_refs]
        m = ls[0]
        for l in ls[1:]:
            m = jnp.maximum(m, l)
        es = [jnp.exp(l - m) for l in ls]
        tot = es[0]
        for e in es[1:]:
            tot = tot + e
        acc = o_refs[0][...] * (es[0] / tot)
        for r, e in zip(o_refs[1:], es[1:]):
            acc = acc + r[...] * (e / tot)
        o_ref[...] = acc.astype(o_ref.dtype)
        t_ref[...] = m + jnp.log(tot)

    blk = pl.BlockSpec((ts, tw), lambda i, j: (i, j))
    return pl.pallas_call(
        body, name=name, out_shape=(_act((S, W)), _f32((S, W))), grid=(S // ts, W // tw),
        in_specs=[blk] * (2 * G), out_specs=(blk, blk), compiler_params=_params("parallel", "parallel"))(*os_, *ls_)


def dilated_delta(do, o, name):
    S, W = do.shape
    ts = _tile(S, 512, 8)

    def body(d_ref, o_ref, out_ref):
        out_ref[...] = jnp.broadcast_to(jnp.sum(d_ref[...] * o_ref[...].astype(jnp.float32), axis=-1, keepdims=True), out_ref.shape)

    blk = pl.BlockSpec((ts, LANES), lambda i, h: (i, h))
    return pl.pallas_call(body, name=name, out_shape=_f32((S, W)), grid=(S // ts, W // LANES), in_specs=[blk, blk],
                          out_specs=blk, compiler_params=_params("parallel", "parallel"))(do, o)


def dilated_bwd(qkv, do, lse, delta, gq, gk, slopes, g, d, Hd, G, name):
    S, C = qkv.shape
    nblk = S // (BLK * d)
    W = Hd * LANES
    scale = 1.0 / math.sqrt(LANES)
    hb = _dil_heads(d, Hd, 1 << 20)
    spec = _dil_specs(d, nblk, Hd, G, g, hb)
    hspec = lambda shift: _head_spec(d, nblk, hb, shift)

    def body(q_ref, qx_ref, kc_ref, kp_ref, vc_ref, vp_ref, do_ref, dox_ref, l_ref, lx_ref, dl_ref, dlx_ref,
             gq_ref, gk_ref, sl_ref, dq_ref, dk_ref, dv_ref, dgq_ref, dgk_ref):
        n = pl.program_id(0)
        gqv, gkv = gq_ref[...], gk_ref[...]
        nrm = lambda t, gg: (t * _rstd(t, LANES) * gg).astype(MXU_DTYPE)
        f32dot = lambda a, b, dn: lax.dot_general(a, b, dn, preferred_element_type=jnp.float32)

        def residue(r, carry):
            rows = _dil_rows(r, d)
            dgq_sum, dgk_sum = carry
            for hh in range(hb):
                cols = slice(hh * LANES, (hh + 1) * LANES)
                slope_d = sl_ref[hh][:, :1] * float(d)
                q, kc = q_ref[rows, cols], kc_ref[rows, cols]
                qn, qxn = nrm(q, gqv), nrm(qx_ref[rows, cols], gqv)
                kcn, kpn = nrm(kc, gkv), nrm(kp_ref[rows, cols], gkv)
                vc, vp = vc_ref[rows, cols].astype(MXU_DTYPE), vp_ref[rows, cols].astype(MXU_DTYPE)
                dob, doxb = do_ref[rows, cols].astype(MXU_DTYPE), dox_ref[rows, cols].astype(MXU_DTYPE)
                lrow, lxrow = l_ref[rows, cols][:, :1], lx_ref[rows, cols][:, :1]
                drow, dxrow = dl_ref[rows, cols][:, :1], dlx_ref[rows, cols][:, :1]
                pc = jnp.exp(_dil_scores(qn, kcn, scale, slope_d, False, True) - lrow)
                pp = jnp.exp(_dil_scores(qn, kpn, scale, slope_d, True, n > 0) - lrow)
                dsc = pc * (f32dot(dob, vc, NT) - drow)
                dsp = pp * (f32dot(dob, vp, NT) - drow)
                dqn = (jnp.dot(dsc.astype(MXU_DTYPE), kcn, preferred_element_type=jnp.float32)
                       + jnp.dot(dsp.astype(MXU_DTYPE), kpn, preferred_element_type=jnp.float32)) * scale
                dq, dgq = _norm_bwd(q, gqv, dqn, LANES)
                dq_ref[rows, cols] = dq
                px = jnp.exp(_dil_scores(qxn, kcn, scale, slope_d, True, n < nblk - 1) - lxrow)
                dsx = px * (f32dot(doxb, vc, NT) - dxrow)
                dkn = (f32dot(dsc.astype(MXU_DTYPE), qn, TN) + f32dot(dsx.astype(MXU_DTYPE), qxn, TN)) * scale
                dk, dgk = _norm_bwd(kc, gkv, dkn, LANES)
                dk_ref[rows, cols] = dk
                dv_ref[rows, cols] = f32dot(pc.astype(MXU_DTYPE), dob, TN) + f32dot(px.astype(MXU_DTYPE), doxb, TN)
                dgq_sum = dgq_sum + jnp.sum(dgq, axis=0, keepdims=True)
                dgk_sum = dgk_sum + jnp.sum(dgk, axis=0, keepdims=True)
            return dgq_sum, dgk_sum

        zero = jnp.zeros((1, LANES), jnp.float32)
        dgq_sum, dgk_sum = _loop_residues(d, residue, (zero, zero))
        first = jnp.logical_and(n == 0, pl.program_id(1) == 0)
        _accumulate(dgq_ref, dgq_sum, first)
        _accumulate(dgk_ref, dgk_sum, first)

    vec = pl.BlockSpec((1, LANES), lambda n, hg: (0, 0))
    out = hspec(0)
    return pl.pallas_call(
        body, name=name, out_shape=(_f32((S, W)), _f32((S, W)), _f32((S, W)), _f32((1, LANES)), _f32((1, LANES))),
        grid=(nblk, Hd // hb),
        in_specs=[spec(0, 0), spec(0, 1), spec(1, 0), spec(1, -1), spec(2, 0), spec(2, -1), hspec(0), hspec(1), hspec(0), hspec(1),
                  hspec(0), hspec(1), vec, vec, pl.BlockSpec((None, hb, 1, LANES), lambda n, hg: (g, hg, 0, 0))],
        out_specs=(out, out, out, vec, vec), compiler_params=_params("arbitrary", "arbitrary"),
    )(qkv, qkv, qkv, qkv, qkv, qkv, do, do, lse, lse, delta, delta, gq, gk, slopes)


def adamw(w, g, m, v, layer, prev, name):
    L, r, c = w.shape
    tr, tc = _tile(r, 512, 8), _tile(c, 1024)
    c1 = 1.0 / (1.0 - ADAM_B1 ** ADAM_STEP)
    c2 = 1.0 / (1.0 - ADAM_B2 ** ADAM_STEP)

    def body(*refs):
        w_ref, g_ref, m_ref, v_ref = refs[:4]
        go_ref, d_ref, mo_ref, vo_ref = refs[-4:]
        gv = g_ref[...]
        mn = ADAM_B1 * m_ref[...] + (1.0 - ADAM_B1) * gv
        vn = ADAM_B2 * v_ref[...] + (1.0 - ADAM_B2) * (gv * gv)
        go_ref[...] = gv
        d_ref[...] = -ADAM_LR * ((mn * c1) / (jnp.sqrt(vn * c2) + ADAM_EPS) + ADAM_WD * w_ref[...])
        mo_ref[...] = mn
        vo_ref[...] = vn

    lay = pl.BlockSpec((None, tr, tc), lambda i, j: (layer, i, j))
    flat = pl.BlockSpec((tr, tc), lambda i, j: (i, j))
    ins = [w, g, m, v] + (list(prev) if prev is not None else [])
    in_specs = [lay, flat, lay, lay] + ([ANY] * 4 if prev is not None else [])
    return pl.pallas_call(
        body, name=name, out_shape=tuple(_f32((L, r, c)) for _ in range(4)), grid=(r // tr, c // tc),
        in_specs=in_specs, out_specs=(lay, lay, lay, lay),
        input_output_aliases=({4 + k: k for k in range(4)} if prev is not None else {}),
        compiler_params=_params("parallel", "parallel"))(*ins)


def _ffn_fwd(h, g, W, kind, tag):
    xn = rmsnorm_fwd(h, g, tag + "_norm")
    u, a = swiglu_in(xn, W(kind + "_w_in", h), tag + "_in")
    out = matmul(a, W(kind + "_w_out", a), scale=0.5, res=h, name=tag + "_out", tk=2816)
    return out, (h, xn, u, a)


def _ffn_bwd(dout, saved, g, W, emit, kind, tag):
    h, xn, u, a = saved
    emit(kind + "_w_out", matmul(a, dout, ta=True, scale=0.5, out_dtype=WIRE_DTYPE, out_axis=0, name=tag + "_dwout", tm=1408, tk=2048))
    du = Sharded(swiglu_out_bwd(dout, W(kind + "_w_out", None), u, 0.5, tag + "_da"), 0, 1)
    emit(kind + "_w_in", matmul(xn, du, ta=True, out_dtype=WIRE_DTYPE, out_axis=1, name=tag + "_dwin", tn=1408, tk=2048))
    dxn = matmul(du, W(kind + "_w_in", None), tb=True, name=tag + "_dxn", tk=2816)
    return rmsnorm_bwd(h, g, dxn, dout, tag + "_dnorm")


def _pad_gain(g):
    return jnp.pad(g, ((0, 0), (0, QK_PAD - QK_DIM)))


def _mla_fwd(h, P, W, tabs, H):
    g_mix, g_cq, g_ckv = P["mix_norm"][0:1], P["mla_g_cq"], P["mla_g_ckv"]
    pe_blk = (g_cq.shape[1] + g_ckv.shape[1]) // LANES
    xn = rmsnorm_fwd(h, g_mix, "mla_norm")
    w_down = W("mla_w_down", h)
    lat = matmul(xn, w_down, name="mla_down", tn=w_down.shape[1])
    cq, ckv = mla_latent_fwd(lat, g_cq, g_ckv, "mla_latent")
    qraw = matmul(cq, W("mla_w_uq", lat), name="mla_uq")
    kvraw = matmul(ckv, W("mla_w_ukv", qraw), name="mla_ukv")
    q = mla_q_prep_fwd(qraw, _pad_gain(P["mla_g_qn"]), tabs, H, "mla_qprep")
    k, v = mla_k_prep_fwd(kvraw, lat, _pad_gain(P["mla_g_kn"]), tabs, H, pe_blk, "mla_kprep")
    o, lse = mla_attention_fwd(q, k, v, "mla_attn")
    out = matmul(o, W("mla_w_o", lse), res=h, name="mla_o")
    return out, (h, xn, lat, cq, ckv, qraw, kvraw, q, k, v, o, lse, pe_blk)


def _mla_bwd(dout, saved, P, W, emit, tabs, H):
    h, xn, lat, cq, ckv, qraw, kvraw, q, k, v, o, lse, pe_blk = saved
    emit("mla_w_o", matmul(o, dout, ta=True, out_dtype=WIRE_DTYPE, out_axis=0, name="mla_dwo", tm=512))
    do = matmul(dout, W("mla_w_o", None), tb=True, name="mla_do", tn=512)
    delta = attention_delta(do, o, H, "mla_delta")
    dq, dk, dv = mla_attention_bwd(q, k, v, do, lse, delta, "mla_attn_bwd")
    dqraw, dgq = mla_q_prep_bwd(dq, qraw, _pad_gain(P["mla_g_qn"]), tabs, H, "mla_dqprep")
    dkvraw, dkpe, dgk = mla_k_prep_bwd(dk, dv, kvraw, lat, _pad_gain(P["mla_g_kn"]), tabs, H, pe_blk, "mla_dkprep")
    emit("mla_w_uq", matmul(cq, dqraw, ta=True, out_dtype=WIRE_DTYPE, out_axis=1, name="mla_dwuq"))
    dcq = matmul(dqraw, W("mla_w_uq", None), tb=True, name="mla_dcq", tk=1024)
    emit("mla_w_ukv", matmul(ckv, dkvraw, ta=True, out_dtype=WIRE_DTYPE, out_axis=1, name="mla_dwukv"))
    dckv = matmul(dkvraw, W("mla_w_ukv", None), tb=True, name="mla_dckv", tk=1024)
    dlat, dgcq, dgckv = mla_latent_bwd(dcq, dckv, dkpe, lat, P["mla_g_cq"], P["mla_g_ckv"], "mla_dlatent")
    emit("mla_w_down", matmul(xn, dlat, ta=True, out_dtype=WIRE_DTYPE, out_axis=0, name="mla_dwdown", tm=512, tn=dlat.shape[1]))
    dxn = matmul(dlat, W("mla_w_down", None), tb=True, name="mla_dxn", tn=512, tk=dlat.shape[1])
    dh, dgm = rmsnorm_bwd(h, P["mix_norm"][0:1], dxn, dout, "mla_dnorm")
    return dh, dgm, dict(mla_g_qn=dgq[:, :QK_DIM], mla_g_kn=dgk[:, :QK_DIM], mla_g_cq=dgcq, mla_g_ckv=dgckv)


def _dil_fwd(h, P, W, slopes, Hd):
    G = len(DIL_PAIRS)
    xn = rmsnorm_fwd(h, P["mix_norm"][1:2], "dil_norm")
    qkv = matmul(xn, W("dil_w_qkv", h), name="dil_qkv", tn=1152)
    os_, ls_ = [], []
    for g, (_, d) in enumerate(DIL_PAIRS):
        o_g, l_g = dilated_fwd(qkv, P["dil_g_qn"], P["dil_g_kn"], slopes, g, d, Hd, G, f"dil_attn{g}")
        os_.append(o_g)
        ls_.append(l_g)
    o, lse = dilated_merge(os_, ls_, "dil_merge")
    out = matmul(o, W("dil_w_o", lse), res=h, name="dil_o", tn=512)
    return out, (h, xn, qkv, o, lse)


def _dil_bwd(dout, saved, P, W, emit, slopes, Hd):
    h, xn, qkv, o, lse = saved
    ngrp = len(DIL_PAIRS)
    emit("dil_w_o", matmul(o, dout, ta=True, out_dtype=WIRE_DTYPE, out_axis=1, name="dil_dwo", tn=512))
    do = matmul(dout, W("dil_w_o", None), tb=True, name="dil_do", tk=512)
    delta = dilated_delta(do, o, "dil_delta")
    parts = [dilated_bwd(qkv, do, lse, delta, P["dil_g_qn"], P["dil_g_kn"], slopes, g, d, Hd, ngrp, f"dil_dattn{g}")
             for g, (_, d) in enumerate(DIL_PAIRS)]
    dqkv = jnp.concatenate([p[kind] for kind in range(3) for p in parts], axis=1).astype(MXU_DTYPE)
    emit("dil_w_qkv", matmul(xn, dqkv, ta=True, out_dtype=WIRE_DTYPE, out_axis=1, name="dil_dwqkv", tn=1152))
    dxn = matmul(dqkv, W("dil_w_qkv", None), tb=True, name="dil_dxn", tk=1152)
    dh, dgm = rmsnorm_bwd(h, P["mix_norm"][1:2], dxn, dout, "dil_dnorm")
    return dh, dgm, dict(dil_g_qn=parts[0][3] + parts[1][3] + parts[2][3], dil_g_kn=parts[0][4] + parts[1][4] + parts[2][4])


def local_step(x, target, P, get_w, on_grad):
    S, D = x.shape
    H, Hd = MLA_HEADS, DIL_HEADS
    tabs = rope_tables(S)
    slopes = _alibi_slopes(len(DIL_PAIRS), Hd)
    cache = {}

    def weights_of(layer):
        def W(name, after):
            if (name, layer) not in cache:
                cache[name, layer] = Sharded(get_w(name, layer, after), 0, SHARD_AXIS[name])
            return cache[name, layer]
        return W

    row = lambda name, i: P[name][i:i + 1]
    h = x
    saved = []
    for i in range(2):
        W = weights_of(i)
        h, s1 = _ffn_fwd(h, row("ffn1_norm", i), W, "ffn1", f"l{i}_ffn1")
        h, sm = _mla_fwd(h, P, weights_of(0), tabs, H) if i == 0 else _dil_fwd(h, P, weights_of(0), slopes, Hd)
        h, s2 = _ffn_fwd(h, row("ffn2_norm", i), W, "ffn2", f"l{i}_ffn2")
        saved.append((s1, sm, s2))
    loss, dh = loss_head(h, target, "loss")

    gs = {n: [None, None] for n in ("ffn1_norm", "mix_norm", "ffn2_norm")}
    for i in (1, 0):
        s1, sm, s2 = saved[i]
        W = weights_of(i)
        emit = lambda name, g4, layer=i: on_grad(name, layer, g4)
        emit0 = lambda name, g4: on_grad(name, 0, g4)
        dh, gs["ffn2_norm"][i] = _ffn_bwd(dh, s2, row("ffn2_norm", i), W, emit, "ffn2", f"l{i}_ffn2")
        if i == 0:
            dh, gs["mix_norm"][i], gm = _mla_bwd(dh, sm, P, weights_of(0), emit0, tabs, H)
        else:
            dh, gs["mix_norm"][i], gm = _dil_bwd(dh, sm, P, weights_of(0), emit0, slopes, Hd)
        gs.update({n: [val] for n, val in gm.items()})
        dh, gs["ffn1_norm"][i] = _ffn_bwd(dh, s1, row("ffn1_norm", i), W, emit, "ffn1", f"l{i}_ffn1")
    gsmall = {n: jnp.concatenate(v, axis=0) for n, v in gs.items()}
    return loss, dh, gsmall


def _pad_heads(w, real, padded):
    lead, n = w.shape[:-1], w.shape[-1] // real
    w = jnp.pad(w.reshape(*lead, n, real), [(0, 0)] * (len(lead) + 1) + [(0, padded - real)])
    return w.reshape(*lead, n * padded)


def _unpad_heads(w, real, padded):
    lead, n = w.shape[:-1], w.shape[-1] // padded
    return w.reshape(*lead, n, padded)[..., :real].reshape(*lead, n * real)


def _pack_small(gs):
    flat = jnp.concatenate([gs[n].reshape(-1) for n in SMALL])
    rows = -(-flat.shape[0] // LANES)
    rows = -(-rows // 8) * 8
    return jnp.pad(flat, (0, rows * LANES - flat.shape[0])).reshape(rows, LANES)


def _unpack_small(packed, like):
    flat, out, off = packed.reshape(-1), {}, 0
    for n in SMALL:
        size = int(np.prod(like[n].shape))
        out[n] = flat[off:off + size].reshape(like[n].shape)
        off += size
    return out


def kernel(x, ffn1_norm, ffn1_w_in, ffn1_w_out, mix_norm, ffn2_norm, ffn2_w_in, ffn2_w_out, mla_w_down, mla_g_cq, mla_g_ckv, mla_w_uq, mla_w_ukv, mla_g_qn, mla_g_kn, mla_w_o, dil_w_qkv, dil_g_qn, dil_g_kn, dil_w_o, loss_target, m_ffn1_norm, m_ffn1_w_in, m_ffn1_w_out, m_mix_norm, m_ffn2_norm, m_ffn2_w_in, m_ffn2_w_out, m_mla_w_down, m_mla_g_cq, m_mla_g_ckv, m_mla_w_uq, m_mla_w_ukv, m_mla_g_qn, m_mla_g_kn, m_mla_w_o, m_dil_w_qkv, m_dil_g_qn, m_dil_g_kn, m_dil_w_o, v_ffn1_norm, v_ffn1_w_in, v_ffn1_w_out, v_mix_norm, v_ffn2_norm, v_ffn2_w_in, v_ffn2_w_out, v_mla_w_down, v_mla_g_cq, v_mla_g_ckv, v_mla_w_uq, v_mla_w_ukv, v_mla_g_qn, v_mla_g_kn, v_mla_w_o, v_dil_w_qkv, v_dil_g_qn, v_dil_g_kn, v_dil_w_o):
    args = dict(locals())
    w = {n: args[n] for n in WEIGHTS}
    m = {n: args["m_" + n] for n in WEIGHTS}
    v = {n: args["v_" + n] for n in WEIGHTS}
    cx, cy, cc = _me()
    core = jnp.reshape(cc, (1,)).astype(jnp.int32)
    shard = jnp.reshape(2 * cx + cy, (1,)).astype(jnp.int32)
    place = (shard, core)
    pe_pad = LANES - ROPE_DIM

    order = [(n, layer if w[n].shape[0] > 1 else 0) for layer in range(2) for n in USE_ORDER[layer]]
    lands, started = {}, {}

    def cast(key, after):
        _, r, c = w[key[0]].shape
        lands[key] = cast_into_shards(w[key[0]], key[1], shard, after, "ag_%s%d_cast" % key).reshape(N_CHIPS, 1, 2, r // 2, c)
        return lands[key]

    def start(key, after):
        started[key] = exchange_start((lands.pop(key),), _ag_mine, _ag_mine, after, "ag_%s%d_start" % key)
        return started[key][2]

    def get_w(n, l, after):
        k = order.index((n, l))
        behind = after
        if k == 0:
            for key in order[:2]:
                cast(key, after)
            for key in order[:2]:
                behind = start(key, after)
            for key in order[2:]:
                behind = cast(key, behind)
        else:
            for key in order[len(started):min(k + AG_AHEAD, len(order) - 1) + 1]:
                behind = start(key, after)
        sems, bufs, _ = started[n, l]
        _, _, _, h, c = bufs[0].shape
        (land,) = exchange_wait(sems, bufs, _ag_mine, _ag_got, behind, f"ag_{n}{l}_wait")
        full = all_gather_finish(land, f"ag_{n}{l}_finish").reshape(N_CHIPS, 1, 2 * h, c)
        if n == "mla_w_down":
            full = jnp.pad(full, ((0, 0), (0, 0), (0, 0), (0, pe_pad)))
        if n == "mla_w_uq":
            full = _pad_heads(full, QK_DIM, QK_PAD)
        return full

    pending = []
    outs = {n: None for n in BIG}
    last_token = [None]

    def finish_oldest(after):
        n, l, pend = pending.pop(0)
        g = reduce_scatter_finish(pend, place, after, f"rs_{n}{l}")
        outs[n] = adamw(w[n], g, m[n], v[n], l, outs[n], f"adamw_{n}{l}")

    def on_grad(n, l, g4):
        if n == "mla_w_down":
            g4 = g4[..., :g4.shape[-1] - pe_pad]
        if n == "mla_w_uq":
            g4 = _unpad_heads(g4, QK_DIM, QK_PAD)
        after = last_token[0] if last_token[0] is not None else g4
        pending.append((n, l, reduce_scatter_start(g4, core, after, f"rs_{n}{l}")))
        last_token[0] = pending[-1][2][2]
        if len(pending) > RS_WINDOW:
            finish_oldest(last_token[0])

    loss, grad_x, gsmall = local_step(x[0], loss_target[0], {n: w[n] for n in SMALL}, get_w, on_grad)
    loss = lax.psum(loss, ("x", "y", "c"))
    while pending:
        finish_oldest(grad_x)
    small = _unpack_small(all_reduce_small(_pack_small(gsmall), "ar_small"), gsmall)
    for n in SMALL:
        outs[n] = tuple(o[0] for o in adamw(w[n][None], small[n], m[n][None], v[n][None], 0, None, f"adamw_{n}"))

    return (loss, grad_x[None], *[outs[n][0] for n in WEIGHTS], *[outs[n][1] for n in WEIGHTS],
            *[outs[n][2] for n in WEIGHTS], *[outs[n][3] for n in WEIGHTS])
```

```python
import math

import numpy as np
import jax
import jax.numpy as jnp
from jax import lax
from jax.experimental import pallas as pl
from jax.experimental.pallas import tpu as pltpu

MXU_DTYPE = jnp.bfloat16
WIRE_DTYPE = jnp.bfloat16
EPS = 1e-6
NEG = -1e30
N_CHIPS = 4
MESH = pl.DeviceIdType.MESH
ANY = pl.BlockSpec(memory_space=pl.ANY)
LANES = 128

MLA_HEADS = 16
NOPE_DIM = 128
ROPE_DIM = 64
QK_DIM = NOPE_DIM + ROPE_DIM
QK_PAD = 2 * LANES
PREP_ROWS = 1024
ATTN_BLOCK = 1024
AG_AHEAD = 2
RS_WINDOW = 2
ROPE_THETA = 10000.0
DIL_PAIRS = ((128, 1), (512, 4), (2048, 16))
DIL_HEADS = 8
BLK = 128

ADAM_LR = 0.001
ADAM_B1 = 0.9
ADAM_B2 = 0.999
ADAM_EPS = 1e-08
ADAM_WD = 0.01
ADAM_STEP = 10

NT = (((1,), (1,)), ((), ()))
TN = (((0,), (0,)), ((), ()))

SHARD_AXIS = {"ffn1_w_in": 1, "ffn1_w_out": 0, "ffn2_w_in": 1, "ffn2_w_out": 0, "mla_w_down": 0, "mla_w_uq": 1,
              "mla_w_ukv": 1, "mla_w_o": 0, "dil_w_qkv": 1, "dil_w_o": 1}
BIG = tuple(SHARD_AXIS)
USE_ORDER = (("ffn1_w_in", "ffn1_w_out", "mla_w_down", "mla_w_uq", "mla_w_ukv", "mla_w_o", "ffn2_w_in", "ffn2_w_out"),
             ("ffn1_w_in", "ffn1_w_out", "dil_w_qkv", "dil_w_o", "ffn2_w_in", "ffn2_w_out"))
SMALL = ("ffn1_norm", "mix_norm", "ffn2_norm", "mla_g_cq", "mla_g_ckv", "mla_g_qn", "mla_g_kn", "dil_g_qn", "dil_g_kn")
WEIGHTS = ("ffn1_norm", "ffn1_w_in", "ffn1_w_out", "mix_norm", "ffn2_norm", "ffn2_w_in", "ffn2_w_out", "mla_w_down",
           "mla_g_cq", "mla_g_ckv", "mla_w_uq", "mla_w_ukv", "mla_g_qn", "mla_g_kn", "mla_w_o", "dil_w_qkv", "dil_g_qn",
           "dil_g_kn", "dil_w_o")


def _tile(dim, pref, mult=LANES):
    if dim <= pref:
        return dim
    t = (pref // mult) * mult
    while t >= mult:
        if dim % t == 0:
            return t
        t -= mult
    return dim


def _params(*sem):
    return pltpu.CompilerParams(dimension_semantics=sem)


def _f32(shape):
    return jax.ShapeDtypeStruct(shape, jnp.float32)


def _act(shape):
    return jax.ShapeDtypeStruct(shape, MXU_DTYPE)


class Sharded:
    def __init__(self, arr, layer, axis):
        self.arr, self.layer, self.axis = arr, layer, axis
        n, _, r, c = arr.shape
        self.shape = (n * r, c) if axis == 0 else (r, n * c)
        self.per = r if axis == 0 else c

    def spec(self, tr, tc, rc_of):
        l = self.layer
        if self.axis == 0:
            n = self.per // tr

            def imap(*g):
                bi, bj = rc_of(*g)
                return (bi // n, l, bi % n, bj)
        else:
            n = self.per // tc

            def imap(*g):
                bi, bj = rc_of(*g)
                return (bj // n, l, bi, bj % n)
        return pl.BlockSpec((None, None, tr, tc), imap)


def _spec2(tr, tc, rc_of):
    return pl.BlockSpec((tr, tc), lambda *g: rc_of(*g))


def matmul(a, b, *, ta=False, tb=False, out_dtype=jnp.float32, scale=None, res=None, out_axis=None,
           name, tm=1024, tn=1024, tk=2048):
    am, ak = (a.shape[1], a.shape[0]) if ta else a.shape
    bk, bn = (b.shape[1], b.shape[0]) if tb else b.shape
    assert ak == bk, (name, a.shape, b.shape, ta, tb)
    M, N, K = am, bn, ak

    def per(x, axis):
        return x.per if isinstance(x, Sharded) and x.axis == axis else None

    def pick(dim, pref, *pers):
        return _tile(math.gcd(dim, *[p for p in pers if p is not None]), pref)

    tm = pick(M, tm, per(a, 1 if ta else 0), M // N_CHIPS if out_axis == 0 else None)
    tn = pick(N, tn, per(b, 0 if tb else 1), N // N_CHIPS if out_axis == 1 else None)
    tk = pick(K, tk, per(a, 0 if ta else 1), per(b, 1 if tb else 0))
    assert M % tm == 0 and N % tn == 0 and K % tk == 0, (name, M, N, K, tm, tn, tk)
    nk = K // tk

    a_rc = (lambda i, j, k: (k, i)) if ta else (lambda i, j, k: (i, k))
    b_rc = (lambda i, j, k: (j, k)) if tb else (lambda i, j, k: (k, j))
    a_blk = (tk, tm) if ta else (tm, tk)
    b_blk = (tn, tk) if tb else (tk, tn)
    a_spec = a.spec(*a_blk, a_rc) if isinstance(a, Sharded) else _spec2(*a_blk, a_rc)
    b_spec = b.spec(*b_blk, b_rc) if isinstance(b, Sharded) else _spec2(*b_blk, b_rc)
    dn = (((0 if ta else 1,), (1 if tb else 0,)), ((), ()))
    has_res = res is not None

    def body(*refs):
        if has_res:
            a_ref, b_ref, r_ref, o_ref, acc = refs
        else:
            a_ref, b_ref, o_ref, acc = refs
        k = pl.program_id(2)

        @pl.when(k == 0)
        def _():
            acc[...] = jnp.zeros_like(acc)

        acc[...] += lax.dot_general(a_ref[...].astype(MXU_DTYPE), b_ref[...].astype(MXU_DTYPE), dn,
                                    preferred_element_type=jnp.float32)

        @pl.when(k == nk - 1)
        def _():
            r = acc[...]
            if scale is not None:
                r = r * scale
            if has_res:
                r = r + r_ref[...]
            o_ref[...] = r.astype(o_ref.dtype)

    in_specs = [a_spec, b_spec]
    args = [a.arr if isinstance(a, Sharded) else a, b.arr if isinstance(b, Sharded) else b]
    if has_res:
        in_specs.append(_spec2(tm, tn, lambda i, j, k: (i, j)))
        args.append(res)
    o_rc = lambda i, j, k: (i, j)
    if out_axis is None:
        out_shape = jax.ShapeDtypeStruct((M, N), out_dtype)
        out_spec = _spec2(tm, tn, o_rc)
    else:
        shp = (N_CHIPS, 1, M // N_CHIPS, N) if out_axis == 0 else (N_CHIPS, 1, M, N // N_CHIPS)
        out_shape = jax.ShapeDtypeStruct(shp, out_dtype)
        out_spec = Sharded(out_shape, 0, out_axis).spec(tm, tn, o_rc)
    return pl.pallas_call(
        body, name=name, out_shape=out_shape, grid=(M // tm, N // tn, nk),
        in_specs=in_specs, out_specs=out_spec,
        scratch_shapes=[pltpu.VMEM((tm, tn), jnp.float32)],
        compiler_params=_params("parallel", "parallel", "arbitrary"),
    )(*args)


def _me():
    return lax.axis_index("x"), lax.axis_index("y"), lax.axis_index("c")


def _other_chips(x, y):
    return [(1 - x, y), (x, 1 - y), (1 - x, 1 - y)]


HBM = pl.BlockSpec(memory_space=pltpu.HBM)
SEM = pl.BlockSpec(memory_space=pltpu.SEMAPHORE)
N_PEERS = 3
TOKEN = jax.ShapeDtypeStruct((8, LANES), jnp.float32)


def _split_params():
    return pltpu.CompilerParams(has_side_effects=pltpu.SideEffectType.DATAFLOW_SIDE_EFFECTING)


def _in_hbm(a):
    return pltpu.with_memory_space_constraint(a, pltpu.HBM)


def exchange_start(bufs, src_of, dst_of, after, name):
    nb = len(bufs)

    def body(*refs):
        src_ref, land_ref = refs[0], refs[nb - 1]
        sems, token = refs[nb + 1:nb + 1 + 2 * N_PEERS], refs[-1]
        x, y, cc = _me()
        for j, (px, py) in enumerate(_other_chips(x, y)):
            pltpu.make_async_remote_copy(
                src_ref=src_of(src_ref, j, (px, py), (x, y, cc)), dst_ref=dst_of(land_ref, j, (px, py), (x, y, cc)),
                send_sem=sems[j], recv_sem=sems[N_PEERS + j], device_id=(px, py, cc), device_id_type=MESH).start()
        token[...] = jnp.zeros_like(token)

    outs = pl.pallas_call(
        body, name=name,
        out_shape=(pltpu.SemaphoreType.DMA(()),) * (2 * N_PEERS) + tuple(pltpu.HBM(b.shape, b.dtype) for b in bufs) + (TOKEN,),
        in_specs=(HBM,) * nb + (ANY,), out_specs=(SEM,) * (2 * N_PEERS) + (HBM,) * nb + (pl.BlockSpec(memory_space=pltpu.VMEM),),
        input_output_aliases={k: 2 * N_PEERS + k for k in range(nb)}, compiler_params=_split_params(),
    )(*[_in_hbm(b) for b in bufs], after)
    return outs[:2 * N_PEERS], outs[2 * N_PEERS:2 * N_PEERS + nb], outs[-1]


def exchange_wait(sems, bufs, src_of, got_of, after, name):
    nb = len(bufs)

    def body(*refs):
        src_ref, land_ref = refs[0], refs[nb - 1]
        sems_ = refs[nb:nb + 2 * N_PEERS]
        x, y, cc = _me()
        for j, (px, py) in enumerate(_other_chips(x, y)):
            cp = pltpu.make_async_remote_copy(
                src_ref=src_of(src_ref, j, (px, py), (x, y, cc)), dst_ref=got_of(land_ref, j, (px, py), (x, y, cc)),
                send_sem=sems_[j], recv_sem=sems_[N_PEERS + j], device_id=(px, py, cc), device_id_type=MESH)
            cp.wait_send()
            cp.wait_recv()

    return pl.pallas_call(
        body, name=name, out_shape=tuple(pltpu.HBM(b.shape, b.dtype) for b in bufs),
        in_specs=(HBM,) * nb + (SEM,) * (2 * N_PEERS) + (ANY,), out_specs=(HBM,) * nb,
        input_output_aliases={k: k for k in range(nb)}, compiler_params=_split_params(),
    )(*bufs, *sems, after)


def cast_into_shards(w, layer, shard, after, name):
    L, r, c = w.shape
    tr, tc = _tile(r, 1024, 16), _tile(c, 2048)

    def body(shard_ref, w_ref, after_ref, o_ref):
        o_ref[...] = w_ref[...].astype(o_ref.dtype)

    grid_spec = pltpu.PrefetchScalarGridSpec(
        num_scalar_prefetch=1, grid=(r // tr, c // tc),
        in_specs=[pl.BlockSpec((None, tr, tc), lambda i, j, sh: (layer, i, j)), ANY],
        out_specs=pl.BlockSpec((None, None, tr, tc), lambda i, j, sh: (sh[0], 0, i, j)))
    return pl.pallas_call(body, name=name, grid_spec=grid_spec, out_shape=jax.ShapeDtypeStruct((N_CHIPS, 1, r, c), WIRE_DTYPE),
                          compiler_params=_params("parallel", "parallel"))(shard, w, after)


def _ag_mine(ref, j, chip, me):
    return ref.at[2 * me[0] + me[1], :, me[2]]


def _ag_got(ref, j, chip, me):
    return ref.at[2 * chip[0] + chip[1], :, me[2]]


def all_gather_finish(land, name):
    def body(land_ref, o_ref, send_sems, recv_sems):
        x, y, cc = _me()
        cps = []
        for j, (px, py) in enumerate(_other_chips(x, y)):
            cp = pltpu.make_async_remote_copy(
                src_ref=o_ref.at[2 * px + py, :, cc], dst_ref=o_ref.at[2 * px + py, :, cc], send_sem=send_sems.at[j],
                recv_sem=recv_sems.at[j], device_id=(x, y, 1 - cc), device_id_type=MESH)
            cp.start()
            cps.append(cp)
        for j, (px, py) in enumerate(_other_chips(x, y)):
            got = o_ref.at[2 * px + py, :, 1 - cc]
            pltpu.make_async_remote_copy(src_ref=got, dst_ref=got, send_sem=send_sems.at[j], recv_sem=recv_sems.at[j],
                                         device_id=(x, y, 1 - cc), device_id_type=MESH).wait_recv()
        for cp in cps:
            cp.wait_send()

    return pl.pallas_call(
        body, name=name, out_shape=jax.ShapeDtypeStruct(land.shape, land.dtype), in_specs=[ANY], out_specs=ANY,
        input_output_aliases={0: 0},
        scratch_shapes=[pltpu.SemaphoreType.DMA((N_PEERS,)), pltpu.SemaphoreType.DMA((N_PEERS,))],
    )(land)


def _rs_src(ref, j, chip, me):
    return ref.at[2 * chip[0] + chip[1]]


def _rs_dst(ref, j, chip, me):
    return ref.at[j]


def sibling_send_halves(g, after, name):
    n, L, two, h, c = g.shape

    def body(g_ref, after_ref, o_ref, send_sem, recv_sem):
        x, y, cc = _me()
        cp = pltpu.make_async_remote_copy(src_ref=g_ref.at[:, :, 1 - cc], dst_ref=o_ref, send_sem=send_sem,
                                          recv_sem=recv_sem, device_id=(x, y, 1 - cc), device_id_type=MESH)
        cp.start()
        cp.wait()

    return pl.pallas_call(
        body, name=name, out_shape=jax.ShapeDtypeStruct((n, L, h, c), g.dtype),
        in_specs=[ANY, ANY], out_specs=ANY,
        scratch_shapes=[pltpu.SemaphoreType.DMA, pltpu.SemaphoreType.DMA],
    )(g, after)


def sibling_gather_halves(r, name):
    def body(r_ref, o_ref, send_sem, recv_sem):
        x, y, cc = _me()
        cp = pltpu.make_async_remote_copy(src_ref=o_ref.at[:, cc], dst_ref=o_ref.at[:, cc], send_sem=send_sem,
                                          recv_sem=recv_sem, device_id=(x, y, 1 - cc), device_id_type=MESH)
        cp.start()
        cp.wait()

    return pl.pallas_call(
        body, name=name, out_shape=jax.ShapeDtypeStruct(r.shape, r.dtype), in_specs=[ANY], out_specs=ANY,
        input_output_aliases={0: 0}, scratch_shapes=[pltpu.SemaphoreType.DMA, pltpu.SemaphoreType.DMA],
    )(r)


def add_sibling(g, r1, core, name):
    n, L, two, h, c = g.shape
    th = _tile(h, 1024, 16)
    tc = _tile(c, 2048)

    def body(core_ref, g_ref, r_ref, o_ref):
        o_ref[...] = (g_ref[...].astype(jnp.float32) + r_ref[...].astype(jnp.float32)).astype(o_ref.dtype)

    grid_spec = pltpu.PrefetchScalarGridSpec(
        num_scalar_prefetch=1, grid=(n, L, h // th, c // tc),
        in_specs=[pl.BlockSpec((None, None, None, th, tc), lambda s, l, i, j, core: (s, l, core[0], i, j)),
                  pl.BlockSpec((None, None, th, tc), lambda s, l, i, j, core: (s, l, i, j))],
        out_specs=pl.BlockSpec((None, None, th, tc), lambda s, l, i, j, core: (s, l, i, j)))
    return pl.pallas_call(body, name=name, grid_spec=grid_spec, out_shape=jax.ShapeDtypeStruct((n, L, h, c), WIRE_DTYPE),
                          compiler_params=_params("parallel", "parallel", "parallel", "parallel"))(core, g, r1)


def add_chips(p, r2, place, name):
    n, L, h, c = p.shape
    th = _tile(h, 1024, 16)
    tc = _tile(c, 2048)

    def body(shard_ref, core_ref, p_ref, r_ref, o_ref):
        acc = p_ref[...].astype(jnp.float32)
        for j in range(3):
            acc = acc + r_ref[j].astype(jnp.float32)
        o_ref[...] = acc

    grid_spec = pltpu.PrefetchScalarGridSpec(
        num_scalar_prefetch=2, grid=(L, h // th, c // tc),
        in_specs=[pl.BlockSpec((None, None, th, tc), lambda l, i, j, shard, core: (shard[0], l, i, j)),
                  pl.BlockSpec((3, None, th, tc), lambda l, i, j, shard, core: (0, l, i, j))],
        out_specs=pl.BlockSpec((None, None, th, tc), lambda l, i, j, shard, core: (l, core[0], i, j)))
    return pl.pallas_call(body, name=name, grid_spec=grid_spec, out_shape=_f32((L, 2, h, c)),
                          compiler_params=_params("parallel", "parallel", "parallel"))(*place, p, r2)


def reduce_scatter_start(g4, core, after, name):
    n, L, r, c = g4.shape
    g = g4.reshape(n, L, 2, r // 2, c)
    r1 = sibling_send_halves(g, after, name + "_d2d")
    p = add_sibling(g, r1, core, name + "_add1")
    land = lax.empty((N_PEERS, L, r // 2, c), p.dtype)
    return exchange_start((p, land), _rs_src, _rs_dst, r1, name + "_ici_start")


def reduce_scatter_finish(pending, place, after, name):
    sems, bufs, _ = pending
    p, r2 = exchange_wait(sems, bufs, _rs_src, _rs_dst, after, name + "_ici_wait")
    red = add_chips(p, r2, place, name + "_add2")
    L, two, h, c = red.shape
    return sibling_gather_halves(red, name + "_gather").reshape(2 * h, c)


def all_reduce_small(v, name):
    R, C = v.shape

    def body(v_ref, o_ref, buf, send_sems, recv_sems):
        x, y, cc = _me()
        buf[0] = v_ref[...]
        cps = []
        for k in range(1, 8):
            dx, dy, dc = (k >> 2) & 1, (k >> 1) & 1, k & 1
            to = (x ^ dx, y ^ dy, cc ^ dc)
            cp = pltpu.make_async_remote_copy(src_ref=v_ref, dst_ref=buf.at[k], send_sem=send_sems.at[k],
                                              recv_sem=recv_sems.at[k], device_id=to, device_id_type=MESH)
            cp.start()
            cps.append(cp)
        for cp in cps:
            cp.wait()
        me = 4 * x + 2 * y + cc
        acc = buf[me]
        for a in range(1, 8):
            acc = acc + buf[a ^ me]
        o_ref[...] = acc

    vm = pl.BlockSpec(memory_space=pltpu.VMEM)
    return pl.pallas_call(
        body, name=name, out_shape=_f32((R, C)), in_specs=[vm], out_specs=vm,
        scratch_shapes=[pltpu.VMEM((8, R, C), jnp.float32), pltpu.SemaphoreType.DMA((8,)), pltpu.SemaphoreType.DMA((8,))],
    )(v)


def _rstd(x, n):
    return lax.rsqrt(jnp.sum(x * x, axis=-1, keepdims=True) * (1.0 / n) + EPS)


def _accumulate(ref, part, first):
    @pl.when(first)
    def _():
        ref[...] = part

    @pl.when(jnp.logical_not(first))
    def _():
        ref[...] += part


def rmsnorm_fwd(x, g, name):
    S, D = x.shape
    ts = _tile(S, 256, 8)

    def body(x_ref, g_ref, o_ref):
        xv = x_ref[...]
        o_ref[...] = (xv * _rstd(xv, D) * g_ref[...]).astype(o_ref.dtype)

    return pl.pallas_call(
        body, name=name, out_shape=_act((S, D)), grid=(S // ts,),
        in_specs=[pl.BlockSpec((ts, D), lambda i: (i, 0)), pl.BlockSpec((1, D), lambda i: (0, 0))],
        out_specs=pl.BlockSpec((ts, D), lambda i: (i, 0)), compiler_params=_params("parallel"))(x, g)


def _norm_bwd(x, g, dy, n):
    r = _rstd(x, n)
    xh = x * r
    dxh = dy * g
    dx = r * (dxh - xh * (jnp.sum(dxh * xh, axis=-1, keepdims=True) * (1.0 / n)))
    return dx, dy * xh


def rmsnorm_bwd(x, g, dy, dres, name):
    S, D = x.shape
    ts = _tile(S, 256, 8)

    def body(x_ref, g_ref, dy_ref, dres_ref, dx_ref, dg_ref):
        dx, dgp = _norm_bwd(x_ref[...], g_ref[...], dy_ref[...], D)
        dx_ref[...] = dres_ref[...] + dx
        _accumulate(dg_ref, jnp.sum(dgp, axis=0, keepdims=True), pl.program_id(0) == 0)

    row = pl.BlockSpec((ts, D), lambda i: (i, 0))
    vec = pl.BlockSpec((1, D), lambda i: (0, 0))
    return pl.pallas_call(
        body, name=name, out_shape=(_f32((S, D)), _f32((1, D))), grid=(S // ts,),
        in_specs=[row, vec, row, row], out_specs=(row, vec), compiler_params=_params("arbitrary"))(x, g, dy, dres)


def _sigmoid(x):
    return 1.0 / (1.0 + jnp.exp(-x))


def swiglu_in(xn, w_in, name, tm=512, tn=1408):
    S, D = xn.shape
    F = w_in.shape[1] // 2
    tm, tn = _tile(S, tm, 8), _tile(math.gcd(F, w_in.per), tn)
    nf = F // tn

    def body(x_ref, wg_ref, wu_ref, u_ref, a_ref):
        x = x_ref[...].astype(MXU_DTYPE)
        gt = jnp.dot(x, wg_ref[...], preferred_element_type=jnp.float32)
        up = jnp.dot(x, wu_ref[...], preferred_element_type=jnp.float32)
        u_ref[0] = gt.astype(u_ref.dtype)
        u_ref[1] = up.astype(u_ref.dtype)
        a_ref[...] = (gt * _sigmoid(gt) * up).astype(a_ref.dtype)

    return pl.pallas_call(
        body, name=name, out_shape=(_act((2, 1, S, F)), _act((S, F))), grid=(nf, S // tm),
        in_specs=[pl.BlockSpec((tm, D), lambda j, i: (i, 0)), w_in.spec(D, tn, lambda j, i: (0, j)),
                  w_in.spec(D, tn, lambda j, i: (0, j + nf))],
        out_specs=(pl.BlockSpec((2, None, tm, tn), lambda j, i: (0, 0, i, j)), pl.BlockSpec((tm, tn), lambda j, i: (i, j))),
        compiler_params=_params("parallel", "parallel"))(xn, w_in.arr, w_in.arr)


def swiglu_out_bwd(dout, w_out, u, scale, name, tm=512, tn=1408):
    S, D = dout.shape
    F = w_out.shape[0]
    tm, tn = _tile(S, tm, 8), _tile(math.gcd(F, w_out.per), tn)

    def body(d_ref, w_ref, u_ref, o_ref):
        da = lax.dot_general(d_ref[...].astype(MXU_DTYPE), w_ref[...], NT, preferred_element_type=jnp.float32) * scale
        gt, up = u_ref[0].astype(jnp.float32), u_ref[1].astype(jnp.float32)
        s = _sigmoid(gt)
        o_ref[0] = (da * up * (s * (1.0 + gt * (1.0 - s)))).astype(o_ref.dtype)
        o_ref[1] = (da * (gt * s)).astype(o_ref.dtype)

    planes = pl.BlockSpec((2, None, tm, tn), lambda j, i: (0, 0, i, j))
    return pl.pallas_call(
        body, name=name, out_shape=_act((2, 1, S, F)), grid=(F // tn, S // tm),
        in_specs=[pl.BlockSpec((tm, D), lambda j, i: (i, 0)), w_out.spec(tn, D, lambda j, i: (j, 0)), planes],
        out_specs=planes, compiler_params=_params("parallel", "parallel"))(dout, w_out.arr, u)


def loss_head(y, t, name):
    S, D = y.shape
    ts = _tile(S, 256, 8)

    def body(y_ref, t_ref, dy_ref, l_ref):
        e = y_ref[...] - t_ref[...]
        dy_ref[...] = e * (1.0 / D)
        l_ref[...] = jnp.full(l_ref.shape, 0.5 * jnp.sum(jnp.sum(e * e, axis=-1, keepdims=True) * (1.0 / D)), jnp.float32)

    row = pl.BlockSpec((ts, D), lambda i: (i, 0))
    dy, parts = pl.pallas_call(
        body, name=name, out_shape=(_f32((S, D)), _f32((S // ts, 8, LANES))), grid=(S // ts,),
        in_specs=[row, row], out_specs=(row, pl.BlockSpec((None, 8, LANES), lambda i: (i, 0, 0))),
        compiler_params=_params("parallel"))(y, t)
    return jnp.sum(parts[:, 0, 0]), dy


def rope_tables(S):
    inv = 1.0 / (ROPE_THETA ** (jnp.arange(0, ROPE_DIM, 2, dtype=jnp.float32) / ROPE_DIM))
    ang = jnp.arange(S, dtype=jnp.float32)[:, None] * inv[None, :]
    c, s = jnp.cos(ang), jnp.sin(ang)
    z = jnp.zeros_like(c)
    return (jnp.concatenate([c, c, z, z], axis=1), jnp.concatenate([-s, z, z, z], axis=1),
            jnp.concatenate([z, s, z, z], axis=1))


def _rope(x, cos, sa, sb):
    return x * cos + pltpu.roll(x, 96, 1) * sa + pltpu.roll(x, 32, 1) * sb


def _rope_t(d, cos, sa, sb):
    return d * cos + pltpu.roll(d * sa, 32, 1) + pltpu.roll(d * sb, 96, 1)


def _head_norm(x1, x2, g):
    r = lax.rsqrt((jnp.sum(x1 * x1, axis=-1, keepdims=True) + jnp.sum(x2 * x2, axis=-1, keepdims=True)) * (1.0 / QK_DIM) + EPS)
    return x1 * r * g[:, :LANES], x2 * r * g[:, LANES:], r


def _head_norm_bwd(x1, x2, g, d1, d2):
    _, _, r = _head_norm(x1, x2, g)
    h1, h2 = x1 * r, x2 * r
    e1, e2 = d1 * g[:, :LANES], d2 * g[:, LANES:]
    m = (jnp.sum(e1 * h1, axis=-1, keepdims=True) + jnp.sum(e2 * h2, axis=-1, keepdims=True)) * (1.0 / QK_DIM)
    return r * (e1 - h1 * m), r * (e2 - h2 * m), d1 * h1, d2 * h2


def mla_latent_fwd(lat, g_cq, g_ckv, name):
    S, W = lat.shape
    QL, KL = g_cq.shape[1], g_ckv.shape[1]
    ts = _tile(S, 256, 8)

    def body(l_ref, gq_ref, gk_ref, cq_ref, ckv_ref):
        a, b = l_ref[:, :QL], l_ref[:, QL:QL + KL]
        cq_ref[...] = (a * _rstd(a, QL) * gq_ref[...]).astype(cq_ref.dtype)
        ckv_ref[...] = (b * _rstd(b, KL) * gk_ref[...]).astype(ckv_ref.dtype)

    return pl.pallas_call(
        body, name=name, out_shape=(_act((S, QL)), _act((S, KL))), grid=(S // ts,),
        in_specs=[pl.BlockSpec((ts, W), lambda i: (i, 0)), pl.BlockSpec((1, QL), lambda i: (0, 0)),
                  pl.BlockSpec((1, KL), lambda i: (0, 0))],
        out_specs=(pl.BlockSpec((ts, QL), lambda i: (i, 0)), pl.BlockSpec((ts, KL), lambda i: (i, 0))),
        compiler_params=_params("parallel"))(lat, g_cq, g_ckv)


def mla_latent_bwd(dcq, dckv, dkpe, lat, g_cq, g_ckv, name):
    S, W = lat.shape
    QL, KL = g_cq.shape[1], g_ckv.shape[1]
    ts = _tile(S, 256, 8)

    def body(dq_ref, dk_ref, dp_ref, l_ref, gq_ref, gk_ref, o_ref, dgq_ref, dgk_ref):
        first = pl.program_id(0) == 0
        da, ga = _norm_bwd(l_ref[:, :QL], gq_ref[...], dq_ref[...], QL)
        db, gb = _norm_bwd(l_ref[:, QL:QL + KL], gk_ref[...], dk_ref[...], KL)
        o_ref[:, :QL] = da.astype(o_ref.dtype)
        o_ref[:, QL:QL + KL] = db.astype(o_ref.dtype)
        o_ref[:, QL + KL:] = dp_ref[...].astype(o_ref.dtype)
        _accumulate(dgq_ref, jnp.sum(ga, axis=0, keepdims=True), first)
        _accumulate(dgk_ref, jnp.sum(gb, axis=0, keepdims=True), first)

    row = lambda n: pl.BlockSpec((ts, n), lambda i: (i, 0))
    vec = lambda n: pl.BlockSpec((1, n), lambda i: (0, 0))
    return pl.pallas_call(
        body, name=name, out_shape=(_act((S, W)), _f32((1, QL)), _f32((1, KL))), grid=(S // ts,),
        in_specs=[row(QL), row(KL), row(LANES), row(W), vec(QL), vec(KL)], out_specs=(row(W), vec(QL), vec(KL)),
        compiler_params=_params("arbitrary"))(dcq, dckv, dkpe, lat, g_cq, g_ckv)


def mla_q_prep_fwd(qraw, g, tabs, H, name):
    S = qraw.shape[0]
    ts = _tile(S, PREP_ROWS, 8)

    def body(x_ref, g_ref, c_ref, a_ref, b_ref, o_ref):
        y1, y2, _ = _head_norm(x_ref[:, :LANES], x_ref[:, LANES:], g_ref[...])
        o_ref[:, :LANES] = y1.astype(o_ref.dtype)
        o_ref[:, LANES:] = _rope(y2, c_ref[...], a_ref[...], b_ref[...]).astype(o_ref.dtype)

    tab = pl.BlockSpec((ts, LANES), lambda i, h: (i, 0))
    return pl.pallas_call(
        body, name=name, out_shape=_act((H, S, QK_PAD)), grid=(S // ts, H),
        in_specs=[pl.BlockSpec((ts, QK_PAD), lambda i, h: (i, h)), pl.BlockSpec((1, QK_PAD), lambda i, h: (0, 0)), tab, tab, tab],
        out_specs=pl.BlockSpec((None, ts, QK_PAD), lambda i, h: (h, i, 0)),
        compiler_params=_params("parallel", "parallel"))(qraw, g, *tabs)


def mla_q_prep_bwd(dq, qraw, g, tabs, H, name):
    S = qraw.shape[0]
    ts = _tile(S, PREP_ROWS, 8)

    def body(d_ref, x_ref, g_ref, c_ref, a_ref, b_ref, o_ref, dg_ref):
        d2 = _rope_t(d_ref[:, LANES:], c_ref[...], a_ref[...], b_ref[...])
        dx1, dx2, g1, g2 = _head_norm_bwd(x_ref[:, :LANES], x_ref[:, LANES:], g_ref[...], d_ref[:, :LANES], d2)
        o_ref[:, :LANES] = dx1.astype(o_ref.dtype)
        o_ref[:, LANES:] = dx2.astype(o_ref.dtype)
        first = jnp.logical_and(pl.program_id(0) == 0, pl.program_id(1) == 0)
        part = jnp.concatenate([jnp.sum(g1, axis=0, keepdims=True), jnp.sum(g2, axis=0, keepdims=True)], axis=1)
        _accumulate(dg_ref, part, first)

    tab = pl.BlockSpec((ts, LANES), lambda i, h: (i, 0))
    vec = pl.BlockSpec((1, QK_PAD), lambda i, h: (0, 0))
    return pl.pallas_call(
        body, name=name, out_shape=(_act((S, H * QK_PAD)), _f32((1, QK_PAD))), grid=(S // ts, H),
        in_specs=[pl.BlockSpec((None, ts, QK_PAD), lambda i, h: (h, i, 0)), pl.BlockSpec((ts, QK_PAD), lambda i, h: (i, h)),
                  vec, tab, tab, tab],
        out_specs=(pl.BlockSpec((ts, QK_PAD), lambda i, h: (i, h)), vec),
        compiler_params=_params("arbitrary", "arbitrary"))(dq, qraw, g, *tabs)


def mla_k_prep_fwd(kvraw, lat, g, tabs, H, pe_blk, name):
    S = kvraw.shape[0]
    ts = _tile(S, PREP_ROWS, 8)

    def body(x_ref, p_ref, g_ref, c_ref, a_ref, b_ref, k_ref, v_ref):
        y1, y2, _ = _head_norm(x_ref[:, :LANES], p_ref[...], g_ref[...])
        k_ref[:, :LANES] = y1.astype(k_ref.dtype)
        k_ref[:, LANES:] = _rope(y2, c_ref[...], a_ref[...], b_ref[...]).astype(k_ref.dtype)
        v_ref[...] = x_ref[:, LANES:].astype(v_ref.dtype)

    tab = pl.BlockSpec((ts, LANES), lambda i, h: (i, 0))
    return pl.pallas_call(
        body, name=name, out_shape=(_act((H, S, QK_PAD)), _act((H, S, LANES))), grid=(S // ts, H),
        in_specs=[pl.BlockSpec((ts, QK_PAD), lambda i, h: (i, h)), pl.BlockSpec((ts, LANES), lambda i, h: (i, pe_blk)),
                  pl.BlockSpec((1, QK_PAD), lambda i, h: (0, 0)), tab, tab, tab],
        out_specs=(pl.BlockSpec((None, ts, QK_PAD), lambda i, h: (h, i, 0)), pl.BlockSpec((None, ts, LANES), lambda i, h: (h, i, 0))),
        compiler_params=_params("parallel", "parallel"))(kvraw, lat, g, *tabs)


def mla_k_prep_bwd(dk, dv, kvraw, lat, g, tabs, H, pe_blk, name):
    S = kvraw.shape[0]
    ts = _tile(S, PREP_ROWS, 8)

    def body(dk_ref, dv_ref, x_ref, p_ref, g_ref, c_ref, a_ref, b_ref, o_ref, dp_ref, dg_ref):
        i, h = pl.program_id(0), pl.program_id(1)
        d2 = _rope_t(dk_ref[:, LANES:], c_ref[...], a_ref[...], b_ref[...])
        dx1, dx2, g1, g2 = _head_norm_bwd(x_ref[:, :LANES], p_ref[...], g_ref[...], dk_ref[:, :LANES], d2)
        o_ref[:, :LANES] = dx1.astype(o_ref.dtype)
        o_ref[:, LANES:] = dv_ref[...].astype(o_ref.dtype)
        _accumulate(dp_ref, dx2, h == 0)
        part = jnp.concatenate([jnp.sum(g1, axis=0, keepdims=True), jnp.sum(g2, axis=0, keepdims=True)], axis=1)
        _accumulate(dg_ref, part, jnp.logical_and(i == 0, h == 0))

    tab = pl.BlockSpec((ts, LANES), lambda i, h: (i, 0))
    vec = pl.BlockSpec((1, QK_PAD), lambda i, h: (0, 0))
    return pl.pallas_call(
        body, name=name, out_shape=(_act((S, H * QK_PAD)), _f32((S, LANES)), _f32((1, QK_PAD))), grid=(S // ts, H),
        in_specs=[pl.BlockSpec((None, ts, QK_PAD), lambda i, h: (h, i, 0)), pl.BlockSpec((None, ts, LANES), lambda i, h: (h, i, 0)),
                  pl.BlockSpec((ts, QK_PAD), lambda i, h: (i, h)), pl.BlockSpec((ts, LANES), lambda i, h: (i, pe_blk)),
                  vec, tab, tab, tab],
        out_specs=(pl.BlockSpec((ts, QK_PAD), lambda i, h: (i, h)), tab, vec),
        compiler_params=_params("arbitrary", "arbitrary"))(dk, dv, kvraw, lat, g, *tabs)


def _causal_scores(q, k, scale, diagonal):
    s = lax.dot_general(q, k, NT, preferred_element_type=jnp.float32) * scale
    if not diagonal:
        return s
    row = lax.broadcasted_iota(jnp.int32, s.shape, 0)
    col = lax.broadcasted_iota(jnp.int32, s.shape, 1)
    return jnp.where(col <= row, s, NEG)


def _on_causal_blocks(qi, ki, step):
    @pl.when(ki < qi)
    def _():
        step(False)

    @pl.when(ki == qi)
    def _():
        step(True)


def mla_attention_fwd(q, k, v, name):
    H, S, _ = q.shape
    t = _tile(S, ATTN_BLOCK)
    n = S // t
    scale = 1.0 / math.sqrt(QK_DIM)

    def body(q_ref, k_ref, v_ref, o_ref, lse_ref, m_sc, l_sc, acc):
        qi, ki = pl.program_id(1), pl.program_id(2)

        @pl.when(ki == 0)
        def _():
            m_sc[...] = jnp.full(m_sc.shape, NEG, jnp.float32)
            l_sc[...] = jnp.zeros_like(l_sc)
            acc[...] = jnp.zeros_like(acc)

        def step(diagonal):
            s = _causal_scores(q_ref[...], k_ref[...], scale, diagonal)
            m_new = jnp.maximum(m_sc[...], jnp.max(s, axis=-1, keepdims=True))
            alpha = jnp.exp(m_sc[...] - m_new)
            p = jnp.exp(s - m_new)
            l_sc[...] = alpha * l_sc[...] + jnp.sum(p, axis=-1, keepdims=True)
            acc[...] = alpha * acc[...] + jnp.dot(p.astype(MXU_DTYPE), v_ref[...], preferred_element_type=jnp.float32)
            m_sc[...] = m_new

        _on_causal_blocks(qi, ki, step)

        @pl.when(ki == qi)
        def _():
            o_ref[...] = (acc[...] / l_sc[...]).astype(o_ref.dtype)
            lse_ref[...] = m_sc[...] + jnp.log(l_sc[...])

    kv = lambda w: pl.BlockSpec((None, t, w), lambda h, qi, ki: (h, jnp.minimum(ki, qi), 0))
    return pl.pallas_call(
        body, name=name, out_shape=(_act((S, H * LANES)), _f32((H, S, 1))), grid=(H, n, n),
        in_specs=[pl.BlockSpec((None, t, QK_PAD), lambda h, qi, ki: (h, qi, 0)), kv(QK_PAD), kv(LANES)],
        out_specs=(pl.BlockSpec((t, LANES), lambda h, qi, ki: (qi, h)), pl.BlockSpec((None, t, 1), lambda h, qi, ki: (h, qi, 0))),
        scratch_shapes=[pltpu.VMEM((t, 1), jnp.float32), pltpu.VMEM((t, 1), jnp.float32), pltpu.VMEM((t, LANES), jnp.float32)],
        compiler_params=_params("parallel", "parallel", "arbitrary"))(q, k, v)


def mla_attention_bwd(q, k, v, do, o, lse, name):
    H, S, _ = q.shape
    t = _tile(S, ATTN_BLOCK)
    n = S // t
    scale = 1.0 / math.sqrt(QK_DIM)

    def body(q_ref, k_ref, v_ref, do_ref, o_ref, lse_ref, dq_ref, dk_ref, dv_ref, dk_acc, dv_acc):
        ki, qi = pl.program_id(1), pl.program_id(2)
        rows = pl.ds(pl.multiple_of(qi * t, t), t)

        @pl.when(qi == 0)
        def _():
            dk_acc[...] = jnp.zeros_like(dk_acc)
            dv_acc[...] = jnp.zeros_like(dv_acc)

        @pl.when(ki == 0)
        def _():
            dq_ref[rows, :] = jnp.zeros((t, QK_PAD), jnp.float32)

        def step(diagonal):
            p = jnp.exp(_causal_scores(q_ref[...], k_ref[...], scale, diagonal) - lse_ref[...])
            dof = do_ref[...]
            dob = dof.astype(MXU_DTYPE)
            delta = jnp.sum(dof * o_ref[...].astype(jnp.float32), axis=-1, keepdims=True)
            dv_acc[...] += lax.dot_general(p.astype(MXU_DTYPE), dob, TN, preferred_element_type=jnp.float32)
            dp = lax.dot_general(dob, v_ref[...], NT, preferred_element_type=jnp.float32)
            ds = (p * (dp - delta)).astype(MXU_DTYPE)
            dk_acc[...] += lax.dot_general(ds, q_ref[...], TN, preferred_element_type=jnp.float32)
            dq_ref[rows, :] += jnp.dot(ds, k_ref[...], preferred_element_type=jnp.float32) * scale

        _on_causal_blocks(qi, ki, step)

        @pl.when(qi == n - 1)
        def _():
            dk_ref[...] = dk_acc[...] * scale
            dv_ref[...] = dv_acc[...]

    qrow = lambda h, ki, qi: (h, jnp.maximum(qi, ki), 0)
    kv = lambda w: pl.BlockSpec((None, t, w), lambda h, ki, qi: (h, ki, 0))
    col = pl.BlockSpec((None, t, 1), qrow)
    head = pl.BlockSpec((t, LANES), lambda h, ki, qi: (jnp.maximum(qi, ki), h))
    return pl.pallas_call(
        body, name=name, out_shape=(_f32((H, S, QK_PAD)), _f32((H, S, QK_PAD)), _f32((H, S, LANES))), grid=(H, n, n),
        in_specs=[pl.BlockSpec((None, t, QK_PAD), qrow), kv(QK_PAD), kv(LANES),
                  head, head, col],
        out_specs=(pl.BlockSpec((None, S, QK_PAD), lambda h, ki, qi: (h, 0, 0)), kv(QK_PAD), kv(LANES)),
        scratch_shapes=[pltpu.VMEM((t, QK_PAD), jnp.float32), pltpu.VMEM((t, LANES), jnp.float32)],
        compiler_params=_params("parallel", "arbitrary", "arbitrary"))(q, k, v, do, o, lse)


def _alibi_slopes(G, Hd):
    k = np.arange(1, G * Hd + 1, dtype=np.float32)
    s = (2.0 ** (-8.0 * k / (G * Hd))).astype(np.float32).reshape(G, Hd)
    return jnp.asarray(np.broadcast_to(s[:, :, None, None], (G, Hd, 1, LANES)).copy())


def _dil_scores(qn, kn, scale, slope_d, prev, valid):
    s = lax.dot_general(qn, kn, NT, preferred_element_type=jnp.float32) * scale
    iq = lax.broadcasted_iota(jnp.int32, s.shape, 0)
    ik = lax.broadcasted_iota(jnp.int32, s.shape, 1)
    dist = iq - ik + (BLK if prev else 0)
    ok = (ik >= iq) if prev else (ik <= iq)
    s = s - slope_d * dist.astype(jnp.float32)
    return jnp.where(jnp.logical_and(ok, valid), s, NEG)


def _dil_heads(d, Hd, block_bytes):
    hb = max(1, min(Hd, block_bytes // (BLK * LANES * 4))) if d == 1 else 1
    assert Hd % hb == 0, (Hd, hb)
    return hb


def _dil_rows(r, d):
    return pl.ds(r, BLK, stride=d) if d > 1 else slice(None)


def _loop_residues(d, residue, init):
    if d == 1:
        return residue(0, init)
    return lax.fori_loop(0, d // 2, lambda i, carry: residue(2 * i + 1, residue(2 * i, carry)), init)


def _dil_specs(d, nblk, Hd, G, g, hb):
    def spec(kind, shift):
        col0 = (kind * G + g) * Hd // hb
        return pl.BlockSpec((BLK * d, hb * LANES), lambda n, hg: (jnp.clip(n + shift, 0, nblk - 1), col0 + hg))
    return spec


def _head_spec(d, nblk, hb, shift):
    return pl.BlockSpec((BLK * d, hb * LANES), lambda n, hg: (jnp.clip(n + shift, 0, nblk - 1), hg))


def dilated_fwd(qkv, gq, gk, slopes, g, d, Hd, G, name):
    S, C = qkv.shape
    nblk = S // (BLK * d)
    hb = _dil_heads(d, Hd, 2 << 20)
    scale = 1.0 / math.sqrt(LANES)
    spec = _dil_specs(d, nblk, Hd, G, g, hb)

    def body(q_ref, kc_ref, kp_ref, vc_ref, vp_ref, gq_ref, gk_ref, sl_ref, o_ref, l_ref):
        n = pl.program_id(0)
        nrm = lambda t, gg: (t * _rstd(t, LANES) * gg).astype(MXU_DTYPE)

        def residue(r, carry):
            rows = _dil_rows(r, d)
            for hh in range(hb):
                cols = slice(hh * LANES, (hh + 1) * LANES)
                slope_d = sl_ref[hh][:, :1] * float(d)
                qn = nrm(q_ref[rows, cols], gq_ref[...])
                sc = _dil_scores(qn, nrm(kc_ref[rows, cols], gk_ref[...]), scale, slope_d, False, True)
                sp = _dil_scores(qn, nrm(kp_ref[rows, cols], gk_ref[...]), scale, slope_d, True, n > 0)
                m = jnp.maximum(jnp.max(sc, axis=-1, keepdims=True), jnp.max(sp, axis=-1, keepdims=True))
                lse = m + jnp.log(jnp.sum(jnp.exp(sc - m), axis=-1, keepdims=True) + jnp.sum(jnp.exp(sp - m), axis=-1, keepdims=True))
                o = jnp.dot(jnp.exp(sc - lse).astype(MXU_DTYPE), vc_ref[rows, cols].astype(MXU_DTYPE), preferred_element_type=jnp.float32)
                o = o + jnp.dot(jnp.exp(sp - lse).astype(MXU_DTYPE), vp_ref[rows, cols].astype(MXU_DTYPE), preferred_element_type=jnp.float32)
                o_ref[rows, cols] = o
                l_ref[rows, cols] = jnp.broadcast_to(lse, (BLK, LANES))
            return carry

        _loop_residues(d, residue, 0)

    vec = pl.BlockSpec((1, LANES), lambda n, hg: (0, 0))
    out = _head_spec(d, nblk, hb, 0)
    return pl.pallas_call(
        body, name=name, out_shape=(_f32((S, Hd * LANES)), _f32((S, Hd * LANES))), grid=(nblk, Hd // hb),
        in_specs=[spec(0, 0), spec(1, 0), spec(1, -1), spec(2, 0), spec(2, -1), vec, vec,
                  pl.BlockSpec((None, hb, 1, LANES), lambda n, hg: (g, hg, 0, 0))],
        out_specs=(out, out), compiler_params=_params("parallel", "parallel"),
    )(qkv, qkv, qkv, qkv, qkv, gq, gk, slopes)


def dilated_merge(os_, ls_, name):
    S, W = os_[0].shape
    G = len(os_)
    ts, tw = _tile(S, 512, 8), _tile(W, 512)

    def body(*refs):
        o_refs, l_refs, (o_ref, t_ref) = refs[:G], refs[G:2 * G], refs[2 * G:]
        ls = [r[...] for r in l_refs]
        m = ls[0]
        for l in ls[1:]:
            m = jnp.maximum(m, l)
        es = [jnp.exp(l - m) for l in ls]
        tot = es[0]
        for e in es[1:]:
            tot = tot + e
        acc = o_refs[0][...] * (es[0] / tot)
        for r, e in zip(o_refs[1:], es[1:]):
            acc = acc + r[...] * (e / tot)
        o_ref[...] = acc.astype(o_ref.dtype)
        t_ref[...] = m + jnp.log(tot)

    blk = pl.BlockSpec((ts, tw), lambda i, j: (i, j))
    return pl.pallas_call(
        body, name=name, out_shape=(_act((S, W)), _f32((S, W))), grid=(S // ts, W // tw),
        in_specs=[blk] * (2 * G), out_specs=(blk, blk), compiler_params=_params("parallel", "parallel"))(*os_, *ls_)


def dilated_delta(do, o, name):
    S, W = do.shape
    ts = _tile(S, 512, 8)

    def body(d_ref, o_ref, out_ref):
        out_ref[...] = jnp.broadcast_to(jnp.sum(d_ref[...] * o_ref[...].astype(jnp.float32), axis=-1, keepdims=True), out_ref.shape)

    blk = pl.BlockSpec((ts, LANES), lambda i, h: (i, h))
    return pl.pallas_call(body, name=name, out_shape=_f32((S, W)), grid=(S // ts, W // LANES), in_specs=[blk, blk],
                          out_specs=blk, compiler_params=_params("parallel", "parallel"))(do, o)


def dilated_bwd(qkv, do, lse, delta, gq, gk, slopes, g, d, Hd, G, name):
    S, C = qkv.shape
    nblk = S // (BLK * d)
    W = Hd * LANES
    scale = 1.0 / math.sqrt(LANES)
    hb = _dil_heads(d, Hd, 1 << 20)
    spec = _dil_specs(d, nblk, Hd, G, g, hb)
    hspec = lambda shift: _head_spec(d, nblk, hb, shift)

    def body(q_ref, qx_ref, kc_ref, kp_ref, vc_ref, vp_ref, do_ref, dox_ref, l_ref, lx_ref, dl_ref, dlx_ref,
             gq_ref, gk_ref, sl_ref, dq_ref, dk_ref, dv_ref, dgq_ref, dgk_ref):
        n = pl.program_id(0)
        gqv, gkv = gq_ref[...], gk_ref[...]
        nrm = lambda t, gg: (t * _rstd(t, LANES) * gg).astype(MXU_DTYPE)
        f32dot = lambda a, b, dn: lax.dot_general(a, b, dn, preferred_element_type=jnp.float32)

        def residue(r, carry):
            rows = _dil_rows(r, d)
            dgq_sum, dgk_sum = carry
            for hh in range(hb):
                cols = slice(hh * LANES, (hh + 1) * LANES)
                slope_d = sl_ref[hh][:, :1] * float(d)
                q, kc = q_ref[rows, cols], kc_ref[rows, cols]
                qn, qxn = nrm(q, gqv), nrm(qx_ref[rows, cols], gqv)
                kcn, kpn = nrm(kc, gkv), nrm(kp_ref[rows, cols], gkv)
                vc, vp = vc_ref[rows, cols].astype(MXU_DTYPE), vp_ref[rows, cols].astype(MXU_DTYPE)
                dob, doxb = do_ref[rows, cols].astype(MXU_DTYPE), dox_ref[rows, cols].astype(MXU_DTYPE)
                lrow, lxrow = l_ref[rows, cols][:, :1], lx_ref[rows, cols][:, :1]
                drow, dxrow = dl_ref[rows, cols][:, :1], dlx_ref[rows, cols][:, :1]
                pc = jnp.exp(_dil_scores(qn, kcn, scale, slope_d, False, True) - lrow)
                pp = jnp.exp(_dil_scores(qn, kpn, scale, slope_d, True, n > 0) - lrow)
                dsc = pc * (f32dot(dob, vc, NT) - drow)
                dsp = pp * (f32dot(dob, vp, NT) - drow)
                dqn = (jnp.dot(dsc.astype(MXU_DTYPE), kcn, preferred_element_type=jnp.float32)
                       + jnp.dot(dsp.astype(MXU_DTYPE), kpn, preferred_element_type=jnp.float32)) * scale
                dq, dgq = _norm_bwd(q, gqv, dqn, LANES)
                dq_ref[rows, cols] = dq
                px = jnp.exp(_dil_scores(qxn, kcn, scale, slope_d, True, n < nblk - 1) - lxrow)
                dsx = px * (f32dot(doxb, vc, NT) - dxrow)
                dkn = (f32dot(dsc.astype(MXU_DTYPE), qn, TN) + f32dot(dsx.astype(MXU_DTYPE), qxn, TN)) * scale
                dk, dgk = _norm_bwd(kc, gkv, dkn, LANES)
                dk_ref[rows, cols] = dk
                dv_ref[rows, cols] = f32dot(pc.astype(MXU_DTYPE), dob, TN) + f32dot(px.astype(MXU_DTYPE), doxb, TN)
                dgq_sum = dgq_sum + jnp.sum(dgq, axis=0, keepdims=True)
                dgk_sum = dgk_sum + jnp.sum(dgk, axis=0, keepdims=True)
            return dgq_sum, dgk_sum

        zero = jnp.zeros((1, LANES), jnp.float32)
        dgq_sum, dgk_sum = _loop_residues(d, residue, (zero, zero))
        first = jnp.logical_and(n == 0, pl.program_id(1) == 0)
        _accumulate(dgq_ref, dgq_sum, first)
        _accumulate(dgk_ref, dgk_sum, first)

    vec = pl.BlockSpec((1, LANES), lambda n, hg: (0, 0))
    out = hspec(0)
    return pl.pallas_call(
        body, name=name, out_shape=(_f32((S, W)), _f32((S, W)), _f32((S, W)), _f32((1, LANES)), _f32((1, LANES))),
        grid=(nblk, Hd // hb),
        in_specs=[spec(0, 0), spec(0, 1), spec(1, 0), spec(1, -1), spec(2, 0), spec(2, -1), hspec(0), hspec(1), hspec(0), hspec(1),
                  hspec(0), hspec(1), vec, vec, pl.BlockSpec((None, hb, 1, LANES), lambda n, hg: (g, hg, 0, 0))],
        out_specs=(out, out, out, vec, vec), compiler_params=_params("arbitrary", "arbitrary"),
    )(qkv, qkv, qkv, qkv, qkv, qkv, do, do, lse, lse, delta, delta, gq, gk, slopes)


def adamw(w, g, m, v, layer, prev, name):
    L, r, c = w.shape
    tr, tc = _tile(r, 512, 8), _tile(c, 1024)
    c1 = 1.0 / (1.0 - ADAM_B1 ** ADAM_STEP)
    c2 = 1.0 / (1.0 - ADAM_B2 ** ADAM_STEP)

    def body(*refs):
        w_ref, g_ref, m_ref, v_ref = refs[:4]
        go_ref, d_ref, mo_ref, vo_ref = refs[-4:]
        gv = g_ref[...]
        mn = ADAM_B1 * m_ref[...] + (1.0 - ADAM_B1) * gv
        vn = ADAM_B2 * v_ref[...] + (1.0 - ADAM_B2) * (gv * gv)
        go_ref[...] = gv
        d_ref[...] = -ADAM_LR * ((mn * c1) / (jnp.sqrt(vn * c2) + ADAM_EPS) + ADAM_WD * w_ref[...])
        mo_ref[...] = mn
        vo_ref[...] = vn

    lay = pl.BlockSpec((None, tr, tc), lambda i, j: (layer, i, j))
    flat = pl.BlockSpec((tr, tc), lambda i, j: (i, j))
    ins = [w, g, m, v] + (list(prev) if prev is not None else [])
    in_specs = [lay, flat, lay, lay] + ([ANY] * 4 if prev is not None else [])
    return pl.pallas_call(
        body, name=name, out_shape=tuple(_f32((L, r, c)) for _ in range(4)), grid=(r // tr, c // tc),
        in_specs=in_specs, out_specs=(lay, lay, lay, lay),
        input_output_aliases=({4 + k: k for k in range(4)} if prev is not None else {}),
        compiler_params=_params("parallel", "parallel"))(*ins)


def _ffn_fwd(h, g, W, kind, tag):
    xn = rmsnorm_fwd(h, g, tag + "_norm")
    u, a = swiglu_in(xn, W(kind + "_w_in", h), tag + "_in")
    out = matmul(a, W(kind + "_w_out", a), scale=0.5, res=h, name=tag + "_out", tk=2816)
    return out, (h, xn, u, a)


def _ffn_bwd(dout, saved, g, W, emit, kind, tag):
    h, xn, u, a = saved
    emit(kind + "_w_out", matmul(a, dout, ta=True, scale=0.5, out_dtype=WIRE_DTYPE, out_axis=0, name=tag + "_dwout", tm=1408, tk=2048))
    du = Sharded(swiglu_out_bwd(dout, W(kind + "_w_out", None), u, 0.5, tag + "_da"), 0, 1)
    emit(kind + "_w_in", matmul(xn, du, ta=True, out_dtype=WIRE_DTYPE, out_axis=1, name=tag + "_dwin", tn=1408, tk=2048))
    dxn = matmul(du, W(kind + "_w_in", None), tb=True, name=tag + "_dxn", tk=2816)
    return rmsnorm_bwd(h, g, dxn, dout, tag + "_dnorm")


def _pad_gain(g):
    return jnp.pad(g, ((0, 0), (0, QK_PAD - QK_DIM)))


def _mla_fwd(h, P, W, tabs, H):
    g_mix, g_cq, g_ckv = P["mix_norm"][0:1], P["mla_g_cq"], P["mla_g_ckv"]
    pe_blk = (g_cq.shape[1] + g_ckv.shape[1]) // LANES
    xn = rmsnorm_fwd(h, g_mix, "mla_norm")
    w_down = W("mla_w_down", h)
    lat = matmul(xn, w_down, name="mla_down", tn=w_down.shape[1])
    cq, ckv = mla_latent_fwd(lat, g_cq, g_ckv, "mla_latent")
    qraw = matmul(cq, W("mla_w_uq", lat), name="mla_uq")
    kvraw = matmul(ckv, W("mla_w_ukv", qraw), name="mla_ukv")
    q = mla_q_prep_fwd(qraw, _pad_gain(P["mla_g_qn"]), tabs, H, "mla_qprep")
    k, v = mla_k_prep_fwd(kvraw, lat, _pad_gain(P["mla_g_kn"]), tabs, H, pe_blk, "mla_kprep")
    o, lse = mla_attention_fwd(q, k, v, "mla_attn")
    out = matmul(o, W("mla_w_o", lse), res=h, name="mla_o")
    return out, (h, xn, lat, cq, ckv, qraw, kvraw, q, k, v, o, lse, pe_blk)


def _mla_bwd(dout, saved, P, W, emit, tabs, H):
    h, xn, lat, cq, ckv, qraw, kvraw, q, k, v, o, lse, pe_blk = saved
    emit("mla_w_o", matmul(o, dout, ta=True, out_dtype=WIRE_DTYPE, out_axis=0, name="mla_dwo", tm=512))
    do = matmul(dout, W("mla_w_o", None), tb=True, name="mla_do", tn=512)
    dq, dk, dv = mla_attention_bwd(q, k, v, do, o, lse, "mla_attn_bwd")
    dqraw, dgq = mla_q_prep_bwd(dq, qraw, _pad_gain(P["mla_g_qn"]), tabs, H, "mla_dqprep")
    dkvraw, dkpe, dgk = mla_k_prep_bwd(dk, dv, kvraw, lat, _pad_gain(P["mla_g_kn"]), tabs, H, pe_blk, "mla_dkprep")
    emit("mla_w_uq", matmul(cq, dqraw, ta=True, out_dtype=WIRE_DTYPE, out_axis=1, name="mla_dwuq"))
    dcq = matmul(dqraw, W("mla_w_uq", None), tb=True, name="mla_dcq", tk=1024)
    emit("mla_w_ukv", matmul(ckv, dkvraw, ta=True, out_dtype=WIRE_DTYPE, out_axis=1, name="mla_dwukv"))
    dckv = matmul(dkvraw, W("mla_w_ukv", None), tb=True, name="mla_dckv", tk=1024)
    dlat, dgcq, dgckv = mla_latent_bwd(dcq, dckv, dkpe, lat, P["mla_g_cq"], P["mla_g_ckv"], "mla_dlatent")
    emit("mla_w_down", matmul(xn, dlat, ta=True, out_dtype=WIRE_DTYPE, out_axis=0, name="mla_dwdown", tm=512, tn=dlat.shape[1]))
    dxn = matmul(dlat, W("mla_w_down", None), tb=True, name="mla_dxn", tn=512, tk=dlat.shape[1])
    dh, dgm = rmsnorm_bwd(h, P["mix_norm"][0:1], dxn, dout, "mla_dnorm")
    return dh, dgm, dict(mla_g_qn=dgq[:, :QK_DIM], mla_g_kn=dgk[:, :QK_DIM], mla_g_cq=dgcq, mla_g_ckv=dgckv)


def _dil_fwd(h, P, W, slopes, Hd):
    G = len(DIL_PAIRS)
    xn = rmsnorm_fwd(h, P["mix_norm"][1:2], "dil_norm")
    qkv = matmul(xn, W("dil_w_qkv", h), name="dil_qkv", tn=1152)
    os_, ls_ = [], []
    for g, (_, d) in enumerate(DIL_PAIRS):
        o_g, l_g = dilated_fwd(qkv, P["dil_g_qn"], P["dil_g_kn"], slopes, g, d, Hd, G, f"dil_attn{g}")
        os_.append(o_g)
        ls_.append(l_g)
    o, lse = dilated_merge(os_, ls_, "dil_merge")
    out = matmul(o, W("dil_w_o", lse), res=h, name="dil_o", tn=512)
    return out, (h, xn, qkv, o, lse)


def _dil_bwd(dout, saved, P, W, emit, slopes, Hd):
    h, xn, qkv, o, lse = saved
    ngrp = len(DIL_PAIRS)
    emit("dil_w_o", matmul(o, dout, ta=True, out_dtype=WIRE_DTYPE, out_axis=1, name="dil_dwo", tn=512))
    do = matmul(dout, W("dil_w_o", None), tb=True, name="dil_do", tk=512)
    delta = dilated_delta(do, o, "dil_delta")
    parts = [dilated_bwd(qkv, do, lse, delta, P["dil_g_qn"], P["dil_g_kn"], slopes, g, d, Hd, ngrp, f"dil_dattn{g}")
             for g, (_, d) in enumerate(DIL_PAIRS)]
    dqkv = jnp.concatenate([p[kind] for kind in range(3) for p in parts], axis=1).astype(MXU_DTYPE)
    emit("dil_w_qkv", matmul(xn, dqkv, ta=True, out_dtype=WIRE_DTYPE, out_axis=1, name="dil_dwqkv", tn=1152))
    dxn = matmul(dqkv, W("dil_w_qkv", None), tb=True, name="dil_dxn", tk=2304)
    dh, dgm = rmsnorm_bwd(h, P["mix_norm"][1:2], dxn, dout, "dil_dnorm")
    return dh, dgm, dict(dil_g_qn=parts[0][3] + parts[1][3] + parts[2][3], dil_g_kn=parts[0][4] + parts[1][4] + parts[2][4])


def local_step(x, target, P, get_w, on_grad):
    S, D = x.shape
    H, Hd = MLA_HEADS, DIL_HEADS
    tabs = rope_tables(S)
    slopes = _alibi_slopes(len(DIL_PAIRS), Hd)
    cache = {}

    def weights_of(layer):
        def W(name, after):
            if (name, layer) not in cache:
                cache[name, layer] = Sharded(get_w(name, layer, after), 0, SHARD_AXIS[name])
            return cache[name, layer]
        return W

    row = lambda name, i: P[name][i:i + 1]
    h = x
    saved = []
    for i in range(2):
        W = weights_of(i)
        h, s1 = _ffn_fwd(h, row("ffn1_norm", i), W, "ffn1", f"l{i}_ffn1")
        h, sm = _mla_fwd(h, P, weights_of(0), tabs, H) if i == 0 else _dil_fwd(h, P, weights_of(0), slopes, Hd)
        h, s2 = _ffn_fwd(h, row("ffn2_norm", i), W, "ffn2", f"l{i}_ffn2")
        saved.append((s1, sm, s2))
    loss, dh = loss_head(h, target, "loss")

    gs = {n: [None, None] for n in ("ffn1_norm", "mix_norm", "ffn2_norm")}
    for i in (1, 0):
        s1, sm, s2 = saved[i]
        W = weights_of(i)
        emit = lambda name, g4, layer=i: on_grad(name, layer, g4)
        emit0 = lambda name, g4: on_grad(name, 0, g4)
        dh, gs["ffn2_norm"][i] = _ffn_bwd(dh, s2, row("ffn2_norm", i), W, emit, "ffn2", f"l{i}_ffn2")
        if i == 0:
            dh, gs["mix_norm"][i], gm = _mla_bwd(dh, sm, P, weights_of(0), emit0, tabs, H)
        else:
            dh, gs["mix_norm"][i], gm = _dil_bwd(dh, sm, P, weights_of(0), emit0, slopes, Hd)
        gs.update({n: [val] for n, val in gm.items()})
        dh, gs["ffn1_norm"][i] = _ffn_bwd(dh, s1, row("ffn1_norm", i), W, emit, "ffn1", f"l{i}_ffn1")
    gsmall = {n: jnp.concatenate(v, axis=0) for n, v in gs.items()}
    return loss, dh, gsmall


def _pad_heads(w, real, padded):
    lead, n = w.shape[:-1], w.shape[-1] // real
    w = jnp.pad(w.reshape(*lead, n, real), [(0, 0)] * (len(lead) + 1) + [(0, padded - real)])
    return w.reshape(*lead, n * padded)


def _unpad_heads(w, real, padded):
    lead, n = w.shape[:-1], w.shape[-1] // padded
    return w.reshape(*lead, n, padded)[..., :real].reshape(*lead, n * real)


def _pack_small(gs):
    flat = jnp.concatenate([gs[n].reshape(-1) for n in SMALL])
    rows = -(-flat.shape[0] // LANES)
    rows = -(-rows // 8) * 8
    return jnp.pad(flat, (0, rows * LANES - flat.shape[0])).reshape(rows, LANES)


def _unpack_small(packed, like):
    flat, out, off = packed.reshape(-1), {}, 0
    for n in SMALL:
        size = int(np.prod(like[n].shape))
        out[n] = flat[off:off + size].reshape(like[n].shape)
        off += size
    return out


def kernel(x, ffn1_norm, ffn1_w_in, ffn1_w_out, mix_norm, ffn2_norm, ffn2_w_in, ffn2_w_out, mla_w_down, mla_g_cq, mla_g_ckv, mla_w_uq, mla_w_ukv, mla_g_qn, mla_g_kn, mla_w_o, dil_w_qkv, dil_g_qn, dil_g_kn, dil_w_o, loss_target, m_ffn1_norm, m_ffn1_w_in, m_ffn1_w_out, m_mix_norm, m_ffn2_norm, m_ffn2_w_in, m_ffn2_w_out, m_mla_w_down, m_mla_g_cq, m_mla_g_ckv, m_mla_w_uq, m_mla_w_ukv, m_mla_g_qn, m_mla_g_kn, m_mla_w_o, m_dil_w_qkv, m_dil_g_qn, m_dil_g_kn, m_dil_w_o, v_ffn1_norm, v_ffn1_w_in, v_ffn1_w_out, v_mix_norm, v_ffn2_norm, v_ffn2_w_in, v_ffn2_w_out, v_mla_w_down, v_mla_g_cq, v_mla_g_ckv, v_mla_w_uq, v_mla_w_ukv, v_mla_g_qn, v_mla_g_kn, v_mla_w_o, v_dil_w_qkv, v_dil_g_qn, v_dil_g_kn, v_dil_w_o):
    args = dict(locals())
    w = {n: args[n] for n in WEIGHTS}
    m = {n: args["m_" + n] for n in WEIGHTS}
    v = {n: args["v_" + n] for n in WEIGHTS}
    cx, cy, cc = _me()
    core = jnp.reshape(cc, (1,)).astype(jnp.int32)
    shard = jnp.reshape(2 * cx + cy, (1,)).astype(jnp.int32)
    place = (shard, core)
    pe_pad = LANES - ROPE_DIM

    order = [(n, layer if w[n].shape[0] > 1 else 0) for layer in range(2) for n in USE_ORDER[layer]]
    lands, started = {}, {}

    def cast(key, after):
        _, r, c = w[key[0]].shape
        lands[key] = cast_into_shards(w[key[0]], key[1], shard, after, "ag_%s%d_cast" % key).reshape(N_CHIPS, 1, 2, r // 2, c)
        return lands[key]

    def start(key, after):
        started[key] = exchange_start((lands.pop(key),), _ag_mine, _ag_mine, after, "ag_%s%d_start" % key)
        return started[key][2]

    def get_w(n, l, after):
        k = order.index((n, l))
        behind = after
        if k == 0:
            for key in order[:2]:
                cast(key, after)
            for key in order[:2]:
                behind = start(key, after)
            for key in order[2:]:
                behind = cast(key, behind)
        else:
            for key in order[len(started):min(k + AG_AHEAD, len(order) - 1) + 1]:
                behind = start(key, after)
        sems, bufs, _ = started[n, l]
        _, _, _, h, c = bufs[0].shape
        (land,) = exchange_wait(sems, bufs, _ag_mine, _ag_got, behind, f"ag_{n}{l}_wait")
        full = all_gather_finish(land, f"ag_{n}{l}_finish").reshape(N_CHIPS, 1, 2 * h, c)
        if n == "mla_w_down":
            full = jnp.pad(full, ((0, 0), (0, 0), (0, 0), (0, pe_pad)))
        if n == "mla_w_uq":
            full = _pad_heads(full, QK_DIM, QK_PAD)
        return full

    pending = []
    outs = {n: None for n in BIG}
    last_token = [None]

    def finish_oldest(after):
        n, l, pend = pending.pop(0)
        g = reduce_scatter_finish(pend, place, after, f"rs_{n}{l}")
        outs[n] = adamw(w[n], g, m[n], v[n], l, outs[n], f"adamw_{n}{l}")

    def on_grad(n, l, g4):
        if n == "mla_w_down":
            g4 = g4[..., :g4.shape[-1] - pe_pad]
        if n == "mla_w_uq":
            g4 = _unpad_heads(g4, QK_DIM, QK_PAD)
        after = last_token[0] if last_token[0] is not None else g4
        pending.append((n, l, reduce_scatter_start(g4, core, after, f"rs_{n}{l}")))
        last_token[0] = pending[-1][2][2]
        if len(pending) > RS_WINDOW:
            finish_oldest(last_token[0])

    loss, grad_x, gsmall = local_step(x[0], loss_target[0], {n: w[n] for n in SMALL}, get_w, on_grad)
    loss = lax.psum(loss, ("x", "y", "c"))
    while pending:
        finish_oldest(grad_x)
    small = _unpack_small(all_reduce_small(_pack_small(gsmall), "ar_small"), gsmall)
    for n in SMALL:
        outs[n] = tuple(o[0] for o in adamw(w[n][None], small[n], m[n][None], v[n][None], 0, None, f"adamw_{n}"))

    return (loss, grad_x[None], *[outs[n][0] for n in WEIGHTS], *[outs[n][1] for n in WEIGHTS],
            *[outs[n][2] for n in WEIGHTS], *[outs[n][3] for n in WEIGHTS])
```

```python
import math

import numpy as np
import jax
import jax.numpy as jnp
from jax import lax
from jax.experimental import pallas as pl
from jax.experimental.pallas import tpu as pltpu

MXU_DTYPE = jnp.bfloat16
WIRE_DTYPE = jnp.bfloat16
EPS = 1e-6
NEG = -1e30
N_CHIPS = 4
MESH = pl.DeviceIdType.MESH
ANY = pl.BlockSpec(memory_space=pl.ANY)
LANES = 128

MLA_HEADS = 16
NOPE_DIM = 128
ROPE_DIM = 64
QK_DIM = NOPE_DIM + ROPE_DIM
QK_PAD = 2 * LANES
PREP_ROWS = 1024
ATTN_BLOCK = 1024
AG_AHEAD = 3
RS_WINDOW = 2
ROPE_THETA = 10000.0
DIL_PAIRS = ((128, 1), (512, 4), (2048, 16))
DIL_HEADS = 8
BLK = 128

ADAM_LR = 0.001
ADAM_B1 = 0.9
ADAM_B2 = 0.999
ADAM_EPS = 1e-08
ADAM_WD = 0.01
ADAM_STEP = 10

NT = (((1,), (1,)), ((), ()))
TN = (((0,), (0,)), ((), ()))

SHARD_AXIS = {"ffn1_w_in": 1, "ffn1_w_out": 0, "ffn2_w_in": 1, "ffn2_w_out": 0, "mla_w_down": 0, "mla_w_uq": 1,
              "mla_w_ukv": 1, "mla_w_o": 0, "dil_w_qkv": 1, "dil_w_o": 1}
BIG = tuple(SHARD_AXIS)
USE_ORDER = (("ffn1_w_in", "ffn1_w_out", "mla_w_down", "mla_w_uq", "mla_w_ukv", "mla_w_o", "ffn2_w_in", "ffn2_w_out"),
             ("ffn1_w_in", "ffn1_w_out", "dil_w_qkv", "dil_w_o", "ffn2_w_in", "ffn2_w_out"))
SMALL = ("ffn1_norm", "mix_norm", "ffn2_norm", "mla_g_cq", "mla_g_ckv", "mla_g_qn", "mla_g_kn", "dil_g_qn", "dil_g_kn")
WEIGHTS = ("ffn1_norm", "ffn1_w_in", "ffn1_w_out", "mix_norm", "ffn2_norm", "ffn2_w_in", "ffn2_w_out", "mla_w_down",
           "mla_g_cq", "mla_g_ckv", "mla_w_uq", "mla_w_ukv", "mla_g_qn", "mla_g_kn", "mla_w_o", "dil_w_qkv", "dil_g_qn",
           "dil_g_kn", "dil_w_o")


def _tile(dim, pref, mult=LANES):
    if dim <= pref:
        return dim
    t = (pref // mult) * mult
    while t >= mult:
        if dim % t == 0:
            return t
        t -= mult
    return dim


def _params(*sem):
    return pltpu.CompilerParams(dimension_semantics=sem)


def _f32(shape):
    return jax.ShapeDtypeStruct(shape, jnp.float32)


def _act(shape):
    return jax.ShapeDtypeStruct(shape, MXU_DTYPE)


class Sharded:
    def __init__(self, arr, layer, axis):
        self.arr, self.layer, self.axis = arr, layer, axis
        n, _, r, c = arr.shape
        self.shape = (n * r, c) if axis == 0 else (r, n * c)
        self.per = r if axis == 0 else c

    def spec(self, tr, tc, rc_of):
        l = self.layer
        if self.axis == 0:
            n = self.per // tr

            def imap(*g):
                bi, bj = rc_of(*g)
                return (bi // n, l, bi % n, bj)
        else:
            n = self.per // tc

            def imap(*g):
                bi, bj = rc_of(*g)
                return (bj // n, l, bi, bj % n)
        return pl.BlockSpec((None, None, tr, tc), imap)


def _spec2(tr, tc, rc_of):
    return pl.BlockSpec((tr, tc), lambda *g: rc_of(*g))


def matmul(a, b, *, ta=False, tb=False, out_dtype=jnp.float32, scale=None, res=None, out_axis=None,
           name, tm=1024, tn=1024, tk=2048):
    am, ak = (a.shape[1], a.shape[0]) if ta else a.shape
    bk, bn = (b.shape[1], b.shape[0]) if tb else b.shape
    assert ak == bk, (name, a.shape, b.shape, ta, tb)
    M, N, K = am, bn, ak

    def per(x, axis):
        return x.per if isinstance(x, Sharded) and x.axis == axis else None

    def pick(dim, pref, *pers):
        return _tile(math.gcd(dim, *[p for p in pers if p is not None]), pref)

    tm = pick(M, tm, per(a, 1 if ta else 0), M // N_CHIPS if out_axis == 0 else None)
    tn = pick(N, tn, per(b, 0 if tb else 1), N // N_CHIPS if out_axis == 1 else None)
    tk = pick(K, tk, per(a, 0 if ta else 1), per(b, 1 if tb else 0))
    assert M % tm == 0 and N % tn == 0 and K % tk == 0, (name, M, N, K, tm, tn, tk)
    nk = K // tk

    a_rc = (lambda i, j, k: (k, i)) if ta else (lambda i, j, k: (i, k))
    b_rc = (lambda i, j, k: (j, k)) if tb else (lambda i, j, k: (k, j))
    a_blk = (tk, tm) if ta else (tm, tk)
    b_blk = (tn, tk) if tb else (tk, tn)
    a_spec = a.spec(*a_blk, a_rc) if isinstance(a, Sharded) else _spec2(*a_blk, a_rc)
    b_spec = b.spec(*b_blk, b_rc) if isinstance(b, Sharded) else _spec2(*b_blk, b_rc)
    dn = (((0 if ta else 1,), (1 if tb else 0,)), ((), ()))
    has_res = res is not None

    def body(*refs):
        if has_res:
            a_ref, b_ref, r_ref, o_ref, acc = refs
        else:
            a_ref, b_ref, o_ref, acc = refs
        k = pl.program_id(2)

        @pl.when(k == 0)
        def _():
            acc[...] = jnp.zeros_like(acc)

        acc[...] += lax.dot_general(a_ref[...].astype(MXU_DTYPE), b_ref[...].astype(MXU_DTYPE), dn,
                                    preferred_element_type=jnp.float32)

        @pl.when(k == nk - 1)
        def _():
            r = acc[...]
            if scale is not None:
                r = r * scale
            if has_res:
                r = r + r_ref[...]
            o_ref[...] = r.astype(o_ref.dtype)

    in_specs = [a_spec, b_spec]
    args = [a.arr if isinstance(a, Sharded) else a, b.arr if isinstance(b, Sharded) else b]
    if has_res:
        in_specs.append(_spec2(tm, tn, lambda i, j, k: (i, j)))
        args.append(res)
    o_rc = lambda i, j, k: (i, j)
    if out_axis is None:
        out_shape = jax.ShapeDtypeStruct((M, N), out_dtype)
        out_spec = _spec2(tm, tn, o_rc)
    else:
        shp = (N_CHIPS, 1, M // N_CHIPS, N) if out_axis == 0 else (N_CHIPS, 1, M, N // N_CHIPS)
        out_shape = jax.ShapeDtypeStruct(shp, out_dtype)
        out_spec = Sharded(out_shape, 0, out_axis).spec(tm, tn, o_rc)
    return pl.pallas_call(
        body, name=name, out_shape=out_shape, grid=(M // tm, N // tn, nk),
        in_specs=in_specs, out_specs=out_spec,
        scratch_shapes=[pltpu.VMEM((tm, tn), jnp.float32)],
        compiler_params=_params("parallel", "parallel", "arbitrary"),
    )(*args)


def _me():
    return lax.axis_index("x"), lax.axis_index("y"), lax.axis_index("c")


def _other_chips(x, y):
    return [(1 - x, y), (x, 1 - y), (1 - x, 1 - y)]


HBM = pl.BlockSpec(memory_space=pltpu.HBM)
SEM = pl.BlockSpec(memory_space=pltpu.SEMAPHORE)
N_PEERS = 3
TOKEN = jax.ShapeDtypeStruct((8, LANES), jnp.float32)


def _split_params():
    return pltpu.CompilerParams(has_side_effects=pltpu.SideEffectType.DATAFLOW_SIDE_EFFECTING)


def _in_hbm(a):
    return pltpu.with_memory_space_constraint(a, pltpu.HBM)


def exchange_start(bufs, plan, after, name):
    nb, n = len(bufs), plan.copies

    def body(*refs):
        sems, token = refs[nb + 1:nb + 1 + 2 * n], refs[-1]
        for j, (src, dst, to) in enumerate(plan(refs[:nb], _me(), False)):
            pltpu.make_async_remote_copy(src_ref=src, dst_ref=dst, send_sem=sems[j], recv_sem=sems[n + j],
                                         device_id=to, device_id_type=MESH).start()
        token[...] = jnp.zeros_like(token)

    outs = pl.pallas_call(
        body, name=name,
        out_shape=(pltpu.SemaphoreType.DMA(()),) * (2 * n) + tuple(pltpu.HBM(b.shape, b.dtype) for b in bufs) + (TOKEN,),
        in_specs=(HBM,) * nb + (ANY,), out_specs=(SEM,) * (2 * n) + (HBM,) * nb + (pl.BlockSpec(memory_space=pltpu.VMEM),),
        input_output_aliases={k: 2 * n + k for k in range(nb)}, compiler_params=_split_params(),
    )(*[_in_hbm(b) for b in bufs], after)
    return outs[:2 * n], outs[2 * n:2 * n + nb], outs[-1]


def exchange_wait(started, plan, after, name):
    sems, bufs, _ = started
    nb, n = len(bufs), plan.copies

    def body(*refs):
        sems_ = refs[nb:nb + 2 * n]
        for j, (src, got, to) in enumerate(plan(refs[:nb], _me(), True)):
            cp = pltpu.make_async_remote_copy(src_ref=src, dst_ref=got, send_sem=sems_[j], recv_sem=sems_[n + j],
                                              device_id=to, device_id_type=MESH)
            cp.wait_send()
            cp.wait_recv()

    return pl.pallas_call(
        body, name=name, out_shape=tuple(pltpu.HBM(b.shape, b.dtype) for b in bufs),
        in_specs=(HBM,) * nb + (SEM,) * (2 * n) + (ANY,), out_specs=(HBM,) * nb,
        input_output_aliases={k: k for k in range(nb)}, compiler_params=_split_params(),
    )(*bufs, *sems, after)


def _plan(copies):
    def mark(f):
        f.copies = copies
        return f
    return mark


@_plan(N_PEERS)
def ag_over_ici(refs, me, arrived):
    (land,), (x, y, cc) = refs, me
    mine = land.at[2 * x + y, :, cc]
    return [(mine, land.at[2 * px + py, :, cc] if arrived else mine, (px, py, cc)) for px, py in _other_chips(x, y)]


@_plan(N_PEERS)
def ag_to_sibling(refs, me, arrived):
    (land,), (x, y, cc) = refs, me
    return [(land.at[2 * px + py, :, cc], land.at[2 * px + py, :, (1 - cc) if arrived else cc], (x, y, 1 - cc))
            for px, py in _other_chips(x, y)]


@_plan(1)
def rs_to_sibling(refs, me, arrived):
    (g, r1), (x, y, cc) = refs, me
    return [(g.at[:, :, 1 - cc], r1, (x, y, 1 - cc))]


@_plan(N_PEERS)
def rs_over_ici(refs, me, arrived):
    (p, land), (x, y, cc) = refs, me
    return [(p.at[2 * px + py], land.at[j], (px, py, cc)) for j, (px, py) in enumerate(_other_chips(x, y))]


@_plan(1)
def rs_gather_sibling(refs, me, arrived):
    (red,), (x, y, cc) = refs, me
    return [(red.at[:, cc], red.at[:, (1 - cc) if arrived else cc], (x, y, 1 - cc))]


def cast_into_shards(w, layer, shard, after, name):
    L, r, c = w.shape
    tr, tc = _tile(r, 1024, 16), _tile(c, 2048)

    def body(shard_ref, w_ref, after_ref, o_ref):
        o_ref[...] = w_ref[...].astype(o_ref.dtype)

    grid_spec = pltpu.PrefetchScalarGridSpec(
        num_scalar_prefetch=1, grid=(r // tr, c // tc),
        in_specs=[pl.BlockSpec((None, tr, tc), lambda i, j, sh: (layer, i, j)), ANY],
        out_specs=pl.BlockSpec((None, None, tr, tc), lambda i, j, sh: (sh[0], 0, i, j)))
    return pl.pallas_call(body, name=name, grid_spec=grid_spec, out_shape=jax.ShapeDtypeStruct((N_CHIPS, 1, r, c), WIRE_DTYPE),
                          compiler_params=_params("parallel", "parallel"))(shard, w, after)


def add_sibling(g, r1, core, name):
    n, L, two, h, c = g.shape
    th = _tile(h, 1024, 16)
    tc = _tile(c, 2048)

    def body(core_ref, g_ref, r_ref, o_ref):
        o_ref[...] = (g_ref[...].astype(jnp.float32) + r_ref[...].astype(jnp.float32)).astype(o_ref.dtype)

    grid_spec = pltpu.PrefetchScalarGridSpec(
        num_scalar_prefetch=1, grid=(n, L, h // th, c // tc),
        in_specs=[pl.BlockSpec((None, None, None, th, tc), lambda s, l, i, j, core: (s, l, core[0], i, j)),
                  pl.BlockSpec((None, None, th, tc), lambda s, l, i, j, core: (s, l, i, j))],
        out_specs=pl.BlockSpec((None, None, th, tc), lambda s, l, i, j, core: (s, l, i, j)))
    return pl.pallas_call(body, name=name, grid_spec=grid_spec, out_shape=jax.ShapeDtypeStruct((n, L, h, c), WIRE_DTYPE),
                          compiler_params=_params("parallel", "parallel", "parallel", "parallel"))(core, g, r1)


def add_chips(p, r2, place, name):
    n, L, h, c = p.shape
    th = _tile(h, 1024, 16)
    tc = _tile(c, 2048)

    def body(shard_ref, core_ref, p_ref, r_ref, o_ref):
        acc = p_ref[...].astype(jnp.float32)
        for j in range(3):
            acc = acc + r_ref[j].astype(jnp.float32)
        o_ref[...] = acc

    grid_spec = pltpu.PrefetchScalarGridSpec(
        num_scalar_prefetch=2, grid=(L, h // th, c // tc),
        in_specs=[pl.BlockSpec((None, None, th, tc), lambda l, i, j, shard, core: (shard[0], l, i, j)),
                  pl.BlockSpec((3, None, th, tc), lambda l, i, j, shard, core: (0, l, i, j))],
        out_specs=pl.BlockSpec((None, None, th, tc), lambda l, i, j, shard, core: (l, core[0], i, j)))
    return pl.pallas_call(body, name=name, grid_spec=grid_spec, out_shape=_f32((L, 2, h, c)),
                          compiler_params=_params("parallel", "parallel", "parallel"))(*place, p, r2)


class ReduceScatter:
    def __init__(self, place, done):
        self.place, self.done = place, done
        self.stages = [[], [], []]
        self.token = None

    def _start(self, bufs, plan, name):
        started = exchange_start(bufs, plan, self.token if self.token is not None else bufs[0], name)
        self.token = started[2]
        return started

    def _advance(self, stage, after):
        key, started = self.stages[stage].pop(0)
        name = "rs_%s%d" % key
        if stage == 0:
            g, r1 = exchange_wait(started, rs_to_sibling, after, name + "_d2d_wait")
            p = add_sibling(g, r1, self.place[1], name + "_add1")
            land = lax.empty((N_PEERS,) + p.shape[1:], p.dtype)
            self.stages[1].append((key, self._start((p, land), rs_over_ici, name + "_ici_start")))
        elif stage == 1:
            p, r2 = exchange_wait(started, rs_over_ici, after, name + "_ici_wait")
            red = add_chips(p, r2, self.place, name + "_add2")
            self.stages[2].append((key, self._start((red,), rs_gather_sibling, name + "_gather_start")))
        else:
            (red,) = exchange_wait(started, rs_gather_sibling, after, name + "_gather_wait")
            L, two, h, c = red.shape
            self.done(key, red.reshape(2 * h, c))

    def push(self, key, g4):
        n, L, r, c = g4.shape
        g = g4.reshape(n, L, 2, r // 2, c)
        r1 = lax.empty((n, L, r // 2, c), g.dtype)
        self.stages[0].append((key, self._start((g, r1), rs_to_sibling, "rs_%s%d_d2d_start" % key)))
        for stage, depth in ((2, 1), (1, RS_WINDOW), (0, 1)):
            if len(self.stages[stage]) > depth:
                self._advance(stage, self.token)

    def drain(self, after):
        for stage in (0, 1, 2):
            while self.stages[stage]:
                self._advance(stage, after)


def all_reduce_small(v, name):
    R, C = v.shape

    def body(v_ref, o_ref, buf, send_sems, recv_sems):
        x, y, cc = _me()
        buf[0] = v_ref[...]
        cps = []
        for k in range(1, 8):
            dx, dy, dc = (k >> 2) & 1, (k >> 1) & 1, k & 1
            to = (x ^ dx, y ^ dy, cc ^ dc)
            cp = pltpu.make_async_remote_copy(src_ref=v_ref, dst_ref=buf.at[k], send_sem=send_sems.at[k],
                                              recv_sem=recv_sems.at[k], device_id=to, device_id_type=MESH)
            cp.start()
            cps.append(cp)
        for cp in cps:
            cp.wait()
        me = 4 * x + 2 * y + cc
        acc = buf[me]
        for a in range(1, 8):
            acc = acc + buf[a ^ me]
        o_ref[...] = acc

    vm = pl.BlockSpec(memory_space=pltpu.VMEM)
    return pl.pallas_call(
        body, name=name, out_shape=_f32((R, C)), in_specs=[vm], out_specs=vm,
        scratch_shapes=[pltpu.VMEM((8, R, C), jnp.float32), pltpu.SemaphoreType.DMA((8,)), pltpu.SemaphoreType.DMA((8,))],
    )(v)


def _rstd(x, n):
    return lax.rsqrt(jnp.sum(x * x, axis=-1, keepdims=True) * (1.0 / n) + EPS)


def _accumulate(ref, part, first):
    @pl.when(first)
    def _():
        ref[...] = part

    @pl.when(jnp.logical_not(first))
    def _():
        ref[...] += part


def rmsnorm_fwd(x, g, name):
    S, D = x.shape
    ts = _tile(S, 256, 8)

    def body(x_ref, g_ref, o_ref):
        xv = x_ref[...]
        o_ref[...] = (xv * _rstd(xv, D) * g_ref[...]).astype(o_ref.dtype)

    return pl.pallas_call(
        body, name=name, out_shape=_act((S, D)), grid=(S // ts,),
        in_specs=[pl.BlockSpec((ts, D), lambda i: (i, 0)), pl.BlockSpec((1, D), lambda i: (0, 0))],
        out_specs=pl.BlockSpec((ts, D), lambda i: (i, 0)), compiler_params=_params("parallel"))(x, g)


def _norm_bwd(x, g, dy, n):
    r = _rstd(x, n)
    xh = x * r
    dxh = dy * g
    dx = r * (dxh - xh * (jnp.sum(dxh * xh, axis=-1, keepdims=True) * (1.0 / n)))
    return dx, dy * xh


def rmsnorm_bwd(x, g, dy, dres, name):
    S, D = x.shape
    ts = _tile(S, 256, 8)

    def body(x_ref, g_ref, dy_ref, dres_ref, dx_ref, dg_ref):
        dx, dgp = _norm_bwd(x_ref[...], g_ref[...], dy_ref[...], D)
        dx_ref[...] = dres_ref[...] + dx
        _accumulate(dg_ref, jnp.sum(dgp, axis=0, keepdims=True), pl.program_id(0) == 0)

    row = pl.BlockSpec((ts, D), lambda i: (i, 0))
    vec = pl.BlockSpec((1, D), lambda i: (0, 0))
    return pl.pallas_call(
        body, name=name, out_shape=(_f32((S, D)), _f32((1, D))), grid=(S // ts,),
        in_specs=[row, vec, row, row], out_specs=(row, vec), compiler_params=_params("arbitrary"))(x, g, dy, dres)


def _sigmoid(x):
    return 1.0 / (1.0 + jnp.exp(-x))


def swiglu_in(xn, w_in, name, tm=512, tn=1408):
    S, D = xn.shape
    F = w_in.shape[1] // 2
    tm, tn = _tile(S, tm, 8), _tile(math.gcd(F, w_in.per), tn)
    nf = F // tn

    def body(x_ref, wg_ref, wu_ref, u_ref, a_ref):
        x = x_ref[...].astype(MXU_DTYPE)
        gt = jnp.dot(x, wg_ref[...], preferred_element_type=jnp.float32)
        up = jnp.dot(x, wu_ref[...], preferred_element_type=jnp.float32)
        u_ref[0] = gt.astype(u_ref.dtype)
        u_ref[1] = up.astype(u_ref.dtype)
        a_ref[...] = (gt * _sigmoid(gt) * up).astype(a_ref.dtype)

    return pl.pallas_call(
        body, name=name, out_shape=(_act((2, 1, S, F)), _act((S, F))), grid=(nf, S // tm),
        in_specs=[pl.BlockSpec((tm, D), lambda j, i: (i, 0)), w_in.spec(D, tn, lambda j, i: (0, j)),
                  w_in.spec(D, tn, lambda j, i: (0, j + nf))],
        out_specs=(pl.BlockSpec((2, None, tm, tn), lambda j, i: (0, 0, i, j)), pl.BlockSpec((tm, tn), lambda j, i: (i, j))),
        compiler_params=_params("parallel", "parallel"))(xn, w_in.arr, w_in.arr)


def swiglu_out_bwd(dout, w_out, u, scale, name, tm=512, tn=1408):
    S, D = dout.shape
    F = w_out.shape[0]
    tm, tn = _tile(S, tm, 8), _tile(math.gcd(F, w_out.per), tn)

    def body(d_ref, w_ref, u_ref, o_ref):
        da = lax.dot_general(d_ref[...].astype(MXU_DTYPE), w_ref[...], NT, preferred_element_type=jnp.float32) * scale
        gt, up = u_ref[0].astype(jnp.float32), u_ref[1].astype(jnp.float32)
        s = _sigmoid(gt)
        o_ref[0] = (da * up * (s * (1.0 + gt * (1.0 - s)))).astype(o_ref.dtype)
        o_ref[1] = (da * (gt * s)).astype(o_ref.dtype)

    planes = pl.BlockSpec((2, None, tm, tn), lambda j, i: (0, 0, i, j))
    return pl.pallas_call(
        body, name=name, out_shape=_act((2, 1, S, F)), grid=(F // tn, S // tm),
        in_specs=[pl.BlockSpec((tm, D), lambda j, i: (i, 0)), w_out.spec(tn, D, lambda j, i: (j, 0)), planes],
        out_specs=planes, compiler_params=_params("parallel", "parallel"))(dout, w_out.arr, u)


def loss_head(y, t, name):
    S, D = y.shape
    ts = _tile(S, 256, 8)

    def body(y_ref, t_ref, dy_ref, l_ref):
        e = y_ref[...] - t_ref[...]
        dy_ref[...] = e * (1.0 / D)
        l_ref[...] = jnp.full(l_ref.shape, 0.5 * jnp.sum(jnp.sum(e * e, axis=-1, keepdims=True) * (1.0 / D)), jnp.float32)

    row = pl.BlockSpec((ts, D), lambda i: (i, 0))
    dy, parts = pl.pallas_call(
        body, name=name, out_shape=(_f32((S, D)), _f32((S // ts, 8, LANES))), grid=(S // ts,),
        in_specs=[row, row], out_specs=(row, pl.BlockSpec((None, 8, LANES), lambda i: (i, 0, 0))),
        compiler_params=_params("parallel"))(y, t)
    return jnp.sum(parts[:, 0, 0]), dy


def rope_tables(S):
    inv = 1.0 / (ROPE_THETA ** (jnp.arange(0, ROPE_DIM, 2, dtype=jnp.float32) / ROPE_DIM))
    ang = jnp.arange(S, dtype=jnp.float32)[:, None] * inv[None, :]
    c, s = jnp.cos(ang), jnp.sin(ang)
    z = jnp.zeros_like(c)
    return (jnp.concatenate([c, c, z, z], axis=1), jnp.concatenate([-s, z, z, z], axis=1),
            jnp.concatenate([z, s, z, z], axis=1))


def _rope(x, cos, sa, sb):
    return x * cos + pltpu.roll(x, 96, 1) * sa + pltpu.roll(x, 32, 1) * sb


def _rope_t(d, cos, sa, sb):
    return d * cos + pltpu.roll(d * sa, 32, 1) + pltpu.roll(d * sb, 96, 1)


def _head_norm(x1, x2, g):
    r = lax.rsqrt((jnp.sum(x1 * x1, axis=-1, keepdims=True) + jnp.sum(x2 * x2, axis=-1, keepdims=True)) * (1.0 / QK_DIM) + EPS)
    return x1 * r * g[:, :LANES], x2 * r * g[:, LANES:], r


def _head_norm_bwd(x1, x2, g, d1, d2):
    _, _, r = _head_norm(x1, x2, g)
    h1, h2 = x1 * r, x2 * r
    e1, e2 = d1 * g[:, :LANES], d2 * g[:, LANES:]
    m = (jnp.sum(e1 * h1, axis=-1, keepdims=True) + jnp.sum(e2 * h2, axis=-1, keepdims=True)) * (1.0 / QK_DIM)
    return r * (e1 - h1 * m), r * (e2 - h2 * m), d1 * h1, d2 * h2


def mla_latent_fwd(lat, g_cq, g_ckv, name):
    S, W = lat.shape
    QL, KL = g_cq.shape[1], g_ckv.shape[1]
    ts = _tile(S, 256, 8)

    def body(l_ref, gq_ref, gk_ref, cq_ref, ckv_ref):
        a, b = l_ref[:, :QL], l_ref[:, QL:QL + KL]
        cq_ref[...] = (a * _rstd(a, QL) * gq_ref[...]).astype(cq_ref.dtype)
        ckv_ref[...] = (b * _rstd(b, KL) * gk_ref[...]).astype(ckv_ref.dtype)

    return pl.pallas_call(
        body, name=name, out_shape=(_act((S, QL)), _act((S, KL))), grid=(S // ts,),
        in_specs=[pl.BlockSpec((ts, W), lambda i: (i, 0)), pl.BlockSpec((1, QL), lambda i: (0, 0)),
                  pl.BlockSpec((1, KL), lambda i: (0, 0))],
        out_specs=(pl.BlockSpec((ts, QL), lambda i: (i, 0)), pl.BlockSpec((ts, KL), lambda i: (i, 0))),
        compiler_params=_params("parallel"))(lat, g_cq, g_ckv)


def mla_latent_bwd(dcq, dckv, dkpe, lat, g_cq, g_ckv, name):
    S, W = lat.shape
    QL, KL = g_cq.shape[1], g_ckv.shape[1]
    ts = _tile(S, 256, 8)

    def body(dq_ref, dk_ref, dp_ref, l_ref, gq_ref, gk_ref, o_ref, dgq_ref, dgk_ref):
        first = pl.program_id(0) == 0
        da, ga = _norm_bwd(l_ref[:, :QL], gq_ref[...], dq_ref[...], QL)
        db, gb = _norm_bwd(l_ref[:, QL:QL + KL], gk_ref[...], dk_ref[...], KL)
        o_ref[:, :QL] = da.astype(o_ref.dtype)
        o_ref[:, QL:QL + KL] = db.astype(o_ref.dtype)
        o_ref[:, QL + KL:] = dp_ref[...].astype(o_ref.dtype)
        _accumulate(dgq_ref, jnp.sum(ga, axis=0, keepdims=True), first)
        _accumulate(dgk_ref, jnp.sum(gb, axis=0, keepdims=True), first)

    row = lambda n: pl.BlockSpec((ts, n), lambda i: (i, 0))
    vec = lambda n: pl.BlockSpec((1, n), lambda i: (0, 0))
    return pl.pallas_call(
        body, name=name, out_shape=(_act((S, W)), _f32((1, QL)), _f32((1, KL))), grid=(S // ts,),
        in_specs=[row(QL), row(KL), row(LANES), row(W), vec(QL), vec(KL)], out_specs=(row(W), vec(QL), vec(KL)),
        compiler_params=_params("arbitrary"))(dcq, dckv, dkpe, lat, g_cq, g_ckv)


def mla_q_prep_fwd(qraw, g, tabs, H, name):
    S = qraw.shape[0]
    ts = _tile(S, PREP_ROWS, 8)

    def body(x_ref, g_ref, c_ref, a_ref, b_ref, o_ref):
        y1, y2, _ = _head_norm(x_ref[:, :LANES], x_ref[:, LANES:], g_ref[...])
        o_ref[:, :LANES] = y1.astype(o_ref.dtype)
        o_ref[:, LANES:] = _rope(y2, c_ref[...], a_ref[...], b_ref[...]).astype(o_ref.dtype)

    tab = pl.BlockSpec((ts, LANES), lambda i, h: (i, 0))
    return pl.pallas_call(
        body, name=name, out_shape=_act((H, S, QK_PAD)), grid=(S // ts, H),
        in_specs=[pl.BlockSpec((ts, QK_PAD), lambda i, h: (i, h)), pl.BlockSpec((1, QK_PAD), lambda i, h: (0, 0)), tab, tab, tab],
        out_specs=pl.BlockSpec((None, ts, QK_PAD), lambda i, h: (h, i, 0)),
        compiler_params=_params("parallel", "parallel"))(qraw, g, *tabs)


def mla_q_prep_bwd(dq, qraw, g, tabs, H, name):
    S = qraw.shape[0]
    ts = _tile(S, PREP_ROWS, 8)

    def body(d_ref, x_ref, g_ref, c_ref, a_ref, b_ref, o_ref, dg_ref):
        d2 = _rope_t(d_ref[:, LANES:], c_ref[...], a_ref[...], b_ref[...])
        dx1, dx2, g1, g2 = _head_norm_bwd(x_ref[:, :LANES], x_ref[:, LANES:], g_ref[...], d_ref[:, :LANES], d2)
        o_ref[:, :LANES] = dx1.astype(o_ref.dtype)
        o_ref[:, LANES:] = dx2.astype(o_ref.dtype)
        first = jnp.logical_and(pl.program_id(0) == 0, pl.program_id(1) == 0)
        part = jnp.concatenate([jnp.sum(g1, axis=0, keepdims=True), jnp.sum(g2, axis=0, keepdims=True)], axis=1)
        _accumulate(dg_ref, part, first)

    tab = pl.BlockSpec((ts, LANES), lambda i, h: (i, 0))
    vec = pl.BlockSpec((1, QK_PAD), lambda i, h: (0, 0))
    return pl.pallas_call(
        body, name=name, out_shape=(_act((S, H * QK_PAD)), _f32((1, QK_PAD))), grid=(S // ts, H),
        in_specs=[pl.BlockSpec((None, ts, QK_PAD), lambda i, h: (h, i, 0)), pl.BlockSpec((ts, QK_PAD), lambda i, h: (i, h)),
                  vec, tab, tab, tab],
        out_specs=(pl.BlockSpec((ts, QK_PAD), lambda i, h: (i, h)), vec),
        compiler_params=_params("arbitrary", "arbitrary"))(dq, qraw, g, *tabs)


def mla_k_prep_fwd(kvraw, lat, g, tabs, H, pe_blk, name):
    S = kvraw.shape[0]
    ts = _tile(S, PREP_ROWS, 8)

    def body(x_ref, p_ref, g_ref, c_ref, a_ref, b_ref, k_ref, v_ref):
        y1, y2, _ = _head_norm(x_ref[:, :LANES], p_ref[...], g_ref[...])
        k_ref[:, :LANES] = y1.astype(k_ref.dtype)
        k_ref[:, LANES:] = _rope(y2, c_ref[...], a_ref[...], b_ref[...]).astype(k_ref.dtype)
        v_ref[...] = x_ref[:, LANES:].astype(v_ref.dtype)

    tab = pl.BlockSpec((ts, LANES), lambda i, h: (i, 0))
    return pl.pallas_call(
        body, name=name, out_shape=(_act((H, S, QK_PAD)), _act((H, S, LANES))), grid=(S // ts, H),
        in_specs=[pl.BlockSpec((ts, QK_PAD), lambda i, h: (i, h)), pl.BlockSpec((ts, LANES), lambda i, h: (i, pe_blk)),
                  pl.BlockSpec((1, QK_PAD), lambda i, h: (0, 0)), tab, tab, tab],
        out_specs=(pl.BlockSpec((None, ts, QK_PAD), lambda i, h: (h, i, 0)), pl.BlockSpec((None, ts, LANES), lambda i, h: (h, i, 0))),
        compiler_params=_params("parallel", "parallel"))(kvraw, lat, g, *tabs)


def mla_k_prep_bwd(dk, dv, kvraw, lat, g, tabs, H, pe_blk, name):
    S = kvraw.shape[0]
    ts = _tile(S, PREP_ROWS, 8)

    def body(dk_ref, dv_ref, x_ref, p_ref, g_ref, c_ref, a_ref, b_ref, o_ref, dp_ref, dg_ref):
        i, h = pl.program_id(0), pl.program_id(1)
        d2 = _rope_t(dk_ref[:, LANES:], c_ref[...], a_ref[...], b_ref[...])
        dx1, dx2, g1, g2 = _head_norm_bwd(x_ref[:, :LANES], p_ref[...], g_ref[...], dk_ref[:, :LANES], d2)
        o_ref[:, :LANES] = dx1.astype(o_ref.dtype)
        o_ref[:, LANES:] = dv_ref[...].astype(o_ref.dtype)
        _accumulate(dp_ref, dx2, h == 0)
        part = jnp.concatenate([jnp.sum(g1, axis=0, keepdims=True), jnp.sum(g2, axis=0, keepdims=True)], axis=1)
        _accumulate(dg_ref, part, jnp.logical_and(i == 0, h == 0))

    tab = pl.BlockSpec((ts, LANES), lambda i, h: (i, 0))
    vec = pl.BlockSpec((1, QK_PAD), lambda i, h: (0, 0))
    return pl.pallas_call(
        body, name=name, out_shape=(_act((S, H * QK_PAD)), _f32((S, LANES)), _f32((1, QK_PAD))), grid=(S // ts, H),
        in_specs=[pl.BlockSpec((None, ts, QK_PAD), lambda i, h: (h, i, 0)), pl.BlockSpec((None, ts, LANES), lambda i, h: (h, i, 0)),
                  pl.BlockSpec((ts, QK_PAD), lambda i, h: (i, h)), pl.BlockSpec((ts, LANES), lambda i, h: (i, pe_blk)),
                  vec, tab, tab, tab],
        out_specs=(pl.BlockSpec((ts, QK_PAD), lambda i, h: (i, h)), tab, vec),
        compiler_params=_params("arbitrary", "arbitrary"))(dk, dv, kvraw, lat, g, *tabs)


def _causal_scores(q, k, scale, diagonal):
    s = lax.dot_general(q, k, NT, preferred_element_type=jnp.float32) * scale
    if not diagonal:
        return s
    row = lax.broadcasted_iota(jnp.int32, s.shape, 0)
    col = lax.broadcasted_iota(jnp.int32, s.shape, 1)
    return jnp.where(col <= row, s, NEG)


def _on_causal_blocks(qi, ki, step):
    @pl.when(ki < qi)
    def _():
        step(False)

    @pl.when(ki == qi)
    def _():
        step(True)


def mla_attention_fwd(q, k, v, name):
    H, S, _ = q.shape
    t = _tile(S, ATTN_BLOCK)
    n = S // t
    scale = 1.0 / math.sqrt(QK_DIM)

    def body(q_ref, k_ref, v_ref, o_ref, lse_ref, m_sc, l_sc, acc):
        qi, ki = pl.program_id(1), pl.program_id(2)

        @pl.when(ki == 0)
        def _():
            m_sc[...] = jnp.full(m_sc.shape, NEG, jnp.float32)
            l_sc[...] = jnp.zeros_like(l_sc)
            acc[...] = jnp.zeros_like(acc)

        def step(diagonal):
            s = _causal_scores(q_ref[...], k_ref[...], scale, diagonal)
            m_new = jnp.maximum(m_sc[...], jnp.max(s, axis=-1, keepdims=True))
            alpha = jnp.exp(m_sc[...] - m_new)
            p = jnp.exp(s - m_new)
            l_sc[...] = alpha * l_sc[...] + jnp.sum(p, axis=-1, keepdims=True)
            acc[...] = alpha * acc[...] + jnp.dot(p.astype(MXU_DTYPE), v_ref[...], preferred_element_type=jnp.float32)
            m_sc[...] = m_new

        _on_causal_blocks(qi, ki, step)

        @pl.when(ki == qi)
        def _():
            o_ref[...] = (acc[...] / l_sc[...]).astype(o_ref.dtype)
            lse_ref[...] = m_sc[...] + jnp.log(l_sc[...])

    kv = lambda w: pl.BlockSpec((None, t, w), lambda h, qi, ki: (h, jnp.minimum(ki, qi), 0))
    return pl.pallas_call(
        body, name=name, out_shape=(_act((S, H * LANES)), _f32((H, S, 1))), grid=(H, n, n),
        in_specs=[pl.BlockSpec((None, t, QK_PAD), lambda h, qi, ki: (h, qi, 0)), kv(QK_PAD), kv(LANES)],
        out_specs=(pl.BlockSpec((t, LANES), lambda h, qi, ki: (qi, h)), pl.BlockSpec((None, t, 1), lambda h, qi, ki: (h, qi, 0))),
        scratch_shapes=[pltpu.VMEM((t, 1), jnp.float32), pltpu.VMEM((t, 1), jnp.float32), pltpu.VMEM((t, LANES), jnp.float32)],
        compiler_params=_params("parallel", "parallel", "arbitrary"))(q, k, v)


def mla_attention_bwd(q, k, v, do, o, lse, name):
    H, S, _ = q.shape
    t = _tile(S, ATTN_BLOCK)
    n = S // t
    scale = 1.0 / math.sqrt(QK_DIM)

    def body(q_ref, k_ref, v_ref, do_ref, o_ref, lse_ref, dq_ref, dk_ref, dv_ref, dk_acc, dv_acc):
        ki, qi = pl.program_id(1), pl.program_id(2)
        rows = pl.ds(pl.multiple_of(qi * t, t), t)

        @pl.when(qi == 0)
        def _():
            dk_acc[...] = jnp.zeros_like(dk_acc)
            dv_acc[...] = jnp.zeros_like(dv_acc)

        @pl.when(ki == 0)
        def _():
            dq_ref[rows, :] = jnp.zeros((t, QK_PAD), jnp.float32)

        def step(diagonal):
            p = jnp.exp(_causal_scores(q_ref[...], k_ref[...], scale, diagonal) - lse_ref[...])
            dof = do_ref[...]
            dob = dof.astype(MXU_DTYPE)
            delta = jnp.sum(dof * o_ref[...].astype(jnp.float32), axis=-1, keepdims=True)
            dv_acc[...] += lax.dot_general(p.astype(MXU_DTYPE), dob, TN, preferred_element_type=jnp.float32)
            dp = lax.dot_general(dob, v_ref[...], NT, preferred_element_type=jnp.float32)
            ds = (p * (dp - delta)).astype(MXU_DTYPE)
            dk_acc[...] += lax.dot_general(ds, q_ref[...], TN, preferred_element_type=jnp.float32)
            dq_ref[rows, :] += jnp.dot(ds, k_ref[...], preferred_element_type=jnp.float32) * scale

        _on_causal_blocks(qi, ki, step)

        @pl.when(qi == n - 1)
        def _():
            dk_ref[...] = dk_acc[...] * scale
            dv_ref[...] = dv_acc[...]

    qrow = lambda h, ki, qi: (h, jnp.maximum(qi, ki), 0)
    kv = lambda w: pl.BlockSpec((None, t, w), lambda h, ki, qi: (h, ki, 0))
    col = pl.BlockSpec((None, t, 1), qrow)
    head = pl.BlockSpec((t, LANES), lambda h, ki, qi: (jnp.maximum(qi, ki), h))
    return pl.pallas_call(
        body, name=name, out_shape=(_f32((H, S, QK_PAD)), _f32((H, S, QK_PAD)), _f32((H, S, LANES))), grid=(H, n, n),
        in_specs=[pl.BlockSpec((None, t, QK_PAD), qrow), kv(QK_PAD), kv(LANES),
                  head, head, col],
        out_specs=(pl.BlockSpec((None, S, QK_PAD), lambda h, ki, qi: (h, 0, 0)), kv(QK_PAD), kv(LANES)),
        scratch_shapes=[pltpu.VMEM((t, QK_PAD), jnp.float32), pltpu.VMEM((t, LANES), jnp.float32)],
        compiler_params=_params("parallel", "arbitrary", "arbitrary"))(q, k, v, do, o, lse)


def _alibi_slopes(G, Hd):
    k = np.arange(1, G * Hd + 1, dtype=np.float32)
    s = (2.0 ** (-8.0 * k / (G * Hd))).astype(np.float32).reshape(G, Hd)
    return jnp.asarray(np.broadcast_to(s[:, :, None, None], (G, Hd, 1, LANES)).copy())


def _dil_scores(qn, kn, scale, slope_d, prev, valid):
    s = lax.dot_general(qn, kn, NT, preferred_element_type=jnp.float32) * scale
    iq = lax.broadcasted_iota(jnp.int32, s.shape, 0)
    ik = lax.broadcasted_iota(jnp.int32, s.shape, 1)
    dist = iq - ik + (BLK if prev else 0)
    ok = (ik >= iq) if prev else (ik <= iq)
    s = s - slope_d * dist.astype(jnp.float32)
    return jnp.where(jnp.logical_and(ok, valid), s, NEG)


def _dil_heads(d, Hd, block_bytes):
    hb = max(1, min(Hd, block_bytes // (BLK * LANES * 4))) if d == 1 else 1
    assert Hd % hb == 0, (Hd, hb)
    return hb


def _dil_rows(r, d):
    return pl.ds(r, BLK, stride=d) if d > 1 else slice(None)


def _loop_residues(d, residue, init):
    if d == 1:
        return residue(0, init)
    return lax.fori_loop(0, d // 2, lambda i, carry: residue(2 * i + 1, residue(2 * i, carry)), init)


def _dil_specs(d, nblk, Hd, G, g, hb):
    def spec(kind, shift):
        col0 = (kind * G + g) * Hd // hb
        return pl.BlockSpec((BLK * d, hb * LANES), lambda n, hg: (jnp.clip(n + shift, 0, nblk - 1), col0 + hg))
    return spec


def _head_spec(d, nblk, hb, shift):
    return pl.BlockSpec((BLK * d, hb * LANES), lambda n, hg: (jnp.clip(n + shift, 0, nblk - 1), hg))


def dilated_fwd(qkv, gq, gk, slopes, g, d, Hd, G, name):
    S, C = qkv.shape
    nblk = S // (BLK * d)
    hb = _dil_heads(d, Hd, 2 << 20)
    scale = 1.0 / math.sqrt(LANES)
    spec = _dil_specs(d, nblk, Hd, G, g, hb)

    def body(q_ref, kc_ref, kp_ref, vc_ref, vp_ref, gq_ref, gk_ref, sl_ref, o_ref, l_ref):
        n = pl.program_id(0)
        nrm = lambda t, gg: (t * _rstd(t, LANES) * gg).astype(MXU_DTYPE)

        def residue(r, carry):
            rows = _dil_rows(r, d)
            for hh in range(hb):
                cols = slice(hh * LANES, (hh + 1) * LANES)
                slope_d = sl_ref[hh][:, :1] * float(d)
                qn = nrm(q_ref[rows, cols], gq_ref[...])
                sc = _dil_scores(qn, nrm(kc_ref[rows, cols], gk_ref[...]), scale, slope_d, False, True)
                sp = _dil_scores(qn, nrm(kp_ref[rows, cols], gk_ref[...]), scale, slope_d, True, n > 0)
                m = jnp.maximum(jnp.max(sc, axis=-1, keepdims=True), jnp.max(sp, axis=-1, keepdims=True))
                lse = m + jnp.log(jnp.sum(jnp.exp(sc - m), axis=-1, keepdims=True) + jnp.sum(jnp.exp(sp - m), axis=-1, keepdims=True))
                o = jnp.dot(jnp.exp(sc - lse).astype(MXU_DTYPE), vc_ref[rows, cols].astype(MXU_DTYPE), preferred_element_type=jnp.float32)
                o = o + jnp.dot(jnp.exp(sp - lse).astype(MXU_DTYPE), vp_ref[rows, cols].astype(MXU_DTYPE), preferred_element_type=jnp.float32)
                o_ref[rows, cols] = o
                l_ref[rows, cols] = jnp.broadcast_to(lse, (BLK, LANES))
            return carry

        _loop_residues(d, residue, 0)

    vec = pl.BlockSpec((1, LANES), lambda n, hg: (0, 0))
    out = _head_spec(d, nblk, hb, 0)
    return pl.pallas_call(
        body, name=name, out_shape=(_f32((S, Hd * LANES)), _f32((S, Hd * LANES))), grid=(nblk, Hd // hb),
        in_specs=[spec(0, 0), spec(1, 0), spec(1, -1), spec(2, 0), spec(2, -1), vec, vec,
                  pl.BlockSpec((None, hb, 1, LANES), lambda n, hg: (g, hg, 0, 0))],
        out_specs=(out, out), compiler_params=_params("parallel", "parallel"),
    )(qkv, qkv, qkv, qkv, qkv, gq, gk, slopes)


def dilated_merge(os_, ls_, name):
    S, W = os_[0].shape
    G = len(os_)
    ts, tw = _tile(S, 512, 8), _tile(W, 512)

    def body(*refs):
        o_refs, l_refs, (o_ref, t_ref) = refs[:G], refs[G:2 * G], refs[2 * G:]
        ls = [r[...] for r in l_refs]
        m = ls[0]
        for l in ls[1:]:
            m = jnp.maximum(m, l)
        es = [jnp.exp(l - m) for l in ls]
        tot = es[0]
        for e in es[1:]:
            tot = tot + e
        acc = o_refs[0][...] * (es[0] / tot)
        for r, e in zip(o_refs[1:], es[1:]):
            acc = acc + r[...] * (e / tot)
        o_ref[...] = acc.astype(o_ref.dtype)
        t_ref[...] = m + jnp.log(tot)

    blk = pl.BlockSpec((ts, tw), lambda i, j: (i, j))
    return pl.pallas_call(
        body, name=name, out_shape=(_act((S, W)), _f32((S, W))), grid=(S // ts, W // tw),
        in_specs=[blk] * (2 * G), out_specs=(blk, blk), compiler_params=_params("parallel", "parallel"))(*os_, *ls_)


def dilated_delta(do, o, name):
    S, W = do.shape
    ts = _tile(S, 512, 8)

    def body(d_ref, o_ref, out_ref):
        out_ref[...] = jnp.broadcast_to(jnp.sum(d_ref[...] * o_ref[...].astype(jnp.float32), axis=-1, keepdims=True), out_ref.shape)

    blk = pl.BlockSpec((ts, LANES), lambda i, h: (i, h))
    return pl.pallas_call(body, name=name, out_shape=_f32((S, W)), grid=(S // ts, W // LANES), in_specs=[blk, blk],
                          out_specs=blk, compiler_params=_params("parallel", "parallel"))(do, o)


def dilated_bwd(qkv, do, lse, delta, gq, gk, slopes, g, d, Hd, G, name):
    S, C = qkv.shape
    nblk = S // (BLK * d)
    W = Hd * LANES
    scale = 1.0 / math.sqrt(LANES)
    hb = _dil_heads(d, Hd, 1 << 20)
    spec = _dil_specs(d, nblk, Hd, G, g, hb)
    hspec = lambda shift: _head_spec(d, nblk, hb, shift)

    def body(q_ref, qx_ref, kc_ref, kp_ref, vc_ref, vp_ref, do_ref, dox_ref, l_ref, lx_ref, dl_ref, dlx_ref,
             gq_ref, gk_ref, sl_ref, dq_ref, dk_ref, dv_ref, dgq_ref, dgk_ref):
        n = pl.program_id(0)
        gqv, gkv = gq_ref[...], gk_ref[...]
        nrm = lambda t, gg: (t * _rstd(t, LANES) * gg).astype(MXU_DTYPE)
        f32dot = lambda a, b, dn: lax.dot_general(a, b, dn, preferred_element_type=jnp.float32)

        def residue(r, carry):
            rows = _dil_rows(r, d)
            dgq_sum, dgk_sum = carry
            for hh in range(hb):
                cols = slice(hh * LANES, (hh + 1) * LANES)
                slope_d = sl_ref[hh][:, :1] * float(d)
                q, kc = q_ref[rows, cols], kc_ref[rows, cols]
                qn, qxn = nrm(q, gqv), nrm(qx_ref[rows, cols], gqv)
                kcn, kpn = nrm(kc, gkv), nrm(kp_ref[rows, cols], gkv)
                vc, vp = vc_ref[rows, cols].astype(MXU_DTYPE), vp_ref[rows, cols].astype(MXU_DTYPE)
                dob, doxb = do_ref[rows, cols].astype(MXU_DTYPE), dox_ref[rows, cols].astype(MXU_DTYPE)
                lrow, lxrow = l_ref[rows, cols][:, :1], lx_ref[rows, cols][:, :1]
                drow, dxrow = dl_ref[rows, cols][:, :1], dlx_ref[rows, cols][:, :1]
                pc = jnp.exp(_dil_scores(qn, kcn, scale, slope_d, False, True) - lrow)
                pp = jnp.exp(_dil_scores(qn, kpn, scale, slope_d, True, n > 0) - lrow)
                dsc = pc * (f32dot(dob, vc, NT) - drow)
                dsp = pp * (f32dot(dob, vp, NT) - drow)
                dqn = (jnp.dot(dsc.astype(MXU_DTYPE), kcn, preferred_element_type=jnp.float32)
                       + jnp.dot(dsp.astype(MXU_DTYPE), kpn, preferred_element_type=jnp.float32)) * scale
                dq, dgq = _norm_bwd(q, gqv, dqn, LANES)
                dq_ref[rows, cols] = dq
                px = jnp.exp(_dil_scores(qxn, kcn, scale, slope_d, True, n < nblk - 1) - lxrow)
                dsx = px * (f32dot(doxb, vc, NT) - dxrow)
                dkn = (f32dot(dsc.astype(MXU_DTYPE), qn, TN) + f32dot(dsx.astype(MXU_DTYPE), qxn, TN)) * scale
                dk, dgk = _norm_bwd(kc, gkv, dkn, LANES)
                dk_ref[rows, cols] = dk
                dv_ref[rows, cols] = f32dot(pc.astype(MXU_DTYPE), dob, TN) + f32dot(px.astype(MXU_DTYPE), doxb, TN)
                dgq_sum = dgq_sum + jnp.sum(dgq, axis=0, keepdims=True)
                dgk_sum = dgk_sum + jnp.sum(dgk, axis=0, keepdims=True)
            return dgq_sum, dgk_sum

        zero = jnp.zeros((1, LANES), jnp.float32)
        dgq_sum, dgk_sum = _loop_residues(d, residue, (zero, zero))
        first = jnp.logical_and(n == 0, pl.program_id(1) == 0)
        _accumulate(dgq_ref, dgq_sum, first)
        _accumulate(dgk_ref, dgk_sum, first)

    vec = pl.BlockSpec((1, LANES), lambda n, hg: (0, 0))
    out = hspec(0)
    return pl.pallas_call(
        body, name=name, out_shape=(_f32((S, W)), _f32((S, W)), _f32((S, W)), _f32((1, LANES)), _f32((1, LANES))),
        grid=(nblk, Hd // hb),
        in_specs=[spec(0, 0), spec(0, 1), spec(1, 0), spec(1, -1), spec(2, 0), spec(2, -1), hspec(0), hspec(1), hspec(0), hspec(1),
                  hspec(0), hspec(1), vec, vec, pl.BlockSpec((None, hb, 1, LANES), lambda n, hg: (g, hg, 0, 0))],
        out_specs=(out, out, out, vec, vec), compiler_params=_params("arbitrary", "arbitrary"),
    )(qkv, qkv, qkv, qkv, qkv, qkv, do, do, lse, lse, delta, delta, gq, gk, slopes)


def adamw(w, g, m, v, layer, prev, name):
    L, r, c = w.shape
    tr, tc = _tile(r, 512, 8), _tile(c, 1024)
    c1 = 1.0 / (1.0 - ADAM_B1 ** ADAM_STEP)
    c2 = 1.0 / (1.0 - ADAM_B2 ** ADAM_STEP)

    def body(*refs):
        w_ref, g_ref, m_ref, v_ref = refs[:4]
        go_ref, d_ref, mo_ref, vo_ref = refs[-4:]
        gv = g_ref[...]
        mn = ADAM_B1 * m_ref[...] + (1.0 - ADAM_B1) * gv
        vn = ADAM_B2 * v_ref[...] + (1.0 - ADAM_B2) * (gv * gv)
        go_ref[...] = gv
        d_ref[...] = -ADAM_LR * ((mn * c1) / (jnp.sqrt(vn * c2) + ADAM_EPS) + ADAM_WD * w_ref[...])
        mo_ref[...] = mn
        vo_ref[...] = vn

    lay = pl.BlockSpec((None, tr, tc), lambda i, j: (layer, i, j))
    flat = pl.BlockSpec((tr, tc), lambda i, j: (i, j))
    ins = [w, g, m, v] + (list(prev) if prev is not None else [])
    in_specs = [lay, flat, lay, lay] + ([ANY] * 4 if prev is not None else [])
    return pl.pallas_call(
        body, name=name, out_shape=tuple(_f32((L, r, c)) for _ in range(4)), grid=(r // tr, c // tc),
        in_specs=in_specs, out_specs=(lay, lay, lay, lay),
        input_output_aliases=({4 + k: k for k in range(4)} if prev is not None else {}),
        compiler_params=_params("parallel", "parallel"))(*ins)


def _ffn_fwd(h, g, W, kind, tag):
    xn = rmsnorm_fwd(h, g, tag + "_norm")
    u, a = swiglu_in(xn, W(kind + "_w_in", h), tag + "_in")
    out = matmul(a, W(kind + "_w_out", a), scale=0.5, res=h, name=tag + "_out", tk=2816)
    return out, (h, xn, u, a)


def _ffn_bwd(dout, saved, g, W, emit, kind, tag):
    h, xn, u, a = saved
    emit(kind + "_w_out", matmul(a, dout, ta=True, scale=0.5, out_dtype=WIRE_DTYPE, out_axis=0, name=tag + "_dwout", tm=1408, tk=2048))
    du = Sharded(swiglu_out_bwd(dout, W(kind + "_w_out", None), u, 0.5, tag + "_da"), 0, 1)
    emit(kind + "_w_in", matmul(xn, du, ta=True, out_dtype=WIRE_DTYPE, out_axis=1, name=tag + "_dwin", tn=1408, tk=2048))
    dxn = matmul(du, W(kind + "_w_in", None), tb=True, name=tag + "_dxn", tk=2816)
    return rmsnorm_bwd(h, g, dxn, dout, tag + "_dnorm")


def _pad_gain(g):
    return jnp.pad(g, ((0, 0), (0, QK_PAD - QK_DIM)))


def _mla_fwd(h, P, W, tabs, H):
    g_mix, g_cq, g_ckv = P["mix_norm"][0:1], P["mla_g_cq"], P["mla_g_ckv"]
    pe_blk = (g_cq.shape[1] + g_ckv.shape[1]) // LANES
    xn = rmsnorm_fwd(h, g_mix, "mla_norm")
    w_down = W("mla_w_down", h)
    lat = matmul(xn, w_down, name="mla_down", tn=w_down.shape[1])
    cq, ckv = mla_latent_fwd(lat, g_cq, g_ckv, "mla_latent")
    qraw = matmul(cq, W("mla_w_uq", lat), name="mla_uq")
    kvraw = matmul(ckv, W("mla_w_ukv", qraw), name="mla_ukv")
    q = mla_q_prep_fwd(qraw, _pad_gain(P["mla_g_qn"]), tabs, H, "mla_qprep")
    k, v = mla_k_prep_fwd(kvraw, lat, _pad_gain(P["mla_g_kn"]), tabs, H, pe_blk, "mla_kprep")
    o, lse = mla_attention_fwd(q, k, v, "mla_attn")
    out = matmul(o, W("mla_w_o", lse), res=h, name="mla_o")
    return out, (h, xn, lat, cq, ckv, qraw, kvraw, q, k, v, o, lse, pe_blk)


def _mla_bwd(dout, saved, P, W, emit, tabs, H):
    h, xn, lat, cq, ckv, qraw, kvraw, q, k, v, o, lse, pe_blk = saved
    emit("mla_w_o", matmul(o, dout, ta=True, out_dtype=WIRE_DTYPE, out_axis=0, name="mla_dwo", tm=512))
    do = matmul(dout, W("mla_w_o", None), tb=True, name="mla_do", tn=512)
    dq, dk, dv = mla_attention_bwd(q, k, v, do, o, lse, "mla_attn_bwd")
    dqraw, dgq = mla_q_prep_bwd(dq, qraw, _pad_gain(P["mla_g_qn"]), tabs, H, "mla_dqprep")
    dkvraw, dkpe, dgk = mla_k_prep_bwd(dk, dv, kvraw, lat, _pad_gain(P["mla_g_kn"]), tabs, H, pe_blk, "mla_dkprep")
    emit("mla_w_uq", matmul(cq, dqraw, ta=True, out_dtype=WIRE_DTYPE, out_axis=1, name="mla_dwuq"))
    dcq = matmul(dqraw, W("mla_w_uq", None), tb=True, name="mla_dcq", tk=1024)
    emit("mla_w_ukv", matmul(ckv, dkvraw, ta=True, out_dtype=WIRE_DTYPE, out_axis=1, name="mla_dwukv"))
    dckv = matmul(dkvraw, W("mla_w_ukv", None), tb=True, name="mla_dckv", tk=1024)
    dlat, dgcq, dgckv = mla_latent_bwd(dcq, dckv, dkpe, lat, P["mla_g_cq"], P["mla_g_ckv"], "mla_dlatent")
    emit("mla_w_down", matmul(xn, dlat, ta=True, out_dtype=WIRE_DTYPE, out_axis=0, name="mla_dwdown", tm=512, tn=dlat.shape[1]))
    dxn = matmul(dlat, W("mla_w_down", None), tb=True, name="mla_dxn", tn=512, tk=dlat.shape[1])
    dh, dgm = rmsnorm_bwd(h, P["mix_norm"][0:1], dxn, dout, "mla_dnorm")
    return dh, dgm, dict(mla_g_qn=dgq[:, :QK_DIM], mla_g_kn=dgk[:, :QK_DIM], mla_g_cq=dgcq, mla_g_ckv=dgckv)


def _dil_fwd(h, P, W, slopes, Hd):
    G = len(DIL_PAIRS)
    xn = rmsnorm_fwd(h, P["mix_norm"][1:2], "dil_norm")
    qkv = matmul(xn, W("dil_w_qkv", h), name="dil_qkv", tn=1152)
    os_, ls_ = [], []
    for g, (_, d) in enumerate(DIL_PAIRS):
        o_g, l_g = dilated_fwd(qkv, P["dil_g_qn"], P["dil_g_kn"], slopes, g, d, Hd, G, f"dil_attn{g}")
        os_.append(o_g)
        ls_.append(l_g)
    o, lse = dilated_merge(os_, ls_, "dil_merge")
    out = matmul(o, W("dil_w_o", lse), res=h, name="dil_o", tn=512)
    return out, (h, xn, qkv, o, lse)


def _dil_bwd(dout, saved, P, W, emit, slopes, Hd):
    h, xn, qkv, o, lse = saved
    ngrp = len(DIL_PAIRS)
    emit("dil_w_o", matmul(o, dout, ta=True, out_dtype=WIRE_DTYPE, out_axis=1, name="dil_dwo", tn=512))
    do = matmul(dout, W("dil_w_o", None), tb=True, name="dil_do", tk=512)
    delta = dilated_delta(do, o, "dil_delta")
    parts = [dilated_bwd(qkv, do, lse, delta, P["dil_g_qn"], P["dil_g_kn"], slopes, g, d, Hd, ngrp, f"dil_dattn{g}")
             for g, (_, d) in enumerate(DIL_PAIRS)]
    dqkv = jnp.concatenate([p[kind] for kind in range(3) for p in parts], axis=1).astype(MXU_DTYPE)
    emit("dil_w_qkv", matmul(xn, dqkv, ta=True, out_dtype=WIRE_DTYPE, out_axis=1, name="dil_dwqkv", tn=1152))
    dxn = matmul(dqkv, W("dil_w_qkv", None), tb=True, name="dil_dxn", tk=2304)
    dh, dgm = rmsnorm_bwd(h, P["mix_norm"][1:2], dxn, dout, "dil_dnorm")
    return dh, dgm, dict(dil_g_qn=parts[0][3] + parts[1][3] + parts[2][3], dil_g_kn=parts[0][4] + parts[1][4] + parts[2][4])


def local_step(x, target, P, get_w, on_grad):
    S, D = x.shape
    H, Hd = MLA_HEADS, DIL_HEADS
    tabs = rope_tables(S)
    slopes = _alibi_slopes(len(DIL_PAIRS), Hd)
    cache = {}

    def weights_of(layer):
        def W(name, after):
            if (name, layer) not in cache:
                cache[name, layer] = Sharded(get_w(name, layer, after), 0, SHARD_AXIS[name])
            return cache[name, layer]
        return W

    row = lambda name, i: P[name][i:i + 1]
    h = x
    saved = []
    for i in range(2):
        W = weights_of(i)
        h, s1 = _ffn_fwd(h, row("ffn1_norm", i), W, "ffn1", f"l{i}_ffn1")
        h, sm = _mla_fwd(h, P, weights_of(0), tabs, H) if i == 0 else _dil_fwd(h, P, weights_of(0), slopes, Hd)
        h, s2 = _ffn_fwd(h, row("ffn2_norm", i), W, "ffn2", f"l{i}_ffn2")
        saved.append((s1, sm, s2))
    loss, dh = loss_head(h, target, "loss")

    gs = {n: [None, None] for n in ("ffn1_norm", "mix_norm", "ffn2_norm")}
    for i in (1, 0):
        s1, sm, s2 = saved[i]
        W = weights_of(i)
        emit = lambda name, g4, layer=i: on_grad(name, layer, g4)
        emit0 = lambda name, g4: on_grad(name, 0, g4)
        dh, gs["ffn2_norm"][i] = _ffn_bwd(dh, s2, row("ffn2_norm", i), W, emit, "ffn2", f"l{i}_ffn2")
        if i == 0:
            dh, gs["mix_norm"][i], gm = _mla_bwd(dh, sm, P, weights_of(0), emit0, tabs, H)
        else:
            dh, gs["mix_norm"][i], gm = _dil_bwd(dh, sm, P, weights_of(0), emit0, slopes, Hd)
        gs.update({n: [val] for n, val in gm.items()})
        dh, gs["ffn1_norm"][i] = _ffn_bwd(dh, s1, row("ffn1_norm", i), W, emit, "ffn1", f"l{i}_ffn1")
    gsmall = {n: jnp.concatenate(v, axis=0) for n, v in gs.items()}
    return loss, dh, gsmall


def _pad_heads(w, real, padded):
    lead, n = w.shape[:-1], w.shape[-1] // real
    w = jnp.pad(w.reshape(*lead, n, real), [(0, 0)] * (len(lead) + 1) + [(0, padded - real)])
    return w.reshape(*lead, n * padded)


def _unpad_heads(w, real, padded):
    lead, n = w.shape[:-1], w.shape[-1] // padded
    return w.reshape(*lead, n, padded)[..., :real].reshape(*lead, n * real)


def _pack_small(gs):
    flat = jnp.concatenate([gs[n].reshape(-1) for n in SMALL])
    rows = -(-flat.shape[0] // LANES)
    rows = -(-rows // 8) * 8
    return jnp.pad(flat, (0, rows * LANES - flat.shape[0])).reshape(rows, LANES)


def _unpack_small(packed, like):
    flat, out, off = packed.reshape(-1), {}, 0
    for n in SMALL:
        size = int(np.prod(like[n].shape))
        out[n] = flat[off:off + size].reshape(like[n].shape)
        off += size
    return out


def kernel(x, ffn1_norm, ffn1_w_in, ffn1_w_out, mix_norm, ffn2_norm, ffn2_w_in, ffn2_w_out, mla_w_down, mla_g_cq, mla_g_ckv, mla_w_uq, mla_w_ukv, mla_g_qn, mla_g_kn, mla_w_o, dil_w_qkv, dil_g_qn, dil_g_kn, dil_w_o, loss_target, m_ffn1_norm, m_ffn1_w_in, m_ffn1_w_out, m_mix_norm, m_ffn2_norm, m_ffn2_w_in, m_ffn2_w_out, m_mla_w_down, m_mla_g_cq, m_mla_g_ckv, m_mla_w_uq, m_mla_w_ukv, m_mla_g_qn, m_mla_g_kn, m_mla_w_o, m_dil_w_qkv, m_dil_g_qn, m_dil_g_kn, m_dil_w_o, v_ffn1_norm, v_ffn1_w_in, v_ffn1_w_out, v_mix_norm, v_ffn2_norm, v_ffn2_w_in, v_ffn2_w_out, v_mla_w_down, v_mla_g_cq, v_mla_g_ckv, v_mla_w_uq, v_mla_w_ukv, v_mla_g_qn, v_mla_g_kn, v_mla_w_o, v_dil_w_qkv, v_dil_g_qn, v_dil_g_kn, v_dil_w_o):
    args = dict(locals())
    w = {n: args[n] for n in WEIGHTS}
    m = {n: args["m_" + n] for n in WEIGHTS}
    v = {n: args["v_" + n] for n in WEIGHTS}
    cx, cy, cc = _me()
    core = jnp.reshape(cc, (1,)).astype(jnp.int32)
    shard = jnp.reshape(2 * cx + cy, (1,)).astype(jnp.int32)
    place = (shard, core)
    pe_pad = LANES - ROPE_DIM

    order = [(n, layer if w[n].shape[0] > 1 else 0) for layer in range(2) for n in USE_ORDER[layer]]
    lands, over_ici, to_sibling = {}, {}, {}

    def cast(key, after):
        _, r, c = w[key[0]].shape
        lands[key] = cast_into_shards(w[key[0]], key[1], shard, after, "ag_%s%d_cast" % key).reshape(N_CHIPS, 1, 2, r // 2, c)
        return lands[key]

    def start(key, after):
        over_ici[key] = exchange_start((lands.pop(key),), ag_over_ici, after, "ag_%s%d_start" % key)
        return over_ici[key][2]

    def pass_on(key, after):
        if key not in to_sibling:
            bufs = exchange_wait(over_ici[key], ag_over_ici, after, "ag_%s%d_wait" % key)
            to_sibling[key] = exchange_start(bufs, ag_to_sibling, after, "ag_%s%d_pass" % key)
        return to_sibling[key][2]

    def get_w(n, l, after):
        k = order.index((n, l))
        behind = after
        if k == 0:
            for key in order[:2]:
                cast(key, after)
            for key in order[:2]:
                behind = start(key, after)
            for key in order[2:]:
                behind = cast(key, behind)
        else:
            for key in order[len(over_ici):min(k + AG_AHEAD, len(order) - 1) + 1]:
                behind = start(key, after)
        behind = pass_on((n, l), behind)
        if 0 < k < len(order) - 1:
            behind = pass_on(order[k + 1], behind)
        (land,) = exchange_wait(to_sibling[n, l], ag_to_sibling, behind, f"ag_{n}{l}_passed")
        _, _, _, h, c = land.shape
        full = land.reshape(N_CHIPS, 1, 2 * h, c)
        if n == "mla_w_down":
            full = jnp.pad(full, ((0, 0), (0, 0), (0, 0), (0, pe_pad)))
        if n == "mla_w_uq":
            full = _pad_heads(full, QK_DIM, QK_PAD)
        return full

    outs = {n: None for n in BIG}

    def update(key, g):
        n, l = key
        outs[n] = adamw(w[n], g, m[n], v[n], l, outs[n], f"adamw_{n}{l}")

    reduce_scatter = ReduceScatter(place, update)

    def on_grad(n, l, g4):
        if n == "mla_w_down":
            g4 = g4[..., :g4.shape[-1] - pe_pad]
        if n == "mla_w_uq":
            g4 = _unpad_heads(g4, QK_DIM, QK_PAD)
        reduce_scatter.push((n, l), g4)

    loss, grad_x, gsmall = local_step(x[0], loss_target[0], {n: w[n] for n in SMALL}, get_w, on_grad)
    loss = lax.psum(loss, ("x", "y", "c"))
    reduce_scatter.drain(grad_x)
    small = _unpack_small(all_reduce_small(_pack_small(gsmall), "ar_small"), gsmall)
    for n in SMALL:
        outs[n] = tuple(o[0] for o in adamw(w[n][None], small[n], m[n][None], v[n][None], 0, None, f"adamw_{n}"))

    return (loss, grad_x[None], *[outs[n][0] for n in WEIGHTS], *[outs[n][1] for n in WEIGHTS],
            *[outs[n][2] for n in WEIGHTS], *[outs[n][3] for n in WEIGHTS])
```

```python
import math

import numpy as np
import jax
import jax.numpy as jnp
from jax import lax
from jax.experimental import pallas as pl
from jax.experimental.pallas import tpu as pltpu

MXU_DTYPE = jnp.bfloat16
WIRE_DTYPE = jnp.bfloat16
EPS = 1e-6
NEG = -1e30
N_CHIPS = 4
MESH = pl.DeviceIdType.MESH
ANY = pl.BlockSpec(memory_space=pl.ANY)
LANES = 128

MLA_HEADS = 16
NOPE_DIM = 128
ROPE_DIM = 64
QK_DIM = NOPE_DIM + ROPE_DIM
QK_PAD = 2 * LANES
PREP_ROWS = 1024
ATTN_BLOCK = 1024
AG_AHEAD = 3
RS_WINDOW = 2
ROPE_THETA = 10000.0
DIL_PAIRS = ((128, 1), (512, 4), (2048, 16))
DIL_HEADS = 8
BLK = 128

ADAM_LR = 0.001
ADAM_B1 = 0.9
ADAM_B2 = 0.999
ADAM_EPS = 1e-08
ADAM_WD = 0.01
ADAM_STEP = 10

NT = (((1,), (1,)), ((), ()))
TN = (((0,), (0,)), ((), ()))

SHARD_AXIS = {"ffn1_w_in": 1, "ffn1_w_out": 0, "ffn2_w_in": 1, "ffn2_w_out": 0, "mla_w_down": 0, "mla_w_uq": 1,
              "mla_w_ukv": 1, "mla_w_o": 0, "dil_w_qkv": 1, "dil_w_o": 1}
BIG = tuple(SHARD_AXIS)
USE_ORDER = (("ffn1_w_in", "ffn1_w_out", "mla_w_down", "mla_w_uq", "mla_w_ukv", "mla_w_o", "ffn2_w_in", "ffn2_w_out"),
             ("ffn1_w_in", "ffn1_w_out", "dil_w_qkv", "dil_w_o", "ffn2_w_in", "ffn2_w_out"))
SMALL = ("ffn1_norm", "mix_norm", "ffn2_norm", "mla_g_cq", "mla_g_ckv", "mla_g_qn", "mla_g_kn", "dil_g_qn", "dil_g_kn")
WEIGHTS = ("ffn1_norm", "ffn1_w_in", "ffn1_w_out", "mix_norm", "ffn2_norm", "ffn2_w_in", "ffn2_w_out", "mla_w_down",
           "mla_g_cq", "mla_g_ckv", "mla_w_uq", "mla_w_ukv", "mla_g_qn", "mla_g_kn", "mla_w_o", "dil_w_qkv", "dil_g_qn",
           "dil_g_kn", "dil_w_o")


def _tile(dim, pref, mult=LANES):
    if dim <= pref:
        return dim
    t = (pref // mult) * mult
    while t >= mult:
        if dim % t == 0:
            return t
        t -= mult
    return dim


def _params(*sem):
    return pltpu.CompilerParams(dimension_semantics=sem)


def _f32(shape):
    return jax.ShapeDtypeStruct(shape, jnp.float32)


def _act(shape):
    return jax.ShapeDtypeStruct(shape, MXU_DTYPE)


class Sharded:
    def __init__(self, arr, layer, axis):
        self.arr, self.layer, self.axis = arr, layer, axis
        n, _, r, c = arr.shape
        self.shape = (n * r, c) if axis == 0 else (r, n * c)
        self.per = r if axis == 0 else c

    def spec(self, tr, tc, rc_of):
        l = self.layer
        if self.axis == 0:
            n = self.per // tr

            def imap(*g):
                bi, bj = rc_of(*g)
                return (bi // n, l, bi % n, bj)
        else:
            n = self.per // tc

            def imap(*g):
                bi, bj = rc_of(*g)
                return (bj // n, l, bi, bj % n)
        return pl.BlockSpec((None, None, tr, tc), imap)


def _spec2(tr, tc, rc_of):
    return pl.BlockSpec((tr, tc), lambda *g: rc_of(*g))


def matmul(a, b, *, ta=False, tb=False, out_dtype=jnp.float32, scale=None, res=None, out_axis=None,
           name, tm=1024, tn=1024, tk=2048):
    am, ak = (a.shape[1], a.shape[0]) if ta else a.shape
    bk, bn = (b.shape[1], b.shape[0]) if tb else b.shape
    assert ak == bk, (name, a.shape, b.shape, ta, tb)
    M, N, K = am, bn, ak

    def per(x, axis):
        return x.per if isinstance(x, Sharded) and x.axis == axis else None

    def pick(dim, pref, *pers):
        return _tile(math.gcd(dim, *[p for p in pers if p is not None]), pref)

    tm = pick(M, tm, per(a, 1 if ta else 0), M // N_CHIPS if out_axis == 0 else None)
    tn = pick(N, tn, per(b, 0 if tb else 1), N // N_CHIPS if out_axis == 1 else None)
    tk = pick(K, tk, per(a, 0 if ta else 1), per(b, 1 if tb else 0))
    assert M % tm == 0 and N % tn == 0 and K % tk == 0, (name, M, N, K, tm, tn, tk)
    nk = K // tk

    a_rc = (lambda i, j, k: (k, i)) if ta else (lambda i, j, k: (i, k))
    b_rc = (lambda i, j, k: (j, k)) if tb else (lambda i, j, k: (k, j))
    a_blk = (tk, tm) if ta else (tm, tk)
    b_blk = (tn, tk) if tb else (tk, tn)
    a_spec = a.spec(*a_blk, a_rc) if isinstance(a, Sharded) else _spec2(*a_blk, a_rc)
    b_spec = b.spec(*b_blk, b_rc) if isinstance(b, Sharded) else _spec2(*b_blk, b_rc)
    dn = (((0 if ta else 1,), (1 if tb else 0,)), ((), ()))
    has_res = res is not None

    def body(*refs):
        if has_res:
            a_ref, b_ref, r_ref, o_ref, acc = refs
        else:
            a_ref, b_ref, o_ref, acc = refs
        k = pl.program_id(2)

        @pl.when(k == 0)
        def _():
            acc[...] = jnp.zeros_like(acc)

        acc[...] += lax.dot_general(a_ref[...].astype(MXU_DTYPE), b_ref[...].astype(MXU_DTYPE), dn,
                                    preferred_element_type=jnp.float32)

        @pl.when(k == nk - 1)
        def _():
            r = acc[...]
            if scale is not None:
                r = r * scale
            if has_res:
                r = r + r_ref[...]
            o_ref[...] = r.astype(o_ref.dtype)

    in_specs = [a_spec, b_spec]
    args = [a.arr if isinstance(a, Sharded) else a, b.arr if isinstance(b, Sharded) else b]
    if has_res:
        in_specs.append(_spec2(tm, tn, lambda i, j, k: (i, j)))
        args.append(res)
    o_rc = lambda i, j, k: (i, j)
    if out_axis is None:
        out_shape = jax.ShapeDtypeStruct((M, N), out_dtype)
        out_spec = _spec2(tm, tn, o_rc)
    else:
        shp = (N_CHIPS, 1, M // N_CHIPS, N) if out_axis == 0 else (N_CHIPS, 1, M, N // N_CHIPS)
        out_shape = jax.ShapeDtypeStruct(shp, out_dtype)
        out_spec = Sharded(out_shape, 0, out_axis).spec(tm, tn, o_rc)
    return pl.pallas_call(
        body, name=name, out_shape=out_shape, grid=(M // tm, N // tn, nk),
        in_specs=in_specs, out_specs=out_spec,
        scratch_shapes=[pltpu.VMEM((tm, tn), jnp.float32)],
        compiler_params=_params("parallel", "parallel", "arbitrary"),
    )(*args)


def _me():
    return lax.axis_index("x"), lax.axis_index("y"), lax.axis_index("c")


def _other_chips(x, y):
    return [(1 - x, y), (x, 1 - y), (1 - x, 1 - y)]


HBM = pl.BlockSpec(memory_space=pltpu.HBM)
SEM = pl.BlockSpec(memory_space=pltpu.SEMAPHORE)
N_PEERS = 3
TOKEN = jax.ShapeDtypeStruct((8, LANES), jnp.float32)


def _split_params():
    return pltpu.CompilerParams(has_side_effects=pltpu.SideEffectType.DATAFLOW_SIDE_EFFECTING)


def _in_hbm(a):
    return pltpu.with_memory_space_constraint(a, pltpu.HBM)


def exchange_start(bufs, plan, after, name):
    nb, n = len(bufs), plan.copies

    def body(*refs):
        sems, token = refs[nb + 1:nb + 1 + 2 * n], refs[-1]
        for j, (src, dst, to) in enumerate(plan(refs[:nb], _me(), False)):
            pltpu.make_async_remote_copy(src_ref=src, dst_ref=dst, send_sem=sems[j], recv_sem=sems[n + j],
                                         device_id=to, device_id_type=MESH).start()
        token[...] = jnp.zeros_like(token)

    outs = pl.pallas_call(
        body, name=name,
        out_shape=(pltpu.SemaphoreType.DMA(()),) * (2 * n) + tuple(pltpu.HBM(b.shape, b.dtype) for b in bufs) + (TOKEN,),
        in_specs=(HBM,) * nb + (ANY,), out_specs=(SEM,) * (2 * n) + (HBM,) * nb + (pl.BlockSpec(memory_space=pltpu.VMEM),),
        input_output_aliases={k: 2 * n + k for k in range(nb)}, compiler_params=_split_params(),
    )(*[_in_hbm(b) for b in bufs], after)
    return outs[:2 * n], outs[2 * n:2 * n + nb], outs[-1]


def exchange_wait(started, plan, after, name):
    sems, bufs, _ = started
    nb, n = len(bufs), plan.copies

    def body(*refs):
        sems_ = refs[nb:nb + 2 * n]
        for j, (src, got, to) in enumerate(plan(refs[:nb], _me(), True)):
            cp = pltpu.make_async_remote_copy(src_ref=src, dst_ref=got, send_sem=sems_[j], recv_sem=sems_[n + j],
                                              device_id=to, device_id_type=MESH)
            cp.wait_send()
            cp.wait_recv()

    return pl.pallas_call(
        body, name=name, out_shape=tuple(pltpu.HBM(b.shape, b.dtype) for b in bufs),
        in_specs=(HBM,) * nb + (SEM,) * (2 * n) + (ANY,), out_specs=(HBM,) * nb,
        input_output_aliases={k: k for k in range(nb)}, compiler_params=_split_params(),
    )(*bufs, *sems, after)


def _plan(copies):
    def mark(f):
        f.copies = copies
        return f
    return mark


@_plan(N_PEERS)
def ag_over_ici(refs, me, arrived):
    (land,), (x, y, cc) = refs, me
    mine = land.at[2 * x + y, :, cc]
    return [(mine, land.at[2 * px + py, :, cc] if arrived else mine, (px, py, cc)) for px, py in _other_chips(x, y)]


@_plan(N_PEERS)
def ag_to_sibling(refs, me, arrived):
    (land,), (x, y, cc) = refs, me
    return [(land.at[2 * px + py, :, cc], land.at[2 * px + py, :, (1 - cc) if arrived else cc], (x, y, 1 - cc))
            for px, py in _other_chips(x, y)]


@_plan(1)
def rs_to_sibling(refs, me, arrived):
    (g, r1), (x, y, cc) = refs, me
    return [(g.at[:, :, 1 - cc], r1, (x, y, 1 - cc))]


@_plan(N_PEERS)
def rs_over_ici(refs, me, arrived):
    (p, land), (x, y, cc) = refs, me
    return [(p.at[2 * px + py], land.at[j], (px, py, cc)) for j, (px, py) in enumerate(_other_chips(x, y))]


@_plan(1)
def rs_gather_sibling(refs, me, arrived):
    (red,), (x, y, cc) = refs, me
    return [(red.at[:, cc], red.at[:, (1 - cc) if arrived else cc], (x, y, 1 - cc))]


def cast_into_shards(w, layer, shard, after, name):
    L, r, c = w.shape
    tr, tc = _tile(r, 1024, 16), _tile(c, 2048)

    def body(shard_ref, w_ref, after_ref, o_ref):
        o_ref[...] = w_ref[...].astype(o_ref.dtype)

    grid_spec = pltpu.PrefetchScalarGridSpec(
        num_scalar_prefetch=1, grid=(r // tr, c // tc),
        in_specs=[pl.BlockSpec((None, tr, tc), lambda i, j, sh: (layer, i, j)), ANY],
        out_specs=pl.BlockSpec((None, None, tr, tc), lambda i, j, sh: (sh[0], 0, i, j)))
    return pl.pallas_call(body, name=name, grid_spec=grid_spec, out_shape=jax.ShapeDtypeStruct((N_CHIPS, 1, r, c), WIRE_DTYPE),
                          compiler_params=_params("parallel", "parallel"))(shard, w, after)


def add_sibling(g, r1, core, name):
    n, L, two, h, c = g.shape
    th = _tile(h, 1024, 16)
    tc = _tile(c, 2048)

    def body(core_ref, g_ref, r_ref, o_ref):
        o_ref[...] = (g_ref[...].astype(jnp.float32) + r_ref[...].astype(jnp.float32)).astype(o_ref.dtype)

    grid_spec = pltpu.PrefetchScalarGridSpec(
        num_scalar_prefetch=1, grid=(n, L, h // th, c // tc),
        in_specs=[pl.BlockSpec((None, None, None, th, tc), lambda s, l, i, j, core: (s, l, core[0], i, j)),
                  pl.BlockSpec((None, None, th, tc), lambda s, l, i, j, core: (s, l, i, j))],
        out_specs=pl.BlockSpec((None, None, th, tc), lambda s, l, i, j, core: (s, l, i, j)))
    return pl.pallas_call(body, name=name, grid_spec=grid_spec, out_shape=jax.ShapeDtypeStruct((n, L, h, c), WIRE_DTYPE),
                          compiler_params=_params("parallel", "parallel", "parallel", "parallel"))(core, g, r1)


def add_chips(p, r2, place, name):
    n, L, h, c = p.shape
    th = _tile(h, 1024, 16)
    tc = _tile(c, 2048)

    def body(shard_ref, core_ref, p_ref, r_ref, o_ref):
        acc = p_ref[...].astype(jnp.float32)
        for j in range(3):
            acc = acc + r_ref[j].astype(jnp.float32)
        o_ref[...] = acc

    grid_spec = pltpu.PrefetchScalarGridSpec(
        num_scalar_prefetch=2, grid=(L, h // th, c // tc),
        in_specs=[pl.BlockSpec((None, None, th, tc), lambda l, i, j, shard, core: (shard[0], l, i, j)),
                  pl.BlockSpec((3, None, th, tc), lambda l, i, j, shard, core: (0, l, i, j))],
        out_specs=pl.BlockSpec((None, None, th, tc), lambda l, i, j, shard, core: (l, core[0], i, j)))
    return pl.pallas_call(body, name=name, grid_spec=grid_spec, out_shape=_f32((L, 2, h, c)),
                          compiler_params=_params("parallel", "parallel", "parallel"))(*place, p, r2)


class ReduceScatter:
    def __init__(self, place, done):
        self.place, self.done = place, done
        self.stages = [[], [], []]
        self.token = None

    def _start(self, bufs, plan, name):
        started = exchange_start(bufs, plan, self.token if self.token is not None else bufs[0], name)
        self.token = started[2]
        return started

    def _advance(self, stage, after):
        key, started = self.stages[stage].pop(0)
        name = "rs_%s%d" % key
        if stage == 0:
            g, r1 = exchange_wait(started, rs_to_sibling, after, name + "_d2d_wait")
            p = add_sibling(g, r1, self.place[1], name + "_add1")
            land = lax.empty((N_PEERS,) + p.shape[1:], p.dtype)
            self.stages[1].append((key, self._start((p, land), rs_over_ici, name + "_ici_start")))
        elif stage == 1:
            p, r2 = exchange_wait(started, rs_over_ici, after, name + "_ici_wait")
            red = add_chips(p, r2, self.place, name + "_add2")
            self.stages[2].append((key, self._start((red,), rs_gather_sibling, name + "_gather_start")))
        else:
            (red,) = exchange_wait(started, rs_gather_sibling, after, name + "_gather_wait")
            L, two, h, c = red.shape
            self.done(key, red.reshape(2 * h, c))

    def push(self, key, g4):
        n, L, r, c = g4.shape
        g = g4.reshape(n, L, 2, r // 2, c)
        r1 = lax.empty((n, L, r // 2, c), g.dtype)
        self.stages[0].append((key, self._start((g, r1), rs_to_sibling, "rs_%s%d_d2d_start" % key)))
        for stage, depth in ((2, 1), (1, RS_WINDOW), (0, 1)):
            if len(self.stages[stage]) > depth:
                self._advance(stage, self.token)

    def drain(self, after):
        for stage in (0, 1, 2):
            while self.stages[stage]:
                self._advance(stage, after)


def all_reduce_small(v, name):
    R, C = v.shape

    def body(v_ref, o_ref, buf, send_sems, recv_sems):
        x, y, cc = _me()
        buf[0] = v_ref[...]
        cps = []
        for k in range(1, 8):
            dx, dy, dc = (k >> 2) & 1, (k >> 1) & 1, k & 1
            to = (x ^ dx, y ^ dy, cc ^ dc)
            cp = pltpu.make_async_remote_copy(src_ref=v_ref, dst_ref=buf.at[k], send_sem=send_sems.at[k],
                                              recv_sem=recv_sems.at[k], device_id=to, device_id_type=MESH)
            cp.start()
            cps.append(cp)
        for cp in cps:
            cp.wait()
        me = 4 * x + 2 * y + cc
        acc = buf[me]
        for a in range(1, 8):
            acc = acc + buf[a ^ me]
        o_ref[...] = acc

    vm = pl.BlockSpec(memory_space=pltpu.VMEM)
    return pl.pallas_call(
        body, name=name, out_shape=_f32((R, C)), in_specs=[vm], out_specs=vm,
        scratch_shapes=[pltpu.VMEM((8, R, C), jnp.float32), pltpu.SemaphoreType.DMA((8,)), pltpu.SemaphoreType.DMA((8,))],
    )(v)


def _rstd(x, n):
    return lax.rsqrt(jnp.sum(x * x, axis=-1, keepdims=True) * (1.0 / n) + EPS)


def _accumulate(ref, part, first):
    @pl.when(first)
    def _():
        ref[...] = part

    @pl.when(jnp.logical_not(first))
    def _():
        ref[...] += part


def rmsnorm_fwd(x, g, name):
    S, D = x.shape
    ts = _tile(S, 256, 8)

    def body(x_ref, g_ref, o_ref):
        xv = x_ref[...]
        o_ref[...] = (xv * _rstd(xv, D) * g_ref[...]).astype(o_ref.dtype)

    return pl.pallas_call(
        body, name=name, out_shape=_act((S, D)), grid=(S // ts,),
        in_specs=[pl.BlockSpec((ts, D), lambda i: (i, 0)), pl.BlockSpec((1, D), lambda i: (0, 0))],
        out_specs=pl.BlockSpec((ts, D), lambda i: (i, 0)), compiler_params=_params("parallel"))(x, g)


def _norm_bwd(x, g, dy, n):
    r = _rstd(x, n)
    xh = x * r
    dxh = dy * g
    dx = r * (dxh - xh * (jnp.sum(dxh * xh, axis=-1, keepdims=True) * (1.0 / n)))
    return dx, dy * xh


def rmsnorm_bwd(x, g, dy, dres, name):
    S, D = x.shape
    ts = _tile(S, 256, 8)

    def body(x_ref, g_ref, dy_ref, dres_ref, dx_ref, dg_ref):
        dx, dgp = _norm_bwd(x_ref[...], g_ref[...], dy_ref[...], D)
        dx_ref[...] = dres_ref[...] + dx
        _accumulate(dg_ref, jnp.sum(dgp, axis=0, keepdims=True), pl.program_id(0) == 0)

    row = pl.BlockSpec((ts, D), lambda i: (i, 0))
    vec = pl.BlockSpec((1, D), lambda i: (0, 0))
    return pl.pallas_call(
        body, name=name, out_shape=(_f32((S, D)), _f32((1, D))), grid=(S // ts,),
        in_specs=[row, vec, row, row], out_specs=(row, vec), compiler_params=_params("arbitrary"))(x, g, dy, dres)


def _sigmoid(x):
    return 1.0 / (1.0 + jnp.exp(-x))


def swiglu_in(xn, w_in, name, tm=512, tn=1408):
    S, D = xn.shape
    F = w_in.shape[1] // 2
    tm, tn = _tile(S, tm, 8), _tile(math.gcd(F, w_in.per), tn)
    nf = F // tn

    def body(x_ref, wg_ref, wu_ref, u_ref, a_ref):
        x = x_ref[...].astype(MXU_DTYPE)
        gt = jnp.dot(x, wg_ref[...], preferred_element_type=jnp.float32)
        up = jnp.dot(x, wu_ref[...], preferred_element_type=jnp.float32)
        u_ref[0] = gt.astype(u_ref.dtype)
        u_ref[1] = up.astype(u_ref.dtype)
        a_ref[...] = (gt * _sigmoid(gt) * up).astype(a_ref.dtype)

    return pl.pallas_call(
        body, name=name, out_shape=(_act((2, 1, S, F)), _act((S, F))), grid=(nf, S // tm),
        in_specs=[pl.BlockSpec((tm, D), lambda j, i: (i, 0)), w_in.spec(D, tn, lambda j, i: (0, j)),
                  w_in.spec(D, tn, lambda j, i: (0, j + nf))],
        out_specs=(pl.BlockSpec((2, None, tm, tn), lambda j, i: (0, 0, i, j)), pl.BlockSpec((tm, tn), lambda j, i: (i, j))),
        compiler_params=_params("parallel", "parallel"))(xn, w_in.arr, w_in.arr)


def swiglu_out_bwd(dout, w_out, u, scale, name, tm=512, tn=1408):
    S, D = dout.shape
    F = w_out.shape[0]
    tm, tn = _tile(S, tm, 8), _tile(math.gcd(F, w_out.per), tn)

    def body(d_ref, w_ref, u_ref, o_ref):
        da = lax.dot_general(d_ref[...].astype(MXU_DTYPE), w_ref[...], NT, preferred_element_type=jnp.float32) * scale
        gt, up = u_ref[0].astype(jnp.float32), u_ref[1].astype(jnp.float32)
        s = _sigmoid(gt)
        o_ref[0] = (da * up * (s * (1.0 + gt * (1.0 - s)))).astype(o_ref.dtype)
        o_ref[1] = (da * (gt * s)).astype(o_ref.dtype)

    planes = pl.BlockSpec((2, None, tm, tn), lambda j, i: (0, 0, i, j))
    return pl.pallas_call(
        body, name=name, out_shape=_act((2, 1, S, F)), grid=(F // tn, S // tm),
        in_specs=[pl.BlockSpec((tm, D), lambda j, i: (i, 0)), w_out.spec(tn, D, lambda j, i: (j, 0)), planes],
        out_specs=planes, compiler_params=_params("parallel", "parallel"))(dout, w_out.arr, u)


def loss_head(y, t, name):
    S, D = y.shape
    ts = _tile(S, 256, 8)

    def body(y_ref, t_ref, dy_ref, l_ref):
        e = y_ref[...] - t_ref[...]
        dy_ref[...] = e * (1.0 / D)
        l_ref[...] = jnp.full(l_ref.shape, 0.5 * jnp.sum(jnp.sum(e * e, axis=-1, keepdims=True) * (1.0 / D)), jnp.float32)

    row = pl.BlockSpec((ts, D), lambda i: (i, 0))
    dy, parts = pl.pallas_call(
        body, name=name, out_shape=(_f32((S, D)), _f32((S // ts, 8, LANES))), grid=(S // ts,),
        in_specs=[row, row], out_specs=(row, pl.BlockSpec((None, 8, LANES), lambda i: (i, 0, 0))),
        compiler_params=_params("parallel"))(y, t)
    return jnp.sum(parts[:, 0, 0]), dy


def rope_tables(S):
    inv = 1.0 / (ROPE_THETA ** (jnp.arange(0, ROPE_DIM, 2, dtype=jnp.float32) / ROPE_DIM))
    ang = jnp.arange(S, dtype=jnp.float32)[:, None] * inv[None, :]
    c, s = jnp.cos(ang), jnp.sin(ang)
    z = jnp.zeros_like(c)
    return (jnp.concatenate([c, c, z, z], axis=1), jnp.concatenate([-s, z, z, z], axis=1),
            jnp.concatenate([z, s, z, z], axis=1))


def _rope(x, cos, sa, sb):
    return x * cos + pltpu.roll(x, 96, 1) * sa + pltpu.roll(x, 32, 1) * sb


def _rope_t(d, cos, sa, sb):
    return d * cos + pltpu.roll(d * sa, 32, 1) + pltpu.roll(d * sb, 96, 1)


def _head_norm(x1, x2, g):
    r = lax.rsqrt((jnp.sum(x1 * x1, axis=-1, keepdims=True) + jnp.sum(x2 * x2, axis=-1, keepdims=True)) * (1.0 / QK_DIM) + EPS)
    return x1 * r * g[:, :LANES], x2 * r * g[:, LANES:], r


def _head_norm_bwd(x1, x2, g, d1, d2):
    _, _, r = _head_norm(x1, x2, g)
    h1, h2 = x1 * r, x2 * r
    e1, e2 = d1 * g[:, :LANES], d2 * g[:, LANES:]
    m = (jnp.sum(e1 * h1, axis=-1, keepdims=True) + jnp.sum(e2 * h2, axis=-1, keepdims=True)) * (1.0 / QK_DIM)
    return r * (e1 - h1 * m), r * (e2 - h2 * m), d1 * h1, d2 * h2


def mla_latent_fwd(lat, g_cq, g_ckv, name):
    S, W = lat.shape
    QL, KL = g_cq.shape[1], g_ckv.shape[1]
    ts = _tile(S, 256, 8)

    def body(l_ref, gq_ref, gk_ref, cq_ref, ckv_ref):
        a, b = l_ref[:, :QL], l_ref[:, QL:QL + KL]
        cq_ref[...] = (a * _rstd(a, QL) * gq_ref[...]).astype(cq_ref.dtype)
        ckv_ref[...] = (b * _rstd(b, KL) * gk_ref[...]).astype(ckv_ref.dtype)

    return pl.pallas_call(
        body, name=name, out_shape=(_act((S, QL)), _act((S, KL))), grid=(S // ts,),
        in_specs=[pl.BlockSpec((ts, W), lambda i: (i, 0)), pl.BlockSpec((1, QL), lambda i: (0, 0)),
                  pl.BlockSpec((1, KL), lambda i: (0, 0))],
        out_specs=(pl.BlockSpec((ts, QL), lambda i: (i, 0)), pl.BlockSpec((ts, KL), lambda i: (i, 0))),
        compiler_params=_params("parallel"))(lat, g_cq, g_ckv)


def mla_latent_bwd(dcq, dckv, dkpe, lat, g_cq, g_ckv, name):
    S, W = lat.shape
    QL, KL = g_cq.shape[1], g_ckv.shape[1]
    ts = _tile(S, 256, 8)

    def body(dq_ref, dk_ref, dp_ref, l_ref, gq_ref, gk_ref, o_ref, dgq_ref, dgk_ref):
        first = pl.program_id(0) == 0
        da, ga = _norm_bwd(l_ref[:, :QL], gq_ref[...], dq_ref[...], QL)
        db, gb = _norm_bwd(l_ref[:, QL:QL + KL], gk_ref[...], dk_ref[...], KL)
        o_ref[:, :QL] = da.astype(o_ref.dtype)
        o_ref[:, QL:QL + KL] = db.astype(o_ref.dtype)
        o_ref[:, QL + KL:] = dp_ref[...].astype(o_ref.dtype)
        _accumulate(dgq_ref, jnp.sum(ga, axis=0, keepdims=True), first)
        _accumulate(dgk_ref, jnp.sum(gb, axis=0, keepdims=True), first)

    row = lambda n: pl.BlockSpec((ts, n), lambda i: (i, 0))
    vec = lambda n: pl.BlockSpec((1, n), lambda i: (0, 0))
    return pl.pallas_call(
        body, name=name, out_shape=(_act((S, W)), _f32((1, QL)), _f32((1, KL))), grid=(S // ts,),
        in_specs=[row(QL), row(KL), row(LANES), row(W), vec(QL), vec(KL)], out_specs=(row(W), vec(QL), vec(KL)),
        compiler_params=_params("arbitrary"))(dcq, dckv, dkpe, lat, g_cq, g_ckv)


def mla_q_prep_fwd(qraw, g, tabs, H, name):
    S = qraw.shape[0]
    ts = _tile(S, PREP_ROWS, 8)

    def body(x_ref, g_ref, c_ref, a_ref, b_ref, o_ref):
        y1, y2, _ = _head_norm(x_ref[:, :LANES], x_ref[:, LANES:], g_ref[...])
        o_ref[:, :LANES] = y1.astype(o_ref.dtype)
        o_ref[:, LANES:] = _rope(y2, c_ref[...], a_ref[...], b_ref[...]).astype(o_ref.dtype)

    tab = pl.BlockSpec((ts, LANES), lambda i, h: (i, 0))
    return pl.pallas_call(
        body, name=name, out_shape=_act((H, S, QK_PAD)), grid=(S // ts, H),
        in_specs=[pl.BlockSpec((ts, QK_PAD), lambda i, h: (i, h)), pl.BlockSpec((1, QK_PAD), lambda i, h: (0, 0)), tab, tab, tab],
        out_specs=pl.BlockSpec((None, ts, QK_PAD), lambda i, h: (h, i, 0)),
        compiler_params=_params("parallel", "parallel"))(qraw, g, *tabs)


def mla_q_prep_bwd(dq, qraw, g, tabs, H, name):
    S = qraw.shape[0]
    ts = _tile(S, PREP_ROWS, 8)

    def body(d_ref, x_ref, g_ref, c_ref, a_ref, b_ref, o_ref, dg_ref):
        d2 = _rope_t(d_ref[:, LANES:], c_ref[...], a_ref[...], b_ref[...])
        dx1, dx2, g1, g2 = _head_norm_bwd(x_ref[:, :LANES], x_ref[:, LANES:], g_ref[...], d_ref[:, :LANES], d2)
        o_ref[:, :LANES] = dx1.astype(o_ref.dtype)
        o_ref[:, LANES:] = dx2.astype(o_ref.dtype)
        first = jnp.logical_and(pl.program_id(0) == 0, pl.program_id(1) == 0)
        part = jnp.concatenate([jnp.sum(g1, axis=0, keepdims=True), jnp.sum(g2, axis=0, keepdims=True)], axis=1)
        _accumulate(dg_ref, part, first)

    tab = pl.BlockSpec((ts, LANES), lambda i, h: (i, 0))
    vec = pl.BlockSpec((1, QK_PAD), lambda i, h: (0, 0))
    return pl.pallas_call(
        body, name=name, out_shape=(_act((S, H * QK_PAD)), _f32((1, QK_PAD))), grid=(S // ts, H),
        in_specs=[pl.BlockSpec((None, ts, QK_PAD), lambda i, h: (h, i, 0)), pl.BlockSpec((ts, QK_PAD), lambda i, h: (i, h)),
                  vec, tab, tab, tab],
        out_specs=(pl.BlockSpec((ts, QK_PAD), lambda i, h: (i, h)), vec),
        compiler_params=_params("arbitrary", "arbitrary"))(dq, qraw, g, *tabs)


def mla_k_prep_fwd(kvraw, lat, g, tabs, H, pe_blk, name):
    S = kvraw.shape[0]
    ts = _tile(S, PREP_ROWS, 8)

    def body(x_ref, p_ref, g_ref, c_ref, a_ref, b_ref, k_ref, v_ref):
        y1, y2, _ = _head_norm(x_ref[:, :LANES], p_ref[...], g_ref[...])
        k_ref[:, :LANES] = y1.astype(k_ref.dtype)
        k_ref[:, LANES:] = _rope(y2, c_ref[...], a_ref[...], b_ref[...]).astype(k_ref.dtype)
        v_ref[...] = x_ref[:, LANES:].astype(v_ref.dtype)

    tab = pl.BlockSpec((ts, LANES), lambda i, h: (i, 0))
    return pl.pallas_call(
        body, name=name, out_shape=(_act((H, S, QK_PAD)), _act((H, S, LANES))), grid=(S // ts, H),
        in_specs=[pl.BlockSpec((ts, QK_PAD), lambda i, h: (i, h)), pl.BlockSpec((ts, LANES), lambda i, h: (i, pe_blk)),
                  pl.BlockSpec((1, QK_PAD), lambda i, h: (0, 0)), tab, tab, tab],
        out_specs=(pl.BlockSpec((None, ts, QK_PAD), lambda i, h: (h, i, 0)), pl.BlockSpec((None, ts, LANES), lambda i, h: (h, i, 0))),
        compiler_params=_params("parallel", "parallel"))(kvraw, lat, g, *tabs)


def mla_k_prep_bwd(dk, dv, kvraw, lat, g, tabs, H, pe_blk, name):
    S = kvraw.shape[0]
    ts = _tile(S, PREP_ROWS, 8)

    def body(dk_ref, dv_ref, x_ref, p_ref, g_ref, c_ref, a_ref, b_ref, o_ref, dp_ref, dg_ref):
        i, h = pl.program_id(0), pl.program_id(1)
        d2 = _rope_t(dk_ref[:, LANES:], c_ref[...], a_ref[...], b_ref[...])
        dx1, dx2, g1, g2 = _head_norm_bwd(x_ref[:, :LANES], p_ref[...], g_ref[...], dk_ref[:, :LANES], d2)
        o_ref[:, :LANES] = dx1.astype(o_ref.dtype)
        o_ref[:, LANES:] = dv_ref[...].astype(o_ref.dtype)
        _accumulate(dp_ref, dx2, h == 0)
        part = jnp.concatenate([jnp.sum(g1, axis=0, keepdims=True), jnp.sum(g2, axis=0, keepdims=True)], axis=1)
        _accumulate(dg_ref, part, jnp.logical_and(i == 0, h == 0))

    tab = pl.BlockSpec((ts, LANES), lambda i, h: (i, 0))
    vec = pl.BlockSpec((1, QK_PAD), lambda i, h: (0, 0))
    return pl.pallas_call(
        body, name=name, out_shape=(_act((S, H * QK_PAD)), _f32((S, LANES)), _f32((1, QK_PAD))), grid=(S // ts, H),
        in_specs=[pl.BlockSpec((None, ts, QK_PAD), lambda i, h: (h, i, 0)), pl.BlockSpec((None, ts, LANES), lambda i, h: (h, i, 0)),
                  pl.BlockSpec((ts, QK_PAD), lambda i, h: (i, h)), pl.BlockSpec((ts, LANES), lambda i, h: (i, pe_blk)),
                  vec, tab, tab, tab],
        out_specs=(pl.BlockSpec((ts, QK_PAD), lambda i, h: (i, h)), tab, vec),
        compiler_params=_params("arbitrary", "arbitrary"))(dk, dv, kvraw, lat, g, *tabs)


def _causal_scores(q, k, scale, diagonal):
    s = lax.dot_general(q, k, NT, preferred_element_type=jnp.float32) * scale
    if not diagonal:
        return s
    row = lax.broadcasted_iota(jnp.int32, s.shape, 0)
    col = lax.broadcasted_iota(jnp.int32, s.shape, 1)
    return jnp.where(col <= row, s, NEG)


def _on_causal_blocks(qi, ki, step):
    @pl.when(ki < qi)
    def _():
        step(False)

    @pl.when(ki == qi)
    def _():
        step(True)


def mla_attention_fwd(q, k, v, name):
    H, S, _ = q.shape
    t = _tile(S, ATTN_BLOCK)
    n = S // t
    scale = 1.0 / math.sqrt(QK_DIM)

    def body(q_ref, k_ref, v_ref, o_ref, lse_ref, m_sc, l_sc, acc):
        qi, ki = pl.program_id(1), pl.program_id(2)

        @pl.when(ki == 0)
        def _():
            m_sc[...] = jnp.full(m_sc.shape, NEG, jnp.float32)
            l_sc[...] = jnp.zeros_like(l_sc)
            acc[...] = jnp.zeros_like(acc)

        def step(diagonal):
            s = _causal_scores(q_ref[...], k_ref[...], scale, diagonal)
            m_new = jnp.maximum(m_sc[...], jnp.max(s, axis=-1, keepdims=True))
            alpha = jnp.exp(m_sc[...] - m_new)
            p = jnp.exp(s - m_new)
            l_sc[...] = alpha * l_sc[...] + jnp.sum(p, axis=-1, keepdims=True)
            acc[...] = alpha * acc[...] + jnp.dot(p.astype(MXU_DTYPE), v_ref[...], preferred_element_type=jnp.float32)
            m_sc[...] = m_new

        _on_causal_blocks(qi, ki, step)

        @pl.when(ki == qi)
        def _():
            o_ref[...] = (acc[...] / l_sc[...]).astype(o_ref.dtype)
            lse_ref[...] = m_sc[...] + jnp.log(l_sc[...])

    kv = lambda w: pl.BlockSpec((None, t, w), lambda h, qi, ki: (h, jnp.minimum(ki, qi), 0))
    return pl.pallas_call(
        body, name=name, out_shape=(_act((S, H * LANES)), _f32((H, S, 1))), grid=(H, n, n),
        in_specs=[pl.BlockSpec((None, t, QK_PAD), lambda h, qi, ki: (h, qi, 0)), kv(QK_PAD), kv(LANES)],
        out_specs=(pl.BlockSpec((t, LANES), lambda h, qi, ki: (qi, h)), pl.BlockSpec((None, t, 1), lambda h, qi, ki: (h, qi, 0))),
        scratch_shapes=[pltpu.VMEM((t, 1), jnp.float32), pltpu.VMEM((t, 1), jnp.float32), pltpu.VMEM((t, LANES), jnp.float32)],
        compiler_params=_params("parallel", "parallel", "arbitrary"))(q, k, v)


def mla_attention_bwd(q, k, v, do, o, lse, name):
    H, S, _ = q.shape
    t = _tile(S, ATTN_BLOCK)
    n = S // t
    scale = 1.0 / math.sqrt(QK_DIM)

    def body(q_ref, k_ref, v_ref, do_ref, o_ref, lse_ref, dq_ref, dk_ref, dv_ref, dk_acc, dv_acc):
        ki, qi = pl.program_id(1), pl.program_id(2)
        rows = pl.ds(pl.multiple_of(qi * t, t), t)

        @pl.when(qi == 0)
        def _():
            dk_acc[...] = jnp.zeros_like(dk_acc)
            dv_acc[...] = jnp.zeros_like(dv_acc)

        @pl.when(ki == 0)
        def _():
            dq_ref[rows, :] = jnp.zeros((t, QK_PAD), jnp.float32)

        def step(diagonal):
            p = jnp.exp(_causal_scores(q_ref[...], k_ref[...], scale, diagonal) - lse_ref[...])
            dof = do_ref[...]
            dob = dof.astype(MXU_DTYPE)
            delta = jnp.sum(dof * o_ref[...].astype(jnp.float32), axis=-1, keepdims=True)
            dv_acc[...] += lax.dot_general(p.astype(MXU_DTYPE), dob, TN, preferred_element_type=jnp.float32)
            dp = lax.dot_general(dob, v_ref[...], NT, preferred_element_type=jnp.float32)
            ds = (p * (dp - delta)).astype(MXU_DTYPE)
            dk_acc[...] += lax.dot_general(ds, q_ref[...], TN, preferred_element_type=jnp.float32)
            dq_ref[rows, :] += jnp.dot(ds, k_ref[...], preferred_element_type=jnp.float32) * scale

        _on_causal_blocks(qi, ki, step)

        @pl.when(qi == n - 1)
        def _():
            dk_ref[...] = dk_acc[...] * scale
            dv_ref[...] = dv_acc[...]

    qrow = lambda h, ki, qi: (h, jnp.maximum(qi, ki), 0)
    kv = lambda w: pl.BlockSpec((None, t, w), lambda h, ki, qi: (h, ki, 0))
    col = pl.BlockSpec((None, t, 1), qrow)
    head = pl.BlockSpec((t, LANES), lambda h, ki, qi: (jnp.maximum(qi, ki), h))
    return pl.pallas_call(
        body, name=name, out_shape=(_f32((H, S, QK_PAD)), _f32((H, S, QK_PAD)), _f32((H, S, LANES))), grid=(H, n, n),
        in_specs=[pl.BlockSpec((None, t, QK_PAD), qrow), kv(QK_PAD), kv(LANES),
                  head, head, col],
        out_specs=(pl.BlockSpec((None, S, QK_PAD), lambda h, ki, qi: (h, 0, 0)), kv(QK_PAD), kv(LANES)),
        scratch_shapes=[pltpu.VMEM((t, QK_PAD), jnp.float32), pltpu.VMEM((t, LANES), jnp.float32)],
        compiler_params=_params("parallel", "arbitrary", "arbitrary"))(q, k, v, do, o, lse)


def _alibi_slopes(G, Hd):
    k = np.arange(1, G * Hd + 1, dtype=np.float32)
    s = (2.0 ** (-8.0 * k / (G * Hd))).astype(np.float32).reshape(G, Hd)
    return jnp.asarray(np.broadcast_to(s[:, :, None, None], (G, Hd, 1, LANES)).copy())


def _dil_scores(qn, kn, scale, slope_d, prev, valid):
    s = lax.dot_general(qn, kn, NT, preferred_element_type=jnp.float32) * scale
    iq = lax.broadcasted_iota(jnp.int32, s.shape, 0)
    ik = lax.broadcasted_iota(jnp.int32, s.shape, 1)
    dist = iq - ik + (BLK if prev else 0)
    ok = (ik >= iq) if prev else (ik <= iq)
    s = s - slope_d * dist.astype(jnp.float32)
    return jnp.where(jnp.logical_and(ok, valid), s, NEG)


def _dil_heads(d, Hd, block_bytes):
    hb = max(1, min(Hd, block_bytes // (BLK * LANES * 4))) if d == 1 else 1
    assert Hd % hb == 0, (Hd, hb)
    return hb


def _dil_rows(r, d):
    return pl.ds(r, BLK, stride=d) if d > 1 else slice(None)


def _loop_residues(d, residue, init):
    if d == 1:
        return residue(0, init)
    return lax.fori_loop(0, d // 2, lambda i, carry: residue(2 * i + 1, residue(2 * i, carry)), init)


def _dil_specs(d, nblk, Hd, G, g, hb):
    def spec(kind, shift):
        col0 = (kind * G + g) * Hd // hb
        return pl.BlockSpec((BLK * d, hb * LANES), lambda n, hg: (jnp.clip(n + shift, 0, nblk - 1), col0 + hg))
    return spec


def _head_spec(d, nblk, hb, shift):
    return pl.BlockSpec((BLK * d, hb * LANES), lambda n, hg: (jnp.clip(n + shift, 0, nblk - 1), hg))


def dilated_fwd(qkv, gq, gk, slopes, g, d, Hd, G, name):
    S, C = qkv.shape
    nblk = S // (BLK * d)
    hb = _dil_heads(d, Hd, 2 << 20)
    scale = 1.0 / math.sqrt(LANES)
    spec = _dil_specs(d, nblk, Hd, G, g, hb)

    def body(q_ref, kc_ref, kp_ref, vc_ref, vp_ref, gq_ref, gk_ref, sl_ref, o_ref, l_ref):
        n = pl.program_id(0)
        nrm = lambda t, gg: (t * _rstd(t, LANES) * gg).astype(MXU_DTYPE)

        def residue(r, carry):
            rows = _dil_rows(r, d)
            for hh in range(hb):
                cols = slice(hh * LANES, (hh + 1) * LANES)
                slope_d = sl_ref[hh][:, :1] * float(d)
                qn = nrm(q_ref[rows, cols], gq_ref[...])
                sc = _dil_scores(qn, nrm(kc_ref[rows, cols], gk_ref[...]), scale, slope_d, False, True)
                sp = _dil_scores(qn, nrm(kp_ref[rows, cols], gk_ref[...]), scale, slope_d, True, n > 0)
                m = jnp.maximum(jnp.max(sc, axis=-1, keepdims=True), jnp.max(sp, axis=-1, keepdims=True))
                lse = m + jnp.log(jnp.sum(jnp.exp(sc - m), axis=-1, keepdims=True) + jnp.sum(jnp.exp(sp - m), axis=-1, keepdims=True))
                o = jnp.dot(jnp.exp(sc - lse).astype(MXU_DTYPE), vc_ref[rows, cols].astype(MXU_DTYPE), preferred_element_type=jnp.float32)
                o = o + jnp.dot(jnp.exp(sp - lse).astype(MXU_DTYPE), vp_ref[rows, cols].astype(MXU_DTYPE), preferred_element_type=jnp.float32)
                o_ref[rows, cols] = o
                l_ref[rows, cols] = jnp.broadcast_to(lse, (BLK, LANES))
            return carry

        _loop_residues(d, residue, 0)

    vec = pl.BlockSpec((1, LANES), lambda n, hg: (0, 0))
    out = _head_spec(d, nblk, hb, 0)
    return pl.pallas_call(
        body, name=name, out_shape=(_f32((S, Hd * LANES)), _f32((S, Hd * LANES))), grid=(nblk, Hd // hb),
        in_specs=[spec(0, 0), spec(1, 0), spec(1, -1), spec(2, 0), spec(2, -1), vec, vec,
                  pl.BlockSpec((None, hb, 1, LANES), lambda n, hg: (g, hg, 0, 0))],
        out_specs=(out, out), compiler_params=_params("parallel", "parallel"),
    )(qkv, qkv, qkv, qkv, qkv, gq, gk, slopes)


def dilated_merge(os_, ls_, name):
    S, W = os_[0].shape
    G = len(os_)
    ts, tw = _tile(S, 512, 8), _tile(W, 512)

    def body(*refs):
        o_refs, l_refs, (o_ref, t_ref) = refs[:G], refs[G:2 * G], refs[2 * G:]
        ls = [r[...] for r in l_refs]
        m = ls[0]
        for l in ls[1:]:
            m = jnp.maximum(m, l)
        es = [jnp.exp(l - m) for l in ls]
        tot = es[0]
        for e in es[1:]:
            tot = tot + e
        acc = o_refs[0][...] * (es[0] / tot)
        for r, e in zip(o_refs[1:], es[1:]):
            acc = acc + r[...] * (e / tot)
        o_ref[...] = acc.astype(o_ref.dtype)
        t_ref[...] = m + jnp.log(tot)

    blk = pl.BlockSpec((ts, tw), lambda i, j: (i, j))
    return pl.pallas_call(
        body, name=name, out_shape=(_act((S, W)), _f32((S, W))), grid=(S // ts, W // tw),
        in_specs=[blk] * (2 * G), out_specs=(blk, blk), compiler_params=_params("parallel", "parallel"))(*os_, *ls_)


def dilated_delta(do, o, name):
    S, W = do.shape
    ts = _tile(S, 512, 8)

    def body(d_ref, o_ref, out_ref):
        out_ref[...] = jnp.broadcast_to(jnp.sum(d_ref[...] * o_ref[...].astype(jnp.float32), axis=-1, keepdims=True), out_ref.shape)

    blk = pl.BlockSpec((ts, LANES), lambda i, h: (i, h))
    return pl.pallas_call(body, name=name, out_shape=_f32((S, W)), grid=(S // ts, W // LANES), in_specs=[blk, blk],
                          out_specs=blk, compiler_params=_params("parallel", "parallel"))(do, o)


def dilated_bwd(qkv, do, lse, delta, gq, gk, slopes, g, d, Hd, G, name):
    S, C = qkv.shape
    nblk = S // (BLK * d)
    W = Hd * LANES
    scale = 1.0 / math.sqrt(LANES)
    hb = _dil_heads(d, Hd, 1 << 20)
    spec = _dil_specs(d, nblk, Hd, G, g, hb)
    hspec = lambda shift: _head_spec(d, nblk, hb, shift)

    def body(q_ref, qx_ref, kc_ref, kp_ref, vc_ref, vp_ref, do_ref, dox_ref, l_ref, lx_ref, dl_ref, dlx_ref,
             gq_ref, gk_ref, sl_ref, dq_ref, dk_ref, dv_ref, dgq_ref, dgk_ref):
        n = pl.program_id(0)
        gqv, gkv = gq_ref[...], gk_ref[...]
        nrm = lambda t, gg: (t * _rstd(t, LANES) * gg).astype(MXU_DTYPE)
        f32dot = lambda a, b, dn: lax.dot_general(a, b, dn, preferred_element_type=jnp.float32)

        def residue(r, carry):
            rows = _dil_rows(r, d)
            dgq_sum, dgk_sum = carry
            for hh in range(hb):
                cols = slice(hh * LANES, (hh + 1) * LANES)
                slope_d = sl_ref[hh][:, :1] * float(d)
                q, kc = q_ref[rows, cols], kc_ref[rows, cols]
                qn, qxn = nrm(q, gqv), nrm(qx_ref[rows, cols], gqv)
                kcn, kpn = nrm(kc, gkv), nrm(kp_ref[rows, cols], gkv)
                vc, vp = vc_ref[rows, cols].astype(MXU_DTYPE), vp_ref[rows, cols].astype(MXU_DTYPE)
                dob, doxb = do_ref[rows, cols].astype(MXU_DTYPE), dox_ref[rows, cols].astype(MXU_DTYPE)
                lrow, lxrow = l_ref[rows, cols][:, :1], lx_ref[rows, cols][:, :1]
                drow, dxrow = dl_ref[rows, cols][:, :1], dlx_ref[rows, cols][:, :1]
                pc = jnp.exp(_dil_scores(qn, kcn, scale, slope_d, False, True) - lrow)
                pp = jnp.exp(_dil_scores(qn, kpn, scale, slope_d, True, n > 0) - lrow)
                dsc = pc * (f32dot(dob, vc, NT) - drow)
                dsp = pp * (f32dot(dob, vp, NT) - drow)
                dqn = (jnp.dot(dsc.astype(MXU_DTYPE), kcn, preferred_element_type=jnp.float32)
                       + jnp.dot(dsp.astype(MXU_DTYPE), kpn, preferred_element_type=jnp.float32)) * scale
                dq, dgq = _norm_bwd(q, gqv, dqn, LANES)
                dq_ref[rows, cols] = dq
                px = jnp.exp(_dil_scores(qxn, kcn, scale, slope_d, True, n < nblk - 1) - lxrow)
                dsx = px * (f32dot(doxb, vc, NT) - dxrow)
                dkn = (f32dot(dsc.astype(MXU_DTYPE), qn, TN) + f32dot(dsx.astype(MXU_DTYPE), qxn, TN)) * scale
                dk, dgk = _norm_bwd(kc, gkv, dkn, LANES)
                dk_ref[rows, cols] = dk
                dv_ref[rows, cols] = f32dot(pc.astype(MXU_DTYPE), dob, TN) + f32dot(px.astype(MXU_DTYPE), doxb, TN)
                dgq_sum = dgq_sum + jnp.sum(dgq, axis=0, keepdims=True)
                dgk_sum = dgk_sum + jnp.sum(dgk, axis=0, keepdims=True)
            return dgq_sum, dgk_sum

        zero = jnp.zeros((1, LANES), jnp.float32)
        dgq_sum, dgk_sum = _loop_residues(d, residue, (zero, zero))
        first = jnp.logical_and(n == 0, pl.program_id(1) == 0)
        _accumulate(dgq_ref, dgq_sum, first)
        _accumulate(dgk_ref, dgk_sum, first)

    vec = pl.BlockSpec((1, LANES), lambda n, hg: (0, 0))
    out = hspec(0)
    return pl.pallas_call(
        body, name=name, out_shape=(_f32((S, W)), _f32((S, W)), _f32((S, W)), _f32((1, LANES)), _f32((1, LANES))),
        grid=(nblk, Hd // hb),
        in_specs=[spec(0, 0), spec(0, 1), spec(1, 0), spec(1, -1), spec(2, 0), spec(2, -1), hspec(0), hspec(1), hspec(0), hspec(1),
                  hspec(0), hspec(1), vec, vec, pl.BlockSpec((None, hb, 1, LANES), lambda n, hg: (g, hg, 0, 0))],
        out_specs=(out, out, out, vec, vec), compiler_params=_params("arbitrary", "arbitrary"),
    )(qkv, qkv, qkv, qkv, qkv, qkv, do, do, lse, lse, delta, delta, gq, gk, slopes)


def adamw(w, g, m, v, layer, prev, name):
    L, r, c = w.shape
    tr, tc = _tile(r, 512, 8), _tile(c, 1024)
    c1 = 1.0 / (1.0 - ADAM_B1 ** ADAM_STEP)
    c2 = 1.0 / (1.0 - ADAM_B2 ** ADAM_STEP)

    def body(*refs):
        w_ref, g_ref, m_ref, v_ref = refs[:4]
        go_ref, d_ref, mo_ref, vo_ref = refs[-4:]
        gv = g_ref[...]
        mn = ADAM_B1 * m_ref[...] + (1.0 - ADAM_B1) * gv
        vn = ADAM_B2 * v_ref[...] + (1.0 - ADAM_B2) * (gv * gv)
        go_ref[...] = gv
        d_ref[...] = -ADAM_LR * ((mn * c1) / (jnp.sqrt(vn * c2) + ADAM_EPS) + ADAM_WD * w_ref[...])
        mo_ref[...] = mn
        vo_ref[...] = vn

    lay = pl.BlockSpec((None, tr, tc), lambda i, j: (layer, i, j))
    flat = pl.BlockSpec((tr, tc), lambda i, j: (i, j))
    ins = [w, g, m, v] + (list(prev) if prev is not None else [])
    in_specs = [lay, flat, lay, lay] + ([ANY] * 4 if prev is not None else [])
    return pl.pallas_call(
        body, name=name, out_shape=tuple(_f32((L, r, c)) for _ in range(4)), grid=(r // tr, c // tc),
        in_specs=in_specs, out_specs=(lay, lay, lay, lay),
        input_output_aliases=({4 + k: k for k in range(4)} if prev is not None else {}),
        compiler_params=_params("parallel", "parallel"))(*ins)


def _ffn_fwd(h, g, W, kind, tag):
    xn = rmsnorm_fwd(h, g, tag + "_norm")
    u, a = swiglu_in(xn, W(kind + "_w_in", h), tag + "_in")
    out = matmul(a, W(kind + "_w_out", a), scale=0.5, res=h, name=tag + "_out", tk=2816)
    return out, (h, xn, u, a)


def _ffn_bwd(dout, saved, g, W, emit, kind, tag):
    h, xn, u, a = saved
    emit(kind + "_w_out", matmul(a, dout, ta=True, scale=0.5, out_dtype=WIRE_DTYPE, out_axis=0, name=tag + "_dwout", tm=1408, tk=2048))
    du = Sharded(swiglu_out_bwd(dout, W(kind + "_w_out", None), u, 0.5, tag + "_da"), 0, 1)
    emit(kind + "_w_in", matmul(xn, du, ta=True, out_dtype=WIRE_DTYPE, out_axis=1, name=tag + "_dwin", tn=1408, tk=2048))
    dxn = matmul(du, W(kind + "_w_in", None), tb=True, name=tag + "_dxn", tk=2816)
    return rmsnorm_bwd(h, g, dxn, dout, tag + "_dnorm")


def _pad_gain(g):
    return jnp.pad(g, ((0, 0), (0, QK_PAD - QK_DIM)))


def _mla_fwd(h, P, W, tabs, H):
    g_mix, g_cq, g_ckv = P["mix_norm"][0:1], P["mla_g_cq"], P["mla_g_ckv"]
    pe_blk = (g_cq.shape[1] + g_ckv.shape[1]) // LANES
    xn = rmsnorm_fwd(h, g_mix, "mla_norm")
    w_down = W("mla_w_down", h)
    lat = matmul(xn, w_down, name="mla_down", tn=w_down.shape[1])
    cq, ckv = mla_latent_fwd(lat, g_cq, g_ckv, "mla_latent")
    qraw = matmul(cq, W("mla_w_uq", lat), name="mla_uq")
    kvraw = matmul(ckv, W("mla_w_ukv", qraw), name="mla_ukv")
    q = mla_q_prep_fwd(qraw, _pad_gain(P["mla_g_qn"]), tabs, H, "mla_qprep")
    k, v = mla_k_prep_fwd(kvraw, lat, _pad_gain(P["mla_g_kn"]), tabs, H, pe_blk, "mla_kprep")
    o, lse = mla_attention_fwd(q, k, v, "mla_attn")
    out = matmul(o, W("mla_w_o", lse), res=h, name="mla_o")
    return out, (h, xn, lat, cq, ckv, qraw, kvraw, q, k, v, o, lse, pe_blk)


def _mla_bwd(dout, saved, P, W, emit, tabs, H):
    h, xn, lat, cq, ckv, qraw, kvraw, q, k, v, o, lse, pe_blk = saved
    emit("mla_w_o", matmul(o, dout, ta=True, out_dtype=WIRE_DTYPE, out_axis=0, name="mla_dwo", tm=512))
    do = matmul(dout, W("mla_w_o", None), tb=True, name="mla_do", tn=512)
    dq, dk, dv = mla_attention_bwd(q, k, v, do, o, lse, "mla_attn_bwd")
    dqraw, dgq = mla_q_prep_bwd(dq, qraw, _pad_gain(P["mla_g_qn"]), tabs, H, "mla_dqprep")
    dkvraw, dkpe, dgk = mla_k_prep_bwd(dk, dv, kvraw, lat, _pad_gain(P["mla_g_kn"]), tabs, H, pe_blk, "mla_dkprep")
    emit("mla_w_uq", matmul(cq, dqraw, ta=True, out_dtype=WIRE_DTYPE, out_axis=1, name="mla_dwuq"))
    dcq = matmul(dqraw, W("mla_w_uq", None), tb=True, name="mla_dcq", tk=1024)
    emit("mla_w_ukv", matmul(ckv, dkvraw, ta=True, out_dtype=WIRE_DTYPE, out_axis=1, name="mla_dwukv"))
    dckv = matmul(dkvraw, W("mla_w_ukv", None), tb=True, name="mla_dckv", tk=1024)
    dlat, dgcq, dgckv = mla_latent_bwd(dcq, dckv, dkpe, lat, P["mla_g_cq"], P["mla_g_ckv"], "mla_dlatent")
    emit("mla_w_down", matmul(xn, dlat, ta=True, out_dtype=WIRE_DTYPE, out_axis=0, name="mla_dwdown", tm=512, tn=dlat.shape[1]))
    dxn = matmul(dlat, W("mla_w_down", None), tb=True, name="mla_dxn", tn=512, tk=dlat.shape[1])
    dh, dgm = rmsnorm_bwd(h, P["mix_norm"][0:1], dxn, dout, "mla_dnorm")
    return dh, dgm, dict(mla_g_qn=dgq[:, :QK_DIM], mla_g_kn=dgk[:, :QK_DIM], mla_g_cq=dgcq, mla_g_ckv=dgckv)


def _dil_fwd(h, P, W, slopes, Hd):
    G = len(DIL_PAIRS)
    xn = rmsnorm_fwd(h, P["mix_norm"][1:2], "dil_norm")
    qkv = matmul(xn, W("dil_w_qkv", h), name="dil_qkv", tn=1152)
    os_, ls_ = [], []
    for g, (_, d) in enumerate(DIL_PAIRS):
        o_g, l_g = dilated_fwd(qkv, P["dil_g_qn"], P["dil_g_kn"], slopes, g, d, Hd, G, f"dil_attn{g}")
        os_.append(o_g)
        ls_.append(l_g)
    o, lse = dilated_merge(os_, ls_, "dil_merge")
    out = matmul(o, W("dil_w_o", lse), res=h, name="dil_o", tn=512)
    return out, (h, xn, qkv, o, lse)


def _dil_bwd(dout, saved, P, W, emit, slopes, Hd):
    h, xn, qkv, o, lse = saved
    ngrp = len(DIL_PAIRS)
    emit("dil_w_o", matmul(o, dout, ta=True, out_dtype=WIRE_DTYPE, out_axis=1, name="dil_dwo", tn=512))
    do = matmul(dout, W("dil_w_o", None), tb=True, name="dil_do", tk=512)
    delta = dilated_delta(do, o, "dil_delta")
    parts = [dilated_bwd(qkv, do, lse, delta, P["dil_g_qn"], P["dil_g_kn"], slopes, g, d, Hd, ngrp, f"dil_dattn{g}")
             for g, (_, d) in enumerate(DIL_PAIRS)]
    dqkv = jnp.concatenate([p[kind] for kind in range(3) for p in parts], axis=1).astype(MXU_DTYPE)
    emit("dil_w_qkv", matmul(xn, dqkv, ta=True, out_dtype=WIRE_DTYPE, out_axis=1, name="dil_dwqkv", tn=1152))
    dxn = matmul(dqkv, W("dil_w_qkv", None), tb=True, name="dil_dxn", tk=2304)
    dh, dgm = rmsnorm_bwd(h, P["mix_norm"][1:2], dxn, dout, "dil_dnorm")
    return dh, dgm, dict(dil_g_qn=parts[0][3] + parts[1][3] + parts[2][3], dil_g_kn=parts[0][4] + parts[1][4] + parts[2][4])


def local_step(x, target, P, get_w, on_grad):
    S, D = x.shape
    H, Hd = MLA_HEADS, DIL_HEADS
    tabs = rope_tables(S)
    slopes = _alibi_slopes(len(DIL_PAIRS), Hd)
    cache = {}

    def weights_of(layer):
        def W(name, after):
            if (name, layer) not in cache:
                cache[name, layer] = Sharded(get_w(name, layer, after), 0, SHARD_AXIS[name])
            return cache[name, layer]
        return W

    row = lambda name, i: P[name][i:i + 1]
    h = x
    saved = []
    for i in range(2):
        W = weights_of(i)
        h, s1 = _ffn_fwd(h, row("ffn1_norm", i), W, "ffn1", f"l{i}_ffn1")
        h, sm = _mla_fwd(h, P, weights_of(0), tabs, H) if i == 0 else _dil_fwd(h, P, weights_of(0), slopes, Hd)
        h, s2 = _ffn_fwd(h, row("ffn2_norm", i), W, "ffn2", f"l{i}_ffn2")
        saved.append((s1, sm, s2))
    loss, dh = loss_head(h, target, "loss")

    gs = {n: [None, None] for n in ("ffn1_norm", "mix_norm", "ffn2_norm")}
    for i in (1, 0):
        s1, sm, s2 = saved[i]
        W = weights_of(i)
        emit = lambda name, g4, layer=i: on_grad(name, layer, g4)
        emit0 = lambda name, g4: on_grad(name, 0, g4)
        dh, gs["ffn2_norm"][i] = _ffn_bwd(dh, s2, row("ffn2_norm", i), W, emit, "ffn2", f"l{i}_ffn2")
        if i == 0:
            dh, gs["mix_norm"][i], gm = _mla_bwd(dh, sm, P, weights_of(0), emit0, tabs, H)
        else:
            dh, gs["mix_norm"][i], gm = _dil_bwd(dh, sm, P, weights_of(0), emit0, slopes, Hd)
        gs.update({n: [val] for n, val in gm.items()})
        dh, gs["ffn1_norm"][i] = _ffn_bwd(dh, s1, row("ffn1_norm", i), W, emit, "ffn1", f"l{i}_ffn1")
    gsmall = {n: jnp.concatenate(v, axis=0) for n, v in gs.items()}
    return loss, dh, gsmall


def _pad_heads(w, real, padded):
    lead, n = w.shape[:-1], w.shape[-1] // real
    w = jnp.pad(w.reshape(*lead, n, real), [(0, 0)] * (len(lead) + 1) + [(0, padded - real)])
    return w.reshape(*lead, n * padded)


def _unpad_heads(w, real, padded):
    lead, n = w.shape[:-1], w.shape[-1] // padded
    return w.reshape(*lead, n, padded)[..., :real].reshape(*lead, n * real)


def _pack_small(gs):
    flat = jnp.concatenate([gs[n].reshape(-1) for n in SMALL])
    rows = -(-flat.shape[0] // LANES)
    rows = -(-rows // 8) * 8
    return jnp.pad(flat, (0, rows * LANES - flat.shape[0])).reshape(rows, LANES)


def _unpack_small(packed, like):
    flat, out, off = packed.reshape(-1), {}, 0
    for n in SMALL:
        size = int(np.prod(like[n].shape))
        out[n] = flat[off:off + size].reshape(like[n].shape)
        off += size
    return out


def kernel(x, ffn1_norm, ffn1_w_in, ffn1_w_out, mix_norm, ffn2_norm, ffn2_w_in, ffn2_w_out, mla_w_down, mla_g_cq, mla_g_ckv, mla_w_uq, mla_w_ukv, mla_g_qn, mla_g_kn, mla_w_o, dil_w_qkv, dil_g_qn, dil_g_kn, dil_w_o, loss_target, m_ffn1_norm, m_ffn1_w_in, m_ffn1_w_out, m_mix_norm, m_ffn2_norm, m_ffn2_w_in, m_ffn2_w_out, m_mla_w_down, m_mla_g_cq, m_mla_g_ckv, m_mla_w_uq, m_mla_w_ukv, m_mla_g_qn, m_mla_g_kn, m_mla_w_o, m_dil_w_qkv, m_dil_g_qn, m_dil_g_kn, m_dil_w_o, v_ffn1_norm, v_ffn1_w_in, v_ffn1_w_out, v_mix_norm, v_ffn2_norm, v_ffn2_w_in, v_ffn2_w_out, v_mla_w_down, v_mla_g_cq, v_mla_g_ckv, v_mla_w_uq, v_mla_w_ukv, v_mla_g_qn, v_mla_g_kn, v_mla_w_o, v_dil_w_qkv, v_dil_g_qn, v_dil_g_kn, v_dil_w_o):
    args = dict(locals())
    w = {n: args[n] for n in WEIGHTS}
    m = {n: args["m_" + n] for n in WEIGHTS}
    v = {n: args["v_" + n] for n in WEIGHTS}
    cx, cy, cc = _me()
    core = jnp.reshape(cc, (1,)).astype(jnp.int32)
    shard = jnp.reshape(2 * cx + cy, (1,)).astype(jnp.int32)
    place = (shard, core)
    pe_pad = LANES - ROPE_DIM

    order = [(n, layer if w[n].shape[0] > 1 else 0) for layer in range(2) for n in USE_ORDER[layer]]
    lands, over_ici, to_sibling = {}, {}, {}

    def cast(key, after):
        _, r, c = w[key[0]].shape
        lands[key] = cast_into_shards(w[key[0]], key[1], shard, after, "ag_%s%d_cast" % key).reshape(N_CHIPS, 1, 2, r // 2, c)
        return lands[key]

    def start(key, after):
        over_ici[key] = exchange_start((lands.pop(key),), ag_over_ici, after, "ag_%s%d_start" % key)
        return over_ici[key][2]

    def pass_on(key, after):
        if key in to_sibling:
            return after
        bufs = exchange_wait(over_ici[key], ag_over_ici, after, "ag_%s%d_wait" % key)
        to_sibling[key] = exchange_start(bufs, ag_to_sibling, after, "ag_%s%d_pass" % key)
        return to_sibling[key][2]

    def get_w(n, l, after):
        k = order.index((n, l))
        behind = after
        if k == 0:
            for key in order[:2]:
                cast(key, after)
            for key in order[:2]:
                behind = start(key, behind)
            for key in order[2:]:
                behind = cast(key, behind)
        else:
            for key in order[len(over_ici):min(k + AG_AHEAD, len(order) - 1) + 1]:
                behind = start(key, behind)
        behind = pass_on((n, l), behind)
        if 0 < k < len(order) - 1:
            behind = pass_on(order[k + 1], behind)
        (land,) = exchange_wait(to_sibling[n, l], ag_to_sibling, behind, f"ag_{n}{l}_passed")
        _, _, _, h, c = land.shape
        full = land.reshape(N_CHIPS, 1, 2 * h, c)
        if n == "mla_w_down":
            full = jnp.pad(full, ((0, 0), (0, 0), (0, 0), (0, pe_pad)))
        if n == "mla_w_uq":
            full = _pad_heads(full, QK_DIM, QK_PAD)
        return full

    outs = {n: None for n in BIG}

    def update(key, g):
        n, l = key
        outs[n] = adamw(w[n], g, m[n], v[n], l, outs[n], f"adamw_{n}{l}")

    reduce_scatter = ReduceScatter(place, update)

    def on_grad(n, l, g4):
        if n == "mla_w_down":
            g4 = g4[..., :g4.shape[-1] - pe_pad]
        if n == "mla_w_uq":
            g4 = _unpad_heads(g4, QK_DIM, QK_PAD)
        reduce_scatter.push((n, l), g4)

    loss, grad_x, gsmall = local_step(x[0], loss_target[0], {n: w[n] for n in SMALL}, get_w, on_grad)
    loss = lax.psum(loss, ("x", "y", "c"))
    reduce_scatter.drain(grad_x)
    small = _unpack_small(all_reduce_small(_pack_small(gsmall), "ar_small"), gsmall)
    for n in SMALL:
        outs[n] = tuple(o[0] for o in adamw(w[n][None], small[n], m[n][None], v[n][None], 0, None, f"adamw_{n}"))

    return (loss, grad_x[None], *[outs[n][0] for n in WEIGHTS], *[outs[n][1] for n in WEIGHTS],
            *[outs[n][2] for n in WEIGHTS], *[outs[n][3] for n in WEIGHTS])
```

```python
import math

import numpy as np
import jax
import jax.numpy as jnp
from jax import lax
from jax.experimental import pallas as pl
from jax.experimental.pallas import tpu as pltpu

MXU_DTYPE = jnp.bfloat16
WIRE_DTYPE = jnp.bfloat16
EPS = 1e-6
NEG = -1e30
N_CHIPS = 4
MESH = pl.DeviceIdType.MESH
ANY = pl.BlockSpec(memory_space=pl.ANY)
LANES = 128

MLA_HEADS = 16
NOPE_DIM = 128
ROPE_DIM = 64
QK_DIM = NOPE_DIM + ROPE_DIM
QK_PAD = 2 * LANES
PREP_ROWS = 1024
ATTN_BLOCK = 1024
AG_AHEAD = 3
RS_WINDOW = 2
ROPE_THETA = 10000.0
DIL_PAIRS = ((128, 1), (512, 4), (2048, 16))
DIL_HEADS = 8
BLK = 128

ADAM_LR = 0.001
ADAM_B1 = 0.9
ADAM_B2 = 0.999
ADAM_EPS = 1e-08
ADAM_WD = 0.01
ADAM_STEP = 10

NT = (((1,), (1,)), ((), ()))
TN = (((0,), (0,)), ((), ()))

SHARD_AXIS = {"ffn1_w_in": 1, "ffn1_w_out": 0, "ffn2_w_in": 1, "ffn2_w_out": 0, "mla_w_down": 0, "mla_w_uq": 1,
              "mla_w_ukv": 1, "mla_w_o": 0, "dil_w_qkv": 1, "dil_w_o": 1}
BIG = tuple(SHARD_AXIS)
USE_ORDER = (("ffn1_w_in", "ffn1_w_out", "mla_w_down", "mla_w_uq", "mla_w_ukv", "mla_w_o", "ffn2_w_in", "ffn2_w_out"),
             ("ffn1_w_in", "ffn1_w_out", "dil_w_qkv", "dil_w_o", "ffn2_w_in", "ffn2_w_out"))
SMALL = ("ffn1_norm", "mix_norm", "ffn2_norm", "mla_g_cq", "mla_g_ckv", "mla_g_qn", "mla_g_kn", "dil_g_qn", "dil_g_kn")
WEIGHTS = ("ffn1_norm", "ffn1_w_in", "ffn1_w_out", "mix_norm", "ffn2_norm", "ffn2_w_in", "ffn2_w_out", "mla_w_down",
           "mla_g_cq", "mla_g_ckv", "mla_w_uq", "mla_w_ukv", "mla_g_qn", "mla_g_kn", "mla_w_o", "dil_w_qkv", "dil_g_qn",
           "dil_g_kn", "dil_w_o")


def _tile(dim, pref, mult=LANES):
    if dim <= pref:
        return dim
    t = (pref // mult) * mult
    while t >= mult:
        if dim % t == 0:
            return t
        t -= mult
    return dim


def _column_chunks(width, parts=2):
    tiles = width // LANES
    assert width % LANES == 0, width
    bounds = [LANES * ((tiles * k + parts - 1) // parts) for k in range(parts + 1)]
    return [slice(lo, hi) for lo, hi in zip(bounds, bounds[1:]) if hi > lo]


def _params(*sem):
    return pltpu.CompilerParams(dimension_semantics=sem)


def _f32(shape):
    return jax.ShapeDtypeStruct(shape, jnp.float32)


def _act(shape):
    return jax.ShapeDtypeStruct(shape, MXU_DTYPE)


class Sharded:
    def __init__(self, arr, layer, axis):
        self.arr, self.layer, self.axis = arr, layer, axis
        n, _, r, c = arr.shape
        self.shape = (n * r, c) if axis == 0 else (r, n * c)
        self.per = r if axis == 0 else c

    def spec(self, tr, tc, rc_of):
        l = self.layer
        if self.axis == 0:
            n = self.per // tr

            def imap(*g):
                bi, bj = rc_of(*g)
                return (bi // n, l, bi % n, bj)
        else:
            n = self.per // tc

            def imap(*g):
                bi, bj = rc_of(*g)
                return (bj // n, l, bi, bj % n)
        return pl.BlockSpec((None, None, tr, tc), imap)


def _spec2(tr, tc, rc_of):
    return pl.BlockSpec((tr, tc), lambda *g: rc_of(*g))


def matmul(a, b, *, ta=False, tb=False, out_dtype=jnp.float32, scale=None, res=None, out_axis=None,
           name, tm=1024, tn=1024, tk=2048):
    am, ak = (a.shape[1], a.shape[0]) if ta else a.shape
    bk, bn = (b.shape[1], b.shape[0]) if tb else b.shape
    assert ak == bk, (name, a.shape, b.shape, ta, tb)
    M, N, K = am, bn, ak

    def per(x, axis):
        return x.per if isinstance(x, Sharded) and x.axis == axis else None

    def pick(dim, pref, *pers):
        return _tile(math.gcd(dim, *[p for p in pers if p is not None]), pref)

    tm = pick(M, tm, per(a, 1 if ta else 0), M // N_CHIPS if out_axis == 0 else None)
    tn = pick(N, tn, per(b, 0 if tb else 1), N // N_CHIPS if out_axis == 1 else None)
    tk = pick(K, tk, per(a, 0 if ta else 1), per(b, 1 if tb else 0))
    assert M % tm == 0 and N % tn == 0 and K % tk == 0, (name, M, N, K, tm, tn, tk)
    nk = K // tk

    a_rc = (lambda i, j, k: (k, i)) if ta else (lambda i, j, k: (i, k))
    b_rc = (lambda i, j, k: (j, k)) if tb else (lambda i, j, k: (k, j))
    a_blk = (tk, tm) if ta else (tm, tk)
    b_blk = (tn, tk) if tb else (tk, tn)
    a_spec = a.spec(*a_blk, a_rc) if isinstance(a, Sharded) else _spec2(*a_blk, a_rc)
    b_spec = b.spec(*b_blk, b_rc) if isinstance(b, Sharded) else _spec2(*b_blk, b_rc)
    dn = (((0 if ta else 1,), (1 if tb else 0,)), ((), ()))
    has_res = res is not None

    def body(*refs):
        if has_res:
            a_ref, b_ref, r_ref, o_ref, acc = refs
        else:
            a_ref, b_ref, o_ref, acc = refs
        k = pl.program_id(2)

        @pl.when(k == 0)
        def _():
            acc[...] = jnp.zeros_like(acc)

        acc[...] += lax.dot_general(a_ref[...].astype(MXU_DTYPE), b_ref[...].astype(MXU_DTYPE), dn,
                                    preferred_element_type=jnp.float32)

        @pl.when(k == nk - 1)
        def _():
            r = acc[...]
            if scale is not None:
                r = r * scale
            if has_res:
                r = r + r_ref[...]
            o_ref[...] = r.astype(o_ref.dtype)

    in_specs = [a_spec, b_spec]
    args = [a.arr if isinstance(a, Sharded) else a, b.arr if isinstance(b, Sharded) else b]
    if has_res:
        in_specs.append(_spec2(tm, tn, lambda i, j, k: (i, j)))
        args.append(res)
    o_rc = lambda i, j, k: (i, j)
    if out_axis is None:
        out_shape = jax.ShapeDtypeStruct((M, N), out_dtype)
        out_spec = _spec2(tm, tn, o_rc)
    else:
        shp = (N_CHIPS, 1, M // N_CHIPS, N) if out_axis == 0 else (N_CHIPS, 1, M, N // N_CHIPS)
        out_shape = jax.ShapeDtypeStruct(shp, out_dtype)
        out_spec = Sharded(out_shape, 0, out_axis).spec(tm, tn, o_rc)
    return pl.pallas_call(
        body, name=name, out_shape=out_shape, grid=(M // tm, N // tn, nk),
        in_specs=in_specs, out_specs=out_spec,
        scratch_shapes=[pltpu.VMEM((tm, tn), jnp.float32)],
        compiler_params=_params("parallel", "parallel", "arbitrary"),
    )(*args)


def _me():
    return lax.axis_index("x"), lax.axis_index("y"), lax.axis_index("c")


def _other_chips(x, y):
    return [(1 - x, y), (x, 1 - y), (1 - x, 1 - y)]


HBM = pl.BlockSpec(memory_space=pltpu.HBM)
SEM = pl.BlockSpec(memory_space=pltpu.SEMAPHORE)
N_PEERS = 3
TOKEN = jax.ShapeDtypeStruct((8, LANES), jnp.float32)


def _split_params():
    return pltpu.CompilerParams(has_side_effects=pltpu.SideEffectType.DATAFLOW_SIDE_EFFECTING)


def _in_hbm(a):
    return pltpu.with_memory_space_constraint(a, pltpu.HBM)


def exchange_start(bufs, plan, after, name):
    nb, n = len(bufs), plan.copies

    def body(*refs):
        sems, token = refs[nb + 1:nb + 1 + 2 * n], refs[-1]
        for j, (src, dst, to) in enumerate(plan(refs[:nb], _me(), False)):
            pltpu.make_async_remote_copy(src_ref=src, dst_ref=dst, send_sem=sems[j], recv_sem=sems[n + j],
                                         device_id=to, device_id_type=MESH).start()
        token[...] = jnp.zeros_like(token)

    outs = pl.pallas_call(
        body, name=name,
        out_shape=(pltpu.SemaphoreType.DMA(()),) * (2 * n) + tuple(pltpu.HBM(b.shape, b.dtype) for b in bufs) + (TOKEN,),
        in_specs=(HBM,) * nb + (ANY,), out_specs=(SEM,) * (2 * n) + (HBM,) * nb + (pl.BlockSpec(memory_space=pltpu.VMEM),),
        input_output_aliases={k: 2 * n + k for k in range(nb)}, compiler_params=_split_params(),
    )(*[_in_hbm(b) for b in bufs], after)
    return outs[:2 * n], outs[2 * n:2 * n + nb], outs[-1]


def exchange_wait(started, plan, after, name):
    sems, bufs, _ = started
    nb, n = len(bufs), plan.copies

    def body(*refs):
        sems_ = refs[nb:nb + 2 * n]
        for j, (src, got, to) in enumerate(plan(refs[:nb], _me(), True)):
            cp = pltpu.make_async_remote_copy(src_ref=src, dst_ref=got, send_sem=sems_[j], recv_sem=sems_[n + j],
                                              device_id=to, device_id_type=MESH)
            cp.wait_send()
            cp.wait_recv()

    return pl.pallas_call(
        body, name=name, out_shape=tuple(pltpu.HBM(b.shape, b.dtype) for b in bufs),
        in_specs=(HBM,) * nb + (SEM,) * (2 * n) + (ANY,), out_specs=(HBM,) * nb,
        input_output_aliases={k: k for k in range(nb)}, compiler_params=_split_params(),
    )(*bufs, *sems, after)


def _plan(copies):
    def mark(f):
        f.copies = copies
        return f
    return mark


@_plan(N_PEERS)
def ag_over_ici(refs, me, arrived):
    (land,), (x, y, cc) = refs, me
    mine = land.at[2 * x + y, :, cc]
    return [(mine, land.at[2 * px + py, :, cc] if arrived else mine, (px, py, cc)) for px, py in _other_chips(x, y)]


@_plan(N_PEERS)
def ag_to_sibling(refs, me, arrived):
    (land,), (x, y, cc) = refs, me
    return [(land.at[2 * px + py, :, cc], land.at[2 * px + py, :, (1 - cc) if arrived else cc], (x, y, 1 - cc))
            for px, py in _other_chips(x, y)]


@_plan(1)
def rs_to_sibling(refs, me, arrived):
    (g, r1), (x, y, cc) = refs, me
    return [(g.at[:, :, 1 - cc], r1, (x, y, 1 - cc))]


@_plan(N_PEERS)
def rs_over_ici(refs, me, arrived):
    (p, land), (x, y, cc) = refs, me
    return [(p.at[2 * px + py], land.at[j], (px, py, cc)) for j, (px, py) in enumerate(_other_chips(x, y))]


@_plan(1)
def rs_gather_sibling(refs, me, arrived):
    (red,), (x, y, cc) = refs, me
    return [(red.at[:, cc], red.at[:, (1 - cc) if arrived else cc], (x, y, 1 - cc))]


def cast_into_shards(w, layer, shard, after, name):
    L, r, c = w.shape
    tr, tc = _tile(r, 1024, 16), _tile(c, 2048)

    def body(shard_ref, w_ref, after_ref, o_ref):
        o_ref[...] = w_ref[...].astype(o_ref.dtype)

    grid_spec = pltpu.PrefetchScalarGridSpec(
        num_scalar_prefetch=1, grid=(r // tr, c // tc),
        in_specs=[pl.BlockSpec((None, tr, tc), lambda i, j, sh: (layer, i, j)), ANY],
        out_specs=pl.BlockSpec((None, None, tr, tc), lambda i, j, sh: (sh[0], 0, i, j)))
    return pl.pallas_call(body, name=name, grid_spec=grid_spec, out_shape=jax.ShapeDtypeStruct((N_CHIPS, 1, r, c), WIRE_DTYPE),
                          compiler_params=_params("parallel", "parallel"))(shard, w, after)


def add_sibling(g, r1, core, name):
    n, L, two, h, c = g.shape
    th = _tile(h, 1024, 16)
    tc = _tile(c, 2048)

    def body(core_ref, g_ref, r_ref, o_ref):
        o_ref[...] = (g_ref[...].astype(jnp.float32) + r_ref[...].astype(jnp.float32)).astype(o_ref.dtype)

    grid_spec = pltpu.PrefetchScalarGridSpec(
        num_scalar_prefetch=1, grid=(n, L, h // th, c // tc),
        in_specs=[pl.BlockSpec((None, None, None, th, tc), lambda s, l, i, j, core: (s, l, core[0], i, j)),
                  pl.BlockSpec((None, None, th, tc), lambda s, l, i, j, core: (s, l, i, j))],
        out_specs=pl.BlockSpec((None, None, th, tc), lambda s, l, i, j, core: (s, l, i, j)))
    return pl.pallas_call(body, name=name, grid_spec=grid_spec, out_shape=jax.ShapeDtypeStruct((n, L, h, c), WIRE_DTYPE),
                          compiler_params=_params("parallel", "parallel", "parallel", "parallel"))(core, g, r1)


def add_chips(p, r2, place, name):
    n, L, h, c = p.shape
    th = _tile(h, 1024, 16)
    tc = _tile(c, 2048)

    def body(shard_ref, core_ref, p_ref, r_ref, o_ref):
        acc = p_ref[...].astype(jnp.float32)
        for j in range(3):
            acc = acc + r_ref[j].astype(jnp.float32)
        o_ref[...] = acc

    grid_spec = pltpu.PrefetchScalarGridSpec(
        num_scalar_prefetch=2, grid=(L, h // th, c // tc),
        in_specs=[pl.BlockSpec((None, None, th, tc), lambda l, i, j, shard, core: (shard[0], l, i, j)),
                  pl.BlockSpec((3, None, th, tc), lambda l, i, j, shard, core: (0, l, i, j))],
        out_specs=pl.BlockSpec((None, None, th, tc), lambda l, i, j, shard, core: (l, core[0], i, j)))
    return pl.pallas_call(body, name=name, grid_spec=grid_spec, out_shape=_f32((L, 2, h, c)),
                          compiler_params=_params("parallel", "parallel", "parallel"))(*place, p, r2)


class ReduceScatter:
    def __init__(self, place, done):
        self.place, self.done = place, done
        self.stages = [[], [], []]
        self.token = None

    def _start(self, bufs, plan, name):
        started = exchange_start(bufs, plan, self.token if self.token is not None else bufs[0], name)
        self.token = started[2]
        return started

    def _advance(self, stage, after):
        key, started = self.stages[stage].pop(0)
        name = "rs_%s%d" % key
        if stage == 0:
            g, r1 = exchange_wait(started, rs_to_sibling, after, name + "_d2d_wait")
            p = add_sibling(g, r1, self.place[1], name + "_add1")
            land = lax.empty((N_PEERS,) + p.shape[1:], p.dtype)
            self.stages[1].append((key, self._start((p, land), rs_over_ici, name + "_ici_start")))
        elif stage == 1:
            p, r2 = exchange_wait(started, rs_over_ici, after, name + "_ici_wait")
            red = add_chips(p, r2, self.place, name + "_add2")
            self.stages[2].append((key, self._start((red,), rs_gather_sibling, name + "_gather_start")))
        else:
            (red,) = exchange_wait(started, rs_gather_sibling, after, name + "_gather_wait")
            L, two, h, c = red.shape
            self.done(key, red.reshape(2 * h, c))

    def push(self, key, g4):
        n, L, r, c = g4.shape
        g = g4.reshape(n, L, 2, r // 2, c)
        r1 = lax.empty((n, L, r // 2, c), g.dtype)
        self.stages[0].append((key, self._start((g, r1), rs_to_sibling, "rs_%s%d_d2d_start" % key)))
        for stage, depth in ((2, 1), (1, RS_WINDOW), (0, 1)):
            if len(self.stages[stage]) > depth:
                self._advance(stage, self.token)

    def drain(self, after):
        for stage in (0, 1, 2):
            while self.stages[stage]:
                self._advance(stage, after)


def all_reduce_small(v, name):
    R, C = v.shape

    def body(v_ref, o_ref, buf, send_sems, recv_sems):
        x, y, cc = _me()
        buf[0] = v_ref[...]
        cps = []
        for k in range(1, 8):
            dx, dy, dc = (k >> 2) & 1, (k >> 1) & 1, k & 1
            to = (x ^ dx, y ^ dy, cc ^ dc)
            cp = pltpu.make_async_remote_copy(src_ref=v_ref, dst_ref=buf.at[k], send_sem=send_sems.at[k],
                                              recv_sem=recv_sems.at[k], device_id=to, device_id_type=MESH)
            cp.start()
            cps.append(cp)
        for cp in cps:
            cp.wait()
        me = 4 * x + 2 * y + cc
        acc = buf[me]
        for a in range(1, 8):
            acc = acc + buf[a ^ me]
        o_ref[...] = acc

    vm = pl.BlockSpec(memory_space=pltpu.VMEM)
    return pl.pallas_call(
        body, name=name, out_shape=_f32((R, C)), in_specs=[vm], out_specs=vm,
        scratch_shapes=[pltpu.VMEM((8, R, C), jnp.float32), pltpu.SemaphoreType.DMA((8,)), pltpu.SemaphoreType.DMA((8,))],
    )(v)


def _rstd(x, n):
    return lax.rsqrt(jnp.sum(x * x, axis=-1, keepdims=True) * (1.0 / n) + EPS)


def _accumulate(ref, part, first):
    @pl.when(first)
    def _():
        ref[...] = part

    @pl.when(jnp.logical_not(first))
    def _():
        ref[...] += part


def rmsnorm_fwd(x, g, name):
    S, D = x.shape
    ts = _tile(S, 256, 8)

    def body(x_ref, g_ref, o_ref):
        xv = x_ref[...]
        o_ref[...] = (xv * _rstd(xv, D) * g_ref[...]).astype(o_ref.dtype)

    return pl.pallas_call(
        body, name=name, out_shape=_act((S, D)), grid=(S // ts,),
        in_specs=[pl.BlockSpec((ts, D), lambda i: (i, 0)), pl.BlockSpec((1, D), lambda i: (0, 0))],
        out_specs=pl.BlockSpec((ts, D), lambda i: (i, 0)), compiler_params=_params("parallel"))(x, g)


def _norm_bwd(x, g, dy, n):
    r = _rstd(x, n)
    xh = x * r
    dxh = dy * g
    dx = r * (dxh - xh * (jnp.sum(dxh * xh, axis=-1, keepdims=True) * (1.0 / n)))
    return dx, dy * xh


def rmsnorm_bwd(x, g, dy, dres, name):
    S, D = x.shape
    ts = _tile(S, 256, 8)

    def body(x_ref, g_ref, dy_ref, dres_ref, dx_ref, dg_ref):
        dx, dgp = _norm_bwd(x_ref[...], g_ref[...], dy_ref[...], D)
        dx_ref[...] = dres_ref[...] + dx
        _accumulate(dg_ref, jnp.sum(dgp, axis=0, keepdims=True), pl.program_id(0) == 0)

    row = pl.BlockSpec((ts, D), lambda i: (i, 0))
    vec = pl.BlockSpec((1, D), lambda i: (0, 0))
    return pl.pallas_call(
        body, name=name, out_shape=(_f32((S, D)), _f32((1, D))), grid=(S // ts,),
        in_specs=[row, vec, row, row], out_specs=(row, vec), compiler_params=_params("arbitrary"))(x, g, dy, dres)


def _sigmoid(x):
    return 1.0 / (1.0 + jnp.exp(-x))


def swiglu_in(xn, w_in, name, tm=512, tn=1408):
    S, D = xn.shape
    F = w_in.shape[1] // 2
    tm, tn = _tile(S, tm, 8), _tile(math.gcd(F, w_in.per), tn)
    nf = F // tn

    def body(x_ref, wg_ref, wu_ref, u_ref, a_ref):
        x = x_ref[...].astype(MXU_DTYPE)
        for cols in _column_chunks(tn):
            gt = jnp.dot(x, wg_ref[:, cols], preferred_element_type=jnp.float32)
            up = jnp.dot(x, wu_ref[:, cols], preferred_element_type=jnp.float32)
            u_ref[0, :, cols] = gt.astype(u_ref.dtype)
            u_ref[1, :, cols] = up.astype(u_ref.dtype)
            a_ref[:, cols] = (gt * _sigmoid(gt) * up).astype(a_ref.dtype)

    return pl.pallas_call(
        body, name=name, out_shape=(_act((2, 1, S, F)), _act((S, F))), grid=(nf, S // tm),
        in_specs=[pl.BlockSpec((tm, D), lambda j, i: (i, 0)), w_in.spec(D, tn, lambda j, i: (0, j)),
                  w_in.spec(D, tn, lambda j, i: (0, j + nf))],
        out_specs=(pl.BlockSpec((2, None, tm, tn), lambda j, i: (0, 0, i, j)), pl.BlockSpec((tm, tn), lambda j, i: (i, j))),
        compiler_params=_params("parallel", "parallel"))(xn, w_in.arr, w_in.arr)


def swiglu_out_bwd(dout, w_out, u, scale, name, tm=512, tn=1408):
    S, D = dout.shape
    F = w_out.shape[0]
    tm, tn = _tile(S, tm, 8), _tile(math.gcd(F, w_out.per), tn)

    def body(d_ref, w_ref, u_ref, o_ref):
        d = d_ref[...].astype(MXU_DTYPE)
        for cols in _column_chunks(tn):
            da = lax.dot_general(d, w_ref[cols, :], NT, preferred_element_type=jnp.float32) * scale
            gt, up = u_ref[0, :, cols].astype(jnp.float32), u_ref[1, :, cols].astype(jnp.float32)
            s = _sigmoid(gt)
            o_ref[0, :, cols] = (da * up * (s * (1.0 + gt * (1.0 - s)))).astype(o_ref.dtype)
            o_ref[1, :, cols] = (da * (gt * s)).astype(o_ref.dtype)

    planes = pl.BlockSpec((2, None, tm, tn), lambda j, i: (0, 0, i, j))
    return pl.pallas_call(
        body, name=name, out_shape=_act((2, 1, S, F)), grid=(F // tn, S // tm),
        in_specs=[pl.BlockSpec((tm, D), lambda j, i: (i, 0)), w_out.spec(tn, D, lambda j, i: (j, 0)), planes],
        out_specs=planes, compiler_params=_params("parallel", "parallel"))(dout, w_out.arr, u)


def loss_head(y, t, name):
    S, D = y.shape
    ts = _tile(S, 256, 8)

    def body(y_ref, t_ref, dy_ref, l_ref):
        e = y_ref[...] - t_ref[...]
        dy_ref[...] = e * (1.0 / D)
        l_ref[...] = jnp.full(l_ref.shape, 0.5 * jnp.sum(jnp.sum(e * e, axis=-1, keepdims=True) * (1.0 / D)), jnp.float32)

    row = pl.BlockSpec((ts, D), lambda i: (i, 0))
    dy, parts = pl.pallas_call(
        body, name=name, out_shape=(_f32((S, D)), _f32((S // ts, 8, LANES))), grid=(S // ts,),
        in_specs=[row, row], out_specs=(row, pl.BlockSpec((None, 8, LANES), lambda i: (i, 0, 0))),
        compiler_params=_params("parallel"))(y, t)
    return jnp.sum(parts[:, 0, 0]), dy


def rope_tables(S):
    inv = 1.0 / (ROPE_THETA ** (jnp.arange(0, ROPE_DIM, 2, dtype=jnp.float32) / ROPE_DIM))
    ang = jnp.arange(S, dtype=jnp.float32)[:, None] * inv[None, :]
    c, s = jnp.cos(ang), jnp.sin(ang)
    z = jnp.zeros_like(c)
    return (jnp.concatenate([c, c, z, z], axis=1), jnp.concatenate([-s, z, z, z], axis=1),
            jnp.concatenate([z, s, z, z], axis=1))


def _rope(x, cos, sa, sb):
    return x * cos + pltpu.roll(x, 96, 1) * sa + pltpu.roll(x, 32, 1) * sb


def _rope_t(d, cos, sa, sb):
    return d * cos + pltpu.roll(d * sa, 32, 1) + pltpu.roll(d * sb, 96, 1)


def _head_norm(x1, x2, g):
    r = lax.rsqrt((jnp.sum(x1 * x1, axis=-1, keepdims=True) + jnp.sum(x2 * x2, axis=-1, keepdims=True)) * (1.0 / QK_DIM) + EPS)
    return x1 * r * g[:, :LANES], x2 * r * g[:, LANES:], r


def _head_norm_bwd(x1, x2, g, d1, d2):
    _, _, r = _head_norm(x1, x2, g)
    h1, h2 = x1 * r, x2 * r
    e1, e2 = d1 * g[:, :LANES], d2 * g[:, LANES:]
    m = (jnp.sum(e1 * h1, axis=-1, keepdims=True) + jnp.sum(e2 * h2, axis=-1, keepdims=True)) * (1.0 / QK_DIM)
    return r * (e1 - h1 * m), r * (e2 - h2 * m), d1 * h1, d2 * h2


def mla_latent_fwd(lat, g_cq, g_ckv, name):
    S, W = lat.shape
    QL, KL = g_cq.shape[1], g_ckv.shape[1]
    ts = _tile(S, 256, 8)

    def body(l_ref, gq_ref, gk_ref, cq_ref, ckv_ref):
        a, b = l_ref[:, :QL], l_ref[:, QL:QL + KL]
        cq_ref[...] = (a * _rstd(a, QL) * gq_ref[...]).astype(cq_ref.dtype)
        ckv_ref[...] = (b * _rstd(b, KL) * gk_ref[...]).astype(ckv_ref.dtype)

    return pl.pallas_call(
        body, name=name, out_shape=(_act((S, QL)), _act((S, KL))), grid=(S // ts,),
        in_specs=[pl.BlockSpec((ts, W), lambda i: (i, 0)), pl.BlockSpec((1, QL), lambda i: (0, 0)),
                  pl.BlockSpec((1, KL), lambda i: (0, 0))],
        out_specs=(pl.BlockSpec((ts, QL), lambda i: (i, 0)), pl.BlockSpec((ts, KL), lambda i: (i, 0))),
        compiler_params=_params("parallel"))(lat, g_cq, g_ckv)


def mla_latent_bwd(dcq, dckv, dkpe, lat, g_cq, g_ckv, name):
    S, W = lat.shape
    QL, KL = g_cq.shape[1], g_ckv.shape[1]
    ts = _tile(S, 256, 8)

    def body(dq_ref, dk_ref, dp_ref, l_ref, gq_ref, gk_ref, o_ref, dgq_ref, dgk_ref):
        first = pl.program_id(0) == 0
        da, ga = _norm_bwd(l_ref[:, :QL], gq_ref[...], dq_ref[...], QL)
        db, gb = _norm_bwd(l_ref[:, QL:QL + KL], gk_ref[...], dk_ref[...], KL)
        o_ref[:, :QL] = da.astype(o_ref.dtype)
        o_ref[:, QL:QL + KL] = db.astype(o_ref.dtype)
        o_ref[:, QL + KL:] = dp_ref[...].astype(o_ref.dtype)
        _accumulate(dgq_ref, jnp.sum(ga, axis=0, keepdims=True), first)
        _accumulate(dgk_ref, jnp.sum(gb, axis=0, keepdims=True), first)

    row = lambda n: pl.BlockSpec((ts, n), lambda i: (i, 0))
    vec = lambda n: pl.BlockSpec((1, n), lambda i: (0, 0))
    return pl.pallas_call(
        body, name=name, out_shape=(_act((S, W)), _f32((1, QL)), _f32((1, KL))), grid=(S // ts,),
        in_specs=[row(QL), row(KL), row(LANES), row(W), vec(QL), vec(KL)], out_specs=(row(W), vec(QL), vec(KL)),
        compiler_params=_params("arbitrary"))(dcq, dckv, dkpe, lat, g_cq, g_ckv)


def mla_q_prep_fwd(qraw, g, tabs, H, name):
    S = qraw.shape[0]
    ts = _tile(S, PREP_ROWS, 8)

    def body(x_ref, g_ref, c_ref, a_ref, b_ref, o_ref):
        y1, y2, _ = _head_norm(x_ref[:, :LANES], x_ref[:, LANES:], g_ref[...])
        o_ref[:, :LANES] = y1.astype(o_ref.dtype)
        o_ref[:, LANES:] = _rope(y2, c_ref[...], a_ref[...], b_ref[...]).astype(o_ref.dtype)

    tab = pl.BlockSpec((ts, LANES), lambda i, h: (i, 0))
    return pl.pallas_call(
        body, name=name, out_shape=_act((H, S, QK_PAD)), grid=(S // ts, H),
        in_specs=[pl.BlockSpec((ts, QK_PAD), lambda i, h: (i, h)), pl.BlockSpec((1, QK_PAD), lambda i, h: (0, 0)), tab, tab, tab],
        out_specs=pl.BlockSpec((None, ts, QK_PAD), lambda i, h: (h, i, 0)),
        compiler_params=_params("parallel", "parallel"))(qraw, g, *tabs)


def mla_q_prep_bwd(dq, qraw, g, tabs, H, name):
    S = qraw.shape[0]
    ts = _tile(S, PREP_ROWS, 8)

    def body(d_ref, x_ref, g_ref, c_ref, a_ref, b_ref, o_ref, dg_ref):
        d2 = _rope_t(d_ref[:, LANES:], c_ref[...], a_ref[...], b_ref[...])
        dx1, dx2, g1, g2 = _head_norm_bwd(x_ref[:, :LANES], x_ref[:, LANES:], g_ref[...], d_ref[:, :LANES], d2)
        o_ref[:, :LANES] = dx1.astype(o_ref.dtype)
        o_ref[:, LANES:] = dx2.astype(o_ref.dtype)
        first = jnp.logical_and(pl.program_id(0) == 0, pl.program_id(1) == 0)
        part = jnp.concatenate([jnp.sum(g1, axis=0, keepdims=True), jnp.sum(g2, axis=0, keepdims=True)], axis=1)
        _accumulate(dg_ref, part, first)

    tab = pl.BlockSpec((ts, LANES), lambda i, h: (i, 0))
    vec = pl.BlockSpec((1, QK_PAD), lambda i, h: (0, 0))
    return pl.pallas_call(
        body, name=name, out_shape=(_act((S, H * QK_PAD)), _f32((1, QK_PAD))), grid=(S // ts, H),
        in_specs=[pl.BlockSpec((None, ts, QK_PAD), lambda i, h: (h, i, 0)), pl.BlockSpec((ts, QK_PAD), lambda i, h: (i, h)),
                  vec, tab, tab, tab],
        out_specs=(pl.BlockSpec((ts, QK_PAD), lambda i, h: (i, h)), vec),
        compiler_params=_params("arbitrary", "arbitrary"))(dq, qraw, g, *tabs)


def mla_k_prep_fwd(kvraw, lat, g, tabs, H, pe_blk, name):
    S = kvraw.shape[0]
    ts = _tile(S, PREP_ROWS, 8)

    def body(x_ref, p_ref, g_ref, c_ref, a_ref, b_ref, k_ref, v_ref):
        y1, y2, _ = _head_norm(x_ref[:, :LANES], p_ref[...], g_ref[...])
        k_ref[:, :LANES] = y1.astype(k_ref.dtype)
        k_ref[:, LANES:] = _rope(y2, c_ref[...], a_ref[...], b_ref[...]).astype(k_ref.dtype)
        v_ref[...] = x_ref[:, LANES:].astype(v_ref.dtype)

    tab = pl.BlockSpec((ts, LANES), lambda i, h: (i, 0))
    return pl.pallas_call(
        body, name=name, out_shape=(_act((H, S, QK_PAD)), _act((H, S, LANES))), grid=(S // ts, H),
        in_specs=[pl.BlockSpec((ts, QK_PAD), lambda i, h: (i, h)), pl.BlockSpec((ts, LANES), lambda i, h: (i, pe_blk)),
                  pl.BlockSpec((1, QK_PAD), lambda i, h: (0, 0)), tab, tab, tab],
        out_specs=(pl.BlockSpec((None, ts, QK_PAD), lambda i, h: (h, i, 0)), pl.BlockSpec((None, ts, LANES), lambda i, h: (h, i, 0))),
        compiler_params=_params("parallel", "parallel"))(kvraw, lat, g, *tabs)


def mla_k_prep_bwd(dk, dv, kvraw, lat, g, tabs, H, pe_blk, name):
    S = kvraw.shape[0]
    ts = _tile(S, PREP_ROWS, 8)

    def body(dk_ref, dv_ref, x_ref, p_ref, g_ref, c_ref, a_ref, b_ref, o_ref, dp_ref, dg_ref):
        i, h = pl.program_id(0), pl.program_id(1)
        d2 = _rope_t(dk_ref[:, LANES:], c_ref[...], a_ref[...], b_ref[...])
        dx1, dx2, g1, g2 = _head_norm_bwd(x_ref[:, :LANES], p_ref[...], g_ref[...], dk_ref[:, :LANES], d2)
        o_ref[:, :LANES] = dx1.astype(o_ref.dtype)
        o_ref[:, LANES:] = dv_ref[...].astype(o_ref.dtype)
        _accumulate(dp_ref, dx2, h == 0)
        part = jnp.concatenate([jnp.sum(g1, axis=0, keepdims=True), jnp.sum(g2, axis=0, keepdims=True)], axis=1)
        _accumulate(dg_ref, part, jnp.logical_and(i == 0, h == 0))

    tab = pl.BlockSpec((ts, LANES), lambda i, h: (i, 0))
    vec = pl.BlockSpec((1, QK_PAD), lambda i, h: (0, 0))
    return pl.pallas_call(
        body, name=name, out_shape=(_act((S, H * QK_PAD)), _f32((S, LANES)), _f32((1, QK_PAD))), grid=(S // ts, H),
        in_specs=[pl.BlockSpec((None, ts, QK_PAD), lambda i, h: (h, i, 0)), pl.BlockSpec((None, ts, LANES), lambda i, h: (h, i, 0)),
                  pl.BlockSpec((ts, QK_PAD), lambda i, h: (i, h)), pl.BlockSpec((ts, LANES), lambda i, h: (i, pe_blk)),
                  vec, tab, tab, tab],
        out_specs=(pl.BlockSpec((ts, QK_PAD), lambda i, h: (i, h)), tab, vec),
        compiler_params=_params("arbitrary", "arbitrary"))(dk, dv, kvraw, lat, g, *tabs)


def _causal_scores(q, k, scale, diagonal):
    s = lax.dot_general(q, k, NT, preferred_element_type=jnp.float32) * scale
    if not diagonal:
        return s
    row = lax.broadcasted_iota(jnp.int32, s.shape, 0)
    col = lax.broadcasted_iota(jnp.int32, s.shape, 1)
    return jnp.where(col <= row, s, NEG)


def _on_causal_blocks(qi, ki, step):
    @pl.when(ki < qi)
    def _():
        step(False)

    @pl.when(ki == qi)
    def _():
        step(True)


def mla_attention_fwd(q, k, v, name):
    H, S, _ = q.shape
    t = _tile(S, ATTN_BLOCK)
    n = S // t
    scale = 1.0 / math.sqrt(QK_DIM)

    def body(q_ref, k_ref, v_ref, o_ref, lse_ref, m_sc, l_sc, acc):
        qi, ki = pl.program_id(1), pl.program_id(2)

        @pl.when(ki == 0)
        def _():
            m_sc[...] = jnp.full(m_sc.shape, NEG, jnp.float32)
            l_sc[...] = jnp.zeros_like(l_sc)
            acc[...] = jnp.zeros_like(acc)

        def step(diagonal):
            s = _causal_scores(q_ref[...], k_ref[...], scale, diagonal)
            m_new = jnp.maximum(m_sc[...], jnp.max(s, axis=-1, keepdims=True))
            alpha = jnp.exp(m_sc[...] - m_new)
            p = jnp.exp(s - m_new)
            l_sc[...] = alpha * l_sc[...] + jnp.sum(p, axis=-1, keepdims=True)
            acc[...] = alpha * acc[...] + jnp.dot(p.astype(MXU_DTYPE), v_ref[...], preferred_element_type=jnp.float32)
            m_sc[...] = m_new

        _on_causal_blocks(qi, ki, step)

        @pl.when(ki == qi)
        def _():
            o_ref[...] = (acc[...] / l_sc[...]).astype(o_ref.dtype)
            lse_ref[...] = m_sc[...] + jnp.log(l_sc[...])

    kv = lambda w: pl.BlockSpec((None, t, w), lambda h, qi, ki: (h, jnp.minimum(ki, qi), 0))
    return pl.pallas_call(
        body, name=name, out_shape=(_act((S, H * LANES)), _f32((H, S, 1))), grid=(H, n, n),
        in_specs=[pl.BlockSpec((None, t, QK_PAD), lambda h, qi, ki: (h, qi, 0)), kv(QK_PAD), kv(LANES)],
        out_specs=(pl.BlockSpec((t, LANES), lambda h, qi, ki: (qi, h)), pl.BlockSpec((None, t, 1), lambda h, qi, ki: (h, qi, 0))),
        scratch_shapes=[pltpu.VMEM((t, 1), jnp.float32), pltpu.VMEM((t, 1), jnp.float32), pltpu.VMEM((t, LANES), jnp.float32)],
        compiler_params=_params("parallel", "parallel", "arbitrary"))(q, k, v)


def mla_attention_bwd(q, k, v, do, o, lse, name):
    H, S, _ = q.shape
    t = _tile(S, ATTN_BLOCK)
    n = S // t
    scale = 1.0 / math.sqrt(QK_DIM)

    def body(q_ref, k_ref, v_ref, do_ref, o_ref, lse_ref, dq_ref, dk_ref, dv_ref, dk_acc, dv_acc):
        ki, qi = pl.program_id(1), pl.program_id(2)
        rows = pl.ds(pl.multiple_of(qi * t, t), t)

        @pl.when(qi == 0)
        def _():
            dk_acc[...] = jnp.zeros_like(dk_acc)
            dv_acc[...] = jnp.zeros_like(dv_acc)

        @pl.when(ki == 0)
        def _():
            dq_ref[rows, :] = jnp.zeros((t, QK_PAD), jnp.float32)

        def step(diagonal):
            p = jnp.exp(_causal_scores(q_ref[...], k_ref[...], scale, diagonal) - lse_ref[...])
            dof = do_ref[...]
            dob = dof.astype(MXU_DTYPE)
            delta = jnp.sum(dof * o_ref[...].astype(jnp.float32), axis=-1, keepdims=True)
            dv_acc[...] += lax.dot_general(p.astype(MXU_DTYPE), dob, TN, preferred_element_type=jnp.float32)
            dp = lax.dot_general(dob, v_ref[...], NT, preferred_element_type=jnp.float32)
            ds = (p * (dp - delta)).astype(MXU_DTYPE)
            dk_acc[...] += lax.dot_general(ds, q_ref[...], TN, preferred_element_type=jnp.float32)
            dq_ref[rows, :] += jnp.dot(ds, k_ref[...], preferred_element_type=jnp.float32) * scale

        _on_causal_blocks(qi, ki, step)

        @pl.when(qi == n - 1)
        def _():
            dk_ref[...] = dk_acc[...] * scale
            dv_ref[...] = dv_acc[...]

    qrow = lambda h, ki, qi: (h, jnp.maximum(qi, ki), 0)
    kv = lambda w: pl.BlockSpec((None, t, w), lambda h, ki, qi: (h, ki, 0))
    col = pl.BlockSpec((None, t, 1), qrow)
    head = pl.BlockSpec((t, LANES), lambda h, ki, qi: (jnp.maximum(qi, ki), h))
    return pl.pallas_call(
        body, name=name, out_shape=(_f32((H, S, QK_PAD)), _f32((H, S, QK_PAD)), _f32((H, S, LANES))), grid=(H, n, n),
        in_specs=[pl.BlockSpec((None, t, QK_PAD), qrow), kv(QK_PAD), kv(LANES),
                  head, head, col],
        out_specs=(pl.BlockSpec((None, S, QK_PAD), lambda h, ki, qi: (h, 0, 0)), kv(QK_PAD), kv(LANES)),
        scratch_shapes=[pltpu.VMEM((t, QK_PAD), jnp.float32), pltpu.VMEM((t, LANES), jnp.float32)],
        compiler_params=_params("parallel", "arbitrary", "arbitrary"))(q, k, v, do, o, lse)


def _alibi_slopes(G, Hd):
    k = np.arange(1, G * Hd + 1, dtype=np.float32)
    s = (2.0 ** (-8.0 * k / (G * Hd))).astype(np.float32).reshape(G, Hd)
    return jnp.asarray(np.broadcast_to(s[:, :, None, None], (G, Hd, 1, LANES)).copy())


def _dil_scores(qn, kn, scale, slope_d, prev, valid):
    s = lax.dot_general(qn, kn, NT, preferred_element_type=jnp.float32) * scale
    iq = lax.broadcasted_iota(jnp.int32, s.shape, 0)
    ik = lax.broadcasted_iota(jnp.int32, s.shape, 1)
    dist = iq - ik + (BLK if prev else 0)
    ok = (ik >= iq) if prev else (ik <= iq)
    s = s - slope_d * dist.astype(jnp.float32)
    return jnp.where(jnp.logical_and(ok, valid), s, NEG)


def _dil_heads(d, Hd, block_bytes):
    hb = max(1, min(Hd, block_bytes // (BLK * LANES * 4))) if d == 1 else 1
    assert Hd % hb == 0, (Hd, hb)
    return hb


def _dil_rows(r, d):
    return pl.ds(r, BLK, stride=d) if d > 1 else slice(None)


def _loop_residues(d, residue, init):
    per = math.gcd(d, 4)

    def one_pass(i, carry):
        for k in range(per):
            carry = residue(per * i + k, carry)
        return carry

    return one_pass(0, init) if d == per else lax.fori_loop(0, d // per, one_pass, init)


def _dil_specs(d, nblk, Hd, G, g, hb):
    def spec(kind, shift):
        col0 = (kind * G + g) * Hd // hb
        return pl.BlockSpec((BLK * d, hb * LANES), lambda n, hg: (jnp.clip(n + shift, 0, nblk - 1), col0 + hg))
    return spec


def _head_spec(d, nblk, hb, shift):
    return pl.BlockSpec((BLK * d, hb * LANES), lambda n, hg: (jnp.clip(n + shift, 0, nblk - 1), hg))


def dilated_fwd(qkv, gq, gk, slopes, g, d, Hd, G, name):
    S, C = qkv.shape
    nblk = S // (BLK * d)
    hb = _dil_heads(d, Hd, 2 << 20)
    scale = 1.0 / math.sqrt(LANES)
    spec = _dil_specs(d, nblk, Hd, G, g, hb)

    def body(q_ref, kc_ref, kp_ref, vc_ref, vp_ref, gq_ref, gk_ref, sl_ref, o_ref, l_ref):
        n = pl.program_id(0)
        nrm = lambda t, gg: (t * _rstd(t, LANES) * gg).astype(MXU_DTYPE)

        def residue(r, carry):
            rows = _dil_rows(r, d)
            for hh in range(hb):
                cols = slice(hh * LANES, (hh + 1) * LANES)
                slope_d = sl_ref[hh][:, :1] * float(d)
                qn = nrm(q_ref[rows, cols], gq_ref[...])
                sc = _dil_scores(qn, nrm(kc_ref[rows, cols], gk_ref[...]), scale, slope_d, False, True)
                sp = _dil_scores(qn, nrm(kp_ref[rows, cols], gk_ref[...]), scale, slope_d, True, n > 0)
                m = jnp.maximum(jnp.max(sc, axis=-1, keepdims=True), jnp.max(sp, axis=-1, keepdims=True))
                lse = m + jnp.log(jnp.sum(jnp.exp(sc - m), axis=-1, keepdims=True) + jnp.sum(jnp.exp(sp - m), axis=-1, keepdims=True))
                o = jnp.dot(jnp.exp(sc - lse).astype(MXU_DTYPE), vc_ref[rows, cols].astype(MXU_DTYPE), preferred_element_type=jnp.float32)
                o = o + jnp.dot(jnp.exp(sp - lse).astype(MXU_DTYPE), vp_ref[rows, cols].astype(MXU_DTYPE), preferred_element_type=jnp.float32)
                o_ref[rows, cols] = o
                l_ref[rows, cols] = jnp.broadcast_to(lse, (BLK, LANES))
            return carry

        _loop_residues(d, residue, 0)

    vec = pl.BlockSpec((1, LANES), lambda n, hg: (0, 0))
    out = _head_spec(d, nblk, hb, 0)
    return pl.pallas_call(
        body, name=name, out_shape=(_f32((S, Hd * LANES)), _f32((S, Hd * LANES))), grid=(nblk, Hd // hb),
        in_specs=[spec(0, 0), spec(1, 0), spec(1, -1), spec(2, 0), spec(2, -1), vec, vec,
                  pl.BlockSpec((None, hb, 1, LANES), lambda n, hg: (g, hg, 0, 0))],
        out_specs=(out, out), compiler_params=_params("parallel", "parallel"),
    )(qkv, qkv, qkv, qkv, qkv, gq, gk, slopes)


def dilated_merge(os_, ls_, name):
    S, W = os_[0].shape
    G = len(os_)
    ts, tw = _tile(S, 512, 8), _tile(W, 512)

    def body(*refs):
        o_refs, l_refs, (o_ref, t_ref) = refs[:G], refs[G:2 * G], refs[2 * G:]
        ls = [r[...] for r in l_refs]
        m = ls[0]
        for l in ls[1:]:
            m = jnp.maximum(m, l)
        es = [jnp.exp(l - m) for l in ls]
        tot = es[0]
        for e in es[1:]:
            tot = tot + e
        acc = o_refs[0][...] * (es[0] / tot)
        for r, e in zip(o_refs[1:], es[1:]):
            acc = acc + r[...] * (e / tot)
        o_ref[...] = acc.astype(o_ref.dtype)
        t_ref[...] = m + jnp.log(tot)

    blk = pl.BlockSpec((ts, tw), lambda i, j: (i, j))
    return pl.pallas_call(
        body, name=name, out_shape=(_act((S, W)), _f32((S, W))), grid=(S // ts, W // tw),
        in_specs=[blk] * (2 * G), out_specs=(blk, blk), compiler_params=_params("parallel", "parallel"))(*os_, *ls_)


def dilated_delta(do, o, name):
    S, W = do.shape
    ts = _tile(S, 512, 8)

    def body(d_ref, o_ref, out_ref):
        out_ref[...] = jnp.broadcast_to(jnp.sum(d_ref[...] * o_ref[...].astype(jnp.float32), axis=-1, keepdims=True), out_ref.shape)

    blk = pl.BlockSpec((ts, LANES), lambda i, h: (i, h))
    return pl.pallas_call(body, name=name, out_shape=_f32((S, W)), grid=(S // ts, W // LANES), in_specs=[blk, blk],
                          out_specs=blk, compiler_params=_params("parallel", "parallel"))(do, o)


def dilated_bwd(qkv, do, lse, delta, gq, gk, slopes, g, d, Hd, G, name):
    S, C = qkv.shape
    nblk = S // (BLK * d)
    W = Hd * LANES
    scale = 1.0 / math.sqrt(LANES)
    hb = _dil_heads(d, Hd, 1 << 20)
    spec = _dil_specs(d, nblk, Hd, G, g, hb)
    hspec = lambda shift: _head_spec(d, nblk, hb, shift)

    def body(q_ref, qx_ref, kc_ref, kp_ref, vc_ref, vp_ref, do_ref, dox_ref, l_ref, lx_ref, dl_ref, dlx_ref,
             gq_ref, gk_ref, sl_ref, dq_ref, dk_ref, dv_ref, dgq_ref, dgk_ref):
        n = pl.program_id(0)
        gqv, gkv = gq_ref[...], gk_ref[...]
        nrm = lambda t, gg: (t * _rstd(t, LANES) * gg).astype(MXU_DTYPE)
        f32dot = lambda a, b, dn: lax.dot_general(a, b, dn, preferred_element_type=jnp.float32)

        def residue(r, carry):
            rows = _dil_rows(r, d)
            dgq_sum, dgk_sum = carry
            for hh in range(hb):
                cols = slice(hh * LANES, (hh + 1) * LANES)
                slope_d = sl_ref[hh][:, :1] * float(d)
                q, kc = q_ref[rows, cols], kc_ref[rows, cols]
                qn, qxn = nrm(q, gqv), nrm(qx_ref[rows, cols], gqv)
                kcn, kpn = nrm(kc, gkv), nrm(kp_ref[rows, cols], gkv)
                vc, vp = vc_ref[rows, cols].astype(MXU_DTYPE), vp_ref[rows, cols].astype(MXU_DTYPE)
                dob, doxb = do_ref[rows, cols].astype(MXU_DTYPE), dox_ref[rows, cols].astype(MXU_DTYPE)
                lrow, lxrow = l_ref[rows, cols][:, :1], lx_ref[rows, cols][:, :1]
                drow, dxrow = dl_ref[rows, cols][:, :1], dlx_ref[rows, cols][:, :1]
                pc = jnp.exp(_dil_scores(qn, kcn, scale, slope_d, False, True) - lrow)
                pp = jnp.exp(_dil_scores(qn, kpn, scale, slope_d, True, n > 0) - lrow)
                dsc = pc * (f32dot(dob, vc, NT) - drow)
                dsp = pp * (f32dot(dob, vp, NT) - drow)
                dqn = (jnp.dot(dsc.astype(MXU_DTYPE), kcn, preferred_element_type=jnp.float32)
                       + jnp.dot(dsp.astype(MXU_DTYPE), kpn, preferred_element_type=jnp.float32)) * scale
                dq, dgq = _norm_bwd(q, gqv, dqn, LANES)
                dq_ref[rows, cols] = dq
                px = jnp.exp(_dil_scores(qxn, kcn, scale, slope_d, True, n < nblk - 1) - lxrow)
                dsx = px * (f32dot(doxb, vc, NT) - dxrow)
                dkn = (f32dot(dsc.astype(MXU_DTYPE), qn, TN) + f32dot(dsx.astype(MXU_DTYPE), qxn, TN)) * scale
                dk, dgk = _norm_bwd(kc, gkv, dkn, LANES)
                dk_ref[rows, cols] = dk
                dv_ref[rows, cols] = f32dot(pc.astype(MXU_DTYPE), dob, TN) + f32dot(px.astype(MXU_DTYPE), doxb, TN)
                dgq_sum = dgq_sum + jnp.sum(dgq, axis=0, keepdims=True)
                dgk_sum = dgk_sum + jnp.sum(dgk, axis=0, keepdims=True)
            return dgq_sum, dgk_sum

        zero = jnp.zeros((1, LANES), jnp.float32)
        dgq_sum, dgk_sum = _loop_residues(d, residue, (zero, zero))
        first = jnp.logical_and(n == 0, pl.program_id(1) == 0)
        _accumulate(dgq_ref, dgq_sum, first)
        _accumulate(dgk_ref, dgk_sum, first)

    vec = pl.BlockSpec((1, LANES), lambda n, hg: (0, 0))
    out = hspec(0)
    return pl.pallas_call(
        body, name=name, out_shape=(_f32((S, W)), _f32((S, W)), _f32((S, W)), _f32((1, LANES)), _f32((1, LANES))),
        grid=(nblk, Hd // hb),
        in_specs=[spec(0, 0), spec(0, 1), spec(1, 0), spec(1, -1), spec(2, 0), spec(2, -1), hspec(0), hspec(1), hspec(0), hspec(1),
                  hspec(0), hspec(1), vec, vec, pl.BlockSpec((None, hb, 1, LANES), lambda n, hg: (g, hg, 0, 0))],
        out_specs=(out, out, out, vec, vec), compiler_params=_params("arbitrary", "arbitrary"),
    )(qkv, qkv, qkv, qkv, qkv, qkv, do, do, lse, lse, delta, delta, gq, gk, slopes)


def adamw(w, g, m, v, layer, prev, name):
    L, r, c = w.shape
    tr, tc = _tile(r, 512, 8), _tile(c, 1024)
    c1 = 1.0 / (1.0 - ADAM_B1 ** ADAM_STEP)
    c2 = 1.0 / (1.0 - ADAM_B2 ** ADAM_STEP)

    def body(*refs):
        w_ref, g_ref, m_ref, v_ref = refs[:4]
        go_ref, d_ref, mo_ref, vo_ref = refs[-4:]
        gv = g_ref[...]
        mn = ADAM_B1 * m_ref[...] + (1.0 - ADAM_B1) * gv
        vn = ADAM_B2 * v_ref[...] + (1.0 - ADAM_B2) * (gv * gv)
        go_ref[...] = gv
        d_ref[...] = -ADAM_LR * ((mn * c1) / (jnp.sqrt(vn * c2) + ADAM_EPS) + ADAM_WD * w_ref[...])
        mo_ref[...] = mn
        vo_ref[...] = vn

    lay = pl.BlockSpec((None, tr, tc), lambda i, j: (layer, i, j))
    flat = pl.BlockSpec((tr, tc), lambda i, j: (i, j))
    ins = [w, g, m, v] + (list(prev) if prev is not None else [])
    in_specs = [lay, flat, lay, lay] + ([ANY] * 4 if prev is not None else [])
    return pl.pallas_call(
        body, name=name, out_shape=tuple(_f32((L, r, c)) for _ in range(4)), grid=(r // tr, c // tc),
        in_specs=in_specs, out_specs=(lay, lay, lay, lay),
        input_output_aliases=({4 + k: k for k in range(4)} if prev is not None else {}),
        compiler_params=_params("parallel", "parallel"))(*ins)


def _ffn_fwd(h, g, W, kind, tag):
    xn = rmsnorm_fwd(h, g, tag + "_norm")
    u, a = swiglu_in(xn, W(kind + "_w_in", h), tag + "_in")
    out = matmul(a, W(kind + "_w_out", a), scale=0.5, res=h, name=tag + "_out", tk=2816)
    return out, (h, xn, u, a)


def _ffn_bwd(dout, saved, g, W, emit, kind, tag):
    h, xn, u, a = saved
    emit(kind + "_w_out", matmul(a, dout, ta=True, scale=0.5, out_dtype=WIRE_DTYPE, out_axis=0, name=tag + "_dwout", tm=1408, tk=2048))
    du = Sharded(swiglu_out_bwd(dout, W(kind + "_w_out", None), u, 0.5, tag + "_da"), 0, 1)
    emit(kind + "_w_in", matmul(xn, du, ta=True, out_dtype=WIRE_DTYPE, out_axis=1, name=tag + "_dwin", tn=1408, tk=2048))
    dxn = matmul(du, W(kind + "_w_in", None), tb=True, name=tag + "_dxn", tk=2816)
    return rmsnorm_bwd(h, g, dxn, dout, tag + "_dnorm")


def _pad_gain(g):
    return jnp.pad(g, ((0, 0), (0, QK_PAD - QK_DIM)))


def _mla_fwd(h, P, W, tabs, H):
    g_mix, g_cq, g_ckv = P["mix_norm"][0:1], P["mla_g_cq"], P["mla_g_ckv"]
    pe_blk = (g_cq.shape[1] + g_ckv.shape[1]) // LANES
    xn = rmsnorm_fwd(h, g_mix, "mla_norm")
    w_down = W("mla_w_down", h)
    lat = matmul(xn, w_down, name="mla_down", tn=w_down.shape[1])
    cq, ckv = mla_latent_fwd(lat, g_cq, g_ckv, "mla_latent")
    qraw = matmul(cq, W("mla_w_uq", lat), name="mla_uq")
    kvraw = matmul(ckv, W("mla_w_ukv", qraw), name="mla_ukv")
    q = mla_q_prep_fwd(qraw, _pad_gain(P["mla_g_qn"]), tabs, H, "mla_qprep")
    k, v = mla_k_prep_fwd(kvraw, lat, _pad_gain(P["mla_g_kn"]), tabs, H, pe_blk, "mla_kprep")
    o, lse = mla_attention_fwd(q, k, v, "mla_attn")
    out = matmul(o, W("mla_w_o", lse), res=h, name="mla_o")
    return out, (h, xn, lat, cq, ckv, qraw, kvraw, q, k, v, o, lse, pe_blk)


def _mla_bwd(dout, saved, P, W, emit, tabs, H):
    h, xn, lat, cq, ckv, qraw, kvraw, q, k, v, o, lse, pe_blk = saved
    emit("mla_w_o", matmul(o, dout, ta=True, out_dtype=WIRE_DTYPE, out_axis=0, name="mla_dwo", tm=512))
    do = matmul(dout, W("mla_w_o", None), tb=True, name="mla_do", tn=512)
    dq, dk, dv = mla_attention_bwd(q, k, v, do, o, lse, "mla_attn_bwd")
    dqraw, dgq = mla_q_prep_bwd(dq, qraw, _pad_gain(P["mla_g_qn"]), tabs, H, "mla_dqprep")
    dkvraw, dkpe, dgk = mla_k_prep_bwd(dk, dv, kvraw, lat, _pad_gain(P["mla_g_kn"]), tabs, H, pe_blk, "mla_dkprep")
    emit("mla_w_uq", matmul(cq, dqraw, ta=True, out_dtype=WIRE_DTYPE, out_axis=1, name="mla_dwuq"))
    dcq = matmul(dqraw, W("mla_w_uq", None), tb=True, name="mla_dcq", tk=1024)
    emit("mla_w_ukv", matmul(ckv, dkvraw, ta=True, out_dtype=WIRE_DTYPE, out_axis=1, name="mla_dwukv"))
    dckv = matmul(dkvraw, W("mla_w_ukv", None), tb=True, name="mla_dckv", tk=1024)
    dlat, dgcq, dgckv = mla_latent_bwd(dcq, dckv, dkpe, lat, P["mla_g_cq"], P["mla_g_ckv"], "mla_dlatent")
    emit("mla_w_down", matmul(xn, dlat, ta=True, out_dtype=WIRE_DTYPE, out_axis=0, name="mla_dwdown", tm=512, tn=dlat.shape[1]))
    dxn = matmul(dlat, W("mla_w_down", None), tb=True, name="mla_dxn", tn=512, tk=dlat.shape[1])
    dh, dgm = rmsnorm_bwd(h, P["mix_norm"][0:1], dxn, dout, "mla_dnorm")
    return dh, dgm, dict(mla_g_qn=dgq[:, :QK_DIM], mla_g_kn=dgk[:, :QK_DIM], mla_g_cq=dgcq, mla_g_ckv=dgckv)


def _dil_fwd(h, P, W, slopes, Hd):
    G = len(DIL_PAIRS)
    xn = rmsnorm_fwd(h, P["mix_norm"][1:2], "dil_norm")
    qkv = matmul(xn, W("dil_w_qkv", h), name="dil_qkv", tn=1152)
    os_, ls_ = [], []
    for g, (_, d) in enumerate(DIL_PAIRS):
        o_g, l_g = dilated_fwd(qkv, P["dil_g_qn"], P["dil_g_kn"], slopes, g, d, Hd, G, f"dil_attn{g}")
        os_.append(o_g)
        ls_.append(l_g)
    o, lse = dilated_merge(os_, ls_, "dil_merge")
    out = matmul(o, W("dil_w_o", lse), res=h, name="dil_o", tn=512)
    return out, (h, xn, qkv, o, lse)


def _dil_bwd(dout, saved, P, W, emit, slopes, Hd):
    h, xn, qkv, o, lse = saved
    ngrp = len(DIL_PAIRS)
    emit("dil_w_o", matmul(o, dout, ta=True, out_dtype=WIRE_DTYPE, out_axis=1, name="dil_dwo", tn=512))
    do = matmul(dout, W("dil_w_o", None), tb=True, name="dil_do", tk=512)
    delta = dilated_delta(do, o, "dil_delta")
    parts = [dilated_bwd(qkv, do, lse, delta, P["dil_g_qn"], P["dil_g_kn"], slopes, g, d, Hd, ngrp, f"dil_dattn{g}")
             for g, (_, d) in enumerate(DIL_PAIRS)]
    dqkv = jnp.concatenate([p[kind] for kind in range(3) for p in parts], axis=1).astype(MXU_DTYPE)
    emit("dil_w_qkv", matmul(xn, dqkv, ta=True, out_dtype=WIRE_DTYPE, out_axis=1, name="dil_dwqkv", tn=1152))
    dxn = matmul(dqkv, W("dil_w_qkv", None), tb=True, name="dil_dxn", tk=2304)
    dh, dgm = rmsnorm_bwd(h, P["mix_norm"][1:2], dxn, dout, "dil_dnorm")
    return dh, dgm, dict(dil_g_qn=parts[0][3] + parts[1][3] + parts[2][3], dil_g_kn=parts[0][4] + parts[1][4] + parts[2][4])


def local_step(x, target, P, get_w, on_grad):
    S, D = x.shape
    H, Hd = MLA_HEADS, DIL_HEADS
    tabs = rope_tables(S)
    slopes = _alibi_slopes(len(DIL_PAIRS), Hd)
    cache = {}

    def weights_of(layer):
        def W(name, after):
            if (name, layer) not in cache:
                cache[name, layer] = Sharded(get_w(name, layer, after), 0, SHARD_AXIS[name])
            return cache[name, layer]
        return W

    row = lambda name, i: P[name][i:i + 1]
    h = x
    saved = []
    for i in range(2):
        W = weights_of(i)
        h, s1 = _ffn_fwd(h, row("ffn1_norm", i), W, "ffn1", f"l{i}_ffn1")
        h, sm = _mla_fwd(h, P, weights_of(0), tabs, H) if i == 0 else _dil_fwd(h, P, weights_of(0), slopes, Hd)
        h, s2 = _ffn_fwd(h, row("ffn2_norm", i), W, "ffn2", f"l{i}_ffn2")
        saved.append((s1, sm, s2))
    loss, dh = loss_head(h, target, "loss")

    gs = {n: [None, None] for n in ("ffn1_norm", "mix_norm", "ffn2_norm")}
    for i in (1, 0):
        s1, sm, s2 = saved[i]
        W = weights_of(i)
        emit = lambda name, g4, layer=i: on_grad(name, layer, g4)
        emit0 = lambda name, g4: on_grad(name, 0, g4)
        dh, gs["ffn2_norm"][i] = _ffn_bwd(dh, s2, row("ffn2_norm", i), W, emit, "ffn2", f"l{i}_ffn2")
        if i == 0:
            dh, gs["mix_norm"][i], gm = _mla_bwd(dh, sm, P, weights_of(0), emit0, tabs, H)
        else:
            dh, gs["mix_norm"][i], gm = _dil_bwd(dh, sm, P, weights_of(0), emit0, slopes, Hd)
        gs.update({n: [val] for n, val in gm.items()})
        dh, gs["ffn1_norm"][i] = _ffn_bwd(dh, s1, row("ffn1_norm", i), W, emit, "ffn1", f"l{i}_ffn1")
    gsmall = {n: jnp.concatenate(v, axis=0) for n, v in gs.items()}
    return loss, dh, gsmall


def _pad_heads(w, real, padded):
    lead, n = w.shape[:-1], w.shape[-1] // real
    w = jnp.pad(w.reshape(*lead, n, real), [(0, 0)] * (len(lead) + 1) + [(0, padded - real)])
    return w.reshape(*lead, n * padded)


def _unpad_heads(w, real, padded):
    lead, n = w.shape[:-1], w.shape[-1] // padded
    return w.reshape(*lead, n, padded)[..., :real].reshape(*lead, n * real)


def _pack_small(gs):
    flat = jnp.concatenate([gs[n].reshape(-1) for n in SMALL])
    rows = -(-flat.shape[0] // LANES)
    rows = -(-rows // 8) * 8
    return jnp.pad(flat, (0, rows * LANES - flat.shape[0])).reshape(rows, LANES)


def _unpack_small(packed, like):
    flat, out, off = packed.reshape(-1), {}, 0
    for n in SMALL:
        size = int(np.prod(like[n].shape))
        out[n] = flat[off:off + size].reshape(like[n].shape)
        off += size
    return out


def kernel(x, ffn1_norm, ffn1_w_in, ffn1_w_out, mix_norm, ffn2_norm, ffn2_w_in, ffn2_w_out, mla_w_down, mla_g_cq, mla_g_ckv, mla_w_uq, mla_w_ukv, mla_g_qn, mla_g_kn, mla_w_o, dil_w_qkv, dil_g_qn, dil_g_kn, dil_w_o, loss_target, m_ffn1_norm, m_ffn1_w_in, m_ffn1_w_out, m_mix_norm, m_ffn2_norm, m_ffn2_w_in, m_ffn2_w_out, m_mla_w_down, m_mla_g_cq, m_mla_g_ckv, m_mla_w_uq, m_mla_w_ukv, m_mla_g_qn, m_mla_g_kn, m_mla_w_o, m_dil_w_qkv, m_dil_g_qn, m_dil_g_kn, m_dil_w_o, v_ffn1_norm, v_ffn1_w_in, v_ffn1_w_out, v_mix_norm, v_ffn2_norm, v_ffn2_w_in, v_ffn2_w_out, v_mla_w_down, v_mla_g_cq, v_mla_g_ckv, v_mla_w_uq, v_mla_w_ukv, v_mla_g_qn, v_mla_g_kn, v_mla_w_o, v_dil_w_qkv, v_dil_g_qn, v_dil_g_kn, v_dil_w_o):
    args = dict(locals())
    w = {n: args[n] for n in WEIGHTS}
    m = {n: args["m_" + n] for n in WEIGHTS}
    v = {n: args["v_" + n] for n in WEIGHTS}
    cx, cy, cc = _me()
    core = jnp.reshape(cc, (1,)).astype(jnp.int32)
    shard = jnp.reshape(2 * cx + cy, (1,)).astype(jnp.int32)
    place = (shard, core)
    pe_pad = LANES - ROPE_DIM

    order = [(n, layer if w[n].shape[0] > 1 else 0) for layer in range(2) for n in USE_ORDER[layer]]
    lands, over_ici, to_sibling = {}, {}, {}

    def cast(key, after):
        _, r, c = w[key[0]].shape
        lands[key] = cast_into_shards(w[key[0]], key[1], shard, after, "ag_%s%d_cast" % key).reshape(N_CHIPS, 1, 2, r // 2, c)
        return lands[key]

    def start(key, after):
        over_ici[key] = exchange_start((lands.pop(key),), ag_over_ici, after, "ag_%s%d_start" % key)
        return over_ici[key][2]

    def pass_on(key, after):
        if key in to_sibling:
            return after
        bufs = exchange_wait(over_ici[key], ag_over_ici, after, "ag_%s%d_wait" % key)
        to_sibling[key] = exchange_start(bufs, ag_to_sibling, after, "ag_%s%d_pass" % key)
        return to_sibling[key][2]

    def get_w(n, l, after):
        k = order.index((n, l))
        behind = after
        if k == 0:
            for key in order[:2]:
                cast(key, after)
            for key in order[:2]:
                behind = start(key, behind)
            for key in order[2:]:
                behind = cast(key, behind)
        else:
            for key in order[len(over_ici):min(k + AG_AHEAD, len(order) - 1) + 1]:
                behind = start(key, behind)
        behind = pass_on((n, l), behind)
        if 0 < k < len(order) - 1:
            behind = pass_on(order[k + 1], behind)
        (land,) = exchange_wait(to_sibling[n, l], ag_to_sibling, behind, f"ag_{n}{l}_passed")
        _, _, _, h, c = land.shape
        full = land.reshape(N_CHIPS, 1, 2 * h, c)
        if n == "mla_w_down":
            full = jnp.pad(full, ((0, 0), (0, 0), (0, 0), (0, pe_pad)))
        if n == "mla_w_uq":
            full = _pad_heads(full, QK_DIM, QK_PAD)
        return full

    outs = {n: None for n in BIG}

    def update(key, g):
        n, l = key
        outs[n] = adamw(w[n], g, m[n], v[n], l, outs[n], f"adamw_{n}{l}")

    reduce_scatter = ReduceScatter(place, update)

    def on_grad(n, l, g4):
        if n == "mla_w_down":
            g4 = g4[..., :g4.shape[-1] - pe_pad]
        if n == "mla_w_uq":
            g4 = _unpad_heads(g4, QK_DIM, QK_PAD)
        reduce_scatter.push((n, l), g4)

    loss, grad_x, gsmall = local_step(x[0], loss_target[0], {n: w[n] for n in SMALL}, get_w, on_grad)
    loss = lax.psum(loss, ("x", "y", "c"))
    reduce_scatter.drain(grad_x)
    small = _unpack_small(all_reduce_small(_pack_small(gsmall), "ar_small"), gsmall)
    for n in SMALL:
        outs[n] = tuple(o[0] for o in adamw(w[n][None], small[n], m[n][None], v[n][None], 0, None, f"adamw_{n}"))

    return (loss, grad_x[None], *[outs[n][0] for n in WEIGHTS], *[outs[n][1] for n in WEIGHTS],
            *[outs[n][2] for n in WEIGHTS], *[outs[n][3] for n in WEIGHTS])
```

```python
import math

import numpy as np
import jax
import jax.numpy as jnp
from jax import lax
from jax.experimental import pallas as pl
from jax.experimental.pallas import tpu as pltpu

MXU_DTYPE = jnp.bfloat16
WIRE_DTYPE = jnp.bfloat16
EPS = 1e-6
NEG = -1e30
N_CHIPS = 4
MESH = pl.DeviceIdType.MESH
ANY = pl.BlockSpec(memory_space=pl.ANY)
LANES = 128

MLA_HEADS = 16
NOPE_DIM = 128
ROPE_DIM = 64
QK_DIM = NOPE_DIM + ROPE_DIM
QK_PAD = 2 * LANES
PREP_ROWS = 1024
ATTN_BLOCK = 1024
AG_AHEAD = 3
RS_WINDOW = 2
ROPE_THETA = 10000.0
DIL_PAIRS = ((128, 1), (512, 4), (2048, 16))
DIL_HEADS = 8
BLK = 128

ADAM_LR = 0.001
ADAM_B1 = 0.9
ADAM_B2 = 0.999
ADAM_EPS = 1e-08
ADAM_WD = 0.01
ADAM_STEP = 10

NT = (((1,), (1,)), ((), ()))
TN = (((0,), (0,)), ((), ()))

SHARD_AXIS = {"ffn1_w_in": 1, "ffn1_w_out": 0, "ffn2_w_in": 1, "ffn2_w_out": 0, "mla_w_down": 0, "mla_w_uq": 1,
              "mla_w_ukv": 1, "mla_w_o": 0, "dil_w_qkv": 1, "dil_w_o": 1}
BIG = tuple(SHARD_AXIS)
USE_ORDER = (("ffn1_w_in", "ffn1_w_out", "mla_w_down", "mla_w_uq", "mla_w_ukv", "mla_w_o", "ffn2_w_in", "ffn2_w_out"),
             ("ffn1_w_in", "ffn1_w_out", "dil_w_qkv", "dil_w_o", "ffn2_w_in", "ffn2_w_out"))
SMALL = ("ffn1_norm", "mix_norm", "ffn2_norm", "mla_g_cq", "mla_g_ckv", "mla_g_qn", "mla_g_kn", "dil_g_qn", "dil_g_kn")
WEIGHTS = ("ffn1_norm", "ffn1_w_in", "ffn1_w_out", "mix_norm", "ffn2_norm", "ffn2_w_in", "ffn2_w_out", "mla_w_down",
           "mla_g_cq", "mla_g_ckv", "mla_w_uq", "mla_w_ukv", "mla_g_qn", "mla_g_kn", "mla_w_o", "dil_w_qkv", "dil_g_qn",
           "dil_g_kn", "dil_w_o")


def _tile(dim, pref, mult=LANES):
    if dim <= pref:
        return dim
    t = (pref // mult) * mult
    while t >= mult:
        if dim % t == 0:
            return t
        t -= mult
    return dim


def _column_chunks(width, parts=2):
    tiles = width // LANES
    assert width % LANES == 0, width
    bounds = [LANES * ((tiles * k + parts - 1) // parts) for k in range(parts + 1)]
    return [slice(lo, hi) for lo, hi in zip(bounds, bounds[1:]) if hi > lo]


def _params(*sem):
    return pltpu.CompilerParams(dimension_semantics=sem)


def _f32(shape):
    return jax.ShapeDtypeStruct(shape, jnp.float32)


def _act(shape):
    return jax.ShapeDtypeStruct(shape, MXU_DTYPE)


class Sharded:
    def __init__(self, arr, layer, axis):
        self.arr, self.layer, self.axis = arr, layer, axis
        n, _, r, c = arr.shape
        self.shape = (n * r, c) if axis == 0 else (r, n * c)
        self.per = r if axis == 0 else c

    def spec(self, tr, tc, rc_of):
        l = self.layer
        if self.axis == 0:
            n = self.per // tr

            def imap(*g):
                bi, bj = rc_of(*g)
                return (bi // n, l, bi % n, bj)
        else:
            n = self.per // tc

            def imap(*g):
                bi, bj = rc_of(*g)
                return (bj // n, l, bi, bj % n)
        return pl.BlockSpec((None, None, tr, tc), imap)


def _spec2(tr, tc, rc_of):
    return pl.BlockSpec((tr, tc), lambda *g: rc_of(*g))


def matmul(a, b, *, ta=False, tb=False, out_dtype=jnp.float32, scale=None, res=None, out_axis=None,
           name, tm=1024, tn=1024, tk=2048):
    am, ak = (a.shape[1], a.shape[0]) if ta else a.shape
    bk, bn = (b.shape[1], b.shape[0]) if tb else b.shape
    assert ak == bk, (name, a.shape, b.shape, ta, tb)
    M, N, K = am, bn, ak

    def per(x, axis):
        return x.per if isinstance(x, Sharded) and x.axis == axis else None

    def pick(dim, pref, *pers):
        return _tile(math.gcd(dim, *[p for p in pers if p is not None]), pref)

    tm = pick(M, tm, per(a, 1 if ta else 0), M // N_CHIPS if out_axis == 0 else None)
    tn = pick(N, tn, per(b, 0 if tb else 1), N // N_CHIPS if out_axis == 1 else None)
    tk = pick(K, tk, per(a, 0 if ta else 1), per(b, 1 if tb else 0))
    assert M % tm == 0 and N % tn == 0 and K % tk == 0, (name, M, N, K, tm, tn, tk)
    nk = K // tk

    a_rc = (lambda i, j, k: (k, i)) if ta else (lambda i, j, k: (i, k))
    b_rc = (lambda i, j, k: (j, k)) if tb else (lambda i, j, k: (k, j))
    a_blk = (tk, tm) if ta else (tm, tk)
    b_blk = (tn, tk) if tb else (tk, tn)
    a_spec = a.spec(*a_blk, a_rc) if isinstance(a, Sharded) else _spec2(*a_blk, a_rc)
    b_spec = b.spec(*b_blk, b_rc) if isinstance(b, Sharded) else _spec2(*b_blk, b_rc)
    dn = (((0 if ta else 1,), (1 if tb else 0,)), ((), ()))
    has_res = res is not None

    def body(*refs):
        if has_res:
            a_ref, b_ref, r_ref, o_ref, acc = refs
        else:
            a_ref, b_ref, o_ref, acc = refs
        k = pl.program_id(2)

        @pl.when(k == 0)
        def _():
            acc[...] = jnp.zeros_like(acc)

        acc[...] += lax.dot_general(a_ref[...].astype(MXU_DTYPE), b_ref[...].astype(MXU_DTYPE), dn,
                                    preferred_element_type=jnp.float32)

        @pl.when(k == nk - 1)
        def _():
            r = acc[...]
            if scale is not None:
                r = r * scale
            if has_res:
                r = r + r_ref[...]
            o_ref[...] = r.astype(o_ref.dtype)

    in_specs = [a_spec, b_spec]
    args = [a.arr if isinstance(a, Sharded) else a, b.arr if isinstance(b, Sharded) else b]
    if has_res:
        in_specs.append(_spec2(tm, tn, lambda i, j, k: (i, j)))
        args.append(res)
    o_rc = lambda i, j, k: (i, j)
    if out_axis is None:
        out_shape = jax.ShapeDtypeStruct((M, N), out_dtype)
        out_spec = _spec2(tm, tn, o_rc)
    else:
        shp = (N_CHIPS, 1, M // N_CHIPS, N) if out_axis == 0 else (N_CHIPS, 1, M, N // N_CHIPS)
        out_shape = jax.ShapeDtypeStruct(shp, out_dtype)
        out_spec = Sharded(out_shape, 0, out_axis).spec(tm, tn, o_rc)
    return pl.pallas_call(
        body, name=name, out_shape=out_shape, grid=(M // tm, N // tn, nk),
        in_specs=in_specs, out_specs=out_spec,
        scratch_shapes=[pltpu.VMEM((tm, tn), jnp.float32)],
        compiler_params=_params("parallel", "parallel", "arbitrary"),
    )(*args)


def _me():
    return lax.axis_index("x"), lax.axis_index("y"), lax.axis_index("c")


def _other_chips(x, y):
    return [(1 - x, y), (x, 1 - y), (1 - x, 1 - y)]


HBM = pl.BlockSpec(memory_space=pltpu.HBM)
SEM = pl.BlockSpec(memory_space=pltpu.SEMAPHORE)
N_PEERS = 3
TOKEN = jax.ShapeDtypeStruct((8, LANES), jnp.float32)


def _split_params():
    return pltpu.CompilerParams(has_side_effects=pltpu.SideEffectType.DATAFLOW_SIDE_EFFECTING)


def _in_hbm(a):
    return pltpu.with_memory_space_constraint(a, pltpu.HBM)


def exchange_start(bufs, plan, after, name):
    nb, n = len(bufs), plan.copies

    def body(*refs):
        sems, token = refs[nb + 1:nb + 1 + 2 * n], refs[-1]
        for j, (src, dst, to) in enumerate(plan(refs[:nb], _me(), False)):
            pltpu.make_async_remote_copy(src_ref=src, dst_ref=dst, send_sem=sems[j], recv_sem=sems[n + j],
                                         device_id=to, device_id_type=MESH).start()
        token[...] = jnp.zeros_like(token)

    outs = pl.pallas_call(
        body, name=name,
        out_shape=(pltpu.SemaphoreType.DMA(()),) * (2 * n) + tuple(pltpu.HBM(b.shape, b.dtype) for b in bufs) + (TOKEN,),
        in_specs=(HBM,) * nb + (ANY,), out_specs=(SEM,) * (2 * n) + (HBM,) * nb + (pl.BlockSpec(memory_space=pltpu.VMEM),),
        input_output_aliases={k: 2 * n + k for k in range(nb)}, compiler_params=_split_params(),
    )(*[_in_hbm(b) for b in bufs], after)
    return outs[:2 * n], outs[2 * n:2 * n + nb], outs[-1]


def exchange_wait(started, plan, after, name):
    sems, bufs, _ = started
    nb, n = len(bufs), plan.copies

    def body(*refs):
        sems_ = refs[nb:nb + 2 * n]
        for j, (src, got, to) in enumerate(plan(refs[:nb], _me(), True)):
            cp = pltpu.make_async_remote_copy(src_ref=src, dst_ref=got, send_sem=sems_[j], recv_sem=sems_[n + j],
                                              device_id=to, device_id_type=MESH)
            cp.wait_send()
            cp.wait_recv()

    return pl.pallas_call(
        body, name=name, out_shape=tuple(pltpu.HBM(b.shape, b.dtype) for b in bufs),
        in_specs=(HBM,) * nb + (SEM,) * (2 * n) + (ANY,), out_specs=(HBM,) * nb,
        input_output_aliases={k: k for k in range(nb)}, compiler_params=_split_params(),
    )(*bufs, *sems, after)


def _plan(copies):
    def mark(f):
        f.copies = copies
        return f
    return mark


@_plan(N_PEERS)
def ag_over_ici(refs, me, arrived):
    (land,), (x, y, cc) = refs, me
    mine = land.at[2 * x + y, :, cc]
    return [(mine, land.at[2 * px + py, :, cc] if arrived else mine, (px, py, cc)) for px, py in _other_chips(x, y)]


@_plan(N_PEERS)
def ag_to_sibling(refs, me, arrived):
    (land,), (x, y, cc) = refs, me
    return [(land.at[2 * px + py, :, cc], land.at[2 * px + py, :, (1 - cc) if arrived else cc], (x, y, 1 - cc))
            for px, py in _other_chips(x, y)]


@_plan(1)
def rs_to_sibling(refs, me, arrived):
    (g, r1), (x, y, cc) = refs, me
    return [(g.at[:, :, 1 - cc], r1, (x, y, 1 - cc))]


@_plan(N_PEERS)
def rs_over_ici(refs, me, arrived):
    (p, land), (x, y, cc) = refs, me
    return [(p.at[2 * px + py], land.at[j], (px, py, cc)) for j, (px, py) in enumerate(_other_chips(x, y))]


@_plan(1)
def rs_gather_sibling(refs, me, arrived):
    (red,), (x, y, cc) = refs, me
    return [(red.at[:, cc], red.at[:, (1 - cc) if arrived else cc], (x, y, 1 - cc))]


def cast_into_shards(w, layer, shard, after, name):
    L, r, c = w.shape
    tr, tc = _tile(r, 1024, 16), _tile(c, 2048)

    def body(shard_ref, w_ref, after_ref, o_ref):
        o_ref[...] = w_ref[...].astype(o_ref.dtype)

    grid_spec = pltpu.PrefetchScalarGridSpec(
        num_scalar_prefetch=1, grid=(r // tr, c // tc),
        in_specs=[pl.BlockSpec((None, tr, tc), lambda i, j, sh: (layer, i, j)), ANY],
        out_specs=pl.BlockSpec((None, None, tr, tc), lambda i, j, sh: (sh[0], 0, i, j)))
    return pl.pallas_call(body, name=name, grid_spec=grid_spec, out_shape=jax.ShapeDtypeStruct((N_CHIPS, 1, r, c), WIRE_DTYPE),
                          compiler_params=_params("parallel", "parallel"))(shard, w, after)


def add_sibling(g, r1, core, name):
    n, L, two, h, c = g.shape
    th = _tile(h, 1024, 16)
    tc = _tile(c, 2048)

    def body(core_ref, g_ref, r_ref, o_ref):
        o_ref[...] = (g_ref[...].astype(jnp.float32) + r_ref[...].astype(jnp.float32)).astype(o_ref.dtype)

    grid_spec = pltpu.PrefetchScalarGridSpec(
        num_scalar_prefetch=1, grid=(n, L, h // th, c // tc),
        in_specs=[pl.BlockSpec((None, None, None, th, tc), lambda s, l, i, j, core: (s, l, core[0], i, j)),
                  pl.BlockSpec((None, None, th, tc), lambda s, l, i, j, core: (s, l, i, j))],
        out_specs=pl.BlockSpec((None, None, th, tc), lambda s, l, i, j, core: (s, l, i, j)))
    return pl.pallas_call(body, name=name, grid_spec=grid_spec, out_shape=jax.ShapeDtypeStruct((n, L, h, c), WIRE_DTYPE),
                          compiler_params=_params("parallel", "parallel", "parallel", "parallel"))(core, g, r1)


def add_chips(p, r2, place, name):
    n, L, h, c = p.shape
    th = _tile(h, 1024, 16)
    tc = _tile(c, 2048)

    def body(shard_ref, core_ref, p_ref, r_ref, o_ref):
        acc = p_ref[...].astype(jnp.float32)
        for j in range(3):
            acc = acc + r_ref[j].astype(jnp.float32)
        o_ref[...] = acc

    grid_spec = pltpu.PrefetchScalarGridSpec(
        num_scalar_prefetch=2, grid=(L, h // th, c // tc),
        in_specs=[pl.BlockSpec((None, None, th, tc), lambda l, i, j, shard, core: (shard[0], l, i, j)),
                  pl.BlockSpec((3, None, th, tc), lambda l, i, j, shard, core: (0, l, i, j))],
        out_specs=pl.BlockSpec((None, None, th, tc), lambda l, i, j, shard, core: (l, core[0], i, j)))
    return pl.pallas_call(body, name=name, grid_spec=grid_spec, out_shape=_f32((L, 2, h, c)),
                          compiler_params=_params("parallel", "parallel", "parallel"))(*place, p, r2)


class ReduceScatter:
    def __init__(self, place, done):
        self.place, self.done = place, done
        self.stages = [[], [], []]
        self.token = None

    def _start(self, bufs, plan, name):
        started = exchange_start(bufs, plan, self.token if self.token is not None else bufs[0], name)
        self.token = started[2]
        return started

    def _advance(self, stage, after):
        key, started = self.stages[stage].pop(0)
        name = "rs_%s%d" % key
        if stage == 0:
            g, r1 = exchange_wait(started, rs_to_sibling, after, name + "_d2d_wait")
            p = add_sibling(g, r1, self.place[1], name + "_add1")
            land = lax.empty((N_PEERS,) + p.shape[1:], p.dtype)
            self.stages[1].append((key, self._start((p, land), rs_over_ici, name + "_ici_start")))
        elif stage == 1:
            p, r2 = exchange_wait(started, rs_over_ici, after, name + "_ici_wait")
            red = add_chips(p, r2, self.place, name + "_add2")
            self.stages[2].append((key, self._start((red,), rs_gather_sibling, name + "_gather_start")))
        else:
            (red,) = exchange_wait(started, rs_gather_sibling, after, name + "_gather_wait")
            L, two, h, c = red.shape
            self.done(key, red.reshape(2 * h, c))

    def push(self, key, g4):
        n, L, r, c = g4.shape
        g = g4.reshape(n, L, 2, r // 2, c)
        r1 = lax.empty((n, L, r // 2, c), g.dtype)
        self.stages[0].append((key, self._start((g, r1), rs_to_sibling, "rs_%s%d_d2d_start" % key)))
        for stage, depth in ((2, 1), (1, RS_WINDOW), (0, 1)):
            if len(self.stages[stage]) > depth:
                self._advance(stage, self.token)

    def drain(self, after):
        for stage in (0, 1, 2):
            while self.stages[stage]:
                self._advance(stage, after)


def all_reduce_small(v, name):
    R, C = v.shape

    def body(v_ref, o_ref, buf, send_sems, recv_sems):
        x, y, cc = _me()
        buf[0] = v_ref[...]
        cps = []
        for k in range(1, 8):
            dx, dy, dc = (k >> 2) & 1, (k >> 1) & 1, k & 1
            to = (x ^ dx, y ^ dy, cc ^ dc)
            cp = pltpu.make_async_remote_copy(src_ref=v_ref, dst_ref=buf.at[k], send_sem=send_sems.at[k],
                                              recv_sem=recv_sems.at[k], device_id=to, device_id_type=MESH)
            cp.start()
            cps.append(cp)
        for cp in cps:
            cp.wait()
        me = 4 * x + 2 * y + cc
        acc = buf[me]
        for a in range(1, 8):
            acc = acc + buf[a ^ me]
        o_ref[...] = acc

    vm = pl.BlockSpec(memory_space=pltpu.VMEM)
    return pl.pallas_call(
        body, name=name, out_shape=_f32((R, C)), in_specs=[vm], out_specs=vm,
        scratch_shapes=[pltpu.VMEM((8, R, C), jnp.float32), pltpu.SemaphoreType.DMA((8,)), pltpu.SemaphoreType.DMA((8,))],
    )(v)


def _rstd(x, n):
    return lax.rsqrt(jnp.sum(x * x, axis=-1, keepdims=True) * (1.0 / n) + EPS)


def _accumulate(ref, part, first):
    @pl.when(first)
    def _():
        ref[...] = part

    @pl.when(jnp.logical_not(first))
    def _():
        ref[...] += part


def rmsnorm_fwd(x, g, name):
    S, D = x.shape
    ts = _tile(S, 256, 8)

    def body(x_ref, g_ref, o_ref):
        xv = x_ref[...]
        o_ref[...] = (xv * _rstd(xv, D) * g_ref[...]).astype(o_ref.dtype)

    return pl.pallas_call(
        body, name=name, out_shape=_act((S, D)), grid=(S // ts,),
        in_specs=[pl.BlockSpec((ts, D), lambda i: (i, 0)), pl.BlockSpec((1, D), lambda i: (0, 0))],
        out_specs=pl.BlockSpec((ts, D), lambda i: (i, 0)), compiler_params=_params("parallel"))(x, g)


def _norm_bwd(x, g, dy, n):
    r = _rstd(x, n)
    xh = x * r
    dxh = dy * g
    dx = r * (dxh - xh * (jnp.sum(dxh * xh, axis=-1, keepdims=True) * (1.0 / n)))
    return dx, dy * xh


def rmsnorm_bwd(x, g, dy, dres, name):
    S, D = x.shape
    ts = _tile(S, 256, 8)

    def body(x_ref, g_ref, dy_ref, dres_ref, dx_ref, dl_ref, dg_ref):
        dx, dgp = _norm_bwd(x_ref[...], g_ref[...], dy_ref[...], D)
        dx = dres_ref[...] + dx
        dx_ref[...] = dx
        dl_ref[...] = dx.astype(dl_ref.dtype)
        _accumulate(dg_ref, jnp.sum(dgp, axis=0, keepdims=True), pl.program_id(0) == 0)

    row = pl.BlockSpec((ts, D), lambda i: (i, 0))
    vec = pl.BlockSpec((1, D), lambda i: (0, 0))
    dx, dl, dg = pl.pallas_call(
        body, name=name, out_shape=(_f32((S, D)), _act((S, D)), _f32((1, D))), grid=(S // ts,),
        in_specs=[row, vec, row, row], out_specs=(row, row, vec), compiler_params=_params("arbitrary"))(x, g, dy, dres)
    return (dx, dl), dg


def _sigmoid(x):
    return 1.0 / (1.0 + jnp.exp(-x))


def swiglu_in(xn, w_in, name, tm=512, tn=1408):
    S, D = xn.shape
    F = w_in.shape[1] // 2
    tm, tn = _tile(S, tm, 8), _tile(math.gcd(F, w_in.per), tn)
    nf = F // tn

    def body(x_ref, wg_ref, wu_ref, u_ref, a_ref):
        x = x_ref[...].astype(MXU_DTYPE)
        for cols in _column_chunks(tn):
            gt = jnp.dot(x, wg_ref[:, cols], preferred_element_type=jnp.float32)
            up = jnp.dot(x, wu_ref[:, cols], preferred_element_type=jnp.float32)
            u_ref[0, :, cols] = gt.astype(u_ref.dtype)
            u_ref[1, :, cols] = up.astype(u_ref.dtype)
            a_ref[:, cols] = (gt * _sigmoid(gt) * up).astype(a_ref.dtype)

    return pl.pallas_call(
        body, name=name, out_shape=(_act((2, 1, S, F)), _act((S, F))), grid=(nf, S // tm),
        in_specs=[pl.BlockSpec((tm, D), lambda j, i: (i, 0)), w_in.spec(D, tn, lambda j, i: (0, j)),
                  w_in.spec(D, tn, lambda j, i: (0, j + nf))],
        out_specs=(pl.BlockSpec((2, None, tm, tn), lambda j, i: (0, 0, i, j)), pl.BlockSpec((tm, tn), lambda j, i: (i, j))),
        compiler_params=_params("parallel", "parallel"))(xn, w_in.arr, w_in.arr)


def swiglu_out_bwd(dout, w_out, u, scale, name, tm=512, tn=1408):
    S, D = dout.shape
    F = w_out.shape[0]
    tm, tn = _tile(S, tm, 8), _tile(math.gcd(F, w_out.per), tn)

    def body(d_ref, w_ref, u_ref, o_ref):
        d = d_ref[...].astype(MXU_DTYPE)
        for cols in _column_chunks(tn):
            da = lax.dot_general(d, w_ref[cols, :], NT, preferred_element_type=jnp.float32) * scale
            gt, up = u_ref[0, :, cols].astype(jnp.float32), u_ref[1, :, cols].astype(jnp.float32)
            s = _sigmoid(gt)
            o_ref[0, :, cols] = (da * up * (s * (1.0 + gt * (1.0 - s)))).astype(o_ref.dtype)
            o_ref[1, :, cols] = (da * (gt * s)).astype(o_ref.dtype)

    planes = pl.BlockSpec((2, None, tm, tn), lambda j, i: (0, 0, i, j))
    return pl.pallas_call(
        body, name=name, out_shape=_act((2, 1, S, F)), grid=(F // tn, S // tm),
        in_specs=[pl.BlockSpec((tm, D), lambda j, i: (i, 0)), w_out.spec(tn, D, lambda j, i: (j, 0)), planes],
        out_specs=planes, compiler_params=_params("parallel", "parallel"))(dout, w_out.arr, u)


def loss_head(y, t, name):
    S, D = y.shape
    ts = _tile(S, 256, 8)

    def body(y_ref, t_ref, dy_ref, dl_ref, l_ref):
        e = y_ref[...] - t_ref[...]
        dy_ref[...] = e * (1.0 / D)
        dl_ref[...] = (e * (1.0 / D)).astype(dl_ref.dtype)
        l_ref[...] = jnp.full(l_ref.shape, 0.5 * jnp.sum(jnp.sum(e * e, axis=-1, keepdims=True) * (1.0 / D)), jnp.float32)

    row = pl.BlockSpec((ts, D), lambda i: (i, 0))
    dy, dl, parts = pl.pallas_call(
        body, name=name, out_shape=(_f32((S, D)), _act((S, D)), _f32((S // ts, 8, LANES))), grid=(S // ts,),
        in_specs=[row, row], out_specs=(row, row, pl.BlockSpec((None, 8, LANES), lambda i: (i, 0, 0))),
        compiler_params=_params("parallel"))(y, t)
    return jnp.sum(parts[:, 0, 0]), (dy, dl)


def rope_tables(S):
    inv = 1.0 / (ROPE_THETA ** (jnp.arange(0, ROPE_DIM, 2, dtype=jnp.float32) / ROPE_DIM))
    ang = jnp.arange(S, dtype=jnp.float32)[:, None] * inv[None, :]
    c, s = jnp.cos(ang), jnp.sin(ang)
    z = jnp.zeros_like(c)
    return (jnp.concatenate([c, c, z, z], axis=1), jnp.concatenate([-s, z, z, z], axis=1),
            jnp.concatenate([z, s, z, z], axis=1))


def _rope(x, cos, sa, sb):
    return x * cos + pltpu.roll(x, 96, 1) * sa + pltpu.roll(x, 32, 1) * sb


def _rope_t(d, cos, sa, sb):
    return d * cos + pltpu.roll(d * sa, 32, 1) + pltpu.roll(d * sb, 96, 1)


def _head_norm(x1, x2, g):
    r = lax.rsqrt((jnp.sum(x1 * x1, axis=-1, keepdims=True) + jnp.sum(x2 * x2, axis=-1, keepdims=True)) * (1.0 / QK_DIM) + EPS)
    return x1 * r * g[:, :LANES], x2 * r * g[:, LANES:], r


def _head_norm_bwd(x1, x2, g, d1, d2):
    _, _, r = _head_norm(x1, x2, g)
    h1, h2 = x1 * r, x2 * r
    e1, e2 = d1 * g[:, :LANES], d2 * g[:, LANES:]
    m = (jnp.sum(e1 * h1, axis=-1, keepdims=True) + jnp.sum(e2 * h2, axis=-1, keepdims=True)) * (1.0 / QK_DIM)
    return r * (e1 - h1 * m), r * (e2 - h2 * m), d1 * h1, d2 * h2


def mla_latent_fwd(lat, g_cq, g_ckv, name):
    S, W = lat.shape
    QL, KL = g_cq.shape[1], g_ckv.shape[1]
    ts = _tile(S, 256, 8)

    def body(l_ref, gq_ref, gk_ref, cq_ref, ckv_ref):
        a, b = l_ref[:, :QL], l_ref[:, QL:QL + KL]
        cq_ref[...] = (a * _rstd(a, QL) * gq_ref[...]).astype(cq_ref.dtype)
        ckv_ref[...] = (b * _rstd(b, KL) * gk_ref[...]).astype(ckv_ref.dtype)

    return pl.pallas_call(
        body, name=name, out_shape=(_act((S, QL)), _act((S, KL))), grid=(S // ts,),
        in_specs=[pl.BlockSpec((ts, W), lambda i: (i, 0)), pl.BlockSpec((1, QL), lambda i: (0, 0)),
                  pl.BlockSpec((1, KL), lambda i: (0, 0))],
        out_specs=(pl.BlockSpec((ts, QL), lambda i: (i, 0)), pl.BlockSpec((ts, KL), lambda i: (i, 0))),
        compiler_params=_params("parallel"))(lat, g_cq, g_ckv)


def mla_latent_bwd(dcq, dckv, dkpe, lat, g_cq, g_ckv, name):
    S, W = lat.shape
    QL, KL = g_cq.shape[1], g_ckv.shape[1]
    ts = _tile(S, 256, 8)

    def body(dq_ref, dk_ref, dp_ref, l_ref, gq_ref, gk_ref, o_ref, dgq_ref, dgk_ref):
        first = pl.program_id(0) == 0
        da, ga = _norm_bwd(l_ref[:, :QL], gq_ref[...], dq_ref[...], QL)
        db, gb = _norm_bwd(l_ref[:, QL:QL + KL], gk_ref[...], dk_ref[...], KL)
        o_ref[:, :QL] = da.astype(o_ref.dtype)
        o_ref[:, QL:QL + KL] = db.astype(o_ref.dtype)
        o_ref[:, QL + KL:] = dp_ref[...].astype(o_ref.dtype)
        _accumulate(dgq_ref, jnp.sum(ga, axis=0, keepdims=True), first)
        _accumulate(dgk_ref, jnp.sum(gb, axis=0, keepdims=True), first)

    row = lambda n: pl.BlockSpec((ts, n), lambda i: (i, 0))
    vec = lambda n: pl.BlockSpec((1, n), lambda i: (0, 0))
    return pl.pallas_call(
        body, name=name, out_shape=(_act((S, W)), _f32((1, QL)), _f32((1, KL))), grid=(S // ts,),
        in_specs=[row(QL), row(KL), row(LANES), row(W), vec(QL), vec(KL)], out_specs=(row(W), vec(QL), vec(KL)),
        compiler_params=_params("arbitrary"))(dcq, dckv, dkpe, lat, g_cq, g_ckv)


def mla_q_prep_fwd(qraw, g, tabs, H, name):
    S = qraw.shape[0]
    ts = _tile(S, PREP_ROWS, 8)

    def body(x_ref, g_ref, c_ref, a_ref, b_ref, o_ref):
        y1, y2, _ = _head_norm(x_ref[:, :LANES], x_ref[:, LANES:], g_ref[...])
        o_ref[:, :LANES] = y1.astype(o_ref.dtype)
        o_ref[:, LANES:] = _rope(y2, c_ref[...], a_ref[...], b_ref[...]).astype(o_ref.dtype)

    tab = pl.BlockSpec((ts, LANES), lambda i, h: (i, 0))
    return pl.pallas_call(
        body, name=name, out_shape=_act((H, S, QK_PAD)), grid=(S // ts, H),
        in_specs=[pl.BlockSpec((ts, QK_PAD), lambda i, h: (i, h)), pl.BlockSpec((1, QK_PAD), lambda i, h: (0, 0)), tab, tab, tab],
        out_specs=pl.BlockSpec((None, ts, QK_PAD), lambda i, h: (h, i, 0)),
        compiler_params=_params("parallel", "parallel"))(qraw, g, *tabs)


def mla_q_prep_bwd(dq, qraw, g, tabs, H, name):
    S = qraw.shape[0]
    ts = _tile(S, PREP_ROWS, 8)

    def body(d_ref, x_ref, g_ref, c_ref, a_ref, b_ref, o_ref, dg_ref):
        d2 = _rope_t(d_ref[:, LANES:], c_ref[...], a_ref[...], b_ref[...])
        dx1, dx2, g1, g2 = _head_norm_bwd(x_ref[:, :LANES], x_ref[:, LANES:], g_ref[...], d_ref[:, :LANES], d2)
        o_ref[:, :LANES] = dx1.astype(o_ref.dtype)
        o_ref[:, LANES:] = dx2.astype(o_ref.dtype)
        first = jnp.logical_and(pl.program_id(0) == 0, pl.program_id(1) == 0)
        part = jnp.concatenate([jnp.sum(g1, axis=0, keepdims=True), jnp.sum(g2, axis=0, keepdims=True)], axis=1)
        _accumulate(dg_ref, part, first)

    tab = pl.BlockSpec((ts, LANES), lambda i, h: (i, 0))
    vec = pl.BlockSpec((1, QK_PAD), lambda i, h: (0, 0))
    return pl.pallas_call(
        body, name=name, out_shape=(_act((S, H * QK_PAD)), _f32((1, QK_PAD))), grid=(S // ts, H),
        in_specs=[pl.BlockSpec((None, ts, QK_PAD), lambda i, h: (h, i, 0)), pl.BlockSpec((ts, QK_PAD), lambda i, h: (i, h)),
                  vec, tab, tab, tab],
        out_specs=(pl.BlockSpec((ts, QK_PAD), lambda i, h: (i, h)), vec),
        compiler_params=_params("arbitrary", "arbitrary"))(dq, qraw, g, *tabs)


def mla_k_prep_fwd(kvraw, lat, g, tabs, H, pe_blk, name):
    S = kvraw.shape[0]
    ts = _tile(S, PREP_ROWS, 8)

    def body(x_ref, p_ref, g_ref, c_ref, a_ref, b_ref, k_ref, v_ref):
        y1, y2, _ = _head_norm(x_ref[:, :LANES], p_ref[...], g_ref[...])
        k_ref[:, :LANES] = y1.astype(k_ref.dtype)
        k_ref[:, LANES:] = _rope(y2, c_ref[...], a_ref[...], b_ref[...]).astype(k_ref.dtype)
        v_ref[...] = x_ref[:, LANES:].astype(v_ref.dtype)

    tab = pl.BlockSpec((ts, LANES), lambda i, h: (i, 0))
    return pl.pallas_call(
        body, name=name, out_shape=(_act((H, S, QK_PAD)), _act((H, S, LANES))), grid=(S // ts, H),
        in_specs=[pl.BlockSpec((ts, QK_PAD), lambda i, h: (i, h)), pl.BlockSpec((ts, LANES), lambda i, h: (i, pe_blk)),
                  pl.BlockSpec((1, QK_PAD), lambda i, h: (0, 0)), tab, tab, tab],
        out_specs=(pl.BlockSpec((None, ts, QK_PAD), lambda i, h: (h, i, 0)), pl.BlockSpec((None, ts, LANES), lambda i, h: (h, i, 0))),
        compiler_params=_params("parallel", "parallel"))(kvraw, lat, g, *tabs)


def mla_k_prep_bwd(dk, dv, kvraw, lat, g, tabs, H, pe_blk, name):
    S = kvraw.shape[0]
    ts = _tile(S, PREP_ROWS, 8)

    def body(dk_ref, dv_ref, x_ref, p_ref, g_ref, c_ref, a_ref, b_ref, o_ref, dp_ref, dg_ref):
        i, h = pl.program_id(0), pl.program_id(1)
        d2 = _rope_t(dk_ref[:, LANES:], c_ref[...], a_ref[...], b_ref[...])
        dx1, dx2, g1, g2 = _head_norm_bwd(x_ref[:, :LANES], p_ref[...], g_ref[...], dk_ref[:, :LANES], d2)
        o_ref[:, :LANES] = dx1.astype(o_ref.dtype)
        o_ref[:, LANES:] = dv_ref[...].astype(o_ref.dtype)
        _accumulate(dp_ref, dx2, h == 0)
        part = jnp.concatenate([jnp.sum(g1, axis=0, keepdims=True), jnp.sum(g2, axis=0, keepdims=True)], axis=1)
        _accumulate(dg_ref, part, jnp.logical_and(i == 0, h == 0))

    tab = pl.BlockSpec((ts, LANES), lambda i, h: (i, 0))
    vec = pl.BlockSpec((1, QK_PAD), lambda i, h: (0, 0))
    return pl.pallas_call(
        body, name=name, out_shape=(_act((S, H * QK_PAD)), _f32((S, LANES)), _f32((1, QK_PAD))), grid=(S // ts, H),
        in_specs=[pl.BlockSpec((None, ts, QK_PAD), lambda i, h: (h, i, 0)), pl.BlockSpec((None, ts, LANES), lambda i, h: (h, i, 0)),
                  pl.BlockSpec((ts, QK_PAD), lambda i, h: (i, h)), pl.BlockSpec((ts, LANES), lambda i, h: (i, pe_blk)),
                  vec, tab, tab, tab],
        out_specs=(pl.BlockSpec((ts, QK_PAD), lambda i, h: (i, h)), tab, vec),
        compiler_params=_params("arbitrary", "arbitrary"))(dk, dv, kvraw, lat, g, *tabs)


def _causal_scores(q, k, scale, diagonal):
    s = lax.dot_general(q, k, NT, preferred_element_type=jnp.float32) * scale
    if not diagonal:
        return s
    row = lax.broadcasted_iota(jnp.int32, s.shape, 0)
    col = lax.broadcasted_iota(jnp.int32, s.shape, 1)
    return jnp.where(col <= row, s, NEG)


def _on_causal_blocks(qi, ki, step):
    @pl.when(ki < qi)
    def _():
        step(False)

    @pl.when(ki == qi)
    def _():
        step(True)


def mla_attention_fwd(q, k, v, name):
    H, S, _ = q.shape
    t = _tile(S, ATTN_BLOCK)
    n = S // t
    scale = 1.0 / math.sqrt(QK_DIM)

    def body(q_ref, k_ref, v_ref, o_ref, lse_ref, m_sc, l_sc, acc):
        qi, ki = pl.program_id(1), pl.program_id(2)

        @pl.when(ki == 0)
        def _():
            m_sc[...] = jnp.full(m_sc.shape, NEG, jnp.float32)
            l_sc[...] = jnp.zeros_like(l_sc)
            acc[...] = jnp.zeros_like(acc)

        def step(diagonal):
            s = _causal_scores(q_ref[...], k_ref[...], scale, diagonal)
            m_new = jnp.maximum(m_sc[...], jnp.max(s, axis=-1, keepdims=True))
            alpha = jnp.exp(m_sc[...] - m_new)
            p = jnp.exp(s - m_new)
            l_sc[...] = alpha * l_sc[...] + jnp.sum(p, axis=-1, keepdims=True)
            acc[...] = alpha * acc[...] + jnp.dot(p.astype(MXU_DTYPE), v_ref[...], preferred_element_type=jnp.float32)
            m_sc[...] = m_new

        _on_causal_blocks(qi, ki, step)

        @pl.when(ki == qi)
        def _():
            o_ref[...] = (acc[...] / l_sc[...]).astype(o_ref.dtype)
            lse_ref[...] = m_sc[...] + jnp.log(l_sc[...])

    kv = lambda w: pl.BlockSpec((None, t, w), lambda h, qi, ki: (h, jnp.minimum(ki, qi), 0))
    return pl.pallas_call(
        body, name=name, out_shape=(_act((S, H * LANES)), _f32((H, S, 1))), grid=(H, n, n),
        in_specs=[pl.BlockSpec((None, t, QK_PAD), lambda h, qi, ki: (h, qi, 0)), kv(QK_PAD), kv(LANES)],
        out_specs=(pl.BlockSpec((t, LANES), lambda h, qi, ki: (qi, h)), pl.BlockSpec((None, t, 1), lambda h, qi, ki: (h, qi, 0))),
        scratch_shapes=[pltpu.VMEM((t, 1), jnp.float32), pltpu.VMEM((t, 1), jnp.float32), pltpu.VMEM((t, LANES), jnp.float32)],
        compiler_params=_params("parallel", "parallel", "arbitrary"))(q, k, v)


def mla_attention_bwd(q, k, v, do, o, lse, name):
    H, S, _ = q.shape
    t = _tile(S, ATTN_BLOCK)
    n = S // t
    scale = 1.0 / math.sqrt(QK_DIM)

    def body(q_ref, k_ref, v_ref, do_ref, o_ref, lse_ref, dq_ref, dk_ref, dv_ref, dk_acc, dv_acc):
        ki, qi = pl.program_id(1), pl.program_id(2)
        rows = pl.ds(pl.multiple_of(qi * t, t), t)

        @pl.when(qi == 0)
        def _():
            dk_acc[...] = jnp.zeros_like(dk_acc)
            dv_acc[...] = jnp.zeros_like(dv_acc)

        @pl.when(ki == 0)
        def _():
            dq_ref[rows, :] = jnp.zeros((t, QK_PAD), jnp.float32)

        def step(diagonal):
            p = jnp.exp(_causal_scores(q_ref[...], k_ref[...], scale, diagonal) - lse_ref[...])
            dof = do_ref[...]
            dob = dof.astype(MXU_DTYPE)
            delta = jnp.sum(dof * o_ref[...].astype(jnp.float32), axis=-1, keepdims=True)
            dv_acc[...] += lax.dot_general(p.astype(MXU_DTYPE), dob, TN, preferred_element_type=jnp.float32)
            dp = lax.dot_general(dob, v_ref[...], NT, preferred_element_type=jnp.float32)
            ds = (p * (dp - delta)).astype(MXU_DTYPE)
            dk_acc[...] += lax.dot_general(ds, q_ref[...], TN, preferred_element_type=jnp.float32)
            dq_ref[rows, :] += jnp.dot(ds, k_ref[...], preferred_element_type=jnp.float32) * scale

        _on_causal_blocks(qi, ki, step)

        @pl.when(qi == n - 1)
        def _():
            dk_ref[...] = dk_acc[...] * scale
            dv_ref[...] = dv_acc[...]

    qrow = lambda h, ki, qi: (h, jnp.maximum(qi, ki), 0)
    kv = lambda w: pl.BlockSpec((None, t, w), lambda h, ki, qi: (h, ki, 0))
    col = pl.BlockSpec((None, t, 1), qrow)
    head = pl.BlockSpec((t, LANES), lambda h, ki, qi: (jnp.maximum(qi, ki), h))
    return pl.pallas_call(
        body, name=name, out_shape=(_f32((H, S, QK_PAD)), _f32((H, S, QK_PAD)), _f32((H, S, LANES))), grid=(H, n, n),
        in_specs=[pl.BlockSpec((None, t, QK_PAD), qrow), kv(QK_PAD), kv(LANES),
                  head, head, col],
        out_specs=(pl.BlockSpec((None, S, QK_PAD), lambda h, ki, qi: (h, 0, 0)), kv(QK_PAD), kv(LANES)),
        scratch_shapes=[pltpu.VMEM((t, QK_PAD), jnp.float32), pltpu.VMEM((t, LANES), jnp.float32)],
        compiler_params=_params("parallel", "arbitrary", "arbitrary"))(q, k, v, do, o, lse)


def _alibi_slopes(G, Hd):
    k = np.arange(1, G * Hd + 1, dtype=np.float32)
    s = (2.0 ** (-8.0 * k / (G * Hd))).astype(np.float32).reshape(G, Hd)
    return jnp.asarray(np.broadcast_to(s[:, :, None, None], (G, Hd, 1, LANES)).copy())


def _dil_scores(qn, kn, scale, slope_d, prev, valid):
    s = lax.dot_general(qn, kn, NT, preferred_element_type=jnp.float32) * scale
    iq = lax.broadcasted_iota(jnp.int32, s.shape, 0)
    ik = lax.broadcasted_iota(jnp.int32, s.shape, 1)
    dist = iq - ik + (BLK if prev else 0)
    ok = (ik >= iq) if prev else (ik <= iq)
    s = s - slope_d * dist.astype(jnp.float32)
    return jnp.where(jnp.logical_and(ok, valid), s, NEG)


def _dil_heads(d, Hd, block_bytes):
    hb = max(1, min(Hd, block_bytes // (BLK * LANES * 4))) if d == 1 else 1
    assert Hd % hb == 0, (Hd, hb)
    return hb


def _dil_rows(r, d):
    return pl.ds(r, BLK, stride=d) if d > 1 else slice(None)


def _loop_residues(d, residue, init):
    per = math.gcd(d, 4)

    def one_pass(i, carry):
        for k in range(per):
            carry = residue(per * i + k, carry)
        return carry

    return one_pass(0, init) if d == per else lax.fori_loop(0, d // per, one_pass, init)


def _dil_specs(d, nblk, Hd, G, g, hb):
    def spec(kind, shift):
        col0 = (kind * G + g) * Hd // hb
        return pl.BlockSpec((BLK * d, hb * LANES), lambda n, hg: (jnp.clip(n + shift, 0, nblk - 1), col0 + hg))
    return spec


def _head_spec(d, nblk, hb, shift):
    return pl.BlockSpec((BLK * d, hb * LANES), lambda n, hg: (jnp.clip(n + shift, 0, nblk - 1), hg))


def dilated_fwd(qkv, gq, gk, slopes, g, d, Hd, G, name):
    S, C = qkv.shape
    nblk = S // (BLK * d)
    hb = _dil_heads(d, Hd, 2 << 20)
    scale = 1.0 / math.sqrt(LANES)
    spec = _dil_specs(d, nblk, Hd, G, g, hb)

    def body(q_ref, kc_ref, kp_ref, vc_ref, vp_ref, gq_ref, gk_ref, sl_ref, o_ref, l_ref):
        n = pl.program_id(0)
        nrm = lambda t, gg: (t * _rstd(t, LANES) * gg).astype(MXU_DTYPE)

        def residue(r, carry):
            rows = _dil_rows(r, d)
            for hh in range(hb):
                cols = slice(hh * LANES, (hh + 1) * LANES)
                slope_d = sl_ref[hh][:, :1] * float(d)
                qn = nrm(q_ref[rows, cols], gq_ref[...])
                sc = _dil_scores(qn, nrm(kc_ref[rows, cols], gk_ref[...]), scale, slope_d, False, True)
                sp = _dil_scores(qn, nrm(kp_ref[rows, cols], gk_ref[...]), scale, slope_d, True, n > 0)
                m = jnp.maximum(jnp.max(sc, axis=-1, keepdims=True), jnp.max(sp, axis=-1, keepdims=True))
                lse = m + jnp.log(jnp.sum(jnp.exp(sc - m), axis=-1, keepdims=True) + jnp.sum(jnp.exp(sp - m), axis=-1, keepdims=True))
                o = jnp.dot(jnp.exp(sc - lse).astype(MXU_DTYPE), vc_ref[rows, cols].astype(MXU_DTYPE), preferred_element_type=jnp.float32)
                o = o + jnp.dot(jnp.exp(sp - lse).astype(MXU_DTYPE), vp_ref[rows, cols].astype(MXU_DTYPE), preferred_element_type=jnp.float32)
                o_ref[rows, cols] = o
                l_ref[rows, cols] = jnp.broadcast_to(lse, (BLK, LANES))
            return carry

        _loop_residues(d, residue, 0)

    vec = pl.BlockSpec((1, LANES), lambda n, hg: (0, 0))
    out = _head_spec(d, nblk, hb, 0)
    return pl.pallas_call(
        body, name=name, out_shape=(_f32((S, Hd * LANES)), _f32((S, Hd * LANES))), grid=(nblk, Hd // hb),
        in_specs=[spec(0, 0), spec(1, 0), spec(1, -1), spec(2, 0), spec(2, -1), vec, vec,
                  pl.BlockSpec((None, hb, 1, LANES), lambda n, hg: (g, hg, 0, 0))],
        out_specs=(out, out), compiler_params=_params("parallel", "parallel"),
    )(qkv, qkv, qkv, qkv, qkv, gq, gk, slopes)


def dilated_merge(os_, ls_, name):
    S, W = os_[0].shape
    G = len(os_)
    ts, tw = _tile(S, 512, 8), _tile(W, 512)

    def body(*refs):
        o_refs, l_refs, (o_ref, t_ref) = refs[:G], refs[G:2 * G], refs[2 * G:]
        ls = [r[...] for r in l_refs]
        m = ls[0]
        for l in ls[1:]:
            m = jnp.maximum(m, l)
        es = [jnp.exp(l - m) for l in ls]
        tot = es[0]
        for e in es[1:]:
            tot = tot + e
        acc = o_refs[0][...] * (es[0] / tot)
        for r, e in zip(o_refs[1:], es[1:]):
            acc = acc + r[...] * (e / tot)
        o_ref[...] = acc.astype(o_ref.dtype)
        t_ref[...] = m + jnp.log(tot)

    blk = pl.BlockSpec((ts, tw), lambda i, j: (i, j))
    return pl.pallas_call(
        body, name=name, out_shape=(_act((S, W)), _f32((S, W))), grid=(S // ts, W // tw),
        in_specs=[blk] * (2 * G), out_specs=(blk, blk), compiler_params=_params("parallel", "parallel"))(*os_, *ls_)


def dilated_delta(do, o, name):
    S, W = do.shape
    ts = _tile(S, 512, 8)

    def body(d_ref, o_ref, out_ref):
        out_ref[...] = jnp.broadcast_to(jnp.sum(d_ref[...] * o_ref[...].astype(jnp.float32), axis=-1, keepdims=True), out_ref.shape)

    blk = pl.BlockSpec((ts, LANES), lambda i, h: (i, h))
    return pl.pallas_call(body, name=name, out_shape=_f32((S, W)), grid=(S // ts, W // LANES), in_specs=[blk, blk],
                          out_specs=blk, compiler_params=_params("parallel", "parallel"))(do, o)


def dilated_bwd(qkv, do, lse, delta, gq, gk, slopes, g, d, Hd, G, name):
    S, C = qkv.shape
    nblk = S // (BLK * d)
    W = Hd * LANES
    scale = 1.0 / math.sqrt(LANES)
    hb = _dil_heads(d, Hd, 1 << 20)
    spec = _dil_specs(d, nblk, Hd, G, g, hb)
    hspec = lambda shift: _head_spec(d, nblk, hb, shift)

    def body(q_ref, qx_ref, kc_ref, kp_ref, vc_ref, vp_ref, do_ref, dox_ref, l_ref, lx_ref, dl_ref, dlx_ref,
             gq_ref, gk_ref, sl_ref, dq_ref, dk_ref, dv_ref, dgq_ref, dgk_ref):
        n = pl.program_id(0)
        gqv, gkv = gq_ref[...], gk_ref[...]
        nrm = lambda t, gg: (t * _rstd(t, LANES) * gg).astype(MXU_DTYPE)
        f32dot = lambda a, b, dn: lax.dot_general(a, b, dn, preferred_element_type=jnp.float32)

        def residue(r, carry):
            rows = _dil_rows(r, d)
            dgq_sum, dgk_sum = carry
            for hh in range(hb):
                cols = slice(hh * LANES, (hh + 1) * LANES)
                slope_d = sl_ref[hh][:, :1] * float(d)
                q, kc = q_ref[rows, cols], kc_ref[rows, cols]
                qn, qxn = nrm(q, gqv), nrm(qx_ref[rows, cols], gqv)
                kcn, kpn = nrm(kc, gkv), nrm(kp_ref[rows, cols], gkv)
                vc, vp = vc_ref[rows, cols].astype(MXU_DTYPE), vp_ref[rows, cols].astype(MXU_DTYPE)
                dob, doxb = do_ref[rows, cols].astype(MXU_DTYPE), dox_ref[rows, cols].astype(MXU_DTYPE)
                lrow, lxrow = l_ref[rows, cols][:, :1], lx_ref[rows, cols][:, :1]
                drow, dxrow = dl_ref[rows, cols][:, :1], dlx_ref[rows, cols][:, :1]
                pc = jnp.exp(_dil_scores(qn, kcn, scale, slope_d, False, True) - lrow)
                pp = jnp.exp(_dil_scores(qn, kpn, scale, slope_d, True, n > 0) - lrow)
                dsc = pc * (f32dot(dob, vc, NT) - drow)
                dsp = pp * (f32dot(dob, vp, NT) - drow)
                dqn = (jnp.dot(dsc.astype(MXU_DTYPE), kcn, preferred_element_type=jnp.float32)
                       + jnp.dot(dsp.astype(MXU_DTYPE), kpn, preferred_element_type=jnp.float32)) * scale
                dq, dgq = _norm_bwd(q, gqv, dqn, LANES)
                dq_ref[rows, cols] = dq
                px = jnp.exp(_dil_scores(qxn, kcn, scale, slope_d, True, n < nblk - 1) - lxrow)
                dsx = px * (f32dot(doxb, vc, NT) - dxrow)
                dkn = (f32dot(dsc.astype(MXU_DTYPE), qn, TN) + f32dot(dsx.astype(MXU_DTYPE), qxn, TN)) * scale
                dk, dgk = _norm_bwd(kc, gkv, dkn, LANES)
                dk_ref[rows, cols] = dk
                dv_ref[rows, cols] = f32dot(pc.astype(MXU_DTYPE), dob, TN) + f32dot(px.astype(MXU_DTYPE), doxb, TN)
                dgq_sum = dgq_sum + jnp.sum(dgq, axis=0, keepdims=True)
                dgk_sum = dgk_sum + jnp.sum(dgk, axis=0, keepdims=True)
            return dgq_sum, dgk_sum

        zero = jnp.zeros((1, LANES), jnp.float32)
        dgq_sum, dgk_sum = _loop_residues(d, residue, (zero, zero))
        first = jnp.logical_and(n == 0, pl.program_id(1) == 0)
        _accumulate(dgq_ref, dgq_sum, first)
        _accumulate(dgk_ref, dgk_sum, first)

    vec = pl.BlockSpec((1, LANES), lambda n, hg: (0, 0))
    out = hspec(0)
    return pl.pallas_call(
        body, name=name, out_shape=(_f32((S, W)), _f32((S, W)), _f32((S, W)), _f32((1, LANES)), _f32((1, LANES))),
        grid=(nblk, Hd // hb),
        in_specs=[spec(0, 0), spec(0, 1), spec(1, 0), spec(1, -1), spec(2, 0), spec(2, -1), hspec(0), hspec(1), hspec(0), hspec(1),
                  hspec(0), hspec(1), vec, vec, pl.BlockSpec((None, hb, 1, LANES), lambda n, hg: (g, hg, 0, 0))],
        out_specs=(out, out, out, vec, vec), compiler_params=_params("arbitrary", "arbitrary"),
    )(qkv, qkv, qkv, qkv, qkv, qkv, do, do, lse, lse, delta, delta, gq, gk, slopes)


def adamw(w, g, m, v, layer, prev, name):
    L, r, c = w.shape
    tr, tc = _tile(r, 512, 8), _tile(c, 1024)
    c1 = 1.0 / (1.0 - ADAM_B1 ** ADAM_STEP)
    c2 = 1.0 / (1.0 - ADAM_B2 ** ADAM_STEP)

    def body(*refs):
        w_ref, g_ref, m_ref, v_ref = refs[:4]
        go_ref, d_ref, mo_ref, vo_ref = refs[-4:]
        gv = g_ref[...]
        mn = ADAM_B1 * m_ref[...] + (1.0 - ADAM_B1) * gv
        vn = ADAM_B2 * v_ref[...] + (1.0 - ADAM_B2) * (gv * gv)
        go_ref[...] = gv
        d_ref[...] = -ADAM_LR * ((mn * c1) / (jnp.sqrt(vn * c2) + ADAM_EPS) + ADAM_WD * w_ref[...])
        mo_ref[...] = mn
        vo_ref[...] = vn

    lay = pl.BlockSpec((None, tr, tc), lambda i, j: (layer, i, j))
    flat = pl.BlockSpec((tr, tc), lambda i, j: (i, j))
    ins = [w, g, m, v] + (list(prev) if prev is not None else [])
    in_specs = [lay, flat, lay, lay] + ([ANY] * 4 if prev is not None else [])
    return pl.pallas_call(
        body, name=name, out_shape=tuple(_f32((L, r, c)) for _ in range(4)), grid=(r // tr, c // tc),
        in_specs=in_specs, out_specs=(lay, lay, lay, lay),
        input_output_aliases=({4 + k: k for k in range(4)} if prev is not None else {}),
        compiler_params=_params("parallel", "parallel"))(*ins)


def _ffn_fwd(h, g, W, kind, tag):
    xn = rmsnorm_fwd(h, g, tag + "_norm")
    u, a = swiglu_in(xn, W(kind + "_w_in", h), tag + "_in")
    out = matmul(a, W(kind + "_w_out", a), scale=0.5, res=h, name=tag + "_out", tk=2816)
    return out, (h, xn, u, a)


def _ffn_bwd(dout, saved, g, W, emit, kind, tag):
    h, xn, u, a = saved
    dout, dlow = dout
    emit(kind + "_w_out", matmul(a, dlow, ta=True, scale=0.5, out_dtype=WIRE_DTYPE, out_axis=0, name=tag + "_dwout", tm=1408, tk=2048))
    du = Sharded(swiglu_out_bwd(dlow, W(kind + "_w_out", None), u, 0.5, tag + "_da"), 0, 1)
    emit(kind + "_w_in", matmul(xn, du, ta=True, out_dtype=WIRE_DTYPE, out_axis=1, name=tag + "_dwin", tn=1408, tk=2048))
    dxn = matmul(du, W(kind + "_w_in", None), tb=True, name=tag + "_dxn", tk=2816)
    return rmsnorm_bwd(h, g, dxn, dout, tag + "_dnorm")


def _pad_gain(g):
    return jnp.pad(g, ((0, 0), (0, QK_PAD - QK_DIM)))


def _mla_fwd(h, P, W, tabs, H):
    g_mix, g_cq, g_ckv = P["mix_norm"][0:1], P["mla_g_cq"], P["mla_g_ckv"]
    pe_blk = (g_cq.shape[1] + g_ckv.shape[1]) // LANES
    xn = rmsnorm_fwd(h, g_mix, "mla_norm")
    w_down = W("mla_w_down", h)
    lat = matmul(xn, w_down, name="mla_down", tn=w_down.shape[1])
    cq, ckv = mla_latent_fwd(lat, g_cq, g_ckv, "mla_latent")
    qraw = matmul(cq, W("mla_w_uq", lat), name="mla_uq")
    kvraw = matmul(ckv, W("mla_w_ukv", qraw), name="mla_ukv")
    q = mla_q_prep_fwd(qraw, _pad_gain(P["mla_g_qn"]), tabs, H, "mla_qprep")
    k, v = mla_k_prep_fwd(kvraw, lat, _pad_gain(P["mla_g_kn"]), tabs, H, pe_blk, "mla_kprep")
    o, lse = mla_attention_fwd(q, k, v, "mla_attn")
    out = matmul(o, W("mla_w_o", lse), res=h, name="mla_o")
    return out, (h, xn, lat, cq, ckv, qraw, kvraw, q, k, v, o, lse, pe_blk)


def _mla_bwd(dout, saved, P, W, emit, tabs, H):
    h, xn, lat, cq, ckv, qraw, kvraw, q, k, v, o, lse, pe_blk = saved
    dout, dlow = dout
    emit("mla_w_o", matmul(o, dlow, ta=True, out_dtype=WIRE_DTYPE, out_axis=0, name="mla_dwo", tm=512))
    do = matmul(dlow, W("mla_w_o", None), tb=True, name="mla_do", tn=512)
    dq, dk, dv = mla_attention_bwd(q, k, v, do, o, lse, "mla_attn_bwd")
    dqraw, dgq = mla_q_prep_bwd(dq, qraw, _pad_gain(P["mla_g_qn"]), tabs, H, "mla_dqprep")
    dkvraw, dkpe, dgk = mla_k_prep_bwd(dk, dv, kvraw, lat, _pad_gain(P["mla_g_kn"]), tabs, H, pe_blk, "mla_dkprep")
    emit("mla_w_uq", matmul(cq, dqraw, ta=True, out_dtype=WIRE_DTYPE, out_axis=1, name="mla_dwuq"))
    dcq = matmul(dqraw, W("mla_w_uq", None), tb=True, name="mla_dcq", tk=1024)
    emit("mla_w_ukv", matmul(ckv, dkvraw, ta=True, out_dtype=WIRE_DTYPE, out_axis=1, name="mla_dwukv"))
    dckv = matmul(dkvraw, W("mla_w_ukv", None), tb=True, name="mla_dckv", tk=1024)
    dlat, dgcq, dgckv = mla_latent_bwd(dcq, dckv, dkpe, lat, P["mla_g_cq"], P["mla_g_ckv"], "mla_dlatent")
    emit("mla_w_down", matmul(xn, dlat, ta=True, out_dtype=WIRE_DTYPE, out_axis=0, name="mla_dwdown", tm=512, tn=dlat.shape[1]))
    dxn = matmul(dlat, W("mla_w_down", None), tb=True, name="mla_dxn", tn=512, tk=dlat.shape[1])
    dh, dgm = rmsnorm_bwd(h, P["mix_norm"][0:1], dxn, dout, "mla_dnorm")
    return dh, dgm, dict(mla_g_qn=dgq[:, :QK_DIM], mla_g_kn=dgk[:, :QK_DIM], mla_g_cq=dgcq, mla_g_ckv=dgckv)


def _dil_fwd(h, P, W, slopes, Hd):
    G = len(DIL_PAIRS)
    xn = rmsnorm_fwd(h, P["mix_norm"][1:2], "dil_norm")
    qkv = matmul(xn, W("dil_w_qkv", h), name="dil_qkv", tn=1152)
    os_, ls_ = [], []
    for g, (_, d) in enumerate(DIL_PAIRS):
        o_g, l_g = dilated_fwd(qkv, P["dil_g_qn"], P["dil_g_kn"], slopes, g, d, Hd, G, f"dil_attn{g}")
        os_.append(o_g)
        ls_.append(l_g)
    o, lse = dilated_merge(os_, ls_, "dil_merge")
    out = matmul(o, W("dil_w_o", lse), res=h, name="dil_o", tn=512)
    return out, (h, xn, qkv, o, lse)


def _dil_bwd(dout, saved, P, W, emit, slopes, Hd):
    h, xn, qkv, o, lse = saved
    ngrp = len(DIL_PAIRS)
    dout, dlow = dout
    emit("dil_w_o", matmul(o, dlow, ta=True, out_dtype=WIRE_DTYPE, out_axis=1, name="dil_dwo", tn=512))
    do = matmul(dlow, W("dil_w_o", None), tb=True, name="dil_do", tk=512)
    delta = dilated_delta(do, o, "dil_delta")
    parts = [dilated_bwd(qkv, do, lse, delta, P["dil_g_qn"], P["dil_g_kn"], slopes, g, d, Hd, ngrp, f"dil_dattn{g}")
             for g, (_, d) in enumerate(DIL_PAIRS)]
    dqkv = jnp.concatenate([p[kind] for kind in range(3) for p in parts], axis=1).astype(MXU_DTYPE)
    emit("dil_w_qkv", matmul(xn, dqkv, ta=True, out_dtype=WIRE_DTYPE, out_axis=1, name="dil_dwqkv", tn=1152))
    dxn = matmul(dqkv, W("dil_w_qkv", None), tb=True, name="dil_dxn", tk=2304)
    dh, dgm = rmsnorm_bwd(h, P["mix_norm"][1:2], dxn, dout, "dil_dnorm")
    return dh, dgm, dict(dil_g_qn=parts[0][3] + parts[1][3] + parts[2][3], dil_g_kn=parts[0][4] + parts[1][4] + parts[2][4])


def local_step(x, target, P, get_w, on_grad):
    S, D = x.shape
    H, Hd = MLA_HEADS, DIL_HEADS
    tabs = rope_tables(S)
    slopes = _alibi_slopes(len(DIL_PAIRS), Hd)
    cache = {}

    def weights_of(layer):
        def W(name, after):
            if (name, layer) not in cache:
                cache[name, layer] = Sharded(get_w(name, layer, after), 0, SHARD_AXIS[name])
            return cache[name, layer]
        return W

    row = lambda name, i: P[name][i:i + 1]
    h = x
    saved = []
    for i in range(2):
        W = weights_of(i)
        h, s1 = _ffn_fwd(h, row("ffn1_norm", i), W, "ffn1", f"l{i}_ffn1")
        h, sm = _mla_fwd(h, P, weights_of(0), tabs, H) if i == 0 else _dil_fwd(h, P, weights_of(0), slopes, Hd)
        h, s2 = _ffn_fwd(h, row("ffn2_norm", i), W, "ffn2", f"l{i}_ffn2")
        saved.append((s1, sm, s2))
    loss, dh = loss_head(h, target, "loss")

    gs = {n: [None, None] for n in ("ffn1_norm", "mix_norm", "ffn2_norm")}
    for i in (1, 0):
        s1, sm, s2 = saved[i]
        W = weights_of(i)
        emit = lambda name, g4, layer=i: on_grad(name, layer, g4)
        emit0 = lambda name, g4: on_grad(name, 0, g4)
        dh, gs["ffn2_norm"][i] = _ffn_bwd(dh, s2, row("ffn2_norm", i), W, emit, "ffn2", f"l{i}_ffn2")
        if i == 0:
            dh, gs["mix_norm"][i], gm = _mla_bwd(dh, sm, P, weights_of(0), emit0, tabs, H)
        else:
            dh, gs["mix_norm"][i], gm = _dil_bwd(dh, sm, P, weights_of(0), emit0, slopes, Hd)
        gs.update({n: [val] for n, val in gm.items()})
        dh, gs["ffn1_norm"][i] = _ffn_bwd(dh, s1, row("ffn1_norm", i), W, emit, "ffn1", f"l{i}_ffn1")
    gsmall = {n: jnp.concatenate(v, axis=0) for n, v in gs.items()}
    return loss, dh[0], gsmall


def _pad_heads(w, real, padded):
    lead, n = w.shape[:-1], w.shape[-1] // real
    w = jnp.pad(w.reshape(*lead, n, real), [(0, 0)] * (len(lead) + 1) + [(0, padded - real)])
    return w.reshape(*lead, n * padded)


def _unpad_heads(w, real, padded):
    lead, n = w.shape[:-1], w.shape[-1] // padded
    return w.reshape(*lead, n, padded)[..., :real].reshape(*lead, n * real)


def _pack_small(gs):
    flat = jnp.concatenate([gs[n].reshape(-1) for n in SMALL])
    rows = -(-flat.shape[0] // LANES)
    rows = -(-rows // 8) * 8
    return jnp.pad(flat, (0, rows * LANES - flat.shape[0])).reshape(rows, LANES)


def _unpack_small(packed, like):
    flat, out, off = packed.reshape(-1), {}, 0
    for n in SMALL:
        size = int(np.prod(like[n].shape))
        out[n] = flat[off:off + size].reshape(like[n].shape)
        off += size
    return out


def kernel(x, ffn1_norm, ffn1_w_in, ffn1_w_out, mix_norm, ffn2_norm, ffn2_w_in, ffn2_w_out, mla_w_down, mla_g_cq, mla_g_ckv, mla_w_uq, mla_w_ukv, mla_g_qn, mla_g_kn, mla_w_o, dil_w_qkv, dil_g_qn, dil_g_kn, dil_w_o, loss_target, m_ffn1_norm, m_ffn1_w_in, m_ffn1_w_out, m_mix_norm, m_ffn2_norm, m_ffn2_w_in, m_ffn2_w_out, m_mla_w_down, m_mla_g_cq, m_mla_g_ckv, m_mla_w_uq, m_mla_w_ukv, m_mla_g_qn, m_mla_g_kn, m_mla_w_o, m_dil_w_qkv, m_dil_g_qn, m_dil_g_kn, m_dil_w_o, v_ffn1_norm, v_ffn1_w_in, v_ffn1_w_out, v_mix_norm, v_ffn2_norm, v_ffn2_w_in, v_ffn2_w_out, v_mla_w_down, v_mla_g_cq, v_mla_g_ckv, v_mla_w_uq, v_mla_w_ukv, v_mla_g_qn, v_mla_g_kn, v_mla_w_o, v_dil_w_qkv, v_dil_g_qn, v_dil_g_kn, v_dil_w_o):
    args = dict(locals())
    w = {n: args[n] for n in WEIGHTS}
    m = {n: args["m_" + n] for n in WEIGHTS}
    v = {n: args["v_" + n] for n in WEIGHTS}
    cx, cy, cc = _me()
    core = jnp.reshape(cc, (1,)).astype(jnp.int32)
    shard = jnp.reshape(2 * cx + cy, (1,)).astype(jnp.int32)
    place = (shard, core)
    pe_pad = LANES - ROPE_DIM

    order = [(n, layer if w[n].shape[0] > 1 else 0) for layer in range(2) for n in USE_ORDER[layer]]
    lands, over_ici, to_sibling = {}, {}, {}

    def cast(key, after):
        _, r, c = w[key[0]].shape
        lands[key] = cast_into_shards(w[key[0]], key[1], shard, after, "ag_%s%d_cast" % key).reshape(N_CHIPS, 1, 2, r // 2, c)
        return lands[key]

    def start(key, after):
        over_ici[key] = exchange_start((lands.pop(key),), ag_over_ici, after, "ag_%s%d_start" % key)
        return over_ici[key][2]

    def pass_on(key, after):
        if key in to_sibling:
            return after
        bufs = exchange_wait(over_ici[key], ag_over_ici, after, "ag_%s%d_wait" % key)
        to_sibling[key] = exchange_start(bufs, ag_to_sibling, after, "ag_%s%d_pass" % key)
        return to_sibling[key][2]

    def get_w(n, l, after):
        k = order.index((n, l))
        behind = after
        if k == 0:
            for key in order[:2]:
                cast(key, after)
            for key in order[:2]:
                behind = start(key, behind)
            for key in order[2:]:
                behind = cast(key, behind)
        else:
            for key in order[len(over_ici):min(k + AG_AHEAD, len(order) - 1) + 1]:
                behind = start(key, behind)
        behind = pass_on((n, l), behind)
        if 0 < k < len(order) - 1:
            behind = pass_on(order[k + 1], behind)
        (land,) = exchange_wait(to_sibling[n, l], ag_to_sibling, behind, f"ag_{n}{l}_passed")
        _, _, _, h, c = land.shape
        full = land.reshape(N_CHIPS, 1, 2 * h, c)
        if n == "mla_w_down":
            full = jnp.pad(full, ((0, 0), (0, 0), (0, 0), (0, pe_pad)))
        if n == "mla_w_uq":
            full = _pad_heads(full, QK_DIM, QK_PAD)
        return full

    outs = {n: None for n in BIG}

    def update(key, g):
        n, l = key
        outs[n] = adamw(w[n], g, m[n], v[n], l, outs[n], f"adamw_{n}{l}")

    reduce_scatter = ReduceScatter(place, update)

    def on_grad(n, l, g4):
        if n == "mla_w_down":
            g4 = g4[..., :g4.shape[-1] - pe_pad]
        if n == "mla_w_uq":
            g4 = _unpad_heads(g4, QK_DIM, QK_PAD)
        reduce_scatter.push((n, l), g4)

    loss, grad_x, gsmall = local_step(x[0], loss_target[0], {n: w[n] for n in SMALL}, get_w, on_grad)
    loss = lax.psum(loss, ("x", "y", "c"))
    reduce_scatter.drain(grad_x)
    small = _unpack_small(all_reduce_small(_pack_small(gsmall), "ar_small"), gsmall)
    for n in SMALL:
        outs[n] = tuple(o[0] for o in adamw(w[n][None], small[n], m[n][None], v[n][None], 0, None, f"adamw_{n}"))

    return (loss, grad_x[None], *[outs[n][0] for n in WEIGHTS], *[outs[n][1] for n in WEIGHTS],
            *[outs[n][2] for n in WEIGHTS], *[outs[n][3] for n in WEIGHTS])
```

```python
import math

import numpy as np
import jax
import jax.numpy as jnp
from jax import lax
from jax.experimental import pallas as pl
from jax.experimental.pallas import tpu as pltpu

MXU_DTYPE = jnp.bfloat16
WIRE_DTYPE = jnp.bfloat16
EPS = 1e-6
NEG = -1e30
N_CHIPS = 4
MESH = pl.DeviceIdType.MESH
ANY = pl.BlockSpec(memory_space=pl.ANY)
LANES = 128

MLA_HEADS = 16
NOPE_DIM = 128
ROPE_DIM = 64
QK_DIM = NOPE_DIM + ROPE_DIM
QK_PAD = 2 * LANES
PREP_ROWS = 1024
ATTN_BLOCK = 1024
AG_AHEAD = 3
RS_WINDOW = 2
ROPE_THETA = 10000.0
DIL_PAIRS = ((128, 1), (512, 4), (2048, 16))
DIL_HEADS = 8
BLK = 128

ADAM_LR = 0.001
ADAM_B1 = 0.9
ADAM_B2 = 0.999
ADAM_EPS = 1e-08
ADAM_WD = 0.01
ADAM_STEP = 10

NT = (((1,), (1,)), ((), ()))
TN = (((0,), (0,)), ((), ()))

SHARD_AXIS = {"ffn1_w_in": 1, "ffn1_w_out": 0, "ffn2_w_in": 1, "ffn2_w_out": 0, "mla_w_down": 0, "mla_w_uq": 1,
              "mla_w_ukv": 1, "mla_w_o": 0, "dil_w_qkv": 1, "dil_w_o": 1}
BIG = tuple(SHARD_AXIS)
USE_ORDER = (("ffn1_w_in", "ffn1_w_out", "mla_w_down", "mla_w_uq", "mla_w_ukv", "mla_w_o", "ffn2_w_in", "ffn2_w_out"),
             ("ffn1_w_in", "ffn1_w_out", "dil_w_qkv", "dil_w_o", "ffn2_w_in", "ffn2_w_out"))
SMALL = ("ffn1_norm", "mix_norm", "ffn2_norm", "mla_g_cq", "mla_g_ckv", "mla_g_qn", "mla_g_kn", "dil_g_qn", "dil_g_kn")
WEIGHTS = ("ffn1_norm", "ffn1_w_in", "ffn1_w_out", "mix_norm", "ffn2_norm", "ffn2_w_in", "ffn2_w_out", "mla_w_down",
           "mla_g_cq", "mla_g_ckv", "mla_w_uq", "mla_w_ukv", "mla_g_qn", "mla_g_kn", "mla_w_o", "dil_w_qkv", "dil_g_qn",
           "dil_g_kn", "dil_w_o")


def _tile(dim, pref, mult=LANES):
    if dim <= pref:
        return dim
    t = (pref // mult) * mult
    while t >= mult:
        if dim % t == 0:
            return t
        t -= mult
    return dim


def _column_chunks(width, parts=2):
    tiles = width // LANES
    assert width % LANES == 0, width
    bounds = [LANES * ((tiles * k + parts - 1) // parts) for k in range(parts + 1)]
    return [slice(lo, hi) for lo, hi in zip(bounds, bounds[1:]) if hi > lo]


def _params(*sem):
    return pltpu.CompilerParams(dimension_semantics=sem)


def _f32(shape):
    return jax.ShapeDtypeStruct(shape, jnp.float32)


def _act(shape):
    return jax.ShapeDtypeStruct(shape, MXU_DTYPE)


class Sharded:
    def __init__(self, arr, layer, axis):
        self.arr, self.layer, self.axis = arr, layer, axis
        n, _, r, c = arr.shape
        self.shape = (n * r, c) if axis == 0 else (r, n * c)
        self.per = r if axis == 0 else c

    def spec(self, tr, tc, rc_of):
        l = self.layer
        if self.axis == 0:
            n = self.per // tr

            def imap(*g):
                bi, bj = rc_of(*g)
                return (bi // n, l, bi % n, bj)
        else:
            n = self.per // tc

            def imap(*g):
                bi, bj = rc_of(*g)
                return (bj // n, l, bi, bj % n)
        return pl.BlockSpec((None, None, tr, tc), imap)


def _spec2(tr, tc, rc_of):
    return pl.BlockSpec((tr, tc), lambda *g: rc_of(*g))


def matmul(a, b, *, ta=False, tb=False, out_dtype=jnp.float32, scale=None, res=None, out_axis=None,
           name, tm=1024, tn=1024, tk=2048):
    am, ak = (a.shape[1], a.shape[0]) if ta else a.shape
    bk, bn = (b.shape[1], b.shape[0]) if tb else b.shape
    assert ak == bk, (name, a.shape, b.shape, ta, tb)
    M, N, K = am, bn, ak

    def per(x, axis):
        return x.per if isinstance(x, Sharded) and x.axis == axis else None

    def pick(dim, pref, *pers):
        return _tile(math.gcd(dim, *[p for p in pers if p is not None]), pref)

    tm = pick(M, tm, per(a, 1 if ta else 0), M // N_CHIPS if out_axis == 0 else None)
    tn = pick(N, tn, per(b, 0 if tb else 1), N // N_CHIPS if out_axis == 1 else None)
    tk = pick(K, tk, per(a, 0 if ta else 1), per(b, 1 if tb else 0))
    assert M % tm == 0 and N % tn == 0 and K % tk == 0, (name, M, N, K, tm, tn, tk)
    nk = K // tk

    a_rc = (lambda i, j, k: (k, i)) if ta else (lambda i, j, k: (i, k))
    b_rc = (lambda i, j, k: (j, k)) if tb else (lambda i, j, k: (k, j))
    a_blk = (tk, tm) if ta else (tm, tk)
    b_blk = (tn, tk) if tb else (tk, tn)
    a_spec = a.spec(*a_blk, a_rc) if isinstance(a, Sharded) else _spec2(*a_blk, a_rc)
    b_spec = b.spec(*b_blk, b_rc) if isinstance(b, Sharded) else _spec2(*b_blk, b_rc)
    dn = (((0 if ta else 1,), (1 if tb else 0,)), ((), ()))
    has_res = res is not None

    def body(*refs):
        if has_res:
            a_ref, b_ref, r_ref, o_ref, acc = refs
        else:
            a_ref, b_ref, o_ref, acc = refs
        k = pl.program_id(2)

        @pl.when(k == 0)
        def _():
            acc[...] = jnp.zeros_like(acc)

        acc[...] += lax.dot_general(a_ref[...].astype(MXU_DTYPE), b_ref[...].astype(MXU_DTYPE), dn,
                                    preferred_element_type=jnp.float32)

        @pl.when(k == nk - 1)
        def _():
            r = acc[...]
            if scale is not None:
                r = r * scale
            if has_res:
                r = r + r_ref[...]
            o_ref[...] = r.astype(o_ref.dtype)

    in_specs = [a_spec, b_spec]
    args = [a.arr if isinstance(a, Sharded) else a, b.arr if isinstance(b, Sharded) else b]
    if has_res:
        in_specs.append(_spec2(tm, tn, lambda i, j, k: (i, j)))
        args.append(res)
    o_rc = lambda i, j, k: (i, j)
    if out_axis is None:
        out_shape = jax.ShapeDtypeStruct((M, N), out_dtype)
        out_spec = _spec2(tm, tn, o_rc)
    else:
        shp = (N_CHIPS, 1, M // N_CHIPS, N) if out_axis == 0 else (N_CHIPS, 1, M, N // N_CHIPS)
        out_shape = jax.ShapeDtypeStruct(shp, out_dtype)
        out_spec = Sharded(out_shape, 0, out_axis).spec(tm, tn, o_rc)
    return pl.pallas_call(
        body, name=name, out_shape=out_shape, grid=(M // tm, N // tn, nk),
        in_specs=in_specs, out_specs=out_spec,
        scratch_shapes=[pltpu.VMEM((tm, tn), jnp.float32)],
        compiler_params=_params("parallel", "parallel", "arbitrary"),
    )(*args)


def _me():
    return lax.axis_index("x"), lax.axis_index("y"), lax.axis_index("c")


def _other_chips(x, y):
    return [(1 - x, y), (x, 1 - y), (1 - x, 1 - y)]


HBM = pl.BlockSpec(memory_space=pltpu.HBM)
SEM = pl.BlockSpec(memory_space=pltpu.SEMAPHORE)
N_PEERS = 3
TOKEN = jax.ShapeDtypeStruct((8, LANES), jnp.float32)


def _split_params():
    return pltpu.CompilerParams(has_side_effects=pltpu.SideEffectType.DATAFLOW_SIDE_EFFECTING)


def _in_hbm(a):
    return pltpu.with_memory_space_constraint(a, pltpu.HBM)


def exchange_start(bufs, plan, after, name):
    nb, n = len(bufs), plan.copies

    def body(*refs):
        sems, token = refs[nb + 1:nb + 1 + 2 * n], refs[-1]
        for j, (src, dst, to) in enumerate(plan(refs[:nb], _me(), False)):
            pltpu.make_async_remote_copy(src_ref=src, dst_ref=dst, send_sem=sems[j], recv_sem=sems[n + j],
                                         device_id=to, device_id_type=MESH).start()
        token[...] = jnp.zeros_like(token)

    outs = pl.pallas_call(
        body, name=name,
        out_shape=(pltpu.SemaphoreType.DMA(()),) * (2 * n) + tuple(pltpu.HBM(b.shape, b.dtype) for b in bufs) + (TOKEN,),
        in_specs=(HBM,) * nb + (ANY,), out_specs=(SEM,) * (2 * n) + (HBM,) * nb + (pl.BlockSpec(memory_space=pltpu.VMEM),),
        input_output_aliases={k: 2 * n + k for k in range(nb)}, compiler_params=_split_params(),
    )(*[_in_hbm(b) for b in bufs], after)
    return outs[:2 * n], outs[2 * n:2 * n + nb], outs[-1]


def exchange_wait(started, plan, after, name):
    sems, bufs, _ = started
    nb, n = len(bufs), plan.copies

    def body(*refs):
        sems_ = refs[nb:nb + 2 * n]
        for j, (src, got, to) in enumerate(plan(refs[:nb], _me(), True)):
            cp = pltpu.make_async_remote_copy(src_ref=src, dst_ref=got, send_sem=sems_[j], recv_sem=sems_[n + j],
                                              device_id=to, device_id_type=MESH)
            cp.wait_send()
            cp.wait_recv()

    return pl.pallas_call(
        body, name=name, out_shape=tuple(pltpu.HBM(b.shape, b.dtype) for b in bufs),
        in_specs=(HBM,) * nb + (SEM,) * (2 * n) + (ANY,), out_specs=(HBM,) * nb,
        input_output_aliases={k: k for k in range(nb)}, compiler_params=_split_params(),
    )(*bufs, *sems, after)


def _plan(copies):
    def mark(f):
        f.copies = copies
        return f
    return mark


@_plan(N_PEERS)
def ag_over_ici(refs, me, arrived):
    (land,), (x, y, cc) = refs, me
    mine = land.at[2 * x + y, :, cc]
    return [(mine, land.at[2 * px + py, :, cc] if arrived else mine, (px, py, cc)) for px, py in _other_chips(x, y)]


@_plan(N_PEERS)
def ag_to_sibling(refs, me, arrived):
    (land,), (x, y, cc) = refs, me
    return [(land.at[2 * px + py, :, cc], land.at[2 * px + py, :, (1 - cc) if arrived else cc], (x, y, 1 - cc))
            for px, py in _other_chips(x, y)]


@_plan(1)
def rs_to_sibling(refs, me, arrived):
    (g, r1), (x, y, cc) = refs, me
    return [(g.at[:, :, 1 - cc], r1, (x, y, 1 - cc))]


@_plan(N_PEERS)
def rs_over_ici(refs, me, arrived):
    (p, land), (x, y, cc) = refs, me
    return [(p.at[2 * px + py], land.at[j], (px, py, cc)) for j, (px, py) in enumerate(_other_chips(x, y))]


@_plan(1)
def rs_gather_sibling(refs, me, arrived):
    (red,), (x, y, cc) = refs, me
    return [(red.at[:, cc], red.at[:, (1 - cc) if arrived else cc], (x, y, 1 - cc))]


def cast_into_shards(w, layer, shard, after, name):
    L, r, c = w.shape
    tr, tc = _tile(r, 1024, 16), _tile(c, 2048)

    def body(shard_ref, w_ref, after_ref, o_ref):
        o_ref[...] = w_ref[...].astype(o_ref.dtype)

    grid_spec = pltpu.PrefetchScalarGridSpec(
        num_scalar_prefetch=1, grid=(r // tr, c // tc),
        in_specs=[pl.BlockSpec((None, tr, tc), lambda i, j, sh: (layer, i, j)), ANY],
        out_specs=pl.BlockSpec((None, None, tr, tc), lambda i, j, sh: (sh[0], 0, i, j)))
    return pl.pallas_call(body, name=name, grid_spec=grid_spec, out_shape=jax.ShapeDtypeStruct((N_CHIPS, 1, r, c), WIRE_DTYPE),
                          compiler_params=_params("parallel", "parallel"))(shard, w, after)


def add_sibling(g, r1, core, name):
    n, L, two, h, c = g.shape
    th = _tile(h, 1024, 16)
    tc = _tile(c, 2048)

    def body(core_ref, g_ref, r_ref, o_ref):
        o_ref[...] = (g_ref[...].astype(jnp.float32) + r_ref[...].astype(jnp.float32)).astype(o_ref.dtype)

    grid_spec = pltpu.PrefetchScalarGridSpec(
        num_scalar_prefetch=1, grid=(n, L, h // th, c // tc),
        in_specs=[pl.BlockSpec((None, None, None, th, tc), lambda s, l, i, j, core: (s, l, core[0], i, j)),
                  pl.BlockSpec((None, None, th, tc), lambda s, l, i, j, core: (s, l, i, j))],
        out_specs=pl.BlockSpec((None, None, th, tc), lambda s, l, i, j, core: (s, l, i, j)))
    return pl.pallas_call(body, name=name, grid_spec=grid_spec, out_shape=jax.ShapeDtypeStruct((n, L, h, c), WIRE_DTYPE),
                          compiler_params=_params("parallel", "parallel", "parallel", "parallel"))(core, g, r1)


def add_chips(p, r2, place, name):
    n, L, h, c = p.shape
    th = _tile(h, 1024, 16)
    tc = _tile(c, 2048)

    def body(shard_ref, core_ref, p_ref, r_ref, o_ref):
        acc = p_ref[...].astype(jnp.float32)
        for j in range(3):
            acc = acc + r_ref[j].astype(jnp.float32)
        o_ref[...] = acc

    grid_spec = pltpu.PrefetchScalarGridSpec(
        num_scalar_prefetch=2, grid=(L, h // th, c // tc),
        in_specs=[pl.BlockSpec((None, None, th, tc), lambda l, i, j, shard, core: (shard[0], l, i, j)),
                  pl.BlockSpec((3, None, th, tc), lambda l, i, j, shard, core: (0, l, i, j))],
        out_specs=pl.BlockSpec((None, None, th, tc), lambda l, i, j, shard, core: (l, core[0], i, j)))
    return pl.pallas_call(body, name=name, grid_spec=grid_spec, out_shape=_f32((L, 2, h, c)),
                          compiler_params=_params("parallel", "parallel", "parallel"))(*place, p, r2)


class ReduceScatter:
    def __init__(self, place, done):
        self.place, self.done = place, done
        self.stages = [[], [], []]
        self.token = None

    def _start(self, bufs, plan, name):
        started = exchange_start(bufs, plan, self.token if self.token is not None else bufs[0], name)
        self.token = started[2]
        return started

    def _advance(self, stage, after):
        key, started = self.stages[stage].pop(0)
        name = "rs_%s%d" % key
        if stage == 0:
            g, r1 = exchange_wait(started, rs_to_sibling, after, name + "_d2d_wait")
            p = add_sibling(g, r1, self.place[1], name + "_add1")
            land = lax.empty((N_PEERS,) + p.shape[1:], p.dtype)
            self.stages[1].append((key, self._start((p, land), rs_over_ici, name + "_ici_start")))
        elif stage == 1:
            p, r2 = exchange_wait(started, rs_over_ici, after, name + "_ici_wait")
            red = add_chips(p, r2, self.place, name + "_add2")
            self.stages[2].append((key, self._start((red,), rs_gather_sibling, name + "_gather_start")))
        else:
            (red,) = exchange_wait(started, rs_gather_sibling, after, name + "_gather_wait")
            L, two, h, c = red.shape
            self.done(key, red.reshape(2 * h, c))

    def push(self, key, g4):
        n, L, r, c = g4.shape
        g = g4.reshape(n, L, 2, r // 2, c)
        r1 = lax.empty((n, L, r // 2, c), g.dtype)
        self.stages[0].append((key, self._start((g, r1), rs_to_sibling, "rs_%s%d_d2d_start" % key)))
        for stage, depth in ((2, 1), (1, RS_WINDOW), (0, 1)):
            if len(self.stages[stage]) > depth:
                self._advance(stage, self.token)

    def drain(self, after):
        for stage in (0, 1, 2):
            while self.stages[stage]:
                self._advance(stage, after)


def all_reduce_small(v, name):
    R, C = v.shape

    def body(v_ref, o_ref, buf, send_sems, recv_sems):
        x, y, cc = _me()
        buf[0] = v_ref[...]
        cps = []
        for k in range(1, 8):
            dx, dy, dc = (k >> 2) & 1, (k >> 1) & 1, k & 1
            to = (x ^ dx, y ^ dy, cc ^ dc)
            cp = pltpu.make_async_remote_copy(src_ref=v_ref, dst_ref=buf.at[k], send_sem=send_sems.at[k],
                                              recv_sem=recv_sems.at[k], device_id=to, device_id_type=MESH)
            cp.start()
            cps.append(cp)
        for cp in cps:
            cp.wait()
        me = 4 * x + 2 * y + cc
        acc = buf[me]
        for a in range(1, 8):
            acc = acc + buf[a ^ me]
        o_ref[...] = acc

    vm = pl.BlockSpec(memory_space=pltpu.VMEM)
    return pl.pallas_call(
        body, name=name, out_shape=_f32((R, C)), in_specs=[vm], out_specs=vm,
        scratch_shapes=[pltpu.VMEM((8, R, C), jnp.float32), pltpu.SemaphoreType.DMA((8,)), pltpu.SemaphoreType.DMA((8,))],
    )(v)


def _rstd(x, n):
    return lax.rsqrt(jnp.sum(x * x, axis=-1, keepdims=True) * (1.0 / n) + EPS)


def _accumulate(ref, part, first):
    @pl.when(first)
    def _():
        ref[...] = part

    @pl.when(jnp.logical_not(first))
    def _():
        ref[...] += part


def rmsnorm_fwd(x, g, name):
    S, D = x.shape
    ts = _tile(S, 256, 8)

    def body(x_ref, g_ref, o_ref):
        xv = x_ref[...]
        o_ref[...] = (xv * _rstd(xv, D) * g_ref[...]).astype(o_ref.dtype)

    return pl.pallas_call(
        body, name=name, out_shape=_act((S, D)), grid=(S // ts,),
        in_specs=[pl.BlockSpec((ts, D), lambda i: (i, 0)), pl.BlockSpec((1, D), lambda i: (0, 0))],
        out_specs=pl.BlockSpec((ts, D), lambda i: (i, 0)), compiler_params=_params("parallel"))(x, g)


def _norm_bwd(x, g, dy, n):
    r = _rstd(x, n)
    xh = x * r
    dxh = dy * g
    dx = r * (dxh - xh * (jnp.sum(dxh * xh, axis=-1, keepdims=True) * (1.0 / n)))
    return dx, dy * xh


def rmsnorm_bwd(x, g, dy, dres, name):
    S, D = x.shape
    ts = _tile(S, 256, 8)

    def body(x_ref, g_ref, dy_ref, dres_ref, dx_ref, dg_ref):
        dx, dgp = _norm_bwd(x_ref[...], g_ref[...], dy_ref[...], D)
        dx_ref[...] = dres_ref[...] + dx
        _accumulate(dg_ref, jnp.sum(dgp, axis=0, keepdims=True), pl.program_id(0) == 0)

    row = pl.BlockSpec((ts, D), lambda i: (i, 0))
    vec = pl.BlockSpec((1, D), lambda i: (0, 0))
    return pl.pallas_call(
        body, name=name, out_shape=(_f32((S, D)), _f32((1, D))), grid=(S // ts,),
        in_specs=[row, vec, row, row], out_specs=(row, vec), compiler_params=_params("arbitrary"))(x, g, dy, dres)


def _sigmoid(x):
    return 1.0 / (1.0 + jnp.exp(-x))


def swiglu_in(xn, w_in, name, tm=512, tn=1408):
    S, D = xn.shape
    F = w_in.shape[1] // 2
    tm, tn = _tile(S, tm, 8), _tile(math.gcd(F, w_in.per), tn)
    nf = F // tn

    def body(x_ref, wg_ref, wu_ref, u_ref, a_ref):
        x = x_ref[...].astype(MXU_DTYPE)
        for cols in _column_chunks(tn):
            gt = jnp.dot(x, wg_ref[:, cols], preferred_element_type=jnp.float32)
            up = jnp.dot(x, wu_ref[:, cols], preferred_element_type=jnp.float32)
            u_ref[0, :, cols] = gt.astype(u_ref.dtype)
            u_ref[1, :, cols] = up.astype(u_ref.dtype)
            a_ref[:, cols] = (gt * _sigmoid(gt) * up).astype(a_ref.dtype)

    return pl.pallas_call(
        body, name=name, out_shape=(_act((2, 1, S, F)), _act((S, F))), grid=(nf, S // tm),
        in_specs=[pl.BlockSpec((tm, D), lambda j, i: (i, 0)), w_in.spec(D, tn, lambda j, i: (0, j)),
                  w_in.spec(D, tn, lambda j, i: (0, j + nf))],
        out_specs=(pl.BlockSpec((2, None, tm, tn), lambda j, i: (0, 0, i, j)), pl.BlockSpec((tm, tn), lambda j, i: (i, j))),
        compiler_params=_params("parallel", "parallel"))(xn, w_in.arr, w_in.arr)


def swiglu_out_bwd(dout, w_out, u, scale, name, tm=512, tn=1408):
    S, D = dout.shape
    F = w_out.shape[0]
    tm, tn = _tile(S, tm, 8), _tile(math.gcd(F, w_out.per), tn)

    def body(d_ref, w_ref, u_ref, o_ref):
        d = d_ref[...].astype(MXU_DTYPE)
        for cols in _column_chunks(tn):
            da = lax.dot_general(d, w_ref[cols, :], NT, preferred_element_type=jnp.float32) * scale
            gt, up = u_ref[0, :, cols].astype(jnp.float32), u_ref[1, :, cols].astype(jnp.float32)
            s = _sigmoid(gt)
            o_ref[0, :, cols] = (da * up * (s * (1.0 + gt * (1.0 - s)))).astype(o_ref.dtype)
            o_ref[1, :, cols] = (da * (gt * s)).astype(o_ref.dtype)

    planes = pl.BlockSpec((2, None, tm, tn), lambda j, i: (0, 0, i, j))
    return pl.pallas_call(
        body, name=name, out_shape=_act((2, 1, S, F)), grid=(F // tn, S // tm),
        in_specs=[pl.BlockSpec((tm, D), lambda j, i: (i, 0)), w_out.spec(tn, D, lambda j, i: (j, 0)), planes],
        out_specs=planes, compiler_params=_params("parallel", "parallel"))(dout, w_out.arr, u)


def loss_head(y, t, name):
    S, D = y.shape
    ts = _tile(S, 256, 8)

    def body(y_ref, t_ref, dy_ref, l_ref):
        e = y_ref[...] - t_ref[...]
        dy_ref[...] = e * (1.0 / D)
        l_ref[...] = jnp.full(l_ref.shape, 0.5 * jnp.sum(jnp.sum(e * e, axis=-1, keepdims=True) * (1.0 / D)), jnp.float32)

    row = pl.BlockSpec((ts, D), lambda i: (i, 0))
    dy, parts = pl.pallas_call(
        body, name=name, out_shape=(_f32((S, D)), _f32((S // ts, 8, LANES))), grid=(S // ts,),
        in_specs=[row, row], out_specs=(row, pl.BlockSpec((None, 8, LANES), lambda i: (i, 0, 0))),
        compiler_params=_params("parallel"))(y, t)
    return jnp.sum(parts[:, 0, 0]), dy


def rope_tables(S):
    inv = 1.0 / (ROPE_THETA ** (jnp.arange(0, ROPE_DIM, 2, dtype=jnp.float32) / ROPE_DIM))
    ang = jnp.arange(S, dtype=jnp.float32)[:, None] * inv[None, :]
    c, s = jnp.cos(ang), jnp.sin(ang)
    z = jnp.zeros_like(c)
    return (jnp.concatenate([c, c, z, z], axis=1), jnp.concatenate([-s, z, z, z], axis=1),
            jnp.concatenate([z, s, z, z], axis=1))


def _rope(x, cos, sa, sb):
    return x * cos + pltpu.roll(x, 96, 1) * sa + pltpu.roll(x, 32, 1) * sb


def _rope_t(d, cos, sa, sb):
    return d * cos + pltpu.roll(d * sa, 32, 1) + pltpu.roll(d * sb, 96, 1)


def _head_norm(x1, x2, g):
    r = lax.rsqrt((jnp.sum(x1 * x1, axis=-1, keepdims=True) + jnp.sum(x2 * x2, axis=-1, keepdims=True)) * (1.0 / QK_DIM) + EPS)
    return x1 * r * g[:, :LANES], x2 * r * g[:, LANES:], r


def _head_norm_bwd(x1, x2, g, d1, d2):
    _, _, r = _head_norm(x1, x2, g)
    h1, h2 = x1 * r, x2 * r
    e1, e2 = d1 * g[:, :LANES], d2 * g[:, LANES:]
    m = (jnp.sum(e1 * h1, axis=-1, keepdims=True) + jnp.sum(e2 * h2, axis=-1, keepdims=True)) * (1.0 / QK_DIM)
    return r * (e1 - h1 * m), r * (e2 - h2 * m), d1 * h1, d2 * h2


def mla_latent_fwd(lat, g_cq, g_ckv, name):
    S, W = lat.shape
    QL, KL = g_cq.shape[1], g_ckv.shape[1]
    ts = _tile(S, 256, 8)

    def body(l_ref, gq_ref, gk_ref, cq_ref, ckv_ref):
        a, b = l_ref[:, :QL], l_ref[:, QL:QL + KL]
        cq_ref[...] = (a * _rstd(a, QL) * gq_ref[...]).astype(cq_ref.dtype)
        ckv_ref[...] = (b * _rstd(b, KL) * gk_ref[...]).astype(ckv_ref.dtype)

    return pl.pallas_call(
        body, name=name, out_shape=(_act((S, QL)), _act((S, KL))), grid=(S // ts,),
        in_specs=[pl.BlockSpec((ts, W), lambda i: (i, 0)), pl.BlockSpec((1, QL), lambda i: (0, 0)),
                  pl.BlockSpec((1, KL), lambda i: (0, 0))],
        out_specs=(pl.BlockSpec((ts, QL), lambda i: (i, 0)), pl.BlockSpec((ts, KL), lambda i: (i, 0))),
        compiler_params=_params("parallel"))(lat, g_cq, g_ckv)


def mla_latent_bwd(dcq, dckv, dkpe, lat, g_cq, g_ckv, name):
    S, W = lat.shape
    QL, KL = g_cq.shape[1], g_ckv.shape[1]
    ts = _tile(S, 256, 8)

    def body(dq_ref, dk_ref, dp_ref, l_ref, gq_ref, gk_ref, o_ref, dgq_ref, dgk_ref):
        first = pl.program_id(0) == 0
        da, ga = _norm_bwd(l_ref[:, :QL], gq_ref[...], dq_ref[...], QL)
        db, gb = _norm_bwd(l_ref[:, QL:QL + KL], gk_ref[...], dk_ref[...], KL)
        o_ref[:, :QL] = da.astype(o_ref.dtype)
        o_ref[:, QL:QL + KL] = db.astype(o_ref.dtype)
        o_ref[:, QL + KL:] = dp_ref[...].astype(o_ref.dtype)
        _accumulate(dgq_ref, jnp.sum(ga, axis=0, keepdims=True), first)
        _accumulate(dgk_ref, jnp.sum(gb, axis=0, keepdims=True), first)

    row = lambda n: pl.BlockSpec((ts, n), lambda i: (i, 0))
    vec = lambda n: pl.BlockSpec((1, n), lambda i: (0, 0))
    return pl.pallas_call(
        body, name=name, out_shape=(_act((S, W)), _f32((1, QL)), _f32((1, KL))), grid=(S // ts,),
        in_specs=[row(QL), row(KL), row(LANES), row(W), vec(QL), vec(KL)], out_specs=(row(W), vec(QL), vec(KL)),
        compiler_params=_params("arbitrary"))(dcq, dckv, dkpe, lat, g_cq, g_ckv)


def mla_q_prep_fwd(qraw, g, tabs, H, name):
    S = qraw.shape[0]
    ts = _tile(S, PREP_ROWS, 8)

    def body(x_ref, g_ref, c_ref, a_ref, b_ref, o_ref):
        y1, y2, _ = _head_norm(x_ref[:, :LANES], x_ref[:, LANES:], g_ref[...])
        o_ref[:, :LANES] = y1.astype(o_ref.dtype)
        o_ref[:, LANES:] = _rope(y2, c_ref[...], a_ref[...], b_ref[...]).astype(o_ref.dtype)

    tab = pl.BlockSpec((ts, LANES), lambda i, h: (i, 0))
    return pl.pallas_call(
        body, name=name, out_shape=_act((H, S, QK_PAD)), grid=(S // ts, H),
        in_specs=[pl.BlockSpec((ts, QK_PAD), lambda i, h: (i, h)), pl.BlockSpec((1, QK_PAD), lambda i, h: (0, 0)), tab, tab, tab],
        out_specs=pl.BlockSpec((None, ts, QK_PAD), lambda i, h: (h, i, 0)),
        compiler_params=_params("parallel", "parallel"))(qraw, g, *tabs)


def mla_q_prep_bwd(dq, qraw, g, tabs, H, name):
    S = qraw.shape[0]
    ts = _tile(S, PREP_ROWS, 8)

    def body(d_ref, x_ref, g_ref, c_ref, a_ref, b_ref, o_ref, dg_ref):
        d2 = _rope_t(d_ref[:, LANES:], c_ref[...], a_ref[...], b_ref[...])
        dx1, dx2, g1, g2 = _head_norm_bwd(x_ref[:, :LANES], x_ref[:, LANES:], g_ref[...], d_ref[:, :LANES], d2)
        o_ref[:, :LANES] = dx1.astype(o_ref.dtype)
        o_ref[:, LANES:] = dx2.astype(o_ref.dtype)
        first = jnp.logical_and(pl.program_id(0) == 0, pl.program_id(1) == 0)
        part = jnp.concatenate([jnp.sum(g1, axis=0, keepdims=True), jnp.sum(g2, axis=0, keepdims=True)], axis=1)
        _accumulate(dg_ref, part, first)

    tab = pl.BlockSpec((ts, LANES), lambda i, h: (i, 0))
    vec = pl.BlockSpec((1, QK_PAD), lambda i, h: (0, 0))
    return pl.pallas_call(
        body, name=name, out_shape=(_act((S, H * QK_PAD)), _f32((1, QK_PAD))), grid=(S // ts, H),
        in_specs=[pl.BlockSpec((None, ts, QK_PAD), lambda i, h: (h, i, 0)), pl.BlockSpec((ts, QK_PAD), lambda i, h: (i, h)),
                  vec, tab, tab, tab],
        out_specs=(pl.BlockSpec((ts, QK_PAD), lambda i, h: (i, h)), vec),
        compiler_params=_params("arbitrary", "arbitrary"))(dq, qraw, g, *tabs)


def mla_k_prep_fwd(kvraw, lat, g, tabs, H, pe_blk, name):
    S = kvraw.shape[0]
    ts = _tile(S, PREP_ROWS, 8)

    def body(x_ref, p_ref, g_ref, c_ref, a_ref, b_ref, k_ref, v_ref):
        y1, y2, _ = _head_norm(x_ref[:, :LANES], p_ref[...], g_ref[...])
        k_ref[:, :LANES] = y1.astype(k_ref.dtype)
        k_ref[:, LANES:] = _rope(y2, c_ref[...], a_ref[...], b_ref[...]).astype(k_ref.dtype)
        v_ref[...] = x_ref[:, LANES:].astype(v_ref.dtype)

    tab = pl.BlockSpec((ts, LANES), lambda i, h: (i, 0))
    return pl.pallas_call(
        body, name=name, out_shape=(_act((H, S, QK_PAD)), _act((H, S, LANES))), grid=(S // ts, H),
        in_specs=[pl.BlockSpec((ts, QK_PAD), lambda i, h: (i, h)), pl.BlockSpec((ts, LANES), lambda i, h: (i, pe_blk)),
                  pl.BlockSpec((1, QK_PAD), lambda i, h: (0, 0)), tab, tab, tab],
        out_specs=(pl.BlockSpec((None, ts, QK_PAD), lambda i, h: (h, i, 0)), pl.BlockSpec((None, ts, LANES), lambda i, h: (h, i, 0))),
        compiler_params=_params("parallel", "parallel"))(kvraw, lat, g, *tabs)


def mla_k_prep_bwd(dk, dv, kvraw, lat, g, tabs, H, pe_blk, name):
    S = kvraw.shape[0]
    ts = _tile(S, PREP_ROWS, 8)

    def body(dk_ref, dv_ref, x_ref, p_ref, g_ref, c_ref, a_ref, b_ref, o_ref, dp_ref, dg_ref):
        i, h = pl.program_id(0), pl.program_id(1)
        d2 = _rope_t(dk_ref[:, LANES:], c_ref[...], a_ref[...], b_ref[...])
        dx1, dx2, g1, g2 = _head_norm_bwd(x_ref[:, :LANES], p_ref[...], g_ref[...], dk_ref[:, :LANES], d2)
        o_ref[:, :LANES] = dx1.astype(o_ref.dtype)
        o_ref[:, LANES:] = dv_ref[...].astype(o_ref.dtype)
        _accumulate(dp_ref, dx2, h == 0)
        part = jnp.concatenate([jnp.sum(g1, axis=0, keepdims=True), jnp.sum(g2, axis=0, keepdims=True)], axis=1)
        _accumulate(dg_ref, part, jnp.logical_and(i == 0, h == 0))

    tab = pl.BlockSpec((ts, LANES), lambda i, h: (i, 0))
    vec = pl.BlockSpec((1, QK_PAD), lambda i, h: (0, 0))
    return pl.pallas_call(
        body, name=name, out_shape=(_act((S, H * QK_PAD)), _f32((S, LANES)), _f32((1, QK_PAD))), grid=(S // ts, H),
        in_specs=[pl.BlockSpec((None, ts, QK_PAD), lambda i, h: (h, i, 0)), pl.BlockSpec((None, ts, LANES), lambda i, h: (h, i, 0)),
                  pl.BlockSpec((ts, QK_PAD), lambda i, h: (i, h)), pl.BlockSpec((ts, LANES), lambda i, h: (i, pe_blk)),
                  vec, tab, tab, tab],
        out_specs=(pl.BlockSpec((ts, QK_PAD), lambda i, h: (i, h)), tab, vec),
        compiler_params=_params("arbitrary", "arbitrary"))(dk, dv, kvraw, lat, g, *tabs)


def _causal_scores(q, k, scale, diagonal):
    s = lax.dot_general(q, k, NT, preferred_element_type=jnp.float32) * scale
    if not diagonal:
        return s
    row = lax.broadcasted_iota(jnp.int32, s.shape, 0)
    col = lax.broadcasted_iota(jnp.int32, s.shape, 1)
    return jnp.where(col <= row, s, NEG)


def _on_causal_blocks(qi, ki, step):
    @pl.when(ki < qi)
    def _():
        step(False)

    @pl.when(ki == qi)
    def _():
        step(True)


def mla_attention_fwd(q, k, v, name):
    H, S, _ = q.shape
    t = _tile(S, ATTN_BLOCK)
    n = S // t
    scale = 1.0 / math.sqrt(QK_DIM)

    def body(q_ref, k_ref, v_ref, o_ref, lse_ref, m_sc, l_sc, acc):
        qi, ki = pl.program_id(1), pl.program_id(2)

        @pl.when(ki == 0)
        def _():
            m_sc[...] = jnp.full(m_sc.shape, NEG, jnp.float32)
            l_sc[...] = jnp.zeros_like(l_sc)
            acc[...] = jnp.zeros_like(acc)

        def step(diagonal):
            s = _causal_scores(q_ref[...], k_ref[...], scale, diagonal)
            m_new = jnp.maximum(m_sc[...], jnp.max(s, axis=-1, keepdims=True))
            alpha = jnp.exp(m_sc[...] - m_new)
            p = jnp.exp(s - m_new)
            l_sc[...] = alpha * l_sc[...] + jnp.sum(p, axis=-1, keepdims=True)
            acc[...] = alpha * acc[...] + jnp.dot(p.astype(MXU_DTYPE), v_ref[...], preferred_element_type=jnp.float32)
            m_sc[...] = m_new

        _on_causal_blocks(qi, ki, step)

        @pl.when(ki == qi)
        def _():
            o_ref[...] = (acc[...] / l_sc[...]).astype(o_ref.dtype)
            lse_ref[...] = m_sc[...] + jnp.log(l_sc[...])

    kv = lambda w: pl.BlockSpec((None, t, w), lambda h, qi, ki: (h, jnp.minimum(ki, qi), 0))
    return pl.pallas_call(
        body, name=name, out_shape=(_act((S, H * LANES)), _f32((H, S, 1))), grid=(H, n, n),
        in_specs=[pl.BlockSpec((None, t, QK_PAD), lambda h, qi, ki: (h, qi, 0)), kv(QK_PAD), kv(LANES)],
        out_specs=(pl.BlockSpec((t, LANES), lambda h, qi, ki: (qi, h)), pl.BlockSpec((None, t, 1), lambda h, qi, ki: (h, qi, 0))),
        scratch_shapes=[pltpu.VMEM((t, 1), jnp.float32), pltpu.VMEM((t, 1), jnp.float32), pltpu.VMEM((t, LANES), jnp.float32)],
        compiler_params=_params("parallel", "parallel", "arbitrary"))(q, k, v)


def mla_attention_bwd(q, k, v, do, o, lse, name):
    H, S, _ = q.shape
    t = _tile(S, ATTN_BLOCK)
    n = S // t
    scale = 1.0 / math.sqrt(QK_DIM)

    def body(q_ref, k_ref, v_ref, do_ref, o_ref, lse_ref, dq_ref, dk_ref, dv_ref, dk_acc, dv_acc):
        ki, qi = pl.program_id(1), pl.program_id(2)
        rows = pl.ds(pl.multiple_of(qi * t, t), t)

        @pl.when(qi == 0)
        def _():
            dk_acc[...] = jnp.zeros_like(dk_acc)
            dv_acc[...] = jnp.zeros_like(dv_acc)

        @pl.when(ki == 0)
        def _():
            dq_ref[rows, :] = jnp.zeros((t, QK_PAD), jnp.float32)

        def step(diagonal):
            p = jnp.exp(_causal_scores(q_ref[...], k_ref[...], scale, diagonal) - lse_ref[...])
            dof = do_ref[...]
            dob = dof.astype(MXU_DTYPE)
            delta = jnp.sum(dof * o_ref[...].astype(jnp.float32), axis=-1, keepdims=True)
            dv_acc[...] += lax.dot_general(p.astype(MXU_DTYPE), dob, TN, preferred_element_type=jnp.float32)
            dp = lax.dot_general(dob, v_ref[...], NT, preferred_element_type=jnp.float32)
            ds = (p * (dp - delta)).astype(MXU_DTYPE)
            dk_acc[...] += lax.dot_general(ds, q_ref[...], TN, preferred_element_type=jnp.float32)
            dq_ref[rows, :] += jnp.dot(ds, k_ref[...], preferred_element_type=jnp.float32) * scale

        _on_causal_blocks(qi, ki, step)

        @pl.when(qi == n - 1)
        def _():
            dk_ref[...] = dk_acc[...] * scale
            dv_ref[...] = dv_acc[...]

    qrow = lambda h, ki, qi: (h, jnp.maximum(qi, ki), 0)
    kv = lambda w: pl.BlockSpec((None, t, w), lambda h, ki, qi: (h, ki, 0))
    col = pl.BlockSpec((None, t, 1), qrow)
    head = pl.BlockSpec((t, LANES), lambda h, ki, qi: (jnp.maximum(qi, ki), h))
    return pl.pallas_call(
        body, name=name, out_shape=(_f32((H, S, QK_PAD)), _f32((H, S, QK_PAD)), _f32((H, S, LANES))), grid=(H, n, n),
        in_specs=[pl.BlockSpec((None, t, QK_PAD), qrow), kv(QK_PAD), kv(LANES),
                  head, head, col],
        out_specs=(pl.BlockSpec((None, S, QK_PAD), lambda h, ki, qi: (h, 0, 0)), kv(QK_PAD), kv(LANES)),
        scratch_shapes=[pltpu.VMEM((t, QK_PAD), jnp.float32), pltpu.VMEM((t, LANES), jnp.float32)],
        compiler_params=_params("parallel", "arbitrary", "arbitrary"))(q, k, v, do, o, lse)


def _alibi_slopes(G, Hd):
    k = np.arange(1, G * Hd + 1, dtype=np.float32)
    s = (2.0 ** (-8.0 * k / (G * Hd))).astype(np.float32).reshape(G, Hd)
    return jnp.asarray(np.broadcast_to(s[:, :, None, None], (G, Hd, 1, LANES)).copy())


def _dil_scores(qn, kn, scale, slope_d, prev, valid):
    s = lax.dot_general(qn, kn, NT, preferred_element_type=jnp.float32) * scale
    iq = lax.broadcasted_iota(jnp.int32, s.shape, 0)
    ik = lax.broadcasted_iota(jnp.int32, s.shape, 1)
    dist = iq - ik + (BLK if prev else 0)
    ok = (ik >= iq) if prev else (ik <= iq)
    s = s - slope_d * dist.astype(jnp.float32)
    return jnp.where(jnp.logical_and(ok, valid), s, NEG)


def _dil_scores_both(qn, k2, scale, slope_d, has_prev):
    s = lax.dot_general(qn, k2, NT, preferred_element_type=jnp.float32) * scale
    iq = lax.broadcasted_iota(jnp.int32, s.shape, 0)
    ik = lax.broadcasted_iota(jnp.int32, s.shape, 1)
    dist = iq + BLK - ik
    ok = jnp.logical_and(jnp.logical_and(dist >= 0, dist <= BLK), jnp.logical_or(ik >= BLK, has_prev))
    return jnp.where(ok, s - slope_d * dist.astype(jnp.float32), NEG)


def _dil_heads(d, Hd, block_bytes):
    hb = max(1, min(Hd, block_bytes // (BLK * LANES * 4))) if d == 1 else 1
    assert Hd % hb == 0, (Hd, hb)
    return hb


def _dil_rows(r, d):
    return pl.ds(r, BLK, stride=d) if d > 1 else slice(None)


def _loop_residues(d, residue, init):
    per = math.gcd(d, 4)

    def one_pass(i, carry):
        for k in range(per):
            carry = residue(per * i + k, carry)
        return carry

    return one_pass(0, init) if d == per else lax.fori_loop(0, d // per, one_pass, init)


def _dil_specs(d, nblk, Hd, G, g, hb):
    def spec(kind, shift):
        col0 = (kind * G + g) * Hd // hb
        return pl.BlockSpec((BLK * d, hb * LANES), lambda n, hg: (jnp.clip(n + shift, 0, nblk - 1), col0 + hg))
    return spec


def _head_spec(d, nblk, hb, shift):
    return pl.BlockSpec((BLK * d, hb * LANES), lambda n, hg: (jnp.clip(n + shift, 0, nblk - 1), hg))


def dilated_fwd(qkv, gq, gk, slopes, g, d, Hd, G, name):
    S, C = qkv.shape
    nblk = S // (BLK * d)
    hb = _dil_heads(d, Hd, 2 << 20)
    scale = 1.0 / math.sqrt(LANES)
    spec = _dil_specs(d, nblk, Hd, G, g, hb)

    def body(q_ref, kc_ref, kp_ref, vc_ref, vp_ref, gq_ref, gk_ref, sl_ref, o_ref, l_ref):
        n = pl.program_id(0)
        nrm = lambda t, gg: (t * _rstd(t, LANES) * gg).astype(MXU_DTYPE)

        def residue(r, carry):
            rows = _dil_rows(r, d)
            for hh in range(hb):
                cols = slice(hh * LANES, (hh + 1) * LANES)
                slope_d = sl_ref[hh][:, :1] * float(d)
                qn = nrm(q_ref[rows, cols], gq_ref[...])
                k2 = jnp.concatenate([nrm(kp_ref[rows, cols], gk_ref[...]), nrm(kc_ref[rows, cols], gk_ref[...])], axis=0)
                v2 = jnp.concatenate([vp_ref[rows, cols], vc_ref[rows, cols]], axis=0).astype(MXU_DTYPE)
                s = _dil_scores_both(qn, k2, scale, slope_d, n > 0)
                m = jnp.max(s, axis=-1, keepdims=True)
                e = jnp.exp(s - m)
                tot = jnp.sum(e, axis=-1, keepdims=True)
                o = jnp.dot((e * (1.0 / tot)).astype(MXU_DTYPE), v2, preferred_element_type=jnp.float32)
                o_ref[rows, cols] = o
                l_ref[rows, cols] = jnp.broadcast_to(m + jnp.log(tot), (BLK, LANES))
            return carry

        _loop_residues(d, residue, 0)

    vec = pl.BlockSpec((1, LANES), lambda n, hg: (0, 0))
    out = _head_spec(d, nblk, hb, 0)
    return pl.pallas_call(
        body, name=name, out_shape=(_f32((S, Hd * LANES)), _f32((S, Hd * LANES))), grid=(nblk, Hd // hb),
        in_specs=[spec(0, 0), spec(1, 0), spec(1, -1), spec(2, 0), spec(2, -1), vec, vec,
                  pl.BlockSpec((None, hb, 1, LANES), lambda n, hg: (g, hg, 0, 0))],
        out_specs=(out, out), compiler_params=_params("parallel", "parallel"),
    )(qkv, qkv, qkv, qkv, qkv, gq, gk, slopes)


def dilated_merge(os_, ls_, name):
    S, W = os_[0].shape
    G = len(os_)
    ts, tw = _tile(S, 512, 8), _tile(W, 512)

    def body(*refs):
        o_refs, l_refs, (o_ref, t_ref) = refs[:G], refs[G:2 * G], refs[2 * G:]
        ls = [r[...] for r in l_refs]
        m = ls[0]
        for l in ls[1:]:
            m = jnp.maximum(m, l)
        es = [jnp.exp(l - m) for l in ls]
        tot = es[0]
        for e in es[1:]:
            tot = tot + e
        acc = o_refs[0][...] * (es[0] / tot)
        for r, e in zip(o_refs[1:], es[1:]):
            acc = acc + r[...] * (e / tot)
        o_ref[...] = acc.astype(o_ref.dtype)
        t_ref[...] = m + jnp.log(tot)

    blk = pl.BlockSpec((ts, tw), lambda i, j: (i, j))
    return pl.pallas_call(
        body, name=name, out_shape=(_act((S, W)), _f32((S, W))), grid=(S // ts, W // tw),
        in_specs=[blk] * (2 * G), out_specs=(blk, blk), compiler_params=_params("parallel", "parallel"))(*os_, *ls_)


def dilated_delta(do, o, name):
    S, W = do.shape
    ts = _tile(S, 512, 8)

    def body(d_ref, o_ref, out_ref):
        out_ref[...] = jnp.broadcast_to(jnp.sum(d_ref[...] * o_ref[...].astype(jnp.float32), axis=-1, keepdims=True), out_ref.shape)

    blk = pl.BlockSpec((ts, LANES), lambda i, h: (i, h))
    return pl.pallas_call(body, name=name, out_shape=_f32((S, W)), grid=(S // ts, W // LANES), in_specs=[blk, blk],
                          out_specs=blk, compiler_params=_params("parallel", "parallel"))(do, o)


def dilated_bwd(qkv, do, lse, delta, gq, gk, slopes, g, d, Hd, G, name):
    S, C = qkv.shape
    nblk = S // (BLK * d)
    W = Hd * LANES
    scale = 1.0 / math.sqrt(LANES)
    hb = _dil_heads(d, Hd, 1 << 20)
    spec = _dil_specs(d, nblk, Hd, G, g, hb)
    hspec = lambda shift: _head_spec(d, nblk, hb, shift)

    def body(q_ref, qx_ref, kc_ref, kp_ref, vc_ref, vp_ref, do_ref, dox_ref, l_ref, lx_ref, dl_ref, dlx_ref,
             gq_ref, gk_ref, sl_ref, dq_ref, dk_ref, dv_ref, dgq_ref, dgk_ref):
        n = pl.program_id(0)
        gqv, gkv = gq_ref[...], gk_ref[...]
        nrm = lambda t, gg: (t * _rstd(t, LANES) * gg).astype(MXU_DTYPE)
        f32dot = lambda a, b, dn: lax.dot_general(a, b, dn, preferred_element_type=jnp.float32)

        def residue(r, carry):
            rows = _dil_rows(r, d)
            dgq_sum, dgk_sum = carry
            for hh in range(hb):
                cols = slice(hh * LANES, (hh + 1) * LANES)
                slope_d = sl_ref[hh][:, :1] * float(d)
                q, kc = q_ref[rows, cols], kc_ref[rows, cols]
                qn, qxn = nrm(q, gqv), nrm(qx_ref[rows, cols], gqv)
                kcn, kpn = nrm(kc, gkv), nrm(kp_ref[rows, cols], gkv)
                vc, vp = vc_ref[rows, cols].astype(MXU_DTYPE), vp_ref[rows, cols].astype(MXU_DTYPE)
                dob, doxb = do_ref[rows, cols].astype(MXU_DTYPE), dox_ref[rows, cols].astype(MXU_DTYPE)
                lrow, lxrow = l_ref[rows, cols][:, :1], lx_ref[rows, cols][:, :1]
                drow, dxrow = dl_ref[rows, cols][:, :1], dlx_ref[rows, cols][:, :1]
                pc = jnp.exp(_dil_scores(qn, kcn, scale, slope_d, False, True) - lrow)
                pp = jnp.exp(_dil_scores(qn, kpn, scale, slope_d, True, n > 0) - lrow)
                dsc = pc * (f32dot(dob, vc, NT) - drow)
                dsp = pp * (f32dot(dob, vp, NT) - drow)
                dqn = (jnp.dot(dsc.astype(MXU_DTYPE), kcn, preferred_element_type=jnp.float32)
                       + jnp.dot(dsp.astype(MXU_DTYPE), kpn, preferred_element_type=jnp.float32)) * scale
                dq, dgq = _norm_bwd(q, gqv, dqn, LANES)
                dq_ref[rows, cols] = dq
                px = jnp.exp(_dil_scores(qxn, kcn, scale, slope_d, True, n < nblk - 1) - lxrow)
                dsx = px * (f32dot(doxb, vc, NT) - dxrow)
                dkn = (f32dot(dsc.astype(MXU_DTYPE), qn, TN) + f32dot(dsx.astype(MXU_DTYPE), qxn, TN)) * scale
                dk, dgk = _norm_bwd(kc, gkv, dkn, LANES)
                dk_ref[rows, cols] = dk
                dv_ref[rows, cols] = f32dot(pc.astype(MXU_DTYPE), dob, TN) + f32dot(px.astype(MXU_DTYPE), doxb, TN)
                dgq_sum = dgq_sum + jnp.sum(dgq, axis=0, keepdims=True)
                dgk_sum = dgk_sum + jnp.sum(dgk, axis=0, keepdims=True)
            return dgq_sum, dgk_sum

        zero = jnp.zeros((1, LANES), jnp.float32)
        dgq_sum, dgk_sum = _loop_residues(d, residue, (zero, zero))
        first = jnp.logical_and(n == 0, pl.program_id(1) == 0)
        _accumulate(dgq_ref, dgq_sum, first)
        _accumulate(dgk_ref, dgk_sum, first)

    vec = pl.BlockSpec((1, LANES), lambda n, hg: (0, 0))
    out = hspec(0)
    return pl.pallas_call(
        body, name=name, out_shape=(_f32((S, W)), _f32((S, W)), _f32((S, W)), _f32((1, LANES)), _f32((1, LANES))),
        grid=(nblk, Hd // hb),
        in_specs=[spec(0, 0), spec(0, 1), spec(1, 0), spec(1, -1), spec(2, 0), spec(2, -1), hspec(0), hspec(1), hspec(0), hspec(1),
                  hspec(0), hspec(1), vec, vec, pl.BlockSpec((None, hb, 1, LANES), lambda n, hg: (g, hg, 0, 0))],
        out_specs=(out, out, out, vec, vec), compiler_params=_params("arbitrary", "arbitrary"),
    )(qkv, qkv, qkv, qkv, qkv, qkv, do, do, lse, lse, delta, delta, gq, gk, slopes)


def adamw(w, g, m, v, layer, prev, name):
    L, r, c = w.shape
    tr, tc = _tile(r, 512, 8), _tile(c, 1024)
    c1 = 1.0 / (1.0 - ADAM_B1 ** ADAM_STEP)
    c2 = 1.0 / (1.0 - ADAM_B2 ** ADAM_STEP)

    def body(*refs):
        w_ref, g_ref, m_ref, v_ref = refs[:4]
        go_ref, d_ref, mo_ref, vo_ref = refs[-4:]
        gv = g_ref[...]
        mn = ADAM_B1 * m_ref[...] + (1.0 - ADAM_B1) * gv
        vn = ADAM_B2 * v_ref[...] + (1.0 - ADAM_B2) * (gv * gv)
        go_ref[...] = gv
        d_ref[...] = -ADAM_LR * ((mn * c1) / (jnp.sqrt(vn * c2) + ADAM_EPS) + ADAM_WD * w_ref[...])
        mo_ref[...] = mn
        vo_ref[...] = vn

    lay = pl.BlockSpec((None, tr, tc), lambda i, j: (layer, i, j))
    flat = pl.BlockSpec((tr, tc), lambda i, j: (i, j))
    ins = [w, g, m, v] + (list(prev) if prev is not None else [])
    in_specs = [lay, flat, lay, lay] + ([ANY] * 4 if prev is not None else [])
    return pl.pallas_call(
        body, name=name, out_shape=tuple(_f32((L, r, c)) for _ in range(4)), grid=(r // tr, c // tc),
        in_specs=in_specs, out_specs=(lay, lay, lay, lay),
        input_output_aliases=({4 + k: k for k in range(4)} if prev is not None else {}),
        compiler_params=_params("parallel", "parallel"))(*ins)


def _ffn_fwd(h, g, W, kind, tag):
    xn = rmsnorm_fwd(h, g, tag + "_norm")
    u, a = swiglu_in(xn, W(kind + "_w_in", h), tag + "_in")
    out = matmul(a, W(kind + "_w_out", a), scale=0.5, res=h, name=tag + "_out", tk=2816)
    return out, (h, xn, u, a)


def _ffn_bwd(dout, saved, g, W, emit, kind, tag):
    h, xn, u, a = saved
    emit(kind + "_w_out", matmul(a, dout, ta=True, scale=0.5, out_dtype=WIRE_DTYPE, out_axis=0, name=tag + "_dwout", tm=1408, tk=2048))
    du = Sharded(swiglu_out_bwd(dout, W(kind + "_w_out", None), u, 0.5, tag + "_da"), 0, 1)
    emit(kind + "_w_in", matmul(xn, du, ta=True, out_dtype=WIRE_DTYPE, out_axis=1, name=tag + "_dwin", tn=1408, tk=2048))
    dxn = matmul(du, W(kind + "_w_in", None), tb=True, name=tag + "_dxn", tk=2816)
    return rmsnorm_bwd(h, g, dxn, dout, tag + "_dnorm")


def _pad_gain(g):
    return jnp.pad(g, ((0, 0), (0, QK_PAD - QK_DIM)))


def _mla_fwd(h, P, W, tabs, H):
    g_mix, g_cq, g_ckv = P["mix_norm"][0:1], P["mla_g_cq"], P["mla_g_ckv"]
    pe_blk = (g_cq.shape[1] + g_ckv.shape[1]) // LANES
    xn = rmsnorm_fwd(h, g_mix, "mla_norm")
    w_down = W("mla_w_down", h)
    lat = matmul(xn, w_down, name="mla_down", tn=w_down.shape[1])
    cq, ckv = mla_latent_fwd(lat, g_cq, g_ckv, "mla_latent")
    qraw = matmul(cq, W("mla_w_uq", lat), name="mla_uq")
    kvraw = matmul(ckv, W("mla_w_ukv", qraw), name="mla_ukv")
    q = mla_q_prep_fwd(qraw, _pad_gain(P["mla_g_qn"]), tabs, H, "mla_qprep")
    k, v = mla_k_prep_fwd(kvraw, lat, _pad_gain(P["mla_g_kn"]), tabs, H, pe_blk, "mla_kprep")
    o, lse = mla_attention_fwd(q, k, v, "mla_attn")
    out = matmul(o, W("mla_w_o", lse), res=h, name="mla_o")
    return out, (h, xn, lat, cq, ckv, qraw, kvraw, q, k, v, o, lse, pe_blk)


def _mla_bwd(dout, saved, P, W, emit, tabs, H):
    h, xn, lat, cq, ckv, qraw, kvraw, q, k, v, o, lse, pe_blk = saved
    emit("mla_w_o", matmul(o, dout, ta=True, out_dtype=WIRE_DTYPE, out_axis=0, name="mla_dwo", tm=512))
    do = matmul(dout, W("mla_w_o", None), tb=True, name="mla_do", tn=512)
    dq, dk, dv = mla_attention_bwd(q, k, v, do, o, lse, "mla_attn_bwd")
    dqraw, dgq = mla_q_prep_bwd(dq, qraw, _pad_gain(P["mla_g_qn"]), tabs, H, "mla_dqprep")
    dkvraw, dkpe, dgk = mla_k_prep_bwd(dk, dv, kvraw, lat, _pad_gain(P["mla_g_kn"]), tabs, H, pe_blk, "mla_dkprep")
    emit("mla_w_uq", matmul(cq, dqraw, ta=True, out_dtype=WIRE_DTYPE, out_axis=1, name="mla_dwuq"))
    dcq = matmul(dqraw, W("mla_w_uq", None), tb=True, name="mla_dcq", tk=1024)
    emit("mla_w_ukv", matmul(ckv, dkvraw, ta=True, out_dtype=WIRE_DTYPE, out_axis=1, name="mla_dwukv"))
    dckv = matmul(dkvraw, W("mla_w_ukv", None), tb=True, name="mla_dckv", tk=1024)
    dlat, dgcq, dgckv = mla_latent_bwd(dcq, dckv, dkpe, lat, P["mla_g_cq"], P["mla_g_ckv"], "mla_dlatent")
    emit("mla_w_down", matmul(xn, dlat, ta=True, out_dtype=WIRE_DTYPE, out_axis=0, name="mla_dwdown", tm=512, tn=dlat.shape[1]))
    dxn = matmul(dlat, W("mla_w_down", None), tb=True, name="mla_dxn", tn=512, tk=dlat.shape[1])
    dh, dgm = rmsnorm_bwd(h, P["mix_norm"][0:1], dxn, dout, "mla_dnorm")
    return dh, dgm, dict(mla_g_qn=dgq[:, :QK_DIM], mla_g_kn=dgk[:, :QK_DIM], mla_g_cq=dgcq, mla_g_ckv=dgckv)


def _dil_fwd(h, P, W, slopes, Hd):
    G = len(DIL_PAIRS)
    xn = rmsnorm_fwd(h, P["mix_norm"][1:2], "dil_norm")
    qkv = matmul(xn, W("dil_w_qkv", h), name="dil_qkv", tn=1152)
    os_, ls_ = [], []
    for g, (_, d) in enumerate(DIL_PAIRS):
        o_g, l_g = dilated_fwd(qkv, P["dil_g_qn"], P["dil_g_kn"], slopes, g, d, Hd, G, f"dil_attn{g}")
        os_.append(o_g)
        ls_.append(l_g)
    o, lse = dilated_merge(os_, ls_, "dil_merge")
    out = matmul(o, W("dil_w_o", lse), res=h, name="dil_o", tn=512)
    return out, (h, xn, qkv, o, lse)


def _dil_bwd(dout, saved, P, W, emit, slopes, Hd):
    h, xn, qkv, o, lse = saved
    ngrp = len(DIL_PAIRS)
    emit("dil_w_o", matmul(o, dout, ta=True, out_dtype=WIRE_DTYPE, out_axis=1, name="dil_dwo", tn=512))
    do = matmul(dout, W("dil_w_o", None), tb=True, name="dil_do", tk=512)
    delta = dilated_delta(do, o, "dil_delta")
    parts = [dilated_bwd(qkv, do, lse, delta, P["dil_g_qn"], P["dil_g_kn"], slopes, g, d, Hd, ngrp, f"dil_dattn{g}")
             for g, (_, d) in enumerate(DIL_PAIRS)]
    dqkv = jnp.concatenate([p[kind] for kind in range(3) for p in parts], axis=1).astype(MXU_DTYPE)
    emit("dil_w_qkv", matmul(xn, dqkv, ta=True, out_dtype=WIRE_DTYPE, out_axis=1, name="dil_dwqkv", tn=1152))
    dxn = matmul(dqkv, W("dil_w_qkv", None), tb=True, name="dil_dxn", tk=2304)
    dh, dgm = rmsnorm_bwd(h, P["mix_norm"][1:2], dxn, dout, "dil_dnorm")
    return dh, dgm, dict(dil_g_qn=parts[0][3] + parts[1][3] + parts[2][3], dil_g_kn=parts[0][4] + parts[1][4] + parts[2][4])


def local_step(x, target, P, get_w, on_grad):
    S, D = x.shape
    H, Hd = MLA_HEADS, DIL_HEADS
    tabs = rope_tables(S)
    slopes = _alibi_slopes(len(DIL_PAIRS), Hd)
    cache = {}

    def weights_of(layer):
        def W(name, after):
            if (name, layer) not in cache:
                cache[name, layer] = Sharded(get_w(name, layer, after), 0, SHARD_AXIS[name])
            return cache[name, layer]
        return W

    row = lambda name, i: P[name][i:i + 1]
    h = x
    saved = []
    for i in range(2):
        W = weights_of(i)
        h, s1 = _ffn_fwd(h, row("ffn1_norm", i), W, "ffn1", f"l{i}_ffn1")
        h, sm = _mla_fwd(h, P, weights_of(0), tabs, H) if i == 0 else _dil_fwd(h, P, weights_of(0), slopes, Hd)
        h, s2 = _ffn_fwd(h, row("ffn2_norm", i), W, "ffn2", f"l{i}_ffn2")
        saved.append((s1, sm, s2))
    loss, dh = loss_head(h, target, "loss")

    gs = {n: [None, None] for n in ("ffn1_norm", "mix_norm", "ffn2_norm")}
    for i in (1, 0):
        s1, sm, s2 = saved[i]
        W = weights_of(i)
        emit = lambda name, g4, layer=i: on_grad(name, layer, g4)
        emit0 = lambda name, g4: on_grad(name, 0, g4)
        dh, gs["ffn2_norm"][i] = _ffn_bwd(dh, s2, row("ffn2_norm", i), W, emit, "ffn2", f"l{i}_ffn2")
        if i == 0:
            dh, gs["mix_norm"][i], gm = _mla_bwd(dh, sm, P, weights_of(0), emit0, tabs, H)
        else:
            dh, gs["mix_norm"][i], gm = _dil_bwd(dh, sm, P, weights_of(0), emit0, slopes, Hd)
        gs.update({n: [val] for n, val in gm.items()})
        dh, gs["ffn1_norm"][i] = _ffn_bwd(dh, s1, row("ffn1_norm", i), W, emit, "ffn1", f"l{i}_ffn1")
    gsmall = {n: jnp.concatenate(v, axis=0) for n, v in gs.items()}
    return loss, dh, gsmall


def _pad_heads(w, real, padded):
    lead, n = w.shape[:-1], w.shape[-1] // real
    w = jnp.pad(w.reshape(*lead, n, real), [(0, 0)] * (len(lead) + 1) + [(0, padded - real)])
    return w.reshape(*lead, n * padded)


def _unpad_heads(w, real, padded):
    lead, n = w.shape[:-1], w.shape[-1] // padded
    return w.reshape(*lead, n, padded)[..., :real].reshape(*lead, n * real)


def _pack_small(gs):
    flat = jnp.concatenate([gs[n].reshape(-1) for n in SMALL])
    rows = -(-flat.shape[0] // LANES)
    rows = -(-rows // 8) * 8
    return jnp.pad(flat, (0, rows * LANES - flat.shape[0])).reshape(rows, LANES)


def _unpack_small(packed, like):
    flat, out, off = packed.reshape(-1), {}, 0
    for n in SMALL:
        size = int(np.prod(like[n].shape))
        out[n] = flat[off:off + size].reshape(like[n].shape)
        off += size
    return out


def kernel(x, ffn1_norm, ffn1_w_in, ffn1_w_out, mix_norm, ffn2_norm, ffn2_w_in, ffn2_w_out, mla_w_down, mla_g_cq, mla_g_ckv, mla_w_uq, mla_w_ukv, mla_g_qn, mla_g_kn, mla_w_o, dil_w_qkv, dil_g_qn, dil_g_kn, dil_w_o, loss_target, m_ffn1_norm, m_ffn1_w_in, m_ffn1_w_out, m_mix_norm, m_ffn2_norm, m_ffn2_w_in, m_ffn2_w_out, m_mla_w_down, m_mla_g_cq, m_mla_g_ckv, m_mla_w_uq, m_mla_w_ukv, m_mla_g_qn, m_mla_g_kn, m_mla_w_o, m_dil_w_qkv, m_dil_g_qn, m_dil_g_kn, m_dil_w_o, v_ffn1_norm, v_ffn1_w_in, v_ffn1_w_out, v_mix_norm, v_ffn2_norm, v_ffn2_w_in, v_ffn2_w_out, v_mla_w_down, v_mla_g_cq, v_mla_g_ckv, v_mla_w_uq, v_mla_w_ukv, v_mla_g_qn, v_mla_g_kn, v_mla_w_o, v_dil_w_qkv, v_dil_g_qn, v_dil_g_kn, v_dil_w_o):
    args = dict(locals())
    w = {n: args[n] for n in WEIGHTS}
    m = {n: args["m_" + n] for n in WEIGHTS}
    v = {n: args["v_" + n] for n in WEIGHTS}
    cx, cy, cc = _me()
    core = jnp.reshape(cc, (1,)).astype(jnp.int32)
    shard = jnp.reshape(2 * cx + cy, (1,)).astype(jnp.int32)
    place = (shard, core)
    pe_pad = LANES - ROPE_DIM

    order = [(n, layer if w[n].shape[0] > 1 else 0) for layer in range(2) for n in USE_ORDER[layer]]
    lands, over_ici, to_sibling = {}, {}, {}

    def cast(key, after):
        _, r, c = w[key[0]].shape
        lands[key] = cast_into_shards(w[key[0]], key[1], shard, after, "ag_%s%d_cast" % key).reshape(N_CHIPS, 1, 2, r // 2, c)
        return lands[key]

    def start(key, after):
        over_ici[key] = exchange_start((lands.pop(key),), ag_over_ici, after, "ag_%s%d_start" % key)
        return over_ici[key][2]

    def pass_on(key, after):
        if key in to_sibling:
            return after
        bufs = exchange_wait(over_ici[key], ag_over_ici, after, "ag_%s%d_wait" % key)
        to_sibling[key] = exchange_start(bufs, ag_to_sibling, after, "ag_%s%d_pass" % key)
        return to_sibling[key][2]

    def get_w(n, l, after):
        k = order.index((n, l))
        behind = after
        if k == 0:
            for key in order[:2]:
                cast(key, after)
            for key in order[:2]:
                behind = start(key, behind)
            for key in order[2:]:
                behind = cast(key, behind)
        else:
            for key in order[len(over_ici):min(k + AG_AHEAD, len(order) - 1) + 1]:
                behind = start(key, behind)
        behind = pass_on((n, l), behind)
        if 0 < k < len(order) - 1:
            behind = pass_on(order[k + 1], behind)
        (land,) = exchange_wait(to_sibling[n, l], ag_to_sibling, behind, f"ag_{n}{l}_passed")
        _, _, _, h, c = land.shape
        full = land.reshape(N_CHIPS, 1, 2 * h, c)
        if n == "mla_w_down":
            full = jnp.pad(full, ((0, 0), (0, 0), (0, 0), (0, pe_pad)))
        if n == "mla_w_uq":
            full = _pad_heads(full, QK_DIM, QK_PAD)
        return full

    outs = {n: None for n in BIG}

    def update(key, g):
        n, l = key
        outs[n] = adamw(w[n], g, m[n], v[n], l, outs[n], f"adamw_{n}{l}")

    reduce_scatter = ReduceScatter(place, update)

    def on_grad(n, l, g4):
        if n == "mla_w_down":
            g4 = g4[..., :g4.shape[-1] - pe_pad]
        if n == "mla_w_uq":
            g4 = _unpad_heads(g4, QK_DIM, QK_PAD)
        reduce_scatter.push((n, l), g4)

    loss, grad_x, gsmall = local_step(x[0], loss_target[0], {n: w[n] for n in SMALL}, get_w, on_grad)
    loss = lax.psum(loss, ("x", "y", "c"))
    reduce_scatter.drain(grad_x)
    small = _unpack_small(all_reduce_small(_pack_small(gsmall), "ar_small"), gsmall)
    for n in SMALL:
        outs[n] = tuple(o[0] for o in adamw(w[n][None], small[n], m[n][None], v[n][None], 0, None, f"adamw_{n}"))

    return (loss, grad_x[None], *[outs[n][0] for n in WEIGHTS], *[outs[n][1] for n in WEIGHTS],
            *[outs[n][2] for n in WEIGHTS], *[outs[n][3] for n in WEIGHTS])
```

```python
import math

import numpy as np
import jax
import jax.numpy as jnp
from jax import lax
from jax.experimental import pallas as pl
from jax.experimental.pallas import tpu as pltpu

MXU_DTYPE = jnp.bfloat16
WIRE_DTYPE = jnp.bfloat16
EPS = 1e-6
NEG = -1e30
N_CHIPS = 4
MESH = pl.DeviceIdType.MESH
ANY = pl.BlockSpec(memory_space=pl.ANY)
LANES = 128

MLA_HEADS = 16
NOPE_DIM = 128
ROPE_DIM = 64
QK_DIM = NOPE_DIM + ROPE_DIM
QK_PAD = 2 * LANES
PREP_ROWS = 1024
ATTN_BLOCK = 1024
AG_AHEAD = 3
RS_WINDOW = 2
ROPE_THETA = 10000.0
DIL_PAIRS = ((128, 1), (512, 4), (2048, 16))
DIL_HEADS = 8
BLK = 128

ADAM_LR = 0.001
ADAM_B1 = 0.9
ADAM_B2 = 0.999
ADAM_EPS = 1e-08
ADAM_WD = 0.01
ADAM_STEP = 10

NT = (((1,), (1,)), ((), ()))
TN = (((0,), (0,)), ((), ()))

SHARD_AXIS = {"ffn1_w_in": 1, "ffn1_w_out": 0, "ffn2_w_in": 1, "ffn2_w_out": 0, "mla_w_down": 0, "mla_w_uq": 1,
              "mla_w_ukv": 1, "mla_w_o": 0, "dil_w_qkv": 1, "dil_w_o": 1}
BIG = tuple(SHARD_AXIS)
USE_ORDER = (("ffn1_w_in", "ffn1_w_out", "mla_w_down", "mla_w_uq", "mla_w_ukv", "mla_w_o", "ffn2_w_in", "ffn2_w_out"),
             ("ffn1_w_in", "ffn1_w_out", "dil_w_qkv", "dil_w_o", "ffn2_w_in", "ffn2_w_out"))
SMALL = ("ffn1_norm", "mix_norm", "ffn2_norm", "mla_g_cq", "mla_g_ckv", "mla_g_qn", "mla_g_kn", "dil_g_qn", "dil_g_kn")
WEIGHTS = ("ffn1_norm", "ffn1_w_in", "ffn1_w_out", "mix_norm", "ffn2_norm", "ffn2_w_in", "ffn2_w_out", "mla_w_down",
           "mla_g_cq", "mla_g_ckv", "mla_w_uq", "mla_w_ukv", "mla_g_qn", "mla_g_kn", "mla_w_o", "dil_w_qkv", "dil_g_qn",
           "dil_g_kn", "dil_w_o")


def _tile(dim, pref, mult=LANES):
    if dim <= pref:
        return dim
    t = (pref // mult) * mult
    while t >= mult:
        if dim % t == 0:
            return t
        t -= mult
    return dim


def _column_chunks(width, parts=2):
    tiles = width // LANES
    assert width % LANES == 0, width
    bounds = [LANES * ((tiles * k + parts - 1) // parts) for k in range(parts + 1)]
    return [slice(lo, hi) for lo, hi in zip(bounds, bounds[1:]) if hi > lo]


def _params(*sem):
    return pltpu.CompilerParams(dimension_semantics=sem)


def _f32(shape):
    return jax.ShapeDtypeStruct(shape, jnp.float32)


def _act(shape):
    return jax.ShapeDtypeStruct(shape, MXU_DTYPE)


class Sharded:
    def __init__(self, arr, layer, axis):
        self.arr, self.layer, self.axis = arr, layer, axis
        n, _, r, c = arr.shape
        self.shape = (n * r, c) if axis == 0 else (r, n * c)
        self.per = r if axis == 0 else c

    def spec(self, tr, tc, rc_of):
        l = self.layer
        if self.axis == 0:
            n = self.per // tr

            def imap(*g):
                bi, bj = rc_of(*g)
                return (bi // n, l, bi % n, bj)
        else:
            n = self.per // tc

            def imap(*g):
                bi, bj = rc_of(*g)
                return (bj // n, l, bi, bj % n)
        return pl.BlockSpec((None, None, tr, tc), imap)


def _spec2(tr, tc, rc_of):
    return pl.BlockSpec((tr, tc), lambda *g: rc_of(*g))


def matmul(a, b, *, ta=False, tb=False, out_dtype=jnp.float32, scale=None, res=None, out_axis=None,
           name, tm=1024, tn=1024, tk=2048):
    am, ak = (a.shape[1], a.shape[0]) if ta else a.shape
    bk, bn = (b.shape[1], b.shape[0]) if tb else b.shape
    assert ak == bk, (name, a.shape, b.shape, ta, tb)
    M, N, K = am, bn, ak

    def per(x, axis):
        return x.per if isinstance(x, Sharded) and x.axis == axis else None

    def pick(dim, pref, *pers):
        return _tile(math.gcd(dim, *[p for p in pers if p is not None]), pref)

    tm = pick(M, tm, per(a, 1 if ta else 0), M // N_CHIPS if out_axis == 0 else None)
    tn = pick(N, tn, per(b, 0 if tb else 1), N // N_CHIPS if out_axis == 1 else None)
    tk = pick(K, tk, per(a, 0 if ta else 1), per(b, 1 if tb else 0))
    assert M % tm == 0 and N % tn == 0 and K % tk == 0, (name, M, N, K, tm, tn, tk)
    nk = K // tk

    a_rc = (lambda i, j, k: (k, i)) if ta else (lambda i, j, k: (i, k))
    b_rc = (lambda i, j, k: (j, k)) if tb else (lambda i, j, k: (k, j))
    a_blk = (tk, tm) if ta else (tm, tk)
    b_blk = (tn, tk) if tb else (tk, tn)
    a_spec = a.spec(*a_blk, a_rc) if isinstance(a, Sharded) else _spec2(*a_blk, a_rc)
    b_spec = b.spec(*b_blk, b_rc) if isinstance(b, Sharded) else _spec2(*b_blk, b_rc)
    dn = (((0 if ta else 1,), (1 if tb else 0,)), ((), ()))
    has_res = res is not None

    def body(*refs):
        if has_res:
            a_ref, b_ref, r_ref, o_ref, acc = refs
        else:
            a_ref, b_ref, o_ref, acc = refs
        k = pl.program_id(2)

        @pl.when(k == 0)
        def _():
            acc[...] = jnp.zeros_like(acc)

        acc[...] += lax.dot_general(a_ref[...].astype(MXU_DTYPE), b_ref[...].astype(MXU_DTYPE), dn,
                                    preferred_element_type=jnp.float32)

        @pl.when(k == nk - 1)
        def _():
            r = acc[...]
            if scale is not None:
                r = r * scale
            if has_res:
                r = r + r_ref[...]
            o_ref[...] = r.astype(o_ref.dtype)

    in_specs = [a_spec, b_spec]
    args = [a.arr if isinstance(a, Sharded) else a, b.arr if isinstance(b, Sharded) else b]
    if has_res:
        in_specs.append(_spec2(tm, tn, lambda i, j, k: (i, j)))
        args.append(res)
    o_rc = lambda i, j, k: (i, j)
    if out_axis is None:
        out_shape = jax.ShapeDtypeStruct((M, N), out_dtype)
        out_spec = _spec2(tm, tn, o_rc)
    else:
        shp = (N_CHIPS, 1, M // N_CHIPS, N) if out_axis == 0 else (N_CHIPS, 1, M, N // N_CHIPS)
        out_shape = jax.ShapeDtypeStruct(shp, out_dtype)
        out_spec = Sharded(out_shape, 0, out_axis).spec(tm, tn, o_rc)
    return pl.pallas_call(
        body, name=name, out_shape=out_shape, grid=(M // tm, N // tn, nk),
        in_specs=in_specs, out_specs=out_spec,
        scratch_shapes=[pltpu.VMEM((tm, tn), jnp.float32)],
        compiler_params=_params("parallel", "parallel", "arbitrary"),
    )(*args)


def _me():
    return lax.axis_index("x"), lax.axis_index("y"), lax.axis_index("c")


def _other_chips(x, y):
    return [(1 - x, y), (x, 1 - y), (1 - x, 1 - y)]


HBM = pl.BlockSpec(memory_space=pltpu.HBM)
SEM = pl.BlockSpec(memory_space=pltpu.SEMAPHORE)
N_PEERS = 3
TOKEN = jax.ShapeDtypeStruct((8, LANES), jnp.float32)


def _split_params():
    return pltpu.CompilerParams(has_side_effects=pltpu.SideEffectType.DATAFLOW_SIDE_EFFECTING)


def _in_hbm(a):
    return pltpu.with_memory_space_constraint(a, pltpu.HBM)


def exchange_start(bufs, plan, after, name):
    nb, n = len(bufs), plan.copies

    def body(*refs):
        sems, token = refs[nb + 1:nb + 1 + 2 * n], refs[-1]
        for j, (src, dst, to) in enumerate(plan(refs[:nb], _me(), False)):
            pltpu.make_async_remote_copy(src_ref=src, dst_ref=dst, send_sem=sems[j], recv_sem=sems[n + j],
                                         device_id=to, device_id_type=MESH).start()
        token[...] = jnp.zeros_like(token)

    outs = pl.pallas_call(
        body, name=name,
        out_shape=(pltpu.SemaphoreType.DMA(()),) * (2 * n) + tuple(pltpu.HBM(b.shape, b.dtype) for b in bufs) + (TOKEN,),
        in_specs=(HBM,) * nb + (ANY,), out_specs=(SEM,) * (2 * n) + (HBM,) * nb + (pl.BlockSpec(memory_space=pltpu.VMEM),),
        input_output_aliases={k: 2 * n + k for k in range(nb)}, compiler_params=_split_params(),
    )(*[_in_hbm(b) for b in bufs], after)
    return outs[:2 * n], outs[2 * n:2 * n + nb], outs[-1]


def exchange_wait(started, plan, after, name):
    sems, bufs, _ = started
    nb, n = len(bufs), plan.copies

    def body(*refs):
        sems_ = refs[nb:nb + 2 * n]
        for j, (src, got, to) in enumerate(plan(refs[:nb], _me(), True)):
            cp = pltpu.make_async_remote_copy(src_ref=src, dst_ref=got, send_sem=sems_[j], recv_sem=sems_[n + j],
                                              device_id=to, device_id_type=MESH)
            cp.wait_send()
            cp.wait_recv()

    return pl.pallas_call(
        body, name=name, out_shape=tuple(pltpu.HBM(b.shape, b.dtype) for b in bufs),
        in_specs=(HBM,) * nb + (SEM,) * (2 * n) + (ANY,), out_specs=(HBM,) * nb,
        input_output_aliases={k: k for k in range(nb)}, compiler_params=_split_params(),
    )(*bufs, *sems, after)


def _plan(copies):
    def mark(f):
        f.copies = copies
        return f
    return mark


@_plan(N_PEERS)
def ag_over_ici(refs, me, arrived):
    (land,), (x, y, cc) = refs, me
    mine = land.at[2 * x + y, :, cc]
    return [(mine, land.at[2 * px + py, :, cc] if arrived else mine, (px, py, cc)) for px, py in _other_chips(x, y)]


@_plan(N_PEERS)
def ag_to_sibling(refs, me, arrived):
    (land,), (x, y, cc) = refs, me
    return [(land.at[2 * px + py, :, cc], land.at[2 * px + py, :, (1 - cc) if arrived else cc], (x, y, 1 - cc))
            for px, py in _other_chips(x, y)]


@_plan(1)
def rs_to_sibling(refs, me, arrived):
    (g, r1), (x, y, cc) = refs, me
    return [(g.at[:, :, 1 - cc], r1, (x, y, 1 - cc))]


@_plan(N_PEERS)
def rs_over_ici(refs, me, arrived):
    (p, land), (x, y, cc) = refs, me
    return [(p.at[2 * px + py], land.at[j], (px, py, cc)) for j, (px, py) in enumerate(_other_chips(x, y))]


@_plan(1)
def rs_gather_sibling(refs, me, arrived):
    (red,), (x, y, cc) = refs, me
    return [(red.at[:, cc], red.at[:, (1 - cc) if arrived else cc], (x, y, 1 - cc))]


def cast_into_shards(w, layer, shard, after, name):
    L, r, c = w.shape
    tr, tc = _tile(r, 1024, 16), _tile(c, 2048)

    def body(shard_ref, w_ref, after_ref, o_ref):
        o_ref[...] = w_ref[...].astype(o_ref.dtype)

    grid_spec = pltpu.PrefetchScalarGridSpec(
        num_scalar_prefetch=1, grid=(r // tr, c // tc),
        in_specs=[pl.BlockSpec((None, tr, tc), lambda i, j, sh: (layer, i, j)), ANY],
        out_specs=pl.BlockSpec((None, None, tr, tc), lambda i, j, sh: (sh[0], 0, i, j)))
    return pl.pallas_call(body, name=name, grid_spec=grid_spec, out_shape=jax.ShapeDtypeStruct((N_CHIPS, 1, r, c), WIRE_DTYPE),
                          compiler_params=_params("parallel", "parallel"))(shard, w, after)


def add_sibling(g, r1, core, name):
    n, L, two, h, c = g.shape
    th = _tile(h, 1024, 16)
    tc = _tile(c, 2048)

    def body(core_ref, g_ref, r_ref, o_ref):
        o_ref[...] = (g_ref[...].astype(jnp.float32) + r_ref[...].astype(jnp.float32)).astype(o_ref.dtype)

    grid_spec = pltpu.PrefetchScalarGridSpec(
        num_scalar_prefetch=1, grid=(n, L, h // th, c // tc),
        in_specs=[pl.BlockSpec((None, None, None, th, tc), lambda s, l, i, j, core: (s, l, core[0], i, j)),
                  pl.BlockSpec((None, None, th, tc), lambda s, l, i, j, core: (s, l, i, j))],
        out_specs=pl.BlockSpec((None, None, th, tc), lambda s, l, i, j, core: (s, l, i, j)))
    return pl.pallas_call(body, name=name, grid_spec=grid_spec, out_shape=jax.ShapeDtypeStruct((n, L, h, c), WIRE_DTYPE),
                          compiler_params=_params("parallel", "parallel", "parallel", "parallel"))(core, g, r1)


def add_chips(p, r2, place, name):
    n, L, h, c = p.shape
    th = _tile(h, 1024, 16)
    tc = _tile(c, 2048)

    def body(shard_ref, core_ref, p_ref, r_ref, o_ref):
        acc = p_ref[...].astype(jnp.float32)
        for j in range(3):
            acc = acc + r_ref[j].astype(jnp.float32)
        o_ref[...] = acc

    grid_spec = pltpu.PrefetchScalarGridSpec(
        num_scalar_prefetch=2, grid=(L, h // th, c // tc),
        in_specs=[pl.BlockSpec((None, None, th, tc), lambda l, i, j, shard, core: (shard[0], l, i, j)),
                  pl.BlockSpec((3, None, th, tc), lambda l, i, j, shard, core: (0, l, i, j))],
        out_specs=pl.BlockSpec((None, None, th, tc), lambda l, i, j, shard, core: (l, core[0], i, j)))
    return pl.pallas_call(body, name=name, grid_spec=grid_spec, out_shape=_f32((L, 2, h, c)),
                          compiler_params=_params("parallel", "parallel", "parallel"))(*place, p, r2)


class ReduceScatter:
    def __init__(self, place, done):
        self.place, self.done = place, done
        self.stages = [[], [], []]
        self.token = None

    def _start(self, bufs, plan, name):
        started = exchange_start(bufs, plan, self.token if self.token is not None else bufs[0], name)
        self.token = started[2]
        return started

    def _advance(self, stage, after):
        key, started = self.stages[stage].pop(0)
        name = "rs_%s%d" % key
        if stage == 0:
            g, r1 = exchange_wait(started, rs_to_sibling, after, name + "_d2d_wait")
            p = add_sibling(g, r1, self.place[1], name + "_add1")
            land = lax.empty((N_PEERS,) + p.shape[1:], p.dtype)
            self.stages[1].append((key, self._start((p, land), rs_over_ici, name + "_ici_start")))
        elif stage == 1:
            p, r2 = exchange_wait(started, rs_over_ici, after, name + "_ici_wait")
            red = add_chips(p, r2, self.place, name + "_add2")
            self.stages[2].append((key, self._start((red,), rs_gather_sibling, name + "_gather_start")))
        else:
            (red,) = exchange_wait(started, rs_gather_sibling, after, name + "_gather_wait")
            L, two, h, c = red.shape
            self.done(key, red.reshape(2 * h, c))

    def push(self, key, g4):
        n, L, r, c = g4.shape
        g = g4.reshape(n, L, 2, r // 2, c)
        r1 = lax.empty((n, L, r // 2, c), g.dtype)
        self.stages[0].append((key, self._start((g, r1), rs_to_sibling, "rs_%s%d_d2d_start" % key)))
        for stage, depth in ((2, 1), (1, RS_WINDOW), (0, 1)):
            if len(self.stages[stage]) > depth:
                self._advance(stage, self.token)

    def drain(self, after):
        for stage in (0, 1, 2):
            while self.stages[stage]:
                self._advance(stage, after)


def all_reduce_small(v, name):
    R, C = v.shape

    def body(v_ref, o_ref, buf, send_sems, recv_sems):
        x, y, cc = _me()
        buf[0] = v_ref[...]
        cps = []
        for k in range(1, 8):
            dx, dy, dc = (k >> 2) & 1, (k >> 1) & 1, k & 1
            to = (x ^ dx, y ^ dy, cc ^ dc)
            cp = pltpu.make_async_remote_copy(src_ref=v_ref, dst_ref=buf.at[k], send_sem=send_sems.at[k],
                                              recv_sem=recv_sems.at[k], device_id=to, device_id_type=MESH)
            cp.start()
            cps.append(cp)
        for cp in cps:
            cp.wait()
        me = 4 * x + 2 * y + cc
        acc = buf[me]
        for a in range(1, 8):
            acc = acc + buf[a ^ me]
        o_ref[...] = acc

    vm = pl.BlockSpec(memory_space=pltpu.VMEM)
    return pl.pallas_call(
        body, name=name, out_shape=_f32((R, C)), in_specs=[vm], out_specs=vm,
        scratch_shapes=[pltpu.VMEM((8, R, C), jnp.float32), pltpu.SemaphoreType.DMA((8,)), pltpu.SemaphoreType.DMA((8,))],
    )(v)


def _rstd(x, n):
    return lax.rsqrt(jnp.sum(x * x, axis=-1, keepdims=True) * (1.0 / n) + EPS)


def _accumulate(ref, part, first):
    @pl.when(first)
    def _():
        ref[...] = part

    @pl.when(jnp.logical_not(first))
    def _():
        ref[...] += part


def rmsnorm_fwd(x, g, name):
    S, D = x.shape
    ts = _tile(S, 256, 8)

    def body(x_ref, g_ref, o_ref):
        xv = x_ref[...]
        o_ref[...] = (xv * _rstd(xv, D) * g_ref[...]).astype(o_ref.dtype)

    return pl.pallas_call(
        body, name=name, out_shape=_act((S, D)), grid=(S // ts,),
        in_specs=[pl.BlockSpec((ts, D), lambda i: (i, 0)), pl.BlockSpec((1, D), lambda i: (0, 0))],
        out_specs=pl.BlockSpec((ts, D), lambda i: (i, 0)), compiler_params=_params("parallel"))(x, g)


def _norm_bwd(x, g, dy, n):
    r = _rstd(x, n)
    xh = x * r
    dxh = dy * g
    dx = r * (dxh - xh * (jnp.sum(dxh * xh, axis=-1, keepdims=True) * (1.0 / n)))
    return dx, dy * xh


def rmsnorm_bwd(x, g, dy, dres, name):
    S, D = x.shape
    ts = _tile(S, 256, 8)

    def body(x_ref, g_ref, dy_ref, dres_ref, dx_ref, dg_ref):
        dx, dgp = _norm_bwd(x_ref[...], g_ref[...], dy_ref[...], D)
        dx_ref[...] = dres_ref[...] + dx
        _accumulate(dg_ref, jnp.sum(dgp, axis=0, keepdims=True), pl.program_id(0) == 0)

    row = pl.BlockSpec((ts, D), lambda i: (i, 0))
    vec = pl.BlockSpec((1, D), lambda i: (0, 0))
    return pl.pallas_call(
        body, name=name, out_shape=(_f32((S, D)), _f32((1, D))), grid=(S // ts,),
        in_specs=[row, vec, row, row], out_specs=(row, vec), compiler_params=_params("arbitrary"))(x, g, dy, dres)


def _sigmoid(x):
    return 1.0 / (1.0 + jnp.exp(-x))


def swiglu_in(xn, w_in, name, tm=512, tn=1408):
    S, D = xn.shape
    F = w_in.shape[1] // 2
    tm, tn = _tile(S, tm, 8), _tile(math.gcd(F, w_in.per), tn)
    nf = F // tn

    def body(x_ref, wg_ref, wu_ref, u_ref, a_ref):
        x = x_ref[...].astype(MXU_DTYPE)
        for cols in _column_chunks(tn):
            gt = jnp.dot(x, wg_ref[:, cols], preferred_element_type=jnp.float32)
            up = jnp.dot(x, wu_ref[:, cols], preferred_element_type=jnp.float32)
            u_ref[0, :, cols] = gt.astype(u_ref.dtype)
            u_ref[1, :, cols] = up.astype(u_ref.dtype)
            a_ref[:, cols] = (gt * _sigmoid(gt) * up).astype(a_ref.dtype)

    return pl.pallas_call(
        body, name=name, out_shape=(_act((2, 1, S, F)), _act((S, F))), grid=(nf, S // tm),
        in_specs=[pl.BlockSpec((tm, D), lambda j, i: (i, 0)), w_in.spec(D, tn, lambda j, i: (0, j)),
                  w_in.spec(D, tn, lambda j, i: (0, j + nf))],
        out_specs=(pl.BlockSpec((2, None, tm, tn), lambda j, i: (0, 0, i, j)), pl.BlockSpec((tm, tn), lambda j, i: (i, j))),
        compiler_params=_params("parallel", "parallel"))(xn, w_in.arr, w_in.arr)


def swiglu_out_bwd(dout, w_out, u, scale, name, tm=512, tn=1408):
    S, D = dout.shape
    F = w_out.shape[0]
    tm, tn = _tile(S, tm, 8), _tile(math.gcd(F, w_out.per), tn)

    def body(d_ref, w_ref, u_ref, o_ref):
        d = d_ref[...].astype(MXU_DTYPE)
        for cols in _column_chunks(tn):
            da = lax.dot_general(d, w_ref[cols, :], NT, preferred_element_type=jnp.float32) * scale
            gt, up = u_ref[0, :, cols].astype(jnp.float32), u_ref[1, :, cols].astype(jnp.float32)
            s = _sigmoid(gt)
            o_ref[0, :, cols] = (da * up * (s * (1.0 + gt * (1.0 - s)))).astype(o_ref.dtype)
            o_ref[1, :, cols] = (da * (gt * s)).astype(o_ref.dtype)

    planes = pl.BlockSpec((2, None, tm, tn), lambda j, i: (0, 0, i, j))
    return pl.pallas_call(
        body, name=name, out_shape=_act((2, 1, S, F)), grid=(F // tn, S // tm),
        in_specs=[pl.BlockSpec((tm, D), lambda j, i: (i, 0)), w_out.spec(tn, D, lambda j, i: (j, 0)), planes],
        out_specs=planes, compiler_params=_params("parallel", "parallel"))(dout, w_out.arr, u)


def loss_head(y, t, name):
    S, D = y.shape
    ts = _tile(S, 256, 8)

    def body(y_ref, t_ref, dy_ref, l_ref):
        e = y_ref[...] - t_ref[...]
        dy_ref[...] = e * (1.0 / D)
        l_ref[...] = jnp.full(l_ref.shape, 0.5 * jnp.sum(jnp.sum(e * e, axis=-1, keepdims=True) * (1.0 / D)), jnp.float32)

    row = pl.BlockSpec((ts, D), lambda i: (i, 0))
    dy, parts = pl.pallas_call(
        body, name=name, out_shape=(_f32((S, D)), _f32((S // ts, 8, LANES))), grid=(S // ts,),
        in_specs=[row, row], out_specs=(row, pl.BlockSpec((None, 8, LANES), lambda i: (i, 0, 0))),
        compiler_params=_params("parallel"))(y, t)
    return jnp.sum(parts[:, 0, 0]), dy


def rope_tables(S):
    inv = 1.0 / (ROPE_THETA ** (jnp.arange(0, ROPE_DIM, 2, dtype=jnp.float32) / ROPE_DIM))
    ang = jnp.arange(S, dtype=jnp.float32)[:, None] * inv[None, :]
    c, s = jnp.cos(ang), jnp.sin(ang)
    z = jnp.zeros_like(c)
    return (jnp.concatenate([c, c, z, z], axis=1), jnp.concatenate([-s, z, z, z], axis=1),
            jnp.concatenate([z, s, z, z], axis=1))


def _rope(x, cos, sa, sb):
    return x * cos + pltpu.roll(x, 96, 1) * sa + pltpu.roll(x, 32, 1) * sb


def _rope_t(d, cos, sa, sb):
    return d * cos + pltpu.roll(d * sa, 32, 1) + pltpu.roll(d * sb, 96, 1)


def _head_norm(x1, x2, g):
    r = lax.rsqrt((jnp.sum(x1 * x1, axis=-1, keepdims=True) + jnp.sum(x2 * x2, axis=-1, keepdims=True)) * (1.0 / QK_DIM) + EPS)
    return x1 * r * g[:, :LANES], x2 * r * g[:, LANES:], r


def _head_norm_bwd(x1, x2, g, d1, d2):
    _, _, r = _head_norm(x1, x2, g)
    h1, h2 = x1 * r, x2 * r
    e1, e2 = d1 * g[:, :LANES], d2 * g[:, LANES:]
    m = (jnp.sum(e1 * h1, axis=-1, keepdims=True) + jnp.sum(e2 * h2, axis=-1, keepdims=True)) * (1.0 / QK_DIM)
    return r * (e1 - h1 * m), r * (e2 - h2 * m), d1 * h1, d2 * h2


def mla_latent_fwd(lat, g_cq, g_ckv, name):
    S, W = lat.shape
    QL, KL = g_cq.shape[1], g_ckv.shape[1]
    ts = _tile(S, 256, 8)

    def body(l_ref, gq_ref, gk_ref, cq_ref, ckv_ref):
        a, b = l_ref[:, :QL], l_ref[:, QL:QL + KL]
        cq_ref[...] = (a * _rstd(a, QL) * gq_ref[...]).astype(cq_ref.dtype)
        ckv_ref[...] = (b * _rstd(b, KL) * gk_ref[...]).astype(ckv_ref.dtype)

    return pl.pallas_call(
        body, name=name, out_shape=(_act((S, QL)), _act((S, KL))), grid=(S // ts,),
        in_specs=[pl.BlockSpec((ts, W), lambda i: (i, 0)), pl.BlockSpec((1, QL), lambda i: (0, 0)),
                  pl.BlockSpec((1, KL), lambda i: (0, 0))],
        out_specs=(pl.BlockSpec((ts, QL), lambda i: (i, 0)), pl.BlockSpec((ts, KL), lambda i: (i, 0))),
        compiler_params=_params("parallel"))(lat, g_cq, g_ckv)


def mla_latent_bwd(dcq, dckv, dkpe, lat, g_cq, g_ckv, name):
    S, W = lat.shape
    QL, KL = g_cq.shape[1], g_ckv.shape[1]
    ts = _tile(S, 256, 8)

    def body(dq_ref, dk_ref, dp_ref, l_ref, gq_ref, gk_ref, o_ref, dgq_ref, dgk_ref):
        first = pl.program_id(0) == 0
        da, ga = _norm_bwd(l_ref[:, :QL], gq_ref[...], dq_ref[...], QL)
        db, gb = _norm_bwd(l_ref[:, QL:QL + KL], gk_ref[...], dk_ref[...], KL)
        o_ref[:, :QL] = da.astype(o_ref.dtype)
        o_ref[:, QL:QL + KL] = db.astype(o_ref.dtype)
        o_ref[:, QL + KL:] = dp_ref[...].astype(o_ref.dtype)
        _accumulate(dgq_ref, jnp.sum(ga, axis=0, keepdims=True), first)
        _accumulate(dgk_ref, jnp.sum(gb, axis=0, keepdims=True), first)

    row = lambda n: pl.BlockSpec((ts, n), lambda i: (i, 0))
    vec = lambda n: pl.BlockSpec((1, n), lambda i: (0, 0))
    return pl.pallas_call(
        body, name=name, out_shape=(_act((S, W)), _f32((1, QL)), _f32((1, KL))), grid=(S // ts,),
        in_specs=[row(QL), row(KL), row(LANES), row(W), vec(QL), vec(KL)], out_specs=(row(W), vec(QL), vec(KL)),
        compiler_params=_params("arbitrary"))(dcq, dckv, dkpe, lat, g_cq, g_ckv)


def mla_q_prep_fwd(qraw, g, tabs, H, name):
    S = qraw.shape[0]
    ts = _tile(S, PREP_ROWS, 8)

    def body(x_ref, g_ref, c_ref, a_ref, b_ref, o_ref):
        y1, y2, _ = _head_norm(x_ref[:, :LANES], x_ref[:, LANES:], g_ref[...])
        o_ref[:, :LANES] = y1.astype(o_ref.dtype)
        o_ref[:, LANES:] = _rope(y2, c_ref[...], a_ref[...], b_ref[...]).astype(o_ref.dtype)

    tab = pl.BlockSpec((ts, LANES), lambda i, h: (i, 0))
    return pl.pallas_call(
        body, name=name, out_shape=_act((H, S, QK_PAD)), grid=(S // ts, H),
        in_specs=[pl.BlockSpec((ts, QK_PAD), lambda i, h: (i, h)), pl.BlockSpec((1, QK_PAD), lambda i, h: (0, 0)), tab, tab, tab],
        out_specs=pl.BlockSpec((None, ts, QK_PAD), lambda i, h: (h, i, 0)),
        compiler_params=_params("parallel", "parallel"))(qraw, g, *tabs)


def mla_q_prep_bwd(dq, qraw, g, tabs, H, name):
    S = qraw.shape[0]
    ts = _tile(S, PREP_ROWS, 8)

    def body(d_ref, x_ref, g_ref, c_ref, a_ref, b_ref, o_ref, dg_ref):
        d2 = _rope_t(d_ref[:, LANES:], c_ref[...], a_ref[...], b_ref[...])
        dx1, dx2, g1, g2 = _head_norm_bwd(x_ref[:, :LANES], x_ref[:, LANES:], g_ref[...], d_ref[:, :LANES], d2)
        o_ref[:, :LANES] = dx1.astype(o_ref.dtype)
        o_ref[:, LANES:] = dx2.astype(o_ref.dtype)
        first = jnp.logical_and(pl.program_id(0) == 0, pl.program_id(1) == 0)
        part = jnp.concatenate([jnp.sum(g1, axis=0, keepdims=True), jnp.sum(g2, axis=0, keepdims=True)], axis=1)
        _accumulate(dg_ref, part, first)

    tab = pl.BlockSpec((ts, LANES), lambda i, h: (i, 0))
    vec = pl.BlockSpec((1, QK_PAD), lambda i, h: (0, 0))
    return pl.pallas_call(
        body, name=name, out_shape=(_act((S, H * QK_PAD)), _f32((1, QK_PAD))), grid=(S // ts, H),
        in_specs=[pl.BlockSpec((None, ts, QK_PAD), lambda i, h: (h, i, 0)), pl.BlockSpec((ts, QK_PAD), lambda i, h: (i, h)),
                  vec, tab, tab, tab],
        out_specs=(pl.BlockSpec((ts, QK_PAD), lambda i, h: (i, h)), vec),
        compiler_params=_params("arbitrary", "arbitrary"))(dq, qraw, g, *tabs)


def mla_k_prep_fwd(kvraw, lat, g, tabs, H, pe_blk, name):
    S = kvraw.shape[0]
    ts = _tile(S, PREP_ROWS, 8)

    def body(x_ref, p_ref, g_ref, c_ref, a_ref, b_ref, k_ref, v_ref):
        y1, y2, _ = _head_norm(x_ref[:, :LANES], p_ref[...], g_ref[...])
        k_ref[:, :LANES] = y1.astype(k_ref.dtype)
        k_ref[:, LANES:] = _rope(y2, c_ref[...], a_ref[...], b_ref[...]).astype(k_ref.dtype)
        v_ref[...] = x_ref[:, LANES:].astype(v_ref.dtype)

    tab = pl.BlockSpec((ts, LANES), lambda i, h: (i, 0))
    return pl.pallas_call(
        body, name=name, out_shape=(_act((H, S, QK_PAD)), _act((H, S, LANES))), grid=(S // ts, H),
        in_specs=[pl.BlockSpec((ts, QK_PAD), lambda i, h: (i, h)), pl.BlockSpec((ts, LANES), lambda i, h: (i, pe_blk)),
                  pl.BlockSpec((1, QK_PAD), lambda i, h: (0, 0)), tab, tab, tab],
        out_specs=(pl.BlockSpec((None, ts, QK_PAD), lambda i, h: (h, i, 0)), pl.BlockSpec((None, ts, LANES), lambda i, h: (h, i, 0))),
        compiler_params=_params("parallel", "parallel"))(kvraw, lat, g, *tabs)


def mla_k_prep_bwd(dk, dv, kvraw, lat, g, tabs, H, pe_blk, name):
    S = kvraw.shape[0]
    ts = _tile(S, PREP_ROWS, 8)

    def body(dk_ref, dv_ref, x_ref, p_ref, g_ref, c_ref, a_ref, b_ref, o_ref, dp_ref, dg_ref):
        i, h = pl.program_id(0), pl.program_id(1)
        d2 = _rope_t(dk_ref[:, LANES:], c_ref[...], a_ref[...], b_ref[...])
        dx1, dx2, g1, g2 = _head_norm_bwd(x_ref[:, :LANES], p_ref[...], g_ref[...], dk_ref[:, :LANES], d2)
        o_ref[:, :LANES] = dx1.astype(o_ref.dtype)
        o_ref[:, LANES:] = dv_ref[...].astype(o_ref.dtype)
        _accumulate(dp_ref, dx2, h == 0)
        part = jnp.concatenate([jnp.sum(g1, axis=0, keepdims=True), jnp.sum(g2, axis=0, keepdims=True)], axis=1)
        _accumulate(dg_ref, part, jnp.logical_and(i == 0, h == 0))

    tab = pl.BlockSpec((ts, LANES), lambda i, h: (i, 0))
    vec = pl.BlockSpec((1, QK_PAD), lambda i, h: (0, 0))
    return pl.pallas_call(
        body, name=name, out_shape=(_act((S, H * QK_PAD)), _f32((S, LANES)), _f32((1, QK_PAD))), grid=(S // ts, H),
        in_specs=[pl.BlockSpec((None, ts, QK_PAD), lambda i, h: (h, i, 0)), pl.BlockSpec((None, ts, LANES), lambda i, h: (h, i, 0)),
                  pl.BlockSpec((ts, QK_PAD), lambda i, h: (i, h)), pl.BlockSpec((ts, LANES), lambda i, h: (i, pe_blk)),
                  vec, tab, tab, tab],
        out_specs=(pl.BlockSpec((ts, QK_PAD), lambda i, h: (i, h)), tab, vec),
        compiler_params=_params("arbitrary", "arbitrary"))(dk, dv, kvraw, lat, g, *tabs)


def _causal_scores(q, k, scale, diagonal):
    s = lax.dot_general(q, k, NT, preferred_element_type=jnp.float32) * scale
    if not diagonal:
        return s
    row = lax.broadcasted_iota(jnp.int32, s.shape, 0)
    col = lax.broadcasted_iota(jnp.int32, s.shape, 1)
    return jnp.where(col <= row, s, NEG)


def _on_causal_blocks(qi, ki, step):
    @pl.when(ki < qi)
    def _():
        step(False)

    @pl.when(ki == qi)
    def _():
        step(True)


def mla_attention_fwd(q, k, v, name):
    H, S, _ = q.shape
    t = _tile(S, ATTN_BLOCK)
    n = S // t
    scale = 1.0 / math.sqrt(QK_DIM)

    def body(q_ref, k_ref, v_ref, o_ref, lse_ref, m_sc, l_sc, acc):
        qi, ki = pl.program_id(1), pl.program_id(2)

        @pl.when(ki == 0)
        def _():
            m_sc[...] = jnp.full(m_sc.shape, NEG, jnp.float32)
            l_sc[...] = jnp.zeros_like(l_sc)
            acc[...] = jnp.zeros_like(acc)

        def step(diagonal):
            s = _causal_scores(q_ref[...], k_ref[...], scale, diagonal)
            m_new = jnp.maximum(m_sc[...], jnp.max(s, axis=-1, keepdims=True))
            alpha = jnp.exp(m_sc[...] - m_new)
            p = jnp.exp(s - m_new)
            l_sc[...] = alpha * l_sc[...] + jnp.sum(p, axis=-1, keepdims=True)
            acc[...] = alpha * acc[...] + jnp.dot(p.astype(MXU_DTYPE), v_ref[...], preferred_element_type=jnp.float32)
            m_sc[...] = m_new

        _on_causal_blocks(qi, ki, step)

        @pl.when(ki == qi)
        def _():
            o_ref[...] = (acc[...] / l_sc[...]).astype(o_ref.dtype)
            lse_ref[...] = m_sc[...] + jnp.log(l_sc[...])

    kv = lambda w: pl.BlockSpec((None, t, w), lambda h, qi, ki: (h, jnp.minimum(ki, qi), 0))
    return pl.pallas_call(
        body, name=name, out_shape=(_act((S, H * LANES)), _f32((H, S, 1))), grid=(H, n, n),
        in_specs=[pl.BlockSpec((None, t, QK_PAD), lambda h, qi, ki: (h, qi, 0)), kv(QK_PAD), kv(LANES)],
        out_specs=(pl.BlockSpec((t, LANES), lambda h, qi, ki: (qi, h)), pl.BlockSpec((None, t, 1), lambda h, qi, ki: (h, qi, 0))),
        scratch_shapes=[pltpu.VMEM((t, 1), jnp.float32), pltpu.VMEM((t, 1), jnp.float32), pltpu.VMEM((t, LANES), jnp.float32)],
        compiler_params=_params("parallel", "parallel", "arbitrary"))(q, k, v)


def mla_attention_bwd(q, k, v, do, o, lse, name):
    H, S, _ = q.shape
    t = _tile(S, ATTN_BLOCK)
    n = S // t
    scale = 1.0 / math.sqrt(QK_DIM)

    def body(q_ref, k_ref, v_ref, do_ref, o_ref, lse_ref, dq_ref, dk_ref, dv_ref, dk_acc, dv_acc):
        ki, qi = pl.program_id(1), pl.program_id(2)
        rows = pl.ds(pl.multiple_of(qi * t, t), t)

        @pl.when(qi == 0)
        def _():
            dk_acc[...] = jnp.zeros_like(dk_acc)
            dv_acc[...] = jnp.zeros_like(dv_acc)

        @pl.when(ki == 0)
        def _():
            dq_ref[rows, :] = jnp.zeros((t, QK_PAD), jnp.float32)

        def step(diagonal):
            p = jnp.exp(_causal_scores(q_ref[...], k_ref[...], scale, diagonal) - lse_ref[...])
            dof = do_ref[...]
            dob = dof.astype(MXU_DTYPE)
            delta = jnp.sum(dof * o_ref[...].astype(jnp.float32), axis=-1, keepdims=True)
            dv_acc[...] += lax.dot_general(p.astype(MXU_DTYPE), dob, TN, preferred_element_type=jnp.float32)
            dp = lax.dot_general(dob, v_ref[...], NT, preferred_element_type=jnp.float32)
            ds = (p * (dp - delta)).astype(MXU_DTYPE)
            dk_acc[...] += lax.dot_general(ds, q_ref[...], TN, preferred_element_type=jnp.float32)
            dq_ref[rows, :] += jnp.dot(ds, k_ref[...], preferred_element_type=jnp.float32) * scale

        _on_causal_blocks(qi, ki, step)

        @pl.when(qi == n - 1)
        def _():
            dk_ref[...] = dk_acc[...] * scale
            dv_ref[...] = dv_acc[...]

    qrow = lambda h, ki, qi: (h, jnp.maximum(qi, ki), 0)
    kv = lambda w: pl.BlockSpec((None, t, w), lambda h, ki, qi: (h, ki, 0))
    col = pl.BlockSpec((None, t, 1), qrow)
    head = pl.BlockSpec((t, LANES), lambda h, ki, qi: (jnp.maximum(qi, ki), h))
    return pl.pallas_call(
        body, name=name, out_shape=(_f32((H, S, QK_PAD)), _f32((H, S, QK_PAD)), _f32((H, S, LANES))), grid=(H, n, n),
        in_specs=[pl.BlockSpec((None, t, QK_PAD), qrow), kv(QK_PAD), kv(LANES),
                  head, head, col],
        out_specs=(pl.BlockSpec((None, S, QK_PAD), lambda h, ki, qi: (h, 0, 0)), kv(QK_PAD), kv(LANES)),
        scratch_shapes=[pltpu.VMEM((t, QK_PAD), jnp.float32), pltpu.VMEM((t, LANES), jnp.float32)],
        compiler_params=_params("parallel", "arbitrary", "arbitrary"))(q, k, v, do, o, lse)


def _alibi_slopes(G, Hd):
    k = np.arange(1, G * Hd + 1, dtype=np.float32)
    s = (2.0 ** (-8.0 * k / (G * Hd))).astype(np.float32).reshape(G, Hd)
    return jnp.asarray(np.broadcast_to(s[:, :, None, None], (G, Hd, 1, LANES)).copy())


def _dil_scores(qn, kn, scale, slope_d, prev, valid):
    s = lax.dot_general(qn, kn, NT, preferred_element_type=jnp.float32) * scale
    iq = lax.broadcasted_iota(jnp.int32, s.shape, 0)
    ik = lax.broadcasted_iota(jnp.int32, s.shape, 1)
    dist = iq - ik + (BLK if prev else 0)
    ok = (ik >= iq) if prev else (ik <= iq)
    s = s - slope_d * dist.astype(jnp.float32)
    return jnp.where(jnp.logical_and(ok, valid), s, NEG)


def _dil_scores_both(qn, k2, scale, slope_d, has_prev):
    s = lax.dot_general(qn, k2, NT, preferred_element_type=jnp.float32) * scale
    iq = lax.broadcasted_iota(jnp.int32, s.shape, 0)
    ik = lax.broadcasted_iota(jnp.int32, s.shape, 1)
    dist = iq + BLK - ik
    ok = jnp.logical_and(jnp.logical_and(dist >= 0, dist <= BLK), jnp.logical_or(ik >= BLK, has_prev))
    return jnp.where(ok, s - slope_d * dist.astype(jnp.float32), NEG)


def _dil_scores_after(q2, kn, scale, slope_d, has_next):
    s = lax.dot_general(q2, kn, NT, preferred_element_type=jnp.float32) * scale
    iq = lax.broadcasted_iota(jnp.int32, s.shape, 0)
    ik = lax.broadcasted_iota(jnp.int32, s.shape, 1)
    dist = iq - ik
    ok = jnp.logical_and(jnp.logical_and(dist >= 0, dist <= BLK), jnp.logical_or(iq < BLK, has_next))
    return jnp.where(ok, s - slope_d * dist.astype(jnp.float32), NEG)


def _dil_heads(d, Hd, block_bytes):
    hb = max(1, min(Hd, block_bytes // (BLK * LANES * 4))) if d == 1 else 1
    assert Hd % hb == 0, (Hd, hb)
    return hb


def _dil_rows(r, d):
    return pl.ds(r, BLK, stride=d) if d > 1 else slice(None)


def _loop_residues(d, residue, init):
    per = math.gcd(d, 4)

    def one_pass(i, carry):
        for k in range(per):
            carry = residue(per * i + k, carry)
        return carry

    return one_pass(0, init) if d == per else lax.fori_loop(0, d // per, one_pass, init)


def _dil_specs(d, nblk, Hd, G, g, hb):
    def spec(kind, shift):
        col0 = (kind * G + g) * Hd // hb
        return pl.BlockSpec((BLK * d, hb * LANES), lambda n, hg: (jnp.clip(n + shift, 0, nblk - 1), col0 + hg))
    return spec


def _head_spec(d, nblk, hb, shift):
    return pl.BlockSpec((BLK * d, hb * LANES), lambda n, hg: (jnp.clip(n + shift, 0, nblk - 1), hg))


def dilated_fwd(qkv, gq, gk, slopes, g, d, Hd, G, name):
    S, C = qkv.shape
    nblk = S // (BLK * d)
    hb = _dil_heads(d, Hd, 2 << 20)
    scale = 1.0 / math.sqrt(LANES)
    spec = _dil_specs(d, nblk, Hd, G, g, hb)

    def body(q_ref, kc_ref, kp_ref, vc_ref, vp_ref, gq_ref, gk_ref, sl_ref, o_ref, l_ref):
        n = pl.program_id(0)
        nrm = lambda t, gg: (t * _rstd(t, LANES) * gg).astype(MXU_DTYPE)

        def residue(r, carry):
            rows = _dil_rows(r, d)
            for hh in range(hb):
                cols = slice(hh * LANES, (hh + 1) * LANES)
                slope_d = sl_ref[hh][:, :1] * float(d)
                qn = nrm(q_ref[rows, cols], gq_ref[...])
                k2 = jnp.concatenate([nrm(kp_ref[rows, cols], gk_ref[...]), nrm(kc_ref[rows, cols], gk_ref[...])], axis=0)
                v2 = jnp.concatenate([vp_ref[rows, cols], vc_ref[rows, cols]], axis=0).astype(MXU_DTYPE)
                s = _dil_scores_both(qn, k2, scale, slope_d, n > 0)
                m = jnp.max(s, axis=-1, keepdims=True)
                e = jnp.exp(s - m)
                tot = jnp.sum(e, axis=-1, keepdims=True)
                o = jnp.dot((e * (1.0 / tot)).astype(MXU_DTYPE), v2, preferred_element_type=jnp.float32)
                o_ref[rows, cols] = o
                l_ref[rows, cols] = jnp.broadcast_to(m + jnp.log(tot), (BLK, LANES))
            return carry

        _loop_residues(d, residue, 0)

    vec = pl.BlockSpec((1, LANES), lambda n, hg: (0, 0))
    out = _head_spec(d, nblk, hb, 0)
    return pl.pallas_call(
        body, name=name, out_shape=(_f32((S, Hd * LANES)), _f32((S, Hd * LANES))), grid=(nblk, Hd // hb),
        in_specs=[spec(0, 0), spec(1, 0), spec(1, -1), spec(2, 0), spec(2, -1), vec, vec,
                  pl.BlockSpec((None, hb, 1, LANES), lambda n, hg: (g, hg, 0, 0))],
        out_specs=(out, out), compiler_params=_params("parallel", "parallel"),
    )(qkv, qkv, qkv, qkv, qkv, gq, gk, slopes)


def dilated_merge(os_, ls_, name):
    S, W = os_[0].shape
    G = len(os_)
    ts, tw = _tile(S, 512, 8), _tile(W, 512)

    def body(*refs):
        o_refs, l_refs, (o_ref, t_ref) = refs[:G], refs[G:2 * G], refs[2 * G:]
        ls = [r[...] for r in l_refs]
        m = ls[0]
        for l in ls[1:]:
            m = jnp.maximum(m, l)
        es = [jnp.exp(l - m) for l in ls]
        tot = es[0]
        for e in es[1:]:
            tot = tot + e
        acc = o_refs[0][...] * (es[0] / tot)
        for r, e in zip(o_refs[1:], es[1:]):
            acc = acc + r[...] * (e / tot)
        o_ref[...] = acc.astype(o_ref.dtype)
        t_ref[...] = m + jnp.log(tot)

    blk = pl.BlockSpec((ts, tw), lambda i, j: (i, j))
    return pl.pallas_call(
        body, name=name, out_shape=(_act((S, W)), _f32((S, W))), grid=(S // ts, W // tw),
        in_specs=[blk] * (2 * G), out_specs=(blk, blk), compiler_params=_params("parallel", "parallel"))(*os_, *ls_)


def dilated_delta(do, o, name):
    S, W = do.shape
    ts = _tile(S, 512, 8)

    def body(d_ref, o_ref, out_ref):
        out_ref[...] = jnp.broadcast_to(jnp.sum(d_ref[...] * o_ref[...].astype(jnp.float32), axis=-1, keepdims=True), out_ref.shape)

    blk = pl.BlockSpec((ts, LANES), lambda i, h: (i, h))
    return pl.pallas_call(body, name=name, out_shape=_f32((S, W)), grid=(S // ts, W // LANES), in_specs=[blk, blk],
                          out_specs=blk, compiler_params=_params("parallel", "parallel"))(do, o)


def dilated_bwd(qkv, do, lse, delta, gq, gk, slopes, g, d, Hd, G, name):
    S, C = qkv.shape
    nblk = S // (BLK * d)
    W = Hd * LANES
    scale = 1.0 / math.sqrt(LANES)
    hb = _dil_heads(d, Hd, 1 << 20)
    spec = _dil_specs(d, nblk, Hd, G, g, hb)
    hspec = lambda shift: _head_spec(d, nblk, hb, shift)

    def body(q_ref, qx_ref, kc_ref, kp_ref, vc_ref, vp_ref, do_ref, dox_ref, l_ref, lx_ref, dl_ref, dlx_ref,
             gq_ref, gk_ref, sl_ref, dq_ref, dk_ref, dv_ref, dgq_ref, dgk_ref):
        n = pl.program_id(0)
        gqv, gkv = gq_ref[...], gk_ref[...]
        nrm = lambda t, gg: (t * _rstd(t, LANES) * gg).astype(MXU_DTYPE)
        f32dot = lambda a, b, dn: lax.dot_general(a, b, dn, preferred_element_type=jnp.float32)

        def residue(r, carry):
            rows = _dil_rows(r, d)
            dgq_sum, dgk_sum = carry
            for hh in range(hb):
                cols = slice(hh * LANES, (hh + 1) * LANES)
                slope_d = sl_ref[hh][:, :1] * float(d)
                q, kc = q_ref[rows, cols], kc_ref[rows, cols]
                qn, qxn = nrm(q, gqv), nrm(qx_ref[rows, cols], gqv)
                kcn, kpn = nrm(kc, gkv), nrm(kp_ref[rows, cols], gkv)
                vc, vp = vc_ref[rows, cols].astype(MXU_DTYPE), vp_ref[rows, cols].astype(MXU_DTYPE)
                dob, doxb = do_ref[rows, cols].astype(MXU_DTYPE), dox_ref[rows, cols].astype(MXU_DTYPE)
                lrow, lxrow = l_ref[rows, cols][:, :1], lx_ref[rows, cols][:, :1]
                drow, dxrow = dl_ref[rows, cols][:, :1], dlx_ref[rows, cols][:, :1]
                k2 = jnp.concatenate([kpn, kcn], axis=0)
                v2 = jnp.concatenate([vp, vc], axis=0)
                p2 = jnp.exp(_dil_scores_both(qn, k2, scale, slope_d, n > 0) - lrow)
                ds2 = (p2 * (f32dot(dob, v2, NT) - drow)).astype(MXU_DTYPE)
                dq, dgq = _norm_bwd(q, gqv, jnp.dot(ds2, k2, preferred_element_type=jnp.float32) * scale, LANES)
                dq_ref[rows, cols] = dq
                q2 = jnp.concatenate([qn, qxn], axis=0)
                do2 = jnp.concatenate([dob, doxb], axis=0)
                pq = jnp.exp(_dil_scores_after(q2, kcn, scale, slope_d, n < nblk - 1) - jnp.concatenate([lrow, lxrow], axis=0))
                dsq = (pq * (f32dot(do2, vc, NT) - jnp.concatenate([drow, dxrow], axis=0))).astype(MXU_DTYPE)
                dk, dgk = _norm_bwd(kc, gkv, f32dot(dsq, q2, TN) * scale, LANES)
                dk_ref[rows, cols] = dk
                dv_ref[rows, cols] = f32dot(pq.astype(MXU_DTYPE), do2, TN)
                dgq_sum = dgq_sum + jnp.sum(dgq, axis=0, keepdims=True)
                dgk_sum = dgk_sum + jnp.sum(dgk, axis=0, keepdims=True)
            return dgq_sum, dgk_sum

        zero = jnp.zeros((1, LANES), jnp.float32)
        dgq_sum, dgk_sum = _loop_residues(d, residue, (zero, zero))
        first = jnp.logical_and(n == 0, pl.program_id(1) == 0)
        _accumulate(dgq_ref, dgq_sum, first)
        _accumulate(dgk_ref, dgk_sum, first)

    vec = pl.BlockSpec((1, LANES), lambda n, hg: (0, 0))
    out = hspec(0)
    return pl.pallas_call(
        body, name=name, out_shape=(_f32((S, W)), _f32((S, W)), _f32((S, W)), _f32((1, LANES)), _f32((1, LANES))),
        grid=(nblk, Hd // hb),
        in_specs=[spec(0, 0), spec(0, 1), spec(1, 0), spec(1, -1), spec(2, 0), spec(2, -1), hspec(0), hspec(1), hspec(0), hspec(1),
                  hspec(0), hspec(1), vec, vec, pl.BlockSpec((None, hb, 1, LANES), lambda n, hg: (g, hg, 0, 0))],
        out_specs=(out, out, out, vec, vec), compiler_params=_params("arbitrary", "arbitrary"),
    )(qkv, qkv, qkv, qkv, qkv, qkv, do, do, lse, lse, delta, delta, gq, gk, slopes)


def adamw(w, g, m, v, layer, prev, name):
    L, r, c = w.shape
    tr, tc = _tile(r, 512, 8), _tile(c, 1024)
    c1 = 1.0 / (1.0 - ADAM_B1 ** ADAM_STEP)
    c2 = 1.0 / (1.0 - ADAM_B2 ** ADAM_STEP)

    def body(*refs):
        w_ref, g_ref, m_ref, v_ref = refs[:4]
        go_ref, d_ref, mo_ref, vo_ref = refs[-4:]
        gv = g_ref[...]
        mn = ADAM_B1 * m_ref[...] + (1.0 - ADAM_B1) * gv
        vn = ADAM_B2 * v_ref[...] + (1.0 - ADAM_B2) * (gv * gv)
        go_ref[...] = gv
        d_ref[...] = -ADAM_LR * ((mn * c1) / (jnp.sqrt(vn * c2) + ADAM_EPS) + ADAM_WD * w_ref[...])
        mo_ref[...] = mn
        vo_ref[...] = vn

    lay = pl.BlockSpec((None, tr, tc), lambda i, j: (layer, i, j))
    flat = pl.BlockSpec((tr, tc), lambda i, j: (i, j))
    ins = [w, g, m, v] + (list(prev) if prev is not None else [])
    in_specs = [lay, flat, lay, lay] + ([ANY] * 4 if prev is not None else [])
    return pl.pallas_call(
        body, name=name, out_shape=tuple(_f32((L, r, c)) for _ in range(4)), grid=(r // tr, c // tc),
        in_specs=in_specs, out_specs=(lay, lay, lay, lay),
        input_output_aliases=({4 + k: k for k in range(4)} if prev is not None else {}),
        compiler_params=_params("parallel", "parallel"))(*ins)


def _ffn_fwd(h, g, W, kind, tag):
    xn = rmsnorm_fwd(h, g, tag + "_norm")
    u, a = swiglu_in(xn, W(kind + "_w_in", h), tag + "_in")
    out = matmul(a, W(kind + "_w_out", a), scale=0.5, res=h, name=tag + "_out", tk=2816)
    return out, (h, xn, u, a)


def _ffn_bwd(dout, saved, g, W, emit, kind, tag):
    h, xn, u, a = saved
    emit(kind + "_w_out", matmul(a, dout, ta=True, scale=0.5, out_dtype=WIRE_DTYPE, out_axis=0, name=tag + "_dwout", tm=1408, tk=2048))
    du = Sharded(swiglu_out_bwd(dout, W(kind + "_w_out", None), u, 0.5, tag + "_da"), 0, 1)
    emit(kind + "_w_in", matmul(xn, du, ta=True, out_dtype=WIRE_DTYPE, out_axis=1, name=tag + "_dwin", tn=1408, tk=2048))
    dxn = matmul(du, W(kind + "_w_in", None), tb=True, name=tag + "_dxn", tk=2816)
    return rmsnorm_bwd(h, g, dxn, dout, tag + "_dnorm")


def _pad_gain(g):
    return jnp.pad(g, ((0, 0), (0, QK_PAD - QK_DIM)))


def _mla_fwd(h, P, W, tabs, H):
    g_mix, g_cq, g_ckv = P["mix_norm"][0:1], P["mla_g_cq"], P["mla_g_ckv"]
    pe_blk = (g_cq.shape[1] + g_ckv.shape[1]) // LANES
    xn = rmsnorm_fwd(h, g_mix, "mla_norm")
    w_down = W("mla_w_down", h)
    lat = matmul(xn, w_down, name="mla_down", tn=w_down.shape[1])
    cq, ckv = mla_latent_fwd(lat, g_cq, g_ckv, "mla_latent")
    qraw = matmul(cq, W("mla_w_uq", lat), name="mla_uq")
    kvraw = matmul(ckv, W("mla_w_ukv", qraw), name="mla_ukv")
    q = mla_q_prep_fwd(qraw, _pad_gain(P["mla_g_qn"]), tabs, H, "mla_qprep")
    k, v = mla_k_prep_fwd(kvraw, lat, _pad_gain(P["mla_g_kn"]), tabs, H, pe_blk, "mla_kprep")
    o, lse = mla_attention_fwd(q, k, v, "mla_attn")
    out = matmul(o, W("mla_w_o", lse), res=h, name="mla_o")
    return out, (h, xn, lat, cq, ckv, qraw, kvraw, q, k, v, o, lse, pe_blk)


def _mla_bwd(dout, saved, P, W, emit, tabs, H):
    h, xn, lat, cq, ckv, qraw, kvraw, q, k, v, o, lse, pe_blk = saved
    emit("mla_w_o", matmul(o, dout, ta=True, out_dtype=WIRE_DTYPE, out_axis=0, name="mla_dwo", tm=512))
    do = matmul(dout, W("mla_w_o", None), tb=True, name="mla_do", tn=512)
    dq, dk, dv = mla_attention_bwd(q, k, v, do, o, lse, "mla_attn_bwd")
    dqraw, dgq = mla_q_prep_bwd(dq, qraw, _pad_gain(P["mla_g_qn"]), tabs, H, "mla_dqprep")
    dkvraw, dkpe, dgk = mla_k_prep_bwd(dk, dv, kvraw, lat, _pad_gain(P["mla_g_kn"]), tabs, H, pe_blk, "mla_dkprep")
    emit("mla_w_uq", matmul(cq, dqraw, ta=True, out_dtype=WIRE_DTYPE, out_axis=1, name="mla_dwuq"))
    dcq = matmul(dqraw, W("mla_w_uq", None), tb=True, name="mla_dcq", tk=1024)
    emit("mla_w_ukv", matmul(ckv, dkvraw, ta=True, out_dtype=WIRE_DTYPE, out_axis=1, name="mla_dwukv"))
    dckv = matmul(dkvraw, W("mla_w_ukv", None), tb=True, name="mla_dckv", tk=1024)
    dlat, dgcq, dgckv = mla_latent_bwd(dcq, dckv, dkpe, lat, P["mla_g_cq"], P["mla_g_ckv"], "mla_dlatent")
    emit("mla_w_down", matmul(xn, dlat, ta=True, out_dtype=WIRE_DTYPE, out_axis=0, name="mla_dwdown", tm=512, tn=dlat.shape[1]))
    dxn = matmul(dlat, W("mla_w_down", None), tb=True, name="mla_dxn", tn=512, tk=dlat.shape[1])
    dh, dgm = rmsnorm_bwd(h, P["mix_norm"][0:1], dxn, dout, "mla_dnorm")
    return dh, dgm, dict(mla_g_qn=dgq[:, :QK_DIM], mla_g_kn=dgk[:, :QK_DIM], mla_g_cq=dgcq, mla_g_ckv=dgckv)


def _dil_fwd(h, P, W, slopes, Hd):
    G = len(DIL_PAIRS)
    xn = rmsnorm_fwd(h, P["mix_norm"][1:2], "dil_norm")
    qkv = matmul(xn, W("dil_w_qkv", h), name="dil_qkv", tn=1152)
    os_, ls_ = [], []
    for g, (_, d) in enumerate(DIL_PAIRS):
        o_g, l_g = dilated_fwd(qkv, P["dil_g_qn"], P["dil_g_kn"], slopes, g, d, Hd, G, f"dil_attn{g}")
        os_.append(o_g)
        ls_.append(l_g)
    o, lse = dilated_merge(os_, ls_, "dil_merge")
    out = matmul(o, W("dil_w_o", lse), res=h, name="dil_o", tn=512)
    return out, (h, xn, qkv, o, lse)


def _dil_bwd(dout, saved, P, W, emit, slopes, Hd):
    h, xn, qkv, o, lse = saved
    ngrp = len(DIL_PAIRS)
    emit("dil_w_o", matmul(o, dout, ta=True, out_dtype=WIRE_DTYPE, out_axis=1, name="dil_dwo", tn=512))
    do = matmul(dout, W("dil_w_o", None), tb=True, name="dil_do", tk=512)
    delta = dilated_delta(do, o, "dil_delta")
    parts = [dilated_bwd(qkv, do, lse, delta, P["dil_g_qn"], P["dil_g_kn"], slopes, g, d, Hd, ngrp, f"dil_dattn{g}")
             for g, (_, d) in enumerate(DIL_PAIRS)]
    dqkv = jnp.concatenate([p[kind] for kind in range(3) for p in parts], axis=1).astype(MXU_DTYPE)
    emit("dil_w_qkv", matmul(xn, dqkv, ta=True, out_dtype=WIRE_DTYPE, out_axis=1, name="dil_dwqkv", tn=1152))
    dxn = matmul(dqkv, W("dil_w_qkv", None), tb=True, name="dil_dxn", tk=2304)
    dh, dgm = rmsnorm_bwd(h, P["mix_norm"][1:2], dxn, dout, "dil_dnorm")
    return dh, dgm, dict(dil_g_qn=parts[0][3] + parts[1][3] + parts[2][3], dil_g_kn=parts[0][4] + parts[1][4] + parts[2][4])


def local_step(x, target, P, get_w, on_grad):
    S, D = x.shape
    H, Hd = MLA_HEADS, DIL_HEADS
    tabs = rope_tables(S)
    slopes = _alibi_slopes(len(DIL_PAIRS), Hd)
    cache = {}

    def weights_of(layer):
        def W(name, after):
            if (name, layer) not in cache:
                cache[name, layer] = Sharded(get_w(name, layer, after), 0, SHARD_AXIS[name])
            return cache[name, layer]
        return W

    row = lambda name, i: P[name][i:i + 1]
    h = x
    saved = []
    for i in range(2):
        W = weights_of(i)
        h, s1 = _ffn_fwd(h, row("ffn1_norm", i), W, "ffn1", f"l{i}_ffn1")
        h, sm = _mla_fwd(h, P, weights_of(0), tabs, H) if i == 0 else _dil_fwd(h, P, weights_of(0), slopes, Hd)
        h, s2 = _ffn_fwd(h, row("ffn2_norm", i), W, "ffn2", f"l{i}_ffn2")
        saved.append((s1, sm, s2))
    loss, dh = loss_head(h, target, "loss")

    gs = {n: [None, None] for n in ("ffn1_norm", "mix_norm", "ffn2_norm")}
    for i in (1, 0):
        s1, sm, s2 = saved[i]
        W = weights_of(i)
        emit = lambda name, g4, layer=i: on_grad(name, layer, g4)
        emit0 = lambda name, g4: on_grad(name, 0, g4)
        dh, gs["ffn2_norm"][i] = _ffn_bwd(dh, s2, row("ffn2_norm", i), W, emit, "ffn2", f"l{i}_ffn2")
        if i == 0:
            dh, gs["mix_norm"][i], gm = _mla_bwd(dh, sm, P, weights_of(0), emit0, tabs, H)
        else:
            dh, gs["mix_norm"][i], gm = _dil_bwd(dh, sm, P, weights_of(0), emit0, slopes, Hd)
        gs.update({n: [val] for n, val in gm.items()})
        dh, gs["ffn1_norm"][i] = _ffn_bwd(dh, s1, row("ffn1_norm", i), W, emit, "ffn1", f"l{i}_ffn1")
    gsmall = {n: jnp.concatenate(v, axis=0) for n, v in gs.items()}
    return loss, dh, gsmall


def _pad_heads(w, real, padded):
    lead, n = w.shape[:-1], w.shape[-1] // real
    w = jnp.pad(w.reshape(*lead, n, real), [(0, 0)] * (len(lead) + 1) + [(0, padded - real)])
    return w.reshape(*lead, n * padded)


def _unpad_heads(w, real, padded):
    lead, n = w.shape[:-1], w.shape[-1] // padded
    return w.reshape(*lead, n, padded)[..., :real].reshape(*lead, n * real)


def _pack_small(gs):
    flat = jnp.concatenate([gs[n].reshape(-1) for n in SMALL])
    rows = -(-flat.shape[0] // LANES)
    rows = -(-rows // 8) * 8
    return jnp.pad(flat, (0, rows * LANES - flat.shape[0])).reshape(rows, LANES)


def _unpack_small(packed, like):
    flat, out, off = packed.reshape(-1), {}, 0
    for n in SMALL:
        size = int(np.prod(like[n].shape))
        out[n] = flat[off:off + size].reshape(like[n].shape)
        off += size
    return out


def kernel(x, ffn1_norm, ffn1_w_in, ffn1_w_out, mix_norm, ffn2_norm, ffn2_w_in, ffn2_w_out, mla_w_down, mla_g_cq, mla_g_ckv, mla_w_uq, mla_w_ukv, mla_g_qn, mla_g_kn, mla_w_o, dil_w_qkv, dil_g_qn, dil_g_kn, dil_w_o, loss_target, m_ffn1_norm, m_ffn1_w_in, m_ffn1_w_out, m_mix_norm, m_ffn2_norm, m_ffn2_w_in, m_ffn2_w_out, m_mla_w_down, m_mla_g_cq, m_mla_g_ckv, m_mla_w_uq, m_mla_w_ukv, m_mla_g_qn, m_mla_g_kn, m_mla_w_o, m_dil_w_qkv, m_dil_g_qn, m_dil_g_kn, m_dil_w_o, v_ffn1_norm, v_ffn1_w_in, v_ffn1_w_out, v_mix_norm, v_ffn2_norm, v_ffn2_w_in, v_ffn2_w_out, v_mla_w_down, v_mla_g_cq, v_mla_g_ckv, v_mla_w_uq, v_mla_w_ukv, v_mla_g_qn, v_mla_g_kn, v_mla_w_o, v_dil_w_qkv, v_dil_g_qn, v_dil_g_kn, v_dil_w_o):
    args = dict(locals())
    w = {n: args[n] for n in WEIGHTS}
    m = {n: args["m_" + n] for n in WEIGHTS}
    v = {n: args["v_" + n] for n in WEIGHTS}
    cx, cy, cc = _me()
    core = jnp.reshape(cc, (1,)).astype(jnp.int32)
    shard = jnp.reshape(2 * cx + cy, (1,)).astype(jnp.int32)
    place = (shard, core)
    pe_pad = LANES - ROPE_DIM

    order = [(n, layer if w[n].shape[0] > 1 else 0) for layer in range(2) for n in USE_ORDER[layer]]
    lands, over_ici, to_sibling = {}, {}, {}

    def cast(key, after):
        _, r, c = w[key[0]].shape
        lands[key] = cast_into_shards(w[key[0]], key[1], shard, after, "ag_%s%d_cast" % key).reshape(N_CHIPS, 1, 2, r // 2, c)
        return lands[key]

    def start(key, after):
        over_ici[key] = exchange_start((lands.pop(key),), ag_over_ici, after, "ag_%s%d_start" % key)
        return over_ici[key][2]

    def pass_on(key, after):
        if key in to_sibling:
            return after
        bufs = exchange_wait(over_ici[key], ag_over_ici, after, "ag_%s%d_wait" % key)
        to_sibling[key] = exchange_start(bufs, ag_to_sibling, after, "ag_%s%d_pass" % key)
        return to_sibling[key][2]

    def get_w(n, l, after):
        k = order.index((n, l))
        behind = after
        if k == 0:
            for key in order[:2]:
                cast(key, after)
            for key in order[:2]:
                behind = start(key, behind)
            for key in order[2:]:
                behind = cast(key, behind)
        else:
            for key in order[len(over_ici):min(k + AG_AHEAD, len(order) - 1) + 1]:
                behind = start(key, behind)
        behind = pass_on((n, l), behind)
        if 0 < k < len(order) - 1:
            behind = pass_on(order[k + 1], behind)
        (land,) = exchange_wait(to_sibling[n, l], ag_to_sibling, behind, f"ag_{n}{l}_passed")
        _, _, _, h, c = land.shape
        full = land.reshape(N_CHIPS, 1, 2 * h, c)
        if n == "mla_w_down":
            full = jnp.pad(full, ((0, 0), (0, 0), (0, 0), (0, pe_pad)))
        if n == "mla_w_uq":
            full = _pad_heads(full, QK_DIM, QK_PAD)
        return full

    outs = {n: None for n in BIG}

    def update(key, g):
        n, l = key
        outs[n] = adamw(w[n], g, m[n], v[n], l, outs[n], f"adamw_{n}{l}")

    reduce_scatter = ReduceScatter(place, update)

    def on_grad(n, l, g4):
        if n == "mla_w_down":
            g4 = g4[..., :g4.shape[-1] - pe_pad]
        if n == "mla_w_uq":
            g4 = _unpad_heads(g4, QK_DIM, QK_PAD)
        reduce_scatter.push((n, l), g4)

    loss, grad_x, gsmall = local_step(x[0], loss_target[0], {n: w[n] for n in SMALL}, get_w, on_grad)
    loss = lax.psum(loss, ("x", "y", "c"))
    reduce_scatter.drain(grad_x)
    small = _unpack_small(all_reduce_small(_pack_small(gsmall), "ar_small"), gsmall)
    for n in SMALL:
        outs[n] = tuple(o[0] for o in adamw(w[n][None], small[n], m[n][None], v[n][None], 0, None, f"adamw_{n}"))

    return (loss, grad_x[None], *[outs[n][0] for n in WEIGHTS], *[outs[n][1] for n in WEIGHTS],
            *[outs[n][2] for n in WEIGHTS], *[outs[n][3] for n in WEIGHTS])
```

```python
import math

import numpy as np
import jax
import jax.numpy as jnp
from jax import lax
from jax.experimental import pallas as pl
from jax.experimental.pallas import tpu as pltpu

MXU_DTYPE = jnp.bfloat16
WIRE_DTYPE = jnp.bfloat16
EPS = 1e-6
NEG = -1e30
N_CHIPS = 4
MESH = pl.DeviceIdType.MESH
ANY = pl.BlockSpec(memory_space=pl.ANY)
LANES = 128

MLA_HEADS = 16
NOPE_DIM = 128
ROPE_DIM = 64
QK_DIM = NOPE_DIM + ROPE_DIM
QK_PAD = 2 * LANES
PREP_ROWS = 1024
ATTN_BLOCK = 1024
AG_AHEAD = 3
RS_WINDOW = 2
ROPE_THETA = 10000.0
DIL_PAIRS = ((128, 1), (512, 4), (2048, 16))
DIL_HEADS = 8
BLK = 128

ADAM_LR = 0.001
ADAM_B1 = 0.9
ADAM_B2 = 0.999
ADAM_EPS = 1e-08
ADAM_WD = 0.01
ADAM_STEP = 10

NT = (((1,), (1,)), ((), ()))
TN = (((0,), (0,)), ((), ()))

SHARD_AXIS = {"ffn1_w_in": 1, "ffn1_w_out": 0, "ffn2_w_in": 1, "ffn2_w_out": 0, "mla_w_down": 0, "mla_w_uq": 1,
              "mla_w_ukv": 1, "mla_w_o": 0, "dil_w_qkv": 1, "dil_w_o": 1}
BIG = tuple(SHARD_AXIS)
USE_ORDER = (("ffn1_w_in", "ffn1_w_out", "mla_w_down", "mla_w_uq", "mla_w_ukv", "mla_w_o", "ffn2_w_in", "ffn2_w_out"),
             ("ffn1_w_in", "ffn1_w_out", "dil_w_qkv", "dil_w_o", "ffn2_w_in", "ffn2_w_out"))
SMALL = ("ffn1_norm", "mix_norm", "ffn2_norm", "mla_g_cq", "mla_g_ckv", "mla_g_qn", "mla_g_kn", "dil_g_qn", "dil_g_kn")
WEIGHTS = ("ffn1_norm", "ffn1_w_in", "ffn1_w_out", "mix_norm", "ffn2_norm", "ffn2_w_in", "ffn2_w_out", "mla_w_down",
           "mla_g_cq", "mla_g_ckv", "mla_w_uq", "mla_w_ukv", "mla_g_qn", "mla_g_kn", "mla_w_o", "dil_w_qkv", "dil_g_qn",
           "dil_g_kn", "dil_w_o")


def _tile(dim, pref, mult=LANES):
    if dim <= pref:
        return dim
    t = (pref // mult) * mult
    while t >= mult:
        if dim % t == 0:
            return t
        t -= mult
    return dim


def _column_chunks(width, parts=2):
    tiles = width // LANES
    assert width % LANES == 0, width
    bounds = [LANES * ((tiles * k + parts - 1) // parts) for k in range(parts + 1)]
    return [slice(lo, hi) for lo, hi in zip(bounds, bounds[1:]) if hi > lo]


def _params(*sem):
    return pltpu.CompilerParams(dimension_semantics=sem)


def _f32(shape):
    return jax.ShapeDtypeStruct(shape, jnp.float32)


def _act(shape):
    return jax.ShapeDtypeStruct(shape, MXU_DTYPE)


class Sharded:
    def __init__(self, arr, layer, axis):
        self.arr, self.layer, self.axis = arr, layer, axis
        n, _, r, c = arr.shape
        self.shape = (n * r, c) if axis == 0 else (r, n * c)
        self.per = r if axis == 0 else c

    def spec(self, tr, tc, rc_of):
        l = self.layer
        if self.axis == 0:
            n = self.per // tr

            def imap(*g):
                bi, bj = rc_of(*g)
                return (bi // n, l, bi % n, bj)
        else:
            n = self.per // tc

            def imap(*g):
                bi, bj = rc_of(*g)
                return (bj // n, l, bi, bj % n)
        return pl.BlockSpec((None, None, tr, tc), imap)


def _spec2(tr, tc, rc_of):
    return pl.BlockSpec((tr, tc), lambda *g: rc_of(*g))


def matmul(a, b, *, ta=False, tb=False, out_dtype=jnp.float32, scale=None, res=None, out_axis=None,
           name, tm=1024, tn=1024, tk=2048):
    am, ak = (a.shape[1], a.shape[0]) if ta else a.shape
    bk, bn = (b.shape[1], b.shape[0]) if tb else b.shape
    assert ak == bk, (name, a.shape, b.shape, ta, tb)
    M, N, K = am, bn, ak

    def per(x, axis):
        return x.per if isinstance(x, Sharded) and x.axis == axis else None

    def pick(dim, pref, *pers):
        return _tile(math.gcd(dim, *[p for p in pers if p is not None]), pref)

    tm = pick(M, tm, per(a, 1 if ta else 0), M // N_CHIPS if out_axis == 0 else None)
    tn = pick(N, tn, per(b, 0 if tb else 1), N // N_CHIPS if out_axis == 1 else None)
    tk = pick(K, tk, per(a, 0 if ta else 1), per(b, 1 if tb else 0))
    assert M % tm == 0 and N % tn == 0 and K % tk == 0, (name, M, N, K, tm, tn, tk)
    nk = K // tk

    a_rc = (lambda i, j, k: (k, i)) if ta else (lambda i, j, k: (i, k))
    b_rc = (lambda i, j, k: (j, k)) if tb else (lambda i, j, k: (k, j))
    a_blk = (tk, tm) if ta else (tm, tk)
    b_blk = (tn, tk) if tb else (tk, tn)
    a_spec = a.spec(*a_blk, a_rc) if isinstance(a, Sharded) else _spec2(*a_blk, a_rc)
    b_spec = b.spec(*b_blk, b_rc) if isinstance(b, Sharded) else _spec2(*b_blk, b_rc)
    dn = (((0 if ta else 1,), (1 if tb else 0,)), ((), ()))
    has_res = res is not None

    def body(*refs):
        if has_res:
            a_ref, b_ref, r_ref, o_ref, acc = refs
        else:
            a_ref, b_ref, o_ref, acc = refs
        k = pl.program_id(2)

        @pl.when(k == 0)
        def _():
            acc[...] = jnp.zeros_like(acc)

        acc[...] += lax.dot_general(a_ref[...].astype(MXU_DTYPE), b_ref[...].astype(MXU_DTYPE), dn,
                                    preferred_element_type=jnp.float32)

        @pl.when(k == nk - 1)
        def _():
            r = acc[...]
            if scale is not None:
                r = r * scale
            if has_res:
                r = r + r_ref[...]
            o_ref[...] = r.astype(o_ref.dtype)

    in_specs = [a_spec, b_spec]
    args = [a.arr if isinstance(a, Sharded) else a, b.arr if isinstance(b, Sharded) else b]
    if has_res:
        in_specs.append(_spec2(tm, tn, lambda i, j, k: (i, j)))
        args.append(res)
    o_rc = lambda i, j, k: (i, j)
    if out_axis is None:
        out_shape = jax.ShapeDtypeStruct((M, N), out_dtype)
        out_spec = _spec2(tm, tn, o_rc)
    else:
        shp = (N_CHIPS, 1, M // N_CHIPS, N) if out_axis == 0 else (N_CHIPS, 1, M, N // N_CHIPS)
        out_shape = jax.ShapeDtypeStruct(shp, out_dtype)
        out_spec = Sharded(out_shape, 0, out_axis).spec(tm, tn, o_rc)
    return pl.pallas_call(
        body, name=name, out_shape=out_shape, grid=(M // tm, N // tn, nk),
        in_specs=in_specs, out_specs=out_spec,
        scratch_shapes=[pltpu.VMEM((tm, tn), jnp.float32)],
        compiler_params=_params("parallel", "parallel", "arbitrary"),
    )(*args)


def _me():
    return lax.axis_index("x"), lax.axis_index("y"), lax.axis_index("c")


def _other_chips(x, y):
    return [(1 - x, y), (x, 1 - y), (1 - x, 1 - y)]


HBM = pl.BlockSpec(memory_space=pltpu.HBM)
SEM = pl.BlockSpec(memory_space=pltpu.SEMAPHORE)
N_PEERS = 3
TOKEN = jax.ShapeDtypeStruct((8, LANES), jnp.float32)


def _split_params():
    return pltpu.CompilerParams(has_side_effects=pltpu.SideEffectType.DATAFLOW_SIDE_EFFECTING)


def _in_hbm(a):
    return pltpu.with_memory_space_constraint(a, pltpu.HBM)


def exchange_start(bufs, plan, after, name):
    nb, n = len(bufs), plan.copies

    def body(*refs):
        sems, token = refs[nb + 1:nb + 1 + 2 * n], refs[-1]
        for j, (src, dst, to) in enumerate(plan(refs[:nb], _me(), False)):
            pltpu.make_async_remote_copy(src_ref=src, dst_ref=dst, send_sem=sems[j], recv_sem=sems[n + j],
                                         device_id=to, device_id_type=MESH).start()
        token[...] = jnp.zeros_like(token)

    outs = pl.pallas_call(
        body, name=name,
        out_shape=(pltpu.SemaphoreType.DMA(()),) * (2 * n) + tuple(pltpu.HBM(b.shape, b.dtype) for b in bufs) + (TOKEN,),
        in_specs=(HBM,) * nb + (ANY,), out_specs=(SEM,) * (2 * n) + (HBM,) * nb + (pl.BlockSpec(memory_space=pltpu.VMEM),),
        input_output_aliases={k: 2 * n + k for k in range(nb)}, compiler_params=_split_params(),
    )(*[_in_hbm(b) for b in bufs], after)
    return outs[:2 * n], outs[2 * n:2 * n + nb], outs[-1]


def exchange_wait(started, plan, after, name):
    sems, bufs, _ = started
    nb, n = len(bufs), plan.copies

    def body(*refs):
        sems_ = refs[nb:nb + 2 * n]
        for j, (src, got, to) in enumerate(plan(refs[:nb], _me(), True)):
            cp = pltpu.make_async_remote_copy(src_ref=src, dst_ref=got, send_sem=sems_[j], recv_sem=sems_[n + j],
                                              device_id=to, device_id_type=MESH)
            cp.wait_send()
            cp.wait_recv()

    return pl.pallas_call(
        body, name=name, out_shape=tuple(pltpu.HBM(b.shape, b.dtype) for b in bufs),
        in_specs=(HBM,) * nb + (SEM,) * (2 * n) + (ANY,), out_specs=(HBM,) * nb,
        input_output_aliases={k: k for k in range(nb)}, compiler_params=_split_params(),
    )(*bufs, *sems, after)


def _plan(copies):
    def mark(f):
        f.copies = copies
        return f
    return mark


@_plan(N_PEERS)
def ag_over_ici(refs, me, arrived):
    (land,), (x, y, cc) = refs, me
    mine = land.at[2 * x + y, :, cc]
    return [(mine, land.at[2 * px + py, :, cc] if arrived else mine, (px, py, cc)) for px, py in _other_chips(x, y)]


@_plan(N_PEERS)
def ag_to_sibling(refs, me, arrived):
    (land,), (x, y, cc) = refs, me
    return [(land.at[2 * px + py, :, cc], land.at[2 * px + py, :, (1 - cc) if arrived else cc], (x, y, 1 - cc))
            for px, py in _other_chips(x, y)]


@_plan(1)
def rs_to_sibling(refs, me, arrived):
    (g, r1), (x, y, cc) = refs, me
    return [(g.at[:, :, 1 - cc], r1, (x, y, 1 - cc))]


@_plan(N_PEERS)
def rs_over_ici(refs, me, arrived):
    (p, land), (x, y, cc) = refs, me
    return [(p.at[2 * px + py], land.at[j], (px, py, cc)) for j, (px, py) in enumerate(_other_chips(x, y))]


@_plan(1)
def rs_gather_sibling(refs, me, arrived):
    (red,), (x, y, cc) = refs, me
    return [(red.at[:, cc], red.at[:, (1 - cc) if arrived else cc], (x, y, 1 - cc))]


def cast_into_shards(w, layer, shard, after, name):
    L, r, c = w.shape
    tr, tc = _tile(r, 1024, 16), _tile(c, 2048)

    def body(shard_ref, w_ref, after_ref, o_ref):
        o_ref[...] = w_ref[...].astype(o_ref.dtype)

    grid_spec = pltpu.PrefetchScalarGridSpec(
        num_scalar_prefetch=1, grid=(r // tr, c // tc),
        in_specs=[pl.BlockSpec((None, tr, tc), lambda i, j, sh: (layer, i, j)), ANY],
        out_specs=pl.BlockSpec((None, None, tr, tc), lambda i, j, sh: (sh[0], 0, i, j)))
    return pl.pallas_call(body, name=name, grid_spec=grid_spec, out_shape=jax.ShapeDtypeStruct((N_CHIPS, 1, r, c), WIRE_DTYPE),
                          compiler_params=_params("parallel", "parallel"))(shard, w, after)


def add_sibling(g, r1, core, name):
    n, L, two, h, c = g.shape
    th = _tile(h, 1024, 16)
    tc = _tile(c, 2048)

    def body(core_ref, g_ref, r_ref, o_ref):
        o_ref[...] = (g_ref[...].astype(jnp.float32) + r_ref[...].astype(jnp.float32)).astype(o_ref.dtype)

    grid_spec = pltpu.PrefetchScalarGridSpec(
        num_scalar_prefetch=1, grid=(n, L, h // th, c // tc),
        in_specs=[pl.BlockSpec((None, None, None, th, tc), lambda s, l, i, j, core: (s, l, core[0], i, j)),
                  pl.BlockSpec((None, None, th, tc), lambda s, l, i, j, core: (s, l, i, j))],
        out_specs=pl.BlockSpec((None, None, th, tc), lambda s, l, i, j, core: (s, l, i, j)))
    return pl.pallas_call(body, name=name, grid_spec=grid_spec, out_shape=jax.ShapeDtypeStruct((n, L, h, c), WIRE_DTYPE),
                          compiler_params=_params("parallel", "parallel", "parallel", "parallel"))(core, g, r1)


def add_chips(p, r2, place, name):
    n, L, h, c = p.shape
    th = _tile(h, 1024, 16)
    tc = _tile(c, 2048)

    def body(shard_ref, core_ref, p_ref, r_ref, o_ref):
        acc = p_ref[...].astype(jnp.float32)
        for j in range(3):
            acc = acc + r_ref[j].astype(jnp.float32)
        o_ref[...] = acc

    grid_spec = pltpu.PrefetchScalarGridSpec(
        num_scalar_prefetch=2, grid=(L, h // th, c // tc),
        in_specs=[pl.BlockSpec((None, None, th, tc), lambda l, i, j, shard, core: (shard[0], l, i, j)),
                  pl.BlockSpec((3, None, th, tc), lambda l, i, j, shard, core: (0, l, i, j))],
        out_specs=pl.BlockSpec((None, None, th, tc), lambda l, i, j, shard, core: (l, core[0], i, j)))
    return pl.pallas_call(body, name=name, grid_spec=grid_spec, out_shape=_f32((L, 2, h, c)),
                          compiler_params=_params("parallel", "parallel", "parallel"))(*place, p, r2)


class ReduceScatter:
    def __init__(self, place, done):
        self.place, self.done = place, done
        self.stages = [[], [], []]
        self.token = None

    def _start(self, bufs, plan, name):
        started = exchange_start(bufs, plan, self.token if self.token is not None else bufs[0], name)
        self.token = started[2]
        return started

    def _advance(self, stage, after):
        key, started = self.stages[stage].pop(0)
        name = "rs_%s%d" % key
        if stage == 0:
            g, r1 = exchange_wait(started, rs_to_sibling, after, name + "_d2d_wait")
            p = add_sibling(g, r1, self.place[1], name + "_add1")
            land = lax.empty((N_PEERS,) + p.shape[1:], p.dtype)
            self.stages[1].append((key, self._start((p, land), rs_over_ici, name + "_ici_start")))
        elif stage == 1:
            p, r2 = exchange_wait(started, rs_over_ici, after, name + "_ici_wait")
            red = add_chips(p, r2, self.place, name + "_add2")
            self.stages[2].append((key, self._start((red,), rs_gather_sibling, name + "_gather_start")))
        else:
            (red,) = exchange_wait(started, rs_gather_sibling, after, name + "_gather_wait")
            L, two, h, c = red.shape
            self.done(key, red.reshape(2 * h, c))

    def push(self, key, g4):
        n, L, r, c = g4.shape
        g = g4.reshape(n, L, 2, r // 2, c)
        r1 = lax.empty((n, L, r // 2, c), g.dtype)
        self.stages[0].append((key, self._start((g, r1), rs_to_sibling, "rs_%s%d_d2d_start" % key)))
        for stage, depth in ((2, 1), (1, RS_WINDOW), (0, 1)):
            if len(self.stages[stage]) > depth:
                self._advance(stage, self.token)

    def drain(self, after):
        for stage in (0, 1, 2):
            while self.stages[stage]:
                self._advance(stage, after)


def all_reduce_small(v, name):
    R, C = v.shape

    def body(v_ref, o_ref, buf, send_sems, recv_sems):
        x, y, cc = _me()
        buf[0] = v_ref[...]
        cps = []
        for k in range(1, 8):
            dx, dy, dc = (k >> 2) & 1, (k >> 1) & 1, k & 1
            to = (x ^ dx, y ^ dy, cc ^ dc)
            cp = pltpu.make_async_remote_copy(src_ref=v_ref, dst_ref=buf.at[k], send_sem=send_sems.at[k],
                                              recv_sem=recv_sems.at[k], device_id=to, device_id_type=MESH)
            cp.start()
            cps.append(cp)
        for cp in cps:
            cp.wait()
        me = 4 * x + 2 * y + cc
        acc = buf[me]
        for a in range(1, 8):
            acc = acc + buf[a ^ me]
        o_ref[...] = acc

    vm = pl.BlockSpec(memory_space=pltpu.VMEM)
    return pl.pallas_call(
        body, name=name, out_shape=_f32((R, C)), in_specs=[vm], out_specs=vm,
        scratch_shapes=[pltpu.VMEM((8, R, C), jnp.float32), pltpu.SemaphoreType.DMA((8,)), pltpu.SemaphoreType.DMA((8,))],
    )(v)


def _rstd(x, n):
    return lax.rsqrt(jnp.sum(x * x, axis=-1, keepdims=True) * (1.0 / n) + EPS)


def _accumulate(ref, part, first):
    @pl.when(first)
    def _():
        ref[...] = part

    @pl.when(jnp.logical_not(first))
    def _():
        ref[...] += part


def rmsnorm_fwd(x, g, name):
    S, D = x.shape
    ts = _tile(S, 256, 8)

    def body(x_ref, g_ref, o_ref):
        xv = x_ref[...]
        o_ref[...] = (xv * _rstd(xv, D) * g_ref[...]).astype(o_ref.dtype)

    return pl.pallas_call(
        body, name=name, out_shape=_act((S, D)), grid=(S // ts,),
        in_specs=[pl.BlockSpec((ts, D), lambda i: (i, 0)), pl.BlockSpec((1, D), lambda i: (0, 0))],
        out_specs=pl.BlockSpec((ts, D), lambda i: (i, 0)), compiler_params=_params("parallel"))(x, g)


def _norm_bwd(x, g, dy, n):
    r = _rstd(x, n)
    xh = x * r
    dxh = dy * g
    dx = r * (dxh - xh * (jnp.sum(dxh * xh, axis=-1, keepdims=True) * (1.0 / n)))
    return dx, dy * xh


def rmsnorm_bwd(x, g, dy, dres, name):
    S, D = x.shape
    ts = _tile(S, 256, 8)

    def body(x_ref, g_ref, dy_ref, dres_ref, dx_ref, dg_ref):
        dx, dgp = _norm_bwd(x_ref[...], g_ref[...], dy_ref[...], D)
        dx_ref[...] = dres_ref[...] + dx
        _accumulate(dg_ref, jnp.sum(dgp, axis=0, keepdims=True), pl.program_id(0) == 0)

    row = pl.BlockSpec((ts, D), lambda i: (i, 0))
    vec = pl.BlockSpec((1, D), lambda i: (0, 0))
    return pl.pallas_call(
        body, name=name, out_shape=(_f32((S, D)), _f32((1, D))), grid=(S // ts,),
        in_specs=[row, vec, row, row], out_specs=(row, vec), compiler_params=_params("arbitrary"))(x, g, dy, dres)


def _sigmoid(x):
    return 1.0 / (1.0 + jnp.exp(-x))


def swiglu_in(xn, w_in, name, tm=512, tn=1408):
    S, D = xn.shape
    F = w_in.shape[1] // 2
    tm, tn = _tile(S, tm, 8), _tile(math.gcd(F, w_in.per), tn)
    nf = F // tn

    def body(x_ref, wg_ref, wu_ref, u_ref, a_ref):
        x = x_ref[...].astype(MXU_DTYPE)
        for cols in _column_chunks(tn):
            gt = jnp.dot(x, wg_ref[:, cols], preferred_element_type=jnp.float32)
            up = jnp.dot(x, wu_ref[:, cols], preferred_element_type=jnp.float32)
            u_ref[0, :, cols] = gt.astype(u_ref.dtype)
            u_ref[1, :, cols] = up.astype(u_ref.dtype)
            a_ref[:, cols] = (gt * _sigmoid(gt) * up).astype(a_ref.dtype)

    return pl.pallas_call(
        body, name=name, out_shape=(_act((2, 1, S, F)), _act((S, F))), grid=(nf, S // tm),
        in_specs=[pl.BlockSpec((tm, D), lambda j, i: (i, 0)), w_in.spec(D, tn, lambda j, i: (0, j)),
                  w_in.spec(D, tn, lambda j, i: (0, j + nf))],
        out_specs=(pl.BlockSpec((2, None, tm, tn), lambda j, i: (0, 0, i, j)), pl.BlockSpec((tm, tn), lambda j, i: (i, j))),
        compiler_params=_params("parallel", "parallel"))(xn, w_in.arr, w_in.arr)


def swiglu_out_bwd(dout, w_out, u, scale, name, tm=512, tn=1408):
    S, D = dout.shape
    F = w_out.shape[0]
    tm, tn = _tile(S, tm, 8), _tile(math.gcd(F, w_out.per), tn)

    def body(d_ref, w_ref, u_ref, o_ref):
        d = d_ref[...].astype(MXU_DTYPE)
        for cols in _column_chunks(tn):
            da = lax.dot_general(d, w_ref[cols, :], NT, preferred_element_type=jnp.float32) * scale
            gt, up = u_ref[0, :, cols].astype(jnp.float32), u_ref[1, :, cols].astype(jnp.float32)
            s = _sigmoid(gt)
            o_ref[0, :, cols] = (da * up * (s * (1.0 + gt * (1.0 - s)))).astype(o_ref.dtype)
            o_ref[1, :, cols] = (da * (gt * s)).astype(o_ref.dtype)

    planes = pl.BlockSpec((2, None, tm, tn), lambda j, i: (0, 0, i, j))
    return pl.pallas_call(
        body, name=name, out_shape=_act((2, 1, S, F)), grid=(F // tn, S // tm),
        in_specs=[pl.BlockSpec((tm, D), lambda j, i: (i, 0)), w_out.spec(tn, D, lambda j, i: (j, 0)), planes],
        out_specs=planes, compiler_params=_params("parallel", "parallel"))(dout, w_out.arr, u)


def loss_head(y, t, name):
    S, D = y.shape
    ts = _tile(S, 256, 8)

    def body(y_ref, t_ref, dy_ref, l_ref):
        e = y_ref[...] - t_ref[...]
        dy_ref[...] = e * (1.0 / D)
        l_ref[...] = jnp.full(l_ref.shape, 0.5 * jnp.sum(jnp.sum(e * e, axis=-1, keepdims=True) * (1.0 / D)), jnp.float32)

    row = pl.BlockSpec((ts, D), lambda i: (i, 0))
    dy, parts = pl.pallas_call(
        body, name=name, out_shape=(_f32((S, D)), _f32((S // ts, 8, LANES))), grid=(S // ts,),
        in_specs=[row, row], out_specs=(row, pl.BlockSpec((None, 8, LANES), lambda i: (i, 0, 0))),
        compiler_params=_params("parallel"))(y, t)
    return jnp.sum(parts[:, 0, 0]), dy


def rope_tables(S):
    inv = 1.0 / (ROPE_THETA ** (jnp.arange(0, ROPE_DIM, 2, dtype=jnp.float32) / ROPE_DIM))
    ang = jnp.arange(S, dtype=jnp.float32)[:, None] * inv[None, :]
    c, s = jnp.cos(ang), jnp.sin(ang)
    z = jnp.zeros_like(c)
    return (jnp.concatenate([c, c, z, z], axis=1), jnp.concatenate([-s, z, z, z], axis=1),
            jnp.concatenate([z, s, z, z], axis=1))


def _rope(x, cos, sa, sb):
    return x * cos + pltpu.roll(x, 96, 1) * sa + pltpu.roll(x, 32, 1) * sb


def _rope_t(d, cos, sa, sb):
    return d * cos + pltpu.roll(d * sa, 32, 1) + pltpu.roll(d * sb, 96, 1)


def _head_norm(x1, x2, g):
    r = lax.rsqrt((jnp.sum(x1 * x1, axis=-1, keepdims=True) + jnp.sum(x2 * x2, axis=-1, keepdims=True)) * (1.0 / QK_DIM) + EPS)
    return x1 * r * g[:, :LANES], x2 * r * g[:, LANES:], r


def _head_norm_bwd(x1, x2, g, d1, d2):
    _, _, r = _head_norm(x1, x2, g)
    h1, h2 = x1 * r, x2 * r
    e1, e2 = d1 * g[:, :LANES], d2 * g[:, LANES:]
    m = (jnp.sum(e1 * h1, axis=-1, keepdims=True) + jnp.sum(e2 * h2, axis=-1, keepdims=True)) * (1.0 / QK_DIM)
    return r * (e1 - h1 * m), r * (e2 - h2 * m), d1 * h1, d2 * h2


def mla_latent_fwd(lat, g_cq, g_ckv, name):
    S, W = lat.shape
    QL, KL = g_cq.shape[1], g_ckv.shape[1]
    ts = _tile(S, 256, 8)

    def body(l_ref, gq_ref, gk_ref, cq_ref, ckv_ref):
        a, b = l_ref[:, :QL], l_ref[:, QL:QL + KL]
        cq_ref[...] = (a * _rstd(a, QL) * gq_ref[...]).astype(cq_ref.dtype)
        ckv_ref[...] = (b * _rstd(b, KL) * gk_ref[...]).astype(ckv_ref.dtype)

    return pl.pallas_call(
        body, name=name, out_shape=(_act((S, QL)), _act((S, KL))), grid=(S // ts,),
        in_specs=[pl.BlockSpec((ts, W), lambda i: (i, 0)), pl.BlockSpec((1, QL), lambda i: (0, 0)),
                  pl.BlockSpec((1, KL), lambda i: (0, 0))],
        out_specs=(pl.BlockSpec((ts, QL), lambda i: (i, 0)), pl.BlockSpec((ts, KL), lambda i: (i, 0))),
        compiler_params=_params("parallel"))(lat, g_cq, g_ckv)


def mla_latent_bwd(dcq, dckv, dkpe, lat, g_cq, g_ckv, name):
    S, W = lat.shape
    QL, KL = g_cq.shape[1], g_ckv.shape[1]
    ts = _tile(S, 256, 8)

    def body(dq_ref, dk_ref, dp_ref, l_ref, gq_ref, gk_ref, o_ref, dgq_ref, dgk_ref):
        first = pl.program_id(0) == 0
        da, ga = _norm_bwd(l_ref[:, :QL], gq_ref[...], dq_ref[...], QL)
        db, gb = _norm_bwd(l_ref[:, QL:QL + KL], gk_ref[...], dk_ref[...], KL)
        o_ref[:, :QL] = da.astype(o_ref.dtype)
        o_ref[:, QL:QL + KL] = db.astype(o_ref.dtype)
        o_ref[:, QL + KL:] = dp_ref[...].astype(o_ref.dtype)
        _accumulate(dgq_ref, jnp.sum(ga, axis=0, keepdims=True), first)
        _accumulate(dgk_ref, jnp.sum(gb, axis=0, keepdims=True), first)

    row = lambda n: pl.BlockSpec((ts, n), lambda i: (i, 0))
    vec = lambda n: pl.BlockSpec((1, n), lambda i: (0, 0))
    return pl.pallas_call(
        body, name=name, out_shape=(_act((S, W)), _f32((1, QL)), _f32((1, KL))), grid=(S // ts,),
        in_specs=[row(QL), row(KL), row(LANES), row(W), vec(QL), vec(KL)], out_specs=(row(W), vec(QL), vec(KL)),
        compiler_params=_params("arbitrary"))(dcq, dckv, dkpe, lat, g_cq, g_ckv)


def mla_q_prep_fwd(qraw, g, tabs, H, name):
    S = qraw.shape[0]
    ts = _tile(S, PREP_ROWS, 8)

    def body(x_ref, g_ref, c_ref, a_ref, b_ref, o_ref):
        y1, y2, _ = _head_norm(x_ref[:, :LANES], x_ref[:, LANES:], g_ref[...])
        o_ref[:, :LANES] = y1.astype(o_ref.dtype)
        o_ref[:, LANES:] = _rope(y2, c_ref[...], a_ref[...], b_ref[...]).astype(o_ref.dtype)

    tab = pl.BlockSpec((ts, LANES), lambda i, h: (i, 0))
    return pl.pallas_call(
        body, name=name, out_shape=_act((H, S, QK_PAD)), grid=(S // ts, H),
        in_specs=[pl.BlockSpec((ts, QK_PAD), lambda i, h: (i, h)), pl.BlockSpec((1, QK_PAD), lambda i, h: (0, 0)), tab, tab, tab],
        out_specs=pl.BlockSpec((None, ts, QK_PAD), lambda i, h: (h, i, 0)),
        compiler_params=_params("parallel", "parallel"))(qraw, g, *tabs)


def mla_q_prep_bwd(dq, qraw, g, tabs, H, name):
    S = qraw.shape[0]
    ts = _tile(S, PREP_ROWS, 8)

    def body(d_ref, x_ref, g_ref, c_ref, a_ref, b_ref, o_ref, dg_ref):
        d2 = _rope_t(d_ref[:, LANES:], c_ref[...], a_ref[...], b_ref[...])
        dx1, dx2, g1, g2 = _head_norm_bwd(x_ref[:, :LANES], x_ref[:, LANES:], g_ref[...], d_ref[:, :LANES], d2)
        o_ref[:, :LANES] = dx1.astype(o_ref.dtype)
        o_ref[:, LANES:] = dx2.astype(o_ref.dtype)
        first = jnp.logical_and(pl.program_id(0) == 0, pl.program_id(1) == 0)
        part = jnp.concatenate([jnp.sum(g1, axis=0, keepdims=True), jnp.sum(g2, axis=0, keepdims=True)], axis=1)
        _accumulate(dg_ref, part, first)

    tab = pl.BlockSpec((ts, LANES), lambda i, h: (i, 0))
    vec = pl.BlockSpec((1, QK_PAD), lambda i, h: (0, 0))
    return pl.pallas_call(
        body, name=name, out_shape=(_act((S, H * QK_PAD)), _f32((1, QK_PAD))), grid=(S // ts, H),
        in_specs=[pl.BlockSpec((None, ts, QK_PAD), lambda i, h: (h, i, 0)), pl.BlockSpec((ts, QK_PAD), lambda i, h: (i, h)),
                  vec, tab, tab, tab],
        out_specs=(pl.BlockSpec((ts, QK_PAD), lambda i, h: (i, h)), vec),
        compiler_params=_params("arbitrary", "arbitrary"))(dq, qraw, g, *tabs)


def mla_k_prep_fwd(kvraw, lat, g, tabs, H, pe_blk, name):
    S = kvraw.shape[0]
    ts = _tile(S, PREP_ROWS, 8)

    def body(x_ref, p_ref, g_ref, c_ref, a_ref, b_ref, k_ref, v_ref):
        y1, y2, _ = _head_norm(x_ref[:, :LANES], p_ref[...], g_ref[...])
        k_ref[:, :LANES] = y1.astype(k_ref.dtype)
        k_ref[:, LANES:] = _rope(y2, c_ref[...], a_ref[...], b_ref[...]).astype(k_ref.dtype)
        v_ref[...] = x_ref[:, LANES:].astype(v_ref.dtype)

    tab = pl.BlockSpec((ts, LANES), lambda i, h: (i, 0))
    return pl.pallas_call(
        body, name=name, out_shape=(_act((H, S, QK_PAD)), _act((H, S, LANES))), grid=(S // ts, H),
        in_specs=[pl.BlockSpec((ts, QK_PAD), lambda i, h: (i, h)), pl.BlockSpec((ts, LANES), lambda i, h: (i, pe_blk)),
                  pl.BlockSpec((1, QK_PAD), lambda i, h: (0, 0)), tab, tab, tab],
        out_specs=(pl.BlockSpec((None, ts, QK_PAD), lambda i, h: (h, i, 0)), pl.BlockSpec((None, ts, LANES), lambda i, h: (h, i, 0))),
        compiler_params=_params("parallel", "parallel"))(kvraw, lat, g, *tabs)


def mla_k_prep_bwd(dk, dv, kvraw, lat, g, tabs, H, pe_blk, name):
    S = kvraw.shape[0]
    ts = _tile(S, PREP_ROWS, 8)

    def body(dk_ref, dv_ref, x_ref, p_ref, g_ref, c_ref, a_ref, b_ref, o_ref, dp_ref, dg_ref):
        i, h = pl.program_id(0), pl.program_id(1)
        d2 = _rope_t(dk_ref[:, LANES:], c_ref[...], a_ref[...], b_ref[...])
        dx1, dx2, g1, g2 = _head_norm_bwd(x_ref[:, :LANES], p_ref[...], g_ref[...], dk_ref[:, :LANES], d2)
        o_ref[:, :LANES] = dx1.astype(o_ref.dtype)
        o_ref[:, LANES:] = dv_ref[...].astype(o_ref.dtype)
        _accumulate(dp_ref, dx2, h == 0)
        part = jnp.concatenate([jnp.sum(g1, axis=0, keepdims=True), jnp.sum(g2, axis=0, keepdims=True)], axis=1)
        _accumulate(dg_ref, part, jnp.logical_and(i == 0, h == 0))

    tab = pl.BlockSpec((ts, LANES), lambda i, h: (i, 0))
    vec = pl.BlockSpec((1, QK_PAD), lambda i, h: (0, 0))
    return pl.pallas_call(
        body, name=name, out_shape=(_act((S, H * QK_PAD)), _f32((S, LANES)), _f32((1, QK_PAD))), grid=(S // ts, H),
        in_specs=[pl.BlockSpec((None, ts, QK_PAD), lambda i, h: (h, i, 0)), pl.BlockSpec((None, ts, LANES), lambda i, h: (h, i, 0)),
                  pl.BlockSpec((ts, QK_PAD), lambda i, h: (i, h)), pl.BlockSpec((ts, LANES), lambda i, h: (i, pe_blk)),
                  vec, tab, tab, tab],
        out_specs=(pl.BlockSpec((ts, QK_PAD), lambda i, h: (i, h)), tab, vec),
        compiler_params=_params("arbitrary", "arbitrary"))(dk, dv, kvraw, lat, g, *tabs)


def _causal_scores(q, k, scale, diagonal, row0=0):
    s = lax.dot_general(q, k, NT, preferred_element_type=jnp.float32) * scale
    if not diagonal:
        return s
    row = lax.broadcasted_iota(jnp.int32, s.shape, 0) + row0
    col = lax.broadcasted_iota(jnp.int32, s.shape, 1)
    return jnp.where(col <= row, s, NEG)


def _on_causal_blocks(qi, ki, step):
    @pl.when(ki < qi)
    def _():
        step(False)

    @pl.when(ki == qi)
    def _():
        step(True)


def mla_attention_fwd(q, k, v, name):
    H, S, _ = q.shape
    t = _tile(S, ATTN_BLOCK)
    n = S // t
    scale = 1.0 / math.sqrt(QK_DIM)

    rc = _tile(t, 256, 8)

    def body(q_ref, k_ref, v_ref, o_ref, lse_ref, m_sc, l_sc, acc):
        qi, ki = pl.program_id(1), pl.program_id(2)

        @pl.when(ki == 0)
        def _():
            m_sc[...] = jnp.full(m_sc.shape, NEG, jnp.float32)
            l_sc[...] = jnp.zeros_like(l_sc)
            acc[...] = jnp.zeros_like(acc)

        def step(diagonal):
            for r0 in range(0, t, rc):
                rows = slice(r0, r0 + rc)
                s = _causal_scores(q_ref[rows, :], k_ref[...], scale, diagonal, r0)
                m_old = m_sc[rows, :]
                m_new = jnp.maximum(m_old, jnp.max(s, axis=-1, keepdims=True))
                alpha = jnp.exp(m_old - m_new)
                p = jnp.exp(s - m_new)
                l_sc[rows, :] = alpha * l_sc[rows, :] + jnp.sum(p, axis=-1, keepdims=True)
                acc[rows, :] = alpha * acc[rows, :] + jnp.dot(p.astype(MXU_DTYPE), v_ref[...], preferred_element_type=jnp.float32)
                m_sc[rows, :] = m_new

        _on_causal_blocks(qi, ki, step)

        @pl.when(ki == qi)
        def _():
            o_ref[...] = (acc[...] / l_sc[...]).astype(o_ref.dtype)
            lse_ref[...] = m_sc[...] + jnp.log(l_sc[...])

    kv = lambda w: pl.BlockSpec((None, t, w), lambda h, qi, ki: (h, jnp.minimum(ki, qi), 0))
    return pl.pallas_call(
        body, name=name, out_shape=(_act((S, H * LANES)), _f32((H, S, 1))), grid=(H, n, n),
        in_specs=[pl.BlockSpec((None, t, QK_PAD), lambda h, qi, ki: (h, qi, 0)), kv(QK_PAD), kv(LANES)],
        out_specs=(pl.BlockSpec((t, LANES), lambda h, qi, ki: (qi, h)), pl.BlockSpec((None, t, 1), lambda h, qi, ki: (h, qi, 0))),
        scratch_shapes=[pltpu.VMEM((t, 1), jnp.float32), pltpu.VMEM((t, 1), jnp.float32), pltpu.VMEM((t, LANES), jnp.float32)],
        compiler_params=_params("parallel", "parallel", "arbitrary"))(q, k, v)


def mla_attention_bwd(q, k, v, do, o, lse, name):
    H, S, _ = q.shape
    t = _tile(S, ATTN_BLOCK)
    n = S // t
    scale = 1.0 / math.sqrt(QK_DIM)

    def body(q_ref, k_ref, v_ref, do_ref, o_ref, lse_ref, dq_ref, dk_ref, dv_ref, dk_acc, dv_acc):
        ki, qi = pl.program_id(1), pl.program_id(2)
        rows = pl.ds(pl.multiple_of(qi * t, t), t)

        @pl.when(qi == 0)
        def _():
            dk_acc[...] = jnp.zeros_like(dk_acc)
            dv_acc[...] = jnp.zeros_like(dv_acc)

        @pl.when(ki == 0)
        def _():
            dq_ref[rows, :] = jnp.zeros((t, QK_PAD), jnp.float32)

        def step(diagonal):
            p = jnp.exp(_causal_scores(q_ref[...], k_ref[...], scale, diagonal) - lse_ref[...])
            dof = do_ref[...]
            dob = dof.astype(MXU_DTYPE)
            delta = jnp.sum(dof * o_ref[...].astype(jnp.float32), axis=-1, keepdims=True)
            dv_acc[...] += lax.dot_general(p.astype(MXU_DTYPE), dob, TN, preferred_element_type=jnp.float32)
            dp = lax.dot_general(dob, v_ref[...], NT, preferred_element_type=jnp.float32)
            ds = (p * (dp - delta)).astype(MXU_DTYPE)
            dk_acc[...] += lax.dot_general(ds, q_ref[...], TN, preferred_element_type=jnp.float32)
            dq_ref[rows, :] += jnp.dot(ds, k_ref[...], preferred_element_type=jnp.float32) * scale

        _on_causal_blocks(qi, ki, step)

        @pl.when(qi == n - 1)
        def _():
            dk_ref[...] = dk_acc[...] * scale
            dv_ref[...] = dv_acc[...]

    qrow = lambda h, ki, qi: (h, jnp.maximum(qi, ki), 0)
    kv = lambda w: pl.BlockSpec((None, t, w), lambda h, ki, qi: (h, ki, 0))
    col = pl.BlockSpec((None, t, 1), qrow)
    head = pl.BlockSpec((t, LANES), lambda h, ki, qi: (jnp.maximum(qi, ki), h))
    return pl.pallas_call(
        body, name=name, out_shape=(_f32((H, S, QK_PAD)), _f32((H, S, QK_PAD)), _f32((H, S, LANES))), grid=(H, n, n),
        in_specs=[pl.BlockSpec((None, t, QK_PAD), qrow), kv(QK_PAD), kv(LANES),
                  head, head, col],
        out_specs=(pl.BlockSpec((None, S, QK_PAD), lambda h, ki, qi: (h, 0, 0)), kv(QK_PAD), kv(LANES)),
        scratch_shapes=[pltpu.VMEM((t, QK_PAD), jnp.float32), pltpu.VMEM((t, LANES), jnp.float32)],
        compiler_params=_params("parallel", "arbitrary", "arbitrary"))(q, k, v, do, o, lse)


def _alibi_slopes(G, Hd):
    k = np.arange(1, G * Hd + 1, dtype=np.float32)
    s = (2.0 ** (-8.0 * k / (G * Hd))).astype(np.float32).reshape(G, Hd)
    return jnp.asarray(np.broadcast_to(s[:, :, None, None], (G, Hd, 1, LANES)).copy())


def _dil_scores(qn, kn, scale, slope_d, prev, valid):
    s = lax.dot_general(qn, kn, NT, preferred_element_type=jnp.float32) * scale
    iq = lax.broadcasted_iota(jnp.int32, s.shape, 0)
    ik = lax.broadcasted_iota(jnp.int32, s.shape, 1)
    dist = iq - ik + (BLK if prev else 0)
    ok = (ik >= iq) if prev else (ik <= iq)
    s = s - slope_d * dist.astype(jnp.float32)
    return jnp.where(jnp.logical_and(ok, valid), s, NEG)


def _dil_scores_both(qn, k2, scale, slope_d, has_prev):
    s = lax.dot_general(qn, k2, NT, preferred_element_type=jnp.float32) * scale
    iq = lax.broadcasted_iota(jnp.int32, s.shape, 0)
    ik = lax.broadcasted_iota(jnp.int32, s.shape, 1)
    dist = iq + BLK - ik
    ok = jnp.logical_and(jnp.logical_and(dist >= 0, dist <= BLK), jnp.logical_or(ik >= BLK, has_prev))
    return jnp.where(ok, s - slope_d * dist.astype(jnp.float32), NEG)


def _dil_heads(d, Hd, block_bytes):
    hb = max(1, min(Hd, block_bytes // (BLK * LANES * 4))) if d == 1 else 1
    assert Hd % hb == 0, (Hd, hb)
    return hb


def _dil_rows(r, d):
    return pl.ds(r, BLK, stride=d) if d > 1 else slice(None)


def _loop_residues(d, residue, init):
    per = math.gcd(d, 4)

    def one_pass(i, carry):
        for k in range(per):
            carry = residue(per * i + k, carry)
        return carry

    return one_pass(0, init) if d == per else lax.fori_loop(0, d // per, one_pass, init)


def _dil_specs(d, nblk, Hd, G, g, hb):
    def spec(kind, shift):
        col0 = (kind * G + g) * Hd // hb
        return pl.BlockSpec((BLK * d, hb * LANES), lambda n, hg: (jnp.clip(n + shift, 0, nblk - 1), col0 + hg))
    return spec


def _head_spec(d, nblk, hb, shift):
    return pl.BlockSpec((BLK * d, hb * LANES), lambda n, hg: (jnp.clip(n + shift, 0, nblk - 1), hg))


def dilated_fwd(qkv, gq, gk, slopes, g, d, Hd, G, name):
    S, C = qkv.shape
    nblk = S // (BLK * d)
    hb = _dil_heads(d, Hd, 2 << 20)
    scale = 1.0 / math.sqrt(LANES)
    spec = _dil_specs(d, nblk, Hd, G, g, hb)

    def body(q_ref, kc_ref, kp_ref, vc_ref, vp_ref, gq_ref, gk_ref, sl_ref, o_ref, l_ref):
        n = pl.program_id(0)
        nrm = lambda t, gg: (t * _rstd(t, LANES) * gg).astype(MXU_DTYPE)

        def residue(r, carry):
            rows = _dil_rows(r, d)
            for hh in range(hb):
                cols = slice(hh * LANES, (hh + 1) * LANES)
                slope_d = sl_ref[hh][:, :1] * float(d)
                qn = nrm(q_ref[rows, cols], gq_ref[...])
                k2 = jnp.concatenate([nrm(kp_ref[rows, cols], gk_ref[...]), nrm(kc_ref[rows, cols], gk_ref[...])], axis=0)
                v2 = jnp.concatenate([vp_ref[rows, cols], vc_ref[rows, cols]], axis=0).astype(MXU_DTYPE)
                s = _dil_scores_both(qn, k2, scale, slope_d, n > 0)
                m = jnp.max(s, axis=-1, keepdims=True)
                e = jnp.exp(s - m)
                tot = jnp.sum(e, axis=-1, keepdims=True)
                o = jnp.dot((e * (1.0 / tot)).astype(MXU_DTYPE), v2, preferred_element_type=jnp.float32)
                o_ref[rows, cols] = o
                l_ref[rows, cols] = jnp.broadcast_to(m + jnp.log(tot), (BLK, LANES))
            return carry

        _loop_residues(d, residue, 0)

    vec = pl.BlockSpec((1, LANES), lambda n, hg: (0, 0))
    out = _head_spec(d, nblk, hb, 0)
    return pl.pallas_call(
        body, name=name, out_shape=(_f32((S, Hd * LANES)), _f32((S, Hd * LANES))), grid=(nblk, Hd // hb),
        in_specs=[spec(0, 0), spec(1, 0), spec(1, -1), spec(2, 0), spec(2, -1), vec, vec,
                  pl.BlockSpec((None, hb, 1, LANES), lambda n, hg: (g, hg, 0, 0))],
        out_specs=(out, out), compiler_params=_params("parallel", "parallel"),
    )(qkv, qkv, qkv, qkv, qkv, gq, gk, slopes)


def dilated_merge(os_, ls_, name):
    S, W = os_[0].shape
    G = len(os_)
    ts, tw = _tile(S, 512, 8), _tile(W, 512)

    def body(*refs):
        o_refs, l_refs, (o_ref, t_ref) = refs[:G], refs[G:2 * G], refs[2 * G:]
        ls = [r[...] for r in l_refs]
        m = ls[0]
        for l in ls[1:]:
            m = jnp.maximum(m, l)
        es = [jnp.exp(l - m) for l in ls]
        tot = es[0]
        for e in es[1:]:
            tot = tot + e
        acc = o_refs[0][...] * (es[0] / tot)
        for r, e in zip(o_refs[1:], es[1:]):
            acc = acc + r[...] * (e / tot)
        o_ref[...] = acc.astype(o_ref.dtype)
        t_ref[...] = m + jnp.log(tot)

    blk = pl.BlockSpec((ts, tw), lambda i, j: (i, j))
    return pl.pallas_call(
        body, name=name, out_shape=(_act((S, W)), _f32((S, W))), grid=(S // ts, W // tw),
        in_specs=[blk] * (2 * G), out_specs=(blk, blk), compiler_params=_params("parallel", "parallel"))(*os_, *ls_)


def dilated_delta(do, o, name):
    S, W = do.shape
    ts = _tile(S, 512, 8)

    def body(d_ref, o_ref, out_ref):
        out_ref[...] = jnp.broadcast_to(jnp.sum(d_ref[...] * o_ref[...].astype(jnp.float32), axis=-1, keepdims=True), out_ref.shape)

    blk = pl.BlockSpec((ts, LANES), lambda i, h: (i, h))
    return pl.pallas_call(body, name=name, out_shape=_f32((S, W)), grid=(S // ts, W // LANES), in_specs=[blk, blk],
                          out_specs=blk, compiler_params=_params("parallel", "parallel"))(do, o)


def dilated_bwd(qkv, do, lse, delta, gq, gk, slopes, g, d, Hd, G, name):
    S, C = qkv.shape
    nblk = S // (BLK * d)
    W = Hd * LANES
    scale = 1.0 / math.sqrt(LANES)
    hb = _dil_heads(d, Hd, 1 << 20)
    spec = _dil_specs(d, nblk, Hd, G, g, hb)
    hspec = lambda shift: _head_spec(d, nblk, hb, shift)

    def body(q_ref, qx_ref, kc_ref, kp_ref, vc_ref, vp_ref, do_ref, dox_ref, l_ref, lx_ref, dl_ref, dlx_ref,
             gq_ref, gk_ref, sl_ref, dq_ref, dk_ref, dv_ref, dgq_ref, dgk_ref):
        n = pl.program_id(0)
        gqv, gkv = gq_ref[...], gk_ref[...]
        nrm = lambda t, gg: (t * _rstd(t, LANES) * gg).astype(MXU_DTYPE)
        f32dot = lambda a, b, dn: lax.dot_general(a, b, dn, preferred_element_type=jnp.float32)

        def residue(r, carry):
            rows = _dil_rows(r, d)
            dgq_sum, dgk_sum = carry
            for hh in range(hb):
                cols = slice(hh * LANES, (hh + 1) * LANES)
                slope_d = sl_ref[hh][:, :1] * float(d)
                q, kc = q_ref[rows, cols], kc_ref[rows, cols]
                qn, qxn = nrm(q, gqv), nrm(qx_ref[rows, cols], gqv)
                kcn, kpn = nrm(kc, gkv), nrm(kp_ref[rows, cols], gkv)
                vc, vp = vc_ref[rows, cols].astype(MXU_DTYPE), vp_ref[rows, cols].astype(MXU_DTYPE)
                dob, doxb = do_ref[rows, cols].astype(MXU_DTYPE), dox_ref[rows, cols].astype(MXU_DTYPE)
                lrow, lxrow = l_ref[rows, cols][:, :1], lx_ref[rows, cols][:, :1]
                drow, dxrow = dl_ref[rows, cols][:, :1], dlx_ref[rows, cols][:, :1]
                pc = jnp.exp(_dil_scores(qn, kcn, scale, slope_d, False, True) - lrow)
                pp = jnp.exp(_dil_scores(qn, kpn, scale, slope_d, True, n > 0) - lrow)
                dsc = pc * (f32dot(dob, vc, NT) - drow)
                dsp = pp * (f32dot(dob, vp, NT) - drow)
                dqn = (jnp.dot(dsc.astype(MXU_DTYPE), kcn, preferred_element_type=jnp.float32)
                       + jnp.dot(dsp.astype(MXU_DTYPE), kpn, preferred_element_type=jnp.float32)) * scale
                dq, dgq = _norm_bwd(q, gqv, dqn, LANES)
                dq_ref[rows, cols] = dq
                px = jnp.exp(_dil_scores(qxn, kcn, scale, slope_d, True, n < nblk - 1) - lxrow)
                dsx = px * (f32dot(doxb, vc, NT) - dxrow)
                dkn = (f32dot(dsc.astype(MXU_DTYPE), qn, TN) + f32dot(dsx.astype(MXU_DTYPE), qxn, TN)) * scale
                dk, dgk = _norm_bwd(kc, gkv, dkn, LANES)
                dk_ref[rows, cols] = dk
                dv_ref[rows, cols] = f32dot(pc.astype(MXU_DTYPE), dob, TN) + f32dot(px.astype(MXU_DTYPE), doxb, TN)
                dgq_sum = dgq_sum + jnp.sum(dgq, axis=0, keepdims=True)
                dgk_sum = dgk_sum + jnp.sum(dgk, axis=0, keepdims=True)
            return dgq_sum, dgk_sum

        zero = jnp.zeros((1, LANES), jnp.float32)
        dgq_sum, dgk_sum = _loop_residues(d, residue, (zero, zero))
        first = jnp.logical_and(n == 0, pl.program_id(1) == 0)
        _accumulate(dgq_ref, dgq_sum, first)
        _accumulate(dgk_ref, dgk_sum, first)

    vec = pl.BlockSpec((1, LANES), lambda n, hg: (0, 0))
    out = hspec(0)
    return pl.pallas_call(
        body, name=name, out_shape=(_f32((S, W)), _f32((S, W)), _f32((S, W)), _f32((1, LANES)), _f32((1, LANES))),
        grid=(nblk, Hd // hb),
        in_specs=[spec(0, 0), spec(0, 1), spec(1, 0), spec(1, -1), spec(2, 0), spec(2, -1), hspec(0), hspec(1), hspec(0), hspec(1),
                  hspec(0), hspec(1), vec, vec, pl.BlockSpec((None, hb, 1, LANES), lambda n, hg: (g, hg, 0, 0))],
        out_specs=(out, out, out, vec, vec), compiler_params=_params("arbitrary", "arbitrary"),
    )(qkv, qkv, qkv, qkv, qkv, qkv, do, do, lse, lse, delta, delta, gq, gk, slopes)


def adamw(w, g, m, v, layer, prev, name):
    L, r, c = w.shape
    tr, tc = _tile(r, 512, 8), _tile(c, 1024)
    c1 = 1.0 / (1.0 - ADAM_B1 ** ADAM_STEP)
    c2 = 1.0 / (1.0 - ADAM_B2 ** ADAM_STEP)

    def body(*refs):
        w_ref, g_ref, m_ref, v_ref = refs[:4]
        go_ref, d_ref, mo_ref, vo_ref = refs[-4:]
        gv = g_ref[...]
        mn = ADAM_B1 * m_ref[...] + (1.0 - ADAM_B1) * gv
        vn = ADAM_B2 * v_ref[...] + (1.0 - ADAM_B2) * (gv * gv)
        go_ref[...] = gv
        d_ref[...] = -ADAM_LR * ((mn * c1) / (jnp.sqrt(vn * c2) + ADAM_EPS) + ADAM_WD * w_ref[...])
        mo_ref[...] = mn
        vo_ref[...] = vn

    lay = pl.BlockSpec((None, tr, tc), lambda i, j: (layer, i, j))
    flat = pl.BlockSpec((tr, tc), lambda i, j: (i, j))
    ins = [w, g, m, v] + (list(prev) if prev is not None else [])
    in_specs = [lay, flat, lay, lay] + ([ANY] * 4 if prev is not None else [])
    return pl.pallas_call(
        body, name=name, out_shape=tuple(_f32((L, r, c)) for _ in range(4)), grid=(r // tr, c // tc),
        in_specs=in_specs, out_specs=(lay, lay, lay, lay),
        input_output_aliases=({4 + k: k for k in range(4)} if prev is not None else {}),
        compiler_params=_params("parallel", "parallel"))(*ins)


def _ffn_fwd(h, g, W, kind, tag):
    xn = rmsnorm_fwd(h, g, tag + "_norm")
    u, a = swiglu_in(xn, W(kind + "_w_in", h), tag + "_in")
    out = matmul(a, W(kind + "_w_out", a), scale=0.5, res=h, name=tag + "_out", tk=2816)
    return out, (h, xn, u, a)


def _ffn_bwd(dout, saved, g, W, emit, kind, tag):
    h, xn, u, a = saved
    emit(kind + "_w_out", matmul(a, dout, ta=True, scale=0.5, out_dtype=WIRE_DTYPE, out_axis=0, name=tag + "_dwout", tm=1408, tk=2048))
    du = Sharded(swiglu_out_bwd(dout, W(kind + "_w_out", None), u, 0.5, tag + "_da"), 0, 1)
    emit(kind + "_w_in", matmul(xn, du, ta=True, out_dtype=WIRE_DTYPE, out_axis=1, name=tag + "_dwin", tn=1408, tk=2048))
    dxn = matmul(du, W(kind + "_w_in", None), tb=True, name=tag + "_dxn", tk=2816)
    return rmsnorm_bwd(h, g, dxn, dout, tag + "_dnorm")


def _pad_gain(g):
    return jnp.pad(g, ((0, 0), (0, QK_PAD - QK_DIM)))


def _mla_fwd(h, P, W, tabs, H):
    g_mix, g_cq, g_ckv = P["mix_norm"][0:1], P["mla_g_cq"], P["mla_g_ckv"]
    pe_blk = (g_cq.shape[1] + g_ckv.shape[1]) // LANES
    xn = rmsnorm_fwd(h, g_mix, "mla_norm")
    w_down = W("mla_w_down", h)
    lat = matmul(xn, w_down, name="mla_down", tn=w_down.shape[1])
    cq, ckv = mla_latent_fwd(lat, g_cq, g_ckv, "mla_latent")
    qraw = matmul(cq, W("mla_w_uq", lat), name="mla_uq")
    kvraw = matmul(ckv, W("mla_w_ukv", qraw), name="mla_ukv")
    q = mla_q_prep_fwd(qraw, _pad_gain(P["mla_g_qn"]), tabs, H, "mla_qprep")
    k, v = mla_k_prep_fwd(kvraw, lat, _pad_gain(P["mla_g_kn"]), tabs, H, pe_blk, "mla_kprep")
    o, lse = mla_attention_fwd(q, k, v, "mla_attn")
    out = matmul(o, W("mla_w_o", lse), res=h, name="mla_o")
    return out, (h, xn, lat, cq, ckv, qraw, kvraw, q, k, v, o, lse, pe_blk)


def _mla_bwd(dout, saved, P, W, emit, tabs, H):
    h, xn, lat, cq, ckv, qraw, kvraw, q, k, v, o, lse, pe_blk = saved
    emit("mla_w_o", matmul(o, dout, ta=True, out_dtype=WIRE_DTYPE, out_axis=0, name="mla_dwo", tm=512))
    do = matmul(dout, W("mla_w_o", None), tb=True, name="mla_do", tn=512)
    dq, dk, dv = mla_attention_bwd(q, k, v, do, o, lse, "mla_attn_bwd")
    dqraw, dgq = mla_q_prep_bwd(dq, qraw, _pad_gain(P["mla_g_qn"]), tabs, H, "mla_dqprep")
    dkvraw, dkpe, dgk = mla_k_prep_bwd(dk, dv, kvraw, lat, _pad_gain(P["mla_g_kn"]), tabs, H, pe_blk, "mla_dkprep")
    emit("mla_w_uq", matmul(cq, dqraw, ta=True, out_dtype=WIRE_DTYPE, out_axis=1, name="mla_dwuq"))
    dcq = matmul(dqraw, W("mla_w_uq", None), tb=True, name="mla_dcq", tk=1024)
    emit("mla_w_ukv", matmul(ckv, dkvraw, ta=True, out_dtype=WIRE_DTYPE, out_axis=1, name="mla_dwukv"))
    dckv = matmul(dkvraw, W("mla_w_ukv", None), tb=True, name="mla_dckv", tk=1024)
    dlat, dgcq, dgckv = mla_latent_bwd(dcq, dckv, dkpe, lat, P["mla_g_cq"], P["mla_g_ckv"], "mla_dlatent")
    emit("mla_w_down", matmul(xn, dlat, ta=True, out_dtype=WIRE_DTYPE, out_axis=0, name="mla_dwdown", tm=512, tn=dlat.shape[1]))
    dxn = matmul(dlat, W("mla_w_down", None), tb=True, name="mla_dxn", tn=512, tk=dlat.shape[1])
    dh, dgm = rmsnorm_bwd(h, P["mix_norm"][0:1], dxn, dout, "mla_dnorm")
    return dh, dgm, dict(mla_g_qn=dgq[:, :QK_DIM], mla_g_kn=dgk[:, :QK_DIM], mla_g_cq=dgcq, mla_g_ckv=dgckv)


def _dil_fwd(h, P, W, slopes, Hd):
    G = len(DIL_PAIRS)
    xn = rmsnorm_fwd(h, P["mix_norm"][1:2], "dil_norm")
    qkv = matmul(xn, W("dil_w_qkv", h), name="dil_qkv", tn=1152)
    os_, ls_ = [], []
    for g, (_, d) in enumerate(DIL_PAIRS):
        o_g, l_g = dilated_fwd(qkv, P["dil_g_qn"], P["dil_g_kn"], slopes, g, d, Hd, G, f"dil_attn{g}")
        os_.append(o_g)
        ls_.append(l_g)
    o, lse = dilated_merge(os_, ls_, "dil_merge")
    out = matmul(o, W("dil_w_o", lse), res=h, name="dil_o", tn=512)
    return out, (h, xn, qkv, o, lse)


def _dil_bwd(dout, saved, P, W, emit, slopes, Hd):
    h, xn, qkv, o, lse = saved
    ngrp = len(DIL_PAIRS)
    emit("dil_w_o", matmul(o, dout, ta=True, out_dtype=WIRE_DTYPE, out_axis=1, name="dil_dwo", tn=512))
    do = matmul(dout, W("dil_w_o", None), tb=True, name="dil_do", tk=512)
    delta = dilated_delta(do, o, "dil_delta")
    parts = [dilated_bwd(qkv, do, lse, delta, P["dil_g_qn"], P["dil_g_kn"], slopes, g, d, Hd, ngrp, f"dil_dattn{g}")
             for g, (_, d) in enumerate(DIL_PAIRS)]
    dqkv = jnp.concatenate([p[kind] for kind in range(3) for p in parts], axis=1).astype(MXU_DTYPE)
    emit("dil_w_qkv", matmul(xn, dqkv, ta=True, out_dtype=WIRE_DTYPE, out_axis=1, name="dil_dwqkv", tn=1152))
    dxn = matmul(dqkv, W("dil_w_qkv", None), tb=True, name="dil_dxn", tk=2304)
    dh, dgm = rmsnorm_bwd(h, P["mix_norm"][1:2], dxn, dout, "dil_dnorm")
    return dh, dgm, dict(dil_g_qn=parts[0][3] + parts[1][3] + parts[2][3], dil_g_kn=parts[0][4] + parts[1][4] + parts[2][4])


def local_step(x, target, P, get_w, on_grad):
    S, D = x.shape
    H, Hd = MLA_HEADS, DIL_HEADS
    tabs = rope_tables(S)
    slopes = _alibi_slopes(len(DIL_PAIRS), Hd)
    cache = {}

    def weights_of(layer):
        def W(name, after):
            if (name, layer) not in cache:
                cache[name, layer] = Sharded(get_w(name, layer, after), 0, SHARD_AXIS[name])
            return cache[name, layer]
        return W

    row = lambda name, i: P[name][i:i + 1]
    h = x
    saved = []
    for i in range(2):
        W = weights_of(i)
        h, s1 = _ffn_fwd(h, row("ffn1_norm", i), W, "ffn1", f"l{i}_ffn1")
        h, sm = _mla_fwd(h, P, weights_of(0), tabs, H) if i == 0 else _dil_fwd(h, P, weights_of(0), slopes, Hd)
        h, s2 = _ffn_fwd(h, row("ffn2_norm", i), W, "ffn2", f"l{i}_ffn2")
        saved.append((s1, sm, s2))
    loss, dh = loss_head(h, target, "loss")

    gs = {n: [None, None] for n in ("ffn1_norm", "mix_norm", "ffn2_norm")}
    for i in (1, 0):
        s1, sm, s2 = saved[i]
        W = weights_of(i)
        emit = lambda name, g4, layer=i: on_grad(name, layer, g4)
        emit0 = lambda name, g4: on_grad(name, 0, g4)
        dh, gs["ffn2_norm"][i] = _ffn_bwd(dh, s2, row("ffn2_norm", i), W, emit, "ffn2", f"l{i}_ffn2")
        if i == 0:
            dh, gs["mix_norm"][i], gm = _mla_bwd(dh, sm, P, weights_of(0), emit0, tabs, H)
        else:
            dh, gs["mix_norm"][i], gm = _dil_bwd(dh, sm, P, weights_of(0), emit0, slopes, Hd)
        gs.update({n: [val] for n, val in gm.items()})
        dh, gs["ffn1_norm"][i] = _ffn_bwd(dh, s1, row("ffn1_norm", i), W, emit, "ffn1", f"l{i}_ffn1")
    gsmall = {n: jnp.concatenate(v, axis=0) for n, v in gs.items()}
    return loss, dh, gsmall


def _pad_heads(w, real, padded):
    lead, n = w.shape[:-1], w.shape[-1] // real
    w = jnp.pad(w.reshape(*lead, n, real), [(0, 0)] * (len(lead) + 1) + [(0, padded - real)])
    return w.reshape(*lead, n * padded)


def _unpad_heads(w, real, padded):
    lead, n = w.shape[:-1], w.shape[-1] // padded
    return w.reshape(*lead, n, padded)[..., :real].reshape(*lead, n * real)


def _pack_small(gs):
    flat = jnp.concatenate([gs[n].reshape(-1) for n in SMALL])
    rows = -(-flat.shape[0] // LANES)
    rows = -(-rows // 8) * 8
    return jnp.pad(flat, (0, rows * LANES - flat.shape[0])).reshape(rows, LANES)


def _unpack_small(packed, like):
    flat, out, off = packed.reshape(-1), {}, 0
    for n in SMALL:
        size = int(np.prod(like[n].shape))
        out[n] = flat[off:off + size].reshape(like[n].shape)
        off += size
    return out


def kernel(x, ffn1_norm, ffn1_w_in, ffn1_w_out, mix_norm, ffn2_norm, ffn2_w_in, ffn2_w_out, mla_w_down, mla_g_cq, mla_g_ckv, mla_w_uq, mla_w_ukv, mla_g_qn, mla_g_kn, mla_w_o, dil_w_qkv, dil_g_qn, dil_g_kn, dil_w_o, loss_target, m_ffn1_norm, m_ffn1_w_in, m_ffn1_w_out, m_mix_norm, m_ffn2_norm, m_ffn2_w_in, m_ffn2_w_out, m_mla_w_down, m_mla_g_cq, m_mla_g_ckv, m_mla_w_uq, m_mla_w_ukv, m_mla_g_qn, m_mla_g_kn, m_mla_w_o, m_dil_w_qkv, m_dil_g_qn, m_dil_g_kn, m_dil_w_o, v_ffn1_norm, v_ffn1_w_in, v_ffn1_w_out, v_mix_norm, v_ffn2_norm, v_ffn2_w_in, v_ffn2_w_out, v_mla_w_down, v_mla_g_cq, v_mla_g_ckv, v_mla_w_uq, v_mla_w_ukv, v_mla_g_qn, v_mla_g_kn, v_mla_w_o, v_dil_w_qkv, v_dil_g_qn, v_dil_g_kn, v_dil_w_o):
    args = dict(locals())
    w = {n: args[n] for n in WEIGHTS}
    m = {n: args["m_" + n] for n in WEIGHTS}
    v = {n: args["v_" + n] for n in WEIGHTS}
    cx, cy, cc = _me()
    core = jnp.reshape(cc, (1,)).astype(jnp.int32)
    shard = jnp.reshape(2 * cx + cy, (1,)).astype(jnp.int32)
    place = (shard, core)
    pe_pad = LANES - ROPE_DIM

    order = [(n, layer if w[n].shape[0] > 1 else 0) for layer in range(2) for n in USE_ORDER[layer]]
    lands, over_ici, to_sibling = {}, {}, {}

    def cast(key, after):
        _, r, c = w[key[0]].shape
        lands[key] = cast_into_shards(w[key[0]], key[1], shard, after, "ag_%s%d_cast" % key).reshape(N_CHIPS, 1, 2, r // 2, c)
        return lands[key]

    def start(key, after):
        over_ici[key] = exchange_start((lands.pop(key),), ag_over_ici, after, "ag_%s%d_start" % key)
        return over_ici[key][2]

    def pass_on(key, after):
        if key in to_sibling:
            return after
        bufs = exchange_wait(over_ici[key], ag_over_ici, after, "ag_%s%d_wait" % key)
        to_sibling[key] = exchange_start(bufs, ag_to_sibling, after, "ag_%s%d_pass" % key)
        return to_sibling[key][2]

    def get_w(n, l, after):
        k = order.index((n, l))
        behind = after
        if k == 0:
            for key in order[:2]:
                cast(key, after)
            for key in order[:2]:
                behind = start(key, behind)
            for key in order[2:]:
                behind = cast(key, behind)
        else:
            for key in order[len(over_ici):min(k + AG_AHEAD, len(order) - 1) + 1]:
                behind = start(key, behind)
        behind = pass_on((n, l), behind)
        if 0 < k < len(order) - 1:
            behind = pass_on(order[k + 1], behind)
        (land,) = exchange_wait(to_sibling[n, l], ag_to_sibling, behind, f"ag_{n}{l}_passed")
        _, _, _, h, c = land.shape
        full = land.reshape(N_CHIPS, 1, 2 * h, c)
        if n == "mla_w_down":
            full = jnp.pad(full, ((0, 0), (0, 0), (0, 0), (0, pe_pad)))
        if n == "mla_w_uq":
            full = _pad_heads(full, QK_DIM, QK_PAD)
        return full

    outs = {n: None for n in BIG}

    def update(key, g):
        n, l = key
        outs[n] = adamw(w[n], g, m[n], v[n], l, outs[n], f"adamw_{n}{l}")

    reduce_scatter = ReduceScatter(place, update)

    def on_grad(n, l, g4):
        if n == "mla_w_down":
            g4 = g4[..., :g4.shape[-1] - pe_pad]
        if n == "mla_w_uq":
            g4 = _unpad_heads(g4, QK_DIM, QK_PAD)
        reduce_scatter.push((n, l), g4)

    loss, grad_x, gsmall = local_step(x[0], loss_target[0], {n: w[n] for n in SMALL}, get_w, on_grad)
    loss = lax.psum(loss, ("x", "y", "c"))
    reduce_scatter.drain(grad_x)
    small = _unpack_small(all_reduce_small(_pack_small(gsmall), "ar_small"), gsmall)
    for n in SMALL:
        outs[n] = tuple(o[0] for o in adamw(w[n][None], small[n], m[n][None], v[n][None], 0, None, f"adamw_{n}"))

    return (loss, grad_x[None], *[outs[n][0] for n in WEIGHTS], *[outs[n][1] for n in WEIGHTS],
            *[outs[n][2] for n in WEIGHTS], *[outs[n][3] for n in WEIGHTS])
```
